```python
import jax, jax.numpy as jnp
from jax import lax
import numpy as np

D_MODEL = 1024
BATCH = 8
SEQ = 8192
DEPTH = 4

CHUNK = 64
D_MIX = D_MODEL
D_CONV = D_MIX // 4
D_POOL = D_MIX // 4
D_ATTN = D_MIX // 2
HEAD_DIM = 64
N_HEADS = D_ATTN // HEAD_DIM
CONV_W = 3
POOL_WINDOWS = (2, 4, 8, 16)
N_POOL = len(POOL_WINDOWS)
POOL_GC = D_POOL // N_POOL
LEFT_CHUNKS = 8
BAND = (LEFT_CHUNKS + 1) * CHUNK
REL_CLIP = 128
D_FF = ((8 * D_MODEL // 3 + 255) // 256) * 256
IN_COLS = 3 * D_CONV + D_POOL + 3 * D_ATTN
EPS = 1e-6

kernel_name = "hybrid_conv_pool_chunkattn_trunk"


def rmsnorm(x, g):
    xf = x.astype(jnp.float32)
    y = xf * lax.rsqrt(jnp.mean(xf * xf, axis=-1, keepdims=True) + EPS)
    return (y * g.astype(jnp.float32)).astype(x.dtype)


def shift_right(u, n):
    s = u.shape[1]
    return jnp.pad(u, ((0, 0), (n, 0), (0, 0)))[:, :s]


def short_conv_mixer(gb, gc, u, conv_w):
    z = gc * u
    conv = conv_w[2] * z + conv_w[1] * shift_right(z, 1) + conv_w[0] * shift_right(z, 2)
    return gb * conv


def pool_mixer(u, pool_w, pool_scale):
    b, s, _ = u.shape
    uf = u.astype(jnp.float32)
    cs = jnp.cumsum(uf, axis=1)
    t = jnp.arange(s)
    outs = []
    for gi, w in enumerate(POOL_WINDOWS):
        sl = slice(gi * POOL_GC, (gi + 1) * POOL_GC)
        c = cs[..., sl]
        cnt = jnp.minimum(t + 1, w).astype(jnp.float32)[None, :, None]
        outs.append((c - shift_right(c, w)) / cnt - uf[..., sl])
    d = jnp.concatenate(outs, axis=-1).astype(u.dtype).reshape(b, s, N_POOL, POOL_GC)
    y = jnp.einsum('bsgc,gcd->bsgd', d, pool_w).reshape(b, s, D_POOL)
    return y * pool_scale


def chunk_attention(q, k, v, rel_bias):
    b, s, h, dh = q.shape
    n_chunks = s // CHUNK
    pad = ((0, 0), (LEFT_CHUNKS * CHUNK, 0), (0, 0), (0, 0))
    kpad = jnp.pad(k, pad)
    vpad = jnp.pad(v, pad)
    qi = jnp.arange(CHUNK)[:, None]
    kj = jnp.arange(BAND)[None, :]
    rel = LEFT_CHUNKS * CHUNK + qi - kj
    idx = jnp.clip(rel, -REL_CLIP, REL_CLIP) + REL_CLIP
    bias = rel_bias[:, idx].astype(jnp.float32)
    key_off = jnp.arange(BAND)
    scale = HEAD_DIM ** -0.5

    def one_chunk(c):
        qc = lax.dynamic_slice_in_dim(q, c * CHUNK, CHUNK, axis=1)
        kb = lax.dynamic_slice_in_dim(kpad, c * CHUNK, BAND, axis=1)
        vb = lax.dynamic_slice_in_dim(vpad, c * CHUNK, BAND, axis=1)
        sc = jnp.einsum('bqhd,bkhd->bhqk', qc, kb).astype(jnp.float32) * scale + bias[None]
        valid = key_off >= (LEFT_CHUNKS - c) * CHUNK
        sc = jnp.where(valid, sc, jnp.finfo(jnp.float32).min)
        p = jax.nn.softmax(sc, axis=-1).astype(vb.dtype)
        return jnp.einsum('bhqk,bkhd->bqhd', p, vb)

    out = lax.map(one_chunk, jnp.arange(n_chunks))
    return out.transpose(1, 0, 2, 3, 4).reshape(b, s, h * dh)


def _fwd_setup_inputs(seed: int = 0) -> dict:
    key = jax.random.key(seed)
    ks = jax.random.split(key, 16)
    f32 = jnp.float32
    nrm = lambda k, shp, sc: jax.random.normal(k, shp, f32) * sc
    gain = lambda k, shp: 1.0 + 0.05 * jax.random.normal(k, shp, f32)
    return {
        "x": jax.random.normal(ks[0], (BATCH, SEQ, D_MODEL), f32),
        "w_in": nrm(ks[1], (DEPTH, D_MODEL, IN_COLS), D_MODEL ** -0.5),
        "w_out": nrm(ks[2], (DEPTH, D_MIX, D_MODEL), D_MIX ** -0.5),
        "conv_w": nrm(ks[3], (DEPTH, CONV_W, D_CONV), CONV_W ** -0.5),
        "pool_w": nrm(ks[4], (DEPTH, N_POOL, POOL_GC, POOL_GC), POOL_GC ** -0.5),
        "pool_scale": gain(ks[5], (DEPTH, D_POOL)),
        "rel_bias": nrm(ks[6], (DEPTH, N_HEADS, 2 * REL_CLIP + 1), 0.1),
        "group_gain": gain(ks[7], (DEPTH, D_MIX)),
        "pre_mix_g": gain(ks[8], (DEPTH, D_MODEL)),
        "post_mix_g": gain(ks[9], (DEPTH, D_MODEL)),
        "pre_ffn_g": gain(ks[10], (DEPTH, D_MODEL)),
        "post_ffn_g": gain(ks[11], (DEPTH, D_MODEL)),
        "w_gate_up": nrm(ks[12], (DEPTH, D_MODEL, 2 * D_FF), D_MODEL ** -0.5),
        "w_down": nrm(ks[13], (DEPTH, D_FF, D_MODEL), D_FF ** -0.5),
    }


def _fwd_reference(x, w_in, w_out, conv_w, pool_w, pool_scale, rel_bias, group_gain,
              pre_mix_g, post_mix_g, pre_ffn_g, post_ffn_g, w_gate_up, w_down):
    b, s, _ = x.shape
    h = x
    for l in range(DEPTH):
        xn = rmsnorm(h, pre_mix_g[l])
        proj = jnp.einsum('bsd,dc->bsc', xn, w_in[l])
        o = 0
        gb = proj[..., o:o + D_CONV]; o += D_CONV
        gc = proj[..., o:o + D_CONV]; o += D_CONV
        u = proj[..., o:o + D_CONV]; o += D_CONV
        pu = proj[..., o:o + D_POOL]; o += D_POOL
        q = proj[..., o:o + D_ATTN].reshape(b, s, N_HEADS, HEAD_DIM); o += D_ATTN
        k = proj[..., o:o + D_ATTN].reshape(b, s, N_HEADS, HEAD_DIM); o += D_ATTN
        v = proj[..., o:o + D_ATTN].reshape(b, s, N_HEADS, HEAD_DIM)

        ya = short_conv_mixer(gb, gc, u, conv_w[l])
        yb = pool_mixer(pu, pool_w[l], pool_scale[l])
        yc = chunk_attention(q, k, v, rel_bias[l])

        gg = group_gain[l]
        ya = rmsnorm(ya, gg[:D_CONV])
        yb = rmsnorm(yb, gg[D_CONV:D_CONV + D_POOL])
        yc = rmsnorm(yc, gg[D_CONV + D_POOL:])
        y = jnp.concatenate([ya, yb, yc], axis=-1)
        mix = jnp.einsum('bsc,cd->bsd', y, w_out[l])
        h = h + rmsnorm(mix, post_mix_g[l])

        hn = rmsnorm(h, pre_ffn_g[l])
        gu = jnp.einsum('bsd,df->bsf', hn, w_gate_up[l])
        ff = jax.nn.silu(gu[..., :D_FF]) * gu[..., D_FF:]
        ffo = jnp.einsum('bsf,fd->bsd', ff, w_down[l])
        h = h + rmsnorm(ffo, post_ffn_g[l])
    return h


import jax as _jax
import jax.numpy as _jnp

TWIN_FORMAT = 'train_step'
FWD_PARAMS = ['x', 'w_in', 'w_out', 'conv_w', 'pool_w', 'pool_scale', 'rel_bias', 'group_gain', 'pre_mix_g', 'post_mix_g', 'pre_ffn_g', 'post_ffn_g', 'w_gate_up', 'w_down']
TWIN_WEIGHTS = ['w_in', 'w_out', 'conv_w', 'pool_w', 'pool_scale', 'rel_bias', 'group_gain', 'pre_mix_g', 'post_mix_g', 'pre_ffn_g', 'post_ffn_g', 'w_gate_up', 'w_down']
TWIN_DIFF_INPUT = 'x'
TWIN_INPUTS = ['x', 'w_in', 'w_out', 'conv_w', 'pool_w', 'pool_scale', 'rel_bias', 'group_gain', 'pre_mix_g', 'post_mix_g', 'pre_ffn_g', 'post_ffn_g', 'w_gate_up', 'w_down', 'loss_target', 'm_w_in', 'm_w_out', 'm_conv_w', 'm_pool_w', 'm_pool_scale', 'm_rel_bias', 'm_group_gain', 'm_pre_mix_g', 'm_post_mix_g', 'm_pre_ffn_g', 'm_post_ffn_g', 'm_w_gate_up', 'm_w_down', 'v_w_in', 'v_w_out', 'v_conv_w', 'v_pool_w', 'v_pool_scale', 'v_rel_bias', 'v_group_gain', 'v_pre_mix_g', 'v_post_mix_g', 'v_pre_ffn_g', 'v_post_ffn_g', 'v_w_gate_up', 'v_w_down']
TWIN_OUTPUTS = ['loss', 'grad_x', 'grad_w_in', 'grad_w_out', 'grad_conv_w', 'grad_pool_w', 'grad_pool_scale', 'grad_rel_bias', 'grad_group_gain', 'grad_pre_mix_g', 'grad_post_mix_g', 'grad_pre_ffn_g', 'grad_post_ffn_g', 'grad_w_gate_up', 'grad_w_down', 'delta_w_in', 'delta_w_out', 'delta_conv_w', 'delta_pool_w', 'delta_pool_scale', 'delta_rel_bias', 'delta_group_gain', 'delta_pre_mix_g', 'delta_post_mix_g', 'delta_pre_ffn_g', 'delta_post_ffn_g', 'delta_w_gate_up', 'delta_w_down', 'new_m_w_in', 'new_m_w_out', 'new_m_conv_w', 'new_m_pool_w', 'new_m_pool_scale', 'new_m_rel_bias', 'new_m_group_gain', 'new_m_pre_mix_g', 'new_m_post_mix_g', 'new_m_pre_ffn_g', 'new_m_post_ffn_g', 'new_m_w_gate_up', 'new_m_w_down', 'new_v_w_in', 'new_v_w_out', 'new_v_conv_w', 'new_v_pool_w', 'new_v_pool_scale', 'new_v_rel_bias', 'new_v_group_gain', 'new_v_pre_mix_g', 'new_v_post_mix_g', 'new_v_pre_ffn_g', 'new_v_post_ffn_g', 'new_v_w_gate_up', 'new_v_w_down']
TWIN_LEAF_KINDS = {'loss': 'loss', 'grad_x': 'grad_x', 'grad_w_in': 'grad_w', 'grad_w_out': 'grad_w', 'grad_conv_w': 'grad_w', 'grad_pool_w': 'grad_w', 'grad_pool_scale': 'grad_w', 'grad_rel_bias': 'grad_w', 'grad_group_gain': 'grad_w', 'grad_pre_mix_g': 'grad_w', 'grad_post_mix_g': 'grad_w', 'grad_pre_ffn_g': 'grad_w', 'grad_post_ffn_g': 'grad_w', 'grad_w_gate_up': 'grad_w', 'grad_w_down': 'grad_w', 'delta_w_in': 'delta_w', 'delta_w_out': 'delta_w', 'delta_conv_w': 'delta_w', 'delta_pool_w': 'delta_w', 'delta_pool_scale': 'delta_w', 'delta_rel_bias': 'delta_w', 'delta_group_gain': 'delta_w', 'delta_pre_mix_g': 'delta_w', 'delta_post_mix_g': 'delta_w', 'delta_pre_ffn_g': 'delta_w', 'delta_post_ffn_g': 'delta_w', 'delta_w_gate_up': 'delta_w', 'delta_w_down': 'delta_w', 'new_m_w_in': 'new_m', 'new_m_w_out': 'new_m', 'new_m_conv_w': 'new_m', 'new_m_pool_w': 'new_m', 'new_m_pool_scale': 'new_m', 'new_m_rel_bias': 'new_m', 'new_m_group_gain': 'new_m', 'new_m_pre_mix_g': 'new_m', 'new_m_post_mix_g': 'new_m', 'new_m_pre_ffn_g': 'new_m', 'new_m_post_ffn_g': 'new_m', 'new_m_w_gate_up': 'new_m', 'new_m_w_down': 'new_m', 'new_v_w_in': 'new_v', 'new_v_w_out': 'new_v', 'new_v_conv_w': 'new_v', 'new_v_pool_w': 'new_v', 'new_v_pool_scale': 'new_v', 'new_v_rel_bias': 'new_v', 'new_v_group_gain': 'new_v', 'new_v_pre_mix_g': 'new_v', 'new_v_post_mix_g': 'new_v', 'new_v_pre_ffn_g': 'new_v', 'new_v_post_ffn_g': 'new_v', 'new_v_w_gate_up': 'new_v', 'new_v_w_down': 'new_v'}


def _forward(args):
    return _fwd_reference(*[args[k] for k in FWD_PARAMS])


def _output_shape():
    out = _jax.eval_shape(lambda: _forward(_fwd_setup_inputs(0)))
    return out.shape, out.dtype

N_MICROBATCH = 1
ADAM_LR = 0.001
ADAM_B1 = 0.9
ADAM_B2 = 0.999
ADAM_EPS = 1e-08
ADAM_WD = 0.01
ADAM_STEP = 10
PER_EXAMPLE_BATCH_AXIS = {'x': 0, 'loss_target': 0}
SHARED_INPUTS = []
_WEIGHT_DTYPES = {'w_in': _jnp.float32, 'w_out': _jnp.float32, 'conv_w': _jnp.float32, 'pool_w': _jnp.float32, 'pool_scale': _jnp.float32, 'rel_bias': _jnp.float32, 'group_gain': _jnp.float32, 'pre_mix_g': _jnp.float32, 'post_mix_g': _jnp.float32, 'pre_ffn_g': _jnp.float32, 'post_ffn_g': _jnp.float32, 'w_gate_up': _jnp.float32, 'w_down': _jnp.float32}
MOMENT_SCALE = {'w_in': 1.343900e+01, 'w_out': 2.279591e+01, 'conv_w': 2.175814e+00, 'pool_w': 3.023218e+00, 'pool_scale': 2.874401e+00, 'rel_bias': 1.457543e+00, 'group_gain': 2.314001e+01, 'pre_mix_g': 2.113622e+01, 'post_mix_g': 6.583633e+01, 'pre_ffn_g': 6.203339e+00, 'post_ffn_g': 6.352067e+01, 'w_gate_up': 2.659672e+00, 'w_down': 5.426057e+00}


def _to_microbatches(a, axis):
    t = _jnp.moveaxis(a, axis, 0)
    t = t.reshape((N_MICROBATCH, t.shape[0] // N_MICROBATCH) + t.shape[1:])
    return _jnp.moveaxis(t, 1, axis + 1)


def setup_inputs(seed: int = 0) -> dict:
    inp = _fwd_setup_inputs(seed)
    key = _jax.random.fold_in(_jax.random.key(seed), 7919)
    shape, _ = _output_shape()
    out = dict(inp)
    out["loss_target"] = _jax.random.normal(_jax.random.fold_in(key, 0), shape, _jnp.float32)
    for i, name in enumerate(TWIN_WEIGHTS):
        w = inp[name].astype(_jnp.float32)
        if MOMENT_SCALE is None:
            s = _jnp.sqrt(_jnp.mean(_jnp.square(w)) + 1e-30)
        else:
            s = MOMENT_SCALE[name]
        km, kv = _jax.random.split(_jax.random.fold_in(key, i + 1))
        out[name] = w
        out["m_" + name] = s * _jax.random.normal(km, w.shape, _jnp.float32)
        out["v_" + name] = (s * s) * _jax.random.uniform(kv, w.shape, _jnp.float32, 0.5, 1.5)
    if N_MICROBATCH > 1:
        for name, axis in PER_EXAMPLE_BATCH_AXIS.items():
            out[name] = _to_microbatches(out[name], axis)
    return {'x': out['x'], 'w_in': out['w_in'], 'w_out': out['w_out'], 'conv_w': out['conv_w'], 'pool_w': out['pool_w'], 'pool_scale': out['pool_scale'], 'rel_bias': out['rel_bias'], 'group_gain': out['group_gain'], 'pre_mix_g': out['pre_mix_g'], 'post_mix_g': out['post_mix_g'], 'pre_ffn_g': out['pre_ffn_g'], 'post_ffn_g': out['post_ffn_g'], 'w_gate_up': out['w_gate_up'], 'w_down': out['w_down'], 'loss_target': out['loss_target'], 'm_w_in': out['m_w_in'], 'm_w_out': out['m_w_out'], 'm_conv_w': out['m_conv_w'], 'm_pool_w': out['m_pool_w'], 'm_pool_scale': out['m_pool_scale'], 'm_rel_bias': out['m_rel_bias'], 'm_group_gain': out['m_group_gain'], 'm_pre_mix_g': out['m_pre_mix_g'], 'm_post_mix_g': out['m_post_mix_g'], 'm_pre_ffn_g': out['m_pre_ffn_g'], 'm_post_ffn_g': out['m_post_ffn_g'], 'm_w_gate_up': out['m_w_gate_up'], 'm_w_down': out['m_w_down'], 'v_w_in': out['v_w_in'], 'v_w_out': out['v_w_out'], 'v_conv_w': out['v_conv_w'], 'v_pool_w': out['v_pool_w'], 'v_pool_scale': out['v_pool_scale'], 'v_rel_bias': out['v_rel_bias'], 'v_group_gain': out['v_group_gain'], 'v_pre_mix_g': out['v_pre_mix_g'], 'v_post_mix_g': out['v_post_mix_g'], 'v_pre_ffn_g': out['v_pre_ffn_g'], 'v_post_ffn_g': out['v_post_ffn_g'], 'v_w_gate_up': out['v_w_gate_up'], 'v_w_down': out['v_w_down']}


def _loss(weights, diff, rest, loss_target):
    with _jax.named_scope("forward"):
        args = {**rest, TWIN_DIFF_INPUT: diff, **{k: w.astype(_WEIGHT_DTYPES[k]) for k, w in weights.items()}}
        y = _forward(args)
    with _jax.named_scope("loss_head"):
        err = _jnp.square(y.astype(_jnp.float32) - loss_target)
        return 0.5 * _jnp.sum(_jnp.mean(err, axis=-1)) if err.ndim else 0.5 * err


def _adamw(w, g, m, v):
    m = ADAM_B1 * m + (1.0 - ADAM_B1) * g
    v = ADAM_B2 * v + (1.0 - ADAM_B2) * _jnp.square(g)
    m_hat = m / (1.0 - ADAM_B1 ** ADAM_STEP)
    v_hat = v / (1.0 - ADAM_B2 ** ADAM_STEP)
    delta = -ADAM_LR * (m_hat / (_jnp.sqrt(v_hat) + ADAM_EPS) + ADAM_WD * w)
    return delta, m, v


def reference(x, w_in, w_out, conv_w, pool_w, pool_scale, rel_bias, group_gain, pre_mix_g, post_mix_g, pre_ffn_g, post_ffn_g, w_gate_up, w_down, loss_target, m_w_in, m_w_out, m_conv_w, m_pool_w, m_pool_scale, m_rel_bias, m_group_gain, m_pre_mix_g, m_post_mix_g, m_pre_ffn_g, m_post_ffn_g, m_w_gate_up, m_w_down, v_w_in, v_w_out, v_conv_w, v_pool_w, v_pool_scale, v_rel_bias, v_group_gain, v_pre_mix_g, v_post_mix_g, v_pre_ffn_g, v_post_ffn_g, v_w_gate_up, v_w_down):
    given = dict(x=x, w_in=w_in, w_out=w_out, conv_w=conv_w, pool_w=pool_w, pool_scale=pool_scale, rel_bias=rel_bias, group_gain=group_gain, pre_mix_g=pre_mix_g, post_mix_g=post_mix_g, pre_ffn_g=pre_ffn_g, post_ffn_g=post_ffn_g, w_gate_up=w_gate_up, w_down=w_down, loss_target=loss_target, m_w_in=m_w_in, m_w_out=m_w_out, m_conv_w=m_conv_w, m_pool_w=m_pool_w, m_pool_scale=m_pool_scale, m_rel_bias=m_rel_bias, m_group_gain=m_group_gain, m_pre_mix_g=m_pre_mix_g, m_post_mix_g=m_post_mix_g, m_pre_ffn_g=m_pre_ffn_g, m_post_ffn_g=m_post_ffn_g, m_w_gate_up=m_w_gate_up, m_w_down=m_w_down, v_w_in=v_w_in, v_w_out=v_w_out, v_conv_w=v_conv_w, v_pool_w=v_pool_w, v_pool_scale=v_pool_scale, v_rel_bias=v_rel_bias, v_group_gain=v_group_gain, v_pre_mix_g=v_pre_mix_g, v_post_mix_g=v_post_mix_g, v_pre_ffn_g=v_pre_ffn_g, v_post_ffn_g=v_post_ffn_g, v_w_gate_up=v_w_gate_up, v_w_down=v_w_down)
    weights = {n: given[n] for n in TWIN_WEIGHTS}
    shared = {n: given[n] for n in SHARED_INPUTS}
    per_example = {n: given[n] for n in ['x']}
    grad_fn = _jax.value_and_grad(_loss, argnums=(0, 1))

    def one_microbatch(ex, loss_target):
        ex = dict(ex)
        diff = ex.pop(TWIN_DIFF_INPUT)
        return grad_fn(weights, diff, {**shared, **ex}, loss_target)

    if N_MICROBATCH == 1:
        loss, (grad_w, grad_x) = one_microbatch(per_example, given["loss_target"])
    else:
        def body(carry, xs):
            loss_sum, grad_sum = carry
            l_k, (gw_k, gx_k) = one_microbatch(xs[0], xs[1])
            with _jax.named_scope("update"):
                return (loss_sum + l_k, _jax.tree.map(_jnp.add, grad_sum, gw_k)), gx_k

        init = (_jnp.zeros((), _jnp.float32), _jax.tree.map(_jnp.zeros_like, weights))
        (loss, grad_w), grad_x = _jax.lax.scan(body, init, (per_example, given["loss_target"]))
    with _jax.named_scope("update"):
        delta_w, new_m, new_v = {}, {}, {}
        for n in TWIN_WEIGHTS:
            delta_w[n], new_m[n], new_v[n] = _adamw(weights[n], grad_w[n], given["m_" + n], given["v_" + n])
    return (loss, grad_x, *[grad_w[n] for n in TWIN_WEIGHTS], *[delta_w[n] for n in TWIN_WEIGHTS],
            *[new_m[n] for n in TWIN_WEIGHTS], *[new_v[n] for n in TWIN_WEIGHTS])
```

```python
import functools

import jax
import jax.numpy as jnp
from jax import lax
from jax.experimental import pallas as pl
from jax.experimental.pallas import tpu as pltpu

F32, BF16 = jnp.float32, jnp.bfloat16
EPS = 1e-6
CHUNK = 64
LEFT_CHUNKS = 8
REL_CLIP = 128
HEAD_DIM = 64
LANES = 128
POOL_WINDOWS = (2, 4, 8, 16)
HALO = 16
TB = LEFT_CHUNKS * CHUNK
TBF = 256
BAND = (LEFT_CHUNKS + 1) * CHUNK
NEG = -1e30
RBP = 384
VMEM_LIMIT = 56 * 1024 * 1024
ADAM_LR, ADAM_B1, ADAM_B2, ADAM_EPS, ADAM_WD, ADAM_STEP = 0.001, 0.9, 0.999, 1e-08, 0.01, 10
MESH = pl.DeviceIdType.MESH
ANY = pl.BlockSpec(memory_space=pl.ANY)


def _params(*sem):
    kw = dict(vmem_limit_bytes=VMEM_LIMIT)
    if sem:
        kw["dimension_semantics"] = sem
    return pltpu.CompilerParams(**kw)


def _dot(a, b):
    return jnp.dot(a, b, preferred_element_type=F32)


def _dot_nt(a, b):
    return lax.dot_general(a, b, (((1,), (1,)), ((), ())), preferred_element_type=F32)


def _dot_tn(a, b):
    return lax.dot_general(a, b, (((0,), (0,)), ((), ())), preferred_element_type=F32)


def _rms(x, g):
    r = lax.rsqrt(jnp.mean(x * x, axis=-1, keepdims=True) + EPS)
    return x * r * g


def _rms_bwd(dy, x, g):
    r = lax.rsqrt(jnp.mean(x * x, axis=-1, keepdims=True) + EPS)
    xh = x * r
    dxh = dy * g
    dx = r * (dxh - xh * jnp.mean(dxh * xh, axis=-1, keepdims=True))
    return dx, jnp.sum(dy * xh, axis=0, keepdims=True)


def _full(shape):
    return pl.BlockSpec(shape, lambda *_: (0,) * len(shape))


def _acc_init(step, *refs):
    @pl.when(step == 0)
    def _():
        for r in refs:
            r[...] = jnp.zeros_like(r)


def _inproj_fwd(h, g, w):
    T, D = h.shape
    NQ = w.shape[1] - D

    def body(h_ref, g_ref, w_ref, xn_ref, pa_ref, qkv_ref):
        xn = _rms(h_ref[...], g_ref[...]).astype(BF16)
        xn_ref[...] = xn
        pa_ref[...] = _dot(xn, w_ref[:, :D])
        qkv_ref[...] = _dot(xn, w_ref[:, D:]).astype(BF16)

    row = lambda n: pl.BlockSpec((TB, n), lambda i: (i, 0))
    return pl.pallas_call(
        body, name="inproj_fwd", grid=(T // TB,),
        in_specs=[row(D), _full((1, D)), _full(w.shape)],
        out_specs=[row(D), row(D), row(NQ)],
        out_shape=[jax.ShapeDtypeStruct((T, D), BF16), jax.ShapeDtypeStruct((T, D), F32),
                   jax.ShapeDtypeStruct((T, NQ), BF16)],
        compiler_params=_params("parallel"),
    )(h, g, w)


def _lane_groups(n, vals):
    lane = lax.broadcasted_iota(jnp.int32, (1, n), 1)
    q = n // 4
    return jnp.where(lane < q, vals[0], jnp.where(lane < 2 * q, vals[1], jnp.where(lane < 3 * q, vals[2], vals[3]))).astype(F32)


def _pick_group(levels, n):
    lane = lax.broadcasted_iota(jnp.int32, (1, n), 1)
    q = n // 4
    return jnp.where(lane < q, levels[0], jnp.where(lane < 2 * q, levels[1], jnp.where(lane < 3 * q, levels[2], levels[3])))


def _pool_count(blk, n):
    t1 = (blk * TB + 1 + lax.broadcasted_iota(jnp.int32, (TB, 1), 0)).astype(F32)
    return jnp.minimum(t1, _lane_groups(n, POOL_WINDOWS))


def _pool_d(pu, pu_halo, cnt):
    e = jnp.concatenate([pu_halo, pu], axis=0)
    s2 = e + pltpu.roll(e, 1, 0)
    s4 = s2 + pltpu.roll(s2, 2, 0)
    s8 = s4 + pltpu.roll(s4, 4, 0)
    s16 = s8 + pltpu.roll(s8, 8, 0)
    num = _pick_group([s2, s4, s8, s16], pu.shape[1])[HALO:]
    return num / cnt - pu


def _conv_taps(z, z_halo):
    e = jnp.concatenate([z_halo, z], axis=0)
    return pltpu.roll(e, 1, 0)[HALO:], pltpu.roll(e, 2, 0)[HALO:]


def _convpool_fwd(pa, cw, wbd, ps):
    T, D = pa.shape
    DC = D // 4

    def body(pa_ref, halo_ref, cw_ref, wbd_ref, ps_ref, yab_ref):
        i = pl.program_id(0)
        x = pa_ref[...]
        hl = jnp.where(i > 0, halo_ref[...], 0.0)
        gb, gc, u, pu = (x[:, k * DC:(k + 1) * DC] for k in range(4))
        z = gc * u
        z1, z2 = _conv_taps(z, hl[:, DC:2 * DC] * hl[:, 2 * DC:3 * DC])
        cwv = cw_ref[...]
        ya = gb * (cwv[2:3] * z + cwv[1:2] * z1 + cwv[0:1] * z2)
        d = _pool_d(pu, hl[:, 3 * DC:], _pool_count(i, DC))
        yb = _dot(d.astype(BF16), wbd_ref[...]) * ps_ref[...]
        yab_ref[...] = jnp.concatenate([ya, yb], axis=1)

    return pl.pallas_call(
        body, name="convpool_fwd", grid=(T // TB,),
        in_specs=[pl.BlockSpec((TB, D), lambda i: (i, 0)),
                  pl.BlockSpec((HALO, D), lambda i: (jnp.maximum(i * (TB // HALO) - 1, 0), 0)),
                  _full((8, DC)), _full((DC, DC)), _full((1, DC))],
        out_specs=pl.BlockSpec((TB, 2 * DC), lambda i: (i, 0)),
        out_shape=jax.ShapeDtypeStruct((T, 2 * DC), F32),
        compiler_params=_params("parallel"),
    )(pa, pa, cw, wbd, ps)


def _bias_bins(shape, col_dim):
    j = lax.broadcasted_iota(jnp.int32, shape, col_dim)
    b = lax.broadcasted_iota(jnp.int32, shape, 1 - col_dim)
    d = jnp.where(j < BAND, j, j - 2 * TB)
    live = jnp.logical_or(j < BAND, j > 2 * TB - CHUNK)
    bins = jnp.minimum(TB - d, REL_CLIP) + REL_CLIP
    return jnp.where(jnp.logical_and(live, bins == b), 1.0, 0.0).astype(F32)


def _bias_build(rb):
    H = rb.shape[0]
    W = 2 * TB

    def body(rb_ref, o_ref):
        v = jnp.dot(jnp.broadcast_to(rb_ref[0], (8, RBP)), _bias_bins((RBP, W), 1),
                    precision=lax.Precision.HIGHEST, preferred_element_type=F32)
        x = jnp.broadcast_to(v[0:1], (TB, W))
        row = lax.broadcasted_iota(jnp.int32, (TB, W), 0)
        col = lax.broadcasted_iota(jnp.int32, (TB, W), 1)
        for b in range(TB.bit_length() - 1):
            x = jnp.where(((row >> b) & 1) == 1, pltpu.roll(x, 1 << b, 1), x)
        qc, kc = row >> (CHUNK.bit_length() - 1), col >> (CHUNK.bit_length() - 1)
        o_ref[0] = jnp.where(jnp.logical_and(kc >= qc, kc <= qc + LEFT_CHUNKS), x, NEG)

    return pl.pallas_call(
        body, name="bias_build", grid=(H,),
        in_specs=[pl.BlockSpec((1, 1, RBP), lambda h: (h, 0, 0))],
        out_specs=pl.BlockSpec((1, TB, W), lambda h: (h, 0, 0)),
        out_shape=jax.ShapeDtypeStruct((H, TB, W), F32),
        compiler_params=_params("parallel"),
    )(rb)


def _bias_fold(db):
    H = db.shape[0]
    W = 2 * TB

    def body(db_ref, o_ref):
        x = db_ref[0]
        row = lax.broadcasted_iota(jnp.int32, (TB, W), 0)
        for b in range(TB.bit_length() - 1):
            x = jnp.where(((row >> b) & 1) == 1, pltpu.roll(x, W - (1 << b), 1), x)
        dsum = jnp.sum(x, axis=0, keepdims=True)
        o_ref[0] = jnp.dot(jnp.broadcast_to(dsum, (8, W)), _bias_bins((W, RBP), 0),
                           precision=lax.Precision.HIGHEST, preferred_element_type=F32)

    return pl.pallas_call(
        body, name="bias_fold", grid=(H,),
        in_specs=[pl.BlockSpec((1, TB, W), lambda h: (h, 0, 0))],
        out_specs=pl.BlockSpec((1, 8, RBP), lambda h: (h, 0, 0)),
        out_shape=jax.ShapeDtypeStruct((H, 8, RBP), F32),
        compiler_params=_params("parallel"),
    )(db)


def _head_masks():
    lane = lax.broadcasted_iota(jnp.int32, (1, LANES), 1)
    return lane < HEAD_DIM, lane >= HEAD_DIM


def _softmax_rows(s):
    e = jnp.exp(s - jnp.max(s, axis=-1, keepdims=True))
    return e * (1.0 / jnp.sum(e, axis=-1, keepdims=True))


def _attn_fwd(qkv, bias):
    T = qkv.shape[0]
    NP = qkv.shape[1] // (3 * LANES)
    NB = T // TB
    scale = HEAD_DIM ** -0.5

    def body(q_ref, k_ref, v_ref, b_ref, o_ref, kprev, vprev):
        i = pl.program_id(1)

        @pl.when(i == 0)
        def _():
            kprev[...] = jnp.zeros_like(kprev)
            vprev[...] = jnp.zeros_like(vprev)

        q, kc, vc = q_ref[...], k_ref[...], v_ref[...]
        kwin = jnp.concatenate([kprev[...], kc], axis=0)
        vwin = jnp.concatenate([vprev[...], vc], axis=0)
        col = lax.broadcasted_iota(jnp.int32, (1, 2 * TB), 1)
        before_start = jnp.logical_and(col < TB, i == 0)
        acc = jnp.zeros((TB, LANES), F32)
        for a, ma in enumerate(_head_masks()):
            s = _dot_nt(jnp.where(ma, q, 0), kwin) * scale + b_ref[a]
            p = _softmax_rows(jnp.where(before_start, NEG, s))
            acc = acc + _dot(p.astype(BF16), jnp.where(ma, vwin, 0))
        o_ref[...] = acc
        kprev[...] = kc
        vprev[...] = vc

    col = lambda off: pl.BlockSpec((TB, LANES), lambda p, i: (i, off + p))
    return pl.pallas_call(
        body, name="attn_fwd", grid=(NP, NB),
        in_specs=[col(0), col(NP), col(2 * NP), pl.BlockSpec((2, TB, 2 * TB), lambda p, i: (p, 0, 0))],
        out_specs=col(0),
        out_shape=jax.ShapeDtypeStruct((T, NP * LANES), F32),
        scratch_shapes=[pltpu.VMEM((TB, LANES), BF16), pltpu.VMEM((TB, LANES), BF16)],
        compiler_params=_params("arbitrary", "arbitrary"),
    )(qkv, qkv, qkv, bias)


def _group_bounds(D):
    return ((0, D // 4), (D // 4, D // 2), (D // 2, D))


def _mix_out_fwd(yab, yc, gg, w, h, g):
    T, D = h.shape

    def body(yab_ref, yc_ref, gg_ref, w_ref, h_ref, g_ref, y_ref, mix_ref, h1_ref):
        yraw = jnp.concatenate([yab_ref[...], yc_ref[...]], axis=1)
        ggv = gg_ref[...]
        y = jnp.concatenate([_rms(yraw[:, a:b], ggv[:, a:b]) for a, b in _group_bounds(D)], axis=1).astype(BF16)
        y_ref[...] = y
        mix = _dot(y, w_ref[...])
        mix_ref[...] = mix
        h1_ref[...] = h_ref[...] + _rms(mix, g_ref[...])

    row = lambda n: pl.BlockSpec((TB, n), lambda i: (i, 0))
    return pl.pallas_call(
        body, name="mix_out_fwd", grid=(T // TB,),
        in_specs=[row(D // 2), row(D // 2), _full((1, D)), _full((D, D)), row(D), _full((1, D))],
        out_specs=[row(D), row(D), row(D)],
        out_shape=[jax.ShapeDtypeStruct((T, D), BF16), jax.ShapeDtypeStruct((T, D), F32), jax.ShapeDtypeStruct((T, D), F32)],
        compiler_params=_params("parallel"),
    )(yab, yc, gg, w, h, g)


def _load_resident(step, w_hbm, w_vmem, sem):
    @pl.when(step == 0)
    def _():
        cp = pltpu.make_async_copy(w_hbm, w_vmem, sem)
        cp.start()
        cp.wait()


def _ffn_up_fwd(h1, g, w):
    T, D = h1.shape
    F2 = w.shape[1]
    DFF = F2 // 2
    TF = DFF // 2

    def body(h_ref, g_ref, w_hbm, hn_ref, gu_ref, ff_ref, w_v, sem):
        _load_resident(pl.program_id(0), w_hbm, w_v, sem)
        hn = _rms(h_ref[...], g_ref[...]).astype(BF16)
        hn_ref[...] = hn
        for j in range(2):
            sg, su = slice(j * TF, (j + 1) * TF), slice(DFF + j * TF, DFF + (j + 1) * TF)
            gate = _dot(hn, w_v[:, sg])
            up = _dot(hn, w_v[:, su])
            gu_ref[:, sg] = gate
            gu_ref[:, su] = up
            ff_ref[:, sg] = (gate * jax.nn.sigmoid(gate) * up).astype(BF16)

    row = lambda n: pl.BlockSpec((TBF, n), lambda i: (i, 0))
    return pl.pallas_call(
        body, name="ffn_up_fwd", grid=(T // TBF,),
        in_specs=[row(D), _full((1, D)), ANY],
        out_specs=[row(D), row(F2), row(DFF)],
        out_shape=[jax.ShapeDtypeStruct((T, D), BF16), jax.ShapeDtypeStruct((T, F2), F32), jax.ShapeDtypeStruct((T, DFF), BF16)],
        scratch_shapes=[pltpu.VMEM(w.shape, BF16), pltpu.SemaphoreType.DMA],
        compiler_params=_params("arbitrary"),
    )(h1, g, w)


def _ffn_down_fwd(ff, w, h1, g):
    T, D = h1.shape
    DFF = ff.shape[1]

    def body(ff_ref, w_ref, h_ref, g_ref, ffo_ref, h2_ref):
        ffo = _dot(ff_ref[...], w_ref[...])
        ffo_ref[...] = ffo
        h2_ref[...] = h_ref[...] + _rms(ffo, g_ref[...])

    row = lambda n: pl.BlockSpec((TBF, n), lambda i: (i, 0))
    return pl.pallas_call(
        body, name="ffn_down_fwd", grid=(T // TBF,),
        in_specs=[row(DFF), _full((DFF, D)), row(D), _full((1, D))],
        out_specs=[row(D), row(D)],
        out_shape=[jax.ShapeDtypeStruct((T, D), F32), jax.ShapeDtypeStruct((T, D), F32)],
        compiler_params=_params("parallel"),
    )(ff, w, h1, g)


def _loss_grad(h, tgt):
    T, D = h.shape

    def body(h_ref, t_ref, dh_ref, loss_ref):
        _acc_init(pl.program_id(0), loss_ref)
        diff = h_ref[...] - t_ref[...]
        dh_ref[...] = diff * (1.0 / D)
        loss_ref[...] += 0.5 * jnp.sum(jnp.mean(diff * diff, axis=-1, keepdims=True))

    row = pl.BlockSpec((TB, D), lambda i: (i, 0))
    return pl.pallas_call(
        body, name="loss_grad", grid=(T // TB,),
        in_specs=[row, row], out_specs=[row, _full((8, LANES))],
        out_shape=[jax.ShapeDtypeStruct((T, D), F32), jax.ShapeDtypeStruct((8, LANES), F32)],
        compiler_params=_params("arbitrary"),
    )(h, tgt)


def _ffn_down_bwd(dh2, ffo, g, gu, w):
    T, D = dh2.shape
    F2 = gu.shape[1]
    DFF = F2 // 2
    TF = DFF // 2

    def body(dh_ref, ffo_ref, g_ref, gu_ref, w_ref, dffo_ref, dgu_ref, dg_ref):
        _acc_init(pl.program_id(0), dg_ref)
        dffo, dg = _rms_bwd(dh_ref[...], ffo_ref[...], g_ref[...])
        dg_ref[0:1, :] += dg
        dffo = dffo.astype(BF16)
        dffo_ref[...] = dffo
        for j in range(2):
            sg, su = slice(j * TF, (j + 1) * TF), slice(DFF + j * TF, DFF + (j + 1) * TF)
            dff = _dot_nt(dffo, w_ref[sg, :])
            gate, up = gu_ref[:, sg], gu_ref[:, su]
            sig = jax.nn.sigmoid(gate)
            dgu_ref[:, sg] = (dff * up * (sig * (1.0 + gate * (1.0 - sig)))).astype(BF16)
            dgu_ref[:, su] = (dff * (gate * sig)).astype(BF16)

    row = lambda n: pl.BlockSpec((TBF, n), lambda i: (i, 0))
    return pl.pallas_call(
        body, name="ffn_down_bwd", grid=(T // TBF,),
        in_specs=[row(D), row(D), _full((1, D)), row(F2), _full((DFF, D))],
        out_specs=[row(D), row(F2), _full((8, D))],
        out_shape=[jax.ShapeDtypeStruct((T, D), BF16), jax.ShapeDtypeStruct((T, F2), BF16), jax.ShapeDtypeStruct((8, D), F32)],
        compiler_params=_params("arbitrary"),
    )(dh2, ffo, g, gu, w)


def _ffn_up_bwd(dgu, w, h1, g, dh2):
    T, D = h1.shape
    F2 = dgu.shape[1]

    def body(dgu_ref, w_hbm, h_ref, g_ref, dh2_ref, dh1_ref, dg_ref, w_v, sem):
        _load_resident(pl.program_id(0), w_hbm, w_v, sem)
        _acc_init(pl.program_id(0), dg_ref)
        dx, dg = _rms_bwd(_dot_nt(dgu_ref[...], w_v[...]), h_ref[...], g_ref[...])
        dg_ref[0:1, :] += dg
        dh1_ref[...] = dh2_ref[...] + dx

    row = lambda n: pl.BlockSpec((TBF, n), lambda i: (i, 0))
    return pl.pallas_call(
        body, name="ffn_up_bwd", grid=(T // TBF,),
        in_specs=[row(F2), ANY, row(D), _full((1, D)), row(D)],
        out_specs=[row(D), _full((8, D))],
        out_shape=[jax.ShapeDtypeStruct((T, D), F32), jax.ShapeDtypeStruct((8, D), F32)],
        scratch_shapes=[pltpu.VMEM(w.shape, BF16), pltpu.SemaphoreType.DMA],
        compiler_params=_params("arbitrary"),
    )(dgu, w, h1, g, dh2)


def _mix_out_bwd(dh1, mix, g, w, yab, yc, gg):
    T, D = dh1.shape

    def body(dh_ref, mix_ref, g_ref, w_ref, yab_ref, yc_ref, gg_ref, dmix_ref, dyab_ref, dyc_ref, dg_ref, dgg_ref):
        _acc_init(pl.program_id(0), dg_ref, dgg_ref)
        dmix, dg = _rms_bwd(dh_ref[...], mix_ref[...], g_ref[...])
        dg_ref[0:1, :] += dg
        dmix = dmix.astype(BF16)
        dmix_ref[...] = dmix
        dy = _dot_nt(dmix, w_ref[...])
        yraw = jnp.concatenate([yab_ref[...], yc_ref[...]], axis=1)
        ggv = gg_ref[...]
        parts = [_rms_bwd(dy[:, a:b], yraw[:, a:b], ggv[:, a:b]) for a, b in _group_bounds(D)]
        dgg_ref[0:1, :] += jnp.concatenate([p[1] for p in parts], axis=1)
        dyab_ref[...] = jnp.concatenate([parts[0][0], parts[1][0]], axis=1)
        dyc_ref[...] = parts[2][0]

    row = lambda n: pl.BlockSpec((TB, n), lambda i: (i, 0))
    return pl.pallas_call(
        body, name="mix_out_bwd", grid=(T // TB,),
        in_specs=[row(D), row(D), _full((1, D)), _full((D, D)), row(D // 2), row(D // 2), _full((1, D))],
        out_specs=[row(D), row(D // 2), row(D // 2), _full((8, D)), _full((8, D))],
        out_shape=[jax.ShapeDtypeStruct((T, D), BF16), jax.ShapeDtypeStruct((T, D // 2), F32), jax.ShapeDtypeStruct((T, D // 2), F32),
                   jax.ShapeDtypeStruct((8, D), F32), jax.ShapeDtypeStruct((8, D), F32)],
        compiler_params=_params("arbitrary"),
    )(dh1, mix, g, w, yab, yc, gg)


def _attn_bwd(qkv, dyc, yc, bias):
    T = qkv.shape[0]
    NP = qkv.shape[1] // (3 * LANES)
    NB = T // TB
    scale = HEAD_DIM ** -0.5

    def body(q_ref, kc_ref, vc_ref, kp_ref, vp_ref, do_ref, o_ref, b_ref, dq_ref, dk_ref, dv_ref, db_ref, dk_carry, dv_carry):
        step = pl.program_id(1)
        i = NB - 1 - step
        _acc_init(step, dk_carry, dv_carry, db_ref)
        q = q_ref[...]
        kwin = jnp.concatenate([kp_ref[...], kc_ref[...]], axis=0)
        vwin = jnp.concatenate([vp_ref[...], vc_ref[...]], axis=0)
        do = do_ref[...]
        dob = do.astype(BF16)
        prod = do * o_ref[...]
        col = lax.broadcasted_iota(jnp.int32, (1, 2 * TB), 1)
        before_start = jnp.logical_and(col < TB, i == 0)
        dq = jnp.zeros((TB, LANES), F32)
        dkw = jnp.zeros((2 * TB, LANES), F32)
        dvw = jnp.zeros((2 * TB, LANES), F32)
        for a, ma in enumerate(_head_masks()):
            qa, doa = jnp.where(ma, q, 0), jnp.where(ma, dob, 0)
            s = _dot_nt(qa, kwin) * scale + b_ref[a]
            p = _softmax_rows(jnp.where(before_start, NEG, s))
            dp = _dot_nt(doa, vwin)
            delta = jnp.sum(jnp.where(ma, prod, 0.0), axis=-1, keepdims=True)
            ds = p * (dp - delta)
            db_ref[a] += ds
            dsb = (ds * scale).astype(BF16)
            dq = dq + _dot(dsb, jnp.where(ma, kwin, 0))
            dkw = dkw + _dot_tn(dsb, qa)
            dvw = dvw + _dot_tn(p.astype(BF16), doa)
        dq_ref[...] = dq.astype(BF16)
        dk_ref[...] = (dkw[TB:] + dk_carry[...]).astype(BF16)
        dv_ref[...] = (dvw[TB:] + dv_carry[...]).astype(BF16)
        dk_carry[...] = dkw[:TB]
        dv_carry[...] = dvw[:TB]

    cur = lambda off: pl.BlockSpec((TB, LANES), lambda p, s: (NB - 1 - s, off + p))
    prev = lambda off: pl.BlockSpec((TB, LANES), lambda p, s: (jnp.maximum(NB - 2 - s, 0), off + p))
    out = jax.ShapeDtypeStruct((T, NP * LANES), BF16)
    return pl.pallas_call(
        body, name="attn_bwd", grid=(NP, NB),
        in_specs=[cur(0), cur(NP), cur(2 * NP), prev(NP), prev(2 * NP), cur(0), cur(0),
                  pl.BlockSpec((2, TB, 2 * TB), lambda p, s: (p, 0, 0))],
        out_specs=[cur(0), cur(0), cur(0), pl.BlockSpec((2, TB, 2 * TB), lambda p, s: (p, 0, 0))],
        out_shape=[out, out, out, jax.ShapeDtypeStruct((2 * NP, TB, 2 * TB), F32)],
        scratch_shapes=[pltpu.VMEM((TB, LANES), F32), pltpu.VMEM((TB, LANES), F32)],
        compiler_params=_params("arbitrary", "arbitrary"),
    )(qkv, qkv, qkv, qkv, qkv, dyc, yc, bias)


def _convpool_bwd(pa, dyab, cw, wbd, ps):
    T, D = pa.shape
    DC = D // 4
    NB = T // TB
    N = TB + HALO

    def body(pa_ref, halo_ref, dy_ref, cw_ref, wbd_ref, ps_ref, dpa_ref, dcw_ref, dwbd_ref, dps_ref, dc_carry, e_carry):
        step = pl.program_id(0)
        i = NB - 1 - step
        _acc_init(step, dcw_ref, dwbd_ref, dps_ref, dc_carry, e_carry)
        x = pa_ref[...]
        hl = jnp.where(i > 0, halo_ref[...], 0.0)
        gb, gc, u, pu = (x[:, k * DC:(k + 1) * DC] for k in range(4))
        dy = dy_ref[...]
        dya, dyb = dy[:, :DC], dy[:, DC:]
        cwv = cw_ref[...]
        z = gc * u
        z1, z2 = _conv_taps(z, hl[:, DC:2 * DC] * hl[:, 2 * DC:3 * DC])
        dgb = dya * (cwv[2:3] * z + cwv[1:2] * z1 + cwv[0:1] * z2)
        dconv = dya * gb
        dcw_ref[0:1, :] += jnp.sum(dconv * z2, axis=0, keepdims=True)
        dcw_ref[1:2, :] += jnp.sum(dconv * z1, axis=0, keepdims=True)
        dcw_ref[2:3, :] += jnp.sum(dconv * z, axis=0, keepdims=True)
        ext = jnp.concatenate([dconv, dc_carry[...]], axis=0)
        dz = cwv[2:3] * dconv + cwv[1:2] * pltpu.roll(ext, N - 1, 0)[:TB] + cwv[0:1] * pltpu.roll(ext, N - 2, 0)[:TB]
        dc_carry[...] = dconv[:HALO]
        cnt = _pool_count(i, DC)
        d = _pool_d(pu, hl[:, 3 * DC:], cnt).astype(BF16)
        psv = ps_ref[...]
        w = wbd_ref[...]
        dps_ref[0:1, :] += jnp.sum(dyb * _dot(d, w), axis=0, keepdims=True)
        dys = (dyb * psv).astype(BF16)
        dwbd_ref[...] += _dot_tn(d, dys)
        dd = _dot_nt(dys, w)
        e = dd / cnt
        ext = jnp.concatenate([e, e_carry[...]], axis=0)
        a2 = ext + pltpu.roll(ext, N - 1, 0)
        a4 = a2 + pltpu.roll(a2, N - 2, 0)
        a8 = a4 + pltpu.roll(a4, N - 4, 0)
        a16 = a8 + pltpu.roll(a8, N - 8, 0)
        dpu = _pick_group([a2, a4, a8, a16], DC)[:TB] - dd
        e_carry[...] = e[:HALO]
        dpa_ref[...] = jnp.concatenate([dgb, dz * u, dz * gc, dpu], axis=1).astype(BF16)

    blk = lambda n: pl.BlockSpec((TB, n), lambda s: (NB - 1 - s, 0))
    return pl.pallas_call(
        body, name="convpool_bwd", grid=(NB,),
        in_specs=[blk(D), pl.BlockSpec((HALO, D), lambda s: (jnp.maximum((NB - 1 - s) * (TB // HALO) - 1, 0), 0)),
                  blk(2 * DC), _full((8, DC)), _full((DC, DC)), _full((1, DC))],
        out_specs=[blk(D), _full((8, DC)), _full((DC, DC)), _full((8, DC))],
        out_shape=[jax.ShapeDtypeStruct((T, D), BF16), jax.ShapeDtypeStruct((8, DC), F32),
                   jax.ShapeDtypeStruct((DC, DC), F32), jax.ShapeDtypeStruct((8, DC), F32)],
        scratch_shapes=[pltpu.VMEM((HALO, DC), F32), pltpu.VMEM((HALO, DC), F32)],
        compiler_params=_params("arbitrary"),
    )(pa, pa, dyab, cw, wbd, ps)


def _inproj_bwd(dparts, w, h, g, dh1):
    T, D = h.shape
    widths = [p.shape[1] for p in dparts]
    n = len(dparts)

    def body(*refs):
        parts, (w_ref, h_ref, g_ref, dh1_ref, dh_ref, dg_ref) = refs[:n], refs[n:]
        _acc_init(pl.program_id(0), dg_ref)
        dxn, off = jnp.zeros((TB, D), F32), 0
        for p_ref, wd in zip(parts, widths):
            dxn = dxn + _dot_nt(p_ref[...], w_ref[:, off:off + wd])
            off += wd
        dx, dg = _rms_bwd(dxn, h_ref[...], g_ref[...])
        dg_ref[0:1, :] += dg
        dh_ref[...] = dh1_ref[...] + dx

    row = lambda m: pl.BlockSpec((TB, m), lambda i: (i, 0))
    return pl.pallas_call(
        body, name="inproj_bwd", grid=(T // TB,),
        in_specs=[row(wd) for wd in widths] + [_full(w.shape), row(D), _full((1, D)), row(D)],
        out_specs=[row(D), _full((8, D))],
        out_shape=[jax.ShapeDtypeStruct((T, D), F32), jax.ShapeDtypeStruct((8, D), F32)],
        compiler_params=_params("arbitrary"),
    )(*dparts, w, h, g, dh1)


def _wgrad(a, b, tk, tn, name):
    T, K = a.shape
    N = b.shape[1]

    def body(a_ref, b_ref, o_ref):
        _acc_init(pl.program_id(2), o_ref)
        o_ref[...] += _dot_tn(a_ref[...], b_ref[...])

    return pl.pallas_call(
        body, name=name, grid=(K // tk, N // tn, T // TB),
        in_specs=[pl.BlockSpec((TB, tk), lambda k, n, t: (t, k)), pl.BlockSpec((TB, tn), lambda k, n, t: (t, n))],
        out_specs=pl.BlockSpec((tk, tn), lambda k, n, t: (k, n)),
        out_shape=jax.ShapeDtypeStruct((K, N), F32),
        compiler_params=_params("parallel", "parallel", "arbitrary"),
    )(a, b)


def _wgrad_concat(a, bs, name):
    T, K = a.shape
    widths = [b.shape[1] for b in bs]
    n = len(bs)

    def body(*refs):
        a_ref, b_refs, o_ref = refs[0], refs[1:1 + n], refs[1 + n]
        _acc_init(pl.program_id(0), o_ref)
        av, off = a_ref[...], 0
        for b_ref, wd in zip(b_refs, widths):
            o_ref[:, off:off + wd] += _dot_tn(av, b_ref[...])
            off += wd

    row = lambda m: pl.BlockSpec((TB, m), lambda t: (t, 0))
    return pl.pallas_call(
        body, name=name, grid=(T // TB,),
        in_specs=[row(K)] + [row(wd) for wd in widths],
        out_specs=_full((K, sum(widths))),
        out_shape=jax.ShapeDtypeStruct((K, sum(widths)), F32),
        compiler_params=_params("arbitrary"),
    )(a, *bs)


def _adamw(w, g, m, v, tr, name):
    R, C = w.shape

    def body(w_ref, g_ref, m_ref, v_ref, d_ref, nm_ref, nv_ref):
        gv = g_ref[...]
        nm = ADAM_B1 * m_ref[...] + (1.0 - ADAM_B1) * gv
        nv = ADAM_B2 * v_ref[...] + (1.0 - ADAM_B2) * (gv * gv)
        m_hat = nm / (1.0 - ADAM_B1 ** ADAM_STEP)
        v_hat = nv / (1.0 - ADAM_B2 ** ADAM_STEP)
        d_ref[...] = -ADAM_LR * (m_hat / (jnp.sqrt(v_hat) + ADAM_EPS) + ADAM_WD * w_ref[...])
        nm_ref[...] = nm
        nv_ref[...] = nv

    blk = pl.BlockSpec((tr, C), lambda i: (i, 0))
    out = jax.ShapeDtypeStruct((R, C), F32)
    return pl.pallas_call(
        body, name=name, grid=(R // tr,), in_specs=[blk] * 4, out_specs=[blk] * 3, out_shape=[out] * 3,
        compiler_params=_params("parallel"),
    )(w, g, m, v)


def _sum_slots(x, tr, name):
    n, R, C = x.shape

    def body(x_ref, o_ref):
        acc = x_ref[0]
        for k in range(1, n):
            acc = acc + x_ref[k]
        o_ref[...] = acc

    return pl.pallas_call(
        body, name=name, grid=(R // tr,),
        in_specs=[pl.BlockSpec((n, tr, C), lambda i: (0, i, 0))],
        out_specs=pl.BlockSpec((tr, C), lambda i: (i, 0)),
        out_shape=jax.ShapeDtypeStruct((R, C), F32),
        compiler_params=_params("parallel"),
    )(x)


def _add_half(view, other, c, name):
    if view.ndim == 3:
        _, R2, N = view.shape
        tr = 128
        grid = (R2 // tr,)
        in_specs = [pl.BlockSpec((1, tr, N), lambda i, c_ref: (c_ref[0], i, 0)), pl.BlockSpec((1, tr, N), lambda i, c_ref: (0, i, 0))]
        out_spec = pl.BlockSpec((tr, N), lambda i, c_ref: (i, 0))
        out_shape = jax.ShapeDtypeStruct((R2, N), F32)

        def body(c_ref, a_ref, b_ref, o_ref):
            o_ref[...] = a_ref[0] + b_ref[0]
    else:
        S, _, R2, C = view.shape
        grid = (S,)
        in_specs = [pl.BlockSpec((1, 1, R2, C), lambda s, c_ref: (s, c_ref[0], 0, 0)), pl.BlockSpec((1, 1, R2, C), lambda s, c_ref: (s, 0, 0, 0))]
        out_spec = pl.BlockSpec((1, R2, C), lambda s, c_ref: (s, 0, 0))
        out_shape = jax.ShapeDtypeStruct((S, R2, C), F32)

        def body(c_ref, a_ref, b_ref, o_ref):
            o_ref[0] = a_ref[0, 0] + b_ref[0, 0]

    return pl.pallas_call(
        body, name=name,
        grid_spec=pltpu.PrefetchScalarGridSpec(num_scalar_prefetch=1, grid=grid, in_specs=in_specs, out_specs=out_spec),
        out_shape=out_shape, compiler_params=_params("parallel"),
    )(c, view, other)


def _place():
    x, y, c = lax.axis_index("x"), lax.axis_index("y"), lax.axis_index("c")
    chips = [(1 - x, y), (x, 1 - y), (1 - x, 1 - y)]
    return x, y, c, 2 * x + y, chips


def _remote(src, dst, send_sems, recv_sems, k, to):
    return pltpu.make_async_remote_copy(src_ref=src, dst_ref=dst, send_sem=send_sems.at[k], recv_sem=recv_sems.at[k],
                                        device_id=to, device_id_type=MESH)


def _allgather_weights(shards, kinds):
    n = len(shards)
    full_shapes = [(s.shape[0], s.shape[1], 4 * s.shape[2]) if k == "col" else (s.shape[0], 4 * s.shape[1], s.shape[2])
                   for s, k in zip(shards, kinds)]

    def body(*refs):
        ins, outs, (send_sems, recv_sems, local_sems) = refs[:n], refs[n:2 * n], refs[2 * n:]
        x, y, c, me, chips = _place()

        def slab(a, chip, half=None):
            _, R, C = shards[a].shape
            rows = pl.ds(0, R) if half is None else pl.ds(half * (R // 2), R // 2)
            if kinds[a] == "col":
                return outs[a].at[:, rows, pl.ds(chip * C, C)]
            start = chip * R if half is None else chip * R + half * (R // 2)
            return outs[a].at[:, pl.ds(start, R if half is None else R // 2), :]

        def my_half(a):
            R = shards[a].shape[1]
            return ins[a].at[:, pl.ds(c * (R // 2), R // 2), :]

        local = [pltpu.make_async_copy(ins[a], slab(a, me), local_sems.at[a]) for a in range(n)]
        for cp in local:
            cp.start()
        first = [_remote(my_half(a), slab(a, me, c), send_sems, recv_sems, j * n + a, (*chip, c))
                 for j, chip in enumerate(chips) for a in range(n)]
        for cp in first:
            cp.start()
        passed = []
        for j, chip in enumerate(chips):
            src = 2 * chip[0] + chip[1]
            for a in range(n):
                _remote(my_half(a), slab(a, src, c), send_sems, recv_sems, j * n + a, (x, y, c)).wait_recv()
                cp = _remote(slab(a, src, c), slab(a, src, c), send_sems, recv_sems, (3 + j) * n + a, (x, y, 1 - c))
                cp.start()
                passed.append(cp)
        for j, chip in enumerate(chips):
            src = 2 * chip[0] + chip[1]
            for a in range(n):
                _remote(my_half(a), slab(a, src, 1 - c), send_sems, recv_sems, (3 + j) * n + a, (x, y, c)).wait_recv()
        for cp in first + passed:
            cp.wait_send()
        for cp in local:
            cp.wait()

    return pl.pallas_call(
        body, name="allgather_weights",
        in_specs=[ANY] * n, out_specs=[ANY] * n,
        out_shape=[jax.ShapeDtypeStruct(s, BF16) for s in full_shapes],
        scratch_shapes=[pltpu.SemaphoreType.DMA((6 * n,)), pltpu.SemaphoreType.DMA((6 * n,)), pltpu.SemaphoreType.DMA((n,))],
        compiler_params=pltpu.CompilerParams(has_side_effects=True),
    )(*shards)


def _swap_halves(views):
    n = len(views)
    out_shapes = [(1,) + v.shape[1:] if v.ndim == 3 else (v.shape[0], 1) + v.shape[2:] for v in views]

    def body(*refs):
        ins, outs, (send_sems, recv_sems) = refs[:n], refs[n:2 * n], refs[2 * n:]
        x, y, c, _, _ = _place()
        cps = []
        for a in range(n):
            src = ins[a].at[pl.ds(1 - c, 1)] if views[a].ndim == 3 else ins[a].at[:, pl.ds(1 - c, 1)]
            cps.append(_remote(src, outs[a], send_sems, recv_sems, a, (x, y, 1 - c)))
        for cp in cps:
            cp.start()
        for cp in cps:
            cp.wait()

    return pl.pallas_call(
        body, name="grad_swap_halves", in_specs=[ANY] * n, out_specs=[ANY] * n,
        out_shape=[jax.ShapeDtypeStruct(s, F32) for s in out_shapes],
        scratch_shapes=[pltpu.SemaphoreType.DMA((n,)), pltpu.SemaphoreType.DMA((n,))],
        compiler_params=pltpu.CompilerParams(has_side_effects=True),
    )(*views)


def _exchange_chips(parts, kinds):
    n = len(parts)
    out_shapes = [(4, p.shape[0], p.shape[1] // 4) if k == "col" else p.shape for p, k in zip(parts, kinds)]

    def body(*refs):
        ins, outs, (send_sems, recv_sems, local_sems) = refs[:n], refs[n:2 * n], refs[2 * n:]
        x, y, c, me, chips = _place()

        def slab(a, chip):
            if kinds[a] == "col":
                C = out_shapes[a][2]
                return ins[a].at[:, pl.ds(chip * C, C)]
            return ins[a].at[chip]

        local = [pltpu.make_async_copy(slab(a, me), outs[a].at[me], local_sems.at[a]) for a in range(n)]
        for cp in local:
            cp.start()
        sends = []
        for j, chip in enumerate(chips):
            dst = 2 * chip[0] + chip[1]
            sends += [_remote(slab(a, dst), outs[a].at[me], send_sems, recv_sems, j * n + a, (*chip, c)) for a in range(n)]
        for cp in sends:
            cp.start()
        for j, chip in enumerate(chips):
            src = 2 * chip[0] + chip[1]
            for a in range(n):
                _remote(slab(a, src), outs[a].at[src], send_sems, recv_sems, j * n + a, (x, y, c)).wait_recv()
        for cp in sends:
            cp.wait_send()
        for cp in local:
            cp.wait()

    return pl.pallas_call(
        body, name="grad_exchange_chips", in_specs=[ANY] * n, out_specs=[ANY] * n,
        out_shape=[jax.ShapeDtypeStruct(s, F32) for s in out_shapes],
        scratch_shapes=[pltpu.SemaphoreType.DMA((3 * n,)), pltpu.SemaphoreType.DMA((3 * n,)), pltpu.SemaphoreType.DMA((n,))],
        compiler_params=pltpu.CompilerParams(has_side_effects=True),
    )(*parts)


def _join_halves(halves):
    n = len(halves)

    def body(*refs):
        ins, outs, (send_sems, recv_sems, local_sems) = refs[:n], refs[n:2 * n], refs[2 * n:]
        x, y, c, _, _ = _place()
        local = [pltpu.make_async_copy(ins[a], outs[a].at[c], local_sems.at[a]) for a in range(n)]
        sends = [_remote(ins[a], outs[a].at[c], send_sems, recv_sems, a, (x, y, 1 - c)) for a in range(n)]
        for cp in local + sends:
            cp.start()
        for a in range(n):
            _remote(ins[a], outs[a].at[1 - c], send_sems, recv_sems, a, (x, y, c)).wait_recv()
        for cp in sends:
            cp.wait_send()
        for cp in local:
            cp.wait()

    return pl.pallas_call(
        body, name="grad_join_halves", in_specs=[ANY] * n, out_specs=[ANY] * n,
        out_shape=[jax.ShapeDtypeStruct((2,) + h.shape, F32) for h in halves],
        scratch_shapes=[pltpu.SemaphoreType.DMA((n,)), pltpu.SemaphoreType.DMA((n,)), pltpu.SemaphoreType.DMA((n,))],
        compiler_params=pltpu.CompilerParams(has_side_effects=True),
    )(*halves)


def _allgather_small(block):
    M, N = block.shape

    def body(x_ref, out_ref, send_sems, recv_sems, local_sem):
        x, y, c, _, chips = _place()
        me, sibling = (x, y, c), (x, y, 1 - c)

        def rows(px, py, pc):
            return out_ref.at[pl.ds((4 * px + 2 * py + pc) * M, M), :]

        def copy(k, blk, to, src=None):
            return _remote(rows(*blk) if src is None else src, rows(*blk), send_sems, recv_sems, k, to)

        mine = pltpu.make_async_copy(x_ref, rows(*me), local_sem)
        mine.start()
        first = [copy(0, me, sibling, src=x_ref)] + [copy(1 + j, me, (*chip, c), src=x_ref) for j, chip in enumerate(chips)]
        for cp in first:
            cp.start()
        passed = [copy(4 + j, (*chip, c), sibling) for j, chip in enumerate(chips)]
        for j, chip in enumerate(chips):
            copy(1 + j, (*chip, c), me).wait_recv()
            passed[j].start()
        copy(0, sibling, me).wait_recv()
        for j, chip in enumerate(chips):
            copy(4 + j, (*chip, 1 - c), me).wait_recv()
        for cp in first + passed:
            cp.wait_send()
        mine.wait()

    vm = pl.BlockSpec(memory_space=pltpu.VMEM)
    return pl.pallas_call(
        body, name="allgather_small", in_specs=[vm], out_specs=vm,
        out_shape=jax.ShapeDtypeStruct((8 * M, N), F32),
        scratch_shapes=[pltpu.SemaphoreType.DMA((7,)), pltpu.SemaphoreType.DMA((7,)), pltpu.SemaphoreType.DMA],
        compiler_params=pltpu.CompilerParams(has_side_effects=True, vmem_limit_bytes=VMEM_LIMIT),
    )(block)


def _pack(arrays):
    flat = jnp.concatenate([a.reshape(-1) for a in arrays])
    pad = (-flat.shape[0]) % (8 * LANES)
    return jnp.pad(flat, (0, pad)).reshape(-1, LANES)


def _unpack(packed, shapes):
    flat, out, off = packed.reshape(-1), [], 0
    for s in shapes:
        size = 1
        for d in s:
            size *= d
        out.append(flat[off:off + size].reshape(s))
        off += size
    return out


def _block_diag(pw):
    G, n, _ = pw.shape
    eye = jnp.eye(G, dtype=pw.dtype)
    return (eye[:, None, :, None] * pw[:, :, None, :]).reshape(G * n, G * n)


def _diag_blocks(m, G):
    n = m.shape[0] // G
    return jnp.stack([m[g * n:(g + 1) * n, g * n:(g + 1) * n] for g in range(G)])


def _pad_rows(a, rows):
    return jnp.pad(a, ((0, rows - a.shape[0]), (0, 0)))


def kernel(x, w_in, w_out, conv_w, pool_w, pool_scale, rel_bias, group_gain, pre_mix_g, post_mix_g, pre_ffn_g, post_ffn_g, w_gate_up, w_down, loss_target, m_w_in, m_w_out, m_conv_w, m_pool_w, m_pool_scale, m_rel_bias, m_group_gain, m_pre_mix_g, m_post_mix_g, m_pre_ffn_g, m_post_ffn_g, m_w_gate_up, m_w_down, v_w_in, v_w_out, v_conv_w, v_pool_w, v_pool_scale, v_rel_bias, v_group_gain, v_pre_mix_g, v_post_mix_g, v_pre_ffn_g, v_post_ffn_g, v_w_gate_up, v_w_down):
    L = w_in.shape[0]
    T, D = x.shape[1], x.shape[2]
    DC = D // 4
    NH = rel_bias.shape[1]
    NREL = rel_bias.shape[2]
    G = pool_w.shape[1]
    cs = conv_w.shape[2]
    assert TB == LEFT_CHUNKS * CHUNK and T % TB == 0 and D % (4 * LANES) == 0 and NH * HEAD_DIM == D // 2
    xi, yi, ci = lax.axis_index("x"), lax.axis_index("y"), lax.axis_index("c")
    chip = 2 * xi + yi

    kinds = ("col", "row", "col", "row")
    big = (w_in, w_out, w_gate_up, w_down)
    wi_f, wo_f, wgu_f, wdn_f = _allgather_weights([w.astype(BF16) for w in big], kinds)
    conv_all = _allgather_small(_pack([conv_w])).reshape(8, -1)[:, :L * 3 * cs].reshape(4, 2, L, 3, cs)[:, 0]
    conv_full = jnp.moveaxis(conv_all, 0, 2).reshape(L, 3, 4 * cs)

    h = x[0]
    saved = []
    for l in range(L):
        g_pre, g_pm, g_pf, g_po = (a[l][None] for a in (pre_mix_g, post_mix_g, pre_ffn_g, post_ffn_g))
        gg, ps = group_gain[l][None], pool_scale[l][None]
        cw = _pad_rows(conv_full[l], 8)
        wbd = _block_diag(pool_w[l]).astype(BF16)
        bias = _bias_build(jnp.pad(rel_bias[l], ((0, 0), (0, RBP - NREL)))[:, None, :])
        xn, pa, qkv = _inproj_fwd(h, g_pre, wi_f[l])
        yab = _convpool_fwd(pa, cw, wbd, ps)
        yc = _attn_fwd(qkv, bias)
        y, mix, h1 = _mix_out_fwd(yab, yc, gg, wo_f[l], h, g_pm)
        hn, gu, ff = _ffn_up_fwd(h1, g_pf, wgu_f[l])
        ffo, h2 = _ffn_down_fwd(ff, wdn_f[l], h1, g_po)
        saved.append(dict(h=h, xn=xn, pa=pa, qkv=qkv, yab=yab, yc=yc, y=y, mix=mix, h1=h1, hn=hn, gu=gu, ff=ff, ffo=ffo,
                          cw=cw, wbd=wbd, bias=bias, ps=ps, gg=gg, g_pre=g_pre, g_pm=g_pm, g_pf=g_pf, g_po=g_po))
        h = h2

    dh, loss_tile = _loss_grad(h, loss_target[0])
    loss = lax.psum(loss_tile[0, 0], ("x", "y", "c"))

    big_grads = [None] * L
    small_grads = [None] * L
    for l in reversed(range(L)):
        s = saved[l]
        dffo, dgu, dg_po = _ffn_down_bwd(dh, s["ffo"], s["g_po"], s["gu"], wdn_f[l])
        dh1, dg_pf = _ffn_up_bwd(dgu, wgu_f[l], s["h1"], s["g_pf"], dh)
        dmix, dyab, dyc, dg_pm, dgg = _mix_out_bwd(dh1, s["mix"], s["g_pm"], wo_f[l], s["yab"], s["yc"], s["gg"])
        dq, dk, dv, dbias = _attn_bwd(s["qkv"], dyc, s["yc"], s["bias"])
        dpa, dcw, dwbd, dps = _convpool_bwd(s["pa"], dyab, s["cw"], s["wbd"], s["ps"])
        dparts = [dpa, dq, dk, dv]
        dh, dg_pre = _inproj_bwd(dparts, wi_f[l], s["h"], s["g_pre"], dh1)
        F2, DFF = s["gu"].shape[1], s["ff"].shape[1]
        big_grads[l] = [_wgrad_concat(s["xn"], dparts, "wgrad_in"),
                        _wgrad(s["y"], dmix, D, D, "wgrad_out"),
                        _wgrad(s["hn"], dgu, D, F2 // 4, "wgrad_gate_up"),
                        _wgrad(s["ff"], dffo, DFF // 2, D, "wgrad_down")]
        drb = _bias_fold(dbias)[:, 0, :NREL]
        small_grads[l] = [dcw[:3], _diag_blocks(dwbd, G), dps[0], drb, dgg[0], dg_pre[0], dg_pm[0], dg_pf[0], dg_po[0]]

    flat_grads = [g for l in range(L) for g in big_grads[l]]
    flat_kinds = kinds * L
    views = [g.reshape(2, g.shape[0] // 2, g.shape[1]) if k == "col" else g.reshape(4, 2, g.shape[0] // 8, g.shape[1])
             for g, k in zip(flat_grads, flat_kinds)]
    theirs = _swap_halves(views)
    cvec = ci.reshape(1).astype(jnp.int32)
    chip_sums = [_add_half(v, t, cvec, "grad_add_half") for v, t in zip(views, theirs)]
    slots = _exchange_chips(chip_sums, flat_kinds)
    halves = [_sum_slots(sl, min(sl.shape[1], 128) if k == "col" else sl.shape[1], "grad_sum_chips") for sl, k in zip(slots, flat_kinds)]
    joined = _join_halves(halves)
    shard_grads = [j.reshape(2 * j.shape[1], j.shape[2]) for j in joined]
    g_big = [jnp.stack([shard_grads[l * 4 + k] for l in range(L)]) for k in range(4)]

    names_shapes = [(L, 3, 4 * cs), pool_w.shape, pool_scale.shape, rel_bias.shape, group_gain.shape,
                    pre_mix_g.shape, post_mix_g.shape, pre_ffn_g.shape, post_ffn_g.shape]
    small_stacked = [jnp.stack([small_grads[l][k] for l in range(L)]) for k in range(len(names_shapes))]
    packed = _pack(small_stacked)
    M = packed.shape[0]
    total = _sum_slots(_allgather_small(packed).reshape(8, M, LANES), M, "small_sum_devices")
    g_small = _unpack(total, names_shapes)
    g_small[0] = lax.dynamic_slice_in_dim(g_small[0], chip * cs, cs, axis=2)

    def adam_big(w, g, m, v, name):
        shp = w.shape
        two = lambda a: a.reshape(shp[0] * shp[1], shp[2])
        return [o.reshape(shp) for o in _adamw(two(w), two(g), two(m), two(v), 256, name)]

    upd_in = adam_big(w_in, g_big[0], m_w_in, v_w_in, "adamw_in")
    upd_out = adam_big(w_out, g_big[1], m_w_out, v_w_out, "adamw_out")
    upd_gu = adam_big(w_gate_up, g_big[2], m_w_gate_up, v_w_gate_up, "adamw_gate_up")
    upd_dn = adam_big(w_down, g_big[3], m_w_down, v_w_down, "adamw_down")

    small_w = [conv_w, pool_w, pool_scale, rel_bias, group_gain, pre_mix_g, post_mix_g, pre_ffn_g, post_ffn_g]
    small_m = [m_conv_w, m_pool_w, m_pool_scale, m_rel_bias, m_group_gain, m_pre_mix_g, m_post_mix_g, m_pre_ffn_g, m_post_ffn_g]
    small_v = [v_conv_w, v_pool_w, v_pool_scale, v_rel_bias, v_group_gain, v_pre_mix_g, v_post_mix_g, v_pre_ffn_g, v_post_ffn_g]
    pw_, pg_, pm_, pv_ = _pack(small_w), _pack(g_small), _pack(small_m), _pack(small_v)
    shapes = [w.shape for w in small_w]
    upd_small = [_unpack(o, shapes) for o in _adamw(pw_, pg_, pm_, pv_, pw_.shape[0], "adamw_small")]

    def ordered(big4, small9):
        return [big4[0], big4[1], *small9, big4[2], big4[3]]

    grads = ordered(g_big, g_small)
    outs = [ordered([upd_in[k], upd_out[k], upd_gu[k], upd_dn[k]], upd_small[k]) for k in range(3)]
    return (loss, dh[None], *grads, *outs[0], *outs[1], *outs[2])
```

```python
import functools

import jax
import jax.numpy as jnp
from jax import lax
from jax.experimental import pallas as pl
from jax.experimental.pallas import tpu as pltpu

F32, BF16 = jnp.float32, jnp.bfloat16
EPS = 1e-6
CHUNK = 64
LEFT_CHUNKS = 8
REL_CLIP = 128
HEAD_DIM = 64
LANES = 128
POOL_WINDOWS = (2, 4, 8, 16)
HALO = 16
TB = LEFT_CHUNKS * CHUNK
TBF = 256
BAND = (LEFT_CHUNKS + 1) * CHUNK
NEG = -1e30
RBP = 384
VMEM_LIMIT = 56 * 1024 * 1024
ADAM_LR, ADAM_B1, ADAM_B2, ADAM_EPS, ADAM_WD, ADAM_STEP = 0.001, 0.9, 0.999, 1e-08, 0.01, 10
MESH = pl.DeviceIdType.MESH
ANY = pl.BlockSpec(memory_space=pl.ANY)


def _params(*sem):
    kw = dict(vmem_limit_bytes=VMEM_LIMIT)
    if sem:
        kw["dimension_semantics"] = sem
    return pltpu.CompilerParams(**kw)


def _dot(a, b):
    return jnp.dot(a, b, preferred_element_type=F32)


def _dot_nt(a, b):
    return lax.dot_general(a, b, (((1,), (1,)), ((), ())), preferred_element_type=F32)


def _dot_tn(a, b):
    return lax.dot_general(a, b, (((0,), (0,)), ((), ())), preferred_element_type=F32)


def _rms(x, g):
    r = lax.rsqrt(jnp.mean(x * x, axis=-1, keepdims=True) + EPS)
    return x * r * g


def _rms_bwd(dy, x, g):
    r = lax.rsqrt(jnp.mean(x * x, axis=-1, keepdims=True) + EPS)
    xh = x * r
    dxh = dy * g
    dx = r * (dxh - xh * jnp.mean(dxh * xh, axis=-1, keepdims=True))
    return dx, jnp.sum(dy * xh, axis=0, keepdims=True)


def _full(shape):
    return pl.BlockSpec(shape, lambda *_: (0,) * len(shape))


def _acc_init(step, *refs):
    @pl.when(step == 0)
    def _():
        for r in refs:
            r[...] = jnp.zeros_like(r)


def _inproj_fwd(h, g, w):
    T, D = h.shape
    NQ = w.shape[1] - D

    def body(h_ref, g_ref, w_ref, xn_ref, pa_ref, qkv_ref):
        xn = _rms(h_ref[...], g_ref[...]).astype(BF16)
        xn_ref[...] = xn
        pa_ref[...] = _dot(xn, w_ref[:, :D])
        qkv_ref[...] = _dot(xn, w_ref[:, D:]).astype(BF16)

    row = lambda n: pl.BlockSpec((TB, n), lambda i: (i, 0))
    return pl.pallas_call(
        body, name="inproj_fwd", grid=(T // TB,),
        in_specs=[row(D), _full((1, D)), _full(w.shape)],
        out_specs=[row(D), row(D), row(NQ)],
        out_shape=[jax.ShapeDtypeStruct((T, D), BF16), jax.ShapeDtypeStruct((T, D), F32),
                   jax.ShapeDtypeStruct((T, NQ), BF16)],
        compiler_params=_params("parallel"),
    )(h, g, w)


def _lane_groups(n, vals):
    lane = lax.broadcasted_iota(jnp.int32, (1, n), 1)
    q = n // 4
    return jnp.where(lane < q, vals[0], jnp.where(lane < 2 * q, vals[1], jnp.where(lane < 3 * q, vals[2], vals[3]))).astype(F32)


def _pick_group(levels, n):
    lane = lax.broadcasted_iota(jnp.int32, (1, n), 1)
    q = n // 4
    return jnp.where(lane < q, levels[0], jnp.where(lane < 2 * q, levels[1], jnp.where(lane < 3 * q, levels[2], levels[3])))


def _pool_count(blk, n):
    t1 = (blk * TB + 1 + lax.broadcasted_iota(jnp.int32, (TB, 1), 0)).astype(F32)
    return jnp.minimum(t1, _lane_groups(n, POOL_WINDOWS))


def _pool_d(pu, pu_halo, cnt):
    e = jnp.concatenate([pu_halo, pu], axis=0)
    s2 = e + pltpu.roll(e, 1, 0)
    s4 = s2 + pltpu.roll(s2, 2, 0)
    s8 = s4 + pltpu.roll(s4, 4, 0)
    s16 = s8 + pltpu.roll(s8, 8, 0)
    num = _pick_group([s2, s4, s8, s16], pu.shape[1])[HALO:]
    return num / cnt - pu


def _conv_taps(z, z_halo):
    e = jnp.concatenate([z_halo, z], axis=0)
    return pltpu.roll(e, 1, 0)[HALO:], pltpu.roll(e, 2, 0)[HALO:]


def _convpool_fwd(pa, cw, wbd, ps):
    T, D = pa.shape
    DC = D // 4

    def body(pa_ref, halo_ref, cw_ref, wbd_ref, ps_ref, yab_ref):
        i = pl.program_id(0)
        x = pa_ref[...]
        hl = jnp.where(i > 0, halo_ref[...], 0.0)
        gb, gc, u, pu = (x[:, k * DC:(k + 1) * DC] for k in range(4))
        z = gc * u
        z1, z2 = _conv_taps(z, hl[:, DC:2 * DC] * hl[:, 2 * DC:3 * DC])
        cwv = cw_ref[...]
        ya = gb * (cwv[2:3] * z + cwv[1:2] * z1 + cwv[0:1] * z2)
        d = _pool_d(pu, hl[:, 3 * DC:], _pool_count(i, DC))
        yb = _dot(d.astype(BF16), wbd_ref[...]) * ps_ref[...]
        yab_ref[...] = jnp.concatenate([ya, yb], axis=1)

    return pl.pallas_call(
        body, name="convpool_fwd", grid=(T // TB,),
        in_specs=[pl.BlockSpec((TB, D), lambda i: (i, 0)),
                  pl.BlockSpec((HALO, D), lambda i: (jnp.maximum(i * (TB // HALO) - 1, 0), 0)),
                  _full((8, DC)), _full((DC, DC)), _full((1, DC))],
        out_specs=pl.BlockSpec((TB, 2 * DC), lambda i: (i, 0)),
        out_shape=jax.ShapeDtypeStruct((T, 2 * DC), F32),
        compiler_params=_params("parallel"),
    )(pa, pa, cw, wbd, ps)


def _bias_bins(shape, col_dim):
    j = lax.broadcasted_iota(jnp.int32, shape, col_dim)
    b = lax.broadcasted_iota(jnp.int32, shape, 1 - col_dim)
    d = jnp.where(j < BAND, j, j - 2 * TB)
    live = jnp.logical_or(j < BAND, j > 2 * TB - CHUNK)
    bins = jnp.minimum(TB - d, REL_CLIP) + REL_CLIP
    return jnp.where(jnp.logical_and(live, bins == b), 1.0, 0.0).astype(F32)


def _bias_build(rb):
    H = rb.shape[0]
    W = 2 * TB

    def body(rb_ref, o_ref):
        v = jnp.dot(jnp.broadcast_to(rb_ref[0], (8, RBP)), _bias_bins((RBP, W), 1),
                    precision=lax.Precision.HIGHEST, preferred_element_type=F32)
        x = jnp.broadcast_to(v[0:1], (TB, W))
        row = lax.broadcasted_iota(jnp.int32, (TB, W), 0)
        col = lax.broadcasted_iota(jnp.int32, (TB, W), 1)
        for b in range(TB.bit_length() - 1):
            x = jnp.where(((row >> b) & 1) == 1, pltpu.roll(x, 1 << b, 1), x)
        qc, kc = row >> (CHUNK.bit_length() - 1), col >> (CHUNK.bit_length() - 1)
        o_ref[0] = jnp.where(jnp.logical_and(kc >= qc, kc <= qc + LEFT_CHUNKS), x, NEG)

    return pl.pallas_call(
        body, name="bias_build", grid=(H,),
        in_specs=[pl.BlockSpec((1, 1, RBP), lambda h: (h, 0, 0))],
        out_specs=pl.BlockSpec((1, TB, W), lambda h: (h, 0, 0)),
        out_shape=jax.ShapeDtypeStruct((H, TB, W), F32),
        compiler_params=_params("parallel"),
    )(rb)


def _bias_fold(db):
    H = db.shape[0]
    W = 2 * TB

    def body(db_ref, o_ref):
        x = db_ref[0]
        row = lax.broadcasted_iota(jnp.int32, (TB, W), 0)
        for b in range(TB.bit_length() - 1):
            x = jnp.where(((row >> b) & 1) == 1, pltpu.roll(x, W - (1 << b), 1), x)
        dsum = jnp.sum(x, axis=0, keepdims=True)
        o_ref[0] = jnp.dot(jnp.broadcast_to(dsum, (8, W)), _bias_bins((W, RBP), 0),
                           precision=lax.Precision.HIGHEST, preferred_element_type=F32)

    return pl.pallas_call(
        body, name="bias_fold", grid=(H,),
        in_specs=[pl.BlockSpec((1, TB, W), lambda h: (h, 0, 0))],
        out_specs=pl.BlockSpec((1, 8, RBP), lambda h: (h, 0, 0)),
        out_shape=jax.ShapeDtypeStruct((H, 8, RBP), F32),
        compiler_params=_params("parallel"),
    )(db)


def _head_masks():
    lane = lax.broadcasted_iota(jnp.int32, (1, LANES), 1)
    return lane < HEAD_DIM, lane >= HEAD_DIM


def _softmax_rows(s):
    e = jnp.exp(s - jnp.max(s, axis=-1, keepdims=True))
    return e * (1.0 / jnp.sum(e, axis=-1, keepdims=True))


def _attn_fwd(qkv, bias):
    T = qkv.shape[0]
    NP = qkv.shape[1] // (3 * LANES)
    NB = T // TB
    scale = HEAD_DIM ** -0.5

    def body(q_ref, k_ref, v_ref, b_ref, o_ref, kprev, vprev):
        i = pl.program_id(1)

        @pl.when(i == 0)
        def _():
            kprev[...] = jnp.zeros_like(kprev)
            vprev[...] = jnp.zeros_like(vprev)

        q, kc, vc = q_ref[...], k_ref[...], v_ref[...]
        kwin = jnp.concatenate([kprev[...], kc], axis=0)
        vwin = jnp.concatenate([vprev[...], vc], axis=0)
        col = lax.broadcasted_iota(jnp.int32, (1, 2 * TB), 1)
        before_start = jnp.logical_and(col < TB, i == 0)
        acc = jnp.zeros((TB, LANES), F32)
        for a, ma in enumerate(_head_masks()):
            s = _dot_nt(jnp.where(ma, q, 0), kwin) * scale + b_ref[a]
            p = _softmax_rows(jnp.where(before_start, NEG, s))
            acc = acc + _dot(p.astype(BF16), jnp.where(ma, vwin, 0))
        o_ref[...] = acc
        kprev[...] = kc
        vprev[...] = vc

    col = lambda off: pl.BlockSpec((TB, LANES), lambda p, i: (i, off + p))
    return pl.pallas_call(
        body, name="attn_fwd", grid=(NP, NB),
        in_specs=[col(0), col(NP), col(2 * NP), pl.BlockSpec((2, TB, 2 * TB), lambda p, i: (p, 0, 0))],
        out_specs=col(0),
        out_shape=jax.ShapeDtypeStruct((T, NP * LANES), F32),
        scratch_shapes=[pltpu.VMEM((TB, LANES), BF16), pltpu.VMEM((TB, LANES), BF16)],
        compiler_params=_params("arbitrary", "arbitrary"),
    )(qkv, qkv, qkv, bias)


def _group_bounds(D):
    return ((0, D // 4), (D // 4, D // 2), (D // 2, D))


def _mix_out_fwd(yab, yc, gg, w, h, g):
    T, D = h.shape

    def body(yab_ref, yc_ref, gg_ref, w_ref, h_ref, g_ref, y_ref, mix_ref, h1_ref):
        yraw = jnp.concatenate([yab_ref[...], yc_ref[...]], axis=1)
        ggv = gg_ref[...]
        y = jnp.concatenate([_rms(yraw[:, a:b], ggv[:, a:b]) for a, b in _group_bounds(D)], axis=1).astype(BF16)
        y_ref[...] = y
        mix = _dot(y, w_ref[...])
        mix_ref[...] = mix
        h1_ref[...] = h_ref[...] + _rms(mix, g_ref[...])

    row = lambda n: pl.BlockSpec((TB, n), lambda i: (i, 0))
    return pl.pallas_call(
        body, name="mix_out_fwd", grid=(T // TB,),
        in_specs=[row(D // 2), row(D // 2), _full((1, D)), _full((D, D)), row(D), _full((1, D))],
        out_specs=[row(D), row(D), row(D)],
        out_shape=[jax.ShapeDtypeStruct((T, D), BF16), jax.ShapeDtypeStruct((T, D), F32), jax.ShapeDtypeStruct((T, D), F32)],
        compiler_params=_params("parallel"),
    )(yab, yc, gg, w, h, g)


def _load_resident(step, w_hbm, w_vmem, sem):
    @pl.when(step == 0)
    def _():
        cp = pltpu.make_async_copy(w_hbm, w_vmem, sem)
        cp.start()
        cp.wait()


def _ffn_up_fwd(h1, g, w):
    T, D = h1.shape
    F2 = w.shape[1]
    DFF = F2 // 2
    TF = DFF // 2

    def body(h_ref, g_ref, w_hbm, hn_ref, gu_ref, ff_ref, w_v, sem):
        _load_resident(pl.program_id(0), w_hbm, w_v, sem)
        hn = _rms(h_ref[...], g_ref[...]).astype(BF16)
        hn_ref[...] = hn
        for j in range(2):
            sg, su = slice(j * TF, (j + 1) * TF), slice(DFF + j * TF, DFF + (j + 1) * TF)
            gate = _dot(hn, w_v[:, sg])
            up = _dot(hn, w_v[:, su])
            gu_ref[:, sg] = gate
            gu_ref[:, su] = up
            ff_ref[:, sg] = (gate * jax.nn.sigmoid(gate) * up).astype(BF16)

    row = lambda n: pl.BlockSpec((TBF, n), lambda i: (i, 0))
    return pl.pallas_call(
        body, name="ffn_up_fwd", grid=(T // TBF,),
        in_specs=[row(D), _full((1, D)), ANY],
        out_specs=[row(D), row(F2), row(DFF)],
        out_shape=[jax.ShapeDtypeStruct((T, D), BF16), jax.ShapeDtypeStruct((T, F2), F32), jax.ShapeDtypeStruct((T, DFF), BF16)],
        scratch_shapes=[pltpu.VMEM(w.shape, BF16), pltpu.SemaphoreType.DMA],
        compiler_params=_params("arbitrary"),
    )(h1, g, w)


def _ffn_down_fwd(ff, w, h1, g):
    T, D = h1.shape
    DFF = ff.shape[1]

    def body(ff_ref, w_ref, h_ref, g_ref, ffo_ref, h2_ref):
        ffo = _dot(ff_ref[...], w_ref[...])
        ffo_ref[...] = ffo
        h2_ref[...] = h_ref[...] + _rms(ffo, g_ref[...])

    row = lambda n: pl.BlockSpec((TBF, n), lambda i: (i, 0))
    return pl.pallas_call(
        body, name="ffn_down_fwd", grid=(T // TBF,),
        in_specs=[row(DFF), _full((DFF, D)), row(D), _full((1, D))],
        out_specs=[row(D), row(D)],
        out_shape=[jax.ShapeDtypeStruct((T, D), F32), jax.ShapeDtypeStruct((T, D), F32)],
        compiler_params=_params("parallel"),
    )(ff, w, h1, g)


def _loss_grad(h, tgt):
    T, D = h.shape

    def body(h_ref, t_ref, dh_ref, loss_ref):
        _acc_init(pl.program_id(0), loss_ref)
        diff = h_ref[...] - t_ref[...]
        dh_ref[...] = diff * (1.0 / D)
        loss_ref[...] += 0.5 * jnp.sum(jnp.mean(diff * diff, axis=-1, keepdims=True))

    row = pl.BlockSpec((TB, D), lambda i: (i, 0))
    return pl.pallas_call(
        body, name="loss_grad", grid=(T // TB,),
        in_specs=[row, row], out_specs=[row, _full((8, LANES))],
        out_shape=[jax.ShapeDtypeStruct((T, D), F32), jax.ShapeDtypeStruct((8, LANES), F32)],
        compiler_params=_params("arbitrary"),
    )(h, tgt)


def _ffn_down_bwd(dh2, ffo, g, gu, w):
    T, D = dh2.shape
    F2 = gu.shape[1]
    DFF = F2 // 2
    TF = DFF // 2

    def body(dh_ref, ffo_ref, g_ref, gu_ref, w_ref, dffo_ref, dgu_ref, dg_ref):
        _acc_init(pl.program_id(0), dg_ref)
        dffo, dg = _rms_bwd(dh_ref[...], ffo_ref[...], g_ref[...])
        dg_ref[0:1, :] += dg
        dffo = dffo.astype(BF16)
        dffo_ref[...] = dffo
        for j in range(2):
            sg, su = slice(j * TF, (j + 1) * TF), slice(DFF + j * TF, DFF + (j + 1) * TF)
            dff = _dot_nt(dffo, w_ref[sg, :])
            gate, up = gu_ref[:, sg], gu_ref[:, su]
            sig = jax.nn.sigmoid(gate)
            dgu_ref[:, sg] = (dff * up * (sig * (1.0 + gate * (1.0 - sig)))).astype(BF16)
            dgu_ref[:, su] = (dff * (gate * sig)).astype(BF16)

    row = lambda n: pl.BlockSpec((TBF, n), lambda i: (i, 0))
    return pl.pallas_call(
        body, name="ffn_down_bwd", grid=(T // TBF,),
        in_specs=[row(D), row(D), _full((1, D)), row(F2), _full((DFF, D))],
        out_specs=[row(D), row(F2), _full((8, D))],
        out_shape=[jax.ShapeDtypeStruct((T, D), BF16), jax.ShapeDtypeStruct((T, F2), BF16), jax.ShapeDtypeStruct((8, D), F32)],
        compiler_params=_params("arbitrary"),
    )(dh2, ffo, g, gu, w)


def _ffn_up_bwd(dgu, w, h1, g, dh2):
    T, D = h1.shape
    F2 = dgu.shape[1]

    def body(dgu_ref, w_hbm, h_ref, g_ref, dh2_ref, dh1_ref, dg_ref, w_v, sem):
        _load_resident(pl.program_id(0), w_hbm, w_v, sem)
        _acc_init(pl.program_id(0), dg_ref)
        dx, dg = _rms_bwd(_dot_nt(dgu_ref[...], w_v[...]), h_ref[...], g_ref[...])
        dg_ref[0:1, :] += dg
        dh1_ref[...] = dh2_ref[...] + dx

    row = lambda n: pl.BlockSpec((TBF, n), lambda i: (i, 0))
    return pl.pallas_call(
        body, name="ffn_up_bwd", grid=(T // TBF,),
        in_specs=[row(F2), ANY, row(D), _full((1, D)), row(D)],
        out_specs=[row(D), _full((8, D))],
        out_shape=[jax.ShapeDtypeStruct((T, D), F32), jax.ShapeDtypeStruct((8, D), F32)],
        scratch_shapes=[pltpu.VMEM(w.shape, BF16), pltpu.SemaphoreType.DMA],
        compiler_params=_params("arbitrary"),
    )(dgu, w, h1, g, dh2)


def _mix_out_bwd(dh1, mix, g, w, yab, yc, gg):
    T, D = dh1.shape

    def body(dh_ref, mix_ref, g_ref, w_ref, yab_ref, yc_ref, gg_ref, dmix_ref, dyab_ref, dyc_ref, dg_ref, dgg_ref):
        _acc_init(pl.program_id(0), dg_ref, dgg_ref)
        dmix, dg = _rms_bwd(dh_ref[...], mix_ref[...], g_ref[...])
        dg_ref[0:1, :] += dg
        dmix = dmix.astype(BF16)
        dmix_ref[...] = dmix
        dy = _dot_nt(dmix, w_ref[...])
        yraw = jnp.concatenate([yab_ref[...], yc_ref[...]], axis=1)
        ggv = gg_ref[...]
        parts = [_rms_bwd(dy[:, a:b], yraw[:, a:b], ggv[:, a:b]) for a, b in _group_bounds(D)]
        dgg_ref[0:1, :] += jnp.concatenate([p[1] for p in parts], axis=1)
        dyab_ref[...] = jnp.concatenate([parts[0][0], parts[1][0]], axis=1)
        dyc_ref[...] = parts[2][0]

    row = lambda n: pl.BlockSpec((TB, n), lambda i: (i, 0))
    return pl.pallas_call(
        body, name="mix_out_bwd", grid=(T // TB,),
        in_specs=[row(D), row(D), _full((1, D)), _full((D, D)), row(D // 2), row(D // 2), _full((1, D))],
        out_specs=[row(D), row(D // 2), row(D // 2), _full((8, D)), _full((8, D))],
        out_shape=[jax.ShapeDtypeStruct((T, D), BF16), jax.ShapeDtypeStruct((T, D // 2), F32), jax.ShapeDtypeStruct((T, D // 2), F32),
                   jax.ShapeDtypeStruct((8, D), F32), jax.ShapeDtypeStruct((8, D), F32)],
        compiler_params=_params("arbitrary"),
    )(dh1, mix, g, w, yab, yc, gg)


def _attn_bwd(qkv, dyc, yc, bias):
    T = qkv.shape[0]
    NP = qkv.shape[1] // (3 * LANES)
    NB = T // TB
    scale = HEAD_DIM ** -0.5

    def body(q_ref, kc_ref, vc_ref, kp_ref, vp_ref, do_ref, o_ref, b_ref, dq_ref, dk_ref, dv_ref, db_ref, dk_carry, dv_carry):
        step = pl.program_id(1)
        i = NB - 1 - step
        _acc_init(step, dk_carry, dv_carry, db_ref)
        q = q_ref[...]
        kwin = jnp.concatenate([kp_ref[...], kc_ref[...]], axis=0)
        vwin = jnp.concatenate([vp_ref[...], vc_ref[...]], axis=0)
        do = do_ref[...]
        dob = do.astype(BF16)
        prod = do * o_ref[...]
        col = lax.broadcasted_iota(jnp.int32, (1, 2 * TB), 1)
        before_start = jnp.logical_and(col < TB, i == 0)
        dq = jnp.zeros((TB, LANES), F32)
        dkw = jnp.zeros((2 * TB, LANES), F32)
        dvw = jnp.zeros((2 * TB, LANES), F32)
        for a, ma in enumerate(_head_masks()):
            qa, doa = jnp.where(ma, q, 0), jnp.where(ma, dob, 0)
            s = _dot_nt(qa, kwin) * scale + b_ref[a]
            p = _softmax_rows(jnp.where(before_start, NEG, s))
            dp = _dot_nt(doa, vwin)
            delta = jnp.sum(jnp.where(ma, prod, 0.0), axis=-1, keepdims=True)
            ds = p * (dp - delta)
            db_ref[a] += ds
            dsb = (ds * scale).astype(BF16)
            dq = dq + _dot(dsb, jnp.where(ma, kwin, 0))
            dkw = dkw + _dot_tn(dsb, qa)
            dvw = dvw + _dot_tn(p.astype(BF16), doa)
        dq_ref[...] = dq.astype(BF16)
        dk_ref[...] = (dkw[TB:] + dk_carry[...]).astype(BF16)
        dv_ref[...] = (dvw[TB:] + dv_carry[...]).astype(BF16)
        dk_carry[...] = dkw[:TB]
        dv_carry[...] = dvw[:TB]

    cur = lambda off: pl.BlockSpec((TB, LANES), lambda p, s: (NB - 1 - s, off + p))
    prev = lambda off: pl.BlockSpec((TB, LANES), lambda p, s: (jnp.maximum(NB - 2 - s, 0), off + p))
    out = jax.ShapeDtypeStruct((T, NP * LANES), BF16)
    return pl.pallas_call(
        body, name="attn_bwd", grid=(NP, NB),
        in_specs=[cur(0), cur(NP), cur(2 * NP), prev(NP), prev(2 * NP), cur(0), cur(0),
                  pl.BlockSpec((2, TB, 2 * TB), lambda p, s: (p, 0, 0))],
        out_specs=[cur(0), cur(0), cur(0), pl.BlockSpec((2, TB, 2 * TB), lambda p, s: (p, 0, 0))],
        out_shape=[out, out, out, jax.ShapeDtypeStruct((2 * NP, TB, 2 * TB), F32)],
        scratch_shapes=[pltpu.VMEM((TB, LANES), F32), pltpu.VMEM((TB, LANES), F32)],
        compiler_params=_params("arbitrary", "arbitrary"),
    )(qkv, qkv, qkv, qkv, qkv, dyc, yc, bias)


def _convpool_bwd(pa, dyab, cw, wbd, ps):
    T, D = pa.shape
    DC = D // 4
    NB = T // TB
    N = TB + HALO

    def body(pa_ref, halo_ref, dy_ref, cw_ref, wbd_ref, ps_ref, dpa_ref, dcw_ref, dwbd_ref, dps_ref, dc_carry, e_carry):
        step = pl.program_id(0)
        i = NB - 1 - step
        _acc_init(step, dcw_ref, dwbd_ref, dps_ref, dc_carry, e_carry)
        x = pa_ref[...]
        hl = jnp.where(i > 0, halo_ref[...], 0.0)
        gb, gc, u, pu = (x[:, k * DC:(k + 1) * DC] for k in range(4))
        dy = dy_ref[...]
        dya, dyb = dy[:, :DC], dy[:, DC:]
        cwv = cw_ref[...]
        z = gc * u
        z1, z2 = _conv_taps(z, hl[:, DC:2 * DC] * hl[:, 2 * DC:3 * DC])
        dgb = dya * (cwv[2:3] * z + cwv[1:2] * z1 + cwv[0:1] * z2)
        dconv = dya * gb
        dcw_ref[0:1, :] += jnp.sum(dconv * z2, axis=0, keepdims=True)
        dcw_ref[1:2, :] += jnp.sum(dconv * z1, axis=0, keepdims=True)
        dcw_ref[2:3, :] += jnp.sum(dconv * z, axis=0, keepdims=True)
        ext = jnp.concatenate([dconv, dc_carry[...]], axis=0)
        dz = cwv[2:3] * dconv + cwv[1:2] * pltpu.roll(ext, N - 1, 0)[:TB] + cwv[0:1] * pltpu.roll(ext, N - 2, 0)[:TB]
        dc_carry[...] = dconv[:HALO]
        cnt = _pool_count(i, DC)
        d = _pool_d(pu, hl[:, 3 * DC:], cnt).astype(BF16)
        psv = ps_ref[...]
        w = wbd_ref[...]
        dps_ref[0:1, :] += jnp.sum(dyb * _dot(d, w), axis=0, keepdims=True)
        dys = (dyb * psv).astype(BF16)
        dwbd_ref[...] += _dot_tn(d, dys)
        dd = _dot_nt(dys, w)
        e = dd / cnt
        ext = jnp.concatenate([e, e_carry[...]], axis=0)
        a2 = ext + pltpu.roll(ext, N - 1, 0)
        a4 = a2 + pltpu.roll(a2, N - 2, 0)
        a8 = a4 + pltpu.roll(a4, N - 4, 0)
        a16 = a8 + pltpu.roll(a8, N - 8, 0)
        dpu = _pick_group([a2, a4, a8, a16], DC)[:TB] - dd
        e_carry[...] = e[:HALO]
        dpa_ref[...] = jnp.concatenate([dgb, dz * u, dz * gc, dpu], axis=1).astype(BF16)

    blk = lambda n: pl.BlockSpec((TB, n), lambda s: (NB - 1 - s, 0))
    return pl.pallas_call(
        body, name="convpool_bwd", grid=(NB,),
        in_specs=[blk(D), pl.BlockSpec((HALO, D), lambda s: (jnp.maximum((NB - 1 - s) * (TB // HALO) - 1, 0), 0)),
                  blk(2 * DC), _full((8, DC)), _full((DC, DC)), _full((1, DC))],
        out_specs=[blk(D), _full((8, DC)), _full((DC, DC)), _full((8, DC))],
        out_shape=[jax.ShapeDtypeStruct((T, D), BF16), jax.ShapeDtypeStruct((8, DC), F32),
                   jax.ShapeDtypeStruct((DC, DC), F32), jax.ShapeDtypeStruct((8, DC), F32)],
        scratch_shapes=[pltpu.VMEM((HALO, DC), F32), pltpu.VMEM((HALO, DC), F32)],
        compiler_params=_params("arbitrary"),
    )(pa, pa, dyab, cw, wbd, ps)


def _inproj_bwd(dparts, w, h, g, dh1):
    T, D = h.shape
    widths = [p.shape[1] for p in dparts]
    n = len(dparts)

    def body(*refs):
        parts, (w_ref, h_ref, g_ref, dh1_ref, dh_ref, dg_ref) = refs[:n], refs[n:]
        _acc_init(pl.program_id(0), dg_ref)
        dxn, off = jnp.zeros((TB, D), F32), 0
        for p_ref, wd in zip(parts, widths):
            dxn = dxn + _dot_nt(p_ref[...], w_ref[:, off:off + wd])
            off += wd
        dx, dg = _rms_bwd(dxn, h_ref[...], g_ref[...])
        dg_ref[0:1, :] += dg
        dh_ref[...] = dh1_ref[...] + dx

    row = lambda m: pl.BlockSpec((TB, m), lambda i: (i, 0))
    return pl.pallas_call(
        body, name="inproj_bwd", grid=(T // TB,),
        in_specs=[row(wd) for wd in widths] + [_full(w.shape), row(D), _full((1, D)), row(D)],
        out_specs=[row(D), _full((8, D))],
        out_shape=[jax.ShapeDtypeStruct((T, D), F32), jax.ShapeDtypeStruct((8, D), F32)],
        compiler_params=_params("arbitrary"),
    )(*dparts, w, h, g, dh1)


def _wgrad(a, b, tk, tn, name):
    T, K = a.shape
    N = b.shape[1]

    def body(a_ref, b_ref, o_ref):
        _acc_init(pl.program_id(2), o_ref)
        o_ref[...] += _dot_tn(a_ref[...], b_ref[...])

    return pl.pallas_call(
        body, name=name, grid=(K // tk, N // tn, T // TB),
        in_specs=[pl.BlockSpec((TB, tk), lambda k, n, t: (t, k)), pl.BlockSpec((TB, tn), lambda k, n, t: (t, n))],
        out_specs=pl.BlockSpec((tk, tn), lambda k, n, t: (k, n)),
        out_shape=jax.ShapeDtypeStruct((K, N), F32),
        compiler_params=_params("parallel", "parallel", "arbitrary"),
    )(a, b)


def _wgrad_concat(a, bs, name):
    T, K = a.shape
    widths = [b.shape[1] for b in bs]
    n = len(bs)

    def body(*refs):
        a_ref, b_refs, o_ref = refs[0], refs[1:1 + n], refs[1 + n]
        _acc_init(pl.program_id(0), o_ref)
        av, off = a_ref[...], 0
        for b_ref, wd in zip(b_refs, widths):
            o_ref[:, off:off + wd] += _dot_tn(av, b_ref[...])
            off += wd

    row = lambda m: pl.BlockSpec((TB, m), lambda t: (t, 0))
    return pl.pallas_call(
        body, name=name, grid=(T // TB,),
        in_specs=[row(K)] + [row(wd) for wd in widths],
        out_specs=_full((K, sum(widths))),
        out_shape=jax.ShapeDtypeStruct((K, sum(widths)), F32),
        compiler_params=_params("arbitrary"),
    )(a, *bs)


def _adamw(w, g, m, v, tr, name):
    R, C = w.shape

    def body(w_ref, g_ref, m_ref, v_ref, d_ref, nm_ref, nv_ref):
        gv = g_ref[...]
        nm = ADAM_B1 * m_ref[...] + (1.0 - ADAM_B1) * gv
        nv = ADAM_B2 * v_ref[...] + (1.0 - ADAM_B2) * (gv * gv)
        m_hat = nm / (1.0 - ADAM_B1 ** ADAM_STEP)
        v_hat = nv / (1.0 - ADAM_B2 ** ADAM_STEP)
        d_ref[...] = -ADAM_LR * (m_hat / (jnp.sqrt(v_hat) + ADAM_EPS) + ADAM_WD * w_ref[...])
        nm_ref[...] = nm
        nv_ref[...] = nv

    blk = pl.BlockSpec((tr, C), lambda i: (i, 0))
    out = jax.ShapeDtypeStruct((R, C), F32)
    return pl.pallas_call(
        body, name=name, grid=(R // tr,), in_specs=[blk] * 4, out_specs=[blk] * 3, out_shape=[out] * 3,
        compiler_params=_params("parallel"),
    )(w, g, m, v)


def _sum_slots(x, tr, name):
    n, R, C = x.shape

    def body(x_ref, o_ref):
        acc = x_ref[0]
        for k in range(1, n):
            acc = acc + x_ref[k]
        o_ref[...] = acc

    return pl.pallas_call(
        body, name=name, grid=(R // tr,),
        in_specs=[pl.BlockSpec((n, tr, C), lambda i: (0, i, 0))],
        out_specs=pl.BlockSpec((tr, C), lambda i: (i, 0)),
        out_shape=jax.ShapeDtypeStruct((R, C), F32),
        compiler_params=_params("parallel"),
    )(x)


def _add_half(view, other, c, name):
    if view.ndim == 3:
        _, R2, N = view.shape
        tr = 128
        grid = (R2 // tr,)
        in_specs = [pl.BlockSpec((1, tr, N), lambda i, c_ref: (c_ref[0], i, 0)), pl.BlockSpec((1, tr, N), lambda i, c_ref: (0, i, 0))]
        out_spec = pl.BlockSpec((tr, N), lambda i, c_ref: (i, 0))
        out_shape = jax.ShapeDtypeStruct((R2, N), F32)

        def body(c_ref, a_ref, b_ref, o_ref):
            o_ref[...] = a_ref[0] + b_ref[0]
    else:
        S, _, R2, C = view.shape
        grid = (S,)
        in_specs = [pl.BlockSpec((1, 1, R2, C), lambda s, c_ref: (s, c_ref[0], 0, 0)), pl.BlockSpec((1, 1, R2, C), lambda s, c_ref: (s, 0, 0, 0))]
        out_spec = pl.BlockSpec((1, R2, C), lambda s, c_ref: (s, 0, 0))
        out_shape = jax.ShapeDtypeStruct((S, R2, C), F32)

        def body(c_ref, a_ref, b_ref, o_ref):
            o_ref[0] = a_ref[0, 0] + b_ref[0, 0]

    return pl.pallas_call(
        body, name=name,
        grid_spec=pltpu.PrefetchScalarGridSpec(num_scalar_prefetch=1, grid=grid, in_specs=in_specs, out_specs=out_spec),
        out_shape=out_shape, compiler_params=_params("parallel"),
    )(c, view, other)


def _sum_chips(slots, part, kind, place, name):
    _, R2, C = slots.shape
    tr = min(R2, 128) if kind == "col" else R2

    def body(p_ref, s_ref, own_ref, o_ref):
        me = p_ref[0]
        own = own_ref[...] if kind == "col" else own_ref[0]
        acc = None
        for k in range(4):
            term = jnp.where(me == k, own, s_ref[k])
            acc = term if acc is None else acc + term
        o_ref[0] = acc

    own_spec = (pl.BlockSpec((tr, C), lambda i, p: (i, p[0])) if kind == "col"
                else pl.BlockSpec((1, R2, C), lambda i, p: (p[0], 0, 0)))
    return pl.pallas_call(
        body, name=name,
        grid_spec=pltpu.PrefetchScalarGridSpec(
            num_scalar_prefetch=1, grid=(R2 // tr,),
            in_specs=[pl.BlockSpec((4, tr, C), lambda i, p: (0, i, 0)), own_spec],
            out_specs=pl.BlockSpec((1, tr, C), lambda i, p: (p[1], i, 0))),
        out_shape=jax.ShapeDtypeStruct((2, R2, C), F32), compiler_params=_params("parallel"),
    )(place, slots, part)


def _place():
    x, y, c = lax.axis_index("x"), lax.axis_index("y"), lax.axis_index("c")
    chips = [(1 - x, y), (x, 1 - y), (1 - x, 1 - y)]
    return x, y, c, 2 * x + y, chips


def _remote(src, dst, send_sems, recv_sems, k, to):
    return pltpu.make_async_remote_copy(src_ref=src, dst_ref=dst, send_sem=send_sems.at[k], recv_sem=recv_sems.at[k],
                                        device_id=to, device_id_type=MESH)


def _full_shape(shard_shape, kind):
    *lead, R, C = shard_shape
    return (*lead, R, 4 * C) if kind == "col" else (*lead, 4 * R, C)


def _slab(full_ref, shard_rc, kind, chip, half=None):
    R, C = shard_rc
    lead = (slice(None),) * (len(full_ref.shape) - 2)
    if kind == "col":
        rows = pl.ds(0, R) if half is None else pl.ds(half * (R // 2), R // 2)
        return full_ref.at[(*lead, rows, pl.ds(chip * C, C))]
    rows = pl.ds(chip * R, R) if half is None else pl.ds(chip * R + half * (R // 2), R // 2)
    return full_ref.at[(*lead, rows, slice(None))]


def _row_half(ref, half):
    R = ref.shape[-2]
    lead = (slice(None),) * (len(ref.shape) - 2)
    return ref.at[(*lead, pl.ds(half * (R // 2), R // 2), slice(None))]


def _allgather_weights(shards, kinds):
    n = len(shards)
    full_shapes = [_full_shape(s.shape, k) for s, k in zip(shards, kinds)]

    def body(*refs):
        ins, outs, (send_sems, recv_sems) = refs[:n], refs[n:2 * n], refs[2 * n:]
        x, y, c, me, chips = _place()
        slab = lambda a, chip, half=None: _slab(outs[a], shards[a].shape[-2:], kinds[a], chip, half)
        my_half = lambda a: _row_half(ins[a], c)
        own = [_remote(ins[a], slab(a, me), send_sems, recv_sems, 6 * n + a, (x, y, 1 - c)) for a in range(n)]
        first = [_remote(my_half(a), slab(a, me, c), send_sems, recv_sems, j * n + a, (*chip, c))
                 for j, chip in enumerate(chips) for a in range(n)]
        for cp in first + own:
            cp.start()
        passed = []
        for j, chip in enumerate(chips):
            src = 2 * chip[0] + chip[1]
            for a in range(n):
                _remote(my_half(a), slab(a, src, c), send_sems, recv_sems, j * n + a, (x, y, c)).wait_recv()
                cp = _remote(slab(a, src, c), slab(a, src, c), send_sems, recv_sems, (3 + j) * n + a, (x, y, 1 - c))
                cp.start()
                passed.append(cp)
        for j, chip in enumerate(chips):
            src = 2 * chip[0] + chip[1]
            for a in range(n):
                _remote(my_half(a), slab(a, src, 1 - c), send_sems, recv_sems, (3 + j) * n + a, (x, y, c)).wait_recv()
        for cp in own:
            cp.wait_recv()
        for cp in first + passed + own:
            cp.wait_send()

    return pl.pallas_call(
        body, name="allgather_weights",
        in_specs=[ANY] * n, out_specs=[ANY] * n,
        out_shape=[jax.ShapeDtypeStruct(s, BF16) for s in full_shapes],
        scratch_shapes=[pltpu.SemaphoreType.DMA((7 * n,)), pltpu.SemaphoreType.DMA((7 * n,))],
        compiler_params=pltpu.CompilerParams(has_side_effects=True),
    )(*shards)


HBM = pl.BlockSpec(memory_space=pltpu.HBM)
SEM = pl.BlockSpec(memory_space=pltpu.SEMAPHORE)
DATAFLOW = pltpu.SideEffectType.DATAFLOW_SIDE_EFFECTING


def _split_start(name, bufs, plan, ncopies, after=None):
    nb = len(bufs)
    nin = nb + (after is not None)

    def body(*refs):
        send_sems, recv_sems, token = refs[nin], refs[nin + 1], refs[-1]
        for k, (src, dst, to) in enumerate(plan(refs[:nb])):
            _remote(src, dst, send_sems, recv_sems, k, to).start()
        token[...] = jnp.zeros_like(token)

    out = pl.pallas_call(
        body, name=name,
        out_shape=(pltpu.SemaphoreType.DMA((ncopies,)), pltpu.SemaphoreType.DMA((ncopies,)),
                   *[pltpu.HBM(b.shape, b.dtype) for b in bufs], jax.ShapeDtypeStruct((8, LANES), F32)),
        in_specs=[HBM] * nb + [ANY] * (nin - nb), out_specs=(SEM, SEM, *[HBM] * nb, pl.BlockSpec(memory_space=pltpu.VMEM)),
        input_output_aliases={i: 2 + i for i in range(nb)},
        compiler_params=pltpu.CompilerParams(has_side_effects=DATAFLOW),
    )(*[pltpu.with_memory_space_constraint(b, pltpu.HBM) for b in bufs], *([] if after is None else [after]))
    return out[0], out[1], list(out[2:2 + nb]), out[-1]


def _split_wait(name, send_sems, recv_sems, bufs, plan, after):
    nb = len(bufs)

    def body(*refs):
        s_sems, r_sems = refs[nb], refs[nb + 1]
        for k, (src, dst, to) in enumerate(plan(refs[:nb])):
            cp = _remote(src, dst, s_sems, r_sems, k, to)
            cp.wait_send()
            cp.wait_recv()

    return pl.pallas_call(
        body, name=name,
        out_shape=tuple(pltpu.HBM(b.shape, b.dtype) for b in bufs),
        in_specs=[HBM] * nb + [SEM, SEM, ANY], out_specs=tuple([HBM] * nb),
        input_output_aliases={i: i for i in range(nb)},
        compiler_params=pltpu.CompilerParams(has_side_effects=DATAFLOW),
    )(*bufs, send_sems, recv_sems, after)


def _gather_plan(n, shard_rcs, kinds):
    def plan(refs):
        x, y, c, me, chips = _place()
        out = []
        for a in range(n):
            shard, full = refs[a], refs[n + a]
            out.append((shard, _slab(full, shard_rcs[a], kinds[a], me), (x, y, 1 - c)))
            for chip in chips:
                for cc in (c, 1 - c):
                    out.append((_row_half(shard, c), _slab(full, shard_rcs[a], kinds[a], me, c), (*chip, cc)))
        return out
    return plan


def _exchange_plan(n, kinds):
    def plan(refs):
        x, y, c, me, chips = _place()
        out = []
        for a in range(n):
            part, slots = refs[a], refs[n + a]
            C = slots.shape[2]
            for chip in chips:
                dst = 2 * chip[0] + chip[1]
                src = part.at[:, pl.ds(dst * C, C)] if kinds[a] == "col" else part.at[dst]
                out.append((src, slots.at[me], (*chip, c)))
        return out
    return plan


def _swap_halves(views):
    n = len(views)
    out_shapes = [(1,) + v.shape[1:] if v.ndim == 3 else (v.shape[0], 1) + v.shape[2:] for v in views]

    def body(*refs):
        ins, outs, (send_sems, recv_sems) = refs[:n], refs[n:2 * n], refs[2 * n:]
        x, y, c, _, _ = _place()
        cps = []
        for a in range(n):
            src = ins[a].at[pl.ds(1 - c, 1)] if views[a].ndim == 3 else ins[a].at[:, pl.ds(1 - c, 1)]
            cps.append(_remote(src, outs[a], send_sems, recv_sems, a, (x, y, 1 - c)))
        for cp in cps:
            cp.start()
        for cp in cps:
            cp.wait()

    return pl.pallas_call(
        body, name="grad_swap_halves", in_specs=[ANY] * n, out_specs=[ANY] * n,
        out_shape=[jax.ShapeDtypeStruct(s, F32) for s in out_shapes],
        scratch_shapes=[pltpu.SemaphoreType.DMA((n,)), pltpu.SemaphoreType.DMA((n,))],
        compiler_params=pltpu.CompilerParams(has_side_effects=True),
    )(*views)


def _join_halves(bufs):
    n = len(bufs)

    def body(*refs):
        outs, (send_sems, recv_sems) = refs[n:2 * n], refs[2 * n:]
        x, y, c, _, _ = _place()
        sends = [_remote(outs[a].at[c], outs[a].at[c], send_sems, recv_sems, a, (x, y, 1 - c)) for a in range(n)]
        for cp in sends:
            cp.start()
        for a in range(n):
            _remote(outs[a].at[c], outs[a].at[1 - c], send_sems, recv_sems, a, (x, y, c)).wait_recv()
        for cp in sends:
            cp.wait_send()

    return pl.pallas_call(
        body, name="grad_join_halves", in_specs=[ANY] * n, out_specs=[ANY] * n,
        out_shape=[jax.ShapeDtypeStruct(b.shape, F32) for b in bufs],
        input_output_aliases={a: a for a in range(n)},
        scratch_shapes=[pltpu.SemaphoreType.DMA((n,)), pltpu.SemaphoreType.DMA((n,))],
        compiler_params=pltpu.CompilerParams(has_side_effects=True),
    )(*bufs)


def _allgather_small(block):
    M, N = block.shape

    def body(x_ref, out_ref, send_sems, recv_sems, local_sem):
        x, y, c, _, chips = _place()
        me, sibling = (x, y, c), (x, y, 1 - c)

        def rows(px, py, pc):
            return out_ref.at[pl.ds((4 * px + 2 * py + pc) * M, M), :]

        def copy(k, blk, to, src=None):
            return _remote(rows(*blk) if src is None else src, rows(*blk), send_sems, recv_sems, k, to)

        mine = pltpu.make_async_copy(x_ref, rows(*me), local_sem)
        mine.start()
        first = [copy(0, me, sibling, src=x_ref)] + [copy(1 + j, me, (*chip, c), src=x_ref) for j, chip in enumerate(chips)]
        for cp in first:
            cp.start()
        passed = [copy(4 + j, (*chip, c), sibling) for j, chip in enumerate(chips)]
        for j, chip in enumerate(chips):
            copy(1 + j, (*chip, c), me).wait_recv()
            passed[j].start()
        copy(0, sibling, me).wait_recv()
        for j, chip in enumerate(chips):
            copy(4 + j, (*chip, 1 - c), me).wait_recv()
        for cp in first + passed:
            cp.wait_send()
        mine.wait()

    vm = pl.BlockSpec(memory_space=pltpu.VMEM)
    return pl.pallas_call(
        body, name="allgather_small", in_specs=[vm], out_specs=vm,
        out_shape=jax.ShapeDtypeStruct((8 * M, N), F32),
        scratch_shapes=[pltpu.SemaphoreType.DMA((7,)), pltpu.SemaphoreType.DMA((7,)), pltpu.SemaphoreType.DMA],
        compiler_params=pltpu.CompilerParams(has_side_effects=True, vmem_limit_bytes=VMEM_LIMIT),
    )(block)


def _pack(arrays):
    flat = jnp.concatenate([a.reshape(-1) for a in arrays])
    pad = (-flat.shape[0]) % (8 * LANES)
    return jnp.pad(flat, (0, pad)).reshape(-1, LANES)


def _unpack(packed, shapes):
    flat, out, off = packed.reshape(-1), [], 0
    for s in shapes:
        size = 1
        for d in s:
            size *= d
        out.append(flat[off:off + size].reshape(s))
        off += size
    return out


def _block_diag(pw):
    G, n, _ = pw.shape
    eye = jnp.eye(G, dtype=pw.dtype)
    return (eye[:, None, :, None] * pw[:, :, None, :]).reshape(G * n, G * n)


def _diag_blocks(m, G):
    n = m.shape[0] // G
    return jnp.stack([m[g * n:(g + 1) * n, g * n:(g + 1) * n] for g in range(G)])


def _pad_rows(a, rows):
    return jnp.pad(a, ((0, rows - a.shape[0]), (0, 0)))


def kernel(x, w_in, w_out, conv_w, pool_w, pool_scale, rel_bias, group_gain, pre_mix_g, post_mix_g, pre_ffn_g, post_ffn_g, w_gate_up, w_down, loss_target, m_w_in, m_w_out, m_conv_w, m_pool_w, m_pool_scale, m_rel_bias, m_group_gain, m_pre_mix_g, m_post_mix_g, m_pre_ffn_g, m_post_ffn_g, m_w_gate_up, m_w_down, v_w_in, v_w_out, v_conv_w, v_pool_w, v_pool_scale, v_rel_bias, v_group_gain, v_pre_mix_g, v_post_mix_g, v_pre_ffn_g, v_post_ffn_g, v_w_gate_up, v_w_down):
    L = w_in.shape[0]
    T, D = x.shape[1], x.shape[2]
    DC = D // 4
    NH = rel_bias.shape[1]
    NREL = rel_bias.shape[2]
    G = pool_w.shape[1]
    cs = conv_w.shape[2]
    assert TB == LEFT_CHUNKS * CHUNK and T % TB == 0 and D % (4 * LANES) == 0 and NH * HEAD_DIM == D // 2
    xi, yi, ci = lax.axis_index("x"), lax.axis_index("y"), lax.axis_index("c")
    chip = 2 * xi + yi

    kinds = ("col", "row", "col", "row")
    big = (w_in, w_out, w_gate_up, w_down)
    rcs = [w.shape[-2:] for w in big]
    gplan = _gather_plan(4, rcs, kinds)
    ncopy_gather = 4 * 7
    full = [[f[0] for f in _allgather_weights([w[0:1].astype(BF16) for w in big], kinds)]]
    conv_all = _allgather_small(_pack([conv_w])).reshape(8, -1)[:, :L * 3 * cs].reshape(4, 2, L, 3, cs)[:, 0]
    conv_full = jnp.moveaxis(conv_all, 0, 2).reshape(L, 3, 4 * cs)

    h = x[0]
    saved = []
    for l in range(L):
        g_pre, g_pm, g_pf, g_po = (a[l][None] for a in (pre_mix_g, post_mix_g, pre_ffn_g, post_ffn_g))
        if l + 1 < L:
            bufs = [w[l + 1].astype(BF16) for w in big] + [lax.empty(_full_shape(rc, k), BF16) for rc, k in zip(rcs, kinds)]
            ssem, rsem, thru, token = _split_start(f"gather_start_{l + 1}", bufs, gplan, ncopy_gather, after=full[l][0])
            g_pre = g_pre + token[0:1, 0:1]
        gg, ps = group_gain[l][None], pool_scale[l][None]
        cw = _pad_rows(conv_full[l], 8)
        wbd = _block_diag(pool_w[l]).astype(BF16)
        bias = _bias_build(jnp.pad(rel_bias[l], ((0, 0), (0, RBP - NREL)))[:, None, :])
        wi, wo, wgu, wdn = full[l]
        xn, pa, qkv = _inproj_fwd(h, g_pre, wi)
        yab = _convpool_fwd(pa, cw, wbd, ps)
        yc = _attn_fwd(qkv, bias)
        y, mix, h1 = _mix_out_fwd(yab, yc, gg, wo, h, g_pm)
        hn, gu, ff = _ffn_up_fwd(h1, g_pf, wgu)
        ffo, h2 = _ffn_down_fwd(ff, wdn, h1, g_po)
        saved.append(dict(h=h, xn=xn, pa=pa, qkv=qkv, yab=yab, yc=yc, y=y, mix=mix, h1=h1, hn=hn, gu=gu, ff=ff, ffo=ffo,
                          cw=cw, wbd=wbd, bias=bias, ps=ps, gg=gg, g_pre=g_pre, g_pm=g_pm, g_pf=g_pf, g_po=g_po))
        h = h2
        if l + 1 < L:
            full.append(list(_split_wait(f"gather_wait_{l + 1}", ssem, rsem, thru, gplan, h2)[4:8]))

    dh, loss_tile = _loss_grad(h, loss_target[0])
    loss = lax.psum(loss_tile[0, 0], ("x", "y", "c"))

    xplan = _exchange_plan(4, kinds)
    place = jnp.stack([chip, ci]).astype(jnp.int32)
    cvec = ci.reshape(1).astype(jnp.int32)
    small_grads = [None] * L
    shard_grads = [None] * L

    def finish_exchange(pending, after):
        lp, ssem, rsem, thru = pending
        landed = _split_wait(f"grad_exchange_wait_{lp}", ssem, rsem, thru, xplan, after)
        bufs = [_sum_chips(landed[4 + a], landed[a], kinds[a], place, "grad_sum_chips") for a in range(4)]
        shard_grads[lp] = [j.reshape(2 * j.shape[1], j.shape[2]) for j in _join_halves(bufs)]

    pending = None
    token = None
    for l in reversed(range(L)):
        s = saved[l]
        wi, wo, wgu, wdn = full[l]
        g_po = s["g_po"] if token is None else s["g_po"] + token[0:1, 0:1]
        dffo, dgu, dg_po = _ffn_down_bwd(dh, s["ffo"], g_po, s["gu"], wdn)
        dh1, dg_pf = _ffn_up_bwd(dgu, wgu, s["h1"], s["g_pf"], dh)
        dmix, dyab, dyc, dg_pm, dgg = _mix_out_bwd(dh1, s["mix"], s["g_pm"], wo, s["yab"], s["yc"], s["gg"])
        dq, dk, dv, dbias = _attn_bwd(s["qkv"], dyc, s["yc"], s["bias"])
        dpa, dcw, dwbd, dps = _convpool_bwd(s["pa"], dyab, s["cw"], s["wbd"], s["ps"])
        dparts = [dpa, dq, dk, dv]
        dh, dg_pre = _inproj_bwd(dparts, wi, s["h"], s["g_pre"], dh1)
        F2, DFF = s["gu"].shape[1], s["ff"].shape[1]
        grads = [_wgrad_concat(s["xn"], dparts, "wgrad_in"),
                 _wgrad(s["y"], dmix, D, D, "wgrad_out"),
                 _wgrad(s["hn"], dgu, D, F2 // 4, "wgrad_gate_up"),
                 _wgrad(s["ff"], dffo, DFF // 2, D, "wgrad_down")]
        drb = _bias_fold(dbias)[:, 0, :NREL]
        small_grads[l] = [dcw[:3], _diag_blocks(dwbd, G), dps[0], drb, dgg[0], dg_pre[0], dg_pm[0], dg_pf[0], dg_po[0]]
        views = [g.reshape(2, g.shape[0] // 2, g.shape[1]) if k == "col" else g.reshape(4, 2, g.shape[0] // 8, g.shape[1])
                 for g, k in zip(grads, kinds)]
        theirs = _swap_halves(views)
        chip_sums = [_add_half(v, t, cvec, "grad_add_half") for v, t in zip(views, theirs)]
        slots = [lax.empty((4, p.shape[0], p.shape[1] // 4) if k == "col" else p.shape, F32) for p, k in zip(chip_sums, kinds)]
        ssem, rsem, thru, token = _split_start(f"grad_exchange_start_{l}", chip_sums + slots, xplan, 4 * 3)
        if pending is not None:
            finish_exchange(pending, dh)
        pending = (l, ssem, rsem, thru)
    finish_exchange(pending, dh)
    g_big = [jnp.stack([shard_grads[l][k] for l in range(L)]) for k in range(4)]

    names_shapes = [(L, 3, 4 * cs), pool_w.shape, pool_scale.shape, rel_bias.shape, group_gain.shape,
                    pre_mix_g.shape, post_mix_g.shape, pre_ffn_g.shape, post_ffn_g.shape]
    small_stacked = [jnp.stack([small_grads[l][k] for l in range(L)]) for k in range(len(names_shapes))]
    packed = _pack(small_stacked)
    M = packed.shape[0]
    total = _sum_slots(_allgather_small(packed).reshape(8, M, LANES), M, "small_sum_devices")
    g_small = _unpack(total, names_shapes)
    g_small[0] = lax.dynamic_slice_in_dim(g_small[0], chip * cs, cs, axis=2)

    def adam_big(w, g, m, v, name):
        shp = w.shape
        two = lambda a: a.reshape(shp[0] * shp[1], shp[2])
        return [o.reshape(shp) for o in _adamw(two(w), two(g), two(m), two(v), 256, name)]

    upd_in = adam_big(w_in, g_big[0], m_w_in, v_w_in, "adamw_in")
    upd_out = adam_big(w_out, g_big[1], m_w_out, v_w_out, "adamw_out")
    upd_gu = adam_big(w_gate_up, g_big[2], m_w_gate_up, v_w_gate_up, "adamw_gate_up")
    upd_dn = adam_big(w_down, g_big[3], m_w_down, v_w_down, "adamw_down")

    small_w = [conv_w, pool_w, pool_scale, rel_bias, group_gain, pre_mix_g, post_mix_g, pre_ffn_g, post_ffn_g]
    small_m = [m_conv_w, m_pool_w, m_pool_scale, m_rel_bias, m_group_gain, m_pre_mix_g, m_post_mix_g, m_pre_ffn_g, m_post_ffn_g]
    small_v = [v_conv_w, v_pool_w, v_pool_scale, v_rel_bias, v_group_gain, v_pre_mix_g, v_post_mix_g, v_pre_ffn_g, v_post_ffn_g]
    pw_, pg_, pm_, pv_ = _pack(small_w), _pack(g_small), _pack(small_m), _pack(small_v)
    shapes = [w.shape for w in small_w]
    upd_small = [_unpack(o, shapes) for o in _adamw(pw_, pg_, pm_, pv_, pw_.shape[0], "adamw_small")]

    def ordered(big4, small9):
        return [big4[0], big4[1], *small9, big4[2], big4[3]]

    grads = ordered(g_big, g_small)
    outs = [ordered([upd_in[k], upd_out[k], upd_gu[k], upd_dn[k]], upd_small[k]) for k in range(3)]
    return (loss, dh[None], *grads, *outs[0], *outs[1], *outs[2])
```

```python
import functools

import jax
import jax.numpy as jnp
from jax import lax
from jax.experimental import pallas as pl
from jax.experimental.pallas import tpu as pltpu

F32, BF16 = jnp.float32, jnp.bfloat16
EPS = 1e-6
CHUNK = 64
LEFT_CHUNKS = 8
REL_CLIP = 128
HEAD_DIM = 64
LANES = 128
POOL_WINDOWS = (2, 4, 8, 16)
HALO = 16
TB = LEFT_CHUNKS * CHUNK
TBF = 256
BAND = (LEFT_CHUNKS + 1) * CHUNK
SUB = 2 * CHUNK
BANDW = SUB + LEFT_CHUNKS * CHUNK
SKEW = 768
NEG = -1e30
RBP = 384
VMEM_LIMIT = 56 * 1024 * 1024
ADAM_LR, ADAM_B1, ADAM_B2, ADAM_EPS, ADAM_WD, ADAM_STEP = 0.001, 0.9, 0.999, 1e-08, 0.01, 10
MESH = pl.DeviceIdType.MESH
ANY = pl.BlockSpec(memory_space=pl.ANY)


def _params(*sem):
    kw = dict(vmem_limit_bytes=VMEM_LIMIT)
    if sem:
        kw["dimension_semantics"] = sem
    return pltpu.CompilerParams(**kw)


def _dot(a, b):
    return jnp.dot(a, b, preferred_element_type=F32)


def _dot_nt(a, b):
    return lax.dot_general(a, b, (((1,), (1,)), ((), ())), preferred_element_type=F32)


def _dot_tn(a, b):
    return lax.dot_general(a, b, (((0,), (0,)), ((), ())), preferred_element_type=F32)


def _rms(x, g):
    r = lax.rsqrt(jnp.mean(x * x, axis=-1, keepdims=True) + EPS)
    return x * r * g


def _rms_bwd(dy, x, g):
    r = lax.rsqrt(jnp.mean(x * x, axis=-1, keepdims=True) + EPS)
    xh = x * r
    dxh = dy * g
    dx = r * (dxh - xh * jnp.mean(dxh * xh, axis=-1, keepdims=True))
    return dx, jnp.sum(dy * xh, axis=0, keepdims=True)


def _full(shape):
    return pl.BlockSpec(shape, lambda *_: (0,) * len(shape))


def _acc_init(step, *refs):
    @pl.when(step == 0)
    def _():
        for r in refs:
            r[...] = jnp.zeros_like(r)


def _inproj_fwd(h, g, w):
    T, D = h.shape
    NQ = w.shape[1] - D

    def body(h_ref, g_ref, w_ref, xn_ref, pa_ref, qkv_ref):
        xn = _rms(h_ref[...], g_ref[...]).astype(BF16)
        xn_ref[...] = xn
        pa_ref[...] = _dot(xn, w_ref[:, :D])
        qkv_ref[...] = _dot(xn, w_ref[:, D:]).astype(BF16)

    row = lambda n: pl.BlockSpec((TB, n), lambda i: (i, 0))
    return pl.pallas_call(
        body, name="inproj_fwd", grid=(T // TB,),
        in_specs=[row(D), _full((1, D)), _full(w.shape)],
        out_specs=[row(D), row(D), row(NQ)],
        out_shape=[jax.ShapeDtypeStruct((T, D), BF16), jax.ShapeDtypeStruct((T, D), F32),
                   jax.ShapeDtypeStruct((T, NQ), BF16)],
        compiler_params=_params("parallel"),
    )(h, g, w)


def _lane_groups(n, vals):
    lane = lax.broadcasted_iota(jnp.int32, (1, n), 1)
    q = n // 4
    return jnp.where(lane < q, vals[0], jnp.where(lane < 2 * q, vals[1], jnp.where(lane < 3 * q, vals[2], vals[3]))).astype(F32)


def _pick_group(levels, n):
    lane = lax.broadcasted_iota(jnp.int32, (1, n), 1)
    q = n // 4
    return jnp.where(lane < q, levels[0], jnp.where(lane < 2 * q, levels[1], jnp.where(lane < 3 * q, levels[2], levels[3])))


def _pool_count(blk, n):
    t1 = (blk * TB + 1 + lax.broadcasted_iota(jnp.int32, (TB, 1), 0)).astype(F32)
    return jnp.minimum(t1, _lane_groups(n, POOL_WINDOWS))


def _pool_d(pu, pu_halo, cnt):
    e = jnp.concatenate([pu_halo, pu], axis=0)
    s2 = e + pltpu.roll(e, 1, 0)
    s4 = s2 + pltpu.roll(s2, 2, 0)
    s8 = s4 + pltpu.roll(s4, 4, 0)
    s16 = s8 + pltpu.roll(s8, 8, 0)
    num = _pick_group([s2, s4, s8, s16], pu.shape[1])[HALO:]
    return num / cnt - pu


def _conv_taps(z, z_halo):
    e = jnp.concatenate([z_halo, z], axis=0)
    return pltpu.roll(e, 1, 0)[HALO:], pltpu.roll(e, 2, 0)[HALO:]


def _convpool_fwd(pa, cw, wbd, ps):
    T, D = pa.shape
    DC = D // 4

    def body(pa_ref, halo_ref, cw_ref, wbd_ref, ps_ref, yab_ref):
        i = pl.program_id(0)
        x = pa_ref[...]
        hl = jnp.where(i > 0, halo_ref[...], 0.0)
        gb, gc, u, pu = (x[:, k * DC:(k + 1) * DC] for k in range(4))
        z = gc * u
        z1, z2 = _conv_taps(z, hl[:, DC:2 * DC] * hl[:, 2 * DC:3 * DC])
        cwv = cw_ref[...]
        ya = gb * (cwv[2:3] * z + cwv[1:2] * z1 + cwv[0:1] * z2)
        d = _pool_d(pu, hl[:, 3 * DC:], _pool_count(i, DC))
        yb = _dot(d.astype(BF16), wbd_ref[...]) * ps_ref[...]
        yab_ref[...] = jnp.concatenate([ya, yb], axis=1)

    return pl.pallas_call(
        body, name="convpool_fwd", grid=(T // TB,),
        in_specs=[pl.BlockSpec((TB, D), lambda i: (i, 0)),
                  pl.BlockSpec((HALO, D), lambda i: (jnp.maximum(i * (TB // HALO) - 1, 0), 0)),
                  _full((8, DC)), _full((DC, DC)), _full((1, DC))],
        out_specs=pl.BlockSpec((TB, 2 * DC), lambda i: (i, 0)),
        out_shape=jax.ShapeDtypeStruct((T, 2 * DC), F32),
        compiler_params=_params("parallel"),
    )(pa, pa, cw, wbd, ps)


def _bias_bins(shape, col_dim):
    j = lax.broadcasted_iota(jnp.int32, shape, col_dim)
    b = lax.broadcasted_iota(jnp.int32, shape, 1 - col_dim)
    d = jnp.where(j < BAND, j, j - SKEW)
    live = jnp.logical_or(j < BAND, j > SKEW - CHUNK)
    bins = jnp.minimum(TB - d, REL_CLIP) + REL_CLIP
    return jnp.where(jnp.logical_and(live, bins == b), 1.0, 0.0).astype(F32)


def _bias_build(rb):
    H = rb.shape[0]

    def body(rb_ref, o_ref):
        v = jnp.dot(jnp.broadcast_to(rb_ref[0], (8, RBP)), _bias_bins((RBP, SKEW), 1),
                    precision=lax.Precision.HIGHEST, preferred_element_type=F32)
        x = jnp.broadcast_to(v[0:1], (SUB, SKEW))
        row = lax.broadcasted_iota(jnp.int32, (SUB, SKEW), 0)
        for b in range(SUB.bit_length() - 1):
            x = jnp.where(((row >> b) & 1) == 1, pltpu.roll(x, 1 << b, 1), x)
        qc = lax.broadcasted_iota(jnp.int32, (SUB, BANDW), 0) >> (CHUNK.bit_length() - 1)
        kc = lax.broadcasted_iota(jnp.int32, (SUB, BANDW), 1) >> (CHUNK.bit_length() - 1)
        o_ref[0] = jnp.where(jnp.logical_and(kc >= qc, kc <= qc + LEFT_CHUNKS), x[:, :BANDW], NEG)

    return pl.pallas_call(
        body, name="bias_build", grid=(H,),
        in_specs=[pl.BlockSpec((1, 1, RBP), lambda h: (h, 0, 0))],
        out_specs=pl.BlockSpec((1, SUB, BANDW), lambda h: (h, 0, 0)),
        out_shape=jax.ShapeDtypeStruct((H, SUB, BANDW), F32),
        compiler_params=_params("parallel"),
    )(rb)


def _bias_fold(db):
    H = db.shape[0]

    def body(db_ref, o_ref):
        x = jnp.concatenate([db_ref[0], jnp.zeros((SUB, SKEW - BANDW), F32)], axis=1)
        row = lax.broadcasted_iota(jnp.int32, (SUB, SKEW), 0)
        for b in range(SUB.bit_length() - 1):
            x = jnp.where(((row >> b) & 1) == 1, pltpu.roll(x, SKEW - (1 << b), 1), x)
        dsum = jnp.sum(x, axis=0, keepdims=True)
        o_ref[0] = jnp.dot(jnp.broadcast_to(dsum, (8, SKEW)), _bias_bins((SKEW, RBP), 0),
                           precision=lax.Precision.HIGHEST, preferred_element_type=F32)

    return pl.pallas_call(
        body, name="bias_fold", grid=(H,),
        in_specs=[pl.BlockSpec((1, SUB, BANDW), lambda h: (h, 0, 0))],
        out_specs=pl.BlockSpec((1, 8, RBP), lambda h: (h, 0, 0)),
        out_shape=jax.ShapeDtypeStruct((H, 8, RBP), F32),
        compiler_params=_params("parallel"),
    )(db)


def _head_masks():
    lane = lax.broadcasted_iota(jnp.int32, (1, LANES), 1)
    return lane < HEAD_DIM, lane >= HEAD_DIM


def _softmax_rows(s):
    e = jnp.exp(s - jnp.max(s, axis=-1, keepdims=True))
    return e * (1.0 / jnp.sum(e, axis=-1, keepdims=True))


def _attn_fwd(qkv, bias):
    T = qkv.shape[0]
    NP = qkv.shape[1] // (3 * LANES)
    NB = T // TB
    scale = HEAD_DIM ** -0.5

    def body(q_ref, k_ref, v_ref, b_ref, o_ref, kprev, vprev):
        i = pl.program_id(1)

        @pl.when(i == 0)
        def _():
            kprev[...] = jnp.zeros_like(kprev)
            vprev[...] = jnp.zeros_like(vprev)

        kc, vc = k_ref[...], v_ref[...]

        def compute(first_block):
            q = q_ref[...] * scale
            kwin = jnp.concatenate([kprev[...], kc], axis=0)
            vwin = jnp.concatenate([vprev[...], vc], axis=0)
            col = lax.broadcasted_iota(jnp.int32, (1, BANDW), 1)
            for s in range(TB // SUB):
                rows, band = slice(s * SUB, (s + 1) * SUB), slice(s * SUB, s * SUB + BANDW)
                acc = jnp.zeros((SUB, LANES), F32)
                for a, ma in enumerate(_head_masks()):
                    sc = _dot_nt(jnp.where(ma, q[rows], 0), kwin[band]) + b_ref[a]
                    if first_block:
                        sc = jnp.where(col < TB - s * SUB, NEG, sc)
                    acc = acc + _dot(_softmax_rows(sc).astype(BF16), jnp.where(ma, vwin[band], 0))
                o_ref[rows, :] = acc

        pl.when(i == 0)(functools.partial(compute, True))
        pl.when(i > 0)(functools.partial(compute, False))
        kprev[...] = kc
        vprev[...] = vc

    col = lambda off: pl.BlockSpec((TB, LANES), lambda p, i: (i, off + p))
    return pl.pallas_call(
        body, name="attn_fwd", grid=(NP, NB),
        in_specs=[col(0), col(NP), col(2 * NP), pl.BlockSpec((2, SUB, BANDW), lambda p, i: (p, 0, 0))],
        out_specs=col(0),
        out_shape=jax.ShapeDtypeStruct((T, NP * LANES), F32),
        scratch_shapes=[pltpu.VMEM((TB, LANES), BF16), pltpu.VMEM((TB, LANES), BF16)],
        compiler_params=_params("arbitrary", "arbitrary"),
    )(qkv, qkv, qkv, bias)


def _group_bounds(D):
    return ((0, D // 4), (D // 4, D // 2), (D // 2, D))


def _mix_out_fwd(yab, yc, gg, w, h, g):
    T, D = h.shape

    def body(yab_ref, yc_ref, gg_ref, w_ref, h_ref, g_ref, y_ref, mix_ref, h1_ref):
        yraw = jnp.concatenate([yab_ref[...], yc_ref[...]], axis=1)
        ggv = gg_ref[...]
        y = jnp.concatenate([_rms(yraw[:, a:b], ggv[:, a:b]) for a, b in _group_bounds(D)], axis=1).astype(BF16)
        y_ref[...] = y
        mix = _dot(y, w_ref[...])
        mix_ref[...] = mix
        h1_ref[...] = h_ref[...] + _rms(mix, g_ref[...])

    row = lambda n: pl.BlockSpec((TB, n), lambda i: (i, 0))
    return pl.pallas_call(
        body, name="mix_out_fwd", grid=(T // TB,),
        in_specs=[row(D // 2), row(D // 2), _full((1, D)), _full((D, D)), row(D), _full((1, D))],
        out_specs=[row(D), row(D), row(D)],
        out_shape=[jax.ShapeDtypeStruct((T, D), BF16), jax.ShapeDtypeStruct((T, D), F32), jax.ShapeDtypeStruct((T, D), F32)],
        compiler_params=_params("parallel"),
    )(yab, yc, gg, w, h, g)


def _load_resident(step, w_hbm, w_vmem, sem):
    @pl.when(step == 0)
    def _():
        cp = pltpu.make_async_copy(w_hbm, w_vmem, sem)
        cp.start()
        cp.wait()


def _ffn_up_fwd(h1, g, w):
    T, D = h1.shape
    F2 = w.shape[1]
    DFF = F2 // 2
    TF = DFF // 2

    def body(h_ref, g_ref, w_hbm, hn_ref, gu_ref, ff_ref, w_v, sem):
        _load_resident(pl.program_id(0), w_hbm, w_v, sem)
        hn = _rms(h_ref[...], g_ref[...]).astype(BF16)
        hn_ref[...] = hn
        for j in range(2):
            sg, su = slice(j * TF, (j + 1) * TF), slice(DFF + j * TF, DFF + (j + 1) * TF)
            gate = _dot(hn, w_v[:, sg])
            up = _dot(hn, w_v[:, su])
            gu_ref[:, sg] = gate
            gu_ref[:, su] = up
            ff_ref[:, sg] = (gate * jax.nn.sigmoid(gate) * up).astype(BF16)

    row = lambda n: pl.BlockSpec((TBF, n), lambda i: (i, 0))
    return pl.pallas_call(
        body, name="ffn_up_fwd", grid=(T // TBF,),
        in_specs=[row(D), _full((1, D)), ANY],
        out_specs=[row(D), row(F2), row(DFF)],
        out_shape=[jax.ShapeDtypeStruct((T, D), BF16), jax.ShapeDtypeStruct((T, F2), F32), jax.ShapeDtypeStruct((T, DFF), BF16)],
        scratch_shapes=[pltpu.VMEM(w.shape, BF16), pltpu.SemaphoreType.DMA],
        compiler_params=_params("arbitrary"),
    )(h1, g, w)


def _ffn_down_fwd(ff, w, h1, g):
    T, D = h1.shape
    DFF = ff.shape[1]

    def body(ff_ref, w_ref, h_ref, g_ref, ffo_ref, h2_ref):
        ffo = _dot(ff_ref[...], w_ref[...])
        ffo_ref[...] = ffo
        h2_ref[...] = h_ref[...] + _rms(ffo, g_ref[...])

    row = lambda n: pl.BlockSpec((TBF, n), lambda i: (i, 0))
    return pl.pallas_call(
        body, name="ffn_down_fwd", grid=(T // TBF,),
        in_specs=[row(DFF), _full((DFF, D)), row(D), _full((1, D))],
        out_specs=[row(D), row(D)],
        out_shape=[jax.ShapeDtypeStruct((T, D), F32), jax.ShapeDtypeStruct((T, D), F32)],
        compiler_params=_params("parallel"),
    )(ff, w, h1, g)


def _loss_grad(h, tgt):
    T, D = h.shape

    def body(h_ref, t_ref, dh_ref, loss_ref):
        _acc_init(pl.program_id(0), loss_ref)
        diff = h_ref[...] - t_ref[...]
        dh_ref[...] = diff * (1.0 / D)
        loss_ref[...] += 0.5 * jnp.sum(jnp.mean(diff * diff, axis=-1, keepdims=True))

    row = pl.BlockSpec((TB, D), lambda i: (i, 0))
    return pl.pallas_call(
        body, name="loss_grad", grid=(T // TB,),
        in_specs=[row, row], out_specs=[row, _full((8, LANES))],
        out_shape=[jax.ShapeDtypeStruct((T, D), F32), jax.ShapeDtypeStruct((8, LANES), F32)],
        compiler_params=_params("arbitrary"),
    )(h, tgt)


def _ffn_down_bwd(dh2, ffo, g, gu, w):
    T, D = dh2.shape
    F2 = gu.shape[1]
    DFF = F2 // 2
    TF = DFF // 2

    def body(dh_ref, ffo_ref, g_ref, gu_ref, w_ref, dffo_ref, dgu_ref, dg_ref):
        _acc_init(pl.program_id(0), dg_ref)
        dffo, dg = _rms_bwd(dh_ref[...], ffo_ref[...], g_ref[...])
        dg_ref[0:1, :] += dg
        dffo = dffo.astype(BF16)
        dffo_ref[...] = dffo
        for j in range(2):
            sg, su = slice(j * TF, (j + 1) * TF), slice(DFF + j * TF, DFF + (j + 1) * TF)
            dff = _dot_nt(dffo, w_ref[sg, :])
            gate, up = gu_ref[:, sg], gu_ref[:, su]
            sig = jax.nn.sigmoid(gate)
            dgu_ref[:, sg] = (dff * up * (sig * (1.0 + gate * (1.0 - sig)))).astype(BF16)
            dgu_ref[:, su] = (dff * (gate * sig)).astype(BF16)

    row = lambda n: pl.BlockSpec((TBF, n), lambda i: (i, 0))
    return pl.pallas_call(
        body, name="ffn_down_bwd", grid=(T // TBF,),
        in_specs=[row(D), row(D), _full((1, D)), row(F2), _full((DFF, D))],
        out_specs=[row(D), row(F2), _full((8, D))],
        out_shape=[jax.ShapeDtypeStruct((T, D), BF16), jax.ShapeDtypeStruct((T, F2), BF16), jax.ShapeDtypeStruct((8, D), F32)],
        compiler_params=_params("arbitrary"),
    )(dh2, ffo, g, gu, w)


def _ffn_up_bwd(dgu, w, h1, g, dh2):
    T, D = h1.shape
    F2 = dgu.shape[1]

    def body(dgu_ref, w_hbm, h_ref, g_ref, dh2_ref, dh1_ref, dg_ref, w_v, sem):
        _load_resident(pl.program_id(0), w_hbm, w_v, sem)
        _acc_init(pl.program_id(0), dg_ref)
        dx, dg = _rms_bwd(_dot_nt(dgu_ref[...], w_v[...]), h_ref[...], g_ref[...])
        dg_ref[0:1, :] += dg
        dh1_ref[...] = dh2_ref[...] + dx

    row = lambda n: pl.BlockSpec((TBF, n), lambda i: (i, 0))
    return pl.pallas_call(
        body, name="ffn_up_bwd", grid=(T // TBF,),
        in_specs=[row(F2), ANY, row(D), _full((1, D)), row(D)],
        out_specs=[row(D), _full((8, D))],
        out_shape=[jax.ShapeDtypeStruct((T, D), F32), jax.ShapeDtypeStruct((8, D), F32)],
        scratch_shapes=[pltpu.VMEM(w.shape, BF16), pltpu.SemaphoreType.DMA],
        compiler_params=_params("arbitrary"),
    )(dgu, w, h1, g, dh2)


def _mix_out_bwd(dh1, mix, g, w, yab, yc, gg):
    T, D = dh1.shape

    def body(dh_ref, mix_ref, g_ref, w_ref, yab_ref, yc_ref, gg_ref, dmix_ref, dyab_ref, dyc_ref, dg_ref, dgg_ref):
        _acc_init(pl.program_id(0), dg_ref, dgg_ref)
        dmix, dg = _rms_bwd(dh_ref[...], mix_ref[...], g_ref[...])
        dg_ref[0:1, :] += dg
        dmix = dmix.astype(BF16)
        dmix_ref[...] = dmix
        dy = _dot_nt(dmix, w_ref[...])
        yraw = jnp.concatenate([yab_ref[...], yc_ref[...]], axis=1)
        ggv = gg_ref[...]
        parts = [_rms_bwd(dy[:, a:b], yraw[:, a:b], ggv[:, a:b]) for a, b in _group_bounds(D)]
        dgg_ref[0:1, :] += jnp.concatenate([p[1] for p in parts], axis=1)
        dyab_ref[...] = jnp.concatenate([parts[0][0], parts[1][0]], axis=1)
        dyc_ref[...] = parts[2][0]

    row = lambda n: pl.BlockSpec((TB, n), lambda i: (i, 0))
    return pl.pallas_call(
        body, name="mix_out_bwd", grid=(T // TB,),
        in_specs=[row(D), row(D), _full((1, D)), _full((D, D)), row(D // 2), row(D // 2), _full((1, D))],
        out_specs=[row(D), row(D // 2), row(D // 2), _full((8, D)), _full((8, D))],
        out_shape=[jax.ShapeDtypeStruct((T, D), BF16), jax.ShapeDtypeStruct((T, D // 2), F32), jax.ShapeDtypeStruct((T, D // 2), F32),
                   jax.ShapeDtypeStruct((8, D), F32), jax.ShapeDtypeStruct((8, D), F32)],
        compiler_params=_params("arbitrary"),
    )(dh1, mix, g, w, yab, yc, gg)


def _attn_bwd(qkv, dyc, yc, bias):
    T = qkv.shape[0]
    NP = qkv.shape[1] // (3 * LANES)
    NB = T // TB
    scale = HEAD_DIM ** -0.5

    def body(q_ref, kc_ref, vc_ref, kp_ref, vp_ref, do_ref, o_ref, b_ref, dq_ref, dk_ref, dv_ref, db_ref,
             dk_carry, dv_carry, dq_acc, ds_full, p_full):
        step = pl.program_id(1)
        i = NB - 1 - step
        _acc_init(step, dk_carry, dv_carry, db_ref, ds_full, p_full)
        q = q_ref[...]
        kwin = jnp.concatenate([kp_ref[...], kc_ref[...]], axis=0)
        vwin = jnp.concatenate([vp_ref[...], vc_ref[...]], axis=0)
        do = do_ref[...]
        dob = do.astype(BF16)
        prod = do * o_ref[...]
        col = lax.broadcasted_iota(jnp.int32, (1, BANDW), 1)
        dkw = jnp.zeros((2 * TB, LANES), F32)
        dvw = jnp.zeros((2 * TB, LANES), F32)
        for a, ma in enumerate(_head_masks()):
            qa, doa = jnp.where(ma, q, 0), jnp.where(ma, dob, 0)
            delta = jnp.sum(jnp.where(ma, prod, 0.0), axis=-1, keepdims=True)
            for s in range(TB // SUB):
                rows, band = slice(s * SUB, (s + 1) * SUB), slice(s * SUB, s * SUB + BANDW)
                before_start = jnp.logical_and(col < TB - s * SUB, i == 0)
                sc = _dot_nt(qa[rows], kwin[band]) * scale + b_ref[a]
                p = _softmax_rows(jnp.where(before_start, NEG, sc))
                ds = p * (_dot_nt(doa[rows], vwin[band]) - delta[rows])
                db_ref[a] += ds
                dsb = (ds * scale).astype(BF16)
                ds_full[rows, band] = dsb
                p_full[rows, band] = p.astype(BF16)
                dq = _dot(dsb, jnp.where(ma, kwin[band], 0))
                if a == 0:
                    dq_acc[rows, :] = dq
                else:
                    dq_ref[rows, :] = (dq_acc[rows, :] + dq).astype(BF16)
            dkw = dkw + _dot_tn(ds_full[...], qa)
            dvw = dvw + _dot_tn(p_full[...], doa)
        dk_ref[...] = (dkw[TB:] + dk_carry[...]).astype(BF16)
        dv_ref[...] = (dvw[TB:] + dv_carry[...]).astype(BF16)
        dk_carry[...] = dkw[:TB]
        dv_carry[...] = dvw[:TB]

    cur = lambda off: pl.BlockSpec((TB, LANES), lambda p, s: (NB - 1 - s, off + p))
    prev = lambda off: pl.BlockSpec((TB, LANES), lambda p, s: (jnp.maximum(NB - 2 - s, 0), off + p))
    out = jax.ShapeDtypeStruct((T, NP * LANES), BF16)
    strip = pl.BlockSpec((2, SUB, BANDW), lambda p, s: (p, 0, 0))
    return pl.pallas_call(
        body, name="attn_bwd", grid=(NP, NB),
        in_specs=[cur(0), cur(NP), cur(2 * NP), prev(NP), prev(2 * NP), cur(0), cur(0), strip],
        out_specs=[cur(0), cur(0), cur(0), strip],
        out_shape=[out, out, out, jax.ShapeDtypeStruct((2 * NP, SUB, BANDW), F32)],
        scratch_shapes=[pltpu.VMEM((TB, LANES), F32), pltpu.VMEM((TB, LANES), F32), pltpu.VMEM((TB, LANES), F32),
                        pltpu.VMEM((TB, 2 * TB), BF16), pltpu.VMEM((TB, 2 * TB), BF16)],
        compiler_params=_params("arbitrary", "arbitrary"),
    )(qkv, qkv, qkv, qkv, qkv, dyc, yc, bias)


def _convpool_bwd(pa, dyab, cw, wbd, ps):
    T, D = pa.shape
    DC = D // 4
    NB = T // TB
    N = TB + HALO

    def body(pa_ref, halo_ref, dy_ref, cw_ref, wbd_ref, ps_ref, dpa_ref, dcw_ref, dwbd_ref, dps_ref, dc_carry, e_carry):
        step = pl.program_id(0)
        i = NB - 1 - step
        _acc_init(step, dcw_ref, dwbd_ref, dps_ref, dc_carry, e_carry)
        x = pa_ref[...]
        hl = jnp.where(i > 0, halo_ref[...], 0.0)
        gb, gc, u, pu = (x[:, k * DC:(k + 1) * DC] for k in range(4))
        dy = dy_ref[...]
        dya, dyb = dy[:, :DC], dy[:, DC:]
        cwv = cw_ref[...]
        z = gc * u
        z1, z2 = _conv_taps(z, hl[:, DC:2 * DC] * hl[:, 2 * DC:3 * DC])
        dgb = dya * (cwv[2:3] * z + cwv[1:2] * z1 + cwv[0:1] * z2)
        dconv = dya * gb
        dcw_ref[0:1, :] += jnp.sum(dconv * z2, axis=0, keepdims=True)
        dcw_ref[1:2, :] += jnp.sum(dconv * z1, axis=0, keepdims=True)
        dcw_ref[2:3, :] += jnp.sum(dconv * z, axis=0, keepdims=True)
        ext = jnp.concatenate([dconv, dc_carry[...]], axis=0)
        dz = cwv[2:3] * dconv + cwv[1:2] * pltpu.roll(ext, N - 1, 0)[:TB] + cwv[0:1] * pltpu.roll(ext, N - 2, 0)[:TB]
        dc_carry[...] = dconv[:HALO]
        cnt = _pool_count(i, DC)
        d = _pool_d(pu, hl[:, 3 * DC:], cnt).astype(BF16)
        psv = ps_ref[...]
        w = wbd_ref[...]
        dps_ref[0:1, :] += jnp.sum(dyb * _dot(d, w), axis=0, keepdims=True)
        dys = (dyb * psv).astype(BF16)
        dwbd_ref[...] += _dot_tn(d, dys)
        dd = _dot_nt(dys, w)
        e = dd / cnt
        ext = jnp.concatenate([e, e_carry[...]], axis=0)
        a2 = ext + pltpu.roll(ext, N - 1, 0)
        a4 = a2 + pltpu.roll(a2, N - 2, 0)
        a8 = a4 + pltpu.roll(a4, N - 4, 0)
        a16 = a8 + pltpu.roll(a8, N - 8, 0)
        dpu = _pick_group([a2, a4, a8, a16], DC)[:TB] - dd
        e_carry[...] = e[:HALO]
        dpa_ref[...] = jnp.concatenate([dgb, dz * u, dz * gc, dpu], axis=1).astype(BF16)

    blk = lambda n: pl.BlockSpec((TB, n), lambda s: (NB - 1 - s, 0))
    return pl.pallas_call(
        body, name="convpool_bwd", grid=(NB,),
        in_specs=[blk(D), pl.BlockSpec((HALO, D), lambda s: (jnp.maximum((NB - 1 - s) * (TB // HALO) - 1, 0), 0)),
                  blk(2 * DC), _full((8, DC)), _full((DC, DC)), _full((1, DC))],
        out_specs=[blk(D), _full((8, DC)), _full((DC, DC)), _full((8, DC))],
        out_shape=[jax.ShapeDtypeStruct((T, D), BF16), jax.ShapeDtypeStruct((8, DC), F32),
                   jax.ShapeDtypeStruct((DC, DC), F32), jax.ShapeDtypeStruct((8, DC), F32)],
        scratch_shapes=[pltpu.VMEM((HALO, DC), F32), pltpu.VMEM((HALO, DC), F32)],
        compiler_params=_params("arbitrary"),
    )(pa, pa, dyab, cw, wbd, ps)


def _inproj_bwd(dparts, w, h, g, dh1):
    T, D = h.shape
    widths = [p.shape[1] for p in dparts]
    n = len(dparts)

    def body(*refs):
        parts, (w_ref, h_ref, g_ref, dh1_ref, dh_ref, dg_ref) = refs[:n], refs[n:]
        _acc_init(pl.program_id(0), dg_ref)
        dxn, off = jnp.zeros((TB, D), F32), 0
        for p_ref, wd in zip(parts, widths):
            dxn = dxn + _dot_nt(p_ref[...], w_ref[:, off:off + wd])
            off += wd
        dx, dg = _rms_bwd(dxn, h_ref[...], g_ref[...])
        dg_ref[0:1, :] += dg
        dh_ref[...] = dh1_ref[...] + dx

    row = lambda m: pl.BlockSpec((TB, m), lambda i: (i, 0))
    return pl.pallas_call(
        body, name="inproj_bwd", grid=(T // TB,),
        in_specs=[row(wd) for wd in widths] + [_full(w.shape), row(D), _full((1, D)), row(D)],
        out_specs=[row(D), _full((8, D))],
        out_shape=[jax.ShapeDtypeStruct((T, D), F32), jax.ShapeDtypeStruct((8, D), F32)],
        compiler_params=_params("arbitrary"),
    )(*dparts, w, h, g, dh1)


def _wgrad(a, b, tk, tn, name):
    T, K = a.shape
    N = b.shape[1]

    def body(a_ref, b_ref, o_ref):
        _acc_init(pl.program_id(2), o_ref)
        o_ref[...] += _dot_tn(a_ref[...], b_ref[...])

    return pl.pallas_call(
        body, name=name, grid=(K // tk, N // tn, T // TB),
        in_specs=[pl.BlockSpec((TB, tk), lambda k, n, t: (t, k)), pl.BlockSpec((TB, tn), lambda k, n, t: (t, n))],
        out_specs=pl.BlockSpec((tk, tn), lambda k, n, t: (k, n)),
        out_shape=jax.ShapeDtypeStruct((K, N), F32),
        compiler_params=_params("parallel", "parallel", "arbitrary"),
    )(a, b)


def _wgrad_concat(a, bs, name):
    T, K = a.shape
    widths = [b.shape[1] for b in bs]
    n = len(bs)

    def body(*refs):
        a_ref, b_refs, o_ref = refs[0], refs[1:1 + n], refs[1 + n]
        _acc_init(pl.program_id(0), o_ref)
        av, off = a_ref[...], 0
        for b_ref, wd in zip(b_refs, widths):
            o_ref[:, off:off + wd] += _dot_tn(av, b_ref[...])
            off += wd

    row = lambda m: pl.BlockSpec((TB, m), lambda t: (t, 0))
    return pl.pallas_call(
        body, name=name, grid=(T // TB,),
        in_specs=[row(K)] + [row(wd) for wd in widths],
        out_specs=_full((K, sum(widths))),
        out_shape=jax.ShapeDtypeStruct((K, sum(widths)), F32),
        compiler_params=_params("arbitrary"),
    )(a, *bs)


def _adamw(w, g, m, v, tr, name):
    R, C = w.shape

    def body(w_ref, g_ref, m_ref, v_ref, d_ref, nm_ref, nv_ref):
        gv = g_ref[...]
        nm = ADAM_B1 * m_ref[...] + (1.0 - ADAM_B1) * gv
        nv = ADAM_B2 * v_ref[...] + (1.0 - ADAM_B2) * (gv * gv)
        m_hat = nm / (1.0 - ADAM_B1 ** ADAM_STEP)
        v_hat = nv / (1.0 - ADAM_B2 ** ADAM_STEP)
        d_ref[...] = -ADAM_LR * (m_hat / (jnp.sqrt(v_hat) + ADAM_EPS) + ADAM_WD * w_ref[...])
        nm_ref[...] = nm
        nv_ref[...] = nv

    blk = pl.BlockSpec((tr, C), lambda i: (i, 0))
    out = jax.ShapeDtypeStruct((R, C), F32)
    return pl.pallas_call(
        body, name=name, grid=(R // tr,), in_specs=[blk] * 4, out_specs=[blk] * 3, out_shape=[out] * 3,
        compiler_params=_params("parallel"),
    )(w, g, m, v)


def _sum_slots(x, tr, name):
    n, R, C = x.shape

    def body(x_ref, o_ref):
        acc = x_ref[0]
        for k in range(1, n):
            acc = acc + x_ref[k]
        o_ref[...] = acc

    return pl.pallas_call(
        body, name=name, grid=(R // tr,),
        in_specs=[pl.BlockSpec((n, tr, C), lambda i: (0, i, 0))],
        out_specs=pl.BlockSpec((tr, C), lambda i: (i, 0)),
        out_shape=jax.ShapeDtypeStruct((R, C), F32),
        compiler_params=_params("parallel"),
    )(x)


def _add_half(view, other, c, name):
    if view.ndim == 3:
        _, R2, N = view.shape
        tr = 128
        grid = (R2 // tr,)
        in_specs = [pl.BlockSpec((1, tr, N), lambda i, c_ref: (c_ref[0], i, 0)), pl.BlockSpec((1, tr, N), lambda i, c_ref: (0, i, 0))]
        out_spec = pl.BlockSpec((tr, N), lambda i, c_ref: (i, 0))
        out_shape = jax.ShapeDtypeStruct((R2, N), F32)

        def body(c_ref, a_ref, b_ref, o_ref):
            o_ref[...] = a_ref[0] + b_ref[0]
    else:
        S, _, R2, C = view.shape
        grid = (S,)
        in_specs = [pl.BlockSpec((1, 1, R2, C), lambda s, c_ref: (s, c_ref[0], 0, 0)), pl.BlockSpec((1, 1, R2, C), lambda s, c_ref: (s, 0, 0, 0))]
        out_spec = pl.BlockSpec((1, R2, C), lambda s, c_ref: (s, 0, 0))
        out_shape = jax.ShapeDtypeStruct((S, R2, C), F32)

        def body(c_ref, a_ref, b_ref, o_ref):
            o_ref[0] = a_ref[0, 0] + b_ref[0, 0]

    return pl.pallas_call(
        body, name=name,
        grid_spec=pltpu.PrefetchScalarGridSpec(num_scalar_prefetch=1, grid=grid, in_specs=in_specs, out_specs=out_spec),
        out_shape=out_shape, compiler_params=_params("parallel"),
    )(c, view, other)


def _sum_chips(slots, part, kind, place, name):
    _, R2, C = slots.shape
    tr = min(R2, 128) if kind == "col" else R2

    def body(p_ref, s_ref, own_ref, o_ref):
        me = p_ref[0]
        own = own_ref[...] if kind == "col" else own_ref[0]
        acc = None
        for k in range(4):
            term = jnp.where(me == k, own, s_ref[k])
            acc = term if acc is None else acc + term
        o_ref[0] = acc

    own_spec = (pl.BlockSpec((tr, C), lambda i, p: (i, p[0])) if kind == "col"
                else pl.BlockSpec((1, R2, C), lambda i, p: (p[0], 0, 0)))
    return pl.pallas_call(
        body, name=name,
        grid_spec=pltpu.PrefetchScalarGridSpec(
            num_scalar_prefetch=1, grid=(R2 // tr,),
            in_specs=[pl.BlockSpec((4, tr, C), lambda i, p: (0, i, 0)), own_spec],
            out_specs=pl.BlockSpec((1, tr, C), lambda i, p: (p[1], i, 0))),
        out_shape=jax.ShapeDtypeStruct((2, R2, C), F32), compiler_params=_params("parallel"),
    )(place, slots, part)


def _place():
    x, y, c = lax.axis_index("x"), lax.axis_index("y"), lax.axis_index("c")
    chips = [(1 - x, y), (x, 1 - y), (1 - x, 1 - y)]
    return x, y, c, 2 * x + y, chips


def _remote(src, dst, send_sems, recv_sems, k, to):
    return pltpu.make_async_remote_copy(src_ref=src, dst_ref=dst, send_sem=send_sems.at[k], recv_sem=recv_sems.at[k],
                                        device_id=to, device_id_type=MESH)


def _full_shape(shard_shape, kind):
    *lead, R, C = shard_shape
    return (*lead, R, 4 * C) if kind == "col" else (*lead, 4 * R, C)


def _slab(full_ref, shard_rc, kind, chip, half=None):
    R, C = shard_rc
    lead = (slice(None),) * (len(full_ref.shape) - 2)
    if kind == "col":
        rows = pl.ds(0, R) if half is None else pl.ds(half * (R // 2), R // 2)
        return full_ref.at[(*lead, rows, pl.ds(chip * C, C))]
    rows = pl.ds(chip * R, R) if half is None else pl.ds(chip * R + half * (R // 2), R // 2)
    return full_ref.at[(*lead, rows, slice(None))]


def _row_half(ref, half):
    R = ref.shape[-2]
    lead = (slice(None),) * (len(ref.shape) - 2)
    return ref.at[(*lead, pl.ds(half * (R // 2), R // 2), slice(None))]


def _allgather_weights(shards, kinds, after):
    n = len(shards)
    full_shapes = [_full_shape(s.shape, k) for s, k in zip(shards, kinds)]

    def body(*refs):
        ins, outs, (send_sems, recv_sems) = refs[:n], refs[n + 1:2 * n + 1], refs[2 * n + 1:]
        x, y, c, me, chips = _place()
        slab = lambda a, chip, half=None: _slab(outs[a], shards[a].shape[-2:], kinds[a], chip, half)
        my_half = lambda a: _row_half(ins[a], c)
        own = [_remote(ins[a], slab(a, me), send_sems, recv_sems, 6 * n + a, (x, y, 1 - c)) for a in range(n)]
        first = [_remote(my_half(a), slab(a, me, c), send_sems, recv_sems, j * n + a, (*chip, c))
                 for j, chip in enumerate(chips) for a in range(n)]
        for cp in first + own:
            cp.start()
        passed = []
        for j, chip in enumerate(chips):
            src = 2 * chip[0] + chip[1]
            for a in range(n):
                _remote(my_half(a), slab(a, src, c), send_sems, recv_sems, j * n + a, (x, y, c)).wait_recv()
                cp = _remote(slab(a, src, c), slab(a, src, c), send_sems, recv_sems, (3 + j) * n + a, (x, y, 1 - c))
                cp.start()
                passed.append(cp)
        for j, chip in enumerate(chips):
            src = 2 * chip[0] + chip[1]
            for a in range(n):
                _remote(my_half(a), slab(a, src, 1 - c), send_sems, recv_sems, (3 + j) * n + a, (x, y, c)).wait_recv()
        for cp in own:
            cp.wait_recv()
        for cp in first + passed + own:
            cp.wait_send()

    return pl.pallas_call(
        body, name="allgather_weights",
        in_specs=[ANY] * (n + 1), out_specs=[ANY] * n,
        out_shape=[jax.ShapeDtypeStruct(s, BF16) for s in full_shapes],
        scratch_shapes=[pltpu.SemaphoreType.DMA((7 * n,)), pltpu.SemaphoreType.DMA((7 * n,))],
        compiler_params=pltpu.CompilerParams(has_side_effects=True),
    )(*shards, after)


HBM = pl.BlockSpec(memory_space=pltpu.HBM)
SEM = pl.BlockSpec(memory_space=pltpu.SEMAPHORE)
DATAFLOW = pltpu.SideEffectType.DATAFLOW_SIDE_EFFECTING


def _split_start(name, bufs, plan, ncopies, after=None):
    nb = len(bufs)
    nin = nb + (after is not None)

    def body(*refs):
        send_sems, recv_sems, token = refs[nin], refs[nin + 1], refs[-1]
        for k, (src, dst, to) in enumerate(plan(refs[:nb])):
            _remote(src, dst, send_sems, recv_sems, k, to).start()
        token[...] = jnp.zeros_like(token)

    out = pl.pallas_call(
        body, name=name,
        out_shape=(pltpu.SemaphoreType.DMA((ncopies,)), pltpu.SemaphoreType.DMA((ncopies,)),
                   *[pltpu.HBM(b.shape, b.dtype) for b in bufs], jax.ShapeDtypeStruct((8, LANES), F32)),
        in_specs=[HBM] * nb + [ANY] * (nin - nb), out_specs=(SEM, SEM, *[HBM] * nb, pl.BlockSpec(memory_space=pltpu.VMEM)),
        input_output_aliases={i: 2 + i for i in range(nb)},
        compiler_params=pltpu.CompilerParams(has_side_effects=DATAFLOW),
    )(*[pltpu.with_memory_space_constraint(b, pltpu.HBM) for b in bufs], *([] if after is None else [after]))
    return out[0], out[1], list(out[2:2 + nb]), out[-1]


def _split_wait(name, send_sems, recv_sems, bufs, plan, after):
    nb = len(bufs)

    def body(*refs):
        s_sems, r_sems = refs[nb], refs[nb + 1]
        for k, (src, dst, to) in enumerate(plan(refs[:nb])):
            cp = _remote(src, dst, s_sems, r_sems, k, to)
            cp.wait_send()
            cp.wait_recv()

    return pl.pallas_call(
        body, name=name,
        out_shape=tuple(pltpu.HBM(b.shape, b.dtype) for b in bufs),
        in_specs=[HBM] * nb + [SEM, SEM, ANY], out_specs=tuple([HBM] * nb),
        input_output_aliases={i: i for i in range(nb)},
        compiler_params=pltpu.CompilerParams(has_side_effects=DATAFLOW),
    )(*bufs, send_sems, recv_sems, after)


def _gather_plan(n, shard_rcs, kinds):
    def plan(refs):
        x, y, c, me, chips = _place()
        out = []
        for a in range(n):
            shard, full = refs[a], refs[n + a]
            out.append((shard, _slab(full, shard_rcs[a], kinds[a], me), (x, y, 1 - c)))
            for chip in chips:
                for cc in (c, 1 - c):
                    out.append((_row_half(shard, c), _slab(full, shard_rcs[a], kinds[a], me, c), (*chip, cc)))
        return out
    return plan


def _exchange_plan(n, kinds):
    def plan(refs):
        x, y, c, me, chips = _place()
        out = []
        for a in range(n):
            part, slots = refs[a], refs[n + a]
            C = slots.shape[2]
            for chip in chips:
                dst = 2 * chip[0] + chip[1]
                src = part.at[:, pl.ds(dst * C, C)] if kinds[a] == "col" else part.at[dst]
                out.append((src, slots.at[me], (*chip, c)))
        return out
    return plan


def _swap_halves(views):
    n = len(views)
    out_shapes = [(1,) + v.shape[1:] if v.ndim == 3 else (v.shape[0], 1) + v.shape[2:] for v in views]

    def body(*refs):
        ins, outs, (send_sems, recv_sems) = refs[:n], refs[n:2 * n], refs[2 * n:]
        x, y, c, _, _ = _place()
        cps = []
        for a in range(n):
            src = ins[a].at[pl.ds(1 - c, 1)] if views[a].ndim == 3 else ins[a].at[:, pl.ds(1 - c, 1)]
            cps.append(_remote(src, outs[a], send_sems, recv_sems, a, (x, y, 1 - c)))
        for cp in cps:
            cp.start()
        for cp in cps:
            cp.wait()

    return pl.pallas_call(
        body, name="grad_swap_halves", in_specs=[ANY] * n, out_specs=[ANY] * n,
        out_shape=[jax.ShapeDtypeStruct(s, F32) for s in out_shapes],
        scratch_shapes=[pltpu.SemaphoreType.DMA((n,)), pltpu.SemaphoreType.DMA((n,))],
        compiler_params=pltpu.CompilerParams(has_side_effects=True),
    )(*views)


def _join_halves(bufs):
    n = len(bufs)

    def body(*refs):
        outs, (send_sems, recv_sems) = refs[n:2 * n], refs[2 * n:]
        x, y, c, _, _ = _place()
        sends = [_remote(outs[a].at[c], outs[a].at[c], send_sems, recv_sems, a, (x, y, 1 - c)) for a in range(n)]
        for cp in sends:
            cp.start()
        for a in range(n):
            _remote(outs[a].at[c], outs[a].at[1 - c], send_sems, recv_sems, a, (x, y, c)).wait_recv()
        for cp in sends:
            cp.wait_send()

    return pl.pallas_call(
        body, name="grad_join_halves", in_specs=[ANY] * n, out_specs=[ANY] * n,
        out_shape=[jax.ShapeDtypeStruct(b.shape, F32) for b in bufs],
        input_output_aliases={a: a for a in range(n)},
        scratch_shapes=[pltpu.SemaphoreType.DMA((n,)), pltpu.SemaphoreType.DMA((n,))],
        compiler_params=pltpu.CompilerParams(has_side_effects=True),
    )(*bufs)


def _allgather_small(block):
    M, N = block.shape

    def body(x_ref, out_ref, send_sems, recv_sems, local_sem):
        x, y, c, _, chips = _place()
        me, sibling = (x, y, c), (x, y, 1 - c)

        def rows(px, py, pc):
            return out_ref.at[pl.ds((4 * px + 2 * py + pc) * M, M), :]

        def copy(k, blk, to, src=None):
            return _remote(rows(*blk) if src is None else src, rows(*blk), send_sems, recv_sems, k, to)

        mine = pltpu.make_async_copy(x_ref, rows(*me), local_sem)
        mine.start()
        first = [copy(0, me, sibling, src=x_ref)] + [copy(1 + j, me, (*chip, c), src=x_ref) for j, chip in enumerate(chips)]
        for cp in first:
            cp.start()
        passed = [copy(4 + j, (*chip, c), sibling) for j, chip in enumerate(chips)]
        for j, chip in enumerate(chips):
            copy(1 + j, (*chip, c), me).wait_recv()
            passed[j].start()
        copy(0, sibling, me).wait_recv()
        for j, chip in enumerate(chips):
            copy(4 + j, (*chip, 1 - c), me).wait_recv()
        for cp in first + passed:
            cp.wait_send()
        mine.wait()

    vm = pl.BlockSpec(memory_space=pltpu.VMEM)
    return pl.pallas_call(
        body, name="allgather_small", in_specs=[vm], out_specs=vm,
        out_shape=jax.ShapeDtypeStruct((8 * M, N), F32),
        scratch_shapes=[pltpu.SemaphoreType.DMA((7,)), pltpu.SemaphoreType.DMA((7,)), pltpu.SemaphoreType.DMA],
        compiler_params=pltpu.CompilerParams(has_side_effects=True, vmem_limit_bytes=VMEM_LIMIT),
    )(block)


def _pack(arrays):
    flat = jnp.concatenate([a.reshape(-1) for a in arrays])
    pad = (-flat.shape[0]) % (8 * LANES)
    return jnp.pad(flat, (0, pad)).reshape(-1, LANES)


def _unpack(packed, shapes):
    flat, out, off = packed.reshape(-1), [], 0
    for s in shapes:
        size = 1
        for d in s:
            size *= d
        out.append(flat[off:off + size].reshape(s))
        off += size
    return out


def _block_diag(pw):
    G, n, _ = pw.shape
    eye = jnp.eye(G, dtype=pw.dtype)
    return (eye[:, None, :, None] * pw[:, :, None, :]).reshape(G * n, G * n)


def _diag_blocks(m, G):
    n = m.shape[0] // G
    return jnp.stack([m[g * n:(g + 1) * n, g * n:(g + 1) * n] for g in range(G)])


def _pad_rows(a, rows):
    return jnp.pad(a, ((0, rows - a.shape[0]), (0, 0)))


def kernel(x, w_in, w_out, conv_w, pool_w, pool_scale, rel_bias, group_gain, pre_mix_g, post_mix_g, pre_ffn_g, post_ffn_g, w_gate_up, w_down, loss_target, m_w_in, m_w_out, m_conv_w, m_pool_w, m_pool_scale, m_rel_bias, m_group_gain, m_pre_mix_g, m_post_mix_g, m_pre_ffn_g, m_post_ffn_g, m_w_gate_up, m_w_down, v_w_in, v_w_out, v_conv_w, v_pool_w, v_pool_scale, v_rel_bias, v_group_gain, v_pre_mix_g, v_post_mix_g, v_pre_ffn_g, v_post_ffn_g, v_w_gate_up, v_w_down):
    L = w_in.shape[0]
    T, D = x.shape[1], x.shape[2]
    DC = D // 4
    NH = rel_bias.shape[1]
    NREL = rel_bias.shape[2]
    G = pool_w.shape[1]
    cs = conv_w.shape[2]
    assert TB == LEFT_CHUNKS * CHUNK and T % TB == 0 and D % (4 * LANES) == 0 and NH * HEAD_DIM == D // 2
    xi, yi, ci = lax.axis_index("x"), lax.axis_index("y"), lax.axis_index("c")
    chip = 2 * xi + yi

    kinds = ("col", "row", "col", "row")
    big = (w_in, w_out, w_gate_up, w_down)
    rcs = [w.shape[-2:] for w in big]
    gplan = _gather_plan(4, rcs, kinds)
    ncopy_gather = 4 * 7
    conv_gathered = _allgather_small(_pack([conv_w]))
    conv_all = conv_gathered.reshape(8, -1)[:, :L * 3 * cs].reshape(4, 2, L, 3, cs)[:, 0]
    conv_full = jnp.moveaxis(conv_all, 0, 2).reshape(L, 3, 4 * cs)
    full = [[f[0] for f in _allgather_weights([w[0:1].astype(BF16) for w in big], kinds, conv_gathered)]]

    h = x[0]
    saved = []
    for l in range(L):
        g_pre, g_pm, g_pf, g_po = (a[l][None] for a in (pre_mix_g, post_mix_g, pre_ffn_g, post_ffn_g))
        if l + 1 < L:
            bufs = [w[l + 1].astype(BF16) for w in big] + [lax.empty(_full_shape(rc, k), BF16) for rc, k in zip(rcs, kinds)]
            ssem, rsem, thru, token = _split_start(f"gather_start_{l + 1}", bufs, gplan, ncopy_gather, after=full[l][0])
            g_pre = g_pre + token[0:1, 0:1]
        gg, ps = group_gain[l][None], pool_scale[l][None]
        cw = _pad_rows(conv_full[l], 8)
        wbd = _block_diag(pool_w[l]).astype(BF16)
        bias = _bias_build(jnp.pad(rel_bias[l], ((0, 0), (0, RBP - NREL)))[:, None, :])
        wi, wo, wgu, wdn = full[l]
        xn, pa, qkv = _inproj_fwd(h, g_pre, wi)
        yab = _convpool_fwd(pa, cw, wbd, ps)
        yc = _attn_fwd(qkv, bias)
        y, mix, h1 = _mix_out_fwd(yab, yc, gg, wo, h, g_pm)
        hn, gu, ff = _ffn_up_fwd(h1, g_pf, wgu)
        ffo, h2 = _ffn_down_fwd(ff, wdn, h1, g_po)
        saved.append(dict(h=h, xn=xn, pa=pa, qkv=qkv, yab=yab, yc=yc, y=y, mix=mix, h1=h1, hn=hn, gu=gu, ff=ff, ffo=ffo,
                          cw=cw, wbd=wbd, bias=bias, ps=ps, gg=gg, g_pre=g_pre, g_pm=g_pm, g_pf=g_pf, g_po=g_po))
        h = h2
        if l + 1 < L:
            full.append(list(_split_wait(f"gather_wait_{l + 1}", ssem, rsem, thru, gplan, h2)[4:8]))

    dh, loss_tile = _loss_grad(h, loss_target[0])
    loss = lax.psum(loss_tile[0, 0], ("x", "y", "c"))

    xplan = _exchange_plan(4, kinds)
    place = jnp.stack([chip, ci]).astype(jnp.int32)
    cvec = ci.reshape(1).astype(jnp.int32)
    small_grads = [None] * L
    shard_grads = [None] * L

    def finish_exchange(pending, after):
        lp, ssem, rsem, thru = pending
        landed = _split_wait(f"grad_exchange_wait_{lp}", ssem, rsem, thru, xplan, after)
        bufs = [_sum_chips(landed[4 + a], landed[a], kinds[a], place, "grad_sum_chips") for a in range(4)]
        shard_grads[lp] = [j.reshape(2 * j.shape[1], j.shape[2]) for j in _join_halves(bufs)]

    pending = None
    token = None
    for l in reversed(range(L)):
        s = saved[l]
        wi, wo, wgu, wdn = full[l]
        g_po = s["g_po"] if token is None else s["g_po"] + token[0:1, 0:1]
        dffo, dgu, dg_po = _ffn_down_bwd(dh, s["ffo"], g_po, s["gu"], wdn)
        dh1, dg_pf = _ffn_up_bwd(dgu, wgu, s["h1"], s["g_pf"], dh)
        dmix, dyab, dyc, dg_pm, dgg = _mix_out_bwd(dh1, s["mix"], s["g_pm"], wo, s["yab"], s["yc"], s["gg"])
        dq, dk, dv, dbias = _attn_bwd(s["qkv"], dyc, s["yc"], s["bias"])
        dpa, dcw, dwbd, dps = _convpool_bwd(s["pa"], dyab, s["cw"], s["wbd"], s["ps"])
        dparts = [dpa, dq, dk, dv]
        dh, dg_pre = _inproj_bwd(dparts, wi, s["h"], s["g_pre"], dh1)
        F2, DFF = s["gu"].shape[1], s["ff"].shape[1]
        grads = [_wgrad_concat(s["xn"], dparts, "wgrad_in"),
                 _wgrad(s["y"], dmix, D, D, "wgrad_out"),
                 _wgrad(s["hn"], dgu, D, F2 // 4, "wgrad_gate_up"),
                 _wgrad(s["ff"], dffo, DFF // 2, D, "wgrad_down")]
        drb = _bias_fold(dbias)[:, 0, :NREL]
        small_grads[l] = [dcw[:3], _diag_blocks(dwbd, G), dps[0], drb, dgg[0], dg_pre[0], dg_pm[0], dg_pf[0], dg_po[0]]
        views = [g.reshape(2, g.shape[0] // 2, g.shape[1]) if k == "col" else g.reshape(4, 2, g.shape[0] // 8, g.shape[1])
                 for g, k in zip(grads, kinds)]
        theirs = _swap_halves(views)
        chip_sums = [_add_half(v, t, cvec, "grad_add_half") for v, t in zip(views, theirs)]
        slots = [lax.empty((4, p.shape[0], p.shape[1] // 4) if k == "col" else p.shape, F32) for p, k in zip(chip_sums, kinds)]
        ssem, rsem, thru, token = _split_start(f"grad_exchange_start_{l}", chip_sums + slots, xplan, 4 * 3)
        if pending is not None:
            finish_exchange(pending, dh)
        pending = (l, ssem, rsem, thru)
    finish_exchange(pending, dh)
    g_big = [jnp.stack([shard_grads[l][k] for l in range(L)]) for k in range(4)]

    names_shapes = [(L, 3, 4 * cs), pool_w.shape, pool_scale.shape, rel_bias.shape, group_gain.shape,
                    pre_mix_g.shape, post_mix_g.shape, pre_ffn_g.shape, post_ffn_g.shape]
    small_stacked = [jnp.stack([small_grads[l][k] for l in range(L)]) for k in range(len(names_shapes))]
    packed = _pack(small_stacked)
    M = packed.shape[0]
    total = _sum_slots(_allgather_small(packed).reshape(8, M, LANES), M, "small_sum_devices")
    g_small = _unpack(total, names_shapes)
    g_small[0] = lax.dynamic_slice_in_dim(g_small[0], chip * cs, cs, axis=2)

    def adam_big(w, g, m, v, name):
        shp = w.shape
        two = lambda a: a.reshape(shp[0] * shp[1], shp[2])
        return [o.reshape(shp) for o in _adamw(two(w), two(g), two(m), two(v), 256, name)]

    upd_in = adam_big(w_in, g_big[0], m_w_in, v_w_in, "adamw_in")
    upd_out = adam_big(w_out, g_big[1], m_w_out, v_w_out, "adamw_out")
    upd_gu = adam_big(w_gate_up, g_big[2], m_w_gate_up, v_w_gate_up, "adamw_gate_up")
    upd_dn = adam_big(w_down, g_big[3], m_w_down, v_w_down, "adamw_down")

    small_w = [conv_w, pool_w, pool_scale, rel_bias, group_gain, pre_mix_g, post_mix_g, pre_ffn_g, post_ffn_g]
    small_m = [m_conv_w, m_pool_w, m_pool_scale, m_rel_bias, m_group_gain, m_pre_mix_g, m_post_mix_g, m_pre_ffn_g, m_post_ffn_g]
    small_v = [v_conv_w, v_pool_w, v_pool_scale, v_rel_bias, v_group_gain, v_pre_mix_g, v_post_mix_g, v_pre_ffn_g, v_post_ffn_g]
    pw_, pg_, pm_, pv_ = _pack(small_w), _pack(g_small), _pack(small_m), _pack(small_v)
    shapes = [w.shape for w in small_w]
    upd_small = [_unpack(o, shapes) for o in _adamw(pw_, pg_, pm_, pv_, pw_.shape[0], "adamw_small")]

    def ordered(big4, small9):
        return [big4[0], big4[1], *small9, big4[2], big4[3]]

    grads = ordered(g_big, g_small)
    outs = [ordered([upd_in[k], upd_out[k], upd_gu[k], upd_dn[k]], upd_small[k]) for k in range(3)]
    return (loss, dh[None], *grads, *outs[0], *outs[1], *outs[2])
```

```python
import functools

import jax
import jax.numpy as jnp
from jax import lax
from jax.experimental import pallas as pl
from jax.experimental.pallas import tpu as pltpu

F32, BF16 = jnp.float32, jnp.bfloat16
EPS = 1e-6
CHUNK = 64
LEFT_CHUNKS = 8
REL_CLIP = 128
HEAD_DIM = 64
LANES = 128
POOL_WINDOWS = (2, 4, 8, 16)
HALO = 16
TB = LEFT_CHUNKS * CHUNK
TBF = 256
BAND = (LEFT_CHUNKS + 1) * CHUNK
SUB = 2 * CHUNK
BANDW = SUB + LEFT_CHUNKS * CHUNK
SKEW = 768
NEG = -1e30
RBP = 384
VMEM_LIMIT = 56 * 1024 * 1024
ADAM_LR, ADAM_B1, ADAM_B2, ADAM_EPS, ADAM_WD, ADAM_STEP = 0.001, 0.9, 0.999, 1e-08, 0.01, 10
MESH = pl.DeviceIdType.MESH
ANY = pl.BlockSpec(memory_space=pl.ANY)


def _params(*sem):
    kw = dict(vmem_limit_bytes=VMEM_LIMIT)
    if sem:
        kw["dimension_semantics"] = sem
    return pltpu.CompilerParams(**kw)


def _dot(a, b):
    return jnp.dot(a, b, preferred_element_type=F32)


def _dot_nt(a, b):
    return lax.dot_general(a, b, (((1,), (1,)), ((), ())), preferred_element_type=F32)


def _dot_tn(a, b):
    return lax.dot_general(a, b, (((0,), (0,)), ((), ())), preferred_element_type=F32)


def _rms(x, g):
    r = lax.rsqrt(jnp.mean(x * x, axis=-1, keepdims=True) + EPS)
    return x * r * g


def _rms_bwd(dy, x, g):
    r = lax.rsqrt(jnp.mean(x * x, axis=-1, keepdims=True) + EPS)
    xh = x * r
    dxh = dy * g
    dx = r * (dxh - xh * jnp.mean(dxh * xh, axis=-1, keepdims=True))
    return dx, jnp.sum(dy * xh, axis=0, keepdims=True)


def _full(shape):
    return pl.BlockSpec(shape, lambda *_: (0,) * len(shape))


def _acc_init(step, *refs):
    @pl.when(step == 0)
    def _():
        for r in refs:
            r[...] = jnp.zeros_like(r)


def _inproj_fwd(h, g, w, wkv_t):
    T, D = h.shape
    NQ = w.shape[1] - D
    NKV = wkv_t.shape[0]

    def body(h_ref, g_ref, w_ref, wt_ref, xn_ref, pa_ref, qkv_ref, kvt_ref):
        xn = _rms(h_ref[...], g_ref[...]).astype(BF16)
        xn_ref[...] = xn
        pa_ref[...] = _dot(xn, w_ref[:, :D])
        qkv_ref[...] = _dot(xn, w_ref[:, D:]).astype(BF16)
        kvt_ref[...] = _dot_nt(wt_ref[...], xn).astype(BF16)

    row = lambda n: pl.BlockSpec((TB, n), lambda i: (i, 0))
    return pl.pallas_call(
        body, name="inproj_fwd", grid=(T // TB,),
        in_specs=[row(D), _full((1, D)), _full(w.shape), _full(wkv_t.shape)],
        out_specs=[row(D), row(D), row(NQ), pl.BlockSpec((NKV, TB), lambda i: (0, i))],
        out_shape=[jax.ShapeDtypeStruct((T, D), BF16), jax.ShapeDtypeStruct((T, D), F32),
                   jax.ShapeDtypeStruct((T, NQ), BF16), jax.ShapeDtypeStruct((NKV, T), BF16)],
        compiler_params=_params("parallel"),
    )(h, g, w, wkv_t)


def _lane_groups(n, vals):
    lane = lax.broadcasted_iota(jnp.int32, (1, n), 1)
    q = n // 4
    return jnp.where(lane < q, vals[0], jnp.where(lane < 2 * q, vals[1], jnp.where(lane < 3 * q, vals[2], vals[3]))).astype(F32)


def _pick_group(levels, n):
    lane = lax.broadcasted_iota(jnp.int32, (1, n), 1)
    q = n // 4
    return jnp.where(lane < q, levels[0], jnp.where(lane < 2 * q, levels[1], jnp.where(lane < 3 * q, levels[2], levels[3])))


def _pool_count(blk, n):
    t1 = (blk * TB + 1 + lax.broadcasted_iota(jnp.int32, (TB, 1), 0)).astype(F32)
    return jnp.minimum(t1, _lane_groups(n, POOL_WINDOWS))


def _pool_d(pu, pu_halo, cnt):
    e = jnp.concatenate([pu_halo, pu], axis=0)
    s2 = e + pltpu.roll(e, 1, 0)
    s4 = s2 + pltpu.roll(s2, 2, 0)
    s8 = s4 + pltpu.roll(s4, 4, 0)
    s16 = s8 + pltpu.roll(s8, 8, 0)
    num = _pick_group([s2, s4, s8, s16], pu.shape[1])[HALO:]
    return num / cnt - pu


def _conv_taps(z, z_halo):
    e = jnp.concatenate([z_halo, z], axis=0)
    return pltpu.roll(e, 1, 0)[HALO:], pltpu.roll(e, 2, 0)[HALO:]


def _convpool_fwd(pa, cw, wbd, ps):
    T, D = pa.shape
    DC = D // 4

    def body(pa_ref, halo_ref, cw_ref, wbd_ref, ps_ref, yab_ref):
        i = pl.program_id(0)
        x = pa_ref[...]
        hl = jnp.where(i > 0, halo_ref[...], 0.0)
        gb, gc, u, pu = (x[:, k * DC:(k + 1) * DC] for k in range(4))
        z = gc * u
        z1, z2 = _conv_taps(z, hl[:, DC:2 * DC] * hl[:, 2 * DC:3 * DC])
        cwv = cw_ref[...]
        ya = gb * (cwv[2:3] * z + cwv[1:2] * z1 + cwv[0:1] * z2)
        d = _pool_d(pu, hl[:, 3 * DC:], _pool_count(i, DC))
        yb = _dot(d.astype(BF16), wbd_ref[...]) * ps_ref[...]
        yab_ref[...] = jnp.concatenate([ya, yb], axis=1)

    return pl.pallas_call(
        body, name="convpool_fwd", grid=(T // TB,),
        in_specs=[pl.BlockSpec((TB, D), lambda i: (i, 0)),
                  pl.BlockSpec((HALO, D), lambda i: (jnp.maximum(i * (TB // HALO) - 1, 0), 0)),
                  _full((8, DC)), _full((DC, DC)), _full((1, DC))],
        out_specs=pl.BlockSpec((TB, 2 * DC), lambda i: (i, 0)),
        out_shape=jax.ShapeDtypeStruct((T, 2 * DC), F32),
        compiler_params=_params("parallel"),
    )(pa, pa, cw, wbd, ps)


def _bias_bins(shape, col_dim):
    j = lax.broadcasted_iota(jnp.int32, shape, col_dim)
    b = lax.broadcasted_iota(jnp.int32, shape, 1 - col_dim)
    d = jnp.where(j < BAND, j, j - SKEW)
    live = jnp.logical_or(j < BAND, j > SKEW - CHUNK)
    bins = jnp.minimum(TB - d, REL_CLIP) + REL_CLIP
    return jnp.where(jnp.logical_and(live, bins == b), 1.0, 0.0).astype(F32)


def _bias_build(rb):
    H = rb.shape[0]

    def body(rb_ref, o_ref):
        v = jnp.dot(jnp.broadcast_to(rb_ref[0], (8, RBP)), _bias_bins((RBP, SKEW), 1),
                    precision=lax.Precision.HIGHEST, preferred_element_type=F32)
        x = jnp.broadcast_to(v[0:1], (SUB, SKEW))
        row = lax.broadcasted_iota(jnp.int32, (SUB, SKEW), 0)
        for b in range(SUB.bit_length() - 1):
            x = jnp.where(((row >> b) & 1) == 1, pltpu.roll(x, 1 << b, 1), x)
        qc = lax.broadcasted_iota(jnp.int32, (SUB, BANDW), 0) >> (CHUNK.bit_length() - 1)
        kc = lax.broadcasted_iota(jnp.int32, (SUB, BANDW), 1) >> (CHUNK.bit_length() - 1)
        o_ref[0] = jnp.where(jnp.logical_and(kc >= qc, kc <= qc + LEFT_CHUNKS), x[:, :BANDW], NEG).T

    return pl.pallas_call(
        body, name="bias_build", grid=(H,),
        in_specs=[pl.BlockSpec((1, 1, RBP), lambda h: (h, 0, 0))],
        out_specs=pl.BlockSpec((1, BANDW, SUB), lambda h: (h, 0, 0)),
        out_shape=jax.ShapeDtypeStruct((H, BANDW, SUB), F32),
        compiler_params=_params("parallel"),
    )(rb)


def _bias_fold(db):
    H = db.shape[0]

    def body(db_ref, o_ref):
        x = jnp.concatenate([db_ref[0].T, jnp.zeros((SUB, SKEW - BANDW), F32)], axis=1)
        row = lax.broadcasted_iota(jnp.int32, (SUB, SKEW), 0)
        for b in range(SUB.bit_length() - 1):
            x = jnp.where(((row >> b) & 1) == 1, pltpu.roll(x, SKEW - (1 << b), 1), x)
        dsum = jnp.sum(x, axis=0, keepdims=True)
        o_ref[0] = jnp.dot(jnp.broadcast_to(dsum, (8, SKEW)), _bias_bins((SKEW, RBP), 0),
                           precision=lax.Precision.HIGHEST, preferred_element_type=F32)

    return pl.pallas_call(
        body, name="bias_fold", grid=(H,),
        in_specs=[pl.BlockSpec((1, BANDW, SUB), lambda h: (h, 0, 0))],
        out_specs=pl.BlockSpec((1, 8, RBP), lambda h: (h, 0, 0)),
        out_shape=jax.ShapeDtypeStruct((H, 8, RBP), F32),
        compiler_params=_params("parallel"),
    )(db)


def _head_masks():
    lane = lax.broadcasted_iota(jnp.int32, (1, LANES), 1)
    sub = lax.broadcasted_iota(jnp.int32, (LANES, 1), 0)
    return (lane < HEAD_DIM, sub < HEAD_DIM), (lane >= HEAD_DIM, sub >= HEAD_DIM)


def _key_tiles(s, first_block):
    return [t for t in range(s, s + BANDW // SUB) if not (first_block and t < TB // SUB)]


def _attn_fwd(qkv, kv_t, bias_t):
    T = qkv.shape[0]
    NP = qkv.shape[1] // (3 * LANES)
    NB = T // TB
    NS = TB // SUB
    scale = HEAD_DIM ** -0.5

    def body(q_ref, kc_ref, kp_ref, vtc_ref, vtp_ref, b_ref, o_ref, lse_ref):
        i = pl.program_id(1)

        def compute(first_block):
            q = q_ref[...] * scale
            kwin = jnp.concatenate([kp_ref[...], kc_ref[...]], axis=0)
            vt = jnp.concatenate([vtp_ref[...], vtc_ref[...]], axis=1)
            for s in range(NS):
                rows = slice(s * SUB, (s + 1) * SUB)
                out_t = None
                for a, (lane_m, sub_m) in enumerate(_head_masks()):
                    qa = jnp.where(lane_m, q[rows], 0)
                    tiles = _key_tiles(s, first_block)
                    keys = slice(tiles[0] * SUB, (tiles[-1] + 1) * SUB)
                    st = _dot_nt(kwin[keys], qa) + b_ref[a, (tiles[0] - s) * SUB:(tiles[-1] - s + 1) * SUB, :]
                    m = jnp.max(st, axis=0, keepdims=True)
                    p = jnp.exp(st - m)
                    l = jnp.sum(p, axis=0, keepdims=True)
                    o_a = _dot(jnp.where(sub_m, vt[:, keys], 0), p.astype(BF16)) * (1.0 / l)
                    out_t = o_a if out_t is None else out_t + o_a
                    lse_ref[0, 0, a * NS + s:a * NS + s + 1, :] = m + jnp.log(l)
                o_ref[rows, :] = out_t.T

        pl.when(i == 0)(functools.partial(compute, True))
        pl.when(i > 0)(functools.partial(compute, False))

    prev = lambda i: jnp.maximum(i - 1, 0)
    return pl.pallas_call(
        body, name="attn_fwd", grid=(NP, NB),
        in_specs=[pl.BlockSpec((TB, LANES), lambda p, i: (i, p)),
                  pl.BlockSpec((TB, LANES), lambda p, i: (i, NP + p)),
                  pl.BlockSpec((TB, LANES), lambda p, i: (prev(i), NP + p)),
                  pl.BlockSpec((LANES, TB), lambda p, i: (NP + p, i)),
                  pl.BlockSpec((LANES, TB), lambda p, i: (NP + p, prev(i))),
                  pl.BlockSpec((2, BANDW, SUB), lambda p, i: (p, 0, 0))],
        out_specs=[pl.BlockSpec((TB, LANES), lambda p, i: (i, p)),
                   pl.BlockSpec((1, 1, 8, LANES), lambda p, i: (p, i, 0, 0))],
        out_shape=[jax.ShapeDtypeStruct((T, NP * LANES), F32), jax.ShapeDtypeStruct((NP, NB, 8, LANES), F32)],
        compiler_params=_params("parallel", "parallel"),
    )(qkv, qkv, qkv, kv_t, kv_t, bias_t)


def _group_bounds(D):
    return ((0, D // 4), (D // 4, D // 2), (D // 2, D))


def _mix_out_fwd(yab, yc, gg, w, h, g):
    T, D = h.shape

    def body(yab_ref, yc_ref, gg_ref, w_ref, h_ref, g_ref, y_ref, mix_ref, h1_ref):
        yraw = jnp.concatenate([yab_ref[...], yc_ref[...]], axis=1)
        ggv = gg_ref[...]
        y = jnp.concatenate([_rms(yraw[:, a:b], ggv[:, a:b]) for a, b in _group_bounds(D)], axis=1).astype(BF16)
        y_ref[...] = y
        mix = _dot(y, w_ref[...])
        mix_ref[...] = mix
        h1_ref[...] = h_ref[...] + _rms(mix, g_ref[...])

    row = lambda n: pl.BlockSpec((TB, n), lambda i: (i, 0))
    return pl.pallas_call(
        body, name="mix_out_fwd", grid=(T // TB,),
        in_specs=[row(D // 2), row(D // 2), _full((1, D)), _full((D, D)), row(D), _full((1, D))],
        out_specs=[row(D), row(D), row(D)],
        out_shape=[jax.ShapeDtypeStruct((T, D), BF16), jax.ShapeDtypeStruct((T, D), F32), jax.ShapeDtypeStruct((T, D), F32)],
        compiler_params=_params("parallel"),
    )(yab, yc, gg, w, h, g)


def _load_resident(step, w_hbm, w_vmem, sem):
    @pl.when(step == 0)
    def _():
        cp = pltpu.make_async_copy(w_hbm, w_vmem, sem)
        cp.start()
        cp.wait()


def _ffn_up_fwd(h1, g, w):
    T, D = h1.shape
    F2 = w.shape[1]
    DFF = F2 // 2
    TF = DFF // 2

    def body(h_ref, g_ref, w_hbm, hn_ref, gu_ref, ff_ref, w_v, sem):
        _load_resident(pl.program_id(0), w_hbm, w_v, sem)
        hn = _rms(h_ref[...], g_ref[...]).astype(BF16)
        hn_ref[...] = hn
        for j in range(2):
            sg, su = slice(j * TF, (j + 1) * TF), slice(DFF + j * TF, DFF + (j + 1) * TF)
            gate = _dot(hn, w_v[:, sg])
            up = _dot(hn, w_v[:, su])
            gu_ref[:, sg] = gate.astype(BF16)
            gu_ref[:, su] = up.astype(BF16)
            ff_ref[:, sg] = (gate * jax.nn.sigmoid(gate) * up).astype(BF16)

    row = lambda n: pl.BlockSpec((TBF, n), lambda i: (i, 0))
    return pl.pallas_call(
        body, name="ffn_up_fwd", grid=(T // TBF,),
        in_specs=[row(D), _full((1, D)), ANY],
        out_specs=[row(D), row(F2), row(DFF)],
        out_shape=[jax.ShapeDtypeStruct((T, D), BF16), jax.ShapeDtypeStruct((T, F2), BF16), jax.ShapeDtypeStruct((T, DFF), BF16)],
        scratch_shapes=[pltpu.VMEM(w.shape, BF16), pltpu.SemaphoreType.DMA],
        compiler_params=_params("arbitrary"),
    )(h1, g, w)


def _ffn_down_fwd(ff, w, h1, g):
    T, D = h1.shape
    DFF = ff.shape[1]

    def body(ff_ref, w_ref, h_ref, g_ref, ffo_ref, h2_ref):
        ffo = _dot(ff_ref[...], w_ref[...])
        ffo_ref[...] = ffo
        h2_ref[...] = h_ref[...] + _rms(ffo, g_ref[...])

    row = lambda n: pl.BlockSpec((TBF, n), lambda i: (i, 0))
    return pl.pallas_call(
        body, name="ffn_down_fwd", grid=(T // TBF,),
        in_specs=[row(DFF), _full((DFF, D)), row(D), _full((1, D))],
        out_specs=[row(D), row(D)],
        out_shape=[jax.ShapeDtypeStruct((T, D), F32), jax.ShapeDtypeStruct((T, D), F32)],
        compiler_params=_params("parallel"),
    )(ff, w, h1, g)


def _loss_grad(h, tgt):
    T, D = h.shape

    def body(h_ref, t_ref, dh_ref, loss_ref):
        _acc_init(pl.program_id(0), loss_ref)
        diff = h_ref[...] - t_ref[...]
        dh_ref[...] = diff * (1.0 / D)
        loss_ref[...] += 0.5 * jnp.sum(jnp.mean(diff * diff, axis=-1, keepdims=True))

    row = pl.BlockSpec((TB, D), lambda i: (i, 0))
    return pl.pallas_call(
        body, name="loss_grad", grid=(T // TB,),
        in_specs=[row, row], out_specs=[row, _full((8, LANES))],
        out_shape=[jax.ShapeDtypeStruct((T, D), F32), jax.ShapeDtypeStruct((8, LANES), F32)],
        compiler_params=_params("arbitrary"),
    )(h, tgt)


def _ffn_bwd(dh2, ffo, g_po, gu, w_dn, w_gu, h1, g_pf):
    T, D = dh2.shape
    F2 = gu.shape[1]
    DFF = F2 // 2
    TF = DFF // 2

    def body(dh_ref, ffo_ref, gpo_ref, gu_ref, wdn_hbm, wgu_hbm, h_ref, gpf_ref,
             dffo_ref, dgu_ref, dh1_ref, dgpo_ref, dgpf_ref, wdn_v, wgu_v, sems):
        step = pl.program_id(0)
        _load_resident(step, wdn_hbm, wdn_v, sems.at[0])
        _load_resident(step, wgu_hbm, wgu_v, sems.at[1])
        _acc_init(step, dgpo_ref, dgpf_ref)
        dh = dh_ref[...]
        dffo, dg = _rms_bwd(dh, ffo_ref[...], gpo_ref[...])
        dgpo_ref[0:1, :] += dg
        dffo = dffo.astype(BF16)
        dffo_ref[...] = dffo
        dhn = jnp.zeros((TBF, D), F32)
        for j in range(2):
            sg, su = slice(j * TF, (j + 1) * TF), slice(DFF + j * TF, DFF + (j + 1) * TF)
            dff = _dot_nt(dffo, wdn_v[sg, :])
            gate, up = gu_ref[:, sg].astype(F32), gu_ref[:, su].astype(F32)
            sig = jax.nn.sigmoid(gate)
            dgate = (dff * up * (sig * (1.0 + gate * (1.0 - sig)))).astype(BF16)
            dup = (dff * (gate * sig)).astype(BF16)
            dgu_ref[:, sg] = dgate
            dgu_ref[:, su] = dup
            dhn = dhn + _dot_nt(dgate, wgu_v[:, sg]) + _dot_nt(dup, wgu_v[:, su])
        dx, dg = _rms_bwd(dhn, h_ref[...], gpf_ref[...])
        dgpf_ref[0:1, :] += dg
        dh1_ref[...] = dh + dx

    row = lambda n: pl.BlockSpec((TBF, n), lambda i: (i, 0))
    return pl.pallas_call(
        body, name="ffn_bwd", grid=(T // TBF,),
        in_specs=[row(D), row(D), _full((1, D)), row(F2), ANY, ANY, row(D), _full((1, D))],
        out_specs=[row(D), row(F2), row(D), _full((8, D)), _full((8, D))],
        out_shape=[jax.ShapeDtypeStruct((T, D), BF16), jax.ShapeDtypeStruct((T, F2), BF16), jax.ShapeDtypeStruct((T, D), F32),
                   jax.ShapeDtypeStruct((8, D), F32), jax.ShapeDtypeStruct((8, D), F32)],
        scratch_shapes=[pltpu.VMEM(w_dn.shape, BF16), pltpu.VMEM(w_gu.shape, BF16), pltpu.SemaphoreType.DMA((2,))],
        compiler_params=_params("arbitrary"),
    )(dh2, ffo, g_po, gu, w_dn, w_gu, h1, g_pf)


def _mix_out_bwd(dh1, mix, g, w, yab, yc, gg):
    T, D = dh1.shape

    def body(dh_ref, mix_ref, g_ref, w_ref, yab_ref, yc_ref, gg_ref, dmix_ref, dyab_ref, dyc_ref, dg_ref, dgg_ref):
        _acc_init(pl.program_id(0), dg_ref, dgg_ref)
        dmix, dg = _rms_bwd(dh_ref[...], mix_ref[...], g_ref[...])
        dg_ref[0:1, :] += dg
        dmix = dmix.astype(BF16)
        dmix_ref[...] = dmix
        dy = _dot_nt(dmix, w_ref[...])
        yraw = jnp.concatenate([yab_ref[...], yc_ref[...]], axis=1)
        ggv = gg_ref[...]
        parts = [_rms_bwd(dy[:, a:b], yraw[:, a:b], ggv[:, a:b]) for a, b in _group_bounds(D)]
        dgg_ref[0:1, :] += jnp.concatenate([p[1] for p in parts], axis=1)
        dyab_ref[...] = jnp.concatenate([parts[0][0], parts[1][0]], axis=1)
        dyc_ref[...] = parts[2][0]

    row = lambda n: pl.BlockSpec((TB, n), lambda i: (i, 0))
    return pl.pallas_call(
        body, name="mix_out_bwd", grid=(T // TB,),
        in_specs=[row(D), row(D), _full((1, D)), _full((D, D)), row(D // 2), row(D // 2), _full((1, D))],
        out_specs=[row(D), row(D // 2), row(D // 2), _full((8, D)), _full((8, D))],
        out_shape=[jax.ShapeDtypeStruct((T, D), BF16), jax.ShapeDtypeStruct((T, D // 2), F32), jax.ShapeDtypeStruct((T, D // 2), F32),
                   jax.ShapeDtypeStruct((8, D), F32), jax.ShapeDtypeStruct((8, D), F32)],
        compiler_params=_params("arbitrary"),
    )(dh1, mix, g, w, yab, yc, gg)


def _attn_bwd(qkv, kv_t, dyc, yc, lse, bias_t):
    T = qkv.shape[0]
    NP = qkv.shape[1] // (3 * LANES)
    NB = T // TB
    NS = TB // SUB
    scale = HEAD_DIM ** -0.5

    def body(q_ref, kc_ref, kp_ref, vc_ref, vp_ref, ktc_ref, ktp_ref, do_ref, o_ref, lse_ref, b_ref,
             dq_ref, dk_ref, dv_ref, db_ref, dk_carry, dv_carry, dkw, dvw):
        step = pl.program_id(1)
        i = NB - 1 - step
        _acc_init(step, dk_carry, dv_carry, db_ref)

        def compute(first_block):
            q = q_ref[...] * scale
            kwin = jnp.concatenate([kp_ref[...], kc_ref[...]], axis=0)
            vwin = jnp.concatenate([vp_ref[...], vc_ref[...]], axis=0)
            kt = jnp.concatenate([ktp_ref[...], ktc_ref[...]], axis=1)
            do = do_ref[...]
            dob = do.astype(BF16)
            prod = do * o_ref[...]
            ones = jnp.ones((8, LANES), F32)
            heads = []
            for lane_m, sub_m in _head_masks():
                delta = lax.dot_general(ones, jnp.where(lane_m, prod, 0.0), (((1,), (1,)), ((), ())),
                                        precision=lax.Precision.HIGHEST, preferred_element_type=F32)
                heads.append((jnp.where(lane_m, q, 0), jnp.where(lane_m, dob, 0), delta, sub_m))
            written = set()
            for s in range(NS):
                rows = slice(s * SUB, (s + 1) * SUB)
                tiles = _key_tiles(s, first_block)
                keys = slice(tiles[0] * SUB, (tiles[-1] + 1) * SUB)
                brows = slice((tiles[0] - s) * SUB, (tiles[-1] - s + 1) * SUB)
                dk_c = dv_c = dqt = None
                for a, (qa, doa, delta, sub_m) in enumerate(heads):
                    st = _dot_nt(kwin[keys], qa[rows]) + b_ref[a, brows, :]
                    p = jnp.exp(st - lse_ref[0, 0, a * NS + s:a * NS + s + 1, :])
                    ds = p * (_dot_nt(vwin[keys], doa[rows]) - delta[0:1, rows])
                    db_ref[a, brows, :] += ds
                    dsb = ds.astype(BF16)
                    terms = (_dot(dsb, qa[rows]), _dot(p.astype(BF16), doa[rows]), _dot(jnp.where(sub_m, kt[:, keys], 0), dsb))
                    dk_c, dv_c, dqt = terms if a == 0 else (dk_c + terms[0], dv_c + terms[1], dqt + terms[2])
                dq_ref[rows, :] = (dqt.T * scale).astype(BF16)
                for n, t in enumerate(tiles):
                    win, loc = slice(t * SUB, (t + 1) * SUB), slice(n * SUB, (n + 1) * SUB)
                    if t in written:
                        dkw[win, :] += dk_c[loc]
                        dvw[win, :] += dv_c[loc]
                    else:
                        dkw[win, :] = dk_c[loc]
                        dvw[win, :] = dv_c[loc]
                        written.add(t)
            dk_ref[...] = (dkw[TB:, :] + dk_carry[...]).astype(BF16)
            dv_ref[...] = (dvw[TB:, :] + dv_carry[...]).astype(BF16)
            if not first_block:
                dk_carry[...] = dkw[:TB, :]
                dv_carry[...] = dvw[:TB, :]

        pl.when(i == 0)(functools.partial(compute, True))
        pl.when(i > 0)(functools.partial(compute, False))

    blk = lambda s: NB - 1 - s
    prev = lambda s: jnp.maximum(NB - 2 - s, 0)
    rows = lambda which, off: pl.BlockSpec((TB, LANES), lambda p, s: (which(s), off + p))
    out = jax.ShapeDtypeStruct((T, NP * LANES), BF16)
    strip = pl.BlockSpec((2, BANDW, SUB), lambda p, s: (p, 0, 0))
    return pl.pallas_call(
        body, name="attn_bwd", grid=(NP, NB),
        in_specs=[rows(blk, 0), rows(blk, NP), rows(prev, NP), rows(blk, 2 * NP), rows(prev, 2 * NP),
                  pl.BlockSpec((LANES, TB), lambda p, s: (p, blk(s))), pl.BlockSpec((LANES, TB), lambda p, s: (p, prev(s))),
                  rows(blk, 0), rows(blk, 0), pl.BlockSpec((1, 1, 8, LANES), lambda p, s: (p, blk(s), 0, 0)), strip],
        out_specs=[rows(blk, 0), rows(blk, 0), rows(blk, 0), strip],
        out_shape=[out, out, out, jax.ShapeDtypeStruct((2 * NP, BANDW, SUB), F32)],
        scratch_shapes=[pltpu.VMEM((TB, LANES), F32), pltpu.VMEM((TB, LANES), F32),
                        pltpu.VMEM((2 * TB, LANES), F32), pltpu.VMEM((2 * TB, LANES), F32)],
        compiler_params=_params("arbitrary", "arbitrary"),
    )(qkv, qkv, qkv, qkv, qkv, kv_t, kv_t, dyc, yc, lse, bias_t)


def _convpool_bwd(pa, dyab, cw, wbd, ps):
    T, D = pa.shape
    DC = D // 4
    NB = T // TB
    N = TB + HALO

    def body(pa_ref, halo_ref, dy_ref, cw_ref, wbd_ref, ps_ref, dpa_ref, dcw_ref, dwbd_ref, dps_ref, dc_carry, e_carry):
        step = pl.program_id(0)
        i = NB - 1 - step
        _acc_init(step, dcw_ref, dwbd_ref, dps_ref, dc_carry, e_carry)
        x = pa_ref[...]
        hl = jnp.where(i > 0, halo_ref[...], 0.0)
        gb, gc, u, pu = (x[:, k * DC:(k + 1) * DC] for k in range(4))
        dy = dy_ref[...]
        dya, dyb = dy[:, :DC], dy[:, DC:]
        cwv = cw_ref[...]
        z = gc * u
        z1, z2 = _conv_taps(z, hl[:, DC:2 * DC] * hl[:, 2 * DC:3 * DC])
        dgb = dya * (cwv[2:3] * z + cwv[1:2] * z1 + cwv[0:1] * z2)
        dconv = dya * gb
        dcw_ref[0:1, :] += jnp.sum(dconv * z2, axis=0, keepdims=True)
        dcw_ref[1:2, :] += jnp.sum(dconv * z1, axis=0, keepdims=True)
        dcw_ref[2:3, :] += jnp.sum(dconv * z, axis=0, keepdims=True)
        ext = jnp.concatenate([dconv, dc_carry[...]], axis=0)
        dz = cwv[2:3] * dconv + cwv[1:2] * pltpu.roll(ext, N - 1, 0)[:TB] + cwv[0:1] * pltpu.roll(ext, N - 2, 0)[:TB]
        dc_carry[...] = dconv[:HALO]
        cnt = _pool_count(i, DC)
        d = _pool_d(pu, hl[:, 3 * DC:], cnt).astype(BF16)
        psv = ps_ref[...]
        w = wbd_ref[...]
        dps_ref[0:1, :] += jnp.sum(dyb * _dot(d, w), axis=0, keepdims=True)
        dys = (dyb * psv).astype(BF16)
        dwbd_ref[...] += _dot_tn(d, dys)
        dd = _dot_nt(dys, w)
        e = dd / cnt
        ext = jnp.concatenate([e, e_carry[...]], axis=0)
        a2 = ext + pltpu.roll(ext, N - 1, 0)
        a4 = a2 + pltpu.roll(a2, N - 2, 0)
        a8 = a4 + pltpu.roll(a4, N - 4, 0)
        a16 = a8 + pltpu.roll(a8, N - 8, 0)
        dpu = _pick_group([a2, a4, a8, a16], DC)[:TB] - dd
        e_carry[...] = e[:HALO]
        dpa_ref[...] = jnp.concatenate([dgb, dz * u, dz * gc, dpu], axis=1).astype(BF16)

    blk = lambda n: pl.BlockSpec((TB, n), lambda s: (NB - 1 - s, 0))
    return pl.pallas_call(
        body, name="convpool_bwd", grid=(NB,),
        in_specs=[blk(D), pl.BlockSpec((HALO, D), lambda s: (jnp.maximum((NB - 1 - s) * (TB // HALO) - 1, 0), 0)),
                  blk(2 * DC), _full((8, DC)), _full((DC, DC)), _full((1, DC))],
        out_specs=[blk(D), _full((8, DC)), _full((DC, DC)), _full((8, DC))],
        out_shape=[jax.ShapeDtypeStruct((T, D), BF16), jax.ShapeDtypeStruct((8, DC), F32),
                   jax.ShapeDtypeStruct((DC, DC), F32), jax.ShapeDtypeStruct((8, DC), F32)],
        scratch_shapes=[pltpu.VMEM((HALO, DC), F32), pltpu.VMEM((HALO, DC), F32)],
        compiler_params=_params("arbitrary"),
    )(pa, pa, dyab, cw, wbd, ps)


def _inproj_bwd(dparts, w, h, g, dh1):
    T, D = h.shape
    widths = [p.shape[1] for p in dparts]
    n = len(dparts)

    def body(*refs):
        parts, (w_ref, h_ref, g_ref, dh1_ref, dh_ref, dg_ref) = refs[:n], refs[n:]
        _acc_init(pl.program_id(0), dg_ref)
        dxn, off = jnp.zeros((TB, D), F32), 0
        for p_ref, wd in zip(parts, widths):
            dxn = dxn + _dot_nt(p_ref[...], w_ref[:, off:off + wd])
            off += wd
        dx, dg = _rms_bwd(dxn, h_ref[...], g_ref[...])
        dg_ref[0:1, :] += dg
        dh_ref[...] = dh1_ref[...] + dx

    row = lambda m: pl.BlockSpec((TB, m), lambda i: (i, 0))
    return pl.pallas_call(
        body, name="inproj_bwd", grid=(T // TB,),
        in_specs=[row(wd) for wd in widths] + [_full(w.shape), row(D), _full((1, D)), row(D)],
        out_specs=[row(D), _full((8, D))],
        out_shape=[jax.ShapeDtypeStruct((T, D), F32), jax.ShapeDtypeStruct((8, D), F32)],
        compiler_params=_params("arbitrary"),
    )(*dparts, w, h, g, dh1)


def _wgrad(a, b, tk, tn, name):
    T, K = a.shape
    N = b.shape[1]

    def body(a_ref, b_ref, o_ref):
        _acc_init(pl.program_id(2), o_ref)
        o_ref[...] += _dot_tn(a_ref[...], b_ref[...])

    return pl.pallas_call(
        body, name=name, grid=(K // tk, N // tn, T // TB),
        in_specs=[pl.BlockSpec((TB, tk), lambda k, n, t: (t, k)), pl.BlockSpec((TB, tn), lambda k, n, t: (t, n))],
        out_specs=pl.BlockSpec((tk, tn), lambda k, n, t: (k, n)),
        out_shape=jax.ShapeDtypeStruct((K, N), F32),
        compiler_params=_params("parallel", "parallel", "arbitrary"),
    )(a, b)


def _wgrad_concat(a, bs, name):
    T, K = a.shape
    widths = [b.shape[1] for b in bs]
    n = len(bs)

    def body(*refs):
        a_ref, b_refs, o_ref = refs[0], refs[1:1 + n], refs[1 + n]
        _acc_init(pl.program_id(0), o_ref)
        av, off = a_ref[...], 0
        for b_ref, wd in zip(b_refs, widths):
            o_ref[:, off:off + wd] += _dot_tn(av, b_ref[...])
            off += wd

    row = lambda m: pl.BlockSpec((TB, m), lambda t: (t, 0))
    return pl.pallas_call(
        body, name=name, grid=(T // TB,),
        in_specs=[row(K)] + [row(wd) for wd in widths],
        out_specs=_full((K, sum(widths))),
        out_shape=jax.ShapeDtypeStruct((K, sum(widths)), F32),
        compiler_params=_params("arbitrary"),
    )(a, *bs)


def _adamw(w, g, m, v, tr, name):
    R, C = w.shape

    def body(w_ref, g_ref, m_ref, v_ref, d_ref, nm_ref, nv_ref):
        gv = g_ref[...]
        nm = ADAM_B1 * m_ref[...] + (1.0 - ADAM_B1) * gv
        nv = ADAM_B2 * v_ref[...] + (1.0 - ADAM_B2) * (gv * gv)
        m_hat = nm / (1.0 - ADAM_B1 ** ADAM_STEP)
        v_hat = nv / (1.0 - ADAM_B2 ** ADAM_STEP)
        d_ref[...] = -ADAM_LR * (m_hat / (jnp.sqrt(v_hat) + ADAM_EPS) + ADAM_WD * w_ref[...])
        nm_ref[...] = nm
        nv_ref[...] = nv

    blk = pl.BlockSpec((tr, C), lambda i: (i, 0))
    out = jax.ShapeDtypeStruct((R, C), F32)
    return pl.pallas_call(
        body, name=name, grid=(R // tr,), in_specs=[blk] * 4, out_specs=[blk] * 3, out_shape=[out] * 3,
        compiler_params=_params("parallel"),
    )(w, g, m, v)


def _sum_slots(x, tr, name):
    n, R, C = x.shape

    def body(x_ref, o_ref):
        acc = x_ref[0]
        for k in range(1, n):
            acc = acc + x_ref[k]
        o_ref[...] = acc

    return pl.pallas_call(
        body, name=name, grid=(R // tr,),
        in_specs=[pl.BlockSpec((n, tr, C), lambda i: (0, i, 0))],
        out_specs=pl.BlockSpec((tr, C), lambda i: (i, 0)),
        out_shape=jax.ShapeDtypeStruct((R, C), F32),
        compiler_params=_params("parallel"),
    )(x)


def _add_half(view, other, c, name):
    if view.ndim == 3:
        _, R2, N = view.shape
        tr = 128
        grid = (R2 // tr,)
        in_specs = [pl.BlockSpec((1, tr, N), lambda i, c_ref: (c_ref[0], i, 0)), pl.BlockSpec((1, tr, N), lambda i, c_ref: (0, i, 0))]
        out_spec = pl.BlockSpec((tr, N), lambda i, c_ref: (i, 0))
        out_shape = jax.ShapeDtypeStruct((R2, N), F32)

        def body(c_ref, a_ref, b_ref, o_ref):
            o_ref[...] = a_ref[0] + b_ref[0]
    else:
        S, _, R2, C = view.shape
        grid = (S,)
        in_specs = [pl.BlockSpec((1, 1, R2, C), lambda s, c_ref: (s, c_ref[0], 0, 0)), pl.BlockSpec((1, 1, R2, C), lambda s, c_ref: (s, 0, 0, 0))]
        out_spec = pl.BlockSpec((1, R2, C), lambda s, c_ref: (s, 0, 0))
        out_shape = jax.ShapeDtypeStruct((S, R2, C), F32)

        def body(c_ref, a_ref, b_ref, o_ref):
            o_ref[0] = a_ref[0, 0] + b_ref[0, 0]

    return pl.pallas_call(
        body, name=name,
        grid_spec=pltpu.PrefetchScalarGridSpec(num_scalar_prefetch=1, grid=grid, in_specs=in_specs, out_specs=out_spec),
        out_shape=out_shape, compiler_params=_params("parallel"),
    )(c, view, other)


def _sum_chips(slots, part, kind, place, name):
    _, R2, C = slots.shape
    tr = min(R2, 128) if kind == "col" else R2

    def body(p_ref, s_ref, own_ref, o_ref):
        me = p_ref[0]
        own = own_ref[...] if kind == "col" else own_ref[0]
        acc = None
        for k in range(4):
            term = jnp.where(me == k, own, s_ref[k])
            acc = term if acc is None else acc + term
        o_ref[0] = acc

    own_spec = (pl.BlockSpec((tr, C), lambda i, p: (i, p[0])) if kind == "col"
                else pl.BlockSpec((1, R2, C), lambda i, p: (p[0], 0, 0)))
    return pl.pallas_call(
        body, name=name,
        grid_spec=pltpu.PrefetchScalarGridSpec(
            num_scalar_prefetch=1, grid=(R2 // tr,),
            in_specs=[pl.BlockSpec((4, tr, C), lambda i, p: (0, i, 0)), own_spec],
            out_specs=pl.BlockSpec((1, tr, C), lambda i, p: (p[1], i, 0))),
        out_shape=jax.ShapeDtypeStruct((2, R2, C), F32), compiler_params=_params("parallel"),
    )(place, slots, part)


def _place():
    x, y, c = lax.axis_index("x"), lax.axis_index("y"), lax.axis_index("c")
    chips = [(1 - x, y), (x, 1 - y), (1 - x, 1 - y)]
    return x, y, c, 2 * x + y, chips


def _remote(src, dst, send_sems, recv_sems, k, to):
    return pltpu.make_async_remote_copy(src_ref=src, dst_ref=dst, send_sem=send_sems.at[k], recv_sem=recv_sems.at[k],
                                        device_id=to, device_id_type=MESH)


def _full_shape(shard_shape, kind):
    *lead, R, C = shard_shape
    return (*lead, R, 4 * C) if kind == "col" else (*lead, 4 * R, C)


def _slab(full_ref, shard_rc, kind, chip, half=None):
    R, C = shard_rc
    lead = (slice(None),) * (len(full_ref.shape) - 2)
    if kind == "col":
        rows = pl.ds(0, R) if half is None else pl.ds(half * (R // 2), R // 2)
        return full_ref.at[(*lead, rows, pl.ds(chip * C, C))]
    rows = pl.ds(chip * R, R) if half is None else pl.ds(chip * R + half * (R // 2), R // 2)
    return full_ref.at[(*lead, rows, slice(None))]


def _row_half(ref, half):
    R = ref.shape[-2]
    lead = (slice(None),) * (len(ref.shape) - 2)
    return ref.at[(*lead, pl.ds(half * (R // 2), R // 2), slice(None))]


def _allgather_weights(shards, kinds, after):
    n = len(shards)
    full_shapes = [_full_shape(s.shape, k) for s, k in zip(shards, kinds)]

    def body(*refs):
        ins, outs, (send_sems, recv_sems) = refs[:n], refs[n + 1:2 * n + 1], refs[2 * n + 1:]
        x, y, c, me, chips = _place()
        slab = lambda a, chip, half=None: _slab(outs[a], shards[a].shape[-2:], kinds[a], chip, half)
        my_half = lambda a: _row_half(ins[a], c)
        own = [_remote(ins[a], slab(a, me), send_sems, recv_sems, 6 * n + a, (x, y, 1 - c)) for a in range(n)]
        first = [_remote(my_half(a), slab(a, me, c), send_sems, recv_sems, j * n + a, (*chip, c))
                 for j, chip in enumerate(chips) for a in range(n)]
        for cp in first + own:
            cp.start()
        passed = []
        for j, chip in enumerate(chips):
            src = 2 * chip[0] + chip[1]
            for a in range(n):
                _remote(my_half(a), slab(a, src, c), send_sems, recv_sems, j * n + a, (x, y, c)).wait_recv()
                cp = _remote(slab(a, src, c), slab(a, src, c), send_sems, recv_sems, (3 + j) * n + a, (x, y, 1 - c))
                cp.start()
                passed.append(cp)
        for j, chip in enumerate(chips):
            src = 2 * chip[0] + chip[1]
            for a in range(n):
                _remote(my_half(a), slab(a, src, 1 - c), send_sems, recv_sems, (3 + j) * n + a, (x, y, c)).wait_recv()
        for cp in own:
            cp.wait_recv()
        for cp in first + passed + own:
            cp.wait_send()

    return pl.pallas_call(
        body, name="allgather_weights",
        in_specs=[ANY] * (n + 1), out_specs=[ANY] * n,
        out_shape=[jax.ShapeDtypeStruct(s, BF16) for s in full_shapes],
        scratch_shapes=[pltpu.SemaphoreType.DMA((7 * n,)), pltpu.SemaphoreType.DMA((7 * n,))],
        compiler_params=pltpu.CompilerParams(has_side_effects=True),
    )(*shards, after)


HBM = pl.BlockSpec(memory_space=pltpu.HBM)
SEM = pl.BlockSpec(memory_space=pltpu.SEMAPHORE)
DATAFLOW = pltpu.SideEffectType.DATAFLOW_SIDE_EFFECTING


def _split_start(name, bufs, plan, ncopies, after=None):
    nb = len(bufs)
    nin = nb + (after is not None)

    def body(*refs):
        send_sems, recv_sems, token = refs[nin], refs[nin + 1], refs[-1]
        for k, (src, dst, to) in enumerate(plan(refs[:nb])):
            _remote(src, dst, send_sems, recv_sems, k, to).start()
        token[...] = jnp.zeros_like(token)

    out = pl.pallas_call(
        body, name=name,
        out_shape=(pltpu.SemaphoreType.DMA((ncopies,)), pltpu.SemaphoreType.DMA((ncopies,)),
                   *[pltpu.HBM(b.shape, b.dtype) for b in bufs], jax.ShapeDtypeStruct((8, LANES), F32)),
        in_specs=[HBM] * nb + [ANY] * (nin - nb), out_specs=(SEM, SEM, *[HBM] * nb, pl.BlockSpec(memory_space=pltpu.VMEM)),
        input_output_aliases={i: 2 + i for i in range(nb)},
        compiler_params=pltpu.CompilerParams(has_side_effects=DATAFLOW),
    )(*[pltpu.with_memory_space_constraint(b, pltpu.HBM) for b in bufs], *([] if after is None else [after]))
    return out[0], out[1], list(out[2:2 + nb]), out[-1]


def _split_wait(name, send_sems, recv_sems, bufs, plan, after):
    nb = len(bufs)

    def body(*refs):
        s_sems, r_sems = refs[nb], refs[nb + 1]
        for k, (src, dst, to) in enumerate(plan(refs[:nb])):
            cp = _remote(src, dst, s_sems, r_sems, k, to)
            cp.wait_send()
            cp.wait_recv()

    return pl.pallas_call(
        body, name=name,
        out_shape=tuple(pltpu.HBM(b.shape, b.dtype) for b in bufs),
        in_specs=[HBM] * nb + [SEM, SEM, ANY], out_specs=tuple([HBM] * nb),
        input_output_aliases={i: i for i in range(nb)},
        compiler_params=pltpu.CompilerParams(has_side_effects=DATAFLOW),
    )(*bufs, send_sems, recv_sems, after)


def _gather_plan(n, shard_rcs, kinds):
    def plan(refs):
        x, y, c, me, chips = _place()
        out = []
        for a in range(n):
            shard, full = refs[a], refs[n + a]
            out.append((shard, _slab(full, shard_rcs[a], kinds[a], me), (x, y, 1 - c)))
            for chip in chips:
                for cc in (c, 1 - c):
                    out.append((_row_half(shard, c), _slab(full, shard_rcs[a], kinds[a], me, c), (*chip, cc)))
        return out
    return plan


def _exchange_plan(n, kinds):
    def plan(refs):
        x, y, c, me, chips = _place()
        out = []
        for a in range(n):
            part, slots = refs[a], refs[n + a]
            C = slots.shape[2]
            for chip in chips:
                dst = 2 * chip[0] + chip[1]
                src = part.at[:, pl.ds(dst * C, C)] if kinds[a] == "col" else part.at[dst]
                out.append((src, slots.at[me], (*chip, c)))
        return out
    return plan


def _swap_halves(views):
    n = len(views)
    out_shapes = [(1,) + v.shape[1:] if v.ndim == 3 else (v.shape[0], 1) + v.shape[2:] for v in views]

    def body(*refs):
        ins, outs, (send_sems, recv_sems) = refs[:n], refs[n:2 * n], refs[2 * n:]
        x, y, c, _, _ = _place()
        cps = []
        for a in range(n):
            src = ins[a].at[pl.ds(1 - c, 1)] if views[a].ndim == 3 else ins[a].at[:, pl.ds(1 - c, 1)]
            cps.append(_remote(src, outs[a], send_sems, recv_sems, a, (x, y, 1 - c)))
        for cp in cps:
            cp.start()
        for cp in cps:
            cp.wait()

    return pl.pallas_call(
        body, name="grad_swap_halves", in_specs=[ANY] * n, out_specs=[ANY] * n,
        out_shape=[jax.ShapeDtypeStruct(s, F32) for s in out_shapes],
        scratch_shapes=[pltpu.SemaphoreType.DMA((n,)), pltpu.SemaphoreType.DMA((n,))],
        compiler_params=pltpu.CompilerParams(has_side_effects=True),
    )(*views)


def _join_halves(bufs):
    n = len(bufs)

    def body(*refs):
        outs, (send_sems, recv_sems) = refs[n:2 * n], refs[2 * n:]
        x, y, c, _, _ = _place()
        sends = [_remote(outs[a].at[c], outs[a].at[c], send_sems, recv_sems, a, (x, y, 1 - c)) for a in range(n)]
        for cp in sends:
            cp.start()
        for a in range(n):
            _remote(outs[a].at[c], outs[a].at[1 - c], send_sems, recv_sems, a, (x, y, c)).wait_recv()
        for cp in sends:
            cp.wait_send()

    return pl.pallas_call(
        body, name="grad_join_halves", in_specs=[ANY] * n, out_specs=[ANY] * n,
        out_shape=[jax.ShapeDtypeStruct(b.shape, F32) for b in bufs],
        input_output_aliases={a: a for a in range(n)},
        scratch_shapes=[pltpu.SemaphoreType.DMA((n,)), pltpu.SemaphoreType.DMA((n,))],
        compiler_params=pltpu.CompilerParams(has_side_effects=True),
    )(*bufs)


def _allgather_small(block):
    M, N = block.shape

    def body(x_ref, out_ref, send_sems, recv_sems, local_sem):
        x, y, c, _, chips = _place()
        me, sibling = (x, y, c), (x, y, 1 - c)

        def rows(px, py, pc):
            return out_ref.at[pl.ds((4 * px + 2 * py + pc) * M, M), :]

        def copy(k, blk, to, src=None):
            return _remote(rows(*blk) if src is None else src, rows(*blk), send_sems, recv_sems, k, to)

        mine = pltpu.make_async_copy(x_ref, rows(*me), local_sem)
        mine.start()
        first = [copy(0, me, sibling, src=x_ref)] + [copy(1 + j, me, (*chip, c), src=x_ref) for j, chip in enumerate(chips)]
        for cp in first:
            cp.start()
        passed = [copy(4 + j, (*chip, c), sibling) for j, chip in enumerate(chips)]
        for j, chip in enumerate(chips):
            copy(1 + j, (*chip, c), me).wait_recv()
            passed[j].start()
        copy(0, sibling, me).wait_recv()
        for j, chip in enumerate(chips):
            copy(4 + j, (*chip, 1 - c), me).wait_recv()
        for cp in first + passed:
            cp.wait_send()
        mine.wait()

    vm = pl.BlockSpec(memory_space=pltpu.VMEM)
    return pl.pallas_call(
        body, name="allgather_small", in_specs=[vm], out_specs=vm,
        out_shape=jax.ShapeDtypeStruct((8 * M, N), F32),
        scratch_shapes=[pltpu.SemaphoreType.DMA((7,)), pltpu.SemaphoreType.DMA((7,)), pltpu.SemaphoreType.DMA],
        compiler_params=pltpu.CompilerParams(has_side_effects=True, vmem_limit_bytes=VMEM_LIMIT),
    )(block)


def _pack(arrays):
    flat = jnp.concatenate([a.reshape(-1) for a in arrays])
    pad = (-flat.shape[0]) % (8 * LANES)
    return jnp.pad(flat, (0, pad)).reshape(-1, LANES)


def _unpack(packed, shapes):
    flat, out, off = packed.reshape(-1), [], 0
    for s in shapes:
        size = 1
        for d in s:
            size *= d
        out.append(flat[off:off + size].reshape(s))
        off += size
    return out


def _block_diag(pw):
    G, n, _ = pw.shape
    eye = jnp.eye(G, dtype=pw.dtype)
    return (eye[:, None, :, None] * pw[:, :, None, :]).reshape(G * n, G * n)


def _diag_blocks(m, G):
    n = m.shape[0] // G
    return jnp.stack([m[g * n:(g + 1) * n, g * n:(g + 1) * n] for g in range(G)])


def _pad_rows(a, rows):
    return jnp.pad(a, ((0, rows - a.shape[0]), (0, 0)))


def kernel(x, w_in, w_out, conv_w, pool_w, pool_scale, rel_bias, group_gain, pre_mix_g, post_mix_g, pre_ffn_g, post_ffn_g, w_gate_up, w_down, loss_target, m_w_in, m_w_out, m_conv_w, m_pool_w, m_pool_scale, m_rel_bias, m_group_gain, m_pre_mix_g, m_post_mix_g, m_pre_ffn_g, m_post_ffn_g, m_w_gate_up, m_w_down, v_w_in, v_w_out, v_conv_w, v_pool_w, v_pool_scale, v_rel_bias, v_group_gain, v_pre_mix_g, v_post_mix_g, v_pre_ffn_g, v_post_ffn_g, v_w_gate_up, v_w_down):
    L = w_in.shape[0]
    T, D = x.shape[1], x.shape[2]
    DC = D // 4
    NH = rel_bias.shape[1]
    NREL = rel_bias.shape[2]
    G = pool_w.shape[1]
    cs = conv_w.shape[2]
    assert TB == LEFT_CHUNKS * CHUNK and T % TB == 0 and D % (4 * LANES) == 0 and NH * HEAD_DIM == D // 2
    xi, yi, ci = lax.axis_index("x"), lax.axis_index("y"), lax.axis_index("c")
    chip = 2 * xi + yi

    kinds = ("col", "row", "col", "row")
    big = (w_in, w_out, w_gate_up, w_down)
    rcs = [w.shape[-2:] for w in big]
    gplan = _gather_plan(4, rcs, kinds)
    ncopy_gather = 4 * 7
    conv_gathered = _allgather_small(_pack([conv_w]))
    conv_all = conv_gathered.reshape(8, -1)[:, :L * 3 * cs].reshape(4, 2, L, 3, cs)[:, 0]
    conv_full = jnp.moveaxis(conv_all, 0, 2).reshape(L, 3, 4 * cs)
    full = [[f[0] for f in _allgather_weights([w[0:1].astype(BF16) for w in big], kinds, conv_gathered)]]

    h = x[0]
    saved = []
    for l in range(L):
        g_pre, g_pm, g_pf, g_po = (a[l][None] for a in (pre_mix_g, post_mix_g, pre_ffn_g, post_ffn_g))
        if l + 1 < L:
            bufs = [w[l + 1].astype(BF16) for w in big] + [lax.empty(_full_shape(rc, k), BF16) for rc, k in zip(rcs, kinds)]
            ssem, rsem, thru, token = _split_start(f"gather_start_{l + 1}", bufs, gplan, ncopy_gather, after=full[l][0])
            g_pre = g_pre + token[0:1, 0:1]
        gg, ps = group_gain[l][None], pool_scale[l][None]
        cw = _pad_rows(conv_full[l], 8)
        wbd = _block_diag(pool_w[l]).astype(BF16)
        bias = _bias_build(jnp.pad(rel_bias[l], ((0, 0), (0, RBP - NREL)))[:, None, :])
        wi, wo, wgu, wdn = full[l]
        xn, pa, qkv, kv_t = _inproj_fwd(h, g_pre, wi, wi[:, D + D // 2:].T)
        yab = _convpool_fwd(pa, cw, wbd, ps)
        yc, lse = _attn_fwd(qkv, kv_t, bias)
        y, mix, h1 = _mix_out_fwd(yab, yc, gg, wo, h, g_pm)
        hn, gu, ff = _ffn_up_fwd(h1, g_pf, wgu)
        ffo, h2 = _ffn_down_fwd(ff, wdn, h1, g_po)
        saved.append(dict(h=h, xn=xn, pa=pa, qkv=qkv, kv_t=kv_t, lse=lse, yab=yab, yc=yc, y=y, mix=mix, h1=h1, hn=hn, gu=gu, ff=ff, ffo=ffo,
                          cw=cw, wbd=wbd, bias=bias, ps=ps, gg=gg, g_pre=g_pre, g_pm=g_pm, g_pf=g_pf, g_po=g_po))
        h = h2
        if l + 1 < L:
            full.append(list(_split_wait(f"gather_wait_{l + 1}", ssem, rsem, thru, gplan, h2)[4:8]))

    dh, loss_tile = _loss_grad(h, loss_target[0])
    loss = lax.psum(loss_tile[0, 0], ("x", "y", "c"))

    xplan = _exchange_plan(4, kinds)
    place = jnp.stack([chip, ci]).astype(jnp.int32)
    cvec = ci.reshape(1).astype(jnp.int32)
    small_grads = [None] * L
    shard_grads = [None] * L

    def finish_exchange(pending, after):
        lp, ssem, rsem, thru = pending
        landed = _split_wait(f"grad_exchange_wait_{lp}", ssem, rsem, thru, xplan, after)
        bufs = [_sum_chips(landed[4 + a], landed[a], kinds[a], place, "grad_sum_chips") for a in range(4)]
        shard_grads[lp] = [j.reshape(2 * j.shape[1], j.shape[2]) for j in _join_halves(bufs)]

    pending = None
    token = None
    for l in reversed(range(L)):
        s = saved[l]
        wi, wo, wgu, wdn = full[l]
        g_po = s["g_po"] if token is None else s["g_po"] + token[0:1, 0:1]
        dffo, dgu, dh1, dg_po, dg_pf = _ffn_bwd(dh, s["ffo"], g_po, s["gu"], wdn, wgu, s["h1"], s["g_pf"])
        dmix, dyab, dyc, dg_pm, dgg = _mix_out_bwd(dh1, s["mix"], s["g_pm"], wo, s["yab"], s["yc"], s["gg"])
        dq, dk, dv, dbias = _attn_bwd(s["qkv"], s["kv_t"], dyc, s["yc"], s["lse"], s["bias"])
        dpa, dcw, dwbd, dps = _convpool_bwd(s["pa"], dyab, s["cw"], s["wbd"], s["ps"])
        dparts = [dpa, dq, dk, dv]
        dh, dg_pre = _inproj_bwd(dparts, wi, s["h"], s["g_pre"], dh1)
        F2, DFF = s["gu"].shape[1], s["ff"].shape[1]
        grads = [_wgrad_concat(s["xn"], dparts, "wgrad_in"),
                 _wgrad(s["y"], dmix, D, D, "wgrad_out"),
                 _wgrad(s["hn"], dgu, D, F2 // 4, "wgrad_gate_up"),
                 _wgrad(s["ff"], dffo, DFF // 2, D, "wgrad_down")]
        drb = _bias_fold(dbias)[:, 0, :NREL]
        small_grads[l] = [dcw[:3], _diag_blocks(dwbd, G), dps[0], drb, dgg[0], dg_pre[0], dg_pm[0], dg_pf[0], dg_po[0]]
        views = [g.reshape(2, g.shape[0] // 2, g.shape[1]) if k == "col" else g.reshape(4, 2, g.shape[0] // 8, g.shape[1])
                 for g, k in zip(grads, kinds)]
        theirs = _swap_halves(views)
        chip_sums = [_add_half(v, t, cvec, "grad_add_half") for v, t in zip(views, theirs)]
        slots = [lax.empty((4, p.shape[0], p.shape[1] // 4) if k == "col" else p.shape, F32) for p, k in zip(chip_sums, kinds)]
        ssem, rsem, thru, token = _split_start(f"grad_exchange_start_{l}", chip_sums + slots, xplan, 4 * 3)
        if pending is not None:
            finish_exchange(pending, dh)
        pending = (l, ssem, rsem, thru)
    finish_exchange(pending, dh)
    g_big = [jnp.stack([shard_grads[l][k] for l in range(L)]) for k in range(4)]

    names_shapes = [(L, 3, 4 * cs), pool_w.shape, pool_scale.shape, rel_bias.shape, group_gain.shape,
                    pre_mix_g.shape, post_mix_g.shape, pre_ffn_g.shape, post_ffn_g.shape]
    small_stacked = [jnp.stack([small_grads[l][k] for l in range(L)]) for k in range(len(names_shapes))]
    packed = _pack(small_stacked)
    M = packed.shape[0]
    total = _sum_slots(_allgather_small(packed).reshape(8, M, LANES), M, "small_sum_devices")
    g_small = _unpack(total, names_shapes)
    g_small[0] = lax.dynamic_slice_in_dim(g_small[0], chip * cs, cs, axis=2)

    def adam_big(w, g, m, v, name):
        shp = w.shape
        two = lambda a: a.reshape(shp[0] * shp[1], shp[2])
        return [o.reshape(shp) for o in _adamw(two(w), two(g), two(m), two(v), 256, name)]

    upd_in = adam_big(w_in, g_big[0], m_w_in, v_w_in, "adamw_in")
    upd_out = adam_big(w_out, g_big[1], m_w_out, v_w_out, "adamw_out")
    upd_gu = adam_big(w_gate_up, g_big[2], m_w_gate_up, v_w_gate_up, "adamw_gate_up")
    upd_dn = adam_big(w_down, g_big[3], m_w_down, v_w_down, "adamw_down")

    small_w = [conv_w, pool_w, pool_scale, rel_bias, group_gain, pre_mix_g, post_mix_g, pre_ffn_g, post_ffn_g]
    small_m = [m_conv_w, m_pool_w, m_pool_scale, m_rel_bias, m_group_gain, m_pre_mix_g, m_post_mix_g, m_pre_ffn_g, m_post_ffn_g]
    small_v = [v_conv_w, v_pool_w, v_pool_scale, v_rel_bias, v_group_gain, v_pre_mix_g, v_post_mix_g, v_pre_ffn_g, v_post_ffn_g]
    pw_, pg_, pm_, pv_ = _pack(small_w), _pack(g_small), _pack(small_m), _pack(small_v)
    shapes = [w.shape for w in small_w]
    upd_small = [_unpack(o, shapes) for o in _adamw(pw_, pg_, pm_, pv_, pw_.shape[0], "adamw_small")]

    def ordered(big4, small9):
        return [big4[0], big4[1], *small9, big4[2], big4[3]]

    grads = ordered(g_big, g_small)
    outs = [ordered([upd_in[k], upd_out[k], upd_gu[k], upd_dn[k]], upd_small[k]) for k in range(3)]
    return (loss, dh[None], *grads, *outs[0], *outs[1], *outs[2])
```

```python
import functools

import jax
import jax.numpy as jnp
from jax import lax
from jax.experimental import pallas as pl
from jax.experimental.pallas import tpu as pltpu

F32, BF16 = jnp.float32, jnp.bfloat16
EPS = 1e-6
CHUNK = 64
LEFT_CHUNKS = 8
REL_CLIP = 128
HEAD_DIM = 64
LANES = 128
POOL_WINDOWS = (2, 4, 8, 16)
HALO = 16
TB = LEFT_CHUNKS * CHUNK
TBF = 256
BAND = (LEFT_CHUNKS + 1) * CHUNK
SUB = 2 * CHUNK
BANDW = SUB + LEFT_CHUNKS * CHUNK
SKEW = 768
NEG = -1e30
RBP = 384
VMEM_LIMIT = 56 * 1024 * 1024
ADAM_LR, ADAM_B1, ADAM_B2, ADAM_EPS, ADAM_WD, ADAM_STEP = 0.001, 0.9, 0.999, 1e-08, 0.01, 10
MESH = pl.DeviceIdType.MESH
ANY = pl.BlockSpec(memory_space=pl.ANY)


def _params(*sem):
    kw = dict(vmem_limit_bytes=VMEM_LIMIT)
    if sem:
        kw["dimension_semantics"] = sem
    return pltpu.CompilerParams(**kw)


def _dot(a, b):
    return jnp.dot(a, b, preferred_element_type=F32)


def _dot_nt(a, b):
    return lax.dot_general(a, b, (((1,), (1,)), ((), ())), preferred_element_type=F32)


def _dot_tn(a, b):
    return lax.dot_general(a, b, (((0,), (0,)), ((), ())), preferred_element_type=F32)


def _rms(x, g):
    r = lax.rsqrt(jnp.mean(x * x, axis=-1, keepdims=True) + EPS)
    return x * r * g


def _rms_bwd(dy, x, g):
    r = lax.rsqrt(jnp.mean(x * x, axis=-1, keepdims=True) + EPS)
    xh = x * r
    dxh = dy * g
    dx = r * (dxh - xh * jnp.mean(dxh * xh, axis=-1, keepdims=True))
    return dx, jnp.sum(dy * xh, axis=0, keepdims=True)


def _full(shape):
    return pl.BlockSpec(shape, lambda *_: (0,) * len(shape))


def _acc_init(step, *refs):
    @pl.when(step == 0)
    def _():
        for r in refs:
            r[...] = jnp.zeros_like(r)


def _inproj_fwd(h, g, w, wkv_t):
    T, D = h.shape
    NQ = w.shape[1] - D
    NKV = wkv_t.shape[0]

    def body(h_ref, g_ref, w_ref, wt_ref, xn_ref, pa_ref, qkv_ref, kvt_ref):
        xn = _rms(h_ref[...], g_ref[...]).astype(BF16)
        xn_ref[...] = xn
        pa_ref[...] = _dot(xn, w_ref[:, :D])
        qkv_ref[...] = _dot(xn, w_ref[:, D:]).astype(BF16)
        kvt_ref[...] = _dot_nt(wt_ref[...], xn).astype(BF16)

    row = lambda n: pl.BlockSpec((TB, n), lambda i: (i, 0))
    return pl.pallas_call(
        body, name="inproj_fwd", grid=(T // TB,),
        in_specs=[row(D), _full((1, D)), _full(w.shape), _full(wkv_t.shape)],
        out_specs=[row(D), row(D), row(NQ), pl.BlockSpec((NKV, TB), lambda i: (0, i))],
        out_shape=[jax.ShapeDtypeStruct((T, D), BF16), jax.ShapeDtypeStruct((T, D), F32),
                   jax.ShapeDtypeStruct((T, NQ), BF16), jax.ShapeDtypeStruct((NKV, T), BF16)],
        compiler_params=_params("parallel"),
    )(h, g, w, wkv_t)


def _lane_groups(n, vals):
    lane = lax.broadcasted_iota(jnp.int32, (1, n), 1)
    q = n // 4
    return jnp.where(lane < q, vals[0], jnp.where(lane < 2 * q, vals[1], jnp.where(lane < 3 * q, vals[2], vals[3]))).astype(F32)


def _pick_group(levels, n):
    lane = lax.broadcasted_iota(jnp.int32, (1, n), 1)
    q = n // 4
    return jnp.where(lane < q, levels[0], jnp.where(lane < 2 * q, levels[1], jnp.where(lane < 3 * q, levels[2], levels[3])))


def _pool_count(blk, n):
    t1 = (blk * TB + 1 + lax.broadcasted_iota(jnp.int32, (TB, 1), 0)).astype(F32)
    return jnp.minimum(t1, _lane_groups(n, POOL_WINDOWS))


def _pool_d(pu, pu_halo, cnt):
    e = jnp.concatenate([pu_halo, pu], axis=0)
    s2 = e + pltpu.roll(e, 1, 0)
    s4 = s2 + pltpu.roll(s2, 2, 0)
    s8 = s4 + pltpu.roll(s4, 4, 0)
    s16 = s8 + pltpu.roll(s8, 8, 0)
    num = _pick_group([s2, s4, s8, s16], pu.shape[1])[HALO:]
    return num / cnt - pu


def _conv_taps(z, z_halo):
    e = jnp.concatenate([z_halo, z], axis=0)
    return pltpu.roll(e, 1, 0)[HALO:], pltpu.roll(e, 2, 0)[HALO:]


def _convpool_fwd(pa, cw, wbd, ps):
    T, D = pa.shape
    DC = D // 4

    def body(pa_ref, halo_ref, cw_ref, wbd_ref, ps_ref, yab_ref):
        i = pl.program_id(0)
        x = pa_ref[...]
        hl = jnp.where(i > 0, halo_ref[...], 0.0)
        gb, gc, u, pu = (x[:, k * DC:(k + 1) * DC] for k in range(4))
        z = gc * u
        z1, z2 = _conv_taps(z, hl[:, DC:2 * DC] * hl[:, 2 * DC:3 * DC])
        cwv = cw_ref[...]
        ya = gb * (cwv[2:3] * z + cwv[1:2] * z1 + cwv[0:1] * z2)
        d = _pool_d(pu, hl[:, 3 * DC:], _pool_count(i, DC))
        yb = _dot(d.astype(BF16), wbd_ref[...]) * ps_ref[...]
        yab_ref[...] = jnp.concatenate([ya, yb], axis=1)

    return pl.pallas_call(
        body, name="convpool_fwd", grid=(T // TB,),
        in_specs=[pl.BlockSpec((TB, D), lambda i: (i, 0)),
                  pl.BlockSpec((HALO, D), lambda i: (jnp.maximum(i * (TB // HALO) - 1, 0), 0)),
                  _full((8, DC)), _full((DC, DC)), _full((1, DC))],
        out_specs=pl.BlockSpec((TB, 2 * DC), lambda i: (i, 0)),
        out_shape=jax.ShapeDtypeStruct((T, 2 * DC), F32),
        compiler_params=_params("parallel"),
    )(pa, pa, cw, wbd, ps)


def _bias_bins(shape, col_dim):
    j = lax.broadcasted_iota(jnp.int32, shape, col_dim)
    b = lax.broadcasted_iota(jnp.int32, shape, 1 - col_dim)
    d = jnp.where(j < BAND, j, j - SKEW)
    live = jnp.logical_or(j < BAND, j > SKEW - CHUNK)
    bins = jnp.minimum(TB - d, REL_CLIP) + REL_CLIP
    return jnp.where(jnp.logical_and(live, bins == b), 1.0, 0.0).astype(F32)


def _bias_build(rb):
    H = rb.shape[0]

    def body(rb_ref, o_ref):
        v = jnp.dot(jnp.broadcast_to(rb_ref[0], (8, RBP)), _bias_bins((RBP, SKEW), 1),
                    precision=lax.Precision.HIGHEST, preferred_element_type=F32)
        x = jnp.broadcast_to(v[0:1], (SUB, SKEW))
        row = lax.broadcasted_iota(jnp.int32, (SUB, SKEW), 0)
        for b in range(SUB.bit_length() - 1):
            x = jnp.where(((row >> b) & 1) == 1, pltpu.roll(x, 1 << b, 1), x)
        qc = lax.broadcasted_iota(jnp.int32, (SUB, BANDW), 0) >> (CHUNK.bit_length() - 1)
        kc = lax.broadcasted_iota(jnp.int32, (SUB, BANDW), 1) >> (CHUNK.bit_length() - 1)
        o_ref[0] = jnp.where(jnp.logical_and(kc >= qc, kc <= qc + LEFT_CHUNKS), x[:, :BANDW], NEG).T

    return pl.pallas_call(
        body, name="bias_build", grid=(H,),
        in_specs=[pl.BlockSpec((1, 1, RBP), lambda h: (h, 0, 0))],
        out_specs=pl.BlockSpec((1, BANDW, SUB), lambda h: (h, 0, 0)),
        out_shape=jax.ShapeDtypeStruct((H, BANDW, SUB), F32),
        compiler_params=_params("parallel"),
    )(rb)


def _bias_fold(db, after):
    H = db.shape[0]

    def body(db_ref, after_ref, o_ref):
        x = jnp.concatenate([db_ref[0].T, jnp.zeros((SUB, SKEW - BANDW), F32)], axis=1)
        row = lax.broadcasted_iota(jnp.int32, (SUB, SKEW), 0)
        for b in range(SUB.bit_length() - 1):
            x = jnp.where(((row >> b) & 1) == 1, pltpu.roll(x, SKEW - (1 << b), 1), x)
        dsum = jnp.sum(x, axis=0, keepdims=True)
        o_ref[0] = jnp.dot(jnp.broadcast_to(dsum, (8, SKEW)), _bias_bins((SKEW, RBP), 0),
                           precision=lax.Precision.HIGHEST, preferred_element_type=F32)

    return pl.pallas_call(
        body, name="bias_fold", grid=(H,),
        in_specs=[pl.BlockSpec((1, BANDW, SUB), lambda h: (h, 0, 0)), ANY],
        out_specs=pl.BlockSpec((1, 8, RBP), lambda h: (h, 0, 0)),
        out_shape=jax.ShapeDtypeStruct((H, 8, RBP), F32),
        compiler_params=_params("parallel"),
    )(db, after)


def _head_masks():
    lane = lax.broadcasted_iota(jnp.int32, (1, LANES), 1)
    sub = lax.broadcasted_iota(jnp.int32, (LANES, 1), 0)
    return (lane < HEAD_DIM, sub < HEAD_DIM), (lane >= HEAD_DIM, sub >= HEAD_DIM)


def _key_tiles(s, first_block):
    return [t for t in range(s, s + BANDW // SUB) if not (first_block and t < TB // SUB)]


def _attn_fwd(qkv, kv_t, bias_t):
    T = qkv.shape[0]
    NP = qkv.shape[1] // (3 * LANES)
    NB = T // TB
    NS = TB // SUB
    scale = HEAD_DIM ** -0.5

    def body(q_ref, kc_ref, kp_ref, vtc_ref, vtp_ref, b_ref, o_ref, lse_ref):
        i = pl.program_id(1)

        def compute(first_block):
            q = q_ref[...] * scale
            kwin = jnp.concatenate([kp_ref[...], kc_ref[...]], axis=0)
            vt = jnp.concatenate([vtp_ref[...], vtc_ref[...]], axis=1)
            for s in range(NS):
                rows = slice(s * SUB, (s + 1) * SUB)
                out_t = None
                for a, (lane_m, sub_m) in enumerate(_head_masks()):
                    qa = jnp.where(lane_m, q[rows], 0)
                    tiles = _key_tiles(s, first_block)
                    keys = slice(tiles[0] * SUB, (tiles[-1] + 1) * SUB)
                    st = _dot_nt(kwin[keys], qa) + b_ref[a, (tiles[0] - s) * SUB:(tiles[-1] - s + 1) * SUB, :]
                    m = jnp.max(st, axis=0, keepdims=True)
                    p = jnp.exp(st - m)
                    l = jnp.sum(p, axis=0, keepdims=True)
                    o_a = _dot(jnp.where(sub_m, vt[:, keys], 0), p.astype(BF16)) * (1.0 / l)
                    out_t = o_a if out_t is None else out_t + o_a
                    lse_ref[0, 0, a * NS + s:a * NS + s + 1, :] = m + jnp.log(l)
                o_ref[rows, :] = out_t.T

        pl.when(i == 0)(functools.partial(compute, True))
        pl.when(i > 0)(functools.partial(compute, False))

    prev = lambda i: jnp.maximum(i - 1, 0)
    return pl.pallas_call(
        body, name="attn_fwd", grid=(NP, NB),
        in_specs=[pl.BlockSpec((TB, LANES), lambda p, i: (i, p)),
                  pl.BlockSpec((TB, LANES), lambda p, i: (i, NP + p)),
                  pl.BlockSpec((TB, LANES), lambda p, i: (prev(i), NP + p)),
                  pl.BlockSpec((LANES, TB), lambda p, i: (NP + p, i)),
                  pl.BlockSpec((LANES, TB), lambda p, i: (NP + p, prev(i))),
                  pl.BlockSpec((2, BANDW, SUB), lambda p, i: (p, 0, 0))],
        out_specs=[pl.BlockSpec((TB, LANES), lambda p, i: (i, p)),
                   pl.BlockSpec((1, 1, 8, LANES), lambda p, i: (p, i, 0, 0))],
        out_shape=[jax.ShapeDtypeStruct((T, NP * LANES), F32), jax.ShapeDtypeStruct((NP, NB, 8, LANES), F32)],
        compiler_params=_params("parallel", "parallel"),
    )(qkv, qkv, qkv, kv_t, kv_t, bias_t)


def _group_bounds(D):
    return ((0, D // 4), (D // 4, D // 2), (D // 2, D))


def _mix_out_fwd(yab, yc, gg, w, h, g):
    T, D = h.shape

    def body(yab_ref, yc_ref, gg_ref, w_ref, h_ref, g_ref, y_ref, mix_ref, h1_ref):
        yraw = jnp.concatenate([yab_ref[...], yc_ref[...]], axis=1)
        ggv = gg_ref[...]
        y = jnp.concatenate([_rms(yraw[:, a:b], ggv[:, a:b]) for a, b in _group_bounds(D)], axis=1).astype(BF16)
        y_ref[...] = y
        mix = _dot(y, w_ref[...])
        mix_ref[...] = mix
        h1_ref[...] = h_ref[...] + _rms(mix, g_ref[...])

    row = lambda n: pl.BlockSpec((TB, n), lambda i: (i, 0))
    return pl.pallas_call(
        body, name="mix_out_fwd", grid=(T // TB,),
        in_specs=[row(D // 2), row(D // 2), _full((1, D)), _full((D, D)), row(D), _full((1, D))],
        out_specs=[row(D), row(D), row(D)],
        out_shape=[jax.ShapeDtypeStruct((T, D), BF16), jax.ShapeDtypeStruct((T, D), F32), jax.ShapeDtypeStruct((T, D), F32)],
        compiler_params=_params("parallel"),
    )(yab, yc, gg, w, h, g)


def _load_resident(step, w_hbm, w_vmem, sem):
    @pl.when(step == 0)
    def _():
        cp = pltpu.make_async_copy(w_hbm, w_vmem, sem)
        cp.start()
        cp.wait()


def _ffn_up_fwd(h1, g, w):
    T, D = h1.shape
    F2 = w.shape[1]
    DFF = F2 // 2
    TF = DFF // 2

    def body(h_ref, g_ref, w_hbm, hn_ref, gu_ref, ff_ref, w_v, sem):
        _load_resident(pl.program_id(0), w_hbm, w_v, sem)
        hn = _rms(h_ref[...], g_ref[...]).astype(BF16)
        hn_ref[...] = hn
        for j in range(2):
            sg, su = slice(j * TF, (j + 1) * TF), slice(DFF + j * TF, DFF + (j + 1) * TF)
            gate = _dot(hn, w_v[:, sg])
            up = _dot(hn, w_v[:, su])
            gu_ref[:, sg] = gate.astype(BF16)
            gu_ref[:, su] = up.astype(BF16)
            ff_ref[:, sg] = (gate * jax.nn.sigmoid(gate) * up).astype(BF16)

    row = lambda n: pl.BlockSpec((TBF, n), lambda i: (i, 0))
    return pl.pallas_call(
        body, name="ffn_up_fwd", grid=(T // TBF,),
        in_specs=[row(D), _full((1, D)), ANY],
        out_specs=[row(D), row(F2), row(DFF)],
        out_shape=[jax.ShapeDtypeStruct((T, D), BF16), jax.ShapeDtypeStruct((T, F2), BF16), jax.ShapeDtypeStruct((T, DFF), BF16)],
        scratch_shapes=[pltpu.VMEM(w.shape, BF16), pltpu.SemaphoreType.DMA],
        compiler_params=_params("arbitrary"),
    )(h1, g, w)


def _ffn_down_fwd(ff, w, h1, g):
    T, D = h1.shape
    DFF = ff.shape[1]

    def body(ff_ref, w_ref, h_ref, g_ref, ffo_ref, h2_ref):
        ffo = _dot(ff_ref[...], w_ref[...])
        ffo_ref[...] = ffo
        h2_ref[...] = h_ref[...] + _rms(ffo, g_ref[...])

    row = lambda n: pl.BlockSpec((TBF, n), lambda i: (i, 0))
    return pl.pallas_call(
        body, name="ffn_down_fwd", grid=(T // TBF,),
        in_specs=[row(DFF), _full((DFF, D)), row(D), _full((1, D))],
        out_specs=[row(D), row(D)],
        out_shape=[jax.ShapeDtypeStruct((T, D), F32), jax.ShapeDtypeStruct((T, D), F32)],
        compiler_params=_params("parallel"),
    )(ff, w, h1, g)


def _loss_grad(h, tgt):
    T, D = h.shape

    def body(h_ref, t_ref, dh_ref, loss_ref):
        _acc_init(pl.program_id(0), loss_ref)
        diff = h_ref[...] - t_ref[...]
        dh_ref[...] = diff * (1.0 / D)
        loss_ref[...] += 0.5 * jnp.sum(jnp.mean(diff * diff, axis=-1, keepdims=True))

    row = pl.BlockSpec((TB, D), lambda i: (i, 0))
    return pl.pallas_call(
        body, name="loss_grad", grid=(T // TB,),
        in_specs=[row, row], out_specs=[row, _full((8, LANES))],
        out_shape=[jax.ShapeDtypeStruct((T, D), F32), jax.ShapeDtypeStruct((8, LANES), F32)],
        compiler_params=_params("arbitrary"),
    )(h, tgt)


def _ffn_bwd(dh2, ffo, g_po, gu, w_dn, w_gu, h1, g_pf):
    T, D = dh2.shape
    F2 = gu.shape[1]
    DFF = F2 // 2
    TF = DFF // 2

    def body(dh_ref, ffo_ref, gpo_ref, gu_ref, wdn_hbm, wgu_hbm, h_ref, gpf_ref,
             dffo_ref, dgu_ref, dh1_ref, dgpo_ref, dgpf_ref, wdn_v, wgu_v, sems):
        step = pl.program_id(0)
        _load_resident(step, wdn_hbm, wdn_v, sems.at[0])
        _load_resident(step, wgu_hbm, wgu_v, sems.at[1])
        _acc_init(step, dgpo_ref, dgpf_ref)
        dh = dh_ref[...]
        dffo, dg = _rms_bwd(dh, ffo_ref[...], gpo_ref[...])
        dgpo_ref[0:1, :] += dg
        dffo = dffo.astype(BF16)
        dffo_ref[...] = dffo
        dhn = jnp.zeros((TBF, D), F32)
        for j in range(2):
            sg, su = slice(j * TF, (j + 1) * TF), slice(DFF + j * TF, DFF + (j + 1) * TF)
            dff = _dot_nt(dffo, wdn_v[sg, :])
            gate, up = gu_ref[:, sg].astype(F32), gu_ref[:, su].astype(F32)
            sig = jax.nn.sigmoid(gate)
            dgate = (dff * up * (sig * (1.0 + gate * (1.0 - sig)))).astype(BF16)
            dup = (dff * (gate * sig)).astype(BF16)
            dgu_ref[:, sg] = dgate
            dgu_ref[:, su] = dup
            dhn = dhn + _dot_nt(dgate, wgu_v[:, sg]) + _dot_nt(dup, wgu_v[:, su])
        dx, dg = _rms_bwd(dhn, h_ref[...], gpf_ref[...])
        dgpf_ref[0:1, :] += dg
        dh1_ref[...] = dh + dx

    row = lambda n: pl.BlockSpec((TBF, n), lambda i: (i, 0))
    return pl.pallas_call(
        body, name="ffn_bwd", grid=(T // TBF,),
        in_specs=[row(D), row(D), _full((1, D)), row(F2), ANY, ANY, row(D), _full((1, D))],
        out_specs=[row(D), row(F2), row(D), _full((8, D)), _full((8, D))],
        out_shape=[jax.ShapeDtypeStruct((T, D), BF16), jax.ShapeDtypeStruct((T, F2), BF16), jax.ShapeDtypeStruct((T, D), F32),
                   jax.ShapeDtypeStruct((8, D), F32), jax.ShapeDtypeStruct((8, D), F32)],
        scratch_shapes=[pltpu.VMEM(w_dn.shape, BF16), pltpu.VMEM(w_gu.shape, BF16), pltpu.SemaphoreType.DMA((2,))],
        compiler_params=_params("arbitrary"),
    )(dh2, ffo, g_po, gu, w_dn, w_gu, h1, g_pf)


def _mix_out_bwd(dh1, mix, g, w, yab, yc, gg):
    T, D = dh1.shape

    def body(dh_ref, mix_ref, g_ref, w_ref, yab_ref, yc_ref, gg_ref, dmix_ref, dyab_ref, dyc_ref, dg_ref, dgg_ref):
        _acc_init(pl.program_id(0), dg_ref, dgg_ref)
        dmix, dg = _rms_bwd(dh_ref[...], mix_ref[...], g_ref[...])
        dg_ref[0:1, :] += dg
        dmix = dmix.astype(BF16)
        dmix_ref[...] = dmix
        dy = _dot_nt(dmix, w_ref[...])
        yraw = jnp.concatenate([yab_ref[...], yc_ref[...]], axis=1)
        ggv = gg_ref[...]
        parts = [_rms_bwd(dy[:, a:b], yraw[:, a:b], ggv[:, a:b]) for a, b in _group_bounds(D)]
        dgg_ref[0:1, :] += jnp.concatenate([p[1] for p in parts], axis=1)
        dyab_ref[...] = jnp.concatenate([parts[0][0], parts[1][0]], axis=1)
        dyc_ref[...] = parts[2][0]

    row = lambda n: pl.BlockSpec((TB, n), lambda i: (i, 0))
    return pl.pallas_call(
        body, name="mix_out_bwd", grid=(T // TB,),
        in_specs=[row(D), row(D), _full((1, D)), _full((D, D)), row(D // 2), row(D // 2), _full((1, D))],
        out_specs=[row(D), row(D // 2), row(D // 2), _full((8, D)), _full((8, D))],
        out_shape=[jax.ShapeDtypeStruct((T, D), BF16), jax.ShapeDtypeStruct((T, D // 2), F32), jax.ShapeDtypeStruct((T, D // 2), F32),
                   jax.ShapeDtypeStruct((8, D), F32), jax.ShapeDtypeStruct((8, D), F32)],
        compiler_params=_params("arbitrary"),
    )(dh1, mix, g, w, yab, yc, gg)


def _attn_bwd(qkv, kv_t, dyc, yc, lse, bias_t):
    T = qkv.shape[0]
    NP = qkv.shape[1] // (3 * LANES)
    NB = T // TB
    NS = TB // SUB
    scale = HEAD_DIM ** -0.5

    def body(q_ref, kc_ref, kp_ref, vc_ref, vp_ref, ktc_ref, ktp_ref, do_ref, o_ref, lse_ref, b_ref,
             dq_ref, dk_ref, dv_ref, db_ref, dk_carry, dv_carry, dkw, dvw):
        step = pl.program_id(1)
        i = NB - 1 - step
        _acc_init(step, dk_carry, dv_carry, db_ref)

        def compute(first_block):
            q = q_ref[...] * scale
            kwin = jnp.concatenate([kp_ref[...], kc_ref[...]], axis=0)
            vwin = jnp.concatenate([vp_ref[...], vc_ref[...]], axis=0)
            kt = jnp.concatenate([ktp_ref[...], ktc_ref[...]], axis=1)
            do = do_ref[...]
            dob = do.astype(BF16)
            prod = do * o_ref[...]
            ones = jnp.ones((8, LANES), F32)
            heads = []
            for lane_m, sub_m in _head_masks():
                delta = lax.dot_general(ones, jnp.where(lane_m, prod, 0.0), (((1,), (1,)), ((), ())),
                                        precision=lax.Precision.HIGHEST, preferred_element_type=F32)
                heads.append((jnp.where(lane_m, q, 0), jnp.where(lane_m, dob, 0), delta, sub_m))
            written = set()
            for s in range(NS):
                rows = slice(s * SUB, (s + 1) * SUB)
                tiles = _key_tiles(s, first_block)
                keys = slice(tiles[0] * SUB, (tiles[-1] + 1) * SUB)
                brows = slice((tiles[0] - s) * SUB, (tiles[-1] - s + 1) * SUB)
                dk_c = dv_c = dqt = None
                for a, (qa, doa, delta, sub_m) in enumerate(heads):
                    st = _dot_nt(kwin[keys], qa[rows]) + b_ref[a, brows, :]
                    p = jnp.exp(st - lse_ref[0, 0, a * NS + s:a * NS + s + 1, :])
                    ds = p * (_dot_nt(vwin[keys], doa[rows]) - delta[0:1, rows])
                    db_ref[a, brows, :] += ds
                    dsb = ds.astype(BF16)
                    terms = (_dot(dsb, qa[rows]), _dot(p.astype(BF16), doa[rows]), _dot(jnp.where(sub_m, kt[:, keys], 0), dsb))
                    dk_c, dv_c, dqt = terms if a == 0 else (dk_c + terms[0], dv_c + terms[1], dqt + terms[2])
                dq_ref[rows, :] = (dqt.T * scale).astype(BF16)
                for n, t in enumerate(tiles):
                    win, loc = slice(t * SUB, (t + 1) * SUB), slice(n * SUB, (n + 1) * SUB)
                    if t in written:
                        dkw[win, :] += dk_c[loc]
                        dvw[win, :] += dv_c[loc]
                    else:
                        dkw[win, :] = dk_c[loc]
                        dvw[win, :] = dv_c[loc]
                        written.add(t)
            dk_ref[...] = (dkw[TB:, :] + dk_carry[...]).astype(BF16)
            dv_ref[...] = (dvw[TB:, :] + dv_carry[...]).astype(BF16)
            if not first_block:
                dk_carry[...] = dkw[:TB, :]
                dv_carry[...] = dvw[:TB, :]

        pl.when(i == 0)(functools.partial(compute, True))
        pl.when(i > 0)(functools.partial(compute, False))

    blk = lambda s: NB - 1 - s
    prev = lambda s: jnp.maximum(NB - 2 - s, 0)
    rows = lambda which, off: pl.BlockSpec((TB, LANES), lambda p, s: (which(s), off + p))
    out = jax.ShapeDtypeStruct((T, NP * LANES), BF16)
    strip = pl.BlockSpec((2, BANDW, SUB), lambda p, s: (p, 0, 0))
    return pl.pallas_call(
        body, name="attn_bwd", grid=(NP, NB),
        in_specs=[rows(blk, 0), rows(blk, NP), rows(prev, NP), rows(blk, 2 * NP), rows(prev, 2 * NP),
                  pl.BlockSpec((LANES, TB), lambda p, s: (p, blk(s))), pl.BlockSpec((LANES, TB), lambda p, s: (p, prev(s))),
                  rows(blk, 0), rows(blk, 0), pl.BlockSpec((1, 1, 8, LANES), lambda p, s: (p, blk(s), 0, 0)), strip],
        out_specs=[rows(blk, 0), rows(blk, 0), rows(blk, 0), strip],
        out_shape=[out, out, out, jax.ShapeDtypeStruct((2 * NP, BANDW, SUB), F32)],
        scratch_shapes=[pltpu.VMEM((TB, LANES), F32), pltpu.VMEM((TB, LANES), F32),
                        pltpu.VMEM((2 * TB, LANES), F32), pltpu.VMEM((2 * TB, LANES), F32)],
        compiler_params=_params("arbitrary", "arbitrary"),
    )(qkv, qkv, qkv, qkv, qkv, kv_t, kv_t, dyc, yc, lse, bias_t)


def _convpool_bwd(pa, dyab, cw, wbd, ps):
    T, D = pa.shape
    DC = D // 4
    NB = T // TB
    N = TB + HALO

    def body(pa_ref, halo_ref, dy_ref, cw_ref, wbd_ref, ps_ref, dpa_ref, dcw_ref, dwbd_ref, dps_ref, dc_carry, e_carry):
        step = pl.program_id(0)
        i = NB - 1 - step
        _acc_init(step, dcw_ref, dwbd_ref, dps_ref, dc_carry, e_carry)
        x = pa_ref[...]
        hl = jnp.where(i > 0, halo_ref[...], 0.0)
        gb, gc, u, pu = (x[:, k * DC:(k + 1) * DC] for k in range(4))
        dy = dy_ref[...]
        dya, dyb = dy[:, :DC], dy[:, DC:]
        cwv = cw_ref[...]
        z = gc * u
        z1, z2 = _conv_taps(z, hl[:, DC:2 * DC] * hl[:, 2 * DC:3 * DC])
        dgb = dya * (cwv[2:3] * z + cwv[1:2] * z1 + cwv[0:1] * z2)
        dconv = dya * gb
        dcw_ref[0:1, :] += jnp.sum(dconv * z2, axis=0, keepdims=True)
        dcw_ref[1:2, :] += jnp.sum(dconv * z1, axis=0, keepdims=True)
        dcw_ref[2:3, :] += jnp.sum(dconv * z, axis=0, keepdims=True)
        ext = jnp.concatenate([dconv, dc_carry[...]], axis=0)
        dz = cwv[2:3] * dconv + cwv[1:2] * pltpu.roll(ext, N - 1, 0)[:TB] + cwv[0:1] * pltpu.roll(ext, N - 2, 0)[:TB]
        dc_carry[...] = dconv[:HALO]
        cnt = _pool_count(i, DC)
        d = _pool_d(pu, hl[:, 3 * DC:], cnt).astype(BF16)
        psv = ps_ref[...]
        w = wbd_ref[...]
        dps_ref[0:1, :] += jnp.sum(dyb * _dot(d, w), axis=0, keepdims=True)
        dys = (dyb * psv).astype(BF16)
        dwbd_ref[...] += _dot_tn(d, dys)
        dd = _dot_nt(dys, w)
        e = dd / cnt
        ext = jnp.concatenate([e, e_carry[...]], axis=0)
        a2 = ext + pltpu.roll(ext, N - 1, 0)
        a4 = a2 + pltpu.roll(a2, N - 2, 0)
        a8 = a4 + pltpu.roll(a4, N - 4, 0)
        a16 = a8 + pltpu.roll(a8, N - 8, 0)
        dpu = _pick_group([a2, a4, a8, a16], DC)[:TB] - dd
        e_carry[...] = e[:HALO]
        dpa_ref[...] = jnp.concatenate([dgb, dz * u, dz * gc, dpu], axis=1).astype(BF16)

    blk = lambda n: pl.BlockSpec((TB, n), lambda s: (NB - 1 - s, 0))
    return pl.pallas_call(
        body, name="convpool_bwd", grid=(NB,),
        in_specs=[blk(D), pl.BlockSpec((HALO, D), lambda s: (jnp.maximum((NB - 1 - s) * (TB // HALO) - 1, 0), 0)),
                  blk(2 * DC), _full((8, DC)), _full((DC, DC)), _full((1, DC))],
        out_specs=[blk(D), _full((8, DC)), _full((DC, DC)), _full((8, DC))],
        out_shape=[jax.ShapeDtypeStruct((T, D), BF16), jax.ShapeDtypeStruct((8, DC), F32),
                   jax.ShapeDtypeStruct((DC, DC), F32), jax.ShapeDtypeStruct((8, DC), F32)],
        scratch_shapes=[pltpu.VMEM((HALO, DC), F32), pltpu.VMEM((HALO, DC), F32)],
        compiler_params=_params("arbitrary"),
    )(pa, pa, dyab, cw, wbd, ps)


def _inproj_bwd(dparts, w, h, g, dh1):
    T, D = h.shape
    widths = [p.shape[1] for p in dparts]
    n = len(dparts)

    def body(*refs):
        parts, (w_ref, h_ref, g_ref, dh1_ref, dh_ref, dg_ref) = refs[:n], refs[n:]
        _acc_init(pl.program_id(0), dg_ref)
        dxn, off = jnp.zeros((TB, D), F32), 0
        for p_ref, wd in zip(parts, widths):
            dxn = dxn + _dot_nt(p_ref[...], w_ref[:, off:off + wd])
            off += wd
        dx, dg = _rms_bwd(dxn, h_ref[...], g_ref[...])
        dg_ref[0:1, :] += dg
        dh_ref[...] = dh1_ref[...] + dx

    row = lambda m: pl.BlockSpec((TB, m), lambda i: (i, 0))
    return pl.pallas_call(
        body, name="inproj_bwd", grid=(T // TB,),
        in_specs=[row(wd) for wd in widths] + [_full(w.shape), row(D), _full((1, D)), row(D)],
        out_specs=[row(D), _full((8, D))],
        out_shape=[jax.ShapeDtypeStruct((T, D), F32), jax.ShapeDtypeStruct((8, D), F32)],
        compiler_params=_params("arbitrary"),
    )(*dparts, w, h, g, dh1)


def _wgrad(a, b, tk, tn, name):
    T, K = a.shape
    N = b.shape[1]

    def body(a_ref, b_ref, o_ref):
        _acc_init(pl.program_id(2), o_ref)
        o_ref[...] += _dot_tn(a_ref[...], b_ref[...])

    return pl.pallas_call(
        body, name=name, grid=(K // tk, N // tn, T // TB),
        in_specs=[pl.BlockSpec((TB, tk), lambda k, n, t: (t, k)), pl.BlockSpec((TB, tn), lambda k, n, t: (t, n))],
        out_specs=pl.BlockSpec((tk, tn), lambda k, n, t: (k, n)),
        out_shape=jax.ShapeDtypeStruct((K, N), F32),
        compiler_params=_params("parallel", "parallel", "arbitrary"),
    )(a, b)


def _wgrad_concat(a, bs, name):
    T, K = a.shape
    widths = [b.shape[1] for b in bs]
    n = len(bs)

    def body(*refs):
        a_ref, b_refs, o_ref = refs[0], refs[1:1 + n], refs[1 + n]
        _acc_init(pl.program_id(0), o_ref)
        av, off = a_ref[...], 0
        for b_ref, wd in zip(b_refs, widths):
            o_ref[:, off:off + wd] += _dot_tn(av, b_ref[...])
            off += wd

    row = lambda m: pl.BlockSpec((TB, m), lambda t: (t, 0))
    return pl.pallas_call(
        body, name=name, grid=(T // TB,),
        in_specs=[row(K)] + [row(wd) for wd in widths],
        out_specs=_full((K, sum(widths))),
        out_shape=jax.ShapeDtypeStruct((K, sum(widths)), F32),
        compiler_params=_params("arbitrary"),
    )(a, *bs)


def _adamw(w, g, m, v, tr, name):
    R, C = w.shape

    def body(w_ref, g_ref, m_ref, v_ref, d_ref, nm_ref, nv_ref):
        gv = g_ref[...]
        nm = ADAM_B1 * m_ref[...] + (1.0 - ADAM_B1) * gv
        nv = ADAM_B2 * v_ref[...] + (1.0 - ADAM_B2) * (gv * gv)
        m_hat = nm / (1.0 - ADAM_B1 ** ADAM_STEP)
        v_hat = nv / (1.0 - ADAM_B2 ** ADAM_STEP)
        d_ref[...] = -ADAM_LR * (m_hat / (jnp.sqrt(v_hat) + ADAM_EPS) + ADAM_WD * w_ref[...])
        nm_ref[...] = nm
        nv_ref[...] = nv

    blk = pl.BlockSpec((tr, C), lambda i: (i, 0))
    out = jax.ShapeDtypeStruct((R, C), F32)
    return pl.pallas_call(
        body, name=name, grid=(R // tr,), in_specs=[blk] * 4, out_specs=[blk] * 3, out_shape=[out] * 3,
        compiler_params=_params("parallel"),
    )(w, g, m, v)


def _sum_slots(x, tr, name):
    n, R, C = x.shape

    def body(x_ref, o_ref):
        acc = x_ref[0]
        for k in range(1, n):
            acc = acc + x_ref[k]
        o_ref[...] = acc

    return pl.pallas_call(
        body, name=name, grid=(R // tr,),
        in_specs=[pl.BlockSpec((n, tr, C), lambda i: (0, i, 0))],
        out_specs=pl.BlockSpec((tr, C), lambda i: (i, 0)),
        out_shape=jax.ShapeDtypeStruct((R, C), F32),
        compiler_params=_params("parallel"),
    )(x)


def _add_half(view, other, c, name):
    if view.ndim == 3:
        _, R2, N = view.shape
        tr = 128
        grid = (R2 // tr,)
        in_specs = [pl.BlockSpec((1, tr, N), lambda i, c_ref: (c_ref[0], i, 0)), pl.BlockSpec((1, tr, N), lambda i, c_ref: (0, i, 0))]
        out_spec = pl.BlockSpec((tr, N), lambda i, c_ref: (i, 0))
        out_shape = jax.ShapeDtypeStruct((R2, N), BF16)

        def body(c_ref, a_ref, b_ref, o_ref):
            o_ref[...] = (a_ref[0] + b_ref[0]).astype(BF16)
    else:
        S, _, R2, C = view.shape
        grid = (S,)
        in_specs = [pl.BlockSpec((1, 1, R2, C), lambda s, c_ref: (s, c_ref[0], 0, 0)), pl.BlockSpec((1, 1, R2, C), lambda s, c_ref: (s, 0, 0, 0))]
        out_spec = pl.BlockSpec((1, R2, C), lambda s, c_ref: (s, 0, 0))
        out_shape = jax.ShapeDtypeStruct((S, R2, C), BF16)

        def body(c_ref, a_ref, b_ref, o_ref):
            o_ref[0] = (a_ref[0, 0] + b_ref[0, 0]).astype(BF16)

    return pl.pallas_call(
        body, name=name,
        grid_spec=pltpu.PrefetchScalarGridSpec(num_scalar_prefetch=1, grid=grid, in_specs=in_specs, out_specs=out_spec),
        out_shape=out_shape, compiler_params=_params("parallel"),
    )(c, view, other)


def _sum_chips(slots, part, kind, place, name):
    _, R2, C = slots.shape
    tr = min(R2, 128) if kind == "col" else R2

    def body(p_ref, s_ref, own_ref, o_ref):
        me = p_ref[0]
        own = own_ref[...] if kind == "col" else own_ref[0]
        acc = None
        for k in range(4):
            term = jnp.where(me == k, own, s_ref[k]).astype(F32)
            acc = term if acc is None else acc + term
        o_ref[0] = acc

    own_spec = (pl.BlockSpec((tr, C), lambda i, p: (i, p[0])) if kind == "col"
                else pl.BlockSpec((1, R2, C), lambda i, p: (p[0], 0, 0)))
    return pl.pallas_call(
        body, name=name,
        grid_spec=pltpu.PrefetchScalarGridSpec(
            num_scalar_prefetch=1, grid=(R2 // tr,),
            in_specs=[pl.BlockSpec((4, tr, C), lambda i, p: (0, i, 0)), own_spec],
            out_specs=pl.BlockSpec((1, tr, C), lambda i, p: (p[1], i, 0))),
        out_shape=jax.ShapeDtypeStruct((2, R2, C), F32), compiler_params=_params("parallel"),
    )(place, slots, part)


def _place():
    x, y, c = lax.axis_index("x"), lax.axis_index("y"), lax.axis_index("c")
    chips = [(1 - x, y), (x, 1 - y), (1 - x, 1 - y)]
    return x, y, c, 2 * x + y, chips


def _remote(src, dst, send_sems, recv_sems, k, to):
    return pltpu.make_async_remote_copy(src_ref=src, dst_ref=dst, send_sem=send_sems.at[k], recv_sem=recv_sems.at[k],
                                        device_id=to, device_id_type=MESH)


def _full_shape(shard_shape, kind):
    *lead, R, C = shard_shape
    return (*lead, R, 4 * C) if kind == "col" else (*lead, 4 * R, C)


def _slab(full_ref, shard_rc, kind, chip, half=None):
    R, C = shard_rc
    lead = (slice(None),) * (len(full_ref.shape) - 2)
    if kind == "col":
        rows = pl.ds(0, R) if half is None else pl.ds(half * (R // 2), R // 2)
        return full_ref.at[(*lead, rows, pl.ds(chip * C, C))]
    rows = pl.ds(chip * R, R) if half is None else pl.ds(chip * R + half * (R // 2), R // 2)
    return full_ref.at[(*lead, rows, slice(None))]


def _row_half(ref, half):
    R = ref.shape[-2]
    lead = (slice(None),) * (len(ref.shape) - 2)
    return ref.at[(*lead, pl.ds(half * (R // 2), R // 2), slice(None))]


def _allgather_weights(shards, kinds, after):
    n = len(shards)
    full_shapes = [_full_shape(s.shape, k) for s, k in zip(shards, kinds)]

    def body(*refs):
        ins, outs, (send_sems, recv_sems) = refs[:n], refs[n + 1:2 * n + 1], refs[2 * n + 1:]
        x, y, c, me, chips = _place()
        slab = lambda a, chip, half=None: _slab(outs[a], shards[a].shape[-2:], kinds[a], chip, half)
        my_half = lambda a: _row_half(ins[a], c)
        own = [_remote(ins[a], slab(a, me), send_sems, recv_sems, 6 * n + a, (x, y, 1 - c)) for a in range(n)]
        first = [_remote(my_half(a), slab(a, me, c), send_sems, recv_sems, j * n + a, (*chip, c))
                 for j, chip in enumerate(chips) for a in range(n)]
        for cp in first + own:
            cp.start()
        passed = []
        for j, chip in enumerate(chips):
            src = 2 * chip[0] + chip[1]
            for a in range(n):
                _remote(my_half(a), slab(a, src, c), send_sems, recv_sems, j * n + a, (x, y, c)).wait_recv()
                cp = _remote(slab(a, src, c), slab(a, src, c), send_sems, recv_sems, (3 + j) * n + a, (x, y, 1 - c))
                cp.start()
                passed.append(cp)
        for j, chip in enumerate(chips):
            src = 2 * chip[0] + chip[1]
            for a in range(n):
                _remote(my_half(a), slab(a, src, 1 - c), send_sems, recv_sems, (3 + j) * n + a, (x, y, c)).wait_recv()
        for cp in own:
            cp.wait_recv()
        for cp in first + passed + own:
            cp.wait_send()

    return pl.pallas_call(
        body, name="allgather_weights",
        in_specs=[ANY] * (n + 1), out_specs=[ANY] * n,
        out_shape=[jax.ShapeDtypeStruct(s, BF16) for s in full_shapes],
        scratch_shapes=[pltpu.SemaphoreType.DMA((7 * n,)), pltpu.SemaphoreType.DMA((7 * n,))],
        compiler_params=pltpu.CompilerParams(has_side_effects=True),
    )(*shards, after)


HBM = pl.BlockSpec(memory_space=pltpu.HBM)
SEM = pl.BlockSpec(memory_space=pltpu.SEMAPHORE)
DATAFLOW = pltpu.SideEffectType.DATAFLOW_SIDE_EFFECTING


def _split_start(name, bufs, plan, ncopies, after=None):
    nb = len(bufs)
    nin = nb + (after is not None)

    def body(*refs):
        send_sems, recv_sems, token = refs[nin], refs[nin + 1], refs[-1]
        for k, (src, dst, to) in enumerate(plan(refs[:nb])):
            _remote(src, dst, send_sems, recv_sems, k, to).start()
        token[...] = jnp.zeros_like(token)

    out = pl.pallas_call(
        body, name=name,
        out_shape=(pltpu.SemaphoreType.DMA((ncopies,)), pltpu.SemaphoreType.DMA((ncopies,)),
                   *[pltpu.HBM(b.shape, b.dtype) for b in bufs], jax.ShapeDtypeStruct((8, LANES), F32)),
        in_specs=[HBM] * nb + [ANY] * (nin - nb), out_specs=(SEM, SEM, *[HBM] * nb, pl.BlockSpec(memory_space=pltpu.VMEM)),
        input_output_aliases={i: 2 + i for i in range(nb)},
        compiler_params=pltpu.CompilerParams(has_side_effects=DATAFLOW),
    )(*[pltpu.with_memory_space_constraint(b, pltpu.HBM) for b in bufs], *([] if after is None else [after]))
    return out[0], out[1], list(out[2:2 + nb]), out[-1]


def _split_wait(name, send_sems, recv_sems, bufs, plan, after):
    nb = len(bufs)

    def body(*refs):
        s_sems, r_sems = refs[nb], refs[nb + 1]
        for k, (src, dst, to) in enumerate(plan(refs[:nb])):
            cp = _remote(src, dst, s_sems, r_sems, k, to)
            cp.wait_send()
            cp.wait_recv()

    return pl.pallas_call(
        body, name=name,
        out_shape=tuple(pltpu.HBM(b.shape, b.dtype) for b in bufs),
        in_specs=[HBM] * nb + [SEM, SEM, ANY], out_specs=tuple([HBM] * nb),
        input_output_aliases={i: i for i in range(nb)},
        compiler_params=pltpu.CompilerParams(has_side_effects=DATAFLOW),
    )(*bufs, send_sems, recv_sems, after)


def _gather_plan(n, shard_rcs, kinds):
    def plan(refs):
        x, y, c, me, chips = _place()
        out = []
        for a in range(n):
            shard, full = refs[a], refs[n + a]
            out.append((shard, _slab(full, shard_rcs[a], kinds[a], me), (x, y, 1 - c)))
            for chip in chips:
                for cc in (c, 1 - c):
                    out.append((_row_half(shard, c), _slab(full, shard_rcs[a], kinds[a], me, c), (*chip, cc)))
        return out
    return plan


def _exchange_plan(n, kinds):
    def plan(refs):
        x, y, c, me, chips = _place()
        out = []
        for a in range(n):
            part, slots = refs[a], refs[n + a]
            C = slots.shape[2]
            for chip in chips:
                dst = 2 * chip[0] + chip[1]
                src = part.at[:, pl.ds(dst * C, C)] if kinds[a] == "col" else part.at[dst]
                out.append((src, slots.at[me], (*chip, c)))
        return out
    return plan


def _swap_plan(ndims):
    n = len(ndims)

    def plan(refs):
        x, y, c, _, _ = _place()
        out = []
        for a in range(n):
            src = refs[a].at[pl.ds(1 - c, 1)] if ndims[a] == 3 else refs[a].at[:, pl.ds(1 - c, 1)]
            out.append((src, refs[n + a], (x, y, 1 - c)))
        return out
    return plan


def _join_halves(bufs):
    n = len(bufs)

    def body(*refs):
        outs, (send_sems, recv_sems) = refs[n:2 * n], refs[2 * n:]
        x, y, c, _, _ = _place()
        sends = [_remote(outs[a].at[c], outs[a].at[c], send_sems, recv_sems, a, (x, y, 1 - c)) for a in range(n)]
        for cp in sends:
            cp.start()
        for a in range(n):
            _remote(outs[a].at[c], outs[a].at[1 - c], send_sems, recv_sems, a, (x, y, c)).wait_recv()
        for cp in sends:
            cp.wait_send()

    return pl.pallas_call(
        body, name="grad_join_halves", in_specs=[ANY] * n, out_specs=[ANY] * n,
        out_shape=[jax.ShapeDtypeStruct(b.shape, F32) for b in bufs],
        input_output_aliases={a: a for a in range(n)},
        scratch_shapes=[pltpu.SemaphoreType.DMA((n,)), pltpu.SemaphoreType.DMA((n,))],
        compiler_params=pltpu.CompilerParams(has_side_effects=True),
    )(*bufs)


def _allgather_small(block):
    M, N = block.shape

    def body(x_ref, out_ref, send_sems, recv_sems, local_sem):
        x, y, c, _, chips = _place()
        me, sibling = (x, y, c), (x, y, 1 - c)

        def rows(px, py, pc):
            return out_ref.at[pl.ds((4 * px + 2 * py + pc) * M, M), :]

        def copy(k, blk, to, src=None):
            return _remote(rows(*blk) if src is None else src, rows(*blk), send_sems, recv_sems, k, to)

        mine = pltpu.make_async_copy(x_ref, rows(*me), local_sem)
        mine.start()
        first = [copy(0, me, sibling, src=x_ref)] + [copy(1 + j, me, (*chip, c), src=x_ref) for j, chip in enumerate(chips)]
        for cp in first:
            cp.start()
        passed = [copy(4 + j, (*chip, c), sibling) for j, chip in enumerate(chips)]
        for j, chip in enumerate(chips):
            copy(1 + j, (*chip, c), me).wait_recv()
            passed[j].start()
        copy(0, sibling, me).wait_recv()
        for j, chip in enumerate(chips):
            copy(4 + j, (*chip, 1 - c), me).wait_recv()
        for cp in first + passed:
            cp.wait_send()
        mine.wait()

    vm = pl.BlockSpec(memory_space=pltpu.VMEM)
    return pl.pallas_call(
        body, name="allgather_small", in_specs=[vm], out_specs=vm,
        out_shape=jax.ShapeDtypeStruct((8 * M, N), F32),
        scratch_shapes=[pltpu.SemaphoreType.DMA((7,)), pltpu.SemaphoreType.DMA((7,)), pltpu.SemaphoreType.DMA],
        compiler_params=pltpu.CompilerParams(has_side_effects=True, vmem_limit_bytes=VMEM_LIMIT),
    )(block)


def _pack(arrays):
    flat = jnp.concatenate([a.reshape(-1) for a in arrays])
    pad = (-flat.shape[0]) % (8 * LANES)
    return jnp.pad(flat, (0, pad)).reshape(-1, LANES)


def _unpack(packed, shapes):
    flat, out, off = packed.reshape(-1), [], 0
    for s in shapes:
        size = 1
        for d in s:
            size *= d
        out.append(flat[off:off + size].reshape(s))
        off += size
    return out


def _block_diag(pw):
    G, n, _ = pw.shape
    eye = jnp.eye(G, dtype=pw.dtype)
    return (eye[:, None, :, None] * pw[:, :, None, :]).reshape(G * n, G * n)


def _diag_blocks(m, G):
    n = m.shape[0] // G
    return jnp.stack([m[g * n:(g + 1) * n, g * n:(g + 1) * n] for g in range(G)])


def _pad_rows(a, rows):
    return jnp.pad(a, ((0, rows - a.shape[0]), (0, 0)))


def kernel(x, w_in, w_out, conv_w, pool_w, pool_scale, rel_bias, group_gain, pre_mix_g, post_mix_g, pre_ffn_g, post_ffn_g, w_gate_up, w_down, loss_target, m_w_in, m_w_out, m_conv_w, m_pool_w, m_pool_scale, m_rel_bias, m_group_gain, m_pre_mix_g, m_post_mix_g, m_pre_ffn_g, m_post_ffn_g, m_w_gate_up, m_w_down, v_w_in, v_w_out, v_conv_w, v_pool_w, v_pool_scale, v_rel_bias, v_group_gain, v_pre_mix_g, v_post_mix_g, v_pre_ffn_g, v_post_ffn_g, v_w_gate_up, v_w_down):
    L = w_in.shape[0]
    T, D = x.shape[1], x.shape[2]
    DC = D // 4
    NH = rel_bias.shape[1]
    NREL = rel_bias.shape[2]
    G = pool_w.shape[1]
    cs = conv_w.shape[2]
    assert TB == LEFT_CHUNKS * CHUNK and T % TB == 0 and D % (4 * LANES) == 0 and NH * HEAD_DIM == D // 2
    xi, yi, ci = lax.axis_index("x"), lax.axis_index("y"), lax.axis_index("c")
    chip = 2 * xi + yi

    kinds = ("col", "row", "col", "row")
    big = (w_in, w_out, w_gate_up, w_down)
    rcs = [w.shape[-2:] for w in big]
    conv_gathered = _allgather_small(_pack([conv_w]))
    conv_all = conv_gathered.reshape(8, -1)[:, :L * 3 * cs].reshape(4, 2, L, 3, cs)[:, 0]
    conv_full = jnp.moveaxis(conv_all, 0, 2).reshape(L, 3, 4 * cs)
    landing = lambda which: [lax.empty(_full_shape(rcs[k], kinds[k]), BF16) for k in which]
    wi0 = _allgather_weights([w_in[0:1].astype(BF16)], kinds[:1], conv_gathered)[0][0]
    rest_plan = _gather_plan(3, rcs[1:], kinds[1:])
    rest = _split_start("gather_start_0", [w[0].astype(BF16) for w in big[1:]] + landing((1, 2, 3)), rest_plan, 3 * 7, after=wi0)
    layer_plan = _gather_plan(4, rcs, kinds)
    full = [None] * L

    h = x[0]
    saved = []
    token = rest[3]
    for l in range(L):
        g_pre, g_pm, g_pf, g_po = (a[l][None] for a in (pre_mix_g, post_mix_g, pre_ffn_g, post_ffn_g))
        if 0 < l < L - 1:
            nxt = _split_start(f"gather_start_{l + 1}", [w[l + 1].astype(BF16) for w in big] + landing(range(4)), layer_plan, 4 * 7,
                               after=full[l][0])
            token = nxt[3]
        if l < L - 1:
            g_pre = g_pre + token[0:1, 0:1]
        gg, ps = group_gain[l][None], pool_scale[l][None]
        cw = _pad_rows(conv_full[l], 8)
        wbd = _block_diag(pool_w[l]).astype(BF16)
        bias = _bias_build(jnp.pad(rel_bias[l], ((0, 0), (0, RBP - NREL)))[:, None, :])
        wi = wi0 if l == 0 else full[l][0]
        xn, pa, qkv, kv_t = _inproj_fwd(h, g_pre, wi, wi[:, D + D // 2:].T)
        yab = _convpool_fwd(pa, cw, wbd, ps)
        yc, lse = _attn_fwd(qkv, kv_t, bias)
        if l == 0:
            full[0] = [wi0, *_split_wait("gather_wait_0", rest[0], rest[1], rest[2], rest_plan, yc)[3:6]]
            if L > 1:
                nxt = _split_start("gather_start_1", [w[1].astype(BF16) for w in big] + landing(range(4)), layer_plan, 4 * 7,
                                   after=full[0][1])
                g_pm = g_pm + nxt[3][0:1, 0:1]
        wi, wo, wgu, wdn = full[l]
        y, mix, h1 = _mix_out_fwd(yab, yc, gg, wo, h, g_pm)
        hn, gu, ff = _ffn_up_fwd(h1, g_pf, wgu)
        ffo, h2 = _ffn_down_fwd(ff, wdn, h1, g_po)
        saved.append(dict(h=h, xn=xn, pa=pa, qkv=qkv, kv_t=kv_t, lse=lse, yab=yab, yc=yc, y=y, mix=mix, h1=h1, hn=hn, gu=gu, ff=ff, ffo=ffo,
                          cw=cw, wbd=wbd, bias=bias, ps=ps, gg=gg, g_pre=g_pre, g_pm=g_pm, g_pf=g_pf, g_po=g_po))
        h = h2
        if l + 1 < L:
            full[l + 1] = list(_split_wait(f"gather_wait_{l + 1}", nxt[0], nxt[1], nxt[2], layer_plan, h2)[4:8])

    dh, loss_tile = _loss_grad(h, loss_target[0])
    loss = lax.psum(loss_tile[0, 0], ("x", "y", "c"))

    xplan = _exchange_plan(4, kinds)
    splan = _swap_plan([3, 4, 3, 4])
    place = jnp.stack([chip, ci]).astype(jnp.int32)
    cvec = ci.reshape(1).astype(jnp.int32)
    small_grads = [None] * L
    shard_grads = [None] * L
    dbiases = [None] * L

    def start_exchange(lp, swap):
        thru = _split_wait(f"grad_swap_wait_{lp}", swap[0], swap[1], swap[2], splan, swap[4])
        chip_sums = [_add_half(v, t, cvec, "grad_add_half") for v, t in zip(thru[:4], thru[4:])]
        slots = [lax.empty((4, p.shape[0], p.shape[1] // 4) if k == "col" else p.shape, BF16) for p, k in zip(chip_sums, kinds)]
        ssem, rsem, bufs, tok = _split_start(f"grad_exchange_start_{lp}", chip_sums + slots, xplan, 4 * 3)
        return (lp, ssem, rsem, bufs), tok

    def finish_exchange(pending, after):
        lp, ssem, rsem, thru = pending
        landed = _split_wait(f"grad_exchange_wait_{lp}", ssem, rsem, thru, xplan, after)
        bufs = [_sum_chips(landed[4 + a], landed[a], kinds[a], place, "grad_sum_chips") for a in range(4)]
        shard_grads[lp] = [j.reshape(2 * j.shape[1], j.shape[2]) for j in _join_halves(bufs)]

    swap = None
    exchange = None
    for l in reversed(range(L)):
        s = saved[l]
        wi, wo, wgu, wdn = full[l]
        g_po = s["g_po"] if swap is None else s["g_po"] + swap[3][0:1, 0:1]
        dffo, dgu, dh1, dg_po, dg_pf = _ffn_bwd(dh, s["ffo"], g_po, s["gu"], wdn, wgu, s["h1"], s["g_pf"])
        g_pm = s["g_pm"]
        if swap is not None:
            started, tok = start_exchange(l + 1, (*swap[:4], dh1))
            if exchange is not None:
                finish_exchange(exchange, tok)
            exchange = started
            g_pm = g_pm + tok[0:1, 0:1]
        dmix, dyab, dyc, dg_pm, dgg = _mix_out_bwd(dh1, s["mix"], g_pm, wo, s["yab"], s["yc"], s["gg"])
        dq, dk, dv, dbiases[l] = _attn_bwd(s["qkv"], s["kv_t"], dyc, s["yc"], s["lse"], s["bias"])
        dpa, dcw, dwbd, dps = _convpool_bwd(s["pa"], dyab, s["cw"], s["wbd"], s["ps"])
        dparts = [dpa, dq, dk, dv]
        dh, dg_pre = _inproj_bwd(dparts, wi, s["h"], s["g_pre"], dh1)
        F2, DFF = s["gu"].shape[1], s["ff"].shape[1]
        grads = [_wgrad_concat(s["xn"], dparts, "wgrad_in"),
                 _wgrad(s["y"], dmix, D, D, "wgrad_out"),
                 _wgrad(s["hn"], dgu, D, F2 // 4, "wgrad_gate_up"),
                 _wgrad(s["ff"], dffo, DFF // 2, D, "wgrad_down")]
        small_grads[l] = [dcw[:3], _diag_blocks(dwbd, G), dps[0], None, dgg[0], dg_pre[0], dg_pm[0], dg_pf[0], dg_po[0]]
        views = [g.reshape(2, g.shape[0] // 2, g.shape[1]) if k == "col" else g.reshape(4, 2, g.shape[0] // 8, g.shape[1])
                 for g, k in zip(grads, kinds)]
        lands = [lax.empty((1,) + v.shape[1:] if v.ndim == 3 else (v.shape[0], 1) + v.shape[2:], F32) for v in views]
        swap = _split_start(f"grad_swap_start_{l}", views + lands, splan, 4)
    last, tok = start_exchange(0, (*swap[:4], swap[3]))
    if exchange is not None:
        finish_exchange(exchange, tok)
    for l in range(L):
        small_grads[l][3] = _bias_fold(dbiases[l], tok)[:, 0, :NREL]

    names_shapes = [(L, 3, 4 * cs), pool_w.shape, pool_scale.shape, rel_bias.shape, group_gain.shape,
                    pre_mix_g.shape, post_mix_g.shape, pre_ffn_g.shape, post_ffn_g.shape]
    small_stacked = [jnp.stack([small_grads[l][k] for l in range(L)]) for k in range(len(names_shapes))]
    packed = _pack(small_stacked)
    M = packed.shape[0]
    total = _sum_slots(_allgather_small(packed).reshape(8, M, LANES), M, "small_sum_devices")
    g_small = _unpack(total, names_shapes)
    g_small[0] = lax.dynamic_slice_in_dim(g_small[0], chip * cs, cs, axis=2)
    finish_exchange(last, total)
    g_big = [jnp.stack([shard_grads[l][k] for l in range(L)]) for k in range(4)]

    def adam_big(w, g, m, v, name):
        shp = w.shape
        two = lambda a: a.reshape(shp[0] * shp[1], shp[2])
        return [o.reshape(shp) for o in _adamw(two(w), two(g), two(m), two(v), 256, name)]

    upd_in = adam_big(w_in, g_big[0], m_w_in, v_w_in, "adamw_in")
    upd_out = adam_big(w_out, g_big[1], m_w_out, v_w_out, "adamw_out")
    upd_gu = adam_big(w_gate_up, g_big[2], m_w_gate_up, v_w_gate_up, "adamw_gate_up")
    upd_dn = adam_big(w_down, g_big[3], m_w_down, v_w_down, "adamw_down")

    small_w = [conv_w, pool_w, pool_scale, rel_bias, group_gain, pre_mix_g, post_mix_g, pre_ffn_g, post_ffn_g]
    small_m = [m_conv_w, m_pool_w, m_pool_scale, m_rel_bias, m_group_gain, m_pre_mix_g, m_post_mix_g, m_pre_ffn_g, m_post_ffn_g]
    small_v = [v_conv_w, v_pool_w, v_pool_scale, v_rel_bias, v_group_gain, v_pre_mix_g, v_post_mix_g, v_pre_ffn_g, v_post_ffn_g]
    pw_, pg_, pm_, pv_ = _pack(small_w), _pack(g_small), _pack(small_m), _pack(small_v)
    shapes = [w.shape for w in small_w]
    upd_small = [_unpack(o, shapes) for o in _adamw(pw_, pg_, pm_, pv_, pw_.shape[0], "adamw_small")]

    def ordered(big4, small9):
        return [big4[0], big4[1], *small9, big4[2], big4[3]]

    grads = ordered(g_big, g_small)
    outs = [ordered([upd_in[k], upd_out[k], upd_gu[k], upd_dn[k]], upd_small[k]) for k in range(3)]
    return (loss, dh[None], *grads, *outs[0], *outs[1], *outs[2])
```

```python
import functools

import jax
import jax.numpy as jnp
from jax import lax
from jax.experimental import pallas as pl
from jax.experimental.pallas import tpu as pltpu

F32, BF16 = jnp.float32, jnp.bfloat16
EPS = 1e-6
CHUNK = 64
LEFT_CHUNKS = 8
REL_CLIP = 128
HEAD_DIM = 64
LANES = 128
POOL_WINDOWS = (2, 4, 8, 16)
HALO = 16
TB = LEFT_CHUNKS * CHUNK
TBF = 256
TBW = 1024
BAND = (LEFT_CHUNKS + 1) * CHUNK
SUB = 2 * CHUNK
BANDW = SUB + LEFT_CHUNKS * CHUNK
SKEW = 768
NEG = -1e30
RBP = 384
VMEM_LIMIT = 56 * 1024 * 1024
ADAM_LR, ADAM_B1, ADAM_B2, ADAM_EPS, ADAM_WD, ADAM_STEP = 0.001, 0.9, 0.999, 1e-08, 0.01, 10
MESH = pl.DeviceIdType.MESH
ANY = pl.BlockSpec(memory_space=pl.ANY)


def _params(*sem):
    kw = dict(vmem_limit_bytes=VMEM_LIMIT)
    if sem:
        kw["dimension_semantics"] = sem
    return pltpu.CompilerParams(**kw)


def _dot(a, b):
    return jnp.dot(a, b, preferred_element_type=F32)


def _dot_nt(a, b):
    return lax.dot_general(a, b, (((1,), (1,)), ((), ())), preferred_element_type=F32)


def _dot_tn(a, b):
    return lax.dot_general(a, b, (((0,), (0,)), ((), ())), preferred_element_type=F32)


def _rms(x, g):
    r = lax.rsqrt(jnp.mean(x * x, axis=-1, keepdims=True) + EPS)
    return x * r * g


def _rms_bwd(dy, x, g):
    r = lax.rsqrt(jnp.mean(x * x, axis=-1, keepdims=True) + EPS)
    xh = x * r
    dxh = dy * g
    dx = r * (dxh - xh * jnp.mean(dxh * xh, axis=-1, keepdims=True))
    return dx, jnp.sum(dy * xh, axis=0, keepdims=True)


def _full(shape):
    return pl.BlockSpec(shape, lambda *_: (0,) * len(shape))


def _acc_init(step, *refs):
    @pl.when(step == 0)
    def _():
        for r in refs:
            r[...] = jnp.zeros_like(r)


def _inproj_fwd(h, g, w, wkv_t):
    T, D = h.shape
    NQ = w.shape[1] - D
    NKV = wkv_t.shape[0]

    def body(h_ref, g_ref, w_ref, wt_ref, xn_ref, pa_ref, qkv_ref, kvt_ref):
        xn = _rms(h_ref[...], g_ref[...]).astype(BF16)
        xn_ref[...] = xn
        pa_ref[...] = _dot(xn, w_ref[:, :D])
        qkv_ref[...] = _dot(xn, w_ref[:, D:]).astype(BF16)
        kvt_ref[...] = _dot_nt(wt_ref[...], xn).astype(BF16)

    row = lambda n: pl.BlockSpec((TB, n), lambda i: (i, 0))
    return pl.pallas_call(
        body, name="inproj_fwd", grid=(T // TB,),
        in_specs=[row(D), _full((1, D)), _full(w.shape), _full(wkv_t.shape)],
        out_specs=[row(D), row(D), row(NQ), pl.BlockSpec((NKV, TB), lambda i: (0, i))],
        out_shape=[jax.ShapeDtypeStruct((T, D), BF16), jax.ShapeDtypeStruct((T, D), F32),
                   jax.ShapeDtypeStruct((T, NQ), BF16), jax.ShapeDtypeStruct((NKV, T), BF16)],
        compiler_params=_params("parallel"),
    )(h, g, w, wkv_t)


def _lane_groups(n, vals):
    lane = lax.broadcasted_iota(jnp.int32, (1, n), 1)
    q = n // 4
    return jnp.where(lane < q, vals[0], jnp.where(lane < 2 * q, vals[1], jnp.where(lane < 3 * q, vals[2], vals[3]))).astype(F32)


def _pick_group(levels, n):
    lane = lax.broadcasted_iota(jnp.int32, (1, n), 1)
    q = n // 4
    return jnp.where(lane < q, levels[0], jnp.where(lane < 2 * q, levels[1], jnp.where(lane < 3 * q, levels[2], levels[3])))


def _pool_count(blk, n):
    t1 = (blk * TB + 1 + lax.broadcasted_iota(jnp.int32, (TB, 1), 0)).astype(F32)
    return jnp.minimum(t1, _lane_groups(n, POOL_WINDOWS))


def _pool_d(pu, pu_halo, cnt):
    e = jnp.concatenate([pu_halo, pu], axis=0)
    s2 = e + pltpu.roll(e, 1, 0)
    s4 = s2 + pltpu.roll(s2, 2, 0)
    s8 = s4 + pltpu.roll(s4, 4, 0)
    s16 = s8 + pltpu.roll(s8, 8, 0)
    num = _pick_group([s2, s4, s8, s16], pu.shape[1])[HALO:]
    return num / cnt - pu


def _conv_taps(z, z_halo):
    e = jnp.concatenate([z_halo, z], axis=0)
    return pltpu.roll(e, 1, 0)[HALO:], pltpu.roll(e, 2, 0)[HALO:]


def _convpool_fwd(pa, cw, wbd, ps):
    T, D = pa.shape
    DC = D // 4

    def body(pa_ref, halo_ref, cw_ref, wbd_ref, ps_ref, yab_ref):
        i = pl.program_id(0)
        x = pa_ref[...]
        hl = jnp.where(i > 0, halo_ref[...], 0.0)
        gb, gc, u, pu = (x[:, k * DC:(k + 1) * DC] for k in range(4))
        z = gc * u
        z1, z2 = _conv_taps(z, hl[:, DC:2 * DC] * hl[:, 2 * DC:3 * DC])
        cwv = cw_ref[...]
        ya = gb * (cwv[2:3] * z + cwv[1:2] * z1 + cwv[0:1] * z2)
        d = _pool_d(pu, hl[:, 3 * DC:], _pool_count(i, DC))
        yb = _dot(d.astype(BF16), wbd_ref[...]) * ps_ref[...]
        yab_ref[...] = jnp.concatenate([ya, yb], axis=1)

    return pl.pallas_call(
        body, name="convpool_fwd", grid=(T // TB,),
        in_specs=[pl.BlockSpec((TB, D), lambda i: (i, 0)),
                  pl.BlockSpec((HALO, D), lambda i: (jnp.maximum(i * (TB // HALO) - 1, 0), 0)),
                  _full((8, DC)), _full((DC, DC)), _full((1, DC))],
        out_specs=pl.BlockSpec((TB, 2 * DC), lambda i: (i, 0)),
        out_shape=jax.ShapeDtypeStruct((T, 2 * DC), F32),
        compiler_params=_params("parallel"),
    )(pa, pa, cw, wbd, ps)


def _bias_bins(shape, col_dim):
    j = lax.broadcasted_iota(jnp.int32, shape, col_dim)
    b = lax.broadcasted_iota(jnp.int32, shape, 1 - col_dim)
    d = jnp.where(j < BAND, j, j - SKEW)
    live = jnp.logical_or(j < BAND, j > SKEW - CHUNK)
    bins = jnp.minimum(TB - d, REL_CLIP) + REL_CLIP
    return jnp.where(jnp.logical_and(live, bins == b), 1.0, 0.0).astype(F32)


def _bias_build(rb):
    H = rb.shape[0]

    def body(rb_ref, o_ref):
        v = jnp.dot(jnp.broadcast_to(rb_ref[0], (8, RBP)), _bias_bins((RBP, SKEW), 1),
                    precision=lax.Precision.HIGHEST, preferred_element_type=F32)
        x = jnp.broadcast_to(v[0:1], (SUB, SKEW))
        row = lax.broadcasted_iota(jnp.int32, (SUB, SKEW), 0)
        for b in range(SUB.bit_length() - 1):
            x = jnp.where(((row >> b) & 1) == 1, pltpu.roll(x, 1 << b, 1), x)
        qc = lax.broadcasted_iota(jnp.int32, (SUB, BANDW), 0) >> (CHUNK.bit_length() - 1)
        kc = lax.broadcasted_iota(jnp.int32, (SUB, BANDW), 1) >> (CHUNK.bit_length() - 1)
        o_ref[0] = jnp.where(jnp.logical_and(kc >= qc, kc <= qc + LEFT_CHUNKS), x[:, :BANDW], NEG).T

    return pl.pallas_call(
        body, name="bias_build", grid=(H,),
        in_specs=[pl.BlockSpec((1, 1, RBP), lambda h: (h, 0, 0))],
        out_specs=pl.BlockSpec((1, BANDW, SUB), lambda h: (h, 0, 0)),
        out_shape=jax.ShapeDtypeStruct((H, BANDW, SUB), F32),
        compiler_params=_params("parallel"),
    )(rb)


def _bias_fold(db, after):
    H = db.shape[0]

    def body(db_ref, after_ref, o_ref):
        x = jnp.concatenate([db_ref[0].T, jnp.zeros((SUB, SKEW - BANDW), F32)], axis=1)
        row = lax.broadcasted_iota(jnp.int32, (SUB, SKEW), 0)
        for b in range(SUB.bit_length() - 1):
            x = jnp.where(((row >> b) & 1) == 1, pltpu.roll(x, SKEW - (1 << b), 1), x)
        dsum = jnp.sum(x, axis=0, keepdims=True)
        o_ref[0] = jnp.dot(jnp.broadcast_to(dsum, (8, SKEW)), _bias_bins((SKEW, RBP), 0),
                           precision=lax.Precision.HIGHEST, preferred_element_type=F32)

    return pl.pallas_call(
        body, name="bias_fold", grid=(H,),
        in_specs=[pl.BlockSpec((1, BANDW, SUB), lambda h: (h, 0, 0)), ANY],
        out_specs=pl.BlockSpec((1, 8, RBP), lambda h: (h, 0, 0)),
        out_shape=jax.ShapeDtypeStruct((H, 8, RBP), F32),
        compiler_params=_params("parallel"),
    )(db, after)


def _head_masks():
    lane = lax.broadcasted_iota(jnp.int32, (1, LANES), 1)
    sub = lax.broadcasted_iota(jnp.int32, (LANES, 1), 0)
    return (lane < HEAD_DIM, sub < HEAD_DIM), (lane >= HEAD_DIM, sub >= HEAD_DIM)


def _key_tiles(s, first_block):
    return [t for t in range(s, s + BANDW // SUB) if not (first_block and t < TB // SUB)]


def _attn_fwd(qkv, kv_t, bias_t):
    T = qkv.shape[0]
    NP = qkv.shape[1] // (3 * LANES)
    NB = T // TB
    NS = TB // SUB
    scale = HEAD_DIM ** -0.5

    def body(q_ref, kc_ref, kp_ref, vtc_ref, vtp_ref, b_ref, o_ref, lse_ref):
        i = pl.program_id(1)

        def compute(first_block):
            q = q_ref[...] * scale
            kwin = jnp.concatenate([kp_ref[...], kc_ref[...]], axis=0)
            vt = jnp.concatenate([vtp_ref[...], vtc_ref[...]], axis=1)
            for s in range(NS):
                rows = slice(s * SUB, (s + 1) * SUB)
                out_t = None
                for a, (lane_m, sub_m) in enumerate(_head_masks()):
                    qa = jnp.where(lane_m, q[rows], 0)
                    tiles = _key_tiles(s, first_block)
                    keys = slice(tiles[0] * SUB, (tiles[-1] + 1) * SUB)
                    st = _dot_nt(kwin[keys], qa) + b_ref[a, (tiles[0] - s) * SUB:(tiles[-1] - s + 1) * SUB, :]
                    m = jnp.max(st, axis=0, keepdims=True)
                    p = jnp.exp(st - m)
                    l = jnp.sum(p, axis=0, keepdims=True)
                    o_a = _dot(jnp.where(sub_m, vt[:, keys], 0), p.astype(BF16)) * (1.0 / l)
                    out_t = o_a if out_t is None else out_t + o_a
                    lse_ref[0, 0, a * NS + s:a * NS + s + 1, :] = m + jnp.log(l)
                o_ref[rows, :] = out_t.T

        pl.when(i == 0)(functools.partial(compute, True))
        pl.when(i > 0)(functools.partial(compute, False))

    prev = lambda i: jnp.maximum(i - 1, 0)
    return pl.pallas_call(
        body, name="attn_fwd", grid=(NP, NB),
        in_specs=[pl.BlockSpec((TB, LANES), lambda p, i: (i, p)),
                  pl.BlockSpec((TB, LANES), lambda p, i: (i, NP + p)),
                  pl.BlockSpec((TB, LANES), lambda p, i: (prev(i), NP + p)),
                  pl.BlockSpec((LANES, TB), lambda p, i: (NP + p, i)),
                  pl.BlockSpec((LANES, TB), lambda p, i: (NP + p, prev(i))),
                  pl.BlockSpec((2, BANDW, SUB), lambda p, i: (p, 0, 0))],
        out_specs=[pl.BlockSpec((TB, LANES), lambda p, i: (i, p)),
                   pl.BlockSpec((1, 1, 8, LANES), lambda p, i: (p, i, 0, 0))],
        out_shape=[jax.ShapeDtypeStruct((T, NP * LANES), F32), jax.ShapeDtypeStruct((NP, NB, 8, LANES), F32)],
        compiler_params=_params("parallel", "parallel"),
    )(qkv, qkv, qkv, kv_t, kv_t, bias_t)


def _group_bounds(D):
    return ((0, D // 4), (D // 4, D // 2), (D // 2, D))


def _load_resident(step, *pairs_and_sems):
    @pl.when(step == 0)
    def _():
        cps = [pltpu.make_async_copy(src, dst, sem) for src, dst, sem in pairs_and_sems]
        for cp in cps:
            cp.start()
        for cp in cps:
            cp.wait()


def _layer_tail_fwd(yab, yc, gg, w_out, h, g_pm, g_pf, w_gu, w_dn, g_po):
    T, D = h.shape
    F2 = w_gu.shape[1]
    DFF = F2 // 2
    TF = DFF // 2

    def body(yab_ref, yc_ref, gg_ref, wo_hbm, h_ref, gpm_ref, gpf_ref, wgu_hbm, wdn_hbm, gpo_ref,
             y_ref, mix_ref, h1_ref, hn_ref, gu_ref, ff_ref, ffo_ref, h2_ref, wo_v, wgu_v, wdn_v, sems):
        _load_resident(pl.program_id(0), (wo_hbm, wo_v, sems.at[0]), (wgu_hbm, wgu_v, sems.at[1]), (wdn_hbm, wdn_v, sems.at[2]))
        yraw = jnp.concatenate([yab_ref[...], yc_ref[...]], axis=1)
        ggv = gg_ref[...]
        y = jnp.concatenate([_rms(yraw[:, a:b], ggv[:, a:b]) for a, b in _group_bounds(D)], axis=1).astype(BF16)
        y_ref[...] = y
        mix = _dot(y, wo_v[...])
        mix_ref[...] = mix
        h1 = h_ref[...] + _rms(mix, gpm_ref[...])
        h1_ref[...] = h1
        hn = _rms(h1, gpf_ref[...]).astype(BF16)
        hn_ref[...] = hn
        ffo = jnp.zeros((TBF, D), F32)
        for j in range(2):
            sg, su = slice(j * TF, (j + 1) * TF), slice(DFF + j * TF, DFF + (j + 1) * TF)
            gate = _dot(hn, wgu_v[:, sg])
            up = _dot(hn, wgu_v[:, su])
            gu_ref[:, sg] = gate.astype(BF16)
            gu_ref[:, su] = up.astype(BF16)
            ff = (gate * jax.nn.sigmoid(gate) * up).astype(BF16)
            ff_ref[:, sg] = ff
            ffo = ffo + _dot(ff, wdn_v[sg, :])
        ffo_ref[...] = ffo
        h2_ref[...] = h1 + _rms(ffo, gpo_ref[...])

    row = lambda n: pl.BlockSpec((TBF, n), lambda i: (i, 0))
    gain = _full((1, D))
    f32, bf16 = (lambda n: jax.ShapeDtypeStruct((T, n), F32)), (lambda n: jax.ShapeDtypeStruct((T, n), BF16))
    return pl.pallas_call(
        body, name="layer_tail_fwd", grid=(T // TBF,),
        in_specs=[row(D // 2), row(D // 2), gain, ANY, row(D), gain, gain, ANY, ANY, gain],
        out_specs=[row(D), row(D), row(D), row(D), row(F2), row(DFF), row(D), row(D)],
        out_shape=[bf16(D), f32(D), f32(D), bf16(D), bf16(F2), bf16(DFF), f32(D), f32(D)],
        scratch_shapes=[pltpu.VMEM(w_out.shape, BF16), pltpu.VMEM(w_gu.shape, BF16), pltpu.VMEM(w_dn.shape, BF16),
                        pltpu.SemaphoreType.DMA((3,))],
        compiler_params=_params("arbitrary"),
    )(yab, yc, gg, w_out, h, g_pm, g_pf, w_gu, w_dn, g_po)


def _loss_grad(h, tgt):
    T, D = h.shape

    def body(h_ref, t_ref, dh_ref, loss_ref):
        _acc_init(pl.program_id(0), loss_ref)
        diff = h_ref[...] - t_ref[...]
        dh_ref[...] = diff * (1.0 / D)
        loss_ref[...] += 0.5 * jnp.sum(jnp.mean(diff * diff, axis=-1, keepdims=True))

    row = pl.BlockSpec((TB, D), lambda i: (i, 0))
    return pl.pallas_call(
        body, name="loss_grad", grid=(T // TB,),
        in_specs=[row, row], out_specs=[row, _full((8, LANES))],
        out_shape=[jax.ShapeDtypeStruct((T, D), F32), jax.ShapeDtypeStruct((8, LANES), F32)],
        compiler_params=_params("arbitrary"),
    )(h, tgt)


def _ffn_bwd(dh2, ffo, g_po, gu, w_dn, w_gu, h1, g_pf):
    T, D = dh2.shape
    F2 = gu.shape[1]
    DFF = F2 // 2
    TF = DFF // 2

    def body(dh_ref, ffo_ref, gpo_ref, gu_ref, wdn_hbm, wgu_hbm, h_ref, gpf_ref,
             dffo_ref, dgu_ref, dh1_ref, dgpo_ref, dgpf_ref, wdn_v, wgu_v, sems):
        step = pl.program_id(0)
        _load_resident(step, (wdn_hbm, wdn_v, sems.at[0]), (wgu_hbm, wgu_v, sems.at[1]))
        _acc_init(step, dgpo_ref, dgpf_ref)
        dh = dh_ref[...]
        dffo, dg = _rms_bwd(dh, ffo_ref[...], gpo_ref[...])
        dgpo_ref[0:1, :] += dg
        dffo = dffo.astype(BF16)
        dffo_ref[...] = dffo
        dhn = jnp.zeros((TBF, D), F32)
        for j in range(2):
            sg, su = slice(j * TF, (j + 1) * TF), slice(DFF + j * TF, DFF + (j + 1) * TF)
            dff = _dot_nt(dffo, wdn_v[sg, :])
            gate, up = gu_ref[:, sg].astype(F32), gu_ref[:, su].astype(F32)
            sig = jax.nn.sigmoid(gate)
            dgate = (dff * up * (sig * (1.0 + gate * (1.0 - sig)))).astype(BF16)
            dup = (dff * (gate * sig)).astype(BF16)
            dgu_ref[:, sg] = dgate
            dgu_ref[:, su] = dup
            dhn = dhn + _dot_nt(dgate, wgu_v[:, sg]) + _dot_nt(dup, wgu_v[:, su])
        dx, dg = _rms_bwd(dhn, h_ref[...], gpf_ref[...])
        dgpf_ref[0:1, :] += dg
        dh1_ref[...] = dh + dx

    row = lambda n: pl.BlockSpec((TBF, n), lambda i: (i, 0))
    return pl.pallas_call(
        body, name="ffn_bwd", grid=(T // TBF,),
        in_specs=[row(D), row(D), _full((1, D)), row(F2), ANY, ANY, row(D), _full((1, D))],
        out_specs=[row(D), row(F2), row(D), _full((8, D)), _full((8, D))],
        out_shape=[jax.ShapeDtypeStruct((T, D), BF16), jax.ShapeDtypeStruct((T, F2), BF16), jax.ShapeDtypeStruct((T, D), F32),
                   jax.ShapeDtypeStruct((8, D), F32), jax.ShapeDtypeStruct((8, D), F32)],
        scratch_shapes=[pltpu.VMEM(w_dn.shape, BF16), pltpu.VMEM(w_gu.shape, BF16), pltpu.SemaphoreType.DMA((2,))],
        compiler_params=_params("arbitrary"),
    )(dh2, ffo, g_po, gu, w_dn, w_gu, h1, g_pf)


def _mix_out_bwd(dh1, mix, g, w, yab, yc, gg):
    T, D = dh1.shape

    def body(dh_ref, mix_ref, g_ref, w_ref, yab_ref, yc_ref, gg_ref, dmix_ref, dyab_ref, dyc_ref, dg_ref, dgg_ref):
        _acc_init(pl.program_id(0), dg_ref, dgg_ref)
        dmix, dg = _rms_bwd(dh_ref[...], mix_ref[...], g_ref[...])
        dg_ref[0:1, :] += dg
        dmix = dmix.astype(BF16)
        dmix_ref[...] = dmix
        dy = _dot_nt(dmix, w_ref[...])
        yraw = jnp.concatenate([yab_ref[...], yc_ref[...]], axis=1)
        ggv = gg_ref[...]
        parts = [_rms_bwd(dy[:, a:b], yraw[:, a:b], ggv[:, a:b]) for a, b in _group_bounds(D)]
        dgg_ref[0:1, :] += jnp.concatenate([p[1] for p in parts], axis=1)
        dyab_ref[...] = jnp.concatenate([parts[0][0], parts[1][0]], axis=1)
        dyc_ref[...] = parts[2][0]

    row = lambda n: pl.BlockSpec((TB, n), lambda i: (i, 0))
    return pl.pallas_call(
        body, name="mix_out_bwd", grid=(T // TB,),
        in_specs=[row(D), row(D), _full((1, D)), _full((D, D)), row(D // 2), row(D // 2), _full((1, D))],
        out_specs=[row(D), row(D // 2), row(D // 2), _full((8, D)), _full((8, D))],
        out_shape=[jax.ShapeDtypeStruct((T, D), BF16), jax.ShapeDtypeStruct((T, D // 2), F32), jax.ShapeDtypeStruct((T, D // 2), F32),
                   jax.ShapeDtypeStruct((8, D), F32), jax.ShapeDtypeStruct((8, D), F32)],
        compiler_params=_params("arbitrary"),
    )(dh1, mix, g, w, yab, yc, gg)


def _attn_bwd(qkv, kv_t, dyc, yc, lse, bias_t):
    T = qkv.shape[0]
    NP = qkv.shape[1] // (3 * LANES)
    NB = T // TB
    NS = TB // SUB
    scale = HEAD_DIM ** -0.5

    def body(q_ref, kc_ref, kp_ref, vc_ref, vp_ref, ktc_ref, ktp_ref, do_ref, o_ref, lse_ref, b_ref,
             dq_ref, dk_ref, dv_ref, db_ref, dk_carry, dv_carry, dkw, dvw):
        step = pl.program_id(1)
        i = NB - 1 - step
        _acc_init(step, dk_carry, dv_carry, db_ref)

        def compute(first_block):
            q = q_ref[...] * scale
            kwin = jnp.concatenate([kp_ref[...], kc_ref[...]], axis=0)
            vwin = jnp.concatenate([vp_ref[...], vc_ref[...]], axis=0)
            kt = jnp.concatenate([ktp_ref[...], ktc_ref[...]], axis=1)
            do = do_ref[...]
            dob = do.astype(BF16)
            prod = do * o_ref[...]
            ones = jnp.ones((8, LANES), F32)
            heads = []
            for lane_m, sub_m in _head_masks():
                delta = lax.dot_general(ones, jnp.where(lane_m, prod, 0.0), (((1,), (1,)), ((), ())),
                                        precision=lax.Precision.HIGHEST, preferred_element_type=F32)
                heads.append((jnp.where(lane_m, q, 0), jnp.where(lane_m, dob, 0), delta, sub_m))
            written = set()
            for s in range(NS):
                rows = slice(s * SUB, (s + 1) * SUB)
                tiles = _key_tiles(s, first_block)
                keys = slice(tiles[0] * SUB, (tiles[-1] + 1) * SUB)
                brows = slice((tiles[0] - s) * SUB, (tiles[-1] - s + 1) * SUB)
                dk_c = dv_c = dqt = None
                for a, (qa, doa, delta, sub_m) in enumerate(heads):
                    st = _dot_nt(kwin[keys], qa[rows]) + b_ref[a, brows, :]
                    p = jnp.exp(st - lse_ref[0, 0, a * NS + s:a * NS + s + 1, :])
                    ds = p * (_dot_nt(vwin[keys], doa[rows]) - delta[0:1, rows])
                    db_ref[a, brows, :] += ds
                    dsb = ds.astype(BF16)
                    terms = (_dot(dsb, qa[rows]), _dot(p.astype(BF16), doa[rows]), _dot(jnp.where(sub_m, kt[:, keys], 0), dsb))
                    dk_c, dv_c, dqt = terms if a == 0 else (dk_c + terms[0], dv_c + terms[1], dqt + terms[2])
                dq_ref[rows, :] = (dqt.T * scale).astype(BF16)
                for n, t in enumerate(tiles):
                    win, loc = slice(t * SUB, (t + 1) * SUB), slice(n * SUB, (n + 1) * SUB)
                    if t in written:
                        dkw[win, :] += dk_c[loc]
                        dvw[win, :] += dv_c[loc]
                    else:
                        dkw[win, :] = dk_c[loc]
                        dvw[win, :] = dv_c[loc]
                        written.add(t)
            dk_ref[...] = (dkw[TB:, :] + dk_carry[...]).astype(BF16)
            dv_ref[...] = (dvw[TB:, :] + dv_carry[...]).astype(BF16)
            if not first_block:
                dk_carry[...] = dkw[:TB, :]
                dv_carry[...] = dvw[:TB, :]

        pl.when(i == 0)(functools.partial(compute, True))
        pl.when(i > 0)(functools.partial(compute, False))

    blk = lambda s: NB - 1 - s
    prev = lambda s: jnp.maximum(NB - 2 - s, 0)
    rows = lambda which, off: pl.BlockSpec((TB, LANES), lambda p, s: (which(s), off + p))
    out = jax.ShapeDtypeStruct((T, NP * LANES), BF16)
    strip = pl.BlockSpec((2, BANDW, SUB), lambda p, s: (p, 0, 0))
    return pl.pallas_call(
        body, name="attn_bwd", grid=(NP, NB),
        in_specs=[rows(blk, 0), rows(blk, NP), rows(prev, NP), rows(blk, 2 * NP), rows(prev, 2 * NP),
                  pl.BlockSpec((LANES, TB), lambda p, s: (p, blk(s))), pl.BlockSpec((LANES, TB), lambda p, s: (p, prev(s))),
                  rows(blk, 0), rows(blk, 0), pl.BlockSpec((1, 1, 8, LANES), lambda p, s: (p, blk(s), 0, 0)), strip],
        out_specs=[rows(blk, 0), rows(blk, 0), rows(blk, 0), strip],
        out_shape=[out, out, out, jax.ShapeDtypeStruct((2 * NP, BANDW, SUB), F32)],
        scratch_shapes=[pltpu.VMEM((TB, LANES), F32), pltpu.VMEM((TB, LANES), F32),
                        pltpu.VMEM((2 * TB, LANES), F32), pltpu.VMEM((2 * TB, LANES), F32)],
        compiler_params=_params("arbitrary", "arbitrary"),
    )(qkv, qkv, qkv, qkv, qkv, kv_t, kv_t, dyc, yc, lse, bias_t)


def _convpool_bwd(pa, dyab, cw, wbd, ps):
    T, D = pa.shape
    DC = D // 4
    NB = T // TB
    N = TB + HALO

    def body(pa_ref, halo_ref, dy_ref, cw_ref, wbd_ref, ps_ref, dpa_ref, dcw_ref, dwbd_ref, dps_ref, dc_carry, e_carry):
        step = pl.program_id(0)
        i = NB - 1 - step
        _acc_init(step, dcw_ref, dwbd_ref, dps_ref, dc_carry, e_carry)
        x = pa_ref[...]
        hl = jnp.where(i > 0, halo_ref[...], 0.0)
        gb, gc, u, pu = (x[:, k * DC:(k + 1) * DC] for k in range(4))
        dy = dy_ref[...]
        dya, dyb = dy[:, :DC], dy[:, DC:]
        cwv = cw_ref[...]
        z = gc * u
        z1, z2 = _conv_taps(z, hl[:, DC:2 * DC] * hl[:, 2 * DC:3 * DC])
        dgb = dya * (cwv[2:3] * z + cwv[1:2] * z1 + cwv[0:1] * z2)
        dconv = dya * gb
        dcw_ref[0:1, :] += jnp.sum(dconv * z2, axis=0, keepdims=True)
        dcw_ref[1:2, :] += jnp.sum(dconv * z1, axis=0, keepdims=True)
        dcw_ref[2:3, :] += jnp.sum(dconv * z, axis=0, keepdims=True)
        ext = jnp.concatenate([dconv, dc_carry[...]], axis=0)
        dz = cwv[2:3] * dconv + cwv[1:2] * pltpu.roll(ext, N - 1, 0)[:TB] + cwv[0:1] * pltpu.roll(ext, N - 2, 0)[:TB]
        dc_carry[...] = dconv[:HALO]
        cnt = _pool_count(i, DC)
        d = _pool_d(pu, hl[:, 3 * DC:], cnt).astype(BF16)
        psv = ps_ref[...]
        w = wbd_ref[...]
        dps_ref[0:1, :] += jnp.sum(dyb * _dot(d, w), axis=0, keepdims=True)
        dys = (dyb * psv).astype(BF16)
        dwbd_ref[...] += _dot_tn(d, dys)
        dd = _dot_nt(dys, w)
        e = dd / cnt
        ext = jnp.concatenate([e, e_carry[...]], axis=0)
        a2 = ext + pltpu.roll(ext, N - 1, 0)
        a4 = a2 + pltpu.roll(a2, N - 2, 0)
        a8 = a4 + pltpu.roll(a4, N - 4, 0)
        a16 = a8 + pltpu.roll(a8, N - 8, 0)
        dpu = _pick_group([a2, a4, a8, a16], DC)[:TB] - dd
        e_carry[...] = e[:HALO]
        dpa_ref[...] = jnp.concatenate([dgb, dz * u, dz * gc, dpu], axis=1).astype(BF16)

    blk = lambda n: pl.BlockSpec((TB, n), lambda s: (NB - 1 - s, 0))
    return pl.pallas_call(
        body, name="convpool_bwd", grid=(NB,),
        in_specs=[blk(D), pl.BlockSpec((HALO, D), lambda s: (jnp.maximum((NB - 1 - s) * (TB // HALO) - 1, 0), 0)),
                  blk(2 * DC), _full((8, DC)), _full((DC, DC)), _full((1, DC))],
        out_specs=[blk(D), _full((8, DC)), _full((DC, DC)), _full((8, DC))],
        out_shape=[jax.ShapeDtypeStruct((T, D), BF16), jax.ShapeDtypeStruct((8, DC), F32),
                   jax.ShapeDtypeStruct((DC, DC), F32), jax.ShapeDtypeStruct((8, DC), F32)],
        scratch_shapes=[pltpu.VMEM((HALO, DC), F32), pltpu.VMEM((HALO, DC), F32)],
        compiler_params=_params("arbitrary"),
    )(pa, pa, dyab, cw, wbd, ps)


def _inproj_bwd(dparts, w, h, g, dh1):
    T, D = h.shape
    widths = [p.shape[1] for p in dparts]
    n = len(dparts)

    def body(*refs):
        parts, (w_ref, h_ref, g_ref, dh1_ref, dh_ref, dg_ref) = refs[:n], refs[n:]
        _acc_init(pl.program_id(0), dg_ref)
        dxn, off = jnp.zeros((TB, D), F32), 0
        for p_ref, wd in zip(parts, widths):
            dxn = dxn + _dot_nt(p_ref[...], w_ref[:, off:off + wd])
            off += wd
        dx, dg = _rms_bwd(dxn, h_ref[...], g_ref[...])
        dg_ref[0:1, :] += dg
        dh_ref[...] = dh1_ref[...] + dx

    row = lambda m: pl.BlockSpec((TB, m), lambda i: (i, 0))
    return pl.pallas_call(
        body, name="inproj_bwd", grid=(T // TB,),
        in_specs=[row(wd) for wd in widths] + [_full(w.shape), row(D), _full((1, D)), row(D)],
        out_specs=[row(D), _full((8, D))],
        out_shape=[jax.ShapeDtypeStruct((T, D), F32), jax.ShapeDtypeStruct((8, D), F32)],
        compiler_params=_params("arbitrary"),
    )(*dparts, w, h, g, dh1)


def _wgrad(a, b, tk, tn, name):
    T, K = a.shape
    N = b.shape[1]

    def body(a_ref, b_ref, o_ref):
        _acc_init(pl.program_id(2), o_ref)
        o_ref[...] += _dot_tn(a_ref[...], b_ref[...])

    return pl.pallas_call(
        body, name=name, grid=(K // tk, N // tn, T // TBW),
        in_specs=[pl.BlockSpec((TBW, tk), lambda k, n, t: (t, k)), pl.BlockSpec((TBW, tn), lambda k, n, t: (t, n))],
        out_specs=pl.BlockSpec((tk, tn), lambda k, n, t: (k, n)),
        out_shape=jax.ShapeDtypeStruct((K, N), F32),
        compiler_params=_params("parallel", "parallel", "arbitrary"),
    )(a, b)


def _wgrad_concat(a, bs, name):
    T, K = a.shape
    widths = [b.shape[1] for b in bs]
    n = len(bs)

    def body(*refs):
        a_ref, b_refs, o_ref = refs[0], refs[1:1 + n], refs[1 + n]
        _acc_init(pl.program_id(0), o_ref)
        av, off = a_ref[...], 0
        for b_ref, wd in zip(b_refs, widths):
            o_ref[:, off:off + wd] += _dot_tn(av, b_ref[...])
            off += wd

    row = lambda m: pl.BlockSpec((TB, m), lambda t: (t, 0))
    return pl.pallas_call(
        body, name=name, grid=(T // TB,),
        in_specs=[row(K)] + [row(wd) for wd in widths],
        out_specs=_full((K, sum(widths))),
        out_shape=jax.ShapeDtypeStruct((K, sum(widths)), F32),
        compiler_params=_params("arbitrary"),
    )(a, *bs)


def _adamw(w, g, m, v, tr, name):
    R, C = w.shape

    def body(w_ref, g_ref, m_ref, v_ref, d_ref, nm_ref, nv_ref):
        gv = g_ref[...]
        nm = ADAM_B1 * m_ref[...] + (1.0 - ADAM_B1) * gv
        nv = ADAM_B2 * v_ref[...] + (1.0 - ADAM_B2) * (gv * gv)
        m_hat = nm / (1.0 - ADAM_B1 ** ADAM_STEP)
        v_hat = nv / (1.0 - ADAM_B2 ** ADAM_STEP)
        d_ref[...] = -ADAM_LR * (m_hat / (jnp.sqrt(v_hat) + ADAM_EPS) + ADAM_WD * w_ref[...])
        nm_ref[...] = nm
        nv_ref[...] = nv

    blk = pl.BlockSpec((tr, C), lambda i: (i, 0))
    out = jax.ShapeDtypeStruct((R, C), F32)
    return pl.pallas_call(
        body, name=name, grid=(R // tr,), in_specs=[blk] * 4, out_specs=[blk] * 3, out_shape=[out] * 3,
        compiler_params=_params("parallel"),
    )(w, g, m, v)


def _sum_slots(x, tr, name):
    n, R, C = x.shape

    def body(x_ref, o_ref):
        acc = x_ref[0]
        for k in range(1, n):
            acc = acc + x_ref[k]
        o_ref[...] = acc

    return pl.pallas_call(
        body, name=name, grid=(R // tr,),
        in_specs=[pl.BlockSpec((n, tr, C), lambda i: (0, i, 0))],
        out_specs=pl.BlockSpec((tr, C), lambda i: (i, 0)),
        out_shape=jax.ShapeDtypeStruct((R, C), F32),
        compiler_params=_params("parallel"),
    )(x)


def _add_half(view, other, c, name):
    if view.ndim == 3:
        _, R2, N = view.shape
        tr = 128
        grid = (R2 // tr,)
        in_specs = [pl.BlockSpec((1, tr, N), lambda i, c_ref: (c_ref[0], i, 0)), pl.BlockSpec((1, tr, N), lambda i, c_ref: (0, i, 0))]
        out_spec = pl.BlockSpec((tr, N), lambda i, c_ref: (i, 0))
        out_shape = jax.ShapeDtypeStruct((R2, N), BF16)

        def body(c_ref, a_ref, b_ref, o_ref):
            o_ref[...] = (a_ref[0] + b_ref[0]).astype(BF16)
    else:
        S, _, R2, C = view.shape
        grid = (S,)
        in_specs = [pl.BlockSpec((1, 1, R2, C), lambda s, c_ref: (s, c_ref[0], 0, 0)), pl.BlockSpec((1, 1, R2, C), lambda s, c_ref: (s, 0, 0, 0))]
        out_spec = pl.BlockSpec((1, R2, C), lambda s, c_ref: (s, 0, 0))
        out_shape = jax.ShapeDtypeStruct((S, R2, C), BF16)

        def body(c_ref, a_ref, b_ref, o_ref):
            o_ref[0] = (a_ref[0, 0] + b_ref[0, 0]).astype(BF16)

    return pl.pallas_call(
        body, name=name,
        grid_spec=pltpu.PrefetchScalarGridSpec(num_scalar_prefetch=1, grid=grid, in_specs=in_specs, out_specs=out_spec),
        out_shape=out_shape, compiler_params=_params("parallel"),
    )(c, view, other)


def _sum_chips(slots, part, kind, place, name):
    _, R2, C = slots.shape
    tr = min(R2, 128) if kind == "col" else R2

    def body(p_ref, s_ref, own_ref, o_ref):
        me = p_ref[0]
        own = own_ref[...] if kind == "col" else own_ref[0]
        acc = None
        for k in range(4):
            term = jnp.where(me == k, own, s_ref[k]).astype(F32)
            acc = term if acc is None else acc + term
        o_ref[0] = acc

    own_spec = (pl.BlockSpec((tr, C), lambda i, p: (i, p[0])) if kind == "col"
                else pl.BlockSpec((1, R2, C), lambda i, p: (p[0], 0, 0)))
    return pl.pallas_call(
        body, name=name,
        grid_spec=pltpu.PrefetchScalarGridSpec(
            num_scalar_prefetch=1, grid=(R2 // tr,),
            in_specs=[pl.BlockSpec((4, tr, C), lambda i, p: (0, i, 0)), own_spec],
            out_specs=pl.BlockSpec((1, tr, C), lambda i, p: (p[1], i, 0))),
        out_shape=jax.ShapeDtypeStruct((2, R2, C), F32), compiler_params=_params("parallel"),
    )(place, slots, part)


def _place():
    x, y, c = lax.axis_index("x"), lax.axis_index("y"), lax.axis_index("c")
    chips = [(1 - x, y), (x, 1 - y), (1 - x, 1 - y)]
    return x, y, c, 2 * x + y, chips


def _remote(src, dst, send_sems, recv_sems, k, to):
    return pltpu.make_async_remote_copy(src_ref=src, dst_ref=dst, send_sem=send_sems.at[k], recv_sem=recv_sems.at[k],
                                        device_id=to, device_id_type=MESH)


def _full_shape(shard_shape, kind):
    *lead, R, C = shard_shape
    return (*lead, R, 4 * C) if kind == "col" else (*lead, 4 * R, C)


def _slab(full_ref, shard_rc, kind, chip, half=None):
    R, C = shard_rc
    lead = (slice(None),) * (len(full_ref.shape) - 2)
    if kind == "col":
        rows = pl.ds(0, R) if half is None else pl.ds(half * (R // 2), R // 2)
        return full_ref.at[(*lead, rows, pl.ds(chip * C, C))]
    rows = pl.ds(chip * R, R) if half is None else pl.ds(chip * R + half * (R // 2), R // 2)
    return full_ref.at[(*lead, rows, slice(None))]


def _row_half(ref, half):
    R = ref.shape[-2]
    lead = (slice(None),) * (len(ref.shape) - 2)
    return ref.at[(*lead, pl.ds(half * (R // 2), R // 2), slice(None))]


def _allgather_weights(shards, kinds, after):
    n = len(shards)
    full_shapes = [_full_shape(s.shape, k) for s, k in zip(shards, kinds)]

    def body(*refs):
        ins, outs, (send_sems, recv_sems) = refs[:n], refs[n + 1:2 * n + 1], refs[2 * n + 1:]
        x, y, c, me, chips = _place()
        slab = lambda a, chip, half=None: _slab(outs[a], shards[a].shape[-2:], kinds[a], chip, half)
        my_half = lambda a: _row_half(ins[a], c)
        own = [_remote(ins[a], slab(a, me), send_sems, recv_sems, 6 * n + a, (x, y, 1 - c)) for a in range(n)]
        first = [_remote(my_half(a), slab(a, me, c), send_sems, recv_sems, j * n + a, (*chip, c))
                 for j, chip in enumerate(chips) for a in range(n)]
        for cp in first + own:
            cp.start()
        passed = []
        for j, chip in enumerate(chips):
            src = 2 * chip[0] + chip[1]
            for a in range(n):
                _remote(my_half(a), slab(a, src, c), send_sems, recv_sems, j * n + a, (x, y, c)).wait_recv()
                cp = _remote(slab(a, src, c), slab(a, src, c), send_sems, recv_sems, (3 + j) * n + a, (x, y, 1 - c))
                cp.start()
                passed.append(cp)
        for j, chip in enumerate(chips):
            src = 2 * chip[0] + chip[1]
            for a in range(n):
                _remote(my_half(a), slab(a, src, 1 - c), send_sems, recv_sems, (3 + j) * n + a, (x, y, c)).wait_recv()
        for cp in own:
            cp.wait_recv()
        for cp in first + passed + own:
            cp.wait_send()

    return pl.pallas_call(
        body, name="allgather_weights",
        in_specs=[ANY] * (n + 1), out_specs=[ANY] * n,
        out_shape=[jax.ShapeDtypeStruct(s, BF16) for s in full_shapes],
        scratch_shapes=[pltpu.SemaphoreType.DMA((7 * n,)), pltpu.SemaphoreType.DMA((7 * n,))],
        compiler_params=pltpu.CompilerParams(has_side_effects=True),
    )(*shards, after)


HBM = pl.BlockSpec(memory_space=pltpu.HBM)
SEM = pl.BlockSpec(memory_space=pltpu.SEMAPHORE)
DATAFLOW = pltpu.SideEffectType.DATAFLOW_SIDE_EFFECTING


def _split_start(name, bufs, plan, ncopies, after=None):
    nb = len(bufs)
    nin = nb + (after is not None)

    def body(*refs):
        send_sems, recv_sems, token = refs[nin], refs[nin + 1], refs[-1]
        for k, (src, dst, to) in enumerate(plan(refs[:nb])):
            _remote(src, dst, send_sems, recv_sems, k, to).start()
        token[...] = jnp.zeros_like(token)

    out = pl.pallas_call(
        body, name=name,
        out_shape=(pltpu.SemaphoreType.DMA((ncopies,)), pltpu.SemaphoreType.DMA((ncopies,)),
                   *[pltpu.HBM(b.shape, b.dtype) for b in bufs], jax.ShapeDtypeStruct((8, LANES), F32)),
        in_specs=[HBM] * nb + [ANY] * (nin - nb), out_specs=(SEM, SEM, *[HBM] * nb, pl.BlockSpec(memory_space=pltpu.VMEM)),
        input_output_aliases={i: 2 + i for i in range(nb)},
        compiler_params=pltpu.CompilerParams(has_side_effects=DATAFLOW),
    )(*[pltpu.with_memory_space_constraint(b, pltpu.HBM) for b in bufs], *([] if after is None else [after]))
    return out[0], out[1], list(out[2:2 + nb]), out[-1]


def _split_wait(name, send_sems, recv_sems, bufs, plan, after):
    nb = len(bufs)

    def body(*refs):
        s_sems, r_sems = refs[nb], refs[nb + 1]
        for k, (src, dst, to) in enumerate(plan(refs[:nb])):
            cp = _remote(src, dst, s_sems, r_sems, k, to)
            cp.wait_send()
            cp.wait_recv()

    return pl.pallas_call(
        body, name=name,
        out_shape=tuple(pltpu.HBM(b.shape, b.dtype) for b in bufs),
        in_specs=[HBM] * nb + [SEM, SEM, ANY], out_specs=tuple([HBM] * nb),
        input_output_aliases={i: i for i in range(nb)},
        compiler_params=pltpu.CompilerParams(has_side_effects=DATAFLOW),
    )(*bufs, send_sems, recv_sems, after)


def _gather_plan(n, shard_rcs, kinds):
    def plan(refs):
        x, y, c, me, chips = _place()
        out = []
        for a in range(n):
            shard, full = refs[a], refs[n + a]
            out.append((shard, _slab(full, shard_rcs[a], kinds[a], me), (x, y, 1 - c)))
            for chip in chips:
                for cc in (c, 1 - c):
                    out.append((_row_half(shard, c), _slab(full, shard_rcs[a], kinds[a], me, c), (*chip, cc)))
        return out
    return plan


def _exchange_plan(n, kinds):
    def plan(refs):
        x, y, c, me, chips = _place()
        out = []
        for a in range(n):
            part, slots = refs[a], refs[n + a]
            C = slots.shape[2]
            for chip in chips:
                dst = 2 * chip[0] + chip[1]
                src = part.at[:, pl.ds(dst * C, C)] if kinds[a] == "col" else part.at[dst]
                out.append((src, slots.at[me], (*chip, c)))
        return out
    return plan


def _swap_plan(ndims):
    n = len(ndims)

    def plan(refs):
        x, y, c, _, _ = _place()
        out = []
        for a in range(n):
            src = refs[a].at[pl.ds(1 - c, 1)] if ndims[a] == 3 else refs[a].at[:, pl.ds(1 - c, 1)]
            out.append((src, refs[n + a], (x, y, 1 - c)))
        return out
    return plan


def _join_halves(bufs):
    n = len(bufs)

    def body(*refs):
        outs, (send_sems, recv_sems) = refs[n:2 * n], refs[2 * n:]
        x, y, c, _, _ = _place()
        sends = [_remote(outs[a].at[c], outs[a].at[c], send_sems, recv_sems, a, (x, y, 1 - c)) for a in range(n)]
        for cp in sends:
            cp.start()
        for a in range(n):
            _remote(outs[a].at[c], outs[a].at[1 - c], send_sems, recv_sems, a, (x, y, c)).wait_recv()
        for cp in sends:
            cp.wait_send()

    return pl.pallas_call(
        body, name="grad_join_halves", in_specs=[ANY] * n, out_specs=[ANY] * n,
        out_shape=[jax.ShapeDtypeStruct(b.shape, F32) for b in bufs],
        input_output_aliases={a: a for a in range(n)},
        scratch_shapes=[pltpu.SemaphoreType.DMA((n,)), pltpu.SemaphoreType.DMA((n,))],
        compiler_params=pltpu.CompilerParams(has_side_effects=True),
    )(*bufs)


def _allgather_small(block):
    M, N = block.shape

    def body(x_ref, out_ref, send_sems, recv_sems, local_sem):
        x, y, c, _, chips = _place()
        me, sibling = (x, y, c), (x, y, 1 - c)

        def rows(px, py, pc):
            return out_ref.at[pl.ds((4 * px + 2 * py + pc) * M, M), :]

        def copy(k, blk, to, src=None):
            return _remote(rows(*blk) if src is None else src, rows(*blk), send_sems, recv_sems, k, to)

        mine = pltpu.make_async_copy(x_ref, rows(*me), local_sem)
        mine.start()
        first = [copy(0, me, sibling, src=x_ref)] + [copy(1 + j, me, (*chip, c), src=x_ref) for j, chip in enumerate(chips)]
        for cp in first:
            cp.start()
        passed = [copy(4 + j, (*chip, c), sibling) for j, chip in enumerate(chips)]
        for j, chip in enumerate(chips):
            copy(1 + j, (*chip, c), me).wait_recv()
            passed[j].start()
        copy(0, sibling, me).wait_recv()
        for j, chip in enumerate(chips):
            copy(4 + j, (*chip, 1 - c), me).wait_recv()
        for cp in first + passed:
            cp.wait_send()
        mine.wait()

    vm = pl.BlockSpec(memory_space=pltpu.VMEM)
    return pl.pallas_call(
        body, name="allgather_small", in_specs=[vm], out_specs=vm,
        out_shape=jax.ShapeDtypeStruct((8 * M, N), F32),
        scratch_shapes=[pltpu.SemaphoreType.DMA((7,)), pltpu.SemaphoreType.DMA((7,)), pltpu.SemaphoreType.DMA],
        compiler_params=pltpu.CompilerParams(has_side_effects=True, vmem_limit_bytes=VMEM_LIMIT),
    )(block)


def _pack(arrays):
    flat = jnp.concatenate([a.reshape(-1) for a in arrays])
    pad = (-flat.shape[0]) % (8 * LANES)
    return jnp.pad(flat, (0, pad)).reshape(-1, LANES)


def _unpack(packed, shapes):
    flat, out, off = packed.reshape(-1), [], 0
    for s in shapes:
        size = 1
        for d in s:
            size *= d
        out.append(flat[off:off + size].reshape(s))
        off += size
    return out


def _block_diag(pw):
    G, n, _ = pw.shape
    eye = jnp.eye(G, dtype=pw.dtype)
    return (eye[:, None, :, None] * pw[:, :, None, :]).reshape(G * n, G * n)


def _diag_blocks(m, G):
    n = m.shape[0] // G
    return jnp.stack([m[g * n:(g + 1) * n, g * n:(g + 1) * n] for g in range(G)])


def _pad_rows(a, rows):
    return jnp.pad(a, ((0, rows - a.shape[0]), (0, 0)))


def kernel(x, w_in, w_out, conv_w, pool_w, pool_scale, rel_bias, group_gain, pre_mix_g, post_mix_g, pre_ffn_g, post_ffn_g, w_gate_up, w_down, loss_target, m_w_in, m_w_out, m_conv_w, m_pool_w, m_pool_scale, m_rel_bias, m_group_gain, m_pre_mix_g, m_post_mix_g, m_pre_ffn_g, m_post_ffn_g, m_w_gate_up, m_w_down, v_w_in, v_w_out, v_conv_w, v_pool_w, v_pool_scale, v_rel_bias, v_group_gain, v_pre_mix_g, v_post_mix_g, v_pre_ffn_g, v_post_ffn_g, v_w_gate_up, v_w_down):
    L = w_in.shape[0]
    T, D = x.shape[1], x.shape[2]
    DC = D // 4
    NH = rel_bias.shape[1]
    NREL = rel_bias.shape[2]
    G = pool_w.shape[1]
    cs = conv_w.shape[2]
    assert TB == LEFT_CHUNKS * CHUNK and T % TB == 0 and D % (4 * LANES) == 0 and NH * HEAD_DIM == D // 2
    xi, yi, ci = lax.axis_index("x"), lax.axis_index("y"), lax.axis_index("c")
    chip = 2 * xi + yi

    kinds = ("col", "row", "col", "row")
    big = (w_in, w_out, w_gate_up, w_down)
    rcs = [w.shape[-2:] for w in big]
    conv_gathered = _allgather_small(_pack([conv_w]))
    conv_all = conv_gathered.reshape(8, -1)[:, :L * 3 * cs].reshape(4, 2, L, 3, cs)[:, 0]
    conv_full = jnp.moveaxis(conv_all, 0, 2).reshape(L, 3, 4 * cs)
    landing = lambda which: [lax.empty(_full_shape(rcs[k], kinds[k]), BF16) for k in which]
    wi0 = _allgather_weights([w_in[0:1].astype(BF16)], kinds[:1], conv_gathered)[0][0]
    rest_plan = _gather_plan(3, rcs[1:], kinds[1:])
    rest = _split_start("gather_start_0", [w[0].astype(BF16) for w in big[1:]] + landing((1, 2, 3)), rest_plan, 3 * 7, after=wi0)
    layer_plan = _gather_plan(4, rcs, kinds)
    full = [None] * L

    h = x[0]
    saved = []
    token = rest[3]
    for l in range(L):
        g_pre, g_pm, g_pf, g_po = (a[l][None] for a in (pre_mix_g, post_mix_g, pre_ffn_g, post_ffn_g))
        if 0 < l < L - 1:
            nxt = _split_start(f"gather_start_{l + 1}", [w[l + 1].astype(BF16) for w in big] + landing(range(4)), layer_plan, 4 * 7,
                               after=full[l][0])
            token = nxt[3]
        if l < L - 1:
            g_pre = g_pre + token[0:1, 0:1]
        gg, ps = group_gain[l][None], pool_scale[l][None]
        cw = _pad_rows(conv_full[l], 8)
        wbd = _block_diag(pool_w[l]).astype(BF16)
        bias = _bias_build(jnp.pad(rel_bias[l], ((0, 0), (0, RBP - NREL)))[:, None, :])
        wi = wi0 if l == 0 else full[l][0]
        xn, pa, qkv, kv_t = _inproj_fwd(h, g_pre, wi, wi[:, D + D // 2:].T)
        yab = _convpool_fwd(pa, cw, wbd, ps)
        yc, lse = _attn_fwd(qkv, kv_t, bias)
        if l == 0:
            full[0] = [wi0, *_split_wait("gather_wait_0", rest[0], rest[1], rest[2], rest_plan, yc)[3:6]]
            if L > 1:
                nxt = _split_start("gather_start_1", [w[1].astype(BF16) for w in big] + landing(range(4)), layer_plan, 4 * 7,
                                   after=full[0][1])
                g_pm = g_pm + nxt[3][0:1, 0:1]
        wi, wo, wgu, wdn = full[l]
        y, mix, h1, hn, gu, ff, ffo, h2 = _layer_tail_fwd(yab, yc, gg, wo, h, g_pm, g_pf, wgu, wdn, g_po)
        saved.append(dict(h=h, xn=xn, pa=pa, qkv=qkv, kv_t=kv_t, lse=lse, yab=yab, yc=yc, y=y, mix=mix, h1=h1, hn=hn, gu=gu, ff=ff, ffo=ffo,
                          cw=cw, wbd=wbd, bias=bias, ps=ps, gg=gg, g_pre=g_pre, g_pm=g_pm, g_pf=g_pf, g_po=g_po))
        h = h2
        if l + 1 < L:
            full[l + 1] = list(_split_wait(f"gather_wait_{l + 1}", nxt[0], nxt[1], nxt[2], layer_plan, h2)[4:8])

    dh, loss_tile = _loss_grad(h, loss_target[0])
    loss = lax.psum(loss_tile[0, 0], ("x", "y", "c"))

    xplan = _exchange_plan(4, kinds)
    splan = _swap_plan([3, 4, 3, 4])
    place = jnp.stack([chip, ci]).astype(jnp.int32)
    cvec = ci.reshape(1).astype(jnp.int32)
    small_grads = [None] * L
    shard_grads = [None] * L
    dbiases = [None] * L

    def start_exchange(lp, swap):
        thru = _split_wait(f"grad_swap_wait_{lp}", swap[0], swap[1], swap[2], splan, swap[4])
        chip_sums = [_add_half(v, t, cvec, "grad_add_half") for v, t in zip(thru[:4], thru[4:])]
        slots = [lax.empty((4, p.shape[0], p.shape[1] // 4) if k == "col" else p.shape, BF16) for p, k in zip(chip_sums, kinds)]
        ssem, rsem, bufs, tok = _split_start(f"grad_exchange_start_{lp}", chip_sums + slots, xplan, 4 * 3)
        return (lp, ssem, rsem, bufs), tok

    def finish_exchange(pending, after):
        lp, ssem, rsem, thru = pending
        landed = _split_wait(f"grad_exchange_wait_{lp}", ssem, rsem, thru, xplan, after)
        bufs = [_sum_chips(landed[4 + a], landed[a], kinds[a], place, "grad_sum_chips") for a in range(4)]
        shard_grads[lp] = [j.reshape(2 * j.shape[1], j.shape[2]) for j in _join_halves(bufs)]

    swap = None
    exchange = None
    for l in reversed(range(L)):
        s = saved[l]
        wi, wo, wgu, wdn = full[l]
        g_po = s["g_po"] if swap is None else s["g_po"] + swap[3][0:1, 0:1]
        dffo, dgu, dh1, dg_po, dg_pf = _ffn_bwd(dh, s["ffo"], g_po, s["gu"], wdn, wgu, s["h1"], s["g_pf"])
        g_pm = s["g_pm"]
        if swap is not None:
            started, tok = start_exchange(l + 1, (*swap[:4], dh1))
            if exchange is not None:
                finish_exchange(exchange, tok)
            exchange = started
            g_pm = g_pm + tok[0:1, 0:1]
        dmix, dyab, dyc, dg_pm, dgg = _mix_out_bwd(dh1, s["mix"], g_pm, wo, s["yab"], s["yc"], s["gg"])
        dq, dk, dv, dbiases[l] = _attn_bwd(s["qkv"], s["kv_t"], dyc, s["yc"], s["lse"], s["bias"])
        dpa, dcw, dwbd, dps = _convpool_bwd(s["pa"], dyab, s["cw"], s["wbd"], s["ps"])
        dparts = [dpa, dq, dk, dv]
        dh, dg_pre = _inproj_bwd(dparts, wi, s["h"], s["g_pre"], dh1)
        F2, DFF = s["gu"].shape[1], s["ff"].shape[1]
        grads = [_wgrad_concat(s["xn"], dparts, "wgrad_in"),
                 _wgrad(s["y"], dmix, D, D, "wgrad_out"),
                 _wgrad(s["hn"], dgu, D, F2 // 4, "wgrad_gate_up"),
                 _wgrad(s["ff"], dffo, DFF // 2, D, "wgrad_down")]
        small_grads[l] = [dcw[:3], _diag_blocks(dwbd, G), dps[0], None, dgg[0], dg_pre[0], dg_pm[0], dg_pf[0], dg_po[0]]
        views = [g.reshape(2, g.shape[0] // 2, g.shape[1]) if k == "col" else g.reshape(4, 2, g.shape[0] // 8, g.shape[1])
                 for g, k in zip(grads, kinds)]
        lands = [lax.empty((1,) + v.shape[1:] if v.ndim == 3 else (v.shape[0], 1) + v.shape[2:], F32) for v in views]
        swap = _split_start(f"grad_swap_start_{l}", views + lands, splan, 4)
    last, tok = start_exchange(0, (*swap[:4], swap[3]))
    if exchange is not None:
        finish_exchange(exchange, tok)
    for l in range(L):
        small_grads[l][3] = _bias_fold(dbiases[l], tok)[:, 0, :NREL]

    names_shapes = [(L, 3, 4 * cs), pool_w.shape, pool_scale.shape, rel_bias.shape, group_gain.shape,
                    pre_mix_g.shape, post_mix_g.shape, pre_ffn_g.shape, post_ffn_g.shape]
    small_stacked = [jnp.stack([small_grads[l][k] for l in range(L)]) for k in range(len(names_shapes))]
    packed = _pack(small_stacked)
    M = packed.shape[0]
    total = _sum_slots(_allgather_small(packed).reshape(8, M, LANES), M, "small_sum_devices")
    g_small = _unpack(total, names_shapes)
    g_small[0] = lax.dynamic_slice_in_dim(g_small[0], chip * cs, cs, axis=2)
    finish_exchange(last, total)
    g_big = [jnp.stack([shard_grads[l][k] for l in range(L)]) for k in range(4)]

    def adam_big(w, g, m, v, name):
        shp = w.shape
        two = lambda a: a.reshape(shp[0] * shp[1], shp[2])
        return [o.reshape(shp) for o in _adamw(two(w), two(g), two(m), two(v), 256, name)]

    upd_in = adam_big(w_in, g_big[0], m_w_in, v_w_in, "adamw_in")
    upd_out = adam_big(w_out, g_big[1], m_w_out, v_w_out, "adamw_out")
    upd_gu = adam_big(w_gate_up, g_big[2], m_w_gate_up, v_w_gate_up, "adamw_gate_up")
    upd_dn = adam_big(w_down, g_big[3], m_w_down, v_w_down, "adamw_down")

    small_w = [conv_w, pool_w, pool_scale, rel_bias, group_gain, pre_mix_g, post_mix_g, pre_ffn_g, post_ffn_g]
    small_m = [m_conv_w, m_pool_w, m_pool_scale, m_rel_bias, m_group_gain, m_pre_mix_g, m_post_mix_g, m_pre_ffn_g, m_post_ffn_g]
    small_v = [v_conv_w, v_pool_w, v_pool_scale, v_rel_bias, v_group_gain, v_pre_mix_g, v_post_mix_g, v_pre_ffn_g, v_post_ffn_g]
    pw_, pg_, pm_, pv_ = _pack(small_w), _pack(g_small), _pack(small_m), _pack(small_v)
    shapes = [w.shape for w in small_w]
    upd_small = [_unpack(o, shapes) for o in _adamw(pw_, pg_, pm_, pv_, pw_.shape[0], "adamw_small")]

    def ordered(big4, small9):
        return [big4[0], big4[1], *small9, big4[2], big4[3]]

    grads = ordered(g_big, g_small)
    outs = [ordered([upd_in[k], upd_out[k], upd_gu[k], upd_dn[k]], upd_small[k]) for k in range(3)]
    return (loss, dh[None], *grads, *outs[0], *outs[1], *outs[2])
```

```python
import functools

import jax
import jax.numpy as jnp
from jax import lax
from jax.experimental import pallas as pl
from jax.experimental.pallas import tpu as pltpu

F32, BF16 = jnp.float32, jnp.bfloat16
EPS = 1e-6
CHUNK = 64
LEFT_CHUNKS = 8
REL_CLIP = 128
HEAD_DIM = 64
LANES = 128
POOL_WINDOWS = (2, 4, 8, 16)
HALO = 16
TB = LEFT_CHUNKS * CHUNK
TBF = 256
TBW = 1024
BAND = (LEFT_CHUNKS + 1) * CHUNK
SUB = 2 * CHUNK
BANDW = SUB + LEFT_CHUNKS * CHUNK
SKEW = 768
NEG = -1e30
RBP = 384
VMEM_LIMIT = 56 * 1024 * 1024
ADAM_LR, ADAM_B1, ADAM_B2, ADAM_EPS, ADAM_WD, ADAM_STEP = 0.001, 0.9, 0.999, 1e-08, 0.01, 10
MESH = pl.DeviceIdType.MESH
ANY = pl.BlockSpec(memory_space=pl.ANY)


def _params(*sem):
    kw = dict(vmem_limit_bytes=VMEM_LIMIT)
    if sem:
        kw["dimension_semantics"] = sem
    return pltpu.CompilerParams(**kw)


def _dot(a, b):
    return jnp.dot(a, b, preferred_element_type=F32)


def _dot_nt(a, b):
    return lax.dot_general(a, b, (((1,), (1,)), ((), ())), preferred_element_type=F32)


def _dot_tn(a, b):
    return lax.dot_general(a, b, (((0,), (0,)), ((), ())), preferred_element_type=F32)


def _rms(x, g):
    r = lax.rsqrt(jnp.mean(x * x, axis=-1, keepdims=True) + EPS)
    return x * r * g


def _rms_bwd(dy, x, g):
    r = lax.rsqrt(jnp.mean(x * x, axis=-1, keepdims=True) + EPS)
    xh = x * r
    dxh = dy * g
    dx = r * (dxh - xh * jnp.mean(dxh * xh, axis=-1, keepdims=True))
    return dx, jnp.sum(dy * xh, axis=0, keepdims=True)


def _full(shape):
    return pl.BlockSpec(shape, lambda *_: (0,) * len(shape))


def _acc_init(step, *refs):
    @pl.when(step == 0)
    def _():
        for r in refs:
            r[...] = jnp.zeros_like(r)


def _inproj_fwd(h, g, w):
    T, D = h.shape
    NQ = w.shape[1] - D
    NKV = 2 * NQ // 3

    def body(h_ref, g_ref, w_ref, xn_ref, pa_ref, qkv_ref, kvt_ref):
        xn = _rms(h_ref[...], g_ref[...]).astype(BF16)
        xn_ref[...] = xn
        pa_ref[...] = _dot(xn, w_ref[:, :D])
        qkv = _dot(xn, w_ref[:, D:])
        qkv_ref[...] = qkv.astype(BF16)
        kvt_ref[...] = qkv[:, NQ - NKV:].T.astype(BF16)

    row = lambda n: pl.BlockSpec((TB, n), lambda i: (i, 0))
    return pl.pallas_call(
        body, name="inproj_fwd", grid=(T // TB,),
        in_specs=[row(D), _full((1, D)), _full(w.shape)],
        out_specs=[row(D), row(D), row(NQ), pl.BlockSpec((NKV, TB), lambda i: (0, i))],
        out_shape=[jax.ShapeDtypeStruct((T, D), BF16), jax.ShapeDtypeStruct((T, D), F32),
                   jax.ShapeDtypeStruct((T, NQ), BF16), jax.ShapeDtypeStruct((NKV, T), BF16)],
        compiler_params=_params("parallel"),
    )(h, g, w)


def _lane_groups(n, vals):
    lane = lax.broadcasted_iota(jnp.int32, (1, n), 1)
    q = n // 4
    return jnp.where(lane < q, vals[0], jnp.where(lane < 2 * q, vals[1], jnp.where(lane < 3 * q, vals[2], vals[3]))).astype(F32)


def _pick_group(levels, n):
    lane = lax.broadcasted_iota(jnp.int32, (1, n), 1)
    q = n // 4
    return jnp.where(lane < q, levels[0], jnp.where(lane < 2 * q, levels[1], jnp.where(lane < 3 * q, levels[2], levels[3])))


def _pool_count(blk, n):
    t1 = (blk * TB + 1 + lax.broadcasted_iota(jnp.int32, (TB, 1), 0)).astype(F32)
    return jnp.minimum(t1, _lane_groups(n, POOL_WINDOWS))


def _pool_d(pu, pu_halo, cnt):
    e = jnp.concatenate([pu_halo, pu], axis=0)
    s2 = e + pltpu.roll(e, 1, 0)
    s4 = s2 + pltpu.roll(s2, 2, 0)
    s8 = s4 + pltpu.roll(s4, 4, 0)
    s16 = s8 + pltpu.roll(s8, 8, 0)
    num = _pick_group([s2, s4, s8, s16], pu.shape[1])[HALO:]
    return num / cnt - pu


def _conv_taps(z, z_halo):
    e = jnp.concatenate([z_halo, z], axis=0)
    return pltpu.roll(e, 1, 0)[HALO:], pltpu.roll(e, 2, 0)[HALO:]


def _convpool_fwd(pa, cw, wbd, ps):
    T, D = pa.shape
    DC = D // 4

    def body(pa_ref, halo_ref, cw_ref, wbd_ref, ps_ref, yab_ref):
        i = pl.program_id(0)
        x = pa_ref[...]
        hl = jnp.where(i > 0, halo_ref[...], 0.0)
        gb, gc, u, pu = (x[:, k * DC:(k + 1) * DC] for k in range(4))
        z = gc * u
        z1, z2 = _conv_taps(z, hl[:, DC:2 * DC] * hl[:, 2 * DC:3 * DC])
        cwv = cw_ref[...]
        ya = gb * (cwv[2:3] * z + cwv[1:2] * z1 + cwv[0:1] * z2)
        d = _pool_d(pu, hl[:, 3 * DC:], _pool_count(i, DC))
        yb = _dot(d.astype(BF16), wbd_ref[...]) * ps_ref[...]
        yab_ref[...] = jnp.concatenate([ya, yb], axis=1)

    return pl.pallas_call(
        body, name="convpool_fwd", grid=(T // TB,),
        in_specs=[pl.BlockSpec((TB, D), lambda i: (i, 0)),
                  pl.BlockSpec((HALO, D), lambda i: (jnp.maximum(i * (TB // HALO) - 1, 0), 0)),
                  _full((8, DC)), _full((DC, DC)), _full((1, DC))],
        out_specs=pl.BlockSpec((TB, 2 * DC), lambda i: (i, 0)),
        out_shape=jax.ShapeDtypeStruct((T, 2 * DC), F32),
        compiler_params=_params("parallel"),
    )(pa, pa, cw, wbd, ps)


def _bias_bins(shape, col_dim):
    j = lax.broadcasted_iota(jnp.int32, shape, col_dim)
    b = lax.broadcasted_iota(jnp.int32, shape, 1 - col_dim)
    d = jnp.where(j < BAND, j, j - SKEW)
    live = jnp.logical_or(j < BAND, j > SKEW - CHUNK)
    bins = jnp.minimum(TB - d, REL_CLIP) + REL_CLIP
    return jnp.where(jnp.logical_and(live, bins == b), 1.0, 0.0).astype(F32)


def _bias_build(rb):
    H = rb.shape[0]

    def body(rb_ref, o_ref):
        v = jnp.dot(jnp.broadcast_to(rb_ref[0], (8, RBP)), _bias_bins((RBP, SKEW), 1),
                    precision=lax.Precision.HIGHEST, preferred_element_type=F32)
        x = jnp.broadcast_to(v[0:1], (SUB, SKEW))
        row = lax.broadcasted_iota(jnp.int32, (SUB, SKEW), 0)
        for b in range(SUB.bit_length() - 1):
            x = jnp.where(((row >> b) & 1) == 1, pltpu.roll(x, 1 << b, 1), x)
        qc = lax.broadcasted_iota(jnp.int32, (SUB, BANDW), 0) >> (CHUNK.bit_length() - 1)
        kc = lax.broadcasted_iota(jnp.int32, (SUB, BANDW), 1) >> (CHUNK.bit_length() - 1)
        o_ref[0] = jnp.where(jnp.logical_and(kc >= qc, kc <= qc + LEFT_CHUNKS), x[:, :BANDW], NEG).T

    return pl.pallas_call(
        body, name="bias_build", grid=(H,),
        in_specs=[pl.BlockSpec((1, 1, RBP), lambda h: (h, 0, 0))],
        out_specs=pl.BlockSpec((1, BANDW, SUB), lambda h: (h, 0, 0)),
        out_shape=jax.ShapeDtypeStruct((H, BANDW, SUB), F32),
        compiler_params=_params("parallel"),
    )(rb)


def _bias_fold(db, after):
    H = db.shape[0]

    def body(db_ref, after_ref, o_ref):
        x = jnp.concatenate([db_ref[0].T, jnp.zeros((SUB, SKEW - BANDW), F32)], axis=1)
        row = lax.broadcasted_iota(jnp.int32, (SUB, SKEW), 0)
        for b in range(SUB.bit_length() - 1):
            x = jnp.where(((row >> b) & 1) == 1, pltpu.roll(x, SKEW - (1 << b), 1), x)
        dsum = jnp.sum(x, axis=0, keepdims=True)
        o_ref[0] = jnp.dot(jnp.broadcast_to(dsum, (8, SKEW)), _bias_bins((SKEW, RBP), 0),
                           precision=lax.Precision.HIGHEST, preferred_element_type=F32)

    return pl.pallas_call(
        body, name="bias_fold", grid=(H,),
        in_specs=[pl.BlockSpec((1, BANDW, SUB), lambda h: (h, 0, 0)), ANY],
        out_specs=pl.BlockSpec((1, 8, RBP), lambda h: (h, 0, 0)),
        out_shape=jax.ShapeDtypeStruct((H, 8, RBP), F32),
        compiler_params=_params("parallel"),
    )(db, after)


def _head_masks():
    lane = lax.broadcasted_iota(jnp.int32, (1, LANES), 1)
    sub = lax.broadcasted_iota(jnp.int32, (LANES, 1), 0)
    return (lane < HEAD_DIM, sub < HEAD_DIM), (lane >= HEAD_DIM, sub >= HEAD_DIM)


def _key_tiles(s, first_block):
    return [t for t in range(s, s + BANDW // SUB) if not (first_block and t < TB // SUB)]


def _attn_fwd(qkv, kv_t, bias_t):
    T = qkv.shape[0]
    NP = qkv.shape[1] // (3 * LANES)
    NB = T // TB
    NS = TB // SUB
    scale = HEAD_DIM ** -0.5

    def body(q_ref, kc_ref, kp_ref, vtc_ref, vtp_ref, b_ref, o_ref, lse_ref):
        i = pl.program_id(1)

        def compute(first_block):
            q = q_ref[...] * scale
            kwin = jnp.concatenate([kp_ref[...], kc_ref[...]], axis=0)
            vt = jnp.concatenate([vtp_ref[...], vtc_ref[...]], axis=1)
            for s in range(NS):
                rows = slice(s * SUB, (s + 1) * SUB)
                out_t = None
                for a, (lane_m, sub_m) in enumerate(_head_masks()):
                    qa = jnp.where(lane_m, q[rows], 0)
                    tiles = _key_tiles(s, first_block)
                    keys = slice(tiles[0] * SUB, (tiles[-1] + 1) * SUB)
                    st = _dot_nt(kwin[keys], qa) + b_ref[a, (tiles[0] - s) * SUB:(tiles[-1] - s + 1) * SUB, :]
                    m = jnp.max(st, axis=0, keepdims=True)
                    p = jnp.exp(st - m)
                    l = jnp.sum(p, axis=0, keepdims=True)
                    o_a = _dot(jnp.where(sub_m, vt[:, keys], 0), p.astype(BF16)) * (1.0 / l)
                    out_t = o_a if out_t is None else out_t + o_a
                    lse_ref[0, 0, a * NS + s:a * NS + s + 1, :] = m + jnp.log(l)
                o_ref[rows, :] = out_t.T

        pl.when(i == 0)(functools.partial(compute, True))
        pl.when(i > 0)(functools.partial(compute, False))

    prev = lambda i: jnp.maximum(i - 1, 0)
    return pl.pallas_call(
        body, name="attn_fwd", grid=(NP, NB),
        in_specs=[pl.BlockSpec((TB, LANES), lambda p, i: (i, p)),
                  pl.BlockSpec((TB, LANES), lambda p, i: (i, NP + p)),
                  pl.BlockSpec((TB, LANES), lambda p, i: (prev(i), NP + p)),
                  pl.BlockSpec((LANES, TB), lambda p, i: (NP + p, i)),
                  pl.BlockSpec((LANES, TB), lambda p, i: (NP + p, prev(i))),
                  pl.BlockSpec((2, BANDW, SUB), lambda p, i: (p, 0, 0))],
        out_specs=[pl.BlockSpec((TB, LANES), lambda p, i: (i, p)),
                   pl.BlockSpec((1, 1, 8, LANES), lambda p, i: (p, i, 0, 0))],
        out_shape=[jax.ShapeDtypeStruct((T, NP * LANES), F32), jax.ShapeDtypeStruct((NP, NB, 8, LANES), F32)],
        compiler_params=_params("parallel", "parallel"),
    )(qkv, qkv, qkv, kv_t, kv_t, bias_t)


def _group_bounds(D):
    return ((0, D // 4), (D // 4, D // 2), (D // 2, D))


def _load_resident(step, *pairs_and_sems):
    @pl.when(step == 0)
    def _():
        cps = [pltpu.make_async_copy(src, dst, sem) for src, dst, sem in pairs_and_sems]
        for cp in cps:
            cp.start()
        for cp in cps:
            cp.wait()


def _layer_tail_fwd(yab, yc, gg, w_out, h, g_pm, g_pf, w_gu, w_dn, g_po):
    T, D = h.shape
    F2 = w_gu.shape[1]
    DFF = F2 // 2
    TF = DFF // 2

    def body(yab_ref, yc_ref, gg_ref, wo_hbm, h_ref, gpm_ref, gpf_ref, wgu_hbm, wdn_hbm, gpo_ref,
             y_ref, mix_ref, h1_ref, hn_ref, gu_ref, ff_ref, ffo_ref, h2_ref, wo_v, wgu_v, wdn_v, sems):
        _load_resident(pl.program_id(0), (wo_hbm, wo_v, sems.at[0]), (wgu_hbm, wgu_v, sems.at[1]), (wdn_hbm, wdn_v, sems.at[2]))
        yraw = jnp.concatenate([yab_ref[...], yc_ref[...]], axis=1)
        ggv = gg_ref[...]
        y = jnp.concatenate([_rms(yraw[:, a:b], ggv[:, a:b]) for a, b in _group_bounds(D)], axis=1).astype(BF16)
        y_ref[...] = y
        mix = _dot(y, wo_v[...])
        mix_ref[...] = mix.astype(BF16)
        h1 = h_ref[...] + _rms(mix, gpm_ref[...])
        h1_ref[...] = h1
        hn = _rms(h1, gpf_ref[...]).astype(BF16)
        hn_ref[...] = hn
        ffo = jnp.zeros((TBF, D), F32)
        for j in range(2):
            sg, su = slice(j * TF, (j + 1) * TF), slice(DFF + j * TF, DFF + (j + 1) * TF)
            gate = _dot(hn, wgu_v[:, sg])
            up = _dot(hn, wgu_v[:, su])
            gu_ref[:, sg] = gate.astype(BF16)
            gu_ref[:, su] = up.astype(BF16)
            ff = (gate * jax.nn.sigmoid(gate) * up).astype(BF16)
            ff_ref[:, sg] = ff
            ffo = ffo + _dot(ff, wdn_v[sg, :])
        ffo_ref[...] = ffo.astype(BF16)
        h2_ref[...] = h1 + _rms(ffo, gpo_ref[...])

    row = lambda n: pl.BlockSpec((TBF, n), lambda i: (i, 0))
    gain = _full((1, D))
    f32, bf16 = (lambda n: jax.ShapeDtypeStruct((T, n), F32)), (lambda n: jax.ShapeDtypeStruct((T, n), BF16))
    return pl.pallas_call(
        body, name="layer_tail_fwd", grid=(T // TBF,),
        in_specs=[row(D // 2), row(D // 2), gain, ANY, row(D), gain, gain, ANY, ANY, gain],
        out_specs=[row(D), row(D), row(D), row(D), row(F2), row(DFF), row(D), row(D)],
        out_shape=[bf16(D), bf16(D), f32(D), bf16(D), bf16(F2), bf16(DFF), bf16(D), f32(D)],
        scratch_shapes=[pltpu.VMEM(w_out.shape, BF16), pltpu.VMEM(w_gu.shape, BF16), pltpu.VMEM(w_dn.shape, BF16),
                        pltpu.SemaphoreType.DMA((3,))],
        compiler_params=_params("arbitrary"),
    )(yab, yc, gg, w_out, h, g_pm, g_pf, w_gu, w_dn, g_po)


def _loss_grad(h, tgt):
    T, D = h.shape

    def body(h_ref, t_ref, dh_ref, loss_ref):
        _acc_init(pl.program_id(0), loss_ref)
        diff = h_ref[...] - t_ref[...]
        dh_ref[...] = diff * (1.0 / D)
        loss_ref[...] += 0.5 * jnp.sum(jnp.mean(diff * diff, axis=-1, keepdims=True))

    row = pl.BlockSpec((TB, D), lambda i: (i, 0))
    return pl.pallas_call(
        body, name="loss_grad", grid=(T // TB,),
        in_specs=[row, row], out_specs=[row, _full((8, LANES))],
        out_shape=[jax.ShapeDtypeStruct((T, D), F32), jax.ShapeDtypeStruct((8, LANES), F32)],
        compiler_params=_params("arbitrary"),
    )(h, tgt)


def _layer_tail_bwd(dh2, ffo, g_po, gu, w_dn, w_gu, h1, g_pf, mix, g_pm, w_out, yab, yc, gg):
    T, D = dh2.shape
    F2 = gu.shape[1]
    DFF = F2 // 2
    TF = DFF // 2

    def body(dh_ref, ffo_ref, gpo_ref, gu_ref, wdn_hbm, wgu_hbm, h_ref, gpf_ref, mix_ref, gpm_ref, wo_hbm, yab_ref, yc_ref, gg_ref,
             dffo_ref, dgu_ref, dh1_ref, dmix_ref, dyab_ref, dyc_ref, dgpo_ref, dgpf_ref, dgpm_ref, dgg_ref,
             wdn_v, wgu_v, wo_v, sems):
        step = pl.program_id(0)
        _load_resident(step, (wdn_hbm, wdn_v, sems.at[0]), (wgu_hbm, wgu_v, sems.at[1]), (wo_hbm, wo_v, sems.at[2]))
        _acc_init(step, dgpo_ref, dgpf_ref, dgpm_ref, dgg_ref)
        dh = dh_ref[...]
        dffo, dg = _rms_bwd(dh, ffo_ref[...].astype(F32), gpo_ref[...])
        dgpo_ref[0:1, :] += dg
        dffo = dffo.astype(BF16)
        dffo_ref[...] = dffo
        dhn = jnp.zeros((TBF, D), F32)
        for j in range(2):
            sg, su = slice(j * TF, (j + 1) * TF), slice(DFF + j * TF, DFF + (j + 1) * TF)
            dff = _dot_nt(dffo, wdn_v[sg, :])
            gate, up = gu_ref[:, sg].astype(F32), gu_ref[:, su].astype(F32)
            sig = jax.nn.sigmoid(gate)
            dgate = (dff * up * (sig * (1.0 + gate * (1.0 - sig)))).astype(BF16)
            dup = (dff * (gate * sig)).astype(BF16)
            dgu_ref[:, sg] = dgate
            dgu_ref[:, su] = dup
            dhn = dhn + _dot_nt(dgate, wgu_v[:, sg]) + _dot_nt(dup, wgu_v[:, su])
        dx, dg = _rms_bwd(dhn, h_ref[...], gpf_ref[...])
        dgpf_ref[0:1, :] += dg
        dh1 = dh + dx
        dh1_ref[...] = dh1
        dmix, dg = _rms_bwd(dh1, mix_ref[...].astype(F32), gpm_ref[...])
        dgpm_ref[0:1, :] += dg
        dmix = dmix.astype(BF16)
        dmix_ref[...] = dmix
        dy = _dot_nt(dmix, wo_v[...])
        yraw = jnp.concatenate([yab_ref[...], yc_ref[...]], axis=1)
        ggv = gg_ref[...]
        parts = [_rms_bwd(dy[:, a:b], yraw[:, a:b], ggv[:, a:b]) for a, b in _group_bounds(D)]
        dgg_ref[0:1, :] += jnp.concatenate([p[1] for p in parts], axis=1)
        dyab_ref[...] = jnp.concatenate([parts[0][0], parts[1][0]], axis=1)
        dyc_ref[...] = parts[2][0]

    row = lambda n: pl.BlockSpec((TBF, n), lambda i: (i, 0))
    gain, acc = _full((1, D)), _full((8, D))
    f32, bf16 = (lambda n: jax.ShapeDtypeStruct((T, n), F32)), (lambda n: jax.ShapeDtypeStruct((T, n), BF16))
    acc_shape = jax.ShapeDtypeStruct((8, D), F32)
    return pl.pallas_call(
        body, name="layer_tail_bwd", grid=(T // TBF,),
        in_specs=[row(D), row(D), gain, row(F2), ANY, ANY, row(D), gain, row(D), gain, ANY, row(D // 2), row(D // 2), gain],
        out_specs=[row(D), row(F2), row(D), row(D), row(D // 2), row(D // 2), acc, acc, acc, acc],
        out_shape=[bf16(D), bf16(F2), f32(D), bf16(D), f32(D // 2), f32(D // 2), acc_shape, acc_shape, acc_shape, acc_shape],
        scratch_shapes=[pltpu.VMEM(w_dn.shape, BF16), pltpu.VMEM(w_gu.shape, BF16), pltpu.VMEM(w_out.shape, BF16),
                        pltpu.SemaphoreType.DMA((3,))],
        compiler_params=_params("arbitrary"),
    )(dh2, ffo, g_po, gu, w_dn, w_gu, h1, g_pf, mix, g_pm, w_out, yab, yc, gg)


def _attn_bwd(qkv, kv_t, dyc, yc, lse, bias_t, after):
    T = qkv.shape[0]
    NP = qkv.shape[1] // (3 * LANES)
    NB = T // TB
    NS = TB // SUB
    scale = HEAD_DIM ** -0.5

    def body(q_ref, kc_ref, kp_ref, vc_ref, vp_ref, ktc_ref, ktp_ref, do_ref, o_ref, lse_ref, b_ref, after_ref,
             dq_ref, dk_ref, dv_ref, db_ref, dk_carry, dv_carry, dkw, dvw):
        step = pl.program_id(1)
        i = NB - 1 - step
        _acc_init(step, dk_carry, dv_carry, db_ref)

        def compute(first_block):
            q = q_ref[...] * scale
            kwin = jnp.concatenate([kp_ref[...], kc_ref[...]], axis=0)
            vwin = jnp.concatenate([vp_ref[...], vc_ref[...]], axis=0)
            kt = jnp.concatenate([ktp_ref[...], ktc_ref[...]], axis=1)
            do = do_ref[...]
            dob = do.astype(BF16)
            prod = do * o_ref[...]
            ones = jnp.ones((8, LANES), F32)
            heads = []
            for lane_m, sub_m in _head_masks():
                delta = lax.dot_general(ones, jnp.where(lane_m, prod, 0.0), (((1,), (1,)), ((), ())),
                                        precision=lax.Precision.HIGHEST, preferred_element_type=F32)
                heads.append((jnp.where(lane_m, q, 0), jnp.where(lane_m, dob, 0), delta, sub_m))
            written = set()
            for s in range(NS):
                rows = slice(s * SUB, (s + 1) * SUB)
                tiles = _key_tiles(s, first_block)
                keys = slice(tiles[0] * SUB, (tiles[-1] + 1) * SUB)
                brows = slice((tiles[0] - s) * SUB, (tiles[-1] - s + 1) * SUB)
                dk_c = dv_c = dqt = None
                for a, (qa, doa, delta, sub_m) in enumerate(heads):
                    st = _dot_nt(kwin[keys], qa[rows]) + b_ref[a, brows, :]
                    p = jnp.exp(st - lse_ref[0, 0, a * NS + s:a * NS + s + 1, :])
                    ds = p * (_dot_nt(vwin[keys], doa[rows]) - delta[0:1, rows])
                    db_ref[a, brows, :] += ds
                    dsb = ds.astype(BF16)
                    terms = (_dot(dsb, qa[rows]), _dot(p.astype(BF16), doa[rows]), _dot(jnp.where(sub_m, kt[:, keys], 0), dsb))
                    dk_c, dv_c, dqt = terms if a == 0 else (dk_c + terms[0], dv_c + terms[1], dqt + terms[2])
                dq_ref[rows, :] = (dqt.T * scale).astype(BF16)
                for n, t in enumerate(tiles):
                    win, loc = slice(t * SUB, (t + 1) * SUB), slice(n * SUB, (n + 1) * SUB)
                    if t in written:
                        dkw[win, :] += dk_c[loc]
                        dvw[win, :] += dv_c[loc]
                    else:
                        dkw[win, :] = dk_c[loc]
                        dvw[win, :] = dv_c[loc]
                        written.add(t)
            dk_ref[...] = (dkw[TB:, :] + dk_carry[...]).astype(BF16)
            dv_ref[...] = (dvw[TB:, :] + dv_carry[...]).astype(BF16)
            if not first_block:
                dk_carry[...] = dkw[:TB, :]
                dv_carry[...] = dvw[:TB, :]

        pl.when(i == 0)(functools.partial(compute, True))
        pl.when(i > 0)(functools.partial(compute, False))

    blk = lambda s: NB - 1 - s
    prev = lambda s: jnp.maximum(NB - 2 - s, 0)
    rows = lambda which, off: pl.BlockSpec((TB, LANES), lambda p, s: (which(s), off + p))
    out = jax.ShapeDtypeStruct((T, NP * LANES), BF16)
    strip = pl.BlockSpec((2, BANDW, SUB), lambda p, s: (p, 0, 0))
    return pl.pallas_call(
        body, name="attn_bwd", grid=(NP, NB),
        in_specs=[rows(blk, 0), rows(blk, NP), rows(prev, NP), rows(blk, 2 * NP), rows(prev, 2 * NP),
                  pl.BlockSpec((LANES, TB), lambda p, s: (p, blk(s))), pl.BlockSpec((LANES, TB), lambda p, s: (p, prev(s))),
                  rows(blk, 0), rows(blk, 0), pl.BlockSpec((1, 1, 8, LANES), lambda p, s: (p, blk(s), 0, 0)), strip, ANY],
        out_specs=[rows(blk, 0), rows(blk, 0), rows(blk, 0), strip],
        out_shape=[out, out, out, jax.ShapeDtypeStruct((2 * NP, BANDW, SUB), F32)],
        scratch_shapes=[pltpu.VMEM((TB, LANES), F32), pltpu.VMEM((TB, LANES), F32),
                        pltpu.VMEM((2 * TB, LANES), F32), pltpu.VMEM((2 * TB, LANES), F32)],
        compiler_params=_params("arbitrary", "arbitrary"),
    )(qkv, qkv, qkv, qkv, qkv, kv_t, kv_t, dyc, yc, lse, bias_t, after)


def _convpool_bwd(pa, dyab, cw, wbd, ps):
    T, D = pa.shape
    DC = D // 4
    NB = T // TB
    N = TB + HALO

    def body(pa_ref, halo_ref, dy_ref, cw_ref, wbd_ref, ps_ref, dpa_ref, dcw_ref, dwbd_ref, dps_ref, dc_carry, e_carry):
        step = pl.program_id(0)
        i = NB - 1 - step
        _acc_init(step, dcw_ref, dwbd_ref, dps_ref, dc_carry, e_carry)
        x = pa_ref[...]
        hl = jnp.where(i > 0, halo_ref[...], 0.0)
        gb, gc, u, pu = (x[:, k * DC:(k + 1) * DC] for k in range(4))
        dy = dy_ref[...]
        dya, dyb = dy[:, :DC], dy[:, DC:]
        cwv = cw_ref[...]
        z = gc * u
        z1, z2 = _conv_taps(z, hl[:, DC:2 * DC] * hl[:, 2 * DC:3 * DC])
        dgb = dya * (cwv[2:3] * z + cwv[1:2] * z1 + cwv[0:1] * z2)
        dconv = dya * gb
        dcw_ref[0:1, :] += jnp.sum(dconv * z2, axis=0, keepdims=True)
        dcw_ref[1:2, :] += jnp.sum(dconv * z1, axis=0, keepdims=True)
        dcw_ref[2:3, :] += jnp.sum(dconv * z, axis=0, keepdims=True)
        ext = jnp.concatenate([dconv, dc_carry[...]], axis=0)
        dz = cwv[2:3] * dconv + cwv[1:2] * pltpu.roll(ext, N - 1, 0)[:TB] + cwv[0:1] * pltpu.roll(ext, N - 2, 0)[:TB]
        dc_carry[...] = dconv[:HALO]
        cnt = _pool_count(i, DC)
        d = _pool_d(pu, hl[:, 3 * DC:], cnt).astype(BF16)
        psv = ps_ref[...]
        w = wbd_ref[...]
        dps_ref[0:1, :] += jnp.sum(dyb * _dot(d, w), axis=0, keepdims=True)
        dys = (dyb * psv).astype(BF16)
        dwbd_ref[...] += _dot_tn(d, dys)
        dd = _dot_nt(dys, w)
        e = dd / cnt
        ext = jnp.concatenate([e, e_carry[...]], axis=0)
        a2 = ext + pltpu.roll(ext, N - 1, 0)
        a4 = a2 + pltpu.roll(a2, N - 2, 0)
        a8 = a4 + pltpu.roll(a4, N - 4, 0)
        a16 = a8 + pltpu.roll(a8, N - 8, 0)
        dpu = _pick_group([a2, a4, a8, a16], DC)[:TB] - dd
        e_carry[...] = e[:HALO]
        dpa_ref[...] = jnp.concatenate([dgb, dz * u, dz * gc, dpu], axis=1).astype(BF16)

    blk = lambda n: pl.BlockSpec((TB, n), lambda s: (NB - 1 - s, 0))
    return pl.pallas_call(
        body, name="convpool_bwd", grid=(NB,),
        in_specs=[blk(D), pl.BlockSpec((HALO, D), lambda s: (jnp.maximum((NB - 1 - s) * (TB // HALO) - 1, 0), 0)),
                  blk(2 * DC), _full((8, DC)), _full((DC, DC)), _full((1, DC))],
        out_specs=[blk(D), _full((8, DC)), _full((DC, DC)), _full((8, DC))],
        out_shape=[jax.ShapeDtypeStruct((T, D), BF16), jax.ShapeDtypeStruct((8, DC), F32),
                   jax.ShapeDtypeStruct((DC, DC), F32), jax.ShapeDtypeStruct((8, DC), F32)],
        scratch_shapes=[pltpu.VMEM((HALO, DC), F32), pltpu.VMEM((HALO, DC), F32)],
        compiler_params=_params("arbitrary"),
    )(pa, pa, dyab, cw, wbd, ps)


def _inproj_bwd(dparts, w, h, g, dh1):
    T, D = h.shape
    widths = [p.shape[1] for p in dparts]
    n = len(dparts)

    def body(*refs):
        parts, (w_ref, h_ref, g_ref, dh1_ref, dh_ref, dg_ref) = refs[:n], refs[n:]
        _acc_init(pl.program_id(0), dg_ref)
        dxn, off = jnp.zeros((TB, D), F32), 0
        for p_ref, wd in zip(parts, widths):
            dxn = dxn + _dot_nt(p_ref[...], w_ref[:, off:off + wd])
            off += wd
        dx, dg = _rms_bwd(dxn, h_ref[...], g_ref[...])
        dg_ref[0:1, :] += dg
        dh_ref[...] = dh1_ref[...] + dx

    row = lambda m: pl.BlockSpec((TB, m), lambda i: (i, 0))
    return pl.pallas_call(
        body, name="inproj_bwd", grid=(T // TB,),
        in_specs=[row(wd) for wd in widths] + [_full(w.shape), row(D), _full((1, D)), row(D)],
        out_specs=[row(D), _full((8, D))],
        out_shape=[jax.ShapeDtypeStruct((T, D), F32), jax.ShapeDtypeStruct((8, D), F32)],
        compiler_params=_params("arbitrary"),
    )(*dparts, w, h, g, dh1)


def _wgrad(a, b, tk, tn, name):
    T, K = a.shape
    N = b.shape[1]

    def body(a_ref, b_ref, o_ref):
        _acc_init(pl.program_id(2), o_ref)
        o_ref[...] += _dot_tn(a_ref[...], b_ref[...])

    return pl.pallas_call(
        body, name=name, grid=(K // tk, N // tn, T // TBW),
        in_specs=[pl.BlockSpec((TBW, tk), lambda k, n, t: (t, k)), pl.BlockSpec((TBW, tn), lambda k, n, t: (t, n))],
        out_specs=pl.BlockSpec((tk, tn), lambda k, n, t: (k, n)),
        out_shape=jax.ShapeDtypeStruct((K, N), F32),
        compiler_params=_params("parallel", "parallel", "arbitrary"),
    )(a, b)


def _wgrad_concat(a, bs, name):
    T, K = a.shape
    widths = [b.shape[1] for b in bs]
    n = len(bs)

    def body(*refs):
        a_ref, b_refs, o_ref = refs[0], refs[1:1 + n], refs[1 + n]
        _acc_init(pl.program_id(0), o_ref)
        av, off = a_ref[...], 0
        for b_ref, wd in zip(b_refs, widths):
            o_ref[:, off:off + wd] += _dot_tn(av, b_ref[...])
            off += wd

    row = lambda m: pl.BlockSpec((TB, m), lambda t: (t, 0))
    return pl.pallas_call(
        body, name=name, grid=(T // TB,),
        in_specs=[row(K)] + [row(wd) for wd in widths],
        out_specs=_full((K, sum(widths))),
        out_shape=jax.ShapeDtypeStruct((K, sum(widths)), F32),
        compiler_params=_params("arbitrary"),
    )(a, *bs)


def _adamw(w, g, m, v, tr, name):
    R, C = w.shape

    def body(w_ref, g_ref, m_ref, v_ref, d_ref, nm_ref, nv_ref):
        gv = g_ref[...]
        nm = ADAM_B1 * m_ref[...] + (1.0 - ADAM_B1) * gv
        nv = ADAM_B2 * v_ref[...] + (1.0 - ADAM_B2) * (gv * gv)
        m_hat = nm / (1.0 - ADAM_B1 ** ADAM_STEP)
        v_hat = nv / (1.0 - ADAM_B2 ** ADAM_STEP)
        d_ref[...] = -ADAM_LR * (m_hat / (jnp.sqrt(v_hat) + ADAM_EPS) + ADAM_WD * w_ref[...])
        nm_ref[...] = nm
        nv_ref[...] = nv

    blk = pl.BlockSpec((tr, C), lambda i: (i, 0))
    out = jax.ShapeDtypeStruct((R, C), F32)
    return pl.pallas_call(
        body, name=name, grid=(R // tr,), in_specs=[blk] * 4, out_specs=[blk] * 3, out_shape=[out] * 3,
        compiler_params=_params("parallel"),
    )(w, g, m, v)


def _sum_slots(x, tr, name):
    n, R, C = x.shape

    def body(x_ref, o_ref):
        acc = x_ref[0]
        for k in range(1, n):
            acc = acc + x_ref[k]
        o_ref[...] = acc

    return pl.pallas_call(
        body, name=name, grid=(R // tr,),
        in_specs=[pl.BlockSpec((n, tr, C), lambda i: (0, i, 0))],
        out_specs=pl.BlockSpec((tr, C), lambda i: (i, 0)),
        out_shape=jax.ShapeDtypeStruct((R, C), F32),
        compiler_params=_params("parallel"),
    )(x)


def _add_half(view, other, c, name):
    if view.ndim == 3:
        _, R2, N = view.shape
        tr = 128
        grid = (R2 // tr,)
        in_specs = [pl.BlockSpec((1, tr, N), lambda i, c_ref: (c_ref[0], i, 0)), pl.BlockSpec((1, tr, N), lambda i, c_ref: (0, i, 0))]
        out_spec = pl.BlockSpec((tr, N), lambda i, c_ref: (i, 0))
        out_shape = jax.ShapeDtypeStruct((R2, N), BF16)

        def body(c_ref, a_ref, b_ref, o_ref):
            o_ref[...] = (a_ref[0] + b_ref[0]).astype(BF16)
    else:
        S, _, R2, C = view.shape
        grid = (S,)
        in_specs = [pl.BlockSpec((1, 1, R2, C), lambda s, c_ref: (s, c_ref[0], 0, 0)), pl.BlockSpec((1, 1, R2, C), lambda s, c_ref: (s, 0, 0, 0))]
        out_spec = pl.BlockSpec((1, R2, C), lambda s, c_ref: (s, 0, 0))
        out_shape = jax.ShapeDtypeStruct((S, R2, C), BF16)

        def body(c_ref, a_ref, b_ref, o_ref):
            o_ref[0] = (a_ref[0, 0] + b_ref[0, 0]).astype(BF16)

    return pl.pallas_call(
        body, name=name,
        grid_spec=pltpu.PrefetchScalarGridSpec(num_scalar_prefetch=1, grid=grid, in_specs=in_specs, out_specs=out_spec),
        out_shape=out_shape, compiler_params=_params("parallel"),
    )(c, view, other)


def _sum_chips(slots, part, kind, place, name):
    _, R2, C = slots.shape
    tr = min(R2, 128) if kind == "col" else R2

    def body(p_ref, s_ref, own_ref, o_ref):
        me = p_ref[0]
        own = own_ref[...] if kind == "col" else own_ref[0]
        acc = None
        for k in range(4):
            term = jnp.where(me == k, own, s_ref[k]).astype(F32)
            acc = term if acc is None else acc + term
        o_ref[0] = acc

    own_spec = (pl.BlockSpec((tr, C), lambda i, p: (i, p[0])) if kind == "col"
                else pl.BlockSpec((1, R2, C), lambda i, p: (p[0], 0, 0)))
    return pl.pallas_call(
        body, name=name,
        grid_spec=pltpu.PrefetchScalarGridSpec(
            num_scalar_prefetch=1, grid=(R2 // tr,),
            in_specs=[pl.BlockSpec((4, tr, C), lambda i, p: (0, i, 0)), own_spec],
            out_specs=pl.BlockSpec((1, tr, C), lambda i, p: (p[1], i, 0))),
        out_shape=jax.ShapeDtypeStruct((2, R2, C), F32), compiler_params=_params("parallel"),
    )(place, slots, part)


def _place():
    x, y, c = lax.axis_index("x"), lax.axis_index("y"), lax.axis_index("c")
    chips = [(1 - x, y), (x, 1 - y), (1 - x, 1 - y)]
    return x, y, c, 2 * x + y, chips


def _remote(src, dst, send_sems, recv_sems, k, to):
    return pltpu.make_async_remote_copy(src_ref=src, dst_ref=dst, send_sem=send_sems.at[k], recv_sem=recv_sems.at[k],
                                        device_id=to, device_id_type=MESH)


def _full_shape(shard_shape, kind):
    *lead, R, C = shard_shape
    return (*lead, R, 4 * C) if kind == "col" else (*lead, 4 * R, C)


def _slab(full_ref, shard_rc, kind, chip, half=None):
    R, C = shard_rc
    lead = (slice(None),) * (len(full_ref.shape) - 2)
    if kind == "col":
        rows = pl.ds(0, R) if half is None else pl.ds(half * (R // 2), R // 2)
        return full_ref.at[(*lead, rows, pl.ds(chip * C, C))]
    rows = pl.ds(chip * R, R) if half is None else pl.ds(chip * R + half * (R // 2), R // 2)
    return full_ref.at[(*lead, rows, slice(None))]


def _row_half(ref, half):
    R = ref.shape[-2]
    lead = (slice(None),) * (len(ref.shape) - 2)
    return ref.at[(*lead, pl.ds(half * (R // 2), R // 2), slice(None))]


def _allgather_weights(shards, kinds, after):
    n = len(shards)
    full_shapes = [_full_shape(s.shape, k) for s, k in zip(shards, kinds)]

    def body(*refs):
        ins, outs, (send_sems, recv_sems) = refs[:n], refs[n + 1:2 * n + 1], refs[2 * n + 1:]
        x, y, c, me, chips = _place()
        slab = lambda a, chip, half=None: _slab(outs[a], shards[a].shape[-2:], kinds[a], chip, half)
        my_half = lambda a: _row_half(ins[a], c)
        own = [_remote(ins[a], slab(a, me), send_sems, recv_sems, 6 * n + a, (x, y, 1 - c)) for a in range(n)]
        first = [_remote(my_half(a), slab(a, me, c), send_sems, recv_sems, j * n + a, (*chip, c))
                 for j, chip in enumerate(chips) for a in range(n)]
        for cp in first + own:
            cp.start()
        passed = []
        for j, chip in enumerate(chips):
            src = 2 * chip[0] + chip[1]
            for a in range(n):
                _remote(my_half(a), slab(a, src, c), send_sems, recv_sems, j * n + a, (x, y, c)).wait_recv()
                cp = _remote(slab(a, src, c), slab(a, src, c), send_sems, recv_sems, (3 + j) * n + a, (x, y, 1 - c))
                cp.start()
                passed.append(cp)
        for j, chip in enumerate(chips):
            src = 2 * chip[0] + chip[1]
            for a in range(n):
                _remote(my_half(a), slab(a, src, 1 - c), send_sems, recv_sems, (3 + j) * n + a, (x, y, c)).wait_recv()
        for cp in own:
            cp.wait_recv()
        for cp in first + passed + own:
            cp.wait_send()

    return pl.pallas_call(
        body, name="allgather_weights",
        in_specs=[ANY] * (n + 1), out_specs=[ANY] * n,
        out_shape=[jax.ShapeDtypeStruct(s, BF16) for s in full_shapes],
        scratch_shapes=[pltpu.SemaphoreType.DMA((7 * n,)), pltpu.SemaphoreType.DMA((7 * n,))],
        compiler_params=pltpu.CompilerParams(has_side_effects=True),
    )(*shards, after)


HBM = pl.BlockSpec(memory_space=pltpu.HBM)
SEM = pl.BlockSpec(memory_space=pltpu.SEMAPHORE)
DATAFLOW = pltpu.SideEffectType.DATAFLOW_SIDE_EFFECTING


def _split_start(name, bufs, plan, ncopies, after=None):
    nb = len(bufs)
    nin = nb + (after is not None)

    def body(*refs):
        send_sems, recv_sems, token = refs[nin], refs[nin + 1], refs[-1]
        for k, (src, dst, to) in enumerate(plan(refs[:nb])):
            _remote(src, dst, send_sems, recv_sems, k, to).start()
        token[...] = jnp.zeros_like(token)

    out = pl.pallas_call(
        body, name=name,
        out_shape=(pltpu.SemaphoreType.DMA((ncopies,)), pltpu.SemaphoreType.DMA((ncopies,)),
                   *[pltpu.HBM(b.shape, b.dtype) for b in bufs], jax.ShapeDtypeStruct((8, LANES), F32)),
        in_specs=[HBM] * nb + [ANY] * (nin - nb), out_specs=(SEM, SEM, *[HBM] * nb, pl.BlockSpec(memory_space=pltpu.VMEM)),
        input_output_aliases={i: 2 + i for i in range(nb)},
        compiler_params=pltpu.CompilerParams(has_side_effects=DATAFLOW),
    )(*[pltpu.with_memory_space_constraint(b, pltpu.HBM) for b in bufs], *([] if after is None else [after]))
    return out[0], out[1], list(out[2:2 + nb]), out[-1]


def _split_wait(name, send_sems, recv_sems, bufs, plan, after):
    nb = len(bufs)

    def body(*refs):
        s_sems, r_sems = refs[nb], refs[nb + 1]
        for k, (src, dst, to) in enumerate(plan(refs[:nb])):
            cp = _remote(src, dst, s_sems, r_sems, k, to)
            cp.wait_send()
            cp.wait_recv()

    return pl.pallas_call(
        body, name=name,
        out_shape=tuple(pltpu.HBM(b.shape, b.dtype) for b in bufs),
        in_specs=[HBM] * nb + [SEM, SEM, ANY], out_specs=tuple([HBM] * nb),
        input_output_aliases={i: i for i in range(nb)},
        compiler_params=pltpu.CompilerParams(has_side_effects=DATAFLOW),
    )(*bufs, send_sems, recv_sems, after)


def _gather_plan(n, shard_rcs, kinds):
    def plan(refs):
        x, y, c, me, chips = _place()
        out = []
        for a in range(n):
            shard, full = refs[a], refs[n + a]
            out.append((shard, _slab(full, shard_rcs[a], kinds[a], me), (x, y, 1 - c)))
            for chip in chips:
                for cc in (c, 1 - c):
                    out.append((_row_half(shard, c), _slab(full, shard_rcs[a], kinds[a], me, c), (*chip, cc)))
        return out
    return plan


def _exchange_plan(n, kinds):
    def plan(refs):
        x, y, c, me, chips = _place()
        out = []
        for a in range(n):
            part, slots = refs[a], refs[n + a]
            C = slots.shape[2]
            for chip in chips:
                dst = 2 * chip[0] + chip[1]
                src = part.at[:, pl.ds(dst * C, C)] if kinds[a] == "col" else part.at[dst]
                out.append((src, slots.at[me], (*chip, c)))
        return out
    return plan


def _swap_plan(ndims):
    n = len(ndims)

    def plan(refs):
        x, y, c, _, _ = _place()
        out = []
        for a in range(n):
            src = refs[a].at[pl.ds(1 - c, 1)] if ndims[a] == 3 else refs[a].at[:, pl.ds(1 - c, 1)]
            out.append((src, refs[n + a], (x, y, 1 - c)))
        return out
    return plan


def _join_halves(bufs):
    n = len(bufs)

    def body(*refs):
        outs, (send_sems, recv_sems) = refs[n:2 * n], refs[2 * n:]
        x, y, c, _, _ = _place()
        sends = [_remote(outs[a].at[c], outs[a].at[c], send_sems, recv_sems, a, (x, y, 1 - c)) for a in range(n)]
        for cp in sends:
            cp.start()
        for a in range(n):
            _remote(outs[a].at[c], outs[a].at[1 - c], send_sems, recv_sems, a, (x, y, c)).wait_recv()
        for cp in sends:
            cp.wait_send()

    return pl.pallas_call(
        body, name="grad_join_halves", in_specs=[ANY] * n, out_specs=[ANY] * n,
        out_shape=[jax.ShapeDtypeStruct(b.shape, F32) for b in bufs],
        input_output_aliases={a: a for a in range(n)},
        scratch_shapes=[pltpu.SemaphoreType.DMA((n,)), pltpu.SemaphoreType.DMA((n,))],
        compiler_params=pltpu.CompilerParams(has_side_effects=True),
    )(*bufs)


def _allgather_small(block):
    M, N = block.shape

    def body(x_ref, out_ref, send_sems, recv_sems, local_sem):
        x, y, c, _, chips = _place()
        me, sibling = (x, y, c), (x, y, 1 - c)

        def rows(px, py, pc):
            return out_ref.at[pl.ds((4 * px + 2 * py + pc) * M, M), :]

        def copy(k, blk, to, src=None):
            return _remote(rows(*blk) if src is None else src, rows(*blk), send_sems, recv_sems, k, to)

        mine = pltpu.make_async_copy(x_ref, rows(*me), local_sem)
        mine.start()
        first = [copy(0, me, sibling, src=x_ref)] + [copy(1 + j, me, (*chip, c), src=x_ref) for j, chip in enumerate(chips)]
        for cp in first:
            cp.start()
        passed = [copy(4 + j, (*chip, c), sibling) for j, chip in enumerate(chips)]
        for j, chip in enumerate(chips):
            copy(1 + j, (*chip, c), me).wait_recv()
            passed[j].start()
        copy(0, sibling, me).wait_recv()
        for j, chip in enumerate(chips):
            copy(4 + j, (*chip, 1 - c), me).wait_recv()
        for cp in first + passed:
            cp.wait_send()
        mine.wait()

    vm = pl.BlockSpec(memory_space=pltpu.VMEM)
    return pl.pallas_call(
        body, name="allgather_small", in_specs=[vm], out_specs=vm,
        out_shape=jax.ShapeDtypeStruct((8 * M, N), F32),
        scratch_shapes=[pltpu.SemaphoreType.DMA((7,)), pltpu.SemaphoreType.DMA((7,)), pltpu.SemaphoreType.DMA],
        compiler_params=pltpu.CompilerParams(has_side_effects=True, vmem_limit_bytes=VMEM_LIMIT),
    )(block)


def _pack(arrays):
    flat = jnp.concatenate([a.reshape(-1) for a in arrays])
    pad = (-flat.shape[0]) % (8 * LANES)
    return jnp.pad(flat, (0, pad)).reshape(-1, LANES)


def _unpack(packed, shapes):
    flat, out, off = packed.reshape(-1), [], 0
    for s in shapes:
        size = 1
        for d in s:
            size *= d
        out.append(flat[off:off + size].reshape(s))
        off += size
    return out


def _block_diag(pw):
    G, n, _ = pw.shape
    eye = jnp.eye(G, dtype=pw.dtype)
    return (eye[:, None, :, None] * pw[:, :, None, :]).reshape(G * n, G * n)


def _diag_blocks(m, G):
    n = m.shape[0] // G
    return jnp.stack([m[g * n:(g + 1) * n, g * n:(g + 1) * n] for g in range(G)])


def _pad_rows(a, rows):
    return jnp.pad(a, ((0, rows - a.shape[0]), (0, 0)))


def kernel(x, w_in, w_out, conv_w, pool_w, pool_scale, rel_bias, group_gain, pre_mix_g, post_mix_g, pre_ffn_g, post_ffn_g, w_gate_up, w_down, loss_target, m_w_in, m_w_out, m_conv_w, m_pool_w, m_pool_scale, m_rel_bias, m_group_gain, m_pre_mix_g, m_post_mix_g, m_pre_ffn_g, m_post_ffn_g, m_w_gate_up, m_w_down, v_w_in, v_w_out, v_conv_w, v_pool_w, v_pool_scale, v_rel_bias, v_group_gain, v_pre_mix_g, v_post_mix_g, v_pre_ffn_g, v_post_ffn_g, v_w_gate_up, v_w_down):
    L = w_in.shape[0]
    T, D = x.shape[1], x.shape[2]
    DC = D // 4
    NH = rel_bias.shape[1]
    NREL = rel_bias.shape[2]
    G = pool_w.shape[1]
    cs = conv_w.shape[2]
    assert TB == LEFT_CHUNKS * CHUNK and T % TB == 0 and D % (4 * LANES) == 0 and NH * HEAD_DIM == D // 2
    xi, yi, ci = lax.axis_index("x"), lax.axis_index("y"), lax.axis_index("c")
    chip = 2 * xi + yi

    kinds = ("col", "row", "col", "row")
    big = (w_in, w_out, w_gate_up, w_down)
    rcs = [w.shape[-2:] for w in big]
    conv_gathered = _allgather_small(_pack([conv_w]))
    conv_all = conv_gathered.reshape(8, -1)[:, :L * 3 * cs].reshape(4, 2, L, 3, cs)[:, 0]
    conv_full = jnp.moveaxis(conv_all, 0, 2).reshape(L, 3, 4 * cs)
    landing = lambda which: [lax.empty(_full_shape(rcs[k], kinds[k]), BF16) for k in which]
    wi0 = _allgather_weights([w_in[0:1].astype(BF16)], kinds[:1], conv_gathered)[0][0]
    rest_plan = _gather_plan(3, rcs[1:], kinds[1:])
    rest = _split_start("gather_start_0", [w[0].astype(BF16) for w in big[1:]] + landing((1, 2, 3)), rest_plan, 3 * 7, after=wi0)
    layer_plan = _gather_plan(4, rcs, kinds)
    full = [None] * L

    h = x[0]
    saved = []
    token = rest[3]
    for l in range(L):
        g_pre, g_pm, g_pf, g_po = (a[l][None] for a in (pre_mix_g, post_mix_g, pre_ffn_g, post_ffn_g))
        if 0 < l < L - 1:
            nxt = _split_start(f"gather_start_{l + 1}", [w[l + 1].astype(BF16) for w in big] + landing(range(4)), layer_plan, 4 * 7,
                               after=full[l][0])
            token = nxt[3]
        if l < L - 1:
            g_pre = g_pre + token[0:1, 0:1]
        gg, ps = group_gain[l][None], pool_scale[l][None]
        cw = _pad_rows(conv_full[l], 8)
        wbd = _block_diag(pool_w[l]).astype(BF16)
        bias = _bias_build(jnp.pad(rel_bias[l], ((0, 0), (0, RBP - NREL)))[:, None, :])
        wi = wi0 if l == 0 else full[l][0]
        xn, pa, qkv, kv_t = _inproj_fwd(h, g_pre, wi)
        yab = _convpool_fwd(pa, cw, wbd, ps)
        yc, lse = _attn_fwd(qkv, kv_t, bias)
        if l == 0:
            full[0] = [wi0, *_split_wait("gather_wait_0", rest[0], rest[1], rest[2], rest_plan, yc)[3:6]]
            if L > 1:
                nxt = _split_start("gather_start_1", [w[1].astype(BF16) for w in big] + landing(range(4)), layer_plan, 4 * 7,
                                   after=full[0][1])
                g_pm = g_pm + nxt[3][0:1, 0:1]
        wi, wo, wgu, wdn = full[l]
        y, mix, h1, hn, gu, ff, ffo, h2 = _layer_tail_fwd(yab, yc, gg, wo, h, g_pm, g_pf, wgu, wdn, g_po)
        saved.append(dict(h=h, xn=xn, pa=pa, qkv=qkv, kv_t=kv_t, lse=lse, yab=yab, yc=yc, y=y, mix=mix, h1=h1, hn=hn, gu=gu, ff=ff, ffo=ffo,
                          cw=cw, wbd=wbd, bias=bias, ps=ps, gg=gg, g_pre=g_pre, g_pm=g_pm, g_pf=g_pf, g_po=g_po))
        h = h2
        if l + 1 < L:
            full[l + 1] = list(_split_wait(f"gather_wait_{l + 1}", nxt[0], nxt[1], nxt[2], layer_plan, h2)[4:8])

    dh, loss_tile = _loss_grad(h, loss_target[0])
    loss = lax.psum(loss_tile[0, 0], ("x", "y", "c"))

    xplan = _exchange_plan(4, kinds)
    splan = _swap_plan([3, 4, 3, 4])
    place = jnp.stack([chip, ci]).astype(jnp.int32)
    cvec = ci.reshape(1).astype(jnp.int32)
    small_grads = [None] * L
    shard_grads = [None] * L
    dbiases = [None] * L

    def start_exchange(lp, swap):
        thru = _split_wait(f"grad_swap_wait_{lp}", swap[0], swap[1], swap[2], splan, swap[4])
        chip_sums = [_add_half(v, t, cvec, "grad_add_half") for v, t in zip(thru[:4], thru[4:])]
        slots = [lax.empty((4, p.shape[0], p.shape[1] // 4) if k == "col" else p.shape, BF16) for p, k in zip(chip_sums, kinds)]
        ssem, rsem, bufs, tok = _split_start(f"grad_exchange_start_{lp}", chip_sums + slots, xplan, 4 * 3)
        return (lp, ssem, rsem, bufs), tok

    def finish_exchange(pending, after):
        lp, ssem, rsem, thru = pending
        landed = _split_wait(f"grad_exchange_wait_{lp}", ssem, rsem, thru, xplan, after)
        bufs = [_sum_chips(landed[4 + a], landed[a], kinds[a], place, "grad_sum_chips") for a in range(4)]
        shard_grads[lp] = [j.reshape(2 * j.shape[1], j.shape[2]) for j in _join_halves(bufs)]

    swap = None
    exchange = None
    for l in reversed(range(L)):
        s = saved[l]
        wi, wo, wgu, wdn = full[l]
        g_po = s["g_po"] if swap is None else s["g_po"] + swap[3][0:1, 0:1]
        dffo, dgu, dh1, dmix, dyab, dyc, dg_po, dg_pf, dg_pm, dgg = _layer_tail_bwd(
            dh, s["ffo"], g_po, s["gu"], wdn, wgu, s["h1"], s["g_pf"], s["mix"], s["g_pm"], wo, s["yab"], s["yc"], s["gg"])
        after = dyc
        if swap is not None:
            started, after = start_exchange(l + 1, (*swap[:4], dh1))
            if exchange is not None:
                finish_exchange(exchange, after)
            exchange = started
        dq, dk, dv, dbiases[l] = _attn_bwd(s["qkv"], s["kv_t"], dyc, s["yc"], s["lse"], s["bias"], after)
        dpa, dcw, dwbd, dps = _convpool_bwd(s["pa"], dyab, s["cw"], s["wbd"], s["ps"])
        dparts = [dpa, dq, dk, dv]
        dh, dg_pre = _inproj_bwd(dparts, wi, s["h"], s["g_pre"], dh1)
        F2, DFF = s["gu"].shape[1], s["ff"].shape[1]
        grads = [_wgrad_concat(s["xn"], dparts, "wgrad_in"),
                 _wgrad(s["y"], dmix, D, D, "wgrad_out"),
                 _wgrad(s["hn"], dgu, D, F2 // 4, "wgrad_gate_up"),
                 _wgrad(s["ff"], dffo, DFF // 2, D, "wgrad_down")]
        small_grads[l] = [dcw[:3], _diag_blocks(dwbd, G), dps[0], None, dgg[0], dg_pre[0], dg_pm[0], dg_pf[0], dg_po[0]]
        views = [g.reshape(2, g.shape[0] // 2, g.shape[1]) if k == "col" else g.reshape(4, 2, g.shape[0] // 8, g.shape[1])
                 for g, k in zip(grads, kinds)]
        lands = [lax.empty((1,) + v.shape[1:] if v.ndim == 3 else (v.shape[0], 1) + v.shape[2:], F32) for v in views]
        swap = _split_start(f"grad_swap_start_{l}", views + lands, splan, 4)
    last, tok = start_exchange(0, (*swap[:4], swap[3]))
    if exchange is not None:
        finish_exchange(exchange, tok)
    for l in range(L):
        small_grads[l][3] = _bias_fold(dbiases[l], tok)[:, 0, :NREL]

    names_shapes = [(L, 3, 4 * cs), pool_w.shape, pool_scale.shape, rel_bias.shape, group_gain.shape,
                    pre_mix_g.shape, post_mix_g.shape, pre_ffn_g.shape, post_ffn_g.shape]
    small_stacked = [jnp.stack([small_grads[l][k] for l in range(L)]) for k in range(len(names_shapes))]
    packed = _pack(small_stacked)
    M = packed.shape[0]
    total = _sum_slots(_allgather_small(packed).reshape(8, M, LANES), M, "small_sum_devices")
    g_small = _unpack(total, names_shapes)
    g_small[0] = lax.dynamic_slice_in_dim(g_small[0], chip * cs, cs, axis=2)
    finish_exchange(last, total)
    g_big = [jnp.stack([shard_grads[l][k] for l in range(L)]) for k in range(4)]

    def adam_big(w, g, m, v, name):
        shp = w.shape
        two = lambda a: a.reshape(shp[0] * shp[1], shp[2])
        return [o.reshape(shp) for o in _adamw(two(w), two(g), two(m), two(v), 256, name)]

    upd_in = adam_big(w_in, g_big[0], m_w_in, v_w_in, "adamw_in")
    upd_out = adam_big(w_out, g_big[1], m_w_out, v_w_out, "adamw_out")
    upd_gu = adam_big(w_gate_up, g_big[2], m_w_gate_up, v_w_gate_up, "adamw_gate_up")
    upd_dn = adam_big(w_down, g_big[3], m_w_down, v_w_down, "adamw_down")

    small_w = [conv_w, pool_w, pool_scale, rel_bias, group_gain, pre_mix_g, post_mix_g, pre_ffn_g, post_ffn_g]
    small_m = [m_conv_w, m_pool_w, m_pool_scale, m_rel_bias, m_group_gain, m_pre_mix_g, m_post_mix_g, m_pre_ffn_g, m_post_ffn_g]
    small_v = [v_conv_w, v_pool_w, v_pool_scale, v_rel_bias, v_group_gain, v_pre_mix_g, v_post_mix_g, v_pre_ffn_g, v_post_ffn_g]
    pw_, pg_, pm_, pv_ = _pack(small_w), _pack(g_small), _pack(small_m), _pack(small_v)
    shapes = [w.shape for w in small_w]
    upd_small = [_unpack(o, shapes) for o in _adamw(pw_, pg_, pm_, pv_, pw_.shape[0], "adamw_small")]

    def ordered(big4, small9):
        return [big4[0], big4[1], *small9, big4[2], big4[3]]

    grads = ordered(g_big, g_small)
    outs = [ordered([upd_in[k], upd_out[k], upd_gu[k], upd_dn[k]], upd_small[k]) for k in range(3)]
    return (loss, dh[None], *grads, *outs[0], *outs[1], *outs[2])
```

```python
import functools

import jax
import jax.numpy as jnp
from jax import lax
from jax.experimental import pallas as pl
from jax.experimental.pallas import tpu as pltpu

F32, BF16 = jnp.float32, jnp.bfloat16
EPS = 1e-6
CHUNK = 64
LEFT_CHUNKS = 8
REL_CLIP = 128
HEAD_DIM = 64
LANES = 128
POOL_WINDOWS = (2, 4, 8, 16)
HALO = 16
TB = LEFT_CHUNKS * CHUNK
TBF = 256
TBW = 1024
BAND = (LEFT_CHUNKS + 1) * CHUNK
SUB = 2 * CHUNK
BANDW = SUB + LEFT_CHUNKS * CHUNK
SKEW = 768
NEG = -1e30
RBP = 384
VMEM_LIMIT = 56 * 1024 * 1024
ADAM_LR, ADAM_B1, ADAM_B2, ADAM_EPS, ADAM_WD, ADAM_STEP = 0.001, 0.9, 0.999, 1e-08, 0.01, 10
MESH = pl.DeviceIdType.MESH
ANY = pl.BlockSpec(memory_space=pl.ANY)


def _params(*sem):
    kw = dict(vmem_limit_bytes=VMEM_LIMIT)
    if sem:
        kw["dimension_semantics"] = sem
    return pltpu.CompilerParams(**kw)


def _dot(a, b):
    return jnp.dot(a, b, preferred_element_type=F32)


def _dot_nt(a, b):
    return lax.dot_general(a, b, (((1,), (1,)), ((), ())), preferred_element_type=F32)


def _dot_tn(a, b):
    return lax.dot_general(a, b, (((0,), (0,)), ((), ())), preferred_element_type=F32)


def _rms(x, g):
    r = lax.rsqrt(jnp.mean(x * x, axis=-1, keepdims=True) + EPS)
    return x * r * g


def _rms_bwd(dy, x, g):
    r = lax.rsqrt(jnp.mean(x * x, axis=-1, keepdims=True) + EPS)
    xh = x * r
    dxh = dy * g
    dx = r * (dxh - xh * jnp.mean(dxh * xh, axis=-1, keepdims=True))
    return dx, jnp.sum(dy * xh, axis=0, keepdims=True)


def _full(shape):
    return pl.BlockSpec(shape, lambda *_: (0,) * len(shape))


def _acc_init(step, *refs):
    @pl.when(step == 0)
    def _():
        for r in refs:
            r[...] = jnp.zeros_like(r)


def _inproj_fwd(h, g, w):
    T, D = h.shape
    NQ = w.shape[1] - D
    NKV = 2 * NQ // 3

    def body(h_ref, g_ref, w_ref, xn_ref, pa_ref, qkv_ref, kvt_ref):
        xn = _rms(h_ref[...], g_ref[...]).astype(BF16)
        xn_ref[...] = xn
        pa_ref[...] = _dot(xn, w_ref[:, :D])
        qkv = _dot(xn, w_ref[:, D:])
        qkv_ref[...] = qkv.astype(BF16)
        kvt_ref[...] = qkv[:, NQ - NKV:].T.astype(BF16)

    row = lambda n: pl.BlockSpec((TB, n), lambda i: (i, 0))
    return pl.pallas_call(
        body, name="inproj_fwd", grid=(T // TB,),
        in_specs=[row(D), _full((1, D)), _full(w.shape)],
        out_specs=[row(D), row(D), row(NQ), pl.BlockSpec((NKV, TB), lambda i: (0, i))],
        out_shape=[jax.ShapeDtypeStruct((T, D), BF16), jax.ShapeDtypeStruct((T, D), F32),
                   jax.ShapeDtypeStruct((T, NQ), BF16), jax.ShapeDtypeStruct((NKV, T), BF16)],
        compiler_params=_params("parallel"),
    )(h, g, w)


def _lane_groups(n, vals):
    lane = lax.broadcasted_iota(jnp.int32, (1, n), 1)
    q = n // 4
    return jnp.where(lane < q, vals[0], jnp.where(lane < 2 * q, vals[1], jnp.where(lane < 3 * q, vals[2], vals[3]))).astype(F32)


def _pick_group(levels, n):
    lane = lax.broadcasted_iota(jnp.int32, (1, n), 1)
    q = n // 4
    return jnp.where(lane < q, levels[0], jnp.where(lane < 2 * q, levels[1], jnp.where(lane < 3 * q, levels[2], levels[3])))


def _pool_count(blk, n):
    t1 = (blk * TB + 1 + lax.broadcasted_iota(jnp.int32, (TB, 1), 0)).astype(F32)
    return jnp.minimum(t1, _lane_groups(n, POOL_WINDOWS))


def _pool_d(pu, pu_halo, cnt):
    e = jnp.concatenate([pu_halo, pu], axis=0)
    s2 = e + pltpu.roll(e, 1, 0)
    s4 = s2 + pltpu.roll(s2, 2, 0)
    s8 = s4 + pltpu.roll(s4, 4, 0)
    s16 = s8 + pltpu.roll(s8, 8, 0)
    num = _pick_group([s2, s4, s8, s16], pu.shape[1])[HALO:]
    return num / cnt - pu


def _conv_taps(z, z_halo):
    e = jnp.concatenate([z_halo, z], axis=0)
    return pltpu.roll(e, 1, 0)[HALO:], pltpu.roll(e, 2, 0)[HALO:]


def _convpool_fwd(pa, cw, wbd, ps):
    T, D = pa.shape
    DC = D // 4

    def body(pa_ref, halo_ref, cw_ref, wbd_ref, ps_ref, yab_ref):
        i = pl.program_id(0)
        x = pa_ref[...]
        hl = jnp.where(i > 0, halo_ref[...], 0.0)
        gb, gc, u, pu = (x[:, k * DC:(k + 1) * DC] for k in range(4))
        z = gc * u
        z1, z2 = _conv_taps(z, hl[:, DC:2 * DC] * hl[:, 2 * DC:3 * DC])
        cwv = cw_ref[...]
        ya = gb * (cwv[2:3] * z + cwv[1:2] * z1 + cwv[0:1] * z2)
        d = _pool_d(pu, hl[:, 3 * DC:], _pool_count(i, DC))
        yb = _dot(d.astype(BF16), wbd_ref[...]) * ps_ref[...]
        yab_ref[...] = jnp.concatenate([ya, yb], axis=1)

    return pl.pallas_call(
        body, name="convpool_fwd", grid=(T // TB,),
        in_specs=[pl.BlockSpec((TB, D), lambda i: (i, 0)),
                  pl.BlockSpec((HALO, D), lambda i: (jnp.maximum(i * (TB // HALO) - 1, 0), 0)),
                  _full((8, DC)), _full((DC, DC)), _full((1, DC))],
        out_specs=pl.BlockSpec((TB, 2 * DC), lambda i: (i, 0)),
        out_shape=jax.ShapeDtypeStruct((T, 2 * DC), F32),
        compiler_params=_params("parallel"),
    )(pa, pa, cw, wbd, ps)


def _bias_bins(shape, col_dim):
    j = lax.broadcasted_iota(jnp.int32, shape, col_dim)
    b = lax.broadcasted_iota(jnp.int32, shape, 1 - col_dim)
    d = jnp.where(j < BAND, j, j - SKEW)
    live = jnp.logical_or(j < BAND, j > SKEW - CHUNK)
    bins = jnp.minimum(TB - d, REL_CLIP) + REL_CLIP
    return jnp.where(jnp.logical_and(live, bins == b), 1.0, 0.0).astype(F32)


def _bias_build(rb):
    H = rb.shape[0]

    def body(rb_ref, o_ref):
        v = jnp.dot(jnp.broadcast_to(rb_ref[0], (8, RBP)), _bias_bins((RBP, SKEW), 1),
                    precision=lax.Precision.HIGHEST, preferred_element_type=F32)
        x = jnp.broadcast_to(v[0:1], (SUB, SKEW))
        row = lax.broadcasted_iota(jnp.int32, (SUB, SKEW), 0)
        for b in range(SUB.bit_length() - 1):
            x = jnp.where(((row >> b) & 1) == 1, pltpu.roll(x, 1 << b, 1), x)
        qc = lax.broadcasted_iota(jnp.int32, (SUB, BANDW), 0) >> (CHUNK.bit_length() - 1)
        kc = lax.broadcasted_iota(jnp.int32, (SUB, BANDW), 1) >> (CHUNK.bit_length() - 1)
        o_ref[0] = jnp.where(jnp.logical_and(kc >= qc, kc <= qc + LEFT_CHUNKS), x[:, :BANDW], NEG).T

    return pl.pallas_call(
        body, name="bias_build", grid=(H,),
        in_specs=[pl.BlockSpec((1, 1, RBP), lambda h: (h, 0, 0))],
        out_specs=pl.BlockSpec((1, BANDW, SUB), lambda h: (h // 2, 0, h % 2)),
        out_shape=jax.ShapeDtypeStruct((H // 2, BANDW, 2 * SUB), F32),
        compiler_params=_params("parallel"),
    )(rb)


def _bias_fold(db, after):
    H = 2 * db.shape[0]

    def body(db_ref, after_ref, o_ref):
        x = jnp.concatenate([db_ref[0].T, jnp.zeros((SUB, SKEW - BANDW), F32)], axis=1)
        row = lax.broadcasted_iota(jnp.int32, (SUB, SKEW), 0)
        for b in range(SUB.bit_length() - 1):
            x = jnp.where(((row >> b) & 1) == 1, pltpu.roll(x, SKEW - (1 << b), 1), x)
        dsum = jnp.sum(x, axis=0, keepdims=True)
        o_ref[0] = jnp.dot(jnp.broadcast_to(dsum, (8, SKEW)), _bias_bins((SKEW, RBP), 0),
                           precision=lax.Precision.HIGHEST, preferred_element_type=F32)

    return pl.pallas_call(
        body, name="bias_fold", grid=(H,),
        in_specs=[pl.BlockSpec((1, BANDW, SUB), lambda h: (h // 2, 0, h % 2)), ANY],
        out_specs=pl.BlockSpec((1, 8, RBP), lambda h: (h, 0, 0)),
        out_shape=jax.ShapeDtypeStruct((H, 8, RBP), F32),
        compiler_params=_params("parallel"),
    )(db, after)


def _both_heads(x):
    first = lax.broadcasted_iota(jnp.int32, (1, LANES), 1) < HEAD_DIM
    return jnp.concatenate([jnp.where(first, x, 0), jnp.where(first, 0, x)], axis=0)


def _own_head_rows(x2):
    n = x2.shape[1] // 2
    first = lax.broadcasted_iota(jnp.int32, (LANES, 1), 0) < HEAD_DIM
    return jnp.where(first, x2[:, :n], x2[:, n:])


def _key_tiles(s, first_block):
    return [t for t in range(s, s + BANDW // SUB) if not (first_block and t < TB // SUB)]


def _attn_fwd(qkv, kv_t, bias_t):
    T = qkv.shape[0]
    NP = qkv.shape[1] // (3 * LANES)
    NB = T // TB
    NS = TB // SUB
    scale = HEAD_DIM ** -0.5

    def body(q_ref, kc_ref, kp_ref, vtc_ref, vtp_ref, b_ref, o_ref, lse_ref):
        i = pl.program_id(1)

        def compute(first_block):
            q = q_ref[...] * scale
            kwin = jnp.concatenate([kp_ref[...], kc_ref[...]], axis=0)
            vt = jnp.concatenate([vtp_ref[...], vtc_ref[...]], axis=1)
            for s in range(NS):
                rows = slice(s * SUB, (s + 1) * SUB)
                tiles = _key_tiles(s, first_block)
                keys = slice(tiles[0] * SUB, (tiles[-1] + 1) * SUB)
                brows = slice((tiles[0] - s) * SUB, (tiles[-1] - s + 1) * SUB)
                q2 = _both_heads(q[rows])
                halves = []
                for a in range(2):
                    st = _dot_nt(kwin[keys], q2[a * SUB:(a + 1) * SUB]) + b_ref[0, brows, a * SUB:(a + 1) * SUB]
                    m = jnp.max(st, axis=0, keepdims=True)
                    p = jnp.exp(st - m)
                    l = jnp.sum(p, axis=0, keepdims=True)
                    halves.append(_dot(vt[:, keys], p.astype(BF16)) * (1.0 / l))
                    lse_ref[0, 0, a * NS + s:a * NS + s + 1, :] = m + jnp.log(l)
                o_ref[rows, :] = _own_head_rows(jnp.concatenate(halves, axis=1)).T

        pl.when(i == 0)(functools.partial(compute, True))
        pl.when(i > 0)(functools.partial(compute, False))

    prev = lambda i: jnp.maximum(i - 1, 0)
    return pl.pallas_call(
        body, name="attn_fwd", grid=(NP, NB),
        in_specs=[pl.BlockSpec((TB, LANES), lambda p, i: (i, p)),
                  pl.BlockSpec((TB, LANES), lambda p, i: (i, NP + p)),
                  pl.BlockSpec((TB, LANES), lambda p, i: (prev(i), NP + p)),
                  pl.BlockSpec((LANES, TB), lambda p, i: (NP + p, i)),
                  pl.BlockSpec((LANES, TB), lambda p, i: (NP + p, prev(i))),
                  pl.BlockSpec((1, BANDW, 2 * SUB), lambda p, i: (p, 0, 0))],
        out_specs=[pl.BlockSpec((TB, LANES), lambda p, i: (i, p)),
                   pl.BlockSpec((1, 1, 8, LANES), lambda p, i: (p, i, 0, 0))],
        out_shape=[jax.ShapeDtypeStruct((T, NP * LANES), F32), jax.ShapeDtypeStruct((NP, NB, 8, LANES), F32)],
        compiler_params=_params("parallel", "parallel"),
    )(qkv, qkv, qkv, kv_t, kv_t, bias_t)


def _group_bounds(D):
    return ((0, D // 4), (D // 4, D // 2), (D // 2, D))


def _load_resident(step, *pairs_and_sems):
    @pl.when(step == 0)
    def _():
        cps = [pltpu.make_async_copy(src, dst, sem) for src, dst, sem in pairs_and_sems]
        for cp in cps:
            cp.start()
        for cp in cps:
            cp.wait()


def _layer_tail_fwd(yab, yc, gg, w_out, h, g_pm, g_pf, w_gu, w_dn, g_po):
    T, D = h.shape
    F2 = w_gu.shape[1]
    DFF = F2 // 2
    TF = DFF // 2

    def body(yab_ref, yc_ref, gg_ref, wo_hbm, h_ref, gpm_ref, gpf_ref, wgu_hbm, wdn_hbm, gpo_ref,
             y_ref, mix_ref, h1_ref, hn_ref, gu_ref, ff_ref, ffo_ref, h2_ref, wo_v, wgu_v, wdn_v, sems):
        _load_resident(pl.program_id(0), (wo_hbm, wo_v, sems.at[0]), (wgu_hbm, wgu_v, sems.at[1]), (wdn_hbm, wdn_v, sems.at[2]))
        yraw = jnp.concatenate([yab_ref[...], yc_ref[...]], axis=1)
        ggv = gg_ref[...]
        y = jnp.concatenate([_rms(yraw[:, a:b], ggv[:, a:b]) for a, b in _group_bounds(D)], axis=1).astype(BF16)
        y_ref[...] = y
        mix = _dot(y, wo_v[...])
        mix_ref[...] = mix.astype(BF16)
        h1 = h_ref[...] + _rms(mix, gpm_ref[...])
        h1_ref[...] = h1
        hn = _rms(h1, gpf_ref[...]).astype(BF16)
        hn_ref[...] = hn
        ffo = jnp.zeros((TBF, D), F32)
        for j in range(2):
            sg, su = slice(j * TF, (j + 1) * TF), slice(DFF + j * TF, DFF + (j + 1) * TF)
            gate = _dot(hn, wgu_v[:, sg])
            up = _dot(hn, wgu_v[:, su])
            gu_ref[:, sg] = gate.astype(BF16)
            gu_ref[:, su] = up.astype(BF16)
            ff = (gate * jax.nn.sigmoid(gate) * up).astype(BF16)
            ff_ref[:, sg] = ff
            ffo = ffo + _dot(ff, wdn_v[sg, :])
        ffo_ref[...] = ffo.astype(BF16)
        h2_ref[...] = h1 + _rms(ffo, gpo_ref[...])

    row = lambda n: pl.BlockSpec((TBF, n), lambda i: (i, 0))
    gain = _full((1, D))
    f32, bf16 = (lambda n: jax.ShapeDtypeStruct((T, n), F32)), (lambda n: jax.ShapeDtypeStruct((T, n), BF16))
    return pl.pallas_call(
        body, name="layer_tail_fwd", grid=(T // TBF,),
        in_specs=[row(D // 2), row(D // 2), gain, ANY, row(D), gain, gain, ANY, ANY, gain],
        out_specs=[row(D), row(D), row(D), row(D), row(F2), row(DFF), row(D), row(D)],
        out_shape=[bf16(D), bf16(D), f32(D), bf16(D), bf16(F2), bf16(DFF), bf16(D), f32(D)],
        scratch_shapes=[pltpu.VMEM(w_out.shape, BF16), pltpu.VMEM(w_gu.shape, BF16), pltpu.VMEM(w_dn.shape, BF16),
                        pltpu.SemaphoreType.DMA((3,))],
        compiler_params=_params("arbitrary"),
    )(yab, yc, gg, w_out, h, g_pm, g_pf, w_gu, w_dn, g_po)


def _loss_grad(h, tgt):
    T, D = h.shape

    def body(h_ref, t_ref, dh_ref, loss_ref):
        _acc_init(pl.program_id(0), loss_ref)
        diff = h_ref[...] - t_ref[...]
        dh_ref[...] = diff * (1.0 / D)
        loss_ref[...] += 0.5 * jnp.sum(jnp.mean(diff * diff, axis=-1, keepdims=True))

    row = pl.BlockSpec((TB, D), lambda i: (i, 0))
    return pl.pallas_call(
        body, name="loss_grad", grid=(T // TB,),
        in_specs=[row, row], out_specs=[row, _full((8, LANES))],
        out_shape=[jax.ShapeDtypeStruct((T, D), F32), jax.ShapeDtypeStruct((8, LANES), F32)],
        compiler_params=_params("arbitrary"),
    )(h, tgt)


def _layer_tail_bwd(dh2, ffo, g_po, gu, w_dn, w_gu, h1, g_pf, mix, g_pm, w_out, yab, yc, gg):
    T, D = dh2.shape
    F2 = gu.shape[1]
    DFF = F2 // 2
    TF = DFF // 2

    def body(dh_ref, ffo_ref, gpo_ref, gu_ref, wdn_hbm, wgu_hbm, h_ref, gpf_ref, mix_ref, gpm_ref, wo_hbm, yab_ref, yc_ref, gg_ref,
             dffo_ref, dgu_ref, dh1_ref, dmix_ref, dyab_ref, dyc_ref, dgpo_ref, dgpf_ref, dgpm_ref, dgg_ref,
             wdn_v, wgu_v, wo_v, sems):
        step = pl.program_id(0)
        _load_resident(step, (wdn_hbm, wdn_v, sems.at[0]), (wgu_hbm, wgu_v, sems.at[1]), (wo_hbm, wo_v, sems.at[2]))
        _acc_init(step, dgpo_ref, dgpf_ref, dgpm_ref, dgg_ref)
        dh = dh_ref[...]
        dffo, dg = _rms_bwd(dh, ffo_ref[...].astype(F32), gpo_ref[...])
        dgpo_ref[0:1, :] += dg
        dffo = dffo.astype(BF16)
        dffo_ref[...] = dffo
        dhn = jnp.zeros((TBF, D), F32)
        for j in range(2):
            sg, su = slice(j * TF, (j + 1) * TF), slice(DFF + j * TF, DFF + (j + 1) * TF)
            dff = _dot_nt(dffo, wdn_v[sg, :])
            gate, up = gu_ref[:, sg].astype(F32), gu_ref[:, su].astype(F32)
            sig = jax.nn.sigmoid(gate)
            dgate = (dff * up * (sig * (1.0 + gate * (1.0 - sig)))).astype(BF16)
            dup = (dff * (gate * sig)).astype(BF16)
            dgu_ref[:, sg] = dgate
            dgu_ref[:, su] = dup
            dhn = dhn + _dot_nt(dgate, wgu_v[:, sg]) + _dot_nt(dup, wgu_v[:, su])
        dx, dg = _rms_bwd(dhn, h_ref[...], gpf_ref[...])
        dgpf_ref[0:1, :] += dg
        dh1 = dh + dx
        dh1_ref[...] = dh1
        dmix, dg = _rms_bwd(dh1, mix_ref[...].astype(F32), gpm_ref[...])
        dgpm_ref[0:1, :] += dg
        dmix = dmix.astype(BF16)
        dmix_ref[...] = dmix
        dy = _dot_nt(dmix, wo_v[...])
        yraw = jnp.concatenate([yab_ref[...], yc_ref[...]], axis=1)
        ggv = gg_ref[...]
        parts = [_rms_bwd(dy[:, a:b], yraw[:, a:b], ggv[:, a:b]) for a, b in _group_bounds(D)]
        dgg_ref[0:1, :] += jnp.concatenate([p[1] for p in parts], axis=1)
        dyab_ref[...] = jnp.concatenate([parts[0][0], parts[1][0]], axis=1)
        dyc_ref[...] = parts[2][0]

    row = lambda n: pl.BlockSpec((TBF, n), lambda i: (i, 0))
    gain, acc = _full((1, D)), _full((8, D))
    f32, bf16 = (lambda n: jax.ShapeDtypeStruct((T, n), F32)), (lambda n: jax.ShapeDtypeStruct((T, n), BF16))
    acc_shape = jax.ShapeDtypeStruct((8, D), F32)
    return pl.pallas_call(
        body, name="layer_tail_bwd", grid=(T // TBF,),
        in_specs=[row(D), row(D), gain, row(F2), ANY, ANY, row(D), gain, row(D), gain, ANY, row(D // 2), row(D // 2), gain],
        out_specs=[row(D), row(F2), row(D), row(D), row(D // 2), row(D // 2), acc, acc, acc, acc],
        out_shape=[bf16(D), bf16(F2), f32(D), bf16(D), f32(D // 2), f32(D // 2), acc_shape, acc_shape, acc_shape, acc_shape],
        scratch_shapes=[pltpu.VMEM(w_dn.shape, BF16), pltpu.VMEM(w_gu.shape, BF16), pltpu.VMEM(w_out.shape, BF16),
                        pltpu.SemaphoreType.DMA((3,))],
        compiler_params=_params("arbitrary"),
    )(dh2, ffo, g_po, gu, w_dn, w_gu, h1, g_pf, mix, g_pm, w_out, yab, yc, gg)


def _attn_bwd(qkv, kv_t, dyc, yc, lse, bias_t, after):
    T = qkv.shape[0]
    NP = qkv.shape[1] // (3 * LANES)
    NB = T // TB
    NS = TB // SUB
    scale = HEAD_DIM ** -0.5

    def body(q_ref, kc_ref, kp_ref, vc_ref, vp_ref, ktc_ref, ktp_ref, do_ref, o_ref, lse_ref, b_ref, after_ref,
             dq_ref, dk_ref, dv_ref, db_ref, dk_carry, dv_carry, dkw, dvw):
        step = pl.program_id(1)
        i = NB - 1 - step
        _acc_init(step, dk_carry, dv_carry, db_ref)

        def compute(first_block):
            q = q_ref[...] * scale
            kwin = jnp.concatenate([kp_ref[...], kc_ref[...]], axis=0)
            vwin = jnp.concatenate([vp_ref[...], vc_ref[...]], axis=0)
            kt = jnp.concatenate([ktp_ref[...], ktc_ref[...]], axis=1)
            do = do_ref[...]
            dob = do.astype(BF16)
            prod = do * o_ref[...]
            first = lax.broadcasted_iota(jnp.int32, (1, LANES), 1) < HEAD_DIM
            ones = jnp.ones((8, LANES), F32)
            row_sums = lambda x: lax.dot_general(ones, x, (((1,), (1,)), ((), ())), precision=lax.Precision.HIGHEST,
                                                 preferred_element_type=F32)
            deltas = (row_sums(jnp.where(first, prod, 0.0)), row_sums(jnp.where(first, 0.0, prod)))
            written = set()
            for s in range(NS):
                rows = slice(s * SUB, (s + 1) * SUB)
                tiles = _key_tiles(s, first_block)
                keys = slice(tiles[0] * SUB, (tiles[-1] + 1) * SUB)
                brows = slice((tiles[0] - s) * SUB, (tiles[-1] - s + 1) * SUB)
                q2, do2 = _both_heads(q[rows]), _both_heads(dob[rows])
                lse = jnp.concatenate([lse_ref[0, 0, s:s + 1, :], lse_ref[0, 0, NS + s:NS + s + 1, :]], axis=1)
                delta = jnp.concatenate([deltas[0][0:1, rows], deltas[1][0:1, rows]], axis=1)
                p = jnp.exp(_dot_nt(kwin[keys], q2) + b_ref[0, brows, :] - lse)
                ds = p * (_dot_nt(vwin[keys], do2) - delta)
                db_ref[0, brows, :] += ds
                dsb = ds.astype(BF16)
                dk_c = _dot(dsb, q2)
                dv_c = _dot(p.astype(BF16), do2)
                dq_ref[rows, :] = (_own_head_rows(_dot(kt[:, keys], dsb)).T * scale).astype(BF16)
                for n, t in enumerate(tiles):
                    win, loc = slice(t * SUB, (t + 1) * SUB), slice(n * SUB, (n + 1) * SUB)
                    if t in written:
                        dkw[win, :] += dk_c[loc]
                        dvw[win, :] += dv_c[loc]
                    else:
                        dkw[win, :] = dk_c[loc]
                        dvw[win, :] = dv_c[loc]
                        written.add(t)
            dk_ref[...] = (dkw[TB:, :] + dk_carry[...]).astype(BF16)
            dv_ref[...] = (dvw[TB:, :] + dv_carry[...]).astype(BF16)
            if not first_block:
                dk_carry[...] = dkw[:TB, :]
                dv_carry[...] = dvw[:TB, :]

        pl.when(i == 0)(functools.partial(compute, True))
        pl.when(i > 0)(functools.partial(compute, False))

    blk = lambda s: NB - 1 - s
    prev = lambda s: jnp.maximum(NB - 2 - s, 0)
    rows = lambda which, off: pl.BlockSpec((TB, LANES), lambda p, s: (which(s), off + p))
    out = jax.ShapeDtypeStruct((T, NP * LANES), BF16)
    strip = pl.BlockSpec((1, BANDW, 2 * SUB), lambda p, s: (p, 0, 0))
    return pl.pallas_call(
        body, name="attn_bwd", grid=(NP, NB),
        in_specs=[rows(blk, 0), rows(blk, NP), rows(prev, NP), rows(blk, 2 * NP), rows(prev, 2 * NP),
                  pl.BlockSpec((LANES, TB), lambda p, s: (p, blk(s))), pl.BlockSpec((LANES, TB), lambda p, s: (p, prev(s))),
                  rows(blk, 0), rows(blk, 0), pl.BlockSpec((1, 1, 8, LANES), lambda p, s: (p, blk(s), 0, 0)), strip, ANY],
        out_specs=[rows(blk, 0), rows(blk, 0), rows(blk, 0), strip],
        out_shape=[out, out, out, jax.ShapeDtypeStruct((NP, BANDW, 2 * SUB), F32)],
        scratch_shapes=[pltpu.VMEM((TB, LANES), F32), pltpu.VMEM((TB, LANES), F32),
                        pltpu.VMEM((2 * TB, LANES), F32), pltpu.VMEM((2 * TB, LANES), F32)],
        compiler_params=_params("arbitrary", "arbitrary"),
    )(qkv, qkv, qkv, qkv, qkv, kv_t, kv_t, dyc, yc, lse, bias_t, after)


def _convpool_bwd(pa, dyab, cw, wbd, ps):
    T, D = pa.shape
    DC = D // 4
    NB = T // TB
    N = TB + HALO

    def body(pa_ref, halo_ref, dy_ref, cw_ref, wbd_ref, ps_ref, dpa_ref, dcw_ref, dwbd_ref, dps_ref, dc_carry, e_carry):
        step = pl.program_id(0)
        i = NB - 1 - step
        _acc_init(step, dcw_ref, dwbd_ref, dps_ref, dc_carry, e_carry)
        x = pa_ref[...]
        hl = jnp.where(i > 0, halo_ref[...], 0.0)
        gb, gc, u, pu = (x[:, k * DC:(k + 1) * DC] for k in range(4))
        dy = dy_ref[...]
        dya, dyb = dy[:, :DC], dy[:, DC:]
        cwv = cw_ref[...]
        z = gc * u
        z1, z2 = _conv_taps(z, hl[:, DC:2 * DC] * hl[:, 2 * DC:3 * DC])
        dgb = dya * (cwv[2:3] * z + cwv[1:2] * z1 + cwv[0:1] * z2)
        dconv = dya * gb
        dcw_ref[0:1, :] += jnp.sum(dconv * z2, axis=0, keepdims=True)
        dcw_ref[1:2, :] += jnp.sum(dconv * z1, axis=0, keepdims=True)
        dcw_ref[2:3, :] += jnp.sum(dconv * z, axis=0, keepdims=True)
        ext = jnp.concatenate([dconv, dc_carry[...]], axis=0)
        dz = cwv[2:3] * dconv + cwv[1:2] * pltpu.roll(ext, N - 1, 0)[:TB] + cwv[0:1] * pltpu.roll(ext, N - 2, 0)[:TB]
        dc_carry[...] = dconv[:HALO]
        cnt = _pool_count(i, DC)
        d = _pool_d(pu, hl[:, 3 * DC:], cnt).astype(BF16)
        psv = ps_ref[...]
        w = wbd_ref[...]
        dps_ref[0:1, :] += jnp.sum(dyb * _dot(d, w), axis=0, keepdims=True)
        dys = (dyb * psv).astype(BF16)
        dwbd_ref[...] += _dot_tn(d, dys)
        dd = _dot_nt(dys, w)
        e = dd / cnt
        ext = jnp.concatenate([e, e_carry[...]], axis=0)
        a2 = ext + pltpu.roll(ext, N - 1, 0)
        a4 = a2 + pltpu.roll(a2, N - 2, 0)
        a8 = a4 + pltpu.roll(a4, N - 4, 0)
        a16 = a8 + pltpu.roll(a8, N - 8, 0)
        dpu = _pick_group([a2, a4, a8, a16], DC)[:TB] - dd
        e_carry[...] = e[:HALO]
        dpa_ref[...] = jnp.concatenate([dgb, dz * u, dz * gc, dpu], axis=1).astype(BF16)

    blk = lambda n: pl.BlockSpec((TB, n), lambda s: (NB - 1 - s, 0))
    return pl.pallas_call(
        body, name="convpool_bwd", grid=(NB,),
        in_specs=[blk(D), pl.BlockSpec((HALO, D), lambda s: (jnp.maximum((NB - 1 - s) * (TB // HALO) - 1, 0), 0)),
                  blk(2 * DC), _full((8, DC)), _full((DC, DC)), _full((1, DC))],
        out_specs=[blk(D), _full((8, DC)), _full((DC, DC)), _full((8, DC))],
        out_shape=[jax.ShapeDtypeStruct((T, D), BF16), jax.ShapeDtypeStruct((8, DC), F32),
                   jax.ShapeDtypeStruct((DC, DC), F32), jax.ShapeDtypeStruct((8, DC), F32)],
        scratch_shapes=[pltpu.VMEM((HALO, DC), F32), pltpu.VMEM((HALO, DC), F32)],
        compiler_params=_params("arbitrary"),
    )(pa, pa, dyab, cw, wbd, ps)


def _inproj_bwd(dparts, w, h, g, dh1):
    T, D = h.shape
    widths = [p.shape[1] for p in dparts]
    n = len(dparts)

    def body(*refs):
        parts, (w_ref, h_ref, g_ref, dh1_ref, dh_ref, dg_ref) = refs[:n], refs[n:]
        _acc_init(pl.program_id(0), dg_ref)
        dxn, off = jnp.zeros((TB, D), F32), 0
        for p_ref, wd in zip(parts, widths):
            dxn = dxn + _dot_nt(p_ref[...], w_ref[:, off:off + wd])
            off += wd
        dx, dg = _rms_bwd(dxn, h_ref[...], g_ref[...])
        dg_ref[0:1, :] += dg
        dh_ref[...] = dh1_ref[...] + dx

    row = lambda m: pl.BlockSpec((TB, m), lambda i: (i, 0))
    return pl.pallas_call(
        body, name="inproj_bwd", grid=(T // TB,),
        in_specs=[row(wd) for wd in widths] + [_full(w.shape), row(D), _full((1, D)), row(D)],
        out_specs=[row(D), _full((8, D))],
        out_shape=[jax.ShapeDtypeStruct((T, D), F32), jax.ShapeDtypeStruct((8, D), F32)],
        compiler_params=_params("arbitrary"),
    )(*dparts, w, h, g, dh1)


def _wgrad(a, b, tk, tn, name):
    T, K = a.shape
    N = b.shape[1]

    def body(a_ref, b_ref, o_ref):
        _acc_init(pl.program_id(2), o_ref)
        o_ref[...] += _dot_tn(a_ref[...], b_ref[...])

    return pl.pallas_call(
        body, name=name, grid=(K // tk, N // tn, T // TBW),
        in_specs=[pl.BlockSpec((TBW, tk), lambda k, n, t: (t, k)), pl.BlockSpec((TBW, tn), lambda k, n, t: (t, n))],
        out_specs=pl.BlockSpec((tk, tn), lambda k, n, t: (k, n)),
        out_shape=jax.ShapeDtypeStruct((K, N), F32),
        compiler_params=_params("parallel", "parallel", "arbitrary"),
    )(a, b)


def _wgrad_concat(a, bs, name):
    T, K = a.shape
    widths = [b.shape[1] for b in bs]
    n = len(bs)

    def body(*refs):
        a_ref, b_refs, o_ref = refs[0], refs[1:1 + n], refs[1 + n]
        _acc_init(pl.program_id(0), o_ref)
        av, off = a_ref[...], 0
        for b_ref, wd in zip(b_refs, widths):
            o_ref[:, off:off + wd] += _dot_tn(av, b_ref[...])
            off += wd

    row = lambda m: pl.BlockSpec((TB, m), lambda t: (t, 0))
    return pl.pallas_call(
        body, name=name, grid=(T // TB,),
        in_specs=[row(K)] + [row(wd) for wd in widths],
        out_specs=_full((K, sum(widths))),
        out_shape=jax.ShapeDtypeStruct((K, sum(widths)), F32),
        compiler_params=_params("arbitrary"),
    )(a, *bs)


def _adamw(w, g, m, v, tr, name):
    R, C = w.shape

    def body(w_ref, g_ref, m_ref, v_ref, d_ref, nm_ref, nv_ref):
        gv = g_ref[...]
        nm = ADAM_B1 * m_ref[...] + (1.0 - ADAM_B1) * gv
        nv = ADAM_B2 * v_ref[...] + (1.0 - ADAM_B2) * (gv * gv)
        m_hat = nm / (1.0 - ADAM_B1 ** ADAM_STEP)
        v_hat = nv / (1.0 - ADAM_B2 ** ADAM_STEP)
        d_ref[...] = -ADAM_LR * (m_hat / (jnp.sqrt(v_hat) + ADAM_EPS) + ADAM_WD * w_ref[...])
        nm_ref[...] = nm
        nv_ref[...] = nv

    blk = pl.BlockSpec((tr, C), lambda i: (i, 0))
    out = jax.ShapeDtypeStruct((R, C), F32)
    return pl.pallas_call(
        body, name=name, grid=(R // tr,), in_specs=[blk] * 4, out_specs=[blk] * 3, out_shape=[out] * 3,
        compiler_params=_params("parallel"),
    )(w, g, m, v)


def _sum_slots(x, tr, name):
    n, R, C = x.shape

    def body(x_ref, o_ref):
        acc = x_ref[0]
        for k in range(1, n):
            acc = acc + x_ref[k]
        o_ref[...] = acc

    return pl.pallas_call(
        body, name=name, grid=(R // tr,),
        in_specs=[pl.BlockSpec((n, tr, C), lambda i: (0, i, 0))],
        out_specs=pl.BlockSpec((tr, C), lambda i: (i, 0)),
        out_shape=jax.ShapeDtypeStruct((R, C), F32),
        compiler_params=_params("parallel"),
    )(x)


def _add_half(view, other, c, name):
    if view.ndim == 3:
        _, R2, N = view.shape
        tr = 128
        grid = (R2 // tr,)
        in_specs = [pl.BlockSpec((1, tr, N), lambda i, c_ref: (c_ref[0], i, 0)), pl.BlockSpec((1, tr, N), lambda i, c_ref: (0, i, 0))]
        out_spec = pl.BlockSpec((tr, N), lambda i, c_ref: (i, 0))
        out_shape = jax.ShapeDtypeStruct((R2, N), BF16)

        def body(c_ref, a_ref, b_ref, o_ref):
            o_ref[...] = (a_ref[0] + b_ref[0]).astype(BF16)
    else:
        S, _, R2, C = view.shape
        grid = (S,)
        in_specs = [pl.BlockSpec((1, 1, R2, C), lambda s, c_ref: (s, c_ref[0], 0, 0)), pl.BlockSpec((1, 1, R2, C), lambda s, c_ref: (s, 0, 0, 0))]
        out_spec = pl.BlockSpec((1, R2, C), lambda s, c_ref: (s, 0, 0))
        out_shape = jax.ShapeDtypeStruct((S, R2, C), BF16)

        def body(c_ref, a_ref, b_ref, o_ref):
            o_ref[0] = (a_ref[0, 0] + b_ref[0, 0]).astype(BF16)

    return pl.pallas_call(
        body, name=name,
        grid_spec=pltpu.PrefetchScalarGridSpec(num_scalar_prefetch=1, grid=grid, in_specs=in_specs, out_specs=out_spec),
        out_shape=out_shape, compiler_params=_params("parallel"),
    )(c, view, other)


def _sum_chips(slots, part, kind, place, name):
    _, R2, C = slots.shape
    tr = min(R2, 128) if kind == "col" else R2

    def body(p_ref, s_ref, own_ref, o_ref):
        me = p_ref[0]
        own = own_ref[...] if kind == "col" else own_ref[0]
        acc = None
        for k in range(4):
            term = jnp.where(me == k, own, s_ref[k]).astype(F32)
            acc = term if acc is None else acc + term
        o_ref[0] = acc

    own_spec = (pl.BlockSpec((tr, C), lambda i, p: (i, p[0])) if kind == "col"
                else pl.BlockSpec((1, R2, C), lambda i, p: (p[0], 0, 0)))
    return pl.pallas_call(
        body, name=name,
        grid_spec=pltpu.PrefetchScalarGridSpec(
            num_scalar_prefetch=1, grid=(R2 // tr,),
            in_specs=[pl.BlockSpec((4, tr, C), lambda i, p: (0, i, 0)), own_spec],
            out_specs=pl.BlockSpec((1, tr, C), lambda i, p: (p[1], i, 0))),
        out_shape=jax.ShapeDtypeStruct((2, R2, C), F32), compiler_params=_params("parallel"),
    )(place, slots, part)


def _place():
    x, y, c = lax.axis_index("x"), lax.axis_index("y"), lax.axis_index("c")
    chips = [(1 - x, y), (x, 1 - y), (1 - x, 1 - y)]
    return x, y, c, 2 * x + y, chips


def _remote(src, dst, send_sems, recv_sems, k, to):
    return pltpu.make_async_remote_copy(src_ref=src, dst_ref=dst, send_sem=send_sems.at[k], recv_sem=recv_sems.at[k],
                                        device_id=to, device_id_type=MESH)


def _full_shape(shard_shape, kind):
    *lead, R, C = shard_shape
    return (*lead, R, 4 * C) if kind == "col" else (*lead, 4 * R, C)


def _slab(full_ref, shard_rc, kind, chip, half=None):
    R, C = shard_rc
    lead = (slice(None),) * (len(full_ref.shape) - 2)
    if kind == "col":
        rows = pl.ds(0, R) if half is None else pl.ds(half * (R // 2), R // 2)
        return full_ref.at[(*lead, rows, pl.ds(chip * C, C))]
    rows = pl.ds(chip * R, R) if half is None else pl.ds(chip * R + half * (R // 2), R // 2)
    return full_ref.at[(*lead, rows, slice(None))]


def _row_half(ref, half):
    R = ref.shape[-2]
    lead = (slice(None),) * (len(ref.shape) - 2)
    return ref.at[(*lead, pl.ds(half * (R // 2), R // 2), slice(None))]


def _allgather_weights(shards, kinds, after):
    n = len(shards)
    full_shapes = [_full_shape(s.shape, k) for s, k in zip(shards, kinds)]

    def body(*refs):
        ins, outs, (send_sems, recv_sems) = refs[:n], refs[n + 1:2 * n + 1], refs[2 * n + 1:]
        x, y, c, me, chips = _place()
        slab = lambda a, chip, half=None: _slab(outs[a], shards[a].shape[-2:], kinds[a], chip, half)
        my_half = lambda a: _row_half(ins[a], c)
        own = [_remote(ins[a], slab(a, me), send_sems, recv_sems, 6 * n + a, (x, y, 1 - c)) for a in range(n)]
        first = [_remote(my_half(a), slab(a, me, c), send_sems, recv_sems, j * n + a, (*chip, c))
                 for j, chip in enumerate(chips) for a in range(n)]
        for cp in first + own:
            cp.start()
        passed = []
        for j, chip in enumerate(chips):
            src = 2 * chip[0] + chip[1]
            for a in range(n):
                _remote(my_half(a), slab(a, src, c), send_sems, recv_sems, j * n + a, (x, y, c)).wait_recv()
                cp = _remote(slab(a, src, c), slab(a, src, c), send_sems, recv_sems, (3 + j) * n + a, (x, y, 1 - c))
                cp.start()
                passed.append(cp)
        for j, chip in enumerate(chips):
            src = 2 * chip[0] + chip[1]
            for a in range(n):
                _remote(my_half(a), slab(a, src, 1 - c), send_sems, recv_sems, (3 + j) * n + a, (x, y, c)).wait_recv()
        for cp in own:
            cp.wait_recv()
        for cp in first + passed + own:
            cp.wait_send()

    return pl.pallas_call(
        body, name="allgather_weights",
        in_specs=[ANY] * (n + 1), out_specs=[ANY] * n,
        out_shape=[jax.ShapeDtypeStruct(s, BF16) for s in full_shapes],
        scratch_shapes=[pltpu.SemaphoreType.DMA((7 * n,)), pltpu.SemaphoreType.DMA((7 * n,))],
        compiler_params=pltpu.CompilerParams(has_side_effects=True),
    )(*shards, after)


HBM = pl.BlockSpec(memory_space=pltpu.HBM)
SEM = pl.BlockSpec(memory_space=pltpu.SEMAPHORE)
DATAFLOW = pltpu.SideEffectType.DATAFLOW_SIDE_EFFECTING


def _split_start(name, bufs, plan, ncopies, after=None):
    nb = len(bufs)
    nin = nb + (after is not None)

    def body(*refs):
        send_sems, recv_sems, token = refs[nin], refs[nin + 1], refs[-1]
        for k, (src, dst, to) in enumerate(plan(refs[:nb])):
            _remote(src, dst, send_sems, recv_sems, k, to).start()
        token[...] = jnp.zeros_like(token)

    out = pl.pallas_call(
        body, name=name,
        out_shape=(pltpu.SemaphoreType.DMA((ncopies,)), pltpu.SemaphoreType.DMA((ncopies,)),
                   *[pltpu.HBM(b.shape, b.dtype) for b in bufs], jax.ShapeDtypeStruct((8, LANES), F32)),
        in_specs=[HBM] * nb + [ANY] * (nin - nb), out_specs=(SEM, SEM, *[HBM] * nb, pl.BlockSpec(memory_space=pltpu.VMEM)),
        input_output_aliases={i: 2 + i for i in range(nb)},
        compiler_params=pltpu.CompilerParams(has_side_effects=DATAFLOW),
    )(*[pltpu.with_memory_space_constraint(b, pltpu.HBM) for b in bufs], *([] if after is None else [after]))
    return out[0], out[1], list(out[2:2 + nb]), out[-1]


def _split_wait(name, send_sems, recv_sems, bufs, plan, after):
    nb = len(bufs)

    def body(*refs):
        s_sems, r_sems = refs[nb], refs[nb + 1]
        for k, (src, dst, to) in enumerate(plan(refs[:nb])):
            cp = _remote(src, dst, s_sems, r_sems, k, to)
            cp.wait_send()
            cp.wait_recv()

    return pl.pallas_call(
        body, name=name,
        out_shape=tuple(pltpu.HBM(b.shape, b.dtype) for b in bufs),
        in_specs=[HBM] * nb + [SEM, SEM, ANY], out_specs=tuple([HBM] * nb),
        input_output_aliases={i: i for i in range(nb)},
        compiler_params=pltpu.CompilerParams(has_side_effects=DATAFLOW),
    )(*bufs, send_sems, recv_sems, after)


def _gather_plan(n, shard_rcs, kinds):
    def plan(refs):
        x, y, c, me, chips = _place()
        out = []
        for a in range(n):
            shard, full = refs[a], refs[n + a]
            out.append((shard, _slab(full, shard_rcs[a], kinds[a], me), (x, y, 1 - c)))
            for chip in chips:
                for cc in (c, 1 - c):
                    out.append((_row_half(shard, c), _slab(full, shard_rcs[a], kinds[a], me, c), (*chip, cc)))
        return out
    return plan


def _exchange_plan(n, kinds):
    def plan(refs):
        x, y, c, me, chips = _place()
        out = []
        for a in range(n):
            part, slots = refs[a], refs[n + a]
            C = slots.shape[2]
            for chip in chips:
                dst = 2 * chip[0] + chip[1]
                src = part.at[:, pl.ds(dst * C, C)] if kinds[a] == "col" else part.at[dst]
                out.append((src, slots.at[me], (*chip, c)))
        return out
    return plan


def _swap_plan(ndims):
    n = len(ndims)

    def plan(refs):
        x, y, c, _, _ = _place()
        out = []
        for a in range(n):
            src = refs[a].at[pl.ds(1 - c, 1)] if ndims[a] == 3 else refs[a].at[:, pl.ds(1 - c, 1)]
            out.append((src, refs[n + a], (x, y, 1 - c)))
        return out
    return plan


def _join_halves(bufs):
    n = len(bufs)

    def body(*refs):
        outs, (send_sems, recv_sems) = refs[n:2 * n], refs[2 * n:]
        x, y, c, _, _ = _place()
        sends = [_remote(outs[a].at[c], outs[a].at[c], send_sems, recv_sems, a, (x, y, 1 - c)) for a in range(n)]
        for cp in sends:
            cp.start()
        for a in range(n):
            _remote(outs[a].at[c], outs[a].at[1 - c], send_sems, recv_sems, a, (x, y, c)).wait_recv()
        for cp in sends:
            cp.wait_send()

    return pl.pallas_call(
        body, name="grad_join_halves", in_specs=[ANY] * n, out_specs=[ANY] * n,
        out_shape=[jax.ShapeDtypeStruct(b.shape, F32) for b in bufs],
        input_output_aliases={a: a for a in range(n)},
        scratch_shapes=[pltpu.SemaphoreType.DMA((n,)), pltpu.SemaphoreType.DMA((n,))],
        compiler_params=pltpu.CompilerParams(has_side_effects=True),
    )(*bufs)


def _allgather_small(block):
    M, N = block.shape

    def body(x_ref, out_ref, send_sems, recv_sems, local_sem):
        x, y, c, _, chips = _place()
        me, sibling = (x, y, c), (x, y, 1 - c)

        def rows(px, py, pc):
            return out_ref.at[pl.ds((4 * px + 2 * py + pc) * M, M), :]

        def copy(k, blk, to, src=None):
            return _remote(rows(*blk) if src is None else src, rows(*blk), send_sems, recv_sems, k, to)

        mine = pltpu.make_async_copy(x_ref, rows(*me), local_sem)
        mine.start()
        first = [copy(0, me, sibling, src=x_ref)] + [copy(1 + j, me, (*chip, c), src=x_ref) for j, chip in enumerate(chips)]
        for cp in first:
            cp.start()
        passed = [copy(4 + j, (*chip, c), sibling) for j, chip in enumerate(chips)]
        for j, chip in enumerate(chips):
            copy(1 + j, (*chip, c), me).wait_recv()
            passed[j].start()
        copy(0, sibling, me).wait_recv()
        for j, chip in enumerate(chips):
            copy(4 + j, (*chip, 1 - c), me).wait_recv()
        for cp in first + passed:
            cp.wait_send()
        mine.wait()

    vm = pl.BlockSpec(memory_space=pltpu.VMEM)
    return pl.pallas_call(
        body, name="allgather_small", in_specs=[vm], out_specs=vm,
        out_shape=jax.ShapeDtypeStruct((8 * M, N), F32),
        scratch_shapes=[pltpu.SemaphoreType.DMA((7,)), pltpu.SemaphoreType.DMA((7,)), pltpu.SemaphoreType.DMA],
        compiler_params=pltpu.CompilerParams(has_side_effects=True, vmem_limit_bytes=VMEM_LIMIT),
    )(block)


def _pack(arrays):
    flat = jnp.concatenate([a.reshape(-1) for a in arrays])
    pad = (-flat.shape[0]) % (8 * LANES)
    return jnp.pad(flat, (0, pad)).reshape(-1, LANES)


def _unpack(packed, shapes):
    flat, out, off = packed.reshape(-1), [], 0
    for s in shapes:
        size = 1
        for d in s:
            size *= d
        out.append(flat[off:off + size].reshape(s))
        off += size
    return out


def _block_diag(pw):
    G, n, _ = pw.shape
    eye = jnp.eye(G, dtype=pw.dtype)
    return (eye[:, None, :, None] * pw[:, :, None, :]).reshape(G * n, G * n)


def _diag_blocks(m, G):
    n = m.shape[0] // G
    return jnp.stack([m[g * n:(g + 1) * n, g * n:(g + 1) * n] for g in range(G)])


def _pad_rows(a, rows):
    return jnp.pad(a, ((0, rows - a.shape[0]), (0, 0)))


def kernel(x, w_in, w_out, conv_w, pool_w, pool_scale, rel_bias, group_gain, pre_mix_g, post_mix_g, pre_ffn_g, post_ffn_g, w_gate_up, w_down, loss_target, m_w_in, m_w_out, m_conv_w, m_pool_w, m_pool_scale, m_rel_bias, m_group_gain, m_pre_mix_g, m_post_mix_g, m_pre_ffn_g, m_post_ffn_g, m_w_gate_up, m_w_down, v_w_in, v_w_out, v_conv_w, v_pool_w, v_pool_scale, v_rel_bias, v_group_gain, v_pre_mix_g, v_post_mix_g, v_pre_ffn_g, v_post_ffn_g, v_w_gate_up, v_w_down):
    L = w_in.shape[0]
    T, D = x.shape[1], x.shape[2]
    DC = D // 4
    NH = rel_bias.shape[1]
    NREL = rel_bias.shape[2]
    G = pool_w.shape[1]
    cs = conv_w.shape[2]
    assert TB == LEFT_CHUNKS * CHUNK and T % TB == 0 and D % (4 * LANES) == 0 and NH * HEAD_DIM == D // 2
    xi, yi, ci = lax.axis_index("x"), lax.axis_index("y"), lax.axis_index("c")
    chip = 2 * xi + yi

    kinds = ("col", "row", "col", "row")
    big = (w_in, w_out, w_gate_up, w_down)
    rcs = [w.shape[-2:] for w in big]
    conv_gathered = _allgather_small(_pack([conv_w]))
    conv_all = conv_gathered.reshape(8, -1)[:, :L * 3 * cs].reshape(4, 2, L, 3, cs)[:, 0]
    conv_full = jnp.moveaxis(conv_all, 0, 2).reshape(L, 3, 4 * cs)
    landing = lambda which: [lax.empty(_full_shape(rcs[k], kinds[k]), BF16) for k in which]
    wi0 = _allgather_weights([w_in[0:1].astype(BF16)], kinds[:1], conv_gathered)[0][0]
    rest_plan = _gather_plan(3, rcs[1:], kinds[1:])
    rest = _split_start("gather_start_0", [w[0].astype(BF16) for w in big[1:]] + landing((1, 2, 3)), rest_plan, 3 * 7, after=wi0)
    layer_plan = _gather_plan(4, rcs, kinds)
    full = [None] * L

    h = x[0]
    saved = []
    token = rest[3]
    for l in range(L):
        g_pre, g_pm, g_pf, g_po = (a[l][None] for a in (pre_mix_g, post_mix_g, pre_ffn_g, post_ffn_g))
        if 0 < l < L - 1:
            nxt = _split_start(f"gather_start_{l + 1}", [w[l + 1].astype(BF16) for w in big] + landing(range(4)), layer_plan, 4 * 7,
                               after=full[l][0])
            token = nxt[3]
        if l < L - 1:
            g_pre = g_pre + token[0:1, 0:1]
        gg, ps = group_gain[l][None], pool_scale[l][None]
        cw = _pad_rows(conv_full[l], 8)
        wbd = _block_diag(pool_w[l]).astype(BF16)
        bias = _bias_build(jnp.pad(rel_bias[l], ((0, 0), (0, RBP - NREL)))[:, None, :])
        wi = wi0 if l == 0 else full[l][0]
        xn, pa, qkv, kv_t = _inproj_fwd(h, g_pre, wi)
        yab = _convpool_fwd(pa, cw, wbd, ps)
        yc, lse = _attn_fwd(qkv, kv_t, bias)
        if l == 0:
            full[0] = [wi0, *_split_wait("gather_wait_0", rest[0], rest[1], rest[2], rest_plan, yc)[3:6]]
            if L > 1:
                nxt = _split_start("gather_start_1", [w[1].astype(BF16) for w in big] + landing(range(4)), layer_plan, 4 * 7,
                                   after=full[0][1])
                g_pm = g_pm + nxt[3][0:1, 0:1]
        wi, wo, wgu, wdn = full[l]
        y, mix, h1, hn, gu, ff, ffo, h2 = _layer_tail_fwd(yab, yc, gg, wo, h, g_pm, g_pf, wgu, wdn, g_po)
        saved.append(dict(h=h, xn=xn, pa=pa, qkv=qkv, kv_t=kv_t, lse=lse, yab=yab, yc=yc, y=y, mix=mix, h1=h1, hn=hn, gu=gu, ff=ff, ffo=ffo,
                          cw=cw, wbd=wbd, bias=bias, ps=ps, gg=gg, g_pre=g_pre, g_pm=g_pm, g_pf=g_pf, g_po=g_po))
        h = h2
        if l + 1 < L:
            full[l + 1] = list(_split_wait(f"gather_wait_{l + 1}", nxt[0], nxt[1], nxt[2], layer_plan, h2)[4:8])

    dh, loss_tile = _loss_grad(h, loss_target[0])
    loss = lax.psum(loss_tile[0, 0], ("x", "y", "c"))

    xplan = _exchange_plan(4, kinds)
    splan = _swap_plan([3, 4, 3, 4])
    place = jnp.stack([chip, ci]).astype(jnp.int32)
    cvec = ci.reshape(1).astype(jnp.int32)
    small_grads = [None] * L
    shard_grads = [None] * L
    dbiases = [None] * L

    def start_exchange(lp, swap):
        thru = _split_wait(f"grad_swap_wait_{lp}", swap[0], swap[1], swap[2], splan, swap[4])
        chip_sums = [_add_half(v, t, cvec, "grad_add_half") for v, t in zip(thru[:4], thru[4:])]
        slots = [lax.empty((4, p.shape[0], p.shape[1] // 4) if k == "col" else p.shape, BF16) for p, k in zip(chip_sums, kinds)]
        ssem, rsem, bufs, tok = _split_start(f"grad_exchange_start_{lp}", chip_sums + slots, xplan, 4 * 3)
        return (lp, ssem, rsem, bufs), tok

    def finish_exchange(pending, after):
        lp, ssem, rsem, thru = pending
        landed = _split_wait(f"grad_exchange_wait_{lp}", ssem, rsem, thru, xplan, after)
        bufs = [_sum_chips(landed[4 + a], landed[a], kinds[a], place, "grad_sum_chips") for a in range(4)]
        shard_grads[lp] = [j.reshape(2 * j.shape[1], j.shape[2]) for j in _join_halves(bufs)]

    swap = None
    exchange = None
    for l in reversed(range(L)):
        s = saved[l]
        wi, wo, wgu, wdn = full[l]
        g_po = s["g_po"] if swap is None else s["g_po"] + swap[3][0:1, 0:1]
        dffo, dgu, dh1, dmix, dyab, dyc, dg_po, dg_pf, dg_pm, dgg = _layer_tail_bwd(
            dh, s["ffo"], g_po, s["gu"], wdn, wgu, s["h1"], s["g_pf"], s["mix"], s["g_pm"], wo, s["yab"], s["yc"], s["gg"])
        after = dyc
        if swap is not None:
            started, after = start_exchange(l + 1, (*swap[:4], dh1))
            if exchange is not None:
                finish_exchange(exchange, after)
            exchange = started
        dq, dk, dv, dbiases[l] = _attn_bwd(s["qkv"], s["kv_t"], dyc, s["yc"], s["lse"], s["bias"], after)
        dpa, dcw, dwbd, dps = _convpool_bwd(s["pa"], dyab, s["cw"], s["wbd"], s["ps"])
        dparts = [dpa, dq, dk, dv]
        dh, dg_pre = _inproj_bwd(dparts, wi, s["h"], s["g_pre"], dh1)
        F2, DFF = s["gu"].shape[1], s["ff"].shape[1]
        grads = [_wgrad_concat(s["xn"], dparts, "wgrad_in"),
                 _wgrad(s["y"], dmix, D, D, "wgrad_out"),
                 _wgrad(s["hn"], dgu, D, F2 // 4, "wgrad_gate_up"),
                 _wgrad(s["ff"], dffo, DFF // 2, D, "wgrad_down")]
        small_grads[l] = [dcw[:3], _diag_blocks(dwbd, G), dps[0], None, dgg[0], dg_pre[0], dg_pm[0], dg_pf[0], dg_po[0]]
        views = [g.reshape(2, g.shape[0] // 2, g.shape[1]) if k == "col" else g.reshape(4, 2, g.shape[0] // 8, g.shape[1])
                 for g, k in zip(grads, kinds)]
        lands = [lax.empty((1,) + v.shape[1:] if v.ndim == 3 else (v.shape[0], 1) + v.shape[2:], F32) for v in views]
        swap = _split_start(f"grad_swap_start_{l}", views + lands, splan, 4)
    last, tok = start_exchange(0, (*swap[:4], swap[3]))
    if exchange is not None:
        finish_exchange(exchange, tok)
    for l in range(L):
        small_grads[l][3] = _bias_fold(dbiases[l], tok)[:, 0, :NREL]

    names_shapes = [(L, 3, 4 * cs), pool_w.shape, pool_scale.shape, rel_bias.shape, group_gain.shape,
                    pre_mix_g.shape, post_mix_g.shape, pre_ffn_g.shape, post_ffn_g.shape]
    small_stacked = [jnp.stack([small_grads[l][k] for l in range(L)]) for k in range(len(names_shapes))]
    packed = _pack(small_stacked)
    M = packed.shape[0]
    total = _sum_slots(_allgather_small(packed).reshape(8, M, LANES), M, "small_sum_devices")
    g_small = _unpack(total, names_shapes)
    g_small[0] = lax.dynamic_slice_in_dim(g_small[0], chip * cs, cs, axis=2)
    finish_exchange(last, total)
    g_big = [jnp.stack([shard_grads[l][k] for l in range(L)]) for k in range(4)]

    def adam_big(w, g, m, v, name):
        shp = w.shape
        two = lambda a: a.reshape(shp[0] * shp[1], shp[2])
        return [o.reshape(shp) for o in _adamw(two(w), two(g), two(m), two(v), 256, name)]

    upd_in = adam_big(w_in, g_big[0], m_w_in, v_w_in, "adamw_in")
    upd_out = adam_big(w_out, g_big[1], m_w_out, v_w_out, "adamw_out")
    upd_gu = adam_big(w_gate_up, g_big[2], m_w_gate_up, v_w_gate_up, "adamw_gate_up")
    upd_dn = adam_big(w_down, g_big[3], m_w_down, v_w_down, "adamw_down")

    small_w = [conv_w, pool_w, pool_scale, rel_bias, group_gain, pre_mix_g, post_mix_g, pre_ffn_g, post_ffn_g]
    small_m = [m_conv_w, m_pool_w, m_pool_scale, m_rel_bias, m_group_gain, m_pre_mix_g, m_post_mix_g, m_pre_ffn_g, m_post_ffn_g]
    small_v = [v_conv_w, v_pool_w, v_pool_scale, v_rel_bias, v_group_gain, v_pre_mix_g, v_post_mix_g, v_pre_ffn_g, v_post_ffn_g]
    pw_, pg_, pm_, pv_ = _pack(small_w), _pack(g_small), _pack(small_m), _pack(small_v)
    shapes = [w.shape for w in small_w]
    upd_small = [_unpack(o, shapes) for o in _adamw(pw_, pg_, pm_, pv_, pw_.shape[0], "adamw_small")]

    def ordered(big4, small9):
        return [big4[0], big4[1], *small9, big4[2], big4[3]]

    grads = ordered(g_big, g_small)
    outs = [ordered([upd_in[k], upd_out[k], upd_gu[k], upd_dn[k]], upd_small[k]) for k in range(3)]
    return (loss, dh[None], *grads, *outs[0], *outs[1], *outs[2])
```

```python
import functools

import jax
import jax.numpy as jnp
from jax import lax
from jax.experimental import pallas as pl
from jax.experimental.pallas import tpu as pltpu

F32, BF16 = jnp.float32, jnp.bfloat16
EPS = 1e-6
CHUNK = 64
LEFT_CHUNKS = 8
REL_CLIP = 128
HEAD_DIM = 64
LANES = 128
POOL_WINDOWS = (2, 4, 8, 16)
HALO = 16
TB = LEFT_CHUNKS * CHUNK
TBF = 256
TBW = 1024
BAND = (LEFT_CHUNKS + 1) * CHUNK
SUB = 2 * CHUNK
BANDW = SUB + LEFT_CHUNKS * CHUNK
SKEW = 768
NEG = -1e30
RBP = 384
VMEM_LIMIT = 56 * 1024 * 1024
ADAM_LR, ADAM_B1, ADAM_B2, ADAM_EPS, ADAM_WD, ADAM_STEP = 0.001, 0.9, 0.999, 1e-08, 0.01, 10
MESH = pl.DeviceIdType.MESH
ANY = pl.BlockSpec(memory_space=pl.ANY)


def _params(*sem):
    kw = dict(vmem_limit_bytes=VMEM_LIMIT)
    if sem:
        kw["dimension_semantics"] = sem
    return pltpu.CompilerParams(**kw)


def _dot(a, b):
    return jnp.dot(a, b, preferred_element_type=F32)


def _dot_nt(a, b):
    return lax.dot_general(a, b, (((1,), (1,)), ((), ())), preferred_element_type=F32)


def _dot_tn(a, b):
    return lax.dot_general(a, b, (((0,), (0,)), ((), ())), preferred_element_type=F32)


def _rms(x, g):
    r = lax.rsqrt(jnp.mean(x * x, axis=-1, keepdims=True) + EPS)
    return x * r * g


def _rms_bwd(dy, x, g):
    r = lax.rsqrt(jnp.mean(x * x, axis=-1, keepdims=True) + EPS)
    xh = x * r
    dxh = dy * g
    dx = r * (dxh - xh * jnp.mean(dxh * xh, axis=-1, keepdims=True))
    return dx, jnp.sum(dy * xh, axis=0, keepdims=True)


def _full(shape):
    return pl.BlockSpec(shape, lambda *_: (0,) * len(shape))


def _acc_init(step, *refs):
    @pl.when(step == 0)
    def _():
        for r in refs:
            r[...] = jnp.zeros_like(r)


def _inproj_mixers_fwd(h, g, w, cw, wbd, ps):
    T, D = h.shape
    NQ = w.shape[1] - D
    NKV = 2 * NQ // 3
    DC = D // 4

    def body(h_ref, g_ref, w_ref, cw_ref, wbd_ref, ps_ref, xn_ref, pa_ref, qkv_ref, kvt_ref, yab_ref, halo):
        i = pl.program_id(0)
        _acc_init(i, halo)
        xn = _rms(h_ref[...], g_ref[...]).astype(BF16)
        xn_ref[...] = xn
        x = _dot(xn, w_ref[:, :D])
        pa_ref[...] = x
        qkv = _dot(xn, w_ref[:, D:])
        qkv_ref[...] = qkv.astype(BF16)
        kvt_ref[...] = qkv[:, NQ - NKV:].T.astype(BF16)
        hl = halo[...]
        gb, gc, u, pu = (x[:, k * DC:(k + 1) * DC] for k in range(4))
        z = gc * u
        z1, z2 = _conv_taps(z, hl[:, DC:2 * DC] * hl[:, 2 * DC:3 * DC])
        cwv = cw_ref[...]
        ya = gb * (cwv[2:3] * z + cwv[1:2] * z1 + cwv[0:1] * z2)
        d = _pool_d(pu, hl[:, 3 * DC:], _pool_count(i, DC))
        yb = _dot(d.astype(BF16), wbd_ref[...]) * ps_ref[...]
        yab_ref[...] = jnp.concatenate([ya, yb], axis=1)
        halo[...] = x[TB - HALO:, :]

    row = lambda n: pl.BlockSpec((TB, n), lambda i: (i, 0))
    return pl.pallas_call(
        body, name="inproj_mixers_fwd", grid=(T // TB,),
        in_specs=[row(D), _full((1, D)), _full(w.shape), _full((8, DC)), _full((DC, DC)), _full((1, DC))],
        out_specs=[row(D), row(D), row(NQ), pl.BlockSpec((NKV, TB), lambda i: (0, i)), row(2 * DC)],
        out_shape=[jax.ShapeDtypeStruct((T, D), BF16), jax.ShapeDtypeStruct((T, D), F32), jax.ShapeDtypeStruct((T, NQ), BF16),
                   jax.ShapeDtypeStruct((NKV, T), BF16), jax.ShapeDtypeStruct((T, 2 * DC), F32)],
        scratch_shapes=[pltpu.VMEM((HALO, D), F32)],
        compiler_params=_params("arbitrary"),
    )(h, g, w, cw, wbd, ps)


def _lane_groups(n, vals):
    lane = lax.broadcasted_iota(jnp.int32, (1, n), 1)
    q = n // 4
    return jnp.where(lane < q, vals[0], jnp.where(lane < 2 * q, vals[1], jnp.where(lane < 3 * q, vals[2], vals[3]))).astype(F32)


def _pick_group(levels, n):
    lane = lax.broadcasted_iota(jnp.int32, (1, n), 1)
    q = n // 4
    return jnp.where(lane < q, levels[0], jnp.where(lane < 2 * q, levels[1], jnp.where(lane < 3 * q, levels[2], levels[3])))


def _pool_count(blk, n):
    t1 = (blk * TB + 1 + lax.broadcasted_iota(jnp.int32, (TB, 1), 0)).astype(F32)
    return jnp.minimum(t1, _lane_groups(n, POOL_WINDOWS))


def _pool_d(pu, pu_halo, cnt):
    e = jnp.concatenate([pu_halo, pu], axis=0)
    s2 = e + pltpu.roll(e, 1, 0)
    s4 = s2 + pltpu.roll(s2, 2, 0)
    s8 = s4 + pltpu.roll(s4, 4, 0)
    s16 = s8 + pltpu.roll(s8, 8, 0)
    num = _pick_group([s2, s4, s8, s16], pu.shape[1])[HALO:]
    return num / cnt - pu


def _conv_taps(z, z_halo):
    e = jnp.concatenate([z_halo, z], axis=0)
    return pltpu.roll(e, 1, 0)[HALO:], pltpu.roll(e, 2, 0)[HALO:]


def _bias_bins(shape, col_dim):
    j = lax.broadcasted_iota(jnp.int32, shape, col_dim)
    b = lax.broadcasted_iota(jnp.int32, shape, 1 - col_dim)
    d = jnp.where(j < BAND, j, j - SKEW)
    live = jnp.logical_or(j < BAND, j > SKEW - CHUNK)
    bins = jnp.minimum(TB - d, REL_CLIP) + REL_CLIP
    return jnp.where(jnp.logical_and(live, bins == b), 1.0, 0.0).astype(F32)


def _skew_rows(x, left):
    row = lax.broadcasted_iota(jnp.int32, x.shape, 0)
    for b in range(SUB.bit_length() - 1):
        x = jnp.where(((row >> b) & 1) == 1, pltpu.roll(x, SKEW - (1 << b) if left else 1 << b, 1), x)
    return x


def _bias_build(rb):
    H = rb.shape[0]

    def body(rb_ref, o_ref):
        v = jnp.dot(rb_ref[...], _bias_bins((RBP, SKEW), 1), precision=lax.Precision.HIGHEST, preferred_element_type=F32)
        qc = lax.broadcasted_iota(jnp.int32, (SUB, BANDW), 0) >> (CHUNK.bit_length() - 1)
        kc = lax.broadcasted_iota(jnp.int32, (SUB, BANDW), 1) >> (CHUNK.bit_length() - 1)
        in_band = jnp.logical_and(kc >= qc, kc <= qc + LEFT_CHUNKS)
        for h in range(H):
            x = _skew_rows(jnp.broadcast_to(v[h:h + 1], (SUB, SKEW)), left=False)
            o_ref[h // 2, :, (h % 2) * SUB:(h % 2 + 1) * SUB] = jnp.where(in_band, x[:, :BANDW], NEG).T

    return pl.pallas_call(
        body, name="bias_build", in_specs=[_full(rb.shape)], out_specs=_full((H // 2, BANDW, 2 * SUB)),
        out_shape=jax.ShapeDtypeStruct((H // 2, BANDW, 2 * SUB), F32), grid=(1,),
        compiler_params=_params("arbitrary"),
    )(rb)


def _bias_fold(db, after):
    H = 2 * db.shape[0]

    def body(db_ref, after_ref, o_ref, sums):
        sums[...] = jnp.zeros_like(sums)
        for h in range(H):
            x = db_ref[h // 2, :, (h % 2) * SUB:(h % 2 + 1) * SUB].T
            x = _skew_rows(jnp.concatenate([x, jnp.zeros((SUB, SKEW - BANDW), F32)], axis=1), left=True)
            sums[h:h + 1, :] = jnp.sum(x, axis=0, keepdims=True)
        o_ref[...] = jnp.dot(sums[...], _bias_bins((SKEW, RBP), 0), precision=lax.Precision.HIGHEST, preferred_element_type=F32)

    return pl.pallas_call(
        body, name="bias_fold", grid=(1,), in_specs=[_full(db.shape), ANY], out_specs=_full((8, RBP)),
        out_shape=jax.ShapeDtypeStruct((8, RBP), F32), scratch_shapes=[pltpu.VMEM((8, SKEW), F32)],
        compiler_params=_params("arbitrary"),
    )(db, after)


def _both_heads(x):
    first = lax.broadcasted_iota(jnp.int32, (1, LANES), 1) < HEAD_DIM
    return jnp.concatenate([jnp.where(first, x, 0), jnp.where(first, 0, x)], axis=0)


def _own_head_rows(x2):
    n = x2.shape[1] // 2
    first = lax.broadcasted_iota(jnp.int32, (LANES, 1), 0) < HEAD_DIM
    return jnp.where(first, x2[:, :n], x2[:, n:])


def _key_tiles(s, first_block):
    return [t for t in range(s, s + BANDW // SUB) if not (first_block and t < TB // SUB)]


def _attn_fwd(qkv, kv_t, bias_t):
    T = qkv.shape[0]
    NP = qkv.shape[1] // (3 * LANES)
    NB = T // TB
    NS = TB // SUB
    scale = HEAD_DIM ** -0.5

    def body(q_ref, kc_ref, kp_ref, vtc_ref, vtp_ref, b_ref, o_ref, lse_ref):
        i = pl.program_id(1)

        def compute(first_block):
            q = q_ref[...] * scale
            kwin = jnp.concatenate([kp_ref[...], kc_ref[...]], axis=0)
            vt = jnp.concatenate([vtp_ref[...], vtc_ref[...]], axis=1)
            for s in range(NS):
                rows = slice(s * SUB, (s + 1) * SUB)
                tiles = _key_tiles(s, first_block)
                keys = slice(tiles[0] * SUB, (tiles[-1] + 1) * SUB)
                brows = slice((tiles[0] - s) * SUB, (tiles[-1] - s + 1) * SUB)
                q2 = _both_heads(q[rows])
                halves = []
                for a in range(2):
                    st = _dot_nt(kwin[keys], q2[a * SUB:(a + 1) * SUB]) + b_ref[0, brows, a * SUB:(a + 1) * SUB]
                    m = jnp.max(st, axis=0, keepdims=True)
                    p = jnp.exp(st - m)
                    l = jnp.sum(p, axis=0, keepdims=True)
                    halves.append(_dot(vt[:, keys], p.astype(BF16)) * (1.0 / l))
                    lse_ref[0, 0, a * NS + s:a * NS + s + 1, :] = m + jnp.log(l)
                o_ref[rows, :] = _own_head_rows(jnp.concatenate(halves, axis=1)).T

        pl.when(i == 0)(functools.partial(compute, True))
        pl.when(i > 0)(functools.partial(compute, False))

    prev = lambda i: jnp.maximum(i - 1, 0)
    return pl.pallas_call(
        body, name="attn_fwd", grid=(NP, NB),
        in_specs=[pl.BlockSpec((TB, LANES), lambda p, i: (i, p)),
                  pl.BlockSpec((TB, LANES), lambda p, i: (i, NP + p)),
                  pl.BlockSpec((TB, LANES), lambda p, i: (prev(i), NP + p)),
                  pl.BlockSpec((LANES, TB), lambda p, i: (NP + p, i)),
                  pl.BlockSpec((LANES, TB), lambda p, i: (NP + p, prev(i))),
                  pl.BlockSpec((1, BANDW, 2 * SUB), lambda p, i: (p, 0, 0))],
        out_specs=[pl.BlockSpec((TB, LANES), lambda p, i: (i, p)),
                   pl.BlockSpec((1, 1, 8, LANES), lambda p, i: (p, i, 0, 0))],
        out_shape=[jax.ShapeDtypeStruct((T, NP * LANES), F32), jax.ShapeDtypeStruct((NP, NB, 8, LANES), F32)],
        compiler_params=_params("parallel", "parallel"),
    )(qkv, qkv, qkv, kv_t, kv_t, bias_t)


def _group_bounds(D):
    return ((0, D // 4), (D // 4, D // 2), (D // 2, D))


def _load_resident(step, *pairs_and_sems):
    @pl.when(step == 0)
    def _():
        cps = [pltpu.make_async_copy(src, dst, sem) for src, dst, sem in pairs_and_sems]
        for cp in cps:
            cp.start()
        for cp in cps:
            cp.wait()


def _layer_tail_fwd(yab, yc, gg, w_out, h, g_pm, g_pf, w_gu, w_dn, g_po):
    T, D = h.shape
    F2 = w_gu.shape[1]
    DFF = F2 // 2
    TF = DFF // 2

    def body(yab_ref, yc_ref, gg_ref, wo_hbm, h_ref, gpm_ref, gpf_ref, wgu_hbm, wdn_hbm, gpo_ref,
             y_ref, mix_ref, h1_ref, hn_ref, gu_ref, ff_ref, ffo_ref, h2_ref, wo_v, wgu_v, wdn_v, sems):
        _load_resident(pl.program_id(0), (wo_hbm, wo_v, sems.at[0]), (wgu_hbm, wgu_v, sems.at[1]), (wdn_hbm, wdn_v, sems.at[2]))
        yraw = jnp.concatenate([yab_ref[...], yc_ref[...]], axis=1)
        ggv = gg_ref[...]
        y = jnp.concatenate([_rms(yraw[:, a:b], ggv[:, a:b]) for a, b in _group_bounds(D)], axis=1).astype(BF16)
        y_ref[...] = y
        mix = _dot(y, wo_v[...])
        mix_ref[...] = mix.astype(BF16)
        h1 = h_ref[...] + _rms(mix, gpm_ref[...])
        h1_ref[...] = h1
        hn = _rms(h1, gpf_ref[...]).astype(BF16)
        hn_ref[...] = hn
        ffo = jnp.zeros((TBF, D), F32)
        for j in range(2):
            sg, su = slice(j * TF, (j + 1) * TF), slice(DFF + j * TF, DFF + (j + 1) * TF)
            gate = _dot(hn, wgu_v[:, sg])
            up = _dot(hn, wgu_v[:, su])
            gu_ref[:, sg] = gate.astype(BF16)
            gu_ref[:, su] = up.astype(BF16)
            ff = (gate * jax.nn.sigmoid(gate) * up).astype(BF16)
            ff_ref[:, sg] = ff
            ffo = ffo + _dot(ff, wdn_v[sg, :])
        ffo_ref[...] = ffo.astype(BF16)
        h2_ref[...] = h1 + _rms(ffo, gpo_ref[...])

    row = lambda n: pl.BlockSpec((TBF, n), lambda i: (i, 0))
    gain = _full((1, D))
    f32, bf16 = (lambda n: jax.ShapeDtypeStruct((T, n), F32)), (lambda n: jax.ShapeDtypeStruct((T, n), BF16))
    return pl.pallas_call(
        body, name="layer_tail_fwd", grid=(T // TBF,),
        in_specs=[row(D // 2), row(D // 2), gain, ANY, row(D), gain, gain, ANY, ANY, gain],
        out_specs=[row(D), row(D), row(D), row(D), row(F2), row(DFF), row(D), row(D)],
        out_shape=[bf16(D), bf16(D), f32(D), bf16(D), bf16(F2), bf16(DFF), bf16(D), f32(D)],
        scratch_shapes=[pltpu.VMEM(w_out.shape, BF16), pltpu.VMEM(w_gu.shape, BF16), pltpu.VMEM(w_dn.shape, BF16),
                        pltpu.SemaphoreType.DMA((3,))],
        compiler_params=_params("arbitrary"),
    )(yab, yc, gg, w_out, h, g_pm, g_pf, w_gu, w_dn, g_po)


def _loss_grad(h, tgt):
    T, D = h.shape

    def body(h_ref, t_ref, dh_ref, loss_ref):
        _acc_init(pl.program_id(0), loss_ref)
        diff = h_ref[...] - t_ref[...]
        dh_ref[...] = diff * (1.0 / D)
        loss_ref[...] += 0.5 * jnp.sum(jnp.mean(diff * diff, axis=-1, keepdims=True))

    row = pl.BlockSpec((TB, D), lambda i: (i, 0))
    return pl.pallas_call(
        body, name="loss_grad", grid=(T // TB,),
        in_specs=[row, row], out_specs=[row, _full((8, LANES))],
        out_shape=[jax.ShapeDtypeStruct((T, D), F32), jax.ShapeDtypeStruct((8, LANES), F32)],
        compiler_params=_params("arbitrary"),
    )(h, tgt)


def _layer_tail_bwd(dh2, ffo, g_po, gu, w_dn, w_gu, h1, g_pf, mix, g_pm, w_out, yab, yc, gg):
    T, D = dh2.shape
    F2 = gu.shape[1]
    DFF = F2 // 2
    TF = DFF // 2

    def body(dh_ref, ffo_ref, gpo_ref, gu_ref, wdn_hbm, wgu_hbm, h_ref, gpf_ref, mix_ref, gpm_ref, wo_hbm, yab_ref, yc_ref, gg_ref,
             dffo_ref, dgu_ref, dh1_ref, dmix_ref, dyab_ref, dyc_ref, dgpo_ref, dgpf_ref, dgpm_ref, dgg_ref,
             wdn_v, wgu_v, wo_v, sems):
        step = pl.program_id(0)
        _load_resident(step, (wdn_hbm, wdn_v, sems.at[0]), (wgu_hbm, wgu_v, sems.at[1]), (wo_hbm, wo_v, sems.at[2]))
        _acc_init(step, dgpo_ref, dgpf_ref, dgpm_ref, dgg_ref)
        dh = dh_ref[...]
        dffo, dg = _rms_bwd(dh, ffo_ref[...].astype(F32), gpo_ref[...])
        dgpo_ref[0:1, :] += dg
        dffo = dffo.astype(BF16)
        dffo_ref[...] = dffo
        dhn = jnp.zeros((TBF, D), F32)
        for j in range(2):
            sg, su = slice(j * TF, (j + 1) * TF), slice(DFF + j * TF, DFF + (j + 1) * TF)
            dff = _dot_nt(dffo, wdn_v[sg, :])
            gate, up = gu_ref[:, sg].astype(F32), gu_ref[:, su].astype(F32)
            sig = jax.nn.sigmoid(gate)
            dgate = (dff * up * (sig * (1.0 + gate * (1.0 - sig)))).astype(BF16)
            dup = (dff * (gate * sig)).astype(BF16)
            dgu_ref[:, sg] = dgate
            dgu_ref[:, su] = dup
            dhn = dhn + _dot_nt(dgate, wgu_v[:, sg]) + _dot_nt(dup, wgu_v[:, su])
        dx, dg = _rms_bwd(dhn, h_ref[...], gpf_ref[...])
        dgpf_ref[0:1, :] += dg
        dh1 = dh + dx
        dh1_ref[...] = dh1
        dmix, dg = _rms_bwd(dh1, mix_ref[...].astype(F32), gpm_ref[...])
        dgpm_ref[0:1, :] += dg
        dmix = dmix.astype(BF16)
        dmix_ref[...] = dmix
        dy = _dot_nt(dmix, wo_v[...])
        yraw = jnp.concatenate([yab_ref[...], yc_ref[...]], axis=1)
        ggv = gg_ref[...]
        parts = [_rms_bwd(dy[:, a:b], yraw[:, a:b], ggv[:, a:b]) for a, b in _group_bounds(D)]
        dgg_ref[0:1, :] += jnp.concatenate([p[1] for p in parts], axis=1)
        dyab_ref[...] = jnp.concatenate([parts[0][0], parts[1][0]], axis=1)
        dyc_ref[...] = parts[2][0]

    row = lambda n: pl.BlockSpec((TBF, n), lambda i: (i, 0))
    gain, acc = _full((1, D)), _full((8, D))
    f32, bf16 = (lambda n: jax.ShapeDtypeStruct((T, n), F32)), (lambda n: jax.ShapeDtypeStruct((T, n), BF16))
    acc_shape = jax.ShapeDtypeStruct((8, D), F32)
    return pl.pallas_call(
        body, name="layer_tail_bwd", grid=(T // TBF,),
        in_specs=[row(D), row(D), gain, row(F2), ANY, ANY, row(D), gain, row(D), gain, ANY, row(D // 2), row(D // 2), gain],
        out_specs=[row(D), row(F2), row(D), row(D), row(D // 2), row(D // 2), acc, acc, acc, acc],
        out_shape=[bf16(D), bf16(F2), f32(D), bf16(D), f32(D // 2), f32(D // 2), acc_shape, acc_shape, acc_shape, acc_shape],
        scratch_shapes=[pltpu.VMEM(w_dn.shape, BF16), pltpu.VMEM(w_gu.shape, BF16), pltpu.VMEM(w_out.shape, BF16),
                        pltpu.SemaphoreType.DMA((3,))],
        compiler_params=_params("arbitrary"),
    )(dh2, ffo, g_po, gu, w_dn, w_gu, h1, g_pf, mix, g_pm, w_out, yab, yc, gg)


def _attn_bwd(qkv, kv_t, dyc, yc, lse, bias_t, after):
    T = qkv.shape[0]
    NP = qkv.shape[1] // (3 * LANES)
    NB = T // TB
    NS = TB // SUB
    scale = HEAD_DIM ** -0.5

    def body(q_ref, kc_ref, kp_ref, vc_ref, vp_ref, ktc_ref, ktp_ref, do_ref, o_ref, lse_ref, b_ref, after_ref,
             dq_ref, dk_ref, dv_ref, db_ref, dk_carry, dv_carry, dkw, dvw):
        step = pl.program_id(1)
        i = NB - 1 - step
        _acc_init(step, dk_carry, dv_carry, db_ref)

        def compute(first_block):
            q = q_ref[...] * scale
            kwin = jnp.concatenate([kp_ref[...], kc_ref[...]], axis=0)
            vwin = jnp.concatenate([vp_ref[...], vc_ref[...]], axis=0)
            kt = jnp.concatenate([ktp_ref[...], ktc_ref[...]], axis=1)
            do = do_ref[...]
            dob = do.astype(BF16)
            prod = do * o_ref[...]
            first = lax.broadcasted_iota(jnp.int32, (1, LANES), 1) < HEAD_DIM
            ones = jnp.ones((8, LANES), F32)
            row_sums = lambda x: lax.dot_general(ones, x, (((1,), (1,)), ((), ())), precision=lax.Precision.HIGHEST,
                                                 preferred_element_type=F32)
            deltas = (row_sums(jnp.where(first, prod, 0.0)), row_sums(jnp.where(first, 0.0, prod)))
            written = set()
            for s in range(NS):
                rows = slice(s * SUB, (s + 1) * SUB)
                tiles = _key_tiles(s, first_block)
                keys = slice(tiles[0] * SUB, (tiles[-1] + 1) * SUB)
                brows = slice((tiles[0] - s) * SUB, (tiles[-1] - s + 1) * SUB)
                q2, do2 = _both_heads(q[rows]), _both_heads(dob[rows])
                lse = jnp.concatenate([lse_ref[0, 0, s:s + 1, :], lse_ref[0, 0, NS + s:NS + s + 1, :]], axis=1)
                delta = jnp.concatenate([deltas[0][0:1, rows], deltas[1][0:1, rows]], axis=1)
                p = jnp.exp(_dot_nt(kwin[keys], q2) + b_ref[0, brows, :] - lse)
                ds = p * (_dot_nt(vwin[keys], do2) - delta)
                db_ref[0, brows, :] += ds
                dsb = ds.astype(BF16)
                dk_c = _dot(dsb, q2)
                dv_c = _dot(p.astype(BF16), do2)
                dq_ref[rows, :] = (_own_head_rows(_dot(kt[:, keys], dsb)).T * scale).astype(BF16)
                for n, t in enumerate(tiles):
                    win, loc = slice(t * SUB, (t + 1) * SUB), slice(n * SUB, (n + 1) * SUB)
                    if t in written:
                        dkw[win, :] += dk_c[loc]
                        dvw[win, :] += dv_c[loc]
                    else:
                        dkw[win, :] = dk_c[loc]
                        dvw[win, :] = dv_c[loc]
                        written.add(t)
            dk_ref[...] = (dkw[TB:, :] + dk_carry[...]).astype(BF16)
            dv_ref[...] = (dvw[TB:, :] + dv_carry[...]).astype(BF16)
            if not first_block:
                dk_carry[...] = dkw[:TB, :]
                dv_carry[...] = dvw[:TB, :]

        pl.when(i == 0)(functools.partial(compute, True))
        pl.when(i > 0)(functools.partial(compute, False))

    blk = lambda s: NB - 1 - s
    prev = lambda s: jnp.maximum(NB - 2 - s, 0)
    rows = lambda which, off: pl.BlockSpec((TB, LANES), lambda p, s: (which(s), off + p))
    out = jax.ShapeDtypeStruct((T, NP * LANES), BF16)
    strip = pl.BlockSpec((1, BANDW, 2 * SUB), lambda p, s: (p, 0, 0))
    return pl.pallas_call(
        body, name="attn_bwd", grid=(NP, NB),
        in_specs=[rows(blk, 0), rows(blk, NP), rows(prev, NP), rows(blk, 2 * NP), rows(prev, 2 * NP),
                  pl.BlockSpec((LANES, TB), lambda p, s: (p, blk(s))), pl.BlockSpec((LANES, TB), lambda p, s: (p, prev(s))),
                  rows(blk, 0), rows(blk, 0), pl.BlockSpec((1, 1, 8, LANES), lambda p, s: (p, blk(s), 0, 0)), strip, ANY],
        out_specs=[rows(blk, 0), rows(blk, 0), rows(blk, 0), strip],
        out_shape=[out, out, out, jax.ShapeDtypeStruct((NP, BANDW, 2 * SUB), F32)],
        scratch_shapes=[pltpu.VMEM((TB, LANES), F32), pltpu.VMEM((TB, LANES), F32),
                        pltpu.VMEM((2 * TB, LANES), F32), pltpu.VMEM((2 * TB, LANES), F32)],
        compiler_params=_params("arbitrary", "arbitrary"),
    )(qkv, qkv, qkv, qkv, qkv, kv_t, kv_t, dyc, yc, lse, bias_t, after)


def _convpool_bwd(pa, dyab, cw, wbd, ps):
    T, D = pa.shape
    DC = D // 4
    NB = T // TB
    N = TB + HALO

    def body(pa_ref, halo_ref, dy_ref, cw_ref, wbd_ref, ps_ref, dpa_ref, dcw_ref, dwbd_ref, dps_ref, dc_carry, e_carry):
        step = pl.program_id(0)
        i = NB - 1 - step
        _acc_init(step, dcw_ref, dwbd_ref, dps_ref, dc_carry, e_carry)
        x = pa_ref[...]
        hl = jnp.where(i > 0, halo_ref[...], 0.0)
        gb, gc, u, pu = (x[:, k * DC:(k + 1) * DC] for k in range(4))
        dy = dy_ref[...]
        dya, dyb = dy[:, :DC], dy[:, DC:]
        cwv = cw_ref[...]
        z = gc * u
        z1, z2 = _conv_taps(z, hl[:, DC:2 * DC] * hl[:, 2 * DC:3 * DC])
        dgb = dya * (cwv[2:3] * z + cwv[1:2] * z1 + cwv[0:1] * z2)
        dconv = dya * gb
        dcw_ref[0:1, :] += jnp.sum(dconv * z2, axis=0, keepdims=True)
        dcw_ref[1:2, :] += jnp.sum(dconv * z1, axis=0, keepdims=True)
        dcw_ref[2:3, :] += jnp.sum(dconv * z, axis=0, keepdims=True)
        ext = jnp.concatenate([dconv, dc_carry[...]], axis=0)
        dz = cwv[2:3] * dconv + cwv[1:2] * pltpu.roll(ext, N - 1, 0)[:TB] + cwv[0:1] * pltpu.roll(ext, N - 2, 0)[:TB]
        dc_carry[...] = dconv[:HALO]
        cnt = _pool_count(i, DC)
        d = _pool_d(pu, hl[:, 3 * DC:], cnt).astype(BF16)
        psv = ps_ref[...]
        w = wbd_ref[...]
        dps_ref[0:1, :] += jnp.sum(dyb * _dot(d, w), axis=0, keepdims=True)
        dys = (dyb * psv).astype(BF16)
        dwbd_ref[...] += _dot_tn(d, dys)
        dd = _dot_nt(dys, w)
        e = dd / cnt
        ext = jnp.concatenate([e, e_carry[...]], axis=0)
        a2 = ext + pltpu.roll(ext, N - 1, 0)
        a4 = a2 + pltpu.roll(a2, N - 2, 0)
        a8 = a4 + pltpu.roll(a4, N - 4, 0)
        a16 = a8 + pltpu.roll(a8, N - 8, 0)
        dpu = _pick_group([a2, a4, a8, a16], DC)[:TB] - dd
        e_carry[...] = e[:HALO]
        dpa_ref[...] = jnp.concatenate([dgb, dz * u, dz * gc, dpu], axis=1).astype(BF16)

    blk = lambda n: pl.BlockSpec((TB, n), lambda s: (NB - 1 - s, 0))
    return pl.pallas_call(
        body, name="convpool_bwd", grid=(NB,),
        in_specs=[blk(D), pl.BlockSpec((HALO, D), lambda s: (jnp.maximum((NB - 1 - s) * (TB // HALO) - 1, 0), 0)),
                  blk(2 * DC), _full((8, DC)), _full((DC, DC)), _full((1, DC))],
        out_specs=[blk(D), _full((8, DC)), _full((DC, DC)), _full((8, DC))],
        out_shape=[jax.ShapeDtypeStruct((T, D), BF16), jax.ShapeDtypeStruct((8, DC), F32),
                   jax.ShapeDtypeStruct((DC, DC), F32), jax.ShapeDtypeStruct((8, DC), F32)],
        scratch_shapes=[pltpu.VMEM((HALO, DC), F32), pltpu.VMEM((HALO, DC), F32)],
        compiler_params=_params("arbitrary"),
    )(pa, pa, dyab, cw, wbd, ps)


def _inproj_bwd(dparts, w, h, g, dh1):
    T, D = h.shape
    widths = [p.shape[1] for p in dparts]
    n = len(dparts)

    def body(*refs):
        parts, (w_ref, h_ref, g_ref, dh1_ref, dh_ref, dg_ref) = refs[:n], refs[n:]
        _acc_init(pl.program_id(0), dg_ref)
        dxn, off = jnp.zeros((TB, D), F32), 0
        for p_ref, wd in zip(parts, widths):
            dxn = dxn + _dot_nt(p_ref[...], w_ref[:, off:off + wd])
            off += wd
        dx, dg = _rms_bwd(dxn, h_ref[...], g_ref[...])
        dg_ref[0:1, :] += dg
        dh_ref[...] = dh1_ref[...] + dx

    row = lambda m: pl.BlockSpec((TB, m), lambda i: (i, 0))
    return pl.pallas_call(
        body, name="inproj_bwd", grid=(T // TB,),
        in_specs=[row(wd) for wd in widths] + [_full(w.shape), row(D), _full((1, D)), row(D)],
        out_specs=[row(D), _full((8, D))],
        out_shape=[jax.ShapeDtypeStruct((T, D), F32), jax.ShapeDtypeStruct((8, D), F32)],
        compiler_params=_params("arbitrary"),
    )(*dparts, w, h, g, dh1)


def _wgrad(a, b, tk, tn, name):
    T, K = a.shape
    N = b.shape[1]

    def body(a_ref, b_ref, o_ref):
        _acc_init(pl.program_id(2), o_ref)
        o_ref[...] += _dot_tn(a_ref[...], b_ref[...])

    return pl.pallas_call(
        body, name=name, grid=(K // tk, N // tn, T // TBW),
        in_specs=[pl.BlockSpec((TBW, tk), lambda k, n, t: (t, k)), pl.BlockSpec((TBW, tn), lambda k, n, t: (t, n))],
        out_specs=pl.BlockSpec((tk, tn), lambda k, n, t: (k, n)),
        out_shape=jax.ShapeDtypeStruct((K, N), F32),
        compiler_params=_params("parallel", "parallel", "arbitrary"),
    )(a, b)


def _wgrad_concat(a, bs, name):
    T, K = a.shape
    widths = [b.shape[1] for b in bs]
    n = len(bs)

    def body(*refs):
        a_ref, b_refs, o_ref = refs[0], refs[1:1 + n], refs[1 + n]
        _acc_init(pl.program_id(0), o_ref)
        av, off = a_ref[...], 0
        for b_ref, wd in zip(b_refs, widths):
            o_ref[:, off:off + wd] += _dot_tn(av, b_ref[...])
            off += wd

    row = lambda m: pl.BlockSpec((TB, m), lambda t: (t, 0))
    return pl.pallas_call(
        body, name=name, grid=(T // TB,),
        in_specs=[row(K)] + [row(wd) for wd in widths],
        out_specs=_full((K, sum(widths))),
        out_shape=jax.ShapeDtypeStruct((K, sum(widths)), F32),
        compiler_params=_params("arbitrary"),
    )(a, *bs)


def _adamw(w, g, m, v, tr, name):
    R, C = w.shape

    def body(w_ref, g_ref, m_ref, v_ref, d_ref, nm_ref, nv_ref):
        gv = g_ref[...]
        nm = ADAM_B1 * m_ref[...] + (1.0 - ADAM_B1) * gv
        nv = ADAM_B2 * v_ref[...] + (1.0 - ADAM_B2) * (gv * gv)
        m_hat = nm / (1.0 - ADAM_B1 ** ADAM_STEP)
        v_hat = nv / (1.0 - ADAM_B2 ** ADAM_STEP)
        d_ref[...] = -ADAM_LR * (m_hat / (jnp.sqrt(v_hat) + ADAM_EPS) + ADAM_WD * w_ref[...])
        nm_ref[...] = nm
        nv_ref[...] = nv

    blk = pl.BlockSpec((tr, C), lambda i: (i, 0))
    out = jax.ShapeDtypeStruct((R, C), F32)
    return pl.pallas_call(
        body, name=name, grid=(R // tr,), in_specs=[blk] * 4, out_specs=[blk] * 3, out_shape=[out] * 3,
        compiler_params=_params("parallel"),
    )(w, g, m, v)


def _sum_slots(x, tr, name):
    n, R, C = x.shape

    def body(x_ref, o_ref):
        acc = x_ref[0]
        for k in range(1, n):
            acc = acc + x_ref[k]
        o_ref[...] = acc

    return pl.pallas_call(
        body, name=name, grid=(R // tr,),
        in_specs=[pl.BlockSpec((n, tr, C), lambda i: (0, i, 0))],
        out_specs=pl.BlockSpec((tr, C), lambda i: (i, 0)),
        out_shape=jax.ShapeDtypeStruct((R, C), F32),
        compiler_params=_params("parallel"),
    )(x)


def _add_half(view, other, c, name):
    if view.ndim == 3:
        _, R2, N = view.shape
        tr = 128
        grid = (R2 // tr,)
        in_specs = [pl.BlockSpec((1, tr, N), lambda i, c_ref: (c_ref[0], i, 0)), pl.BlockSpec((1, tr, N), lambda i, c_ref: (0, i, 0))]
        out_spec = pl.BlockSpec((tr, N), lambda i, c_ref: (i, 0))
        out_shape = jax.ShapeDtypeStruct((R2, N), BF16)

        def body(c_ref, a_ref, b_ref, o_ref):
            o_ref[...] = (a_ref[0] + b_ref[0]).astype(BF16)
    else:
        S, _, R2, C = view.shape
        grid = (S,)
        in_specs = [pl.BlockSpec((1, 1, R2, C), lambda s, c_ref: (s, c_ref[0], 0, 0)), pl.BlockSpec((1, 1, R2, C), lambda s, c_ref: (s, 0, 0, 0))]
        out_spec = pl.BlockSpec((1, R2, C), lambda s, c_ref: (s, 0, 0))
        out_shape = jax.ShapeDtypeStruct((S, R2, C), BF16)

        def body(c_ref, a_ref, b_ref, o_ref):
            o_ref[0] = (a_ref[0, 0] + b_ref[0, 0]).astype(BF16)

    return pl.pallas_call(
        body, name=name,
        grid_spec=pltpu.PrefetchScalarGridSpec(num_scalar_prefetch=1, grid=grid, in_specs=in_specs, out_specs=out_spec),
        out_shape=out_shape, compiler_params=_params("parallel"),
    )(c, view, other)


def _sum_chips(slots, part, kind, place, name):
    _, R2, C = slots.shape
    tr = min(R2, 128) if kind == "col" else R2

    def body(p_ref, s_ref, own_ref, o_ref):
        me = p_ref[0]
        own = own_ref[...] if kind == "col" else own_ref[0]
        acc = None
        for k in range(4):
            term = jnp.where(me == k, own, s_ref[k]).astype(F32)
            acc = term if acc is None else acc + term
        o_ref[0] = acc

    own_spec = (pl.BlockSpec((tr, C), lambda i, p: (i, p[0])) if kind == "col"
                else pl.BlockSpec((1, R2, C), lambda i, p: (p[0], 0, 0)))
    return pl.pallas_call(
        body, name=name,
        grid_spec=pltpu.PrefetchScalarGridSpec(
            num_scalar_prefetch=1, grid=(R2 // tr,),
            in_specs=[pl.BlockSpec((4, tr, C), lambda i, p: (0, i, 0)), own_spec],
            out_specs=pl.BlockSpec((1, tr, C), lambda i, p: (p[1], i, 0))),
        out_shape=jax.ShapeDtypeStruct((2, R2, C), F32), compiler_params=_params("parallel"),
    )(place, slots, part)


def _place():
    x, y, c = lax.axis_index("x"), lax.axis_index("y"), lax.axis_index("c")
    chips = [(1 - x, y), (x, 1 - y), (1 - x, 1 - y)]
    return x, y, c, 2 * x + y, chips


def _remote(src, dst, send_sems, recv_sems, k, to):
    return pltpu.make_async_remote_copy(src_ref=src, dst_ref=dst, send_sem=send_sems.at[k], recv_sem=recv_sems.at[k],
                                        device_id=to, device_id_type=MESH)


def _full_shape(shard_shape, kind):
    *lead, R, C = shard_shape
    return (*lead, R, 4 * C) if kind == "col" else (*lead, 4 * R, C)


def _slab(full_ref, shard_rc, kind, chip, half=None):
    R, C = shard_rc
    lead = (slice(None),) * (len(full_ref.shape) - 2)
    if kind == "col":
        rows = pl.ds(0, R) if half is None else pl.ds(half * (R // 2), R // 2)
        return full_ref.at[(*lead, rows, pl.ds(chip * C, C))]
    rows = pl.ds(chip * R, R) if half is None else pl.ds(chip * R + half * (R // 2), R // 2)
    return full_ref.at[(*lead, rows, slice(None))]


def _row_half(ref, half):
    R = ref.shape[-2]
    lead = (slice(None),) * (len(ref.shape) - 2)
    return ref.at[(*lead, pl.ds(half * (R // 2), R // 2), slice(None))]


def _allgather_weights(shards, kinds, after):
    n = len(shards)
    full_shapes = [_full_shape(s.shape, k) for s, k in zip(shards, kinds)]

    def body(*refs):
        ins, outs, (send_sems, recv_sems) = refs[:n], refs[n + 1:2 * n + 1], refs[2 * n + 1:]
        x, y, c, me, chips = _place()
        slab = lambda a, chip, half=None: _slab(outs[a], shards[a].shape[-2:], kinds[a], chip, half)
        my_half = lambda a: _row_half(ins[a], c)
        own = [_remote(ins[a], slab(a, me), send_sems, recv_sems, 6 * n + a, (x, y, 1 - c)) for a in range(n)]
        first = [_remote(my_half(a), slab(a, me, c), send_sems, recv_sems, j * n + a, (*chip, c))
                 for j, chip in enumerate(chips) for a in range(n)]
        for cp in first + own:
            cp.start()
        passed = []
        for j, chip in enumerate(chips):
            src = 2 * chip[0] + chip[1]
            for a in range(n):
                _remote(my_half(a), slab(a, src, c), send_sems, recv_sems, j * n + a, (x, y, c)).wait_recv()
                cp = _remote(slab(a, src, c), slab(a, src, c), send_sems, recv_sems, (3 + j) * n + a, (x, y, 1 - c))
                cp.start()
                passed.append(cp)
        for j, chip in enumerate(chips):
            src = 2 * chip[0] + chip[1]
            for a in range(n):
                _remote(my_half(a), slab(a, src, 1 - c), send_sems, recv_sems, (3 + j) * n + a, (x, y, c)).wait_recv()
        for cp in own:
            cp.wait_recv()
        for cp in first + passed + own:
            cp.wait_send()

    return pl.pallas_call(
        body, name="allgather_weights",
        in_specs=[ANY] * (n + 1), out_specs=[ANY] * n,
        out_shape=[jax.ShapeDtypeStruct(s, BF16) for s in full_shapes],
        scratch_shapes=[pltpu.SemaphoreType.DMA((7 * n,)), pltpu.SemaphoreType.DMA((7 * n,))],
        compiler_params=pltpu.CompilerParams(has_side_effects=True),
    )(*shards, after)


HBM = pl.BlockSpec(memory_space=pltpu.HBM)
SEM = pl.BlockSpec(memory_space=pltpu.SEMAPHORE)
DATAFLOW = pltpu.SideEffectType.DATAFLOW_SIDE_EFFECTING


def _split_start(name, bufs, plan, ncopies, after=None):
    nb = len(bufs)
    nin = nb + (after is not None)

    def body(*refs):
        send_sems, recv_sems, token = refs[nin], refs[nin + 1], refs[-1]
        for k, (src, dst, to) in enumerate(plan(refs[:nb])):
            _remote(src, dst, send_sems, recv_sems, k, to).start()
        token[...] = jnp.zeros_like(token)

    out = pl.pallas_call(
        body, name=name,
        out_shape=(pltpu.SemaphoreType.DMA((ncopies,)), pltpu.SemaphoreType.DMA((ncopies,)),
                   *[pltpu.HBM(b.shape, b.dtype) for b in bufs], jax.ShapeDtypeStruct((8, LANES), F32)),
        in_specs=[HBM] * nb + [ANY] * (nin - nb), out_specs=(SEM, SEM, *[HBM] * nb, pl.BlockSpec(memory_space=pltpu.VMEM)),
        input_output_aliases={i: 2 + i for i in range(nb)},
        compiler_params=pltpu.CompilerParams(has_side_effects=DATAFLOW),
    )(*[pltpu.with_memory_space_constraint(b, pltpu.HBM) for b in bufs], *([] if after is None else [after]))
    return out[0], out[1], list(out[2:2 + nb]), out[-1]


def _split_wait(name, send_sems, recv_sems, bufs, plan, after):
    nb = len(bufs)

    def body(*refs):
        s_sems, r_sems = refs[nb], refs[nb + 1]
        for k, (src, dst, to) in enumerate(plan(refs[:nb])):
            cp = _remote(src, dst, s_sems, r_sems, k, to)
            cp.wait_send()
            cp.wait_recv()

    return pl.pallas_call(
        body, name=name,
        out_shape=tuple(pltpu.HBM(b.shape, b.dtype) for b in bufs),
        in_specs=[HBM] * nb + [SEM, SEM, ANY], out_specs=tuple([HBM] * nb),
        input_output_aliases={i: i for i in range(nb)},
        compiler_params=pltpu.CompilerParams(has_side_effects=DATAFLOW),
    )(*bufs, send_sems, recv_sems, after)


def _gather_plan(n, shard_rcs, kinds):
    def plan(refs):
        x, y, c, me, chips = _place()
        out = []
        for a in range(n):
            shard, full = refs[a], refs[n + a]
            out.append((shard, _slab(full, shard_rcs[a], kinds[a], me), (x, y, 1 - c)))
            for chip in chips:
                for cc in (c, 1 - c):
                    out.append((_row_half(shard, c), _slab(full, shard_rcs[a], kinds[a], me, c), (*chip, cc)))
        return out
    return plan


def _exchange_plan(n, kinds):
    def plan(refs):
        x, y, c, me, chips = _place()
        out = []
        for a in range(n):
            part, slots = refs[a], refs[n + a]
            C = slots.shape[2]
            for chip in chips:
                dst = 2 * chip[0] + chip[1]
                src = part.at[:, pl.ds(dst * C, C)] if kinds[a] == "col" else part.at[dst]
                out.append((src, slots.at[me], (*chip, c)))
        return out
    return plan


def _swap_plan(ndims):
    n = len(ndims)

    def plan(refs):
        x, y, c, _, _ = _place()
        out = []
        for a in range(n):
            src = refs[a].at[pl.ds(1 - c, 1)] if ndims[a] == 3 else refs[a].at[:, pl.ds(1 - c, 1)]
            out.append((src, refs[n + a], (x, y, 1 - c)))
        return out
    return plan


def _join_halves(bufs):
    n = len(bufs)

    def body(*refs):
        outs, (send_sems, recv_sems) = refs[n:2 * n], refs[2 * n:]
        x, y, c, _, _ = _place()
        sends = [_remote(outs[a].at[c], outs[a].at[c], send_sems, recv_sems, a, (x, y, 1 - c)) for a in range(n)]
        for cp in sends:
            cp.start()
        for a in range(n):
            _remote(outs[a].at[c], outs[a].at[1 - c], send_sems, recv_sems, a, (x, y, c)).wait_recv()
        for cp in sends:
            cp.wait_send()

    return pl.pallas_call(
        body, name="grad_join_halves", in_specs=[ANY] * n, out_specs=[ANY] * n,
        out_shape=[jax.ShapeDtypeStruct(b.shape, F32) for b in bufs],
        input_output_aliases={a: a for a in range(n)},
        scratch_shapes=[pltpu.SemaphoreType.DMA((n,)), pltpu.SemaphoreType.DMA((n,))],
        compiler_params=pltpu.CompilerParams(has_side_effects=True),
    )(*bufs)


def _allgather_small(block):
    M, N = block.shape

    def body(x_ref, out_ref, send_sems, recv_sems, local_sem):
        x, y, c, _, chips = _place()
        me, sibling = (x, y, c), (x, y, 1 - c)

        def rows(px, py, pc):
            return out_ref.at[pl.ds((4 * px + 2 * py + pc) * M, M), :]

        def copy(k, blk, to, src=None):
            return _remote(rows(*blk) if src is None else src, rows(*blk), send_sems, recv_sems, k, to)

        mine = pltpu.make_async_copy(x_ref, rows(*me), local_sem)
        mine.start()
        first = [copy(0, me, sibling, src=x_ref)] + [copy(1 + j, me, (*chip, c), src=x_ref) for j, chip in enumerate(chips)]
        for cp in first:
            cp.start()
        passed = [copy(4 + j, (*chip, c), sibling) for j, chip in enumerate(chips)]
        for j, chip in enumerate(chips):
            copy(1 + j, (*chip, c), me).wait_recv()
            passed[j].start()
        copy(0, sibling, me).wait_recv()
        for j, chip in enumerate(chips):
            copy(4 + j, (*chip, 1 - c), me).wait_recv()
        for cp in first + passed:
            cp.wait_send()
        mine.wait()

    vm = pl.BlockSpec(memory_space=pltpu.VMEM)
    return pl.pallas_call(
        body, name="allgather_small", in_specs=[vm], out_specs=vm,
        out_shape=jax.ShapeDtypeStruct((8 * M, N), F32),
        scratch_shapes=[pltpu.SemaphoreType.DMA((7,)), pltpu.SemaphoreType.DMA((7,)), pltpu.SemaphoreType.DMA],
        compiler_params=pltpu.CompilerParams(has_side_effects=True, vmem_limit_bytes=VMEM_LIMIT),
    )(block)


def _pack(arrays):
    flat = jnp.concatenate([a.reshape(-1) for a in arrays])
    pad = (-flat.shape[0]) % (8 * LANES)
    return jnp.pad(flat, (0, pad)).reshape(-1, LANES)


def _unpack(packed, shapes):
    flat, out, off = packed.reshape(-1), [], 0
    for s in shapes:
        size = 1
        for d in s:
            size *= d
        out.append(flat[off:off + size].reshape(s))
        off += size
    return out


def _block_diag(pw):
    G, n, _ = pw.shape
    eye = jnp.eye(G, dtype=pw.dtype)
    return (eye[:, None, :, None] * pw[:, :, None, :]).reshape(G * n, G * n)


def _diag_blocks(m, G):
    n = m.shape[0] // G
    return jnp.stack([m[g * n:(g + 1) * n, g * n:(g + 1) * n] for g in range(G)])


def _pad_rows(a, rows):
    return jnp.pad(a, ((0, rows - a.shape[0]), (0, 0)))


def kernel(x, w_in, w_out, conv_w, pool_w, pool_scale, rel_bias, group_gain, pre_mix_g, post_mix_g, pre_ffn_g, post_ffn_g, w_gate_up, w_down, loss_target, m_w_in, m_w_out, m_conv_w, m_pool_w, m_pool_scale, m_rel_bias, m_group_gain, m_pre_mix_g, m_post_mix_g, m_pre_ffn_g, m_post_ffn_g, m_w_gate_up, m_w_down, v_w_in, v_w_out, v_conv_w, v_pool_w, v_pool_scale, v_rel_bias, v_group_gain, v_pre_mix_g, v_post_mix_g, v_pre_ffn_g, v_post_ffn_g, v_w_gate_up, v_w_down):
    L = w_in.shape[0]
    T, D = x.shape[1], x.shape[2]
    DC = D // 4
    NH = rel_bias.shape[1]
    NREL = rel_bias.shape[2]
    G = pool_w.shape[1]
    cs = conv_w.shape[2]
    assert TB == LEFT_CHUNKS * CHUNK and T % TB == 0 and D % (4 * LANES) == 0 and NH * HEAD_DIM == D // 2
    xi, yi, ci = lax.axis_index("x"), lax.axis_index("y"), lax.axis_index("c")
    chip = 2 * xi + yi

    kinds = ("col", "row", "col", "row")
    big = (w_in, w_out, w_gate_up, w_down)
    rcs = [w.shape[-2:] for w in big]
    conv_gathered = _allgather_small(_pack([conv_w]))
    conv_all = conv_gathered.reshape(8, -1)[:, :L * 3 * cs].reshape(4, 2, L, 3, cs)[:, 0]
    conv_full = jnp.moveaxis(conv_all, 0, 2).reshape(L, 3, 4 * cs)
    landing = lambda which: [lax.empty(_full_shape(rcs[k], kinds[k]), BF16) for k in which]
    wi0 = _allgather_weights([w_in[0:1].astype(BF16)], kinds[:1], conv_gathered)[0][0]
    rest_plan = _gather_plan(3, rcs[1:], kinds[1:])
    rest = _split_start("gather_start_0", [w[0].astype(BF16) for w in big[1:]] + landing((1, 2, 3)), rest_plan, 3 * 7, after=wi0)
    layer_plan = _gather_plan(4, rcs, kinds)
    full = [None] * L

    h = x[0]
    saved = []
    token = rest[3]
    for l in range(L):
        g_pre, g_pm, g_pf, g_po = (a[l][None] for a in (pre_mix_g, post_mix_g, pre_ffn_g, post_ffn_g))
        if 0 < l < L - 1:
            nxt = _split_start(f"gather_start_{l + 1}", [w[l + 1].astype(BF16) for w in big] + landing(range(4)), layer_plan, 4 * 7,
                               after=full[l][0])
            token = nxt[3]
        if l < L - 1:
            g_pre = g_pre + token[0:1, 0:1]
        gg, ps = group_gain[l][None], pool_scale[l][None]
        cw = _pad_rows(conv_full[l], 8)
        wbd = _block_diag(pool_w[l]).astype(BF16)
        bias = _bias_build(jnp.pad(rel_bias[l], ((0, 0), (0, RBP - NREL))))
        wi = wi0 if l == 0 else full[l][0]
        xn, pa, qkv, kv_t, yab = _inproj_mixers_fwd(h, g_pre, wi, cw, wbd, ps)
        yc, lse = _attn_fwd(qkv, kv_t, bias)
        if l == 0:
            full[0] = [wi0, *_split_wait("gather_wait_0", rest[0], rest[1], rest[2], rest_plan, yc)[3:6]]
            if L > 1:
                nxt = _split_start("gather_start_1", [w[1].astype(BF16) for w in big] + landing(range(4)), layer_plan, 4 * 7,
                                   after=full[0][1])
                g_pm = g_pm + nxt[3][0:1, 0:1]
        wi, wo, wgu, wdn = full[l]
        y, mix, h1, hn, gu, ff, ffo, h2 = _layer_tail_fwd(yab, yc, gg, wo, h, g_pm, g_pf, wgu, wdn, g_po)
        saved.append(dict(h=h, xn=xn, pa=pa, qkv=qkv, kv_t=kv_t, lse=lse, yab=yab, yc=yc, y=y, mix=mix, h1=h1, hn=hn, gu=gu, ff=ff, ffo=ffo,
                          cw=cw, wbd=wbd, bias=bias, ps=ps, gg=gg, g_pre=g_pre, g_pm=g_pm, g_pf=g_pf, g_po=g_po))
        h = h2
        if l + 1 < L:
            full[l + 1] = list(_split_wait(f"gather_wait_{l + 1}", nxt[0], nxt[1], nxt[2], layer_plan, h2)[4:8])

    dh, loss_tile = _loss_grad(h, loss_target[0])
    loss = lax.psum(loss_tile[0, 0], ("x", "y", "c"))

    xplan = _exchange_plan(4, kinds)
    splan = _swap_plan([3, 4, 3, 4])
    place = jnp.stack([chip, ci]).astype(jnp.int32)
    cvec = ci.reshape(1).astype(jnp.int32)
    small_grads = [None] * L
    shard_grads = [None] * L
    dbiases = [None] * L

    def start_exchange(lp, swap):
        thru = _split_wait(f"grad_swap_wait_{lp}", swap[0], swap[1], swap[2], splan, swap[4])
        chip_sums = [_add_half(v, t, cvec, "grad_add_half") for v, t in zip(thru[:4], thru[4:])]
        slots = [lax.empty((4, p.shape[0], p.shape[1] // 4) if k == "col" else p.shape, BF16) for p, k in zip(chip_sums, kinds)]
        ssem, rsem, bufs, tok = _split_start(f"grad_exchange_start_{lp}", chip_sums + slots, xplan, 4 * 3)
        return (lp, ssem, rsem, bufs), tok

    def finish_exchange(pending, after):
        lp, ssem, rsem, thru = pending
        landed = _split_wait(f"grad_exchange_wait_{lp}", ssem, rsem, thru, xplan, after)
        bufs = [_sum_chips(landed[4 + a], landed[a], kinds[a], place, "grad_sum_chips") for a in range(4)]
        shard_grads[lp] = [j.reshape(2 * j.shape[1], j.shape[2]) for j in _join_halves(bufs)]

    swap = None
    exchange = None
    for l in reversed(range(L)):
        s = saved[l]
        wi, wo, wgu, wdn = full[l]
        g_po = s["g_po"] if swap is None else s["g_po"] + swap[3][0:1, 0:1]
        dffo, dgu, dh1, dmix, dyab, dyc, dg_po, dg_pf, dg_pm, dgg = _layer_tail_bwd(
            dh, s["ffo"], g_po, s["gu"], wdn, wgu, s["h1"], s["g_pf"], s["mix"], s["g_pm"], wo, s["yab"], s["yc"], s["gg"])
        after = dyc
        if swap is not None:
            started, after = start_exchange(l + 1, (*swap[:4], dh1))
            if exchange is not None:
                finish_exchange(exchange, after)
            exchange = started
        dq, dk, dv, dbiases[l] = _attn_bwd(s["qkv"], s["kv_t"], dyc, s["yc"], s["lse"], s["bias"], after)
        dpa, dcw, dwbd, dps = _convpool_bwd(s["pa"], dyab, s["cw"], s["wbd"], s["ps"])
        dparts = [dpa, dq, dk, dv]
        dh, dg_pre = _inproj_bwd(dparts, wi, s["h"], s["g_pre"], dh1)
        F2, DFF = s["gu"].shape[1], s["ff"].shape[1]
        grads = [_wgrad_concat(s["xn"], dparts, "wgrad_in"),
                 _wgrad(s["y"], dmix, D, D, "wgrad_out"),
                 _wgrad(s["hn"], dgu, D, F2 // 4, "wgrad_gate_up"),
                 _wgrad(s["ff"], dffo, DFF // 2, D, "wgrad_down")]
        small_grads[l] = [dcw[:3], _diag_blocks(dwbd, G), dps[0], None, dgg[0], dg_pre[0], dg_pm[0], dg_pf[0], dg_po[0]]
        views = [g.reshape(2, g.shape[0] // 2, g.shape[1]) if k == "col" else g.reshape(4, 2, g.shape[0] // 8, g.shape[1])
                 for g, k in zip(grads, kinds)]
        lands = [lax.empty((1,) + v.shape[1:] if v.ndim == 3 else (v.shape[0], 1) + v.shape[2:], F32) for v in views]
        swap = _split_start(f"grad_swap_start_{l}", views + lands, splan, 4)
    last, tok = start_exchange(0, (*swap[:4], swap[3]))
    if exchange is not None:
        finish_exchange(exchange, tok)
    for l in range(L):
        small_grads[l][3] = _bias_fold(dbiases[l], tok)[:NH, :NREL]

    names_shapes = [(L, 3, 4 * cs), pool_w.shape, pool_scale.shape, rel_bias.shape, group_gain.shape,
                    pre_mix_g.shape, post_mix_g.shape, pre_ffn_g.shape, post_ffn_g.shape]
    small_stacked = [jnp.stack([small_grads[l][k] for l in range(L)]) for k in range(len(names_shapes))]
    packed = _pack(small_stacked)
    M = packed.shape[0]
    total = _sum_slots(_allgather_small(packed).reshape(8, M, LANES), M, "small_sum_devices")
    g_small = _unpack(total, names_shapes)
    g_small[0] = lax.dynamic_slice_in_dim(g_small[0], chip * cs, cs, axis=2)
    finish_exchange(last, total)
    g_big = [jnp.stack([shard_grads[l][k] for l in range(L)]) for k in range(4)]

    def adam_big(w, g, m, v, name):
        shp = w.shape
        two = lambda a: a.reshape(shp[0] * shp[1], shp[2])
        return [o.reshape(shp) for o in _adamw(two(w), two(g), two(m), two(v), 256, name)]

    upd_in = adam_big(w_in, g_big[0], m_w_in, v_w_in, "adamw_in")
    upd_out = adam_big(w_out, g_big[1], m_w_out, v_w_out, "adamw_out")
    upd_gu = adam_big(w_gate_up, g_big[2], m_w_gate_up, v_w_gate_up, "adamw_gate_up")
    upd_dn = adam_big(w_down, g_big[3], m_w_down, v_w_down, "adamw_down")

    small_w = [conv_w, pool_w, pool_scale, rel_bias, group_gain, pre_mix_g, post_mix_g, pre_ffn_g, post_ffn_g]
    small_m = [m_conv_w, m_pool_w, m_pool_scale, m_rel_bias, m_group_gain, m_pre_mix_g, m_post_mix_g, m_pre_ffn_g, m_post_ffn_g]
    small_v = [v_conv_w, v_pool_w, v_pool_scale, v_rel_bias, v_group_gain, v_pre_mix_g, v_post_mix_g, v_pre_ffn_g, v_post_ffn_g]
    pw_, pg_, pm_, pv_ = _pack(small_w), _pack(g_small), _pack(small_m), _pack(small_v)
    shapes = [w.shape for w in small_w]
    upd_small = [_unpack(o, shapes) for o in _adamw(pw_, pg_, pm_, pv_, pw_.shape[0], "adamw_small")]

    def ordered(big4, small9):
        return [big4[0], big4[1], *small9, big4[2], big4[3]]

    grads = ordered(g_big, g_small)
    outs = [ordered([upd_in[k], upd_out[k], upd_gu[k], upd_dn[k]], upd_small[k]) for k in range(3)]
    return (loss, dh[None], *grads, *outs[0], *outs[1], *outs[2])
```

```python
import functools

import jax
import jax.numpy as jnp
from jax import lax
from jax.experimental import pallas as pl
from jax.experimental.pallas import tpu as pltpu

F32, BF16 = jnp.float32, jnp.bfloat16
EPS = 1e-6
CHUNK = 64
LEFT_CHUNKS = 8
REL_CLIP = 128
HEAD_DIM = 64
LANES = 128
POOL_WINDOWS = (2, 4, 8, 16)
HALO = 16
TB = LEFT_CHUNKS * CHUNK
TBF = 256
TBW = 1024
BAND = (LEFT_CHUNKS + 1) * CHUNK
SUB = 2 * CHUNK
BANDW = SUB + LEFT_CHUNKS * CHUNK
SKEW = 768
NEG = -1e30
RBP = 384
VMEM_LIMIT = 56 * 1024 * 1024
ADAM_LR, ADAM_B1, ADAM_B2, ADAM_EPS, ADAM_WD, ADAM_STEP = 0.001, 0.9, 0.999, 1e-08, 0.01, 10
MESH = pl.DeviceIdType.MESH
ANY = pl.BlockSpec(memory_space=pl.ANY)


def _params(*sem):
    kw = dict(vmem_limit_bytes=VMEM_LIMIT)
    if sem:
        kw["dimension_semantics"] = sem
    return pltpu.CompilerParams(**kw)


def _dot(a, b):
    return jnp.dot(a, b, preferred_element_type=F32)


def _dot_nt(a, b):
    return lax.dot_general(a, b, (((1,), (1,)), ((), ())), preferred_element_type=F32)


def _dot_tn(a, b):
    return lax.dot_general(a, b, (((0,), (0,)), ((), ())), preferred_element_type=F32)


def _rms(x, g):
    r = lax.rsqrt(jnp.mean(x * x, axis=-1, keepdims=True) + EPS)
    return x * r * g


def _rms_bwd(dy, x, g):
    r = lax.rsqrt(jnp.mean(x * x, axis=-1, keepdims=True) + EPS)
    xh = x * r
    dxh = dy * g
    dx = r * (dxh - xh * jnp.mean(dxh * xh, axis=-1, keepdims=True))
    return dx, jnp.sum(dy * xh, axis=0, keepdims=True)


def _full(shape):
    return pl.BlockSpec(shape, lambda *_: (0,) * len(shape))


def _acc_init(step, *refs):
    @pl.when(step == 0)
    def _():
        for r in refs:
            r[...] = jnp.zeros_like(r)


def _inproj_mixers_fwd(h, g, w, cw, wbd, ps):
    T, D = h.shape
    NQ = w.shape[1] - D
    NKV = 2 * NQ // 3
    DC = D // 4

    def body(h_ref, g_ref, w_ref, cw_ref, wbd_ref, ps_ref, xn_ref, pa_ref, qkv_ref, kvt_ref, yab_ref, halo):
        i = pl.program_id(0)
        _acc_init(i, halo)
        xn = _rms(h_ref[...], g_ref[...]).astype(BF16)
        xn_ref[...] = xn
        x = _dot(xn, w_ref[:, :D])
        pa_ref[...] = x
        qkv = _dot(xn, w_ref[:, D:])
        qkv_ref[...] = qkv.astype(BF16)
        kvt_ref[...] = qkv[:, NQ - NKV:].T.astype(BF16)
        hl = halo[...]
        gb, gc, u, pu = (x[:, k * DC:(k + 1) * DC] for k in range(4))
        z = gc * u
        z1, z2 = _conv_taps(z, hl[:, DC:2 * DC] * hl[:, 2 * DC:3 * DC])
        cwv = cw_ref[...]
        ya = gb * (cwv[2:3] * z + cwv[1:2] * z1 + cwv[0:1] * z2)
        d = _pool_d(pu, hl[:, 3 * DC:], _pool_count(i, DC))
        yb = _dot(d.astype(BF16), wbd_ref[...]) * ps_ref[...]
        yab_ref[...] = jnp.concatenate([ya, yb], axis=1)
        halo[...] = x[TB - HALO:, :]

    row = lambda n: pl.BlockSpec((TB, n), lambda i: (i, 0))
    return pl.pallas_call(
        body, name="inproj_mixers_fwd", grid=(T // TB,),
        in_specs=[row(D), _full((1, D)), _full(w.shape), _full((8, DC)), _full((DC, DC)), _full((1, DC))],
        out_specs=[row(D), row(D), row(NQ), pl.BlockSpec((NKV, TB), lambda i: (0, i)), row(2 * DC)],
        out_shape=[jax.ShapeDtypeStruct((T, D), BF16), jax.ShapeDtypeStruct((T, D), F32), jax.ShapeDtypeStruct((T, NQ), BF16),
                   jax.ShapeDtypeStruct((NKV, T), BF16), jax.ShapeDtypeStruct((T, 2 * DC), F32)],
        scratch_shapes=[pltpu.VMEM((HALO, D), F32)],
        compiler_params=_params("arbitrary"),
    )(h, g, w, cw, wbd, ps)


def _lane_groups(n, vals):
    lane = lax.broadcasted_iota(jnp.int32, (1, n), 1)
    q = n // 4
    return jnp.where(lane < q, vals[0], jnp.where(lane < 2 * q, vals[1], jnp.where(lane < 3 * q, vals[2], vals[3]))).astype(F32)


def _pick_group(levels, n):
    lane = lax.broadcasted_iota(jnp.int32, (1, n), 1)
    q = n // 4
    return jnp.where(lane < q, levels[0], jnp.where(lane < 2 * q, levels[1], jnp.where(lane < 3 * q, levels[2], levels[3])))


def _pool_count(blk, n):
    t1 = (blk * TB + 1 + lax.broadcasted_iota(jnp.int32, (TB, 1), 0)).astype(F32)
    return jnp.minimum(t1, _lane_groups(n, POOL_WINDOWS))


def _pool_d(pu, pu_halo, cnt):
    e = jnp.concatenate([pu_halo, pu], axis=0)
    s2 = e + pltpu.roll(e, 1, 0)
    s4 = s2 + pltpu.roll(s2, 2, 0)
    s8 = s4 + pltpu.roll(s4, 4, 0)
    s16 = s8 + pltpu.roll(s8, 8, 0)
    num = _pick_group([s2, s4, s8, s16], pu.shape[1])[HALO:]
    return num / cnt - pu


def _conv_taps(z, z_halo):
    e = jnp.concatenate([z_halo, z], axis=0)
    return pltpu.roll(e, 1, 0)[HALO:], pltpu.roll(e, 2, 0)[HALO:]


def _bias_bins(shape, col_dim):
    j = lax.broadcasted_iota(jnp.int32, shape, col_dim)
    b = lax.broadcasted_iota(jnp.int32, shape, 1 - col_dim)
    d = jnp.where(j < BAND, j, j - SKEW)
    live = jnp.logical_or(j < BAND, j > SKEW - CHUNK)
    bins = jnp.minimum(TB - d, REL_CLIP) + REL_CLIP
    return jnp.where(jnp.logical_and(live, bins == b), 1.0, 0.0).astype(F32)


def _skew_rows(x, left):
    row = lax.broadcasted_iota(jnp.int32, x.shape, 0)
    for b in range(SUB.bit_length() - 1):
        x = jnp.where(((row >> b) & 1) == 1, pltpu.roll(x, SKEW - (1 << b) if left else 1 << b, 1), x)
    return x


def _bias_build(rb):
    H = rb.shape[0]

    def body(rb_ref, o_ref):
        v = jnp.dot(rb_ref[...], _bias_bins((RBP, SKEW), 1), precision=lax.Precision.HIGHEST, preferred_element_type=F32)
        qc = lax.broadcasted_iota(jnp.int32, (SUB, BANDW), 0) >> (CHUNK.bit_length() - 1)
        kc = lax.broadcasted_iota(jnp.int32, (SUB, BANDW), 1) >> (CHUNK.bit_length() - 1)
        in_band = jnp.logical_and(kc >= qc, kc <= qc + LEFT_CHUNKS)
        for h in range(H):
            x = _skew_rows(jnp.broadcast_to(v[h:h + 1], (SUB, SKEW)), left=False)
            o_ref[h // 2, :, (h % 2) * SUB:(h % 2 + 1) * SUB] = jnp.where(in_band, x[:, :BANDW], NEG).T

    return pl.pallas_call(
        body, name="bias_build", in_specs=[_full(rb.shape)], out_specs=_full((H // 2, BANDW, 2 * SUB)),
        out_shape=jax.ShapeDtypeStruct((H // 2, BANDW, 2 * SUB), F32), grid=(1,),
        compiler_params=_params("arbitrary"),
    )(rb)


def _bias_fold(db, after):
    H = 2 * db.shape[0]

    def body(db_ref, after_ref, o_ref, sums):
        sums[...] = jnp.zeros_like(sums)
        for h in range(H):
            x = db_ref[h // 2, :, (h % 2) * SUB:(h % 2 + 1) * SUB].T
            x = _skew_rows(jnp.concatenate([x, jnp.zeros((SUB, SKEW - BANDW), F32)], axis=1), left=True)
            sums[h:h + 1, :] = jnp.sum(x, axis=0, keepdims=True)
        o_ref[...] = jnp.dot(sums[...], _bias_bins((SKEW, RBP), 0), precision=lax.Precision.HIGHEST, preferred_element_type=F32)

    return pl.pallas_call(
        body, name="bias_fold", grid=(1,), in_specs=[_full(db.shape), ANY], out_specs=_full((8, RBP)),
        out_shape=jax.ShapeDtypeStruct((8, RBP), F32), scratch_shapes=[pltpu.VMEM((8, SKEW), F32)],
        compiler_params=_params("arbitrary"),
    )(db, after)


def _both_heads(x):
    first = lax.broadcasted_iota(jnp.int32, (1, LANES), 1) < HEAD_DIM
    return jnp.concatenate([jnp.where(first, x, 0), jnp.where(first, 0, x)], axis=0)


def _own_head_rows(x2):
    n = x2.shape[1] // 2
    first = lax.broadcasted_iota(jnp.int32, (LANES, 1), 0) < HEAD_DIM
    return jnp.where(first, x2[:, :n], x2[:, n:])


def _key_tiles(s, first_block):
    return [t for t in range(s, s + BANDW // SUB) if not (first_block and t < TB // SUB)]


def _attn_fwd(qkv, kv_t, bias_t):
    T = qkv.shape[0]
    NP = qkv.shape[1] // (3 * LANES)
    NB = T // TB
    NS = TB // SUB
    scale = HEAD_DIM ** -0.5

    def body(q_ref, kc_ref, kp_ref, vtc_ref, vtp_ref, b_ref, o_ref, lse_ref):
        i = pl.program_id(1)

        def compute(first_block):
            q = q_ref[...] * scale
            kwin = jnp.concatenate([kp_ref[...], kc_ref[...]], axis=0)
            vt = jnp.concatenate([vtp_ref[...], vtc_ref[...]], axis=1)
            for s in range(NS):
                rows = slice(s * SUB, (s + 1) * SUB)
                tiles = _key_tiles(s, first_block)
                keys = slice(tiles[0] * SUB, (tiles[-1] + 1) * SUB)
                brows = slice((tiles[0] - s) * SUB, (tiles[-1] - s + 1) * SUB)
                q2 = _both_heads(q[rows])
                halves = []
                for a in range(2):
                    st = _dot_nt(kwin[keys], q2[a * SUB:(a + 1) * SUB]) + b_ref[0, brows, a * SUB:(a + 1) * SUB]
                    m = jnp.max(st, axis=0, keepdims=True)
                    p = jnp.exp(st - m)
                    l = jnp.sum(p, axis=0, keepdims=True)
                    halves.append(_dot(vt[:, keys], p.astype(BF16)) * (1.0 / l))
                    lse_ref[0, 0, a * NS + s:a * NS + s + 1, :] = m + jnp.log(l)
                o_ref[rows, :] = _own_head_rows(jnp.concatenate(halves, axis=1)).T

        pl.when(i == 0)(functools.partial(compute, True))
        pl.when(i > 0)(functools.partial(compute, False))

    prev = lambda i: jnp.maximum(i - 1, 0)
    return pl.pallas_call(
        body, name="attn_fwd", grid=(NP, NB),
        in_specs=[pl.BlockSpec((TB, LANES), lambda p, i: (i, p)),
                  pl.BlockSpec((TB, LANES), lambda p, i: (i, NP + p)),
                  pl.BlockSpec((TB, LANES), lambda p, i: (prev(i), NP + p)),
                  pl.BlockSpec((LANES, TB), lambda p, i: (NP + p, i)),
                  pl.BlockSpec((LANES, TB), lambda p, i: (NP + p, prev(i))),
                  pl.BlockSpec((1, BANDW, 2 * SUB), lambda p, i: (p, 0, 0))],
        out_specs=[pl.BlockSpec((TB, LANES), lambda p, i: (i, p)),
                   pl.BlockSpec((1, 1, 8, LANES), lambda p, i: (p, i, 0, 0))],
        out_shape=[jax.ShapeDtypeStruct((T, NP * LANES), F32), jax.ShapeDtypeStruct((NP, NB, 8, LANES), F32)],
        compiler_params=_params("parallel", "parallel"),
    )(qkv, qkv, qkv, kv_t, kv_t, bias_t)


def _group_bounds(D):
    return ((0, D // 4), (D // 4, D // 2), (D // 2, D))


def _load_resident(step, *pairs_and_sems):
    @pl.when(step == 0)
    def _():
        cps = [pltpu.make_async_copy(src, dst, sem) for src, dst, sem in pairs_and_sems]
        for cp in cps:
            cp.start()
        for cp in cps:
            cp.wait()


def _layer_tail_fwd(yab, yc, gg, w_out, h, g_pm, g_pf, w_gu, w_dn, g_po):
    T, D = h.shape
    F2 = w_gu.shape[1]
    DFF = F2 // 2
    TF = DFF // 2

    def body(yab_ref, yc_ref, gg_ref, wo_hbm, h_ref, gpm_ref, gpf_ref, wgu_hbm, wdn_hbm, gpo_ref,
             y_ref, mix_ref, h1_ref, hn_ref, gu_ref, ff_ref, ffo_ref, h2_ref, wo_v, wgu_v, wdn_v, sems):
        _load_resident(pl.program_id(0), (wo_hbm, wo_v, sems.at[0]), (wgu_hbm, wgu_v, sems.at[1]), (wdn_hbm, wdn_v, sems.at[2]))
        yraw = jnp.concatenate([yab_ref[...], yc_ref[...]], axis=1)
        ggv = gg_ref[...]
        y = jnp.concatenate([_rms(yraw[:, a:b], ggv[:, a:b]) for a, b in _group_bounds(D)], axis=1).astype(BF16)
        y_ref[...] = y
        mix = _dot(y, wo_v[...])
        mix_ref[...] = mix.astype(BF16)
        h1 = h_ref[...] + _rms(mix, gpm_ref[...])
        h1_ref[...] = h1
        hn = _rms(h1, gpf_ref[...]).astype(BF16)
        hn_ref[...] = hn
        ffo = jnp.zeros((TBF, D), F32)
        for j in range(2):
            sg, su = slice(j * TF, (j + 1) * TF), slice(DFF + j * TF, DFF + (j + 1) * TF)
            gate = _dot(hn, wgu_v[:, sg])
            up = _dot(hn, wgu_v[:, su])
            gu_ref[:, sg] = gate.astype(BF16)
            gu_ref[:, su] = up.astype(BF16)
            ff = (gate * jax.nn.sigmoid(gate) * up).astype(BF16)
            ff_ref[:, sg] = ff
            ffo = ffo + _dot(ff, wdn_v[sg, :])
        ffo_ref[...] = ffo.astype(BF16)
        h2_ref[...] = h1 + _rms(ffo, gpo_ref[...])

    row = lambda n: pl.BlockSpec((TBF, n), lambda i: (i, 0))
    gain = _full((1, D))
    f32, bf16 = (lambda n: jax.ShapeDtypeStruct((T, n), F32)), (lambda n: jax.ShapeDtypeStruct((T, n), BF16))
    return pl.pallas_call(
        body, name="layer_tail_fwd", grid=(T // TBF,),
        in_specs=[row(D // 2), row(D // 2), gain, ANY, row(D), gain, gain, ANY, ANY, gain],
        out_specs=[row(D), row(D), row(D), row(D), row(F2), row(DFF), row(D), row(D)],
        out_shape=[bf16(D), bf16(D), f32(D), bf16(D), bf16(F2), bf16(DFF), bf16(D), f32(D)],
        scratch_shapes=[pltpu.VMEM(w_out.shape, BF16), pltpu.VMEM(w_gu.shape, BF16), pltpu.VMEM(w_dn.shape, BF16),
                        pltpu.SemaphoreType.DMA((3,))],
        compiler_params=_params("arbitrary"),
    )(yab, yc, gg, w_out, h, g_pm, g_pf, w_gu, w_dn, g_po)


def _loss_grad(h, tgt):
    T, D = h.shape

    def body(h_ref, t_ref, dh_ref, loss_ref):
        _acc_init(pl.program_id(0), loss_ref)
        diff = h_ref[...] - t_ref[...]
        dh_ref[...] = diff * (1.0 / D)
        loss_ref[...] += 0.5 * jnp.sum(jnp.mean(diff * diff, axis=-1, keepdims=True))

    row = pl.BlockSpec((TB, D), lambda i: (i, 0))
    return pl.pallas_call(
        body, name="loss_grad", grid=(T // TB,),
        in_specs=[row, row], out_specs=[row, _full((8, LANES))],
        out_shape=[jax.ShapeDtypeStruct((T, D), F32), jax.ShapeDtypeStruct((8, LANES), F32)],
        compiler_params=_params("arbitrary"),
    )(h, tgt)


def _layer_tail_bwd(dh2, ffo, g_po, gu, w_dn, w_gu, h1, g_pf, mix, g_pm, w_out, yab, yc, gg):
    T, D = dh2.shape
    F2 = gu.shape[1]
    DFF = F2 // 2
    TF = DFF // 2

    def body(dh_ref, ffo_ref, gpo_ref, gu_ref, wdn_hbm, wgu_hbm, h_ref, gpf_ref, mix_ref, gpm_ref, wo_hbm, yab_ref, yc_ref, gg_ref,
             dffo_ref, dgu_ref, dh1_ref, dmix_ref, dyab_ref, dyc_ref, dgpo_ref, dgpf_ref, dgpm_ref, dgg_ref,
             wdn_v, wgu_v, wo_v, sems):
        step = pl.program_id(0)
        _load_resident(step, (wdn_hbm, wdn_v, sems.at[0]), (wgu_hbm, wgu_v, sems.at[1]), (wo_hbm, wo_v, sems.at[2]))
        _acc_init(step, dgpo_ref, dgpf_ref, dgpm_ref, dgg_ref)
        dh = dh_ref[...]
        dffo, dg = _rms_bwd(dh, ffo_ref[...].astype(F32), gpo_ref[...])
        dgpo_ref[0:1, :] += dg
        dffo = dffo.astype(BF16)
        dffo_ref[...] = dffo
        dhn = jnp.zeros((TBF, D), F32)
        for j in range(2):
            sg, su = slice(j * TF, (j + 1) * TF), slice(DFF + j * TF, DFF + (j + 1) * TF)
            dff = _dot_nt(dffo, wdn_v[sg, :])
            gate, up = gu_ref[:, sg].astype(F32), gu_ref[:, su].astype(F32)
            sig = jax.nn.sigmoid(gate)
            dgate = (dff * up * (sig * (1.0 + gate * (1.0 - sig)))).astype(BF16)
            dup = (dff * (gate * sig)).astype(BF16)
            dgu_ref[:, sg] = dgate
            dgu_ref[:, su] = dup
            dhn = dhn + _dot_nt(dgate, wgu_v[:, sg]) + _dot_nt(dup, wgu_v[:, su])
        dx, dg = _rms_bwd(dhn, h_ref[...], gpf_ref[...])
        dgpf_ref[0:1, :] += dg
        dh1 = dh + dx
        dh1_ref[...] = dh1
        dmix, dg = _rms_bwd(dh1, mix_ref[...].astype(F32), gpm_ref[...])
        dgpm_ref[0:1, :] += dg
        dmix = dmix.astype(BF16)
        dmix_ref[...] = dmix
        dy = _dot_nt(dmix, wo_v[...])
        yraw = jnp.concatenate([yab_ref[...], yc_ref[...]], axis=1)
        ggv = gg_ref[...]
        parts = [_rms_bwd(dy[:, a:b], yraw[:, a:b], ggv[:, a:b]) for a, b in _group_bounds(D)]
        dgg_ref[0:1, :] += jnp.concatenate([p[1] for p in parts], axis=1)
        dyab_ref[...] = jnp.concatenate([parts[0][0], parts[1][0]], axis=1)
        dyc_ref[...] = parts[2][0]

    row = lambda n: pl.BlockSpec((TBF, n), lambda i: (i, 0))
    gain, acc = _full((1, D)), _full((8, D))
    f32, bf16 = (lambda n: jax.ShapeDtypeStruct((T, n), F32)), (lambda n: jax.ShapeDtypeStruct((T, n), BF16))
    acc_shape = jax.ShapeDtypeStruct((8, D), F32)
    return pl.pallas_call(
        body, name="layer_tail_bwd", grid=(T // TBF,),
        in_specs=[row(D), row(D), gain, row(F2), ANY, ANY, row(D), gain, row(D), gain, ANY, row(D // 2), row(D // 2), gain],
        out_specs=[row(D), row(F2), row(D), row(D), row(D // 2), row(D // 2), acc, acc, acc, acc],
        out_shape=[bf16(D), bf16(F2), f32(D), bf16(D), f32(D // 2), f32(D // 2), acc_shape, acc_shape, acc_shape, acc_shape],
        scratch_shapes=[pltpu.VMEM(w_dn.shape, BF16), pltpu.VMEM(w_gu.shape, BF16), pltpu.VMEM(w_out.shape, BF16),
                        pltpu.SemaphoreType.DMA((3,))],
        compiler_params=_params("arbitrary"),
    )(dh2, ffo, g_po, gu, w_dn, w_gu, h1, g_pf, mix, g_pm, w_out, yab, yc, gg)


def _attn_bwd(qkv, kv_t, dyc, yc, lse, bias_t, after):
    T = qkv.shape[0]
    NP = qkv.shape[1] // (3 * LANES)
    NB = T // TB
    NS = TB // SUB
    scale = HEAD_DIM ** -0.5

    def body(q_ref, kc_ref, kp_ref, vc_ref, vp_ref, ktc_ref, ktp_ref, do_ref, o_ref, lse_ref, b_ref, after_ref,
             dq_ref, dk_ref, dv_ref, db_ref, dk_carry, dv_carry, dkw, dvw):
        step = pl.program_id(1)
        i = NB - 1 - step
        _acc_init(step, dk_carry, dv_carry, db_ref)

        def compute(first_block):
            q = q_ref[...] * scale
            kwin = jnp.concatenate([kp_ref[...], kc_ref[...]], axis=0)
            vwin = jnp.concatenate([vp_ref[...], vc_ref[...]], axis=0)
            kt = jnp.concatenate([ktp_ref[...], ktc_ref[...]], axis=1)
            do = do_ref[...]
            dob = do.astype(BF16)
            prod = do * o_ref[...]
            first = lax.broadcasted_iota(jnp.int32, (1, LANES), 1) < HEAD_DIM
            ones = jnp.ones((8, LANES), F32)
            row_sums = lambda x: lax.dot_general(ones, x, (((1,), (1,)), ((), ())), precision=lax.Precision.HIGHEST,
                                                 preferred_element_type=F32)
            deltas = (row_sums(jnp.where(first, prod, 0.0)), row_sums(jnp.where(first, 0.0, prod)))
            written = set()
            for s in range(NS):
                rows = slice(s * SUB, (s + 1) * SUB)
                tiles = _key_tiles(s, first_block)
                keys = slice(tiles[0] * SUB, (tiles[-1] + 1) * SUB)
                brows = slice((tiles[0] - s) * SUB, (tiles[-1] - s + 1) * SUB)
                q2, do2 = _both_heads(q[rows]), _both_heads(dob[rows])
                lse = jnp.concatenate([lse_ref[0, 0, s:s + 1, :], lse_ref[0, 0, NS + s:NS + s + 1, :]], axis=1)
                delta = jnp.concatenate([deltas[0][0:1, rows], deltas[1][0:1, rows]], axis=1)
                p = jnp.exp(_dot_nt(kwin[keys], q2) + b_ref[0, brows, :] - lse)
                ds = p * (_dot_nt(vwin[keys], do2) - delta)
                db_ref[0, brows, :] += ds
                dsb = ds.astype(BF16)
                dk_c = _dot(dsb, q2)
                dv_c = _dot(p.astype(BF16), do2)
                dq_ref[rows, :] = (_own_head_rows(_dot(kt[:, keys], dsb)).T * scale).astype(BF16)
                for n, t in enumerate(tiles):
                    win, loc = slice(t * SUB, (t + 1) * SUB), slice(n * SUB, (n + 1) * SUB)
                    if t in written:
                        dkw[win, :] += dk_c[loc]
                        dvw[win, :] += dv_c[loc]
                    else:
                        dkw[win, :] = dk_c[loc]
                        dvw[win, :] = dv_c[loc]
                        written.add(t)
            dk_ref[...] = (dkw[TB:, :] + dk_carry[...]).astype(BF16)
            dv_ref[...] = (dvw[TB:, :] + dv_carry[...]).astype(BF16)
            if not first_block:
                dk_carry[...] = dkw[:TB, :]
                dv_carry[...] = dvw[:TB, :]

        pl.when(i == 0)(functools.partial(compute, True))
        pl.when(i > 0)(functools.partial(compute, False))

    blk = lambda s: NB - 1 - s
    prev = lambda s: jnp.maximum(NB - 2 - s, 0)
    rows = lambda which, off: pl.BlockSpec((TB, LANES), lambda p, s: (which(s), off + p))
    out = jax.ShapeDtypeStruct((T, NP * LANES), BF16)
    strip = pl.BlockSpec((1, BANDW, 2 * SUB), lambda p, s: (p, 0, 0))
    return pl.pallas_call(
        body, name="attn_bwd", grid=(NP, NB),
        in_specs=[rows(blk, 0), rows(blk, NP), rows(prev, NP), rows(blk, 2 * NP), rows(prev, 2 * NP),
                  pl.BlockSpec((LANES, TB), lambda p, s: (p, blk(s))), pl.BlockSpec((LANES, TB), lambda p, s: (p, prev(s))),
                  rows(blk, 0), rows(blk, 0), pl.BlockSpec((1, 1, 8, LANES), lambda p, s: (p, blk(s), 0, 0)), strip, ANY],
        out_specs=[rows(blk, 0), rows(blk, 0), rows(blk, 0), strip],
        out_shape=[out, out, out, jax.ShapeDtypeStruct((NP, BANDW, 2 * SUB), F32)],
        scratch_shapes=[pltpu.VMEM((TB, LANES), F32), pltpu.VMEM((TB, LANES), F32),
                        pltpu.VMEM((2 * TB, LANES), F32), pltpu.VMEM((2 * TB, LANES), F32)],
        compiler_params=_params("arbitrary", "arbitrary"),
    )(qkv, qkv, qkv, qkv, qkv, kv_t, kv_t, dyc, yc, lse, bias_t, after)


def _inproj_mixers_bwd(pa, dyab, cw, wbd, ps, dqkv, w, h, g, dh1):
    T, D = pa.shape
    DC = D // 4
    NB = T // TB
    N = TB + HALO
    widths = [p.shape[1] for p in dqkv]
    n = len(dqkv)

    def body(*refs):
        pa_ref, halo_ref, dy_ref, cw_ref, wbd_ref, ps_ref = refs[:6]
        parts = refs[6:6 + n]
        w_ref, h_ref, g_ref, dh1_ref, dpa_ref, dh_ref, dcw_ref, dwbd_ref, dps_ref, dg_ref, dc_carry, e_carry = refs[6 + n:]
        step = pl.program_id(0)
        i = NB - 1 - step
        _acc_init(step, dcw_ref, dwbd_ref, dps_ref, dg_ref, dc_carry, e_carry)
        x = pa_ref[...]
        hl = jnp.where(i > 0, halo_ref[...], 0.0)
        gb, gc, u, pu = (x[:, k * DC:(k + 1) * DC] for k in range(4))
        dy = dy_ref[...]
        dya, dyb = dy[:, :DC], dy[:, DC:]
        cwv = cw_ref[...]
        z = gc * u
        z1, z2 = _conv_taps(z, hl[:, DC:2 * DC] * hl[:, 2 * DC:3 * DC])
        dgb = dya * (cwv[2:3] * z + cwv[1:2] * z1 + cwv[0:1] * z2)
        dconv = dya * gb
        dcw_ref[0:1, :] += jnp.sum(dconv * z2, axis=0, keepdims=True)
        dcw_ref[1:2, :] += jnp.sum(dconv * z1, axis=0, keepdims=True)
        dcw_ref[2:3, :] += jnp.sum(dconv * z, axis=0, keepdims=True)
        ext = jnp.concatenate([dconv, dc_carry[...]], axis=0)
        dz = cwv[2:3] * dconv + cwv[1:2] * pltpu.roll(ext, N - 1, 0)[:TB] + cwv[0:1] * pltpu.roll(ext, N - 2, 0)[:TB]
        dc_carry[...] = dconv[:HALO]
        cnt = _pool_count(i, DC)
        d = _pool_d(pu, hl[:, 3 * DC:], cnt).astype(BF16)
        psv = ps_ref[...]
        wb = wbd_ref[...]
        dps_ref[0:1, :] += jnp.sum(dyb * _dot(d, wb), axis=0, keepdims=True)
        dys = (dyb * psv).astype(BF16)
        dwbd_ref[...] += _dot_tn(d, dys)
        dd = _dot_nt(dys, wb)
        e = dd / cnt
        ext = jnp.concatenate([e, e_carry[...]], axis=0)
        a2 = ext + pltpu.roll(ext, N - 1, 0)
        a4 = a2 + pltpu.roll(a2, N - 2, 0)
        a8 = a4 + pltpu.roll(a4, N - 4, 0)
        a16 = a8 + pltpu.roll(a8, N - 8, 0)
        dpu = _pick_group([a2, a4, a8, a16], DC)[:TB] - dd
        e_carry[...] = e[:HALO]
        dpa = jnp.concatenate([dgb, dz * u, dz * gc, dpu], axis=1).astype(BF16)
        dpa_ref[...] = dpa
        dxn, off = _dot_nt(dpa, w_ref[:, :D]), D
        for p_ref, wd in zip(parts, widths):
            dxn = dxn + _dot_nt(p_ref[...], w_ref[:, off:off + wd])
            off += wd
        dx, dg = _rms_bwd(dxn, h_ref[...], g_ref[...])
        dg_ref[0:1, :] += dg
        dh_ref[...] = dh1_ref[...] + dx

    blk = lambda m: pl.BlockSpec((TB, m), lambda s: (NB - 1 - s, 0))
    return pl.pallas_call(
        body, name="inproj_mixers_bwd", grid=(NB,),
        in_specs=[blk(D), pl.BlockSpec((HALO, D), lambda s: (jnp.maximum((NB - 1 - s) * (TB // HALO) - 1, 0), 0)),
                  blk(2 * DC), _full((8, DC)), _full((DC, DC)), _full((1, DC))] + [blk(wd) for wd in widths]
                 + [_full(w.shape), blk(D), _full((1, D)), blk(D)],
        out_specs=[blk(D), blk(D), _full((8, DC)), _full((DC, DC)), _full((8, DC)), _full((8, D))],
        out_shape=[jax.ShapeDtypeStruct((T, D), BF16), jax.ShapeDtypeStruct((T, D), F32), jax.ShapeDtypeStruct((8, DC), F32),
                   jax.ShapeDtypeStruct((DC, DC), F32), jax.ShapeDtypeStruct((8, DC), F32), jax.ShapeDtypeStruct((8, D), F32)],
        scratch_shapes=[pltpu.VMEM((HALO, DC), F32), pltpu.VMEM((HALO, DC), F32)],
        compiler_params=_params("arbitrary"),
    )(pa, pa, dyab, cw, wbd, ps, *dqkv, w, h, g, dh1)


def _wgrad(a, b, tk, tn, name):
    T, K = a.shape
    N = b.shape[1]

    def body(a_ref, b_ref, o_ref):
        _acc_init(pl.program_id(2), o_ref)
        o_ref[...] += _dot_tn(a_ref[...], b_ref[...])

    return pl.pallas_call(
        body, name=name, grid=(K // tk, N // tn, T // TBW),
        in_specs=[pl.BlockSpec((TBW, tk), lambda k, n, t: (t, k)), pl.BlockSpec((TBW, tn), lambda k, n, t: (t, n))],
        out_specs=pl.BlockSpec((tk, tn), lambda k, n, t: (k, n)),
        out_shape=jax.ShapeDtypeStruct((K, N), F32),
        compiler_params=_params("parallel", "parallel", "arbitrary"),
    )(a, b)


def _wgrad_concat(a, bs, name):
    T, K = a.shape
    widths = [b.shape[1] for b in bs]
    n = len(bs)

    def body(*refs):
        a_ref, b_refs, o_ref = refs[0], refs[1:1 + n], refs[1 + n]
        _acc_init(pl.program_id(0), o_ref)
        av, off = a_ref[...], 0
        for b_ref, wd in zip(b_refs, widths):
            o_ref[:, off:off + wd] += _dot_tn(av, b_ref[...])
            off += wd

    row = lambda m: pl.BlockSpec((TB, m), lambda t: (t, 0))
    return pl.pallas_call(
        body, name=name, grid=(T // TB,),
        in_specs=[row(K)] + [row(wd) for wd in widths],
        out_specs=_full((K, sum(widths))),
        out_shape=jax.ShapeDtypeStruct((K, sum(widths)), F32),
        compiler_params=_params("arbitrary"),
    )(a, *bs)


def _adamw(w, g, m, v, tr, name):
    R, C = w.shape

    def body(w_ref, g_ref, m_ref, v_ref, d_ref, nm_ref, nv_ref):
        gv = g_ref[...]
        nm = ADAM_B1 * m_ref[...] + (1.0 - ADAM_B1) * gv
        nv = ADAM_B2 * v_ref[...] + (1.0 - ADAM_B2) * (gv * gv)
        m_hat = nm / (1.0 - ADAM_B1 ** ADAM_STEP)
        v_hat = nv / (1.0 - ADAM_B2 ** ADAM_STEP)
        d_ref[...] = -ADAM_LR * (m_hat / (jnp.sqrt(v_hat) + ADAM_EPS) + ADAM_WD * w_ref[...])
        nm_ref[...] = nm
        nv_ref[...] = nv

    blk = pl.BlockSpec((tr, C), lambda i: (i, 0))
    out = jax.ShapeDtypeStruct((R, C), F32)
    return pl.pallas_call(
        body, name=name, grid=(R // tr,), in_specs=[blk] * 4, out_specs=[blk] * 3, out_shape=[out] * 3,
        compiler_params=_params("parallel"),
    )(w, g, m, v)


def _sum_slots(x, tr, name):
    n, R, C = x.shape

    def body(x_ref, o_ref):
        acc = x_ref[0]
        for k in range(1, n):
            acc = acc + x_ref[k]
        o_ref[...] = acc

    return pl.pallas_call(
        body, name=name, grid=(R // tr,),
        in_specs=[pl.BlockSpec((n, tr, C), lambda i: (0, i, 0))],
        out_specs=pl.BlockSpec((tr, C), lambda i: (i, 0)),
        out_shape=jax.ShapeDtypeStruct((R, C), F32),
        compiler_params=_params("parallel"),
    )(x)


def _add_half(view, other, c, name):
    if view.ndim == 3:
        _, R2, N = view.shape
        tr = 128
        grid = (R2 // tr,)
        in_specs = [pl.BlockSpec((1, tr, N), lambda i, c_ref: (c_ref[0], i, 0)), pl.BlockSpec((1, tr, N), lambda i, c_ref: (0, i, 0))]
        out_spec = pl.BlockSpec((tr, N), lambda i, c_ref: (i, 0))
        out_shape = jax.ShapeDtypeStruct((R2, N), BF16)

        def body(c_ref, a_ref, b_ref, o_ref):
            o_ref[...] = (a_ref[0] + b_ref[0]).astype(BF16)
    else:
        S, _, R2, C = view.shape
        grid = (S,)
        in_specs = [pl.BlockSpec((1, 1, R2, C), lambda s, c_ref: (s, c_ref[0], 0, 0)), pl.BlockSpec((1, 1, R2, C), lambda s, c_ref: (s, 0, 0, 0))]
        out_spec = pl.BlockSpec((1, R2, C), lambda s, c_ref: (s, 0, 0))
        out_shape = jax.ShapeDtypeStruct((S, R2, C), BF16)

        def body(c_ref, a_ref, b_ref, o_ref):
            o_ref[0] = (a_ref[0, 0] + b_ref[0, 0]).astype(BF16)

    return pl.pallas_call(
        body, name=name,
        grid_spec=pltpu.PrefetchScalarGridSpec(num_scalar_prefetch=1, grid=grid, in_specs=in_specs, out_specs=out_spec),
        out_shape=out_shape, compiler_params=_params("parallel"),
    )(c, view, other)


def _sum_chips(slots, part, kind, place, name):
    _, R2, C = slots.shape
    tr = min(R2, 128) if kind == "col" else R2

    def body(p_ref, s_ref, own_ref, o_ref):
        me = p_ref[0]
        own = own_ref[...] if kind == "col" else own_ref[0]
        acc = None
        for k in range(4):
            term = jnp.where(me == k, own, s_ref[k]).astype(F32)
            acc = term if acc is None else acc + term
        o_ref[0] = acc

    own_spec = (pl.BlockSpec((tr, C), lambda i, p: (i, p[0])) if kind == "col"
                else pl.BlockSpec((1, R2, C), lambda i, p: (p[0], 0, 0)))
    return pl.pallas_call(
        body, name=name,
        grid_spec=pltpu.PrefetchScalarGridSpec(
            num_scalar_prefetch=1, grid=(R2 // tr,),
            in_specs=[pl.BlockSpec((4, tr, C), lambda i, p: (0, i, 0)), own_spec],
            out_specs=pl.BlockSpec((1, tr, C), lambda i, p: (p[1], i, 0))),
        out_shape=jax.ShapeDtypeStruct((2, R2, C), F32), compiler_params=_params("parallel"),
    )(place, slots, part)


def _place():
    x, y, c = lax.axis_index("x"), lax.axis_index("y"), lax.axis_index("c")
    chips = [(1 - x, y), (x, 1 - y), (1 - x, 1 - y)]
    return x, y, c, 2 * x + y, chips


def _remote(src, dst, send_sems, recv_sems, k, to):
    return pltpu.make_async_remote_copy(src_ref=src, dst_ref=dst, send_sem=send_sems.at[k], recv_sem=recv_sems.at[k],
                                        device_id=to, device_id_type=MESH)


def _full_shape(shard_shape, kind):
    *lead, R, C = shard_shape
    return (*lead, R, 4 * C) if kind == "col" else (*lead, 4 * R, C)


def _slab(full_ref, shard_rc, kind, chip, half=None):
    R, C = shard_rc
    lead = (slice(None),) * (len(full_ref.shape) - 2)
    if kind == "col":
        rows = pl.ds(0, R) if half is None else pl.ds(half * (R // 2), R // 2)
        return full_ref.at[(*lead, rows, pl.ds(chip * C, C))]
    rows = pl.ds(chip * R, R) if half is None else pl.ds(chip * R + half * (R // 2), R // 2)
    return full_ref.at[(*lead, rows, slice(None))]


def _row_half(ref, half):
    R = ref.shape[-2]
    lead = (slice(None),) * (len(ref.shape) - 2)
    return ref.at[(*lead, pl.ds(half * (R // 2), R // 2), slice(None))]


def _allgather_weights(shards, kinds, after):
    n = len(shards)
    full_shapes = [_full_shape(s.shape, k) for s, k in zip(shards, kinds)]

    def body(*refs):
        ins, outs, (send_sems, recv_sems) = refs[:n], refs[n + 1:2 * n + 1], refs[2 * n + 1:]
        x, y, c, me, chips = _place()
        slab = lambda a, chip, half=None: _slab(outs[a], shards[a].shape[-2:], kinds[a], chip, half)
        my_half = lambda a: _row_half(ins[a], c)
        own = [_remote(ins[a], slab(a, me), send_sems, recv_sems, 6 * n + a, (x, y, 1 - c)) for a in range(n)]
        first = [_remote(my_half(a), slab(a, me, c), send_sems, recv_sems, j * n + a, (*chip, c))
                 for j, chip in enumerate(chips) for a in range(n)]
        for cp in first + own:
            cp.start()
        passed = []
        for j, chip in enumerate(chips):
            src = 2 * chip[0] + chip[1]
            for a in range(n):
                _remote(my_half(a), slab(a, src, c), send_sems, recv_sems, j * n + a, (x, y, c)).wait_recv()
                cp = _remote(slab(a, src, c), slab(a, src, c), send_sems, recv_sems, (3 + j) * n + a, (x, y, 1 - c))
                cp.start()
                passed.append(cp)
        for j, chip in enumerate(chips):
            src = 2 * chip[0] + chip[1]
            for a in range(n):
                _remote(my_half(a), slab(a, src, 1 - c), send_sems, recv_sems, (3 + j) * n + a, (x, y, c)).wait_recv()
        for cp in own:
            cp.wait_recv()
        for cp in first + passed + own:
            cp.wait_send()

    return pl.pallas_call(
        body, name="allgather_weights",
        in_specs=[ANY] * (n + 1), out_specs=[ANY] * n,
        out_shape=[jax.ShapeDtypeStruct(s, BF16) for s in full_shapes],
        scratch_shapes=[pltpu.SemaphoreType.DMA((7 * n,)), pltpu.SemaphoreType.DMA((7 * n,))],
        compiler_params=pltpu.CompilerParams(has_side_effects=True),
    )(*shards, after)


HBM = pl.BlockSpec(memory_space=pltpu.HBM)
SEM = pl.BlockSpec(memory_space=pltpu.SEMAPHORE)
DATAFLOW = pltpu.SideEffectType.DATAFLOW_SIDE_EFFECTING


def _split_start(name, bufs, plan, ncopies, after=None):
    nb = len(bufs)
    nin = nb + (after is not None)

    def body(*refs):
        send_sems, recv_sems, token = refs[nin], refs[nin + 1], refs[-1]
        for k, (src, dst, to) in enumerate(plan(refs[:nb])):
            _remote(src, dst, send_sems, recv_sems, k, to).start()
        token[...] = jnp.zeros_like(token)

    out = pl.pallas_call(
        body, name=name,
        out_shape=(pltpu.SemaphoreType.DMA((ncopies,)), pltpu.SemaphoreType.DMA((ncopies,)),
                   *[pltpu.HBM(b.shape, b.dtype) for b in bufs], jax.ShapeDtypeStruct((8, LANES), F32)),
        in_specs=[HBM] * nb + [ANY] * (nin - nb), out_specs=(SEM, SEM, *[HBM] * nb, pl.BlockSpec(memory_space=pltpu.VMEM)),
        input_output_aliases={i: 2 + i for i in range(nb)},
        compiler_params=pltpu.CompilerParams(has_side_effects=DATAFLOW),
    )(*[pltpu.with_memory_space_constraint(b, pltpu.HBM) for b in bufs], *([] if after is None else [after]))
    return out[0], out[1], list(out[2:2 + nb]), out[-1]


def _split_wait(name, send_sems, recv_sems, bufs, plan, after):
    nb = len(bufs)

    def body(*refs):
        s_sems, r_sems = refs[nb], refs[nb + 1]
        for k, (src, dst, to) in enumerate(plan(refs[:nb])):
            cp = _remote(src, dst, s_sems, r_sems, k, to)
            cp.wait_send()
            cp.wait_recv()

    return pl.pallas_call(
        body, name=name,
        out_shape=tuple(pltpu.HBM(b.shape, b.dtype) for b in bufs),
        in_specs=[HBM] * nb + [SEM, SEM, ANY], out_specs=tuple([HBM] * nb),
        input_output_aliases={i: i for i in range(nb)},
        compiler_params=pltpu.CompilerParams(has_side_effects=DATAFLOW),
    )(*bufs, send_sems, recv_sems, after)


def _gather_plan(n, shard_rcs, kinds):
    def plan(refs):
        x, y, c, me, chips = _place()
        out = []
        for a in range(n):
            shard, full = refs[a], refs[n + a]
            out.append((shard, _slab(full, shard_rcs[a], kinds[a], me), (x, y, 1 - c)))
            for chip in chips:
                for cc in (c, 1 - c):
                    out.append((_row_half(shard, c), _slab(full, shard_rcs[a], kinds[a], me, c), (*chip, cc)))
        return out
    return plan


def _exchange_plan(n, kinds):
    def plan(refs):
        x, y, c, me, chips = _place()
        out = []
        for a in range(n):
            part, slots = refs[a], refs[n + a]
            C = slots.shape[2]
            for chip in chips:
                dst = 2 * chip[0] + chip[1]
                src = part.at[:, pl.ds(dst * C, C)] if kinds[a] == "col" else part.at[dst]
                out.append((src, slots.at[me], (*chip, c)))
        return out
    return plan


def _swap_plan(ndims):
    n = len(ndims)

    def plan(refs):
        x, y, c, _, _ = _place()
        out = []
        for a in range(n):
            src = refs[a].at[pl.ds(1 - c, 1)] if ndims[a] == 3 else refs[a].at[:, pl.ds(1 - c, 1)]
            out.append((src, refs[n + a], (x, y, 1 - c)))
        return out
    return plan


def _join_halves(bufs):
    n = len(bufs)

    def body(*refs):
        outs, (send_sems, recv_sems) = refs[n:2 * n], refs[2 * n:]
        x, y, c, _, _ = _place()
        sends = [_remote(outs[a].at[c], outs[a].at[c], send_sems, recv_sems, a, (x, y, 1 - c)) for a in range(n)]
        for cp in sends:
            cp.start()
        for a in range(n):
            _remote(outs[a].at[c], outs[a].at[1 - c], send_sems, recv_sems, a, (x, y, c)).wait_recv()
        for cp in sends:
            cp.wait_send()

    return pl.pallas_call(
        body, name="grad_join_halves", in_specs=[ANY] * n, out_specs=[ANY] * n,
        out_shape=[jax.ShapeDtypeStruct(b.shape, F32) for b in bufs],
        input_output_aliases={a: a for a in range(n)},
        scratch_shapes=[pltpu.SemaphoreType.DMA((n,)), pltpu.SemaphoreType.DMA((n,))],
        compiler_params=pltpu.CompilerParams(has_side_effects=True),
    )(*bufs)


def _allgather_small(block):
    M, N = block.shape

    def body(x_ref, out_ref, send_sems, recv_sems, local_sem):
        x, y, c, _, chips = _place()
        me, sibling = (x, y, c), (x, y, 1 - c)

        def rows(px, py, pc):
            return out_ref.at[pl.ds((4 * px + 2 * py + pc) * M, M), :]

        def copy(k, blk, to, src=None):
            return _remote(rows(*blk) if src is None else src, rows(*blk), send_sems, recv_sems, k, to)

        mine = pltpu.make_async_copy(x_ref, rows(*me), local_sem)
        mine.start()
        first = [copy(0, me, sibling, src=x_ref)] + [copy(1 + j, me, (*chip, c), src=x_ref) for j, chip in enumerate(chips)]
        for cp in first:
            cp.start()
        passed = [copy(4 + j, (*chip, c), sibling) for j, chip in enumerate(chips)]
        for j, chip in enumerate(chips):
            copy(1 + j, (*chip, c), me).wait_recv()
            passed[j].start()
        copy(0, sibling, me).wait_recv()
        for j, chip in enumerate(chips):
            copy(4 + j, (*chip, 1 - c), me).wait_recv()
        for cp in first + passed:
            cp.wait_send()
        mine.wait()

    vm = pl.BlockSpec(memory_space=pltpu.VMEM)
    return pl.pallas_call(
        body, name="allgather_small", in_specs=[vm], out_specs=vm,
        out_shape=jax.ShapeDtypeStruct((8 * M, N), F32),
        scratch_shapes=[pltpu.SemaphoreType.DMA((7,)), pltpu.SemaphoreType.DMA((7,)), pltpu.SemaphoreType.DMA],
        compiler_params=pltpu.CompilerParams(has_side_effects=True, vmem_limit_bytes=VMEM_LIMIT),
    )(block)


def _pack(arrays):
    flat = jnp.concatenate([a.reshape(-1) for a in arrays])
    pad = (-flat.shape[0]) % (8 * LANES)
    return jnp.pad(flat, (0, pad)).reshape(-1, LANES)


def _unpack(packed, shapes):
    flat, out, off = packed.reshape(-1), [], 0
    for s in shapes:
        size = 1
        for d in s:
            size *= d
        out.append(flat[off:off + size].reshape(s))
        off += size
    return out


def _block_diag(pw):
    G, n, _ = pw.shape
    eye = jnp.eye(G, dtype=pw.dtype)
    return (eye[:, None, :, None] * pw[:, :, None, :]).reshape(G * n, G * n)


def _diag_blocks(m, G):
    n = m.shape[0] // G
    return jnp.stack([m[g * n:(g + 1) * n, g * n:(g + 1) * n] for g in range(G)])


def _pad_rows(a, rows):
    return jnp.pad(a, ((0, rows - a.shape[0]), (0, 0)))


def kernel(x, w_in, w_out, conv_w, pool_w, pool_scale, rel_bias, group_gain, pre_mix_g, post_mix_g, pre_ffn_g, post_ffn_g, w_gate_up, w_down, loss_target, m_w_in, m_w_out, m_conv_w, m_pool_w, m_pool_scale, m_rel_bias, m_group_gain, m_pre_mix_g, m_post_mix_g, m_pre_ffn_g, m_post_ffn_g, m_w_gate_up, m_w_down, v_w_in, v_w_out, v_conv_w, v_pool_w, v_pool_scale, v_rel_bias, v_group_gain, v_pre_mix_g, v_post_mix_g, v_pre_ffn_g, v_post_ffn_g, v_w_gate_up, v_w_down):
    L = w_in.shape[0]
    T, D = x.shape[1], x.shape[2]
    DC = D // 4
    NH = rel_bias.shape[1]
    NREL = rel_bias.shape[2]
    G = pool_w.shape[1]
    cs = conv_w.shape[2]
    assert TB == LEFT_CHUNKS * CHUNK and T % TB == 0 and D % (4 * LANES) == 0 and NH * HEAD_DIM == D // 2
    xi, yi, ci = lax.axis_index("x"), lax.axis_index("y"), lax.axis_index("c")
    chip = 2 * xi + yi

    kinds = ("col", "row", "col", "row")
    big = (w_in, w_out, w_gate_up, w_down)
    rcs = [w.shape[-2:] for w in big]
    conv_gathered = _allgather_small(_pack([conv_w]))
    conv_all = conv_gathered.reshape(8, -1)[:, :L * 3 * cs].reshape(4, 2, L, 3, cs)[:, 0]
    conv_full = jnp.moveaxis(conv_all, 0, 2).reshape(L, 3, 4 * cs)
    landing = lambda which: [lax.empty(_full_shape(rcs[k], kinds[k]), BF16) for k in which]
    wi0 = _allgather_weights([w_in[0:1].astype(BF16)], kinds[:1], conv_gathered)[0][0]
    rest_plan = _gather_plan(3, rcs[1:], kinds[1:])
    rest = _split_start("gather_start_0", [w[0].astype(BF16) for w in big[1:]] + landing((1, 2, 3)), rest_plan, 3 * 7, after=wi0)
    layer_plan = _gather_plan(4, rcs, kinds)
    full = [None] * L

    h = x[0]
    saved = []
    token = rest[3]
    for l in range(L):
        g_pre, g_pm, g_pf, g_po = (a[l][None] for a in (pre_mix_g, post_mix_g, pre_ffn_g, post_ffn_g))
        if 0 < l < L - 1:
            nxt = _split_start(f"gather_start_{l + 1}", [w[l + 1].astype(BF16) for w in big] + landing(range(4)), layer_plan, 4 * 7,
                               after=full[l][0])
            token = nxt[3]
        if l < L - 1:
            g_pre = g_pre + token[0:1, 0:1]
        gg, ps = group_gain[l][None], pool_scale[l][None]
        cw = _pad_rows(conv_full[l], 8)
        wbd = _block_diag(pool_w[l]).astype(BF16)
        bias = _bias_build(jnp.pad(rel_bias[l], ((0, 0), (0, RBP - NREL))))
        wi = wi0 if l == 0 else full[l][0]
        xn, pa, qkv, kv_t, yab = _inproj_mixers_fwd(h, g_pre, wi, cw, wbd, ps)
        yc, lse = _attn_fwd(qkv, kv_t, bias)
        if l == 0:
            full[0] = [wi0, *_split_wait("gather_wait_0", rest[0], rest[1], rest[2], rest_plan, yc)[3:6]]
            if L > 1:
                nxt = _split_start("gather_start_1", [w[1].astype(BF16) for w in big] + landing(range(4)), layer_plan, 4 * 7,
                                   after=full[0][1])
                g_pm = g_pm + nxt[3][0:1, 0:1]
        wi, wo, wgu, wdn = full[l]
        y, mix, h1, hn, gu, ff, ffo, h2 = _layer_tail_fwd(yab, yc, gg, wo, h, g_pm, g_pf, wgu, wdn, g_po)
        saved.append(dict(h=h, xn=xn, pa=pa, qkv=qkv, kv_t=kv_t, lse=lse, yab=yab, yc=yc, y=y, mix=mix, h1=h1, hn=hn, gu=gu, ff=ff, ffo=ffo,
                          cw=cw, wbd=wbd, bias=bias, ps=ps, gg=gg, g_pre=g_pre, g_pm=g_pm, g_pf=g_pf, g_po=g_po))
        h = h2
        if l + 1 < L:
            full[l + 1] = list(_split_wait(f"gather_wait_{l + 1}", nxt[0], nxt[1], nxt[2], layer_plan, h2)[4:8])

    dh, loss_tile = _loss_grad(h, loss_target[0])
    loss = lax.psum(loss_tile[0, 0], ("x", "y", "c"))

    xplan = _exchange_plan(4, kinds)
    splan = _swap_plan([3, 4, 3, 4])
    place = jnp.stack([chip, ci]).astype(jnp.int32)
    cvec = ci.reshape(1).astype(jnp.int32)
    small_grads = [None] * L
    shard_grads = [None] * L
    dbiases = [None] * L

    def start_exchange(lp, swap):
        thru = _split_wait(f"grad_swap_wait_{lp}", swap[0], swap[1], swap[2], splan, swap[4])
        chip_sums = [_add_half(v, t, cvec, "grad_add_half") for v, t in zip(thru[:4], thru[4:])]
        slots = [lax.empty((4, p.shape[0], p.shape[1] // 4) if k == "col" else p.shape, BF16) for p, k in zip(chip_sums, kinds)]
        ssem, rsem, bufs, tok = _split_start(f"grad_exchange_start_{lp}", chip_sums + slots, xplan, 4 * 3)
        return (lp, ssem, rsem, bufs), tok

    def finish_exchange(pending, after):
        lp, ssem, rsem, thru = pending
        landed = _split_wait(f"grad_exchange_wait_{lp}", ssem, rsem, thru, xplan, after)
        bufs = [_sum_chips(landed[4 + a], landed[a], kinds[a], place, "grad_sum_chips") for a in range(4)]
        shard_grads[lp] = [j.reshape(2 * j.shape[1], j.shape[2]) for j in _join_halves(bufs)]

    swap = None
    exchange = None
    for l in reversed(range(L)):
        s = saved[l]
        wi, wo, wgu, wdn = full[l]
        g_po = s["g_po"] if swap is None else s["g_po"] + swap[3][0:1, 0:1]
        dffo, dgu, dh1, dmix, dyab, dyc, dg_po, dg_pf, dg_pm, dgg = _layer_tail_bwd(
            dh, s["ffo"], g_po, s["gu"], wdn, wgu, s["h1"], s["g_pf"], s["mix"], s["g_pm"], wo, s["yab"], s["yc"], s["gg"])
        after = dyc
        if swap is not None:
            started, after = start_exchange(l + 1, (*swap[:4], dh1))
            if exchange is not None:
                finish_exchange(exchange, after)
            exchange = started
        dq, dk, dv, dbiases[l] = _attn_bwd(s["qkv"], s["kv_t"], dyc, s["yc"], s["lse"], s["bias"], after)
        dpa, dh, dcw, dwbd, dps, dg_pre = _inproj_mixers_bwd(s["pa"], dyab, s["cw"], s["wbd"], s["ps"], [dq, dk, dv], wi,
                                                             s["h"], s["g_pre"], dh1)
        dparts = [dpa, dq, dk, dv]
        F2, DFF = s["gu"].shape[1], s["ff"].shape[1]
        grads = [_wgrad_concat(s["xn"], dparts, "wgrad_in"),
                 _wgrad(s["y"], dmix, D, D, "wgrad_out"),
                 _wgrad(s["hn"], dgu, D, F2 // 4, "wgrad_gate_up"),
                 _wgrad(s["ff"], dffo, DFF // 2, D, "wgrad_down")]
        small_grads[l] = [dcw[:3], _diag_blocks(dwbd, G), dps[0], None, dgg[0], dg_pre[0], dg_pm[0], dg_pf[0], dg_po[0]]
        views = [g.reshape(2, g.shape[0] // 2, g.shape[1]) if k == "col" else g.reshape(4, 2, g.shape[0] // 8, g.shape[1])
                 for g, k in zip(grads, kinds)]
        lands = [lax.empty((1,) + v.shape[1:] if v.ndim == 3 else (v.shape[0], 1) + v.shape[2:], F32) for v in views]
        swap = _split_start(f"grad_swap_start_{l}", views + lands, splan, 4)
    last, tok = start_exchange(0, (*swap[:4], swap[3]))
    if exchange is not None:
        finish_exchange(exchange, tok)
    for l in range(L):
        small_grads[l][3] = _bias_fold(dbiases[l], tok)[:NH, :NREL]

    names_shapes = [(L, 3, 4 * cs), pool_w.shape, pool_scale.shape, rel_bias.shape, group_gain.shape,
                    pre_mix_g.shape, post_mix_g.shape, pre_ffn_g.shape, post_ffn_g.shape]
    small_stacked = [jnp.stack([small_grads[l][k] for l in range(L)]) for k in range(len(names_shapes))]
    packed = _pack(small_stacked)
    M = packed.shape[0]
    total = _sum_slots(_allgather_small(packed).reshape(8, M, LANES), M, "small_sum_devices")
    g_small = _unpack(total, names_shapes)
    g_small[0] = lax.dynamic_slice_in_dim(g_small[0], chip * cs, cs, axis=2)
    finish_exchange(last, total)
    g_big = [jnp.stack([shard_grads[l][k] for l in range(L)]) for k in range(4)]

    def adam_big(w, g, m, v, name):
        shp = w.shape
        two = lambda a: a.reshape(shp[0] * shp[1], shp[2])
        return [o.reshape(shp) for o in _adamw(two(w), two(g), two(m), two(v), 256, name)]

    upd_in = adam_big(w_in, g_big[0], m_w_in, v_w_in, "adamw_in")
    upd_out = adam_big(w_out, g_big[1], m_w_out, v_w_out, "adamw_out")
    upd_gu = adam_big(w_gate_up, g_big[2], m_w_gate_up, v_w_gate_up, "adamw_gate_up")
    upd_dn = adam_big(w_down, g_big[3], m_w_down, v_w_down, "adamw_down")

    small_w = [conv_w, pool_w, pool_scale, rel_bias, group_gain, pre_mix_g, post_mix_g, pre_ffn_g, post_ffn_g]
    small_m = [m_conv_w, m_pool_w, m_pool_scale, m_rel_bias, m_group_gain, m_pre_mix_g, m_post_mix_g, m_pre_ffn_g, m_post_ffn_g]
    small_v = [v_conv_w, v_pool_w, v_pool_scale, v_rel_bias, v_group_gain, v_pre_mix_g, v_post_mix_g, v_pre_ffn_g, v_post_ffn_g]
    pw_, pg_, pm_, pv_ = _pack(small_w), _pack(g_small), _pack(small_m), _pack(small_v)
    shapes = [w.shape for w in small_w]
    upd_small = [_unpack(o, shapes) for o in _adamw(pw_, pg_, pm_, pv_, pw_.shape[0], "adamw_small")]

    def ordered(big4, small9):
        return [big4[0], big4[1], *small9, big4[2], big4[3]]

    grads = ordered(g_big, g_small)
    outs = [ordered([upd_in[k], upd_out[k], upd_gu[k], upd_dn[k]], upd_small[k]) for k in range(3)]
    return (loss, dh[None], *grads, *outs[0], *outs[1], *outs[2])
```

```python
import functools

import jax
import jax.numpy as jnp
from jax import lax
from jax.experimental import pallas as pl
from jax.experimental.pallas import tpu as pltpu

F32, BF16 = jnp.float32, jnp.bfloat16
EPS = 1e-6
CHUNK = 64
LEFT_CHUNKS = 8
REL_CLIP = 128
HEAD_DIM = 64
LANES = 128
POOL_WINDOWS = (2, 4, 8, 16)
HALO = 16
TB = LEFT_CHUNKS * CHUNK
TBF = 256
TBW = 1024
FF_TILE = 512
BAND = (LEFT_CHUNKS + 1) * CHUNK
SUB = 2 * CHUNK
BANDW = SUB + LEFT_CHUNKS * CHUNK
SKEW = 768
NEG = -1e30
RBP = 384
VMEM_LIMIT = 56 * 1024 * 1024
ADAM_LR, ADAM_B1, ADAM_B2, ADAM_EPS, ADAM_WD, ADAM_STEP = 0.001, 0.9, 0.999, 1e-08, 0.01, 10
MESH = pl.DeviceIdType.MESH
ANY = pl.BlockSpec(memory_space=pl.ANY)


def _params(*sem):
    kw = dict(vmem_limit_bytes=VMEM_LIMIT)
    if sem:
        kw["dimension_semantics"] = sem
    return pltpu.CompilerParams(**kw)


def _dot(a, b):
    return jnp.dot(a, b, preferred_element_type=F32)


def _dot_nt(a, b):
    return lax.dot_general(a, b, (((1,), (1,)), ((), ())), preferred_element_type=F32)


def _dot_tn(a, b):
    return lax.dot_general(a, b, (((0,), (0,)), ((), ())), preferred_element_type=F32)


def _rms(x, g):
    r = lax.rsqrt(jnp.mean(x * x, axis=-1, keepdims=True) + EPS)
    return x * r * g


def _rms_bwd(dy, x, g):
    r = lax.rsqrt(jnp.mean(x * x, axis=-1, keepdims=True) + EPS)
    xh = x * r
    dxh = dy * g
    dx = r * (dxh - xh * jnp.mean(dxh * xh, axis=-1, keepdims=True))
    return dx, jnp.sum(dy * xh, axis=0, keepdims=True)


def _full(shape):
    return pl.BlockSpec(shape, lambda *_: (0,) * len(shape))


def _acc_init(step, *refs):
    @pl.when(step == 0)
    def _():
        for r in refs:
            r[...] = jnp.zeros_like(r)


def _inproj_mixers_fwd(h, g, w, cw, wbd, ps):
    T, D = h.shape
    NQ = w.shape[1] - D
    NKV = 2 * NQ // 3
    DC = D // 4

    def body(h_ref, g_ref, w_ref, cw_ref, wbd_ref, ps_ref, xn_ref, pa_ref, qkv_ref, kvt_ref, yab_ref, halo):
        i = pl.program_id(0)
        _acc_init(i, halo)
        xn = _rms(h_ref[...], g_ref[...]).astype(BF16)
        xn_ref[...] = xn
        x = _dot(xn, w_ref[:, :D])
        pa_ref[...] = x
        qkv = _dot(xn, w_ref[:, D:])
        qkv_ref[...] = qkv.astype(BF16)
        kvt_ref[...] = qkv[:, NQ - NKV:].T.astype(BF16)
        hl = halo[...]
        gb, gc, u, pu = (x[:, k * DC:(k + 1) * DC] for k in range(4))
        z = gc * u
        z1, z2 = _conv_taps(z, hl[:, DC:2 * DC] * hl[:, 2 * DC:3 * DC])
        cwv = cw_ref[...]
        ya = gb * (cwv[2:3] * z + cwv[1:2] * z1 + cwv[0:1] * z2)
        d = _pool_d(pu, hl[:, 3 * DC:], _pool_count(i, DC))
        yb = _dot(d.astype(BF16), wbd_ref[...]) * ps_ref[...]
        yab_ref[...] = jnp.concatenate([ya, yb], axis=1)
        halo[...] = x[TB - HALO:, :]

    row = lambda n: pl.BlockSpec((TB, n), lambda i: (i, 0))
    return pl.pallas_call(
        body, name="inproj_mixers_fwd", grid=(T // TB,),
        in_specs=[row(D), _full((1, D)), _full(w.shape), _full((8, DC)), _full((DC, DC)), _full((1, DC))],
        out_specs=[row(D), row(D), row(NQ), pl.BlockSpec((NKV, TB), lambda i: (0, i)), row(2 * DC)],
        out_shape=[jax.ShapeDtypeStruct((T, D), BF16), jax.ShapeDtypeStruct((T, D), F32), jax.ShapeDtypeStruct((T, NQ), BF16),
                   jax.ShapeDtypeStruct((NKV, T), BF16), jax.ShapeDtypeStruct((T, 2 * DC), F32)],
        scratch_shapes=[pltpu.VMEM((HALO, D), F32)],
        compiler_params=_params("arbitrary"),
    )(h, g, w, cw, wbd, ps)


def _lane_groups(n, vals):
    lane = lax.broadcasted_iota(jnp.int32, (1, n), 1)
    q = n // 4
    return jnp.where(lane < q, vals[0], jnp.where(lane < 2 * q, vals[1], jnp.where(lane < 3 * q, vals[2], vals[3]))).astype(F32)


def _pick_group(levels, n):
    lane = lax.broadcasted_iota(jnp.int32, (1, n), 1)
    q = n // 4
    return jnp.where(lane < q, levels[0], jnp.where(lane < 2 * q, levels[1], jnp.where(lane < 3 * q, levels[2], levels[3])))


def _pool_count(blk, n):
    t1 = (blk * TB + 1 + lax.broadcasted_iota(jnp.int32, (TB, 1), 0)).astype(F32)
    return jnp.minimum(t1, _lane_groups(n, POOL_WINDOWS))


def _pool_d(pu, pu_halo, cnt):
    e = jnp.concatenate([pu_halo, pu], axis=0)
    s2 = e + pltpu.roll(e, 1, 0)
    s4 = s2 + pltpu.roll(s2, 2, 0)
    s8 = s4 + pltpu.roll(s4, 4, 0)
    s16 = s8 + pltpu.roll(s8, 8, 0)
    num = _pick_group([s2, s4, s8, s16], pu.shape[1])[HALO:]
    return num / cnt - pu


def _conv_taps(z, z_halo):
    e = jnp.concatenate([z_halo, z], axis=0)
    return pltpu.roll(e, 1, 0)[HALO:], pltpu.roll(e, 2, 0)[HALO:]


def _bias_bins(shape, col_dim):
    j = lax.broadcasted_iota(jnp.int32, shape, col_dim)
    b = lax.broadcasted_iota(jnp.int32, shape, 1 - col_dim)
    d = jnp.where(j < BAND, j, j - SKEW)
    live = jnp.logical_or(j < BAND, j > SKEW - CHUNK)
    bins = jnp.minimum(TB - d, REL_CLIP) + REL_CLIP
    return jnp.where(jnp.logical_and(live, bins == b), 1.0, 0.0).astype(F32)


def _skew_rows(x, left):
    row = lax.broadcasted_iota(jnp.int32, x.shape, 0)
    for b in range(SUB.bit_length() - 1):
        x = jnp.where(((row >> b) & 1) == 1, pltpu.roll(x, SKEW - (1 << b) if left else 1 << b, 1), x)
    return x


def _bias_build(rb):
    H = rb.shape[0]

    def body(rb_ref, o_ref):
        v = jnp.dot(rb_ref[...], _bias_bins((RBP, SKEW), 1), precision=lax.Precision.HIGHEST, preferred_element_type=F32)
        qc = lax.broadcasted_iota(jnp.int32, (SUB, BANDW), 0) >> (CHUNK.bit_length() - 1)
        kc = lax.broadcasted_iota(jnp.int32, (SUB, BANDW), 1) >> (CHUNK.bit_length() - 1)
        in_band = jnp.logical_and(kc >= qc, kc <= qc + LEFT_CHUNKS)
        for h in range(H):
            x = _skew_rows(jnp.broadcast_to(v[h:h + 1], (SUB, SKEW)), left=False)
            o_ref[h // 2, :, (h % 2) * SUB:(h % 2 + 1) * SUB] = jnp.where(in_band, x[:, :BANDW], NEG).T

    return pl.pallas_call(
        body, name="bias_build", in_specs=[_full(rb.shape)], out_specs=_full((H // 2, BANDW, 2 * SUB)),
        out_shape=jax.ShapeDtypeStruct((H // 2, BANDW, 2 * SUB), F32), grid=(1,),
        compiler_params=_params("arbitrary"),
    )(rb)


def _bias_fold(db, after):
    H = 2 * db.shape[0]

    def body(db_ref, after_ref, o_ref, sums):
        sums[...] = jnp.zeros_like(sums)
        for h in range(H):
            x = db_ref[h // 2, :, (h % 2) * SUB:(h % 2 + 1) * SUB].T
            x = _skew_rows(jnp.concatenate([x, jnp.zeros((SUB, SKEW - BANDW), F32)], axis=1), left=True)
            sums[h:h + 1, :] = jnp.sum(x, axis=0, keepdims=True)
        o_ref[...] = jnp.dot(sums[...], _bias_bins((SKEW, RBP), 0), precision=lax.Precision.HIGHEST, preferred_element_type=F32)

    return pl.pallas_call(
        body, name="bias_fold", grid=(1,), in_specs=[_full(db.shape), ANY], out_specs=_full((8, RBP)),
        out_shape=jax.ShapeDtypeStruct((8, RBP), F32), scratch_shapes=[pltpu.VMEM((8, SKEW), F32)],
        compiler_params=_params("arbitrary"),
    )(db, after)


def _both_heads(x):
    first = lax.broadcasted_iota(jnp.int32, (1, LANES), 1) < HEAD_DIM
    return jnp.concatenate([jnp.where(first, x, 0), jnp.where(first, 0, x)], axis=0)


def _own_head_rows(x2):
    n = x2.shape[1] // 2
    first = lax.broadcasted_iota(jnp.int32, (LANES, 1), 0) < HEAD_DIM
    return jnp.where(first, x2[:, :n], x2[:, n:])


def _key_tiles(s, first_block):
    return [t for t in range(s, s + BANDW // SUB) if not (first_block and t < TB // SUB)]


def _attn_fwd(qkv, kv_t, bias_t):
    T = qkv.shape[0]
    NP = qkv.shape[1] // (3 * LANES)
    NB = T // TB
    NS = TB // SUB
    scale = HEAD_DIM ** -0.5

    def body(q_ref, kc_ref, kp_ref, vtc_ref, vtp_ref, b_ref, o_ref, lse_ref):
        i = pl.program_id(1)

        def compute(first_block):
            q = q_ref[...] * scale
            kwin = jnp.concatenate([kp_ref[...], kc_ref[...]], axis=0)
            vt = jnp.concatenate([vtp_ref[...], vtc_ref[...]], axis=1)
            for s in range(NS):
                rows = slice(s * SUB, (s + 1) * SUB)
                tiles = _key_tiles(s, first_block)
                keys = slice(tiles[0] * SUB, (tiles[-1] + 1) * SUB)
                brows = slice((tiles[0] - s) * SUB, (tiles[-1] - s + 1) * SUB)
                q2 = _both_heads(q[rows])
                halves = []
                for a in range(2):
                    st = _dot_nt(kwin[keys], q2[a * SUB:(a + 1) * SUB]) + b_ref[0, brows, a * SUB:(a + 1) * SUB]
                    m = jnp.max(st, axis=0, keepdims=True)
                    p = jnp.exp(st - m)
                    l = jnp.sum(p, axis=0, keepdims=True)
                    halves.append(_dot(vt[:, keys], p.astype(BF16)) * (1.0 / l))
                    lse_ref[0, 0, a * NS + s:a * NS + s + 1, :] = m + jnp.log(l)
                o_ref[rows, :] = _own_head_rows(jnp.concatenate(halves, axis=1)).T

        pl.when(i == 0)(functools.partial(compute, True))
        pl.when(i > 0)(functools.partial(compute, False))

    prev = lambda i: jnp.maximum(i - 1, 0)
    return pl.pallas_call(
        body, name="attn_fwd", grid=(NP, NB),
        in_specs=[pl.BlockSpec((TB, LANES), lambda p, i: (i, p)),
                  pl.BlockSpec((TB, LANES), lambda p, i: (i, NP + p)),
                  pl.BlockSpec((TB, LANES), lambda p, i: (prev(i), NP + p)),
                  pl.BlockSpec((LANES, TB), lambda p, i: (NP + p, i)),
                  pl.BlockSpec((LANES, TB), lambda p, i: (NP + p, prev(i))),
                  pl.BlockSpec((1, BANDW, 2 * SUB), lambda p, i: (p, 0, 0))],
        out_specs=[pl.BlockSpec((TB, LANES), lambda p, i: (i, p)),
                   pl.BlockSpec((1, 1, 8, LANES), lambda p, i: (p, i, 0, 0))],
        out_shape=[jax.ShapeDtypeStruct((T, NP * LANES), F32), jax.ShapeDtypeStruct((NP, NB, 8, LANES), F32)],
        compiler_params=_params("parallel", "parallel"),
    )(qkv, qkv, qkv, kv_t, kv_t, bias_t)


def _group_bounds(D):
    return ((0, D // 4), (D // 4, D // 2), (D // 2, D))


def _ff_tiles(dff, width=FF_TILE):
    return [(slice(a, min(a + width, dff)), slice(dff + a, dff + min(a + width, dff))) for a in range(0, dff, width)]


def _load_resident(step, *pairs_and_sems):
    @pl.when(step == 0)
    def _():
        cps = [pltpu.make_async_copy(src, dst, sem) for src, dst, sem in pairs_and_sems]
        for cp in cps:
            cp.start()
        for cp in cps:
            cp.wait()


def _layer_tail_fwd(yab, yc, gg, w_out, h, g_pm, g_pf, w_gu, w_dn, g_po):
    T, D = h.shape
    F2 = w_gu.shape[1]
    DFF = F2 // 2

    def body(yab_ref, yc_ref, gg_ref, wo_hbm, h_ref, gpm_ref, gpf_ref, wgu_hbm, wdn_hbm, gpo_ref,
             y_ref, mix_ref, h1_ref, hn_ref, gu_ref, ff_ref, ffo_ref, h2_ref, wo_v, wgu_v, wdn_v, sems):
        _load_resident(pl.program_id(0), (wo_hbm, wo_v, sems.at[0]), (wgu_hbm, wgu_v, sems.at[1]), (wdn_hbm, wdn_v, sems.at[2]))
        yraw = jnp.concatenate([yab_ref[...], yc_ref[...]], axis=1)
        ggv = gg_ref[...]
        y = jnp.concatenate([_rms(yraw[:, a:b], ggv[:, a:b]) for a, b in _group_bounds(D)], axis=1).astype(BF16)
        y_ref[...] = y
        mix = _dot(y, wo_v[...])
        mix_ref[...] = mix.astype(BF16)
        h1 = h_ref[...] + _rms(mix, gpm_ref[...])
        h1_ref[...] = h1
        hn = _rms(h1, gpf_ref[...]).astype(BF16)
        hn_ref[...] = hn
        ffo = jnp.zeros((TBF, D), F32)
        for sg, su in _ff_tiles(DFF):
            gate = _dot(hn, wgu_v[:, sg])
            up = _dot(hn, wgu_v[:, su])
            gu_ref[:, sg] = gate.astype(BF16)
            gu_ref[:, su] = up.astype(BF16)
            ff = (gate * jax.nn.sigmoid(gate) * up).astype(BF16)
            ff_ref[:, sg] = ff
            ffo = ffo + _dot(ff, wdn_v[sg, :])
        ffo_ref[...] = ffo.astype(BF16)
        h2_ref[...] = h1 + _rms(ffo, gpo_ref[...])

    row = lambda n: pl.BlockSpec((TBF, n), lambda i: (i, 0))
    gain = _full((1, D))
    f32, bf16 = (lambda n: jax.ShapeDtypeStruct((T, n), F32)), (lambda n: jax.ShapeDtypeStruct((T, n), BF16))
    return pl.pallas_call(
        body, name="layer_tail_fwd", grid=(T // TBF,),
        in_specs=[row(D // 2), row(D // 2), gain, ANY, row(D), gain, gain, ANY, ANY, gain],
        out_specs=[row(D), row(D), row(D), row(D), row(F2), row(DFF), row(D), row(D)],
        out_shape=[bf16(D), bf16(D), f32(D), bf16(D), bf16(F2), bf16(DFF), bf16(D), f32(D)],
        scratch_shapes=[pltpu.VMEM(w_out.shape, BF16), pltpu.VMEM(w_gu.shape, BF16), pltpu.VMEM(w_dn.shape, BF16),
                        pltpu.SemaphoreType.DMA((3,))],
        compiler_params=_params("arbitrary"),
    )(yab, yc, gg, w_out, h, g_pm, g_pf, w_gu, w_dn, g_po)


def _loss_grad(h, tgt):
    T, D = h.shape

    def body(h_ref, t_ref, dh_ref, loss_ref):
        _acc_init(pl.program_id(0), loss_ref)
        diff = h_ref[...] - t_ref[...]
        dh_ref[...] = diff * (1.0 / D)
        loss_ref[...] += 0.5 * jnp.sum(jnp.mean(diff * diff, axis=-1, keepdims=True))

    row = pl.BlockSpec((TB, D), lambda i: (i, 0))
    return pl.pallas_call(
        body, name="loss_grad", grid=(T // TB,),
        in_specs=[row, row], out_specs=[row, _full((8, LANES))],
        out_shape=[jax.ShapeDtypeStruct((T, D), F32), jax.ShapeDtypeStruct((8, LANES), F32)],
        compiler_params=_params("arbitrary"),
    )(h, tgt)


def _layer_tail_bwd(dh2, ffo, g_po, gu, w_dn, w_gu, h1, g_pf, mix, g_pm, w_out, yab, yc, gg):
    T, D = dh2.shape
    F2 = gu.shape[1]
    DFF = F2 // 2

    def body(dh_ref, ffo_ref, gpo_ref, gu_ref, wdn_hbm, wgu_hbm, h_ref, gpf_ref, mix_ref, gpm_ref, wo_hbm, yab_ref, yc_ref, gg_ref,
             dffo_ref, dgu_ref, dh1_ref, dmix_ref, dyab_ref, dyc_ref, dgpo_ref, dgpf_ref, dgpm_ref, dgg_ref,
             wdn_v, wgu_v, wo_v, sems):
        step = pl.program_id(0)
        _load_resident(step, (wdn_hbm, wdn_v, sems.at[0]), (wgu_hbm, wgu_v, sems.at[1]), (wo_hbm, wo_v, sems.at[2]))
        _acc_init(step, dgpo_ref, dgpf_ref, dgpm_ref, dgg_ref)
        dh = dh_ref[...]
        dffo, dg = _rms_bwd(dh, ffo_ref[...].astype(F32), gpo_ref[...])
        dgpo_ref[0:1, :] += dg
        dffo = dffo.astype(BF16)
        dffo_ref[...] = dffo
        dhn = jnp.zeros((TBF, D), F32)
        for sg, su in _ff_tiles(DFF, DFF // 2):
            dff = _dot_nt(dffo, wdn_v[sg, :])
            gate, up = gu_ref[:, sg].astype(F32), gu_ref[:, su].astype(F32)
            sig = jax.nn.sigmoid(gate)
            dgate = (dff * up * (sig * (1.0 + gate * (1.0 - sig)))).astype(BF16)
            dup = (dff * (gate * sig)).astype(BF16)
            dgu_ref[:, sg] = dgate
            dgu_ref[:, su] = dup
            dhn = dhn + _dot_nt(dgate, wgu_v[:, sg]) + _dot_nt(dup, wgu_v[:, su])
        dx, dg = _rms_bwd(dhn, h_ref[...], gpf_ref[...])
        dgpf_ref[0:1, :] += dg
        dh1 = dh + dx
        dh1_ref[...] = dh1
        dmix, dg = _rms_bwd(dh1, mix_ref[...].astype(F32), gpm_ref[...])
        dgpm_ref[0:1, :] += dg
        dmix = dmix.astype(BF16)
        dmix_ref[...] = dmix
        dy = _dot_nt(dmix, wo_v[...])
        yraw = jnp.concatenate([yab_ref[...], yc_ref[...]], axis=1)
        ggv = gg_ref[...]
        parts = [_rms_bwd(dy[:, a:b], yraw[:, a:b], ggv[:, a:b]) for a, b in _group_bounds(D)]
        dgg_ref[0:1, :] += jnp.concatenate([p[1] for p in parts], axis=1)
        dyab_ref[...] = jnp.concatenate([parts[0][0], parts[1][0]], axis=1).astype(BF16)
        dyc_ref[...] = parts[2][0].astype(BF16)

    row = lambda n: pl.BlockSpec((TBF, n), lambda i: (i, 0))
    gain, acc = _full((1, D)), _full((8, D))
    f32, bf16 = (lambda n: jax.ShapeDtypeStruct((T, n), F32)), (lambda n: jax.ShapeDtypeStruct((T, n), BF16))
    acc_shape = jax.ShapeDtypeStruct((8, D), F32)
    return pl.pallas_call(
        body, name="layer_tail_bwd", grid=(T // TBF,),
        in_specs=[row(D), row(D), gain, row(F2), ANY, ANY, row(D), gain, row(D), gain, ANY, row(D // 2), row(D // 2), gain],
        out_specs=[row(D), row(F2), row(D), row(D), row(D // 2), row(D // 2), acc, acc, acc, acc],
        out_shape=[bf16(D), bf16(F2), f32(D), bf16(D), bf16(D // 2), bf16(D // 2), acc_shape, acc_shape, acc_shape, acc_shape],
        scratch_shapes=[pltpu.VMEM(w_dn.shape, BF16), pltpu.VMEM(w_gu.shape, BF16), pltpu.VMEM(w_out.shape, BF16),
                        pltpu.SemaphoreType.DMA((3,))],
        compiler_params=_params("arbitrary"),
    )(dh2, ffo, g_po, gu, w_dn, w_gu, h1, g_pf, mix, g_pm, w_out, yab, yc, gg)


def _attn_bwd(qkv, kv_t, dyc, yc, lse, bias_t, after):
    T = qkv.shape[0]
    NP = qkv.shape[1] // (3 * LANES)
    NB = T // TB
    NS = TB // SUB
    scale = HEAD_DIM ** -0.5

    def body(q_ref, kc_ref, kp_ref, vc_ref, vp_ref, ktc_ref, ktp_ref, do_ref, o_ref, lse_ref, b_ref, after_ref,
             dq_ref, dk_ref, dv_ref, db_ref, dk_carry, dv_carry, dkw, dvw):
        step = pl.program_id(1)
        i = NB - 1 - step
        _acc_init(step, dk_carry, dv_carry, db_ref)

        def compute(first_block):
            q = q_ref[...] * scale
            kwin = jnp.concatenate([kp_ref[...], kc_ref[...]], axis=0)
            vwin = jnp.concatenate([vp_ref[...], vc_ref[...]], axis=0)
            kt = jnp.concatenate([ktp_ref[...], ktc_ref[...]], axis=1)
            do = do_ref[...].astype(F32)
            dob = do.astype(BF16)
            prod = do * o_ref[...]
            first = lax.broadcasted_iota(jnp.int32, (1, LANES), 1) < HEAD_DIM
            ones = jnp.ones((8, LANES), F32)
            row_sums = lambda x: lax.dot_general(ones, x, (((1,), (1,)), ((), ())), precision=lax.Precision.HIGHEST,
                                                 preferred_element_type=F32)
            deltas = (row_sums(jnp.where(first, prod, 0.0)), row_sums(jnp.where(first, 0.0, prod)))
            written = set()
            for s in range(NS):
                rows = slice(s * SUB, (s + 1) * SUB)
                tiles = _key_tiles(s, first_block)
                keys = slice(tiles[0] * SUB, (tiles[-1] + 1) * SUB)
                brows = slice((tiles[0] - s) * SUB, (tiles[-1] - s + 1) * SUB)
                q2, do2 = _both_heads(q[rows]), _both_heads(dob[rows])
                lse = jnp.concatenate([lse_ref[0, 0, s:s + 1, :], lse_ref[0, 0, NS + s:NS + s + 1, :]], axis=1)
                delta = jnp.concatenate([deltas[0][0:1, rows], deltas[1][0:1, rows]], axis=1)
                p = jnp.exp(_dot_nt(kwin[keys], q2) + b_ref[0, brows, :] - lse)
                ds = p * (_dot_nt(vwin[keys], do2) - delta)
                db_ref[0, brows, :] += ds
                dsb = ds.astype(BF16)
                dk_c = _dot(dsb, q2)
                dv_c = _dot(p.astype(BF16), do2)
                dq_ref[rows, :] = (_own_head_rows(_dot(kt[:, keys], dsb)).T * scale).astype(BF16)
                for n, t in enumerate(tiles):
                    win, loc = slice(t * SUB, (t + 1) * SUB), slice(n * SUB, (n + 1) * SUB)
                    if t in written:
                        dkw[win, :] += dk_c[loc]
                        dvw[win, :] += dv_c[loc]
                    else:
                        dkw[win, :] = dk_c[loc]
                        dvw[win, :] = dv_c[loc]
                        written.add(t)
            dk_ref[...] = (dkw[TB:, :] + dk_carry[...]).astype(BF16)
            dv_ref[...] = (dvw[TB:, :] + dv_carry[...]).astype(BF16)
            if not first_block:
                dk_carry[...] = dkw[:TB, :]
                dv_carry[...] = dvw[:TB, :]

        pl.when(i == 0)(functools.partial(compute, True))
        pl.when(i > 0)(functools.partial(compute, False))

    blk = lambda s: NB - 1 - s
    prev = lambda s: jnp.maximum(NB - 2 - s, 0)
    rows = lambda which, off: pl.BlockSpec((TB, LANES), lambda p, s: (which(s), off + p))
    out = jax.ShapeDtypeStruct((T, NP * LANES), BF16)
    strip = pl.BlockSpec((1, BANDW, 2 * SUB), lambda p, s: (p, 0, 0))
    return pl.pallas_call(
        body, name="attn_bwd", grid=(NP, NB),
        in_specs=[rows(blk, 0), rows(blk, NP), rows(prev, NP), rows(blk, 2 * NP), rows(prev, 2 * NP),
                  pl.BlockSpec((LANES, TB), lambda p, s: (p, blk(s))), pl.BlockSpec((LANES, TB), lambda p, s: (p, prev(s))),
                  rows(blk, 0), rows(blk, 0), pl.BlockSpec((1, 1, 8, LANES), lambda p, s: (p, blk(s), 0, 0)), strip, ANY],
        out_specs=[rows(blk, 0), rows(blk, 0), rows(blk, 0), strip],
        out_shape=[out, out, out, jax.ShapeDtypeStruct((NP, BANDW, 2 * SUB), F32)],
        scratch_shapes=[pltpu.VMEM((TB, LANES), F32), pltpu.VMEM((TB, LANES), F32),
                        pltpu.VMEM((2 * TB, LANES), F32), pltpu.VMEM((2 * TB, LANES), F32)],
        compiler_params=_params("arbitrary", "arbitrary"),
    )(qkv, qkv, qkv, qkv, qkv, kv_t, kv_t, dyc, yc, lse, bias_t, after)


def _inproj_mixers_bwd(pa, dyab, cw, wbd, ps, dqkv, w, h, g, dh1):
    T, D = pa.shape
    DC = D // 4
    NB = T // TB
    N = TB + HALO
    widths = [p.shape[1] for p in dqkv]
    n = len(dqkv)

    def body(*refs):
        pa_ref, halo_ref, dy_ref, cw_ref, wbd_ref, ps_ref = refs[:6]
        parts = refs[6:6 + n]
        w_ref, h_ref, g_ref, dh1_ref, dpa_ref, dh_ref, dcw_ref, dwbd_ref, dps_ref, dg_ref, dc_carry, e_carry = refs[6 + n:]
        step = pl.program_id(0)
        i = NB - 1 - step
        _acc_init(step, dcw_ref, dwbd_ref, dps_ref, dg_ref, dc_carry, e_carry)
        x = pa_ref[...]
        hl = jnp.where(i > 0, halo_ref[...], 0.0)
        gb, gc, u, pu = (x[:, k * DC:(k + 1) * DC] for k in range(4))
        dy = dy_ref[...].astype(F32)
        dya, dyb = dy[:, :DC], dy[:, DC:]
        cwv = cw_ref[...]
        z = gc * u
        z1, z2 = _conv_taps(z, hl[:, DC:2 * DC] * hl[:, 2 * DC:3 * DC])
        dgb = dya * (cwv[2:3] * z + cwv[1:2] * z1 + cwv[0:1] * z2)
        dconv = dya * gb
        dcw_ref[0:1, :] += jnp.sum(dconv * z2, axis=0, keepdims=True)
        dcw_ref[1:2, :] += jnp.sum(dconv * z1, axis=0, keepdims=True)
        dcw_ref[2:3, :] += jnp.sum(dconv * z, axis=0, keepdims=True)
        ext = jnp.concatenate([dconv, dc_carry[...]], axis=0)
        dz = cwv[2:3] * dconv + cwv[1:2] * pltpu.roll(ext, N - 1, 0)[:TB] + cwv[0:1] * pltpu.roll(ext, N - 2, 0)[:TB]
        dc_carry[...] = dconv[:HALO]
        cnt = _pool_count(i, DC)
        d = _pool_d(pu, hl[:, 3 * DC:], cnt).astype(BF16)
        psv = ps_ref[...]
        wb = wbd_ref[...]
        dps_ref[0:1, :] += jnp.sum(dyb * _dot(d, wb), axis=0, keepdims=True)
        dys = (dyb * psv).astype(BF16)
        dwbd_ref[...] += _dot_tn(d, dys)
        dd = _dot_nt(dys, wb)
        e = dd / cnt
        ext = jnp.concatenate([e, e_carry[...]], axis=0)
        a2 = ext + pltpu.roll(ext, N - 1, 0)
        a4 = a2 + pltpu.roll(a2, N - 2, 0)
        a8 = a4 + pltpu.roll(a4, N - 4, 0)
        a16 = a8 + pltpu.roll(a8, N - 8, 0)
        dpu = _pick_group([a2, a4, a8, a16], DC)[:TB] - dd
        e_carry[...] = e[:HALO]
        dpa = jnp.concatenate([dgb, dz * u, dz * gc, dpu], axis=1).astype(BF16)
        dpa_ref[...] = dpa
        dxn, off = _dot_nt(dpa, w_ref[:, :D]), D
        for p_ref, wd in zip(parts, widths):
            dxn = dxn + _dot_nt(p_ref[...], w_ref[:, off:off + wd])
            off += wd
        dx, dg = _rms_bwd(dxn, h_ref[...], g_ref[...])
        dg_ref[0:1, :] += dg
        dh_ref[...] = dh1_ref[...] + dx

    blk = lambda m: pl.BlockSpec((TB, m), lambda s: (NB - 1 - s, 0))
    return pl.pallas_call(
        body, name="inproj_mixers_bwd", grid=(NB,),
        in_specs=[blk(D), pl.BlockSpec((HALO, D), lambda s: (jnp.maximum((NB - 1 - s) * (TB // HALO) - 1, 0), 0)),
                  blk(2 * DC), _full((8, DC)), _full((DC, DC)), _full((1, DC))] + [blk(wd) for wd in widths]
                 + [_full(w.shape), blk(D), _full((1, D)), blk(D)],
        out_specs=[blk(D), blk(D), _full((8, DC)), _full((DC, DC)), _full((8, DC)), _full((8, D))],
        out_shape=[jax.ShapeDtypeStruct((T, D), BF16), jax.ShapeDtypeStruct((T, D), F32), jax.ShapeDtypeStruct((8, DC), F32),
                   jax.ShapeDtypeStruct((DC, DC), F32), jax.ShapeDtypeStruct((8, DC), F32), jax.ShapeDtypeStruct((8, D), F32)],
        scratch_shapes=[pltpu.VMEM((HALO, DC), F32), pltpu.VMEM((HALO, DC), F32)],
        compiler_params=_params("arbitrary"),
    )(pa, pa, dyab, cw, wbd, ps, *dqkv, w, h, g, dh1)


def _wgrad(a, b, tk, tn, name):
    T, K = a.shape
    N = b.shape[1]

    def body(a_ref, b_ref, o_ref):
        _acc_init(pl.program_id(2), o_ref)
        o_ref[...] += _dot_tn(a_ref[...], b_ref[...])

    return pl.pallas_call(
        body, name=name, grid=(K // tk, N // tn, T // TBW),
        in_specs=[pl.BlockSpec((TBW, tk), lambda k, n, t: (t, k)), pl.BlockSpec((TBW, tn), lambda k, n, t: (t, n))],
        out_specs=pl.BlockSpec((tk, tn), lambda k, n, t: (k, n)),
        out_shape=jax.ShapeDtypeStruct((K, N), F32),
        compiler_params=_params("parallel", "parallel", "arbitrary"),
    )(a, b)


def _wgrad_concat(a, bs, name):
    T, K = a.shape
    widths = [b.shape[1] for b in bs]
    n = len(bs)

    def body(*refs):
        a_ref, b_refs, o_ref = refs[0], refs[1:1 + n], refs[1 + n]
        _acc_init(pl.program_id(0), o_ref)
        av, off = a_ref[...], 0
        for b_ref, wd in zip(b_refs, widths):
            o_ref[:, off:off + wd] += _dot_tn(av, b_ref[...])
            off += wd

    row = lambda m: pl.BlockSpec((TBW, m), lambda t: (t, 0))
    return pl.pallas_call(
        body, name=name, grid=(T // TBW,),
        in_specs=[row(K)] + [row(wd) for wd in widths],
        out_specs=_full((K, sum(widths))),
        out_shape=jax.ShapeDtypeStruct((K, sum(widths)), F32),
        compiler_params=_params("arbitrary"),
    )(a, *bs)


def _adamw(w, g, m, v, tr, name):
    R, C = w.shape

    def body(w_ref, g_ref, m_ref, v_ref, d_ref, nm_ref, nv_ref):
        gv = g_ref[...]
        nm = ADAM_B1 * m_ref[...] + (1.0 - ADAM_B1) * gv
        nv = ADAM_B2 * v_ref[...] + (1.0 - ADAM_B2) * (gv * gv)
        m_hat = nm / (1.0 - ADAM_B1 ** ADAM_STEP)
        v_hat = nv / (1.0 - ADAM_B2 ** ADAM_STEP)
        d_ref[...] = -ADAM_LR * (m_hat / (jnp.sqrt(v_hat) + ADAM_EPS) + ADAM_WD * w_ref[...])
        nm_ref[...] = nm
        nv_ref[...] = nv

    blk = pl.BlockSpec((tr, C), lambda i: (i, 0))
    out = jax.ShapeDtypeStruct((R, C), F32)
    return pl.pallas_call(
        body, name=name, grid=(R // tr,), in_specs=[blk] * 4, out_specs=[blk] * 3, out_shape=[out] * 3,
        compiler_params=_params("parallel"),
    )(w, g, m, v)


def _sum_slots(x, tr, name):
    n, R, C = x.shape

    def body(x_ref, o_ref):
        acc = x_ref[0]
        for k in range(1, n):
            acc = acc + x_ref[k]
        o_ref[...] = acc

    return pl.pallas_call(
        body, name=name, grid=(R // tr,),
        in_specs=[pl.BlockSpec((n, tr, C), lambda i: (0, i, 0))],
        out_specs=pl.BlockSpec((tr, C), lambda i: (i, 0)),
        out_shape=jax.ShapeDtypeStruct((R, C), F32),
        compiler_params=_params("parallel"),
    )(x)


def _add_half(view, other, c, name):
    if view.ndim == 3:
        _, R2, N = view.shape
        tr = 128
        grid = (R2 // tr,)
        in_specs = [pl.BlockSpec((1, tr, N), lambda i, c_ref: (c_ref[0], i, 0)), pl.BlockSpec((1, tr, N), lambda i, c_ref: (0, i, 0))]
        out_spec = pl.BlockSpec((tr, N), lambda i, c_ref: (i, 0))
        out_shape = jax.ShapeDtypeStruct((R2, N), BF16)

        def body(c_ref, a_ref, b_ref, o_ref):
            o_ref[...] = (a_ref[0] + b_ref[0]).astype(BF16)
    else:
        S, _, R2, C = view.shape
        grid = (S,)
        in_specs = [pl.BlockSpec((1, 1, R2, C), lambda s, c_ref: (s, c_ref[0], 0, 0)), pl.BlockSpec((1, 1, R2, C), lambda s, c_ref: (s, 0, 0, 0))]
        out_spec = pl.BlockSpec((1, R2, C), lambda s, c_ref: (s, 0, 0))
        out_shape = jax.ShapeDtypeStruct((S, R2, C), BF16)

        def body(c_ref, a_ref, b_ref, o_ref):
            o_ref[0] = (a_ref[0, 0] + b_ref[0, 0]).astype(BF16)

    return pl.pallas_call(
        body, name=name,
        grid_spec=pltpu.PrefetchScalarGridSpec(num_scalar_prefetch=1, grid=grid, in_specs=in_specs, out_specs=out_spec),
        out_shape=out_shape, compiler_params=_params("parallel"),
    )(c, view, other)


def _sum_chips(slots, part, kind, place, name):
    _, R2, C = slots.shape
    tr = min(R2, 128) if kind == "col" else R2

    def body(p_ref, s_ref, own_ref, o_ref):
        me = p_ref[0]
        own = own_ref[...] if kind == "col" else own_ref[0]
        acc = None
        for k in range(4):
            term = jnp.where(me == k, own, s_ref[k]).astype(F32)
            acc = term if acc is None else acc + term
        o_ref[0] = acc

    own_spec = (pl.BlockSpec((tr, C), lambda i, p: (i, p[0])) if kind == "col"
                else pl.BlockSpec((1, R2, C), lambda i, p: (p[0], 0, 0)))
    return pl.pallas_call(
        body, name=name,
        grid_spec=pltpu.PrefetchScalarGridSpec(
            num_scalar_prefetch=1, grid=(R2 // tr,),
            in_specs=[pl.BlockSpec((4, tr, C), lambda i, p: (0, i, 0)), own_spec],
            out_specs=pl.BlockSpec((1, tr, C), lambda i, p: (p[1], i, 0))),
        out_shape=jax.ShapeDtypeStruct((2, R2, C), F32), compiler_params=_params("parallel"),
    )(place, slots, part)


def _place():
    x, y, c = lax.axis_index("x"), lax.axis_index("y"), lax.axis_index("c")
    chips = [(1 - x, y), (x, 1 - y), (1 - x, 1 - y)]
    return x, y, c, 2 * x + y, chips


def _remote(src, dst, send_sems, recv_sems, k, to):
    return pltpu.make_async_remote_copy(src_ref=src, dst_ref=dst, send_sem=send_sems.at[k], recv_sem=recv_sems.at[k],
                                        device_id=to, device_id_type=MESH)


def _full_shape(shard_shape, kind):
    *lead, R, C = shard_shape
    return (*lead, R, 4 * C) if kind == "col" else (*lead, 4 * R, C)


def _slab(full_ref, shard_rc, kind, chip, half=None):
    R, C = shard_rc
    lead = (slice(None),) * (len(full_ref.shape) - 2)
    if kind == "col":
        rows = pl.ds(0, R) if half is None else pl.ds(half * (R // 2), R // 2)
        return full_ref.at[(*lead, rows, pl.ds(chip * C, C))]
    rows = pl.ds(chip * R, R) if half is None else pl.ds(chip * R + half * (R // 2), R // 2)
    return full_ref.at[(*lead, rows, slice(None))]


def _row_half(ref, half):
    R = ref.shape[-2]
    lead = (slice(None),) * (len(ref.shape) - 2)
    return ref.at[(*lead, pl.ds(half * (R // 2), R // 2), slice(None))]


def _allgather_weights(shards, kinds, after):
    n = len(shards)
    full_shapes = [_full_shape(s.shape, k) for s, k in zip(shards, kinds)]

    def body(*refs):
        ins, outs, (send_sems, recv_sems) = refs[:n], refs[n + 1:2 * n + 1], refs[2 * n + 1:]
        x, y, c, me, chips = _place()
        slab = lambda a, chip, half=None: _slab(outs[a], shards[a].shape[-2:], kinds[a], chip, half)
        my_half = lambda a: _row_half(ins[a], c)
        own = [_remote(ins[a], slab(a, me), send_sems, recv_sems, 6 * n + a, (x, y, 1 - c)) for a in range(n)]
        first = [_remote(my_half(a), slab(a, me, c), send_sems, recv_sems, j * n + a, (*chip, c))
                 for j, chip in enumerate(chips) for a in range(n)]
        for cp in first + own:
            cp.start()
        passed = []
        for j, chip in enumerate(chips):
            src = 2 * chip[0] + chip[1]
            for a in range(n):
                _remote(my_half(a), slab(a, src, c), send_sems, recv_sems, j * n + a, (x, y, c)).wait_recv()
                cp = _remote(slab(a, src, c), slab(a, src, c), send_sems, recv_sems, (3 + j) * n + a, (x, y, 1 - c))
                cp.start()
                passed.append(cp)
        for j, chip in enumerate(chips):
            src = 2 * chip[0] + chip[1]
            for a in range(n):
                _remote(my_half(a), slab(a, src, 1 - c), send_sems, recv_sems, (3 + j) * n + a, (x, y, c)).wait_recv()
        for cp in own:
            cp.wait_recv()
        for cp in first + passed + own:
            cp.wait_send()

    return pl.pallas_call(
        body, name="allgather_weights",
        in_specs=[ANY] * (n + 1), out_specs=[ANY] * n,
        out_shape=[jax.ShapeDtypeStruct(s, BF16) for s in full_shapes],
        scratch_shapes=[pltpu.SemaphoreType.DMA((7 * n,)), pltpu.SemaphoreType.DMA((7 * n,))],
        compiler_params=pltpu.CompilerParams(has_side_effects=True),
    )(*shards, after)


HBM = pl.BlockSpec(memory_space=pltpu.HBM)
SEM = pl.BlockSpec(memory_space=pltpu.SEMAPHORE)
DATAFLOW = pltpu.SideEffectType.DATAFLOW_SIDE_EFFECTING


def _split_start(name, bufs, plan, ncopies, after=None):
    nb = len(bufs)
    nin = nb + (after is not None)

    def body(*refs):
        send_sems, recv_sems, token = refs[nin], refs[nin + 1], refs[-1]
        for k, (src, dst, to) in enumerate(plan(refs[:nb])):
            _remote(src, dst, send_sems, recv_sems, k, to).start()
        token[...] = jnp.zeros_like(token)

    out = pl.pallas_call(
        body, name=name,
        out_shape=(pltpu.SemaphoreType.DMA((ncopies,)), pltpu.SemaphoreType.DMA((ncopies,)),
                   *[pltpu.HBM(b.shape, b.dtype) for b in bufs], jax.ShapeDtypeStruct((8, LANES), F32)),
        in_specs=[HBM] * nb + [ANY] * (nin - nb), out_specs=(SEM, SEM, *[HBM] * nb, pl.BlockSpec(memory_space=pltpu.VMEM)),
        input_output_aliases={i: 2 + i for i in range(nb)},
        compiler_params=pltpu.CompilerParams(has_side_effects=DATAFLOW),
    )(*[pltpu.with_memory_space_constraint(b, pltpu.HBM) for b in bufs], *([] if after is None else [after]))
    return out[0], out[1], list(out[2:2 + nb]), out[-1]


def _split_wait(name, send_sems, recv_sems, bufs, plan, after):
    nb = len(bufs)

    def body(*refs):
        s_sems, r_sems = refs[nb], refs[nb + 1]
        for k, (src, dst, to) in enumerate(plan(refs[:nb])):
            cp = _remote(src, dst, s_sems, r_sems, k, to)
            cp.wait_send()
            cp.wait_recv()

    return pl.pallas_call(
        body, name=name,
        out_shape=tuple(pltpu.HBM(b.shape, b.dtype) for b in bufs),
        in_specs=[HBM] * nb + [SEM, SEM, ANY], out_specs=tuple([HBM] * nb),
        input_output_aliases={i: i for i in range(nb)},
        compiler_params=pltpu.CompilerParams(has_side_effects=DATAFLOW),
    )(*bufs, send_sems, recv_sems, after)


def _gather_plan(n, shard_rcs, kinds):
    def plan(refs):
        x, y, c, me, chips = _place()
        out = []
        for a in range(n):
            shard, full = refs[a], refs[n + a]
            out.append((shard, _slab(full, shard_rcs[a], kinds[a], me), (x, y, 1 - c)))
            for chip in chips:
                for cc in (c, 1 - c):
                    out.append((_row_half(shard, c), _slab(full, shard_rcs[a], kinds[a], me, c), (*chip, cc)))
        return out
    return plan


def _exchange_plan(n, kinds):
    def plan(refs):
        x, y, c, me, chips = _place()
        out = []
        for a in range(n):
            part, slots = refs[a], refs[n + a]
            C = slots.shape[2]
            for chip in chips:
                dst = 2 * chip[0] + chip[1]
                src = part.at[:, pl.ds(dst * C, C)] if kinds[a] == "col" else part.at[dst]
                out.append((src, slots.at[me], (*chip, c)))
        return out
    return plan


def _swap_plan(ndims):
    n = len(ndims)

    def plan(refs):
        x, y, c, _, _ = _place()
        out = []
        for a in range(n):
            src = refs[a].at[pl.ds(1 - c, 1)] if ndims[a] == 3 else refs[a].at[:, pl.ds(1 - c, 1)]
            out.append((src, refs[n + a], (x, y, 1 - c)))
        return out
    return plan


def _join_halves(bufs):
    n = len(bufs)

    def body(*refs):
        outs, (send_sems, recv_sems) = refs[n:2 * n], refs[2 * n:]
        x, y, c, _, _ = _place()
        sends = [_remote(outs[a].at[c], outs[a].at[c], send_sems, recv_sems, a, (x, y, 1 - c)) for a in range(n)]
        for cp in sends:
            cp.start()
        for a in range(n):
            _remote(outs[a].at[c], outs[a].at[1 - c], send_sems, recv_sems, a, (x, y, c)).wait_recv()
        for cp in sends:
            cp.wait_send()

    return pl.pallas_call(
        body, name="grad_join_halves", in_specs=[ANY] * n, out_specs=[ANY] * n,
        out_shape=[jax.ShapeDtypeStruct(b.shape, F32) for b in bufs],
        input_output_aliases={a: a for a in range(n)},
        scratch_shapes=[pltpu.SemaphoreType.DMA((n,)), pltpu.SemaphoreType.DMA((n,))],
        compiler_params=pltpu.CompilerParams(has_side_effects=True),
    )(*bufs)


def _allgather_small(block):
    M, N = block.shape

    def body(x_ref, out_ref, send_sems, recv_sems, local_sem):
        x, y, c, _, chips = _place()
        me, sibling = (x, y, c), (x, y, 1 - c)

        def rows(px, py, pc):
            return out_ref.at[pl.ds((4 * px + 2 * py + pc) * M, M), :]

        def copy(k, blk, to, src=None):
            return _remote(rows(*blk) if src is None else src, rows(*blk), send_sems, recv_sems, k, to)

        mine = pltpu.make_async_copy(x_ref, rows(*me), local_sem)
        mine.start()
        first = [copy(0, me, sibling, src=x_ref)] + [copy(1 + j, me, (*chip, c), src=x_ref) for j, chip in enumerate(chips)]
        for cp in first:
            cp.start()
        passed = [copy(4 + j, (*chip, c), sibling) for j, chip in enumerate(chips)]
        for j, chip in enumerate(chips):
            copy(1 + j, (*chip, c), me).wait_recv()
            passed[j].start()
        copy(0, sibling, me).wait_recv()
        for j, chip in enumerate(chips):
            copy(4 + j, (*chip, 1 - c), me).wait_recv()
        for cp in first + passed:
            cp.wait_send()
        mine.wait()

    vm = pl.BlockSpec(memory_space=pltpu.VMEM)
    return pl.pallas_call(
        body, name="allgather_small", in_specs=[vm], out_specs=vm,
        out_shape=jax.ShapeDtypeStruct((8 * M, N), F32),
        scratch_shapes=[pltpu.SemaphoreType.DMA((7,)), pltpu.SemaphoreType.DMA((7,)), pltpu.SemaphoreType.DMA],
        compiler_params=pltpu.CompilerParams(has_side_effects=True, vmem_limit_bytes=VMEM_LIMIT),
    )(block)


def _pack(arrays):
    flat = jnp.concatenate([a.reshape(-1) for a in arrays])
    pad = (-flat.shape[0]) % (8 * LANES)
    return jnp.pad(flat, (0, pad)).reshape(-1, LANES)


def _unpack(packed, shapes):
    flat, out, off = packed.reshape(-1), [], 0
    for s in shapes:
        size = 1
        for d in s:
            size *= d
        out.append(flat[off:off + size].reshape(s))
        off += size
    return out


def _block_diag(pw):
    G, n, _ = pw.shape
    eye = jnp.eye(G, dtype=pw.dtype)
    return (eye[:, None, :, None] * pw[:, :, None, :]).reshape(G * n, G * n)


def _diag_blocks(m, G):
    n = m.shape[0] // G
    return jnp.stack([m[g * n:(g + 1) * n, g * n:(g + 1) * n] for g in range(G)])


def _pad_rows(a, rows):
    return jnp.pad(a, ((0, rows - a.shape[0]), (0, 0)))


def kernel(x, w_in, w_out, conv_w, pool_w, pool_scale, rel_bias, group_gain, pre_mix_g, post_mix_g, pre_ffn_g, post_ffn_g, w_gate_up, w_down, loss_target, m_w_in, m_w_out, m_conv_w, m_pool_w, m_pool_scale, m_rel_bias, m_group_gain, m_pre_mix_g, m_post_mix_g, m_pre_ffn_g, m_post_ffn_g, m_w_gate_up, m_w_down, v_w_in, v_w_out, v_conv_w, v_pool_w, v_pool_scale, v_rel_bias, v_group_gain, v_pre_mix_g, v_post_mix_g, v_pre_ffn_g, v_post_ffn_g, v_w_gate_up, v_w_down):
    L = w_in.shape[0]
    T, D = x.shape[1], x.shape[2]
    DC = D // 4
    NH = rel_bias.shape[1]
    NREL = rel_bias.shape[2]
    G = pool_w.shape[1]
    cs = conv_w.shape[2]
    assert TB == LEFT_CHUNKS * CHUNK and T % TB == 0 and D % (4 * LANES) == 0 and NH * HEAD_DIM == D // 2
    xi, yi, ci = lax.axis_index("x"), lax.axis_index("y"), lax.axis_index("c")
    chip = 2 * xi + yi

    kinds = ("col", "row", "col", "row")
    big = (w_in, w_out, w_gate_up, w_down)
    rcs = [w.shape[-2:] for w in big]
    conv_gathered = _allgather_small(_pack([conv_w]))
    conv_all = conv_gathered.reshape(8, -1)[:, :L * 3 * cs].reshape(4, 2, L, 3, cs)[:, 0]
    conv_full = jnp.moveaxis(conv_all, 0, 2).reshape(L, 3, 4 * cs)
    landing = lambda which: [lax.empty(_full_shape(rcs[k], kinds[k]), BF16) for k in which]
    wi0 = _allgather_weights([w_in[0:1].astype(BF16)], kinds[:1], conv_gathered)[0][0]
    rest_plan = _gather_plan(3, rcs[1:], kinds[1:])
    rest = _split_start("gather_start_0", [w[0].astype(BF16) for w in big[1:]] + landing((1, 2, 3)), rest_plan, 3 * 7, after=wi0)
    layer_plan = _gather_plan(4, rcs, kinds)
    full = [None] * L

    h = x[0]
    saved = []
    token = rest[3]
    for l in range(L):
        g_pre, g_pm, g_pf, g_po = (a[l][None] for a in (pre_mix_g, post_mix_g, pre_ffn_g, post_ffn_g))
        if 0 < l < L - 1:
            nxt = _split_start(f"gather_start_{l + 1}", [w[l + 1].astype(BF16) for w in big] + landing(range(4)), layer_plan, 4 * 7,
                               after=full[l][0])
            token = nxt[3]
        if l < L - 1:
            g_pre = g_pre + token[0:1, 0:1]
        gg, ps = group_gain[l][None], pool_scale[l][None]
        cw = _pad_rows(conv_full[l], 8)
        wbd = _block_diag(pool_w[l]).astype(BF16)
        bias = _bias_build(jnp.pad(rel_bias[l], ((0, 0), (0, RBP - NREL))))
        wi = wi0 if l == 0 else full[l][0]
        xn, pa, qkv, kv_t, yab = _inproj_mixers_fwd(h, g_pre, wi, cw, wbd, ps)
        yc, lse = _attn_fwd(qkv, kv_t, bias)
        if l == 0:
            full[0] = [wi0, *_split_wait("gather_wait_0", rest[0], rest[1], rest[2], rest_plan, yc)[3:6]]
            if L > 1:
                nxt = _split_start("gather_start_1", [w[1].astype(BF16) for w in big] + landing(range(4)), layer_plan, 4 * 7,
                                   after=full[0][1])
                g_pm = g_pm + nxt[3][0:1, 0:1]
        wi, wo, wgu, wdn = full[l]
        y, mix, h1, hn, gu, ff, ffo, h2 = _layer_tail_fwd(yab, yc, gg, wo, h, g_pm, g_pf, wgu, wdn, g_po)
        saved.append(dict(h=h, xn=xn, pa=pa, qkv=qkv, kv_t=kv_t, lse=lse, yab=yab, yc=yc, y=y, mix=mix, h1=h1, hn=hn, gu=gu, ff=ff, ffo=ffo,
                          cw=cw, wbd=wbd, bias=bias, ps=ps, gg=gg, g_pre=g_pre, g_pm=g_pm, g_pf=g_pf, g_po=g_po))
        h = h2
        if l + 1 < L:
            full[l + 1] = list(_split_wait(f"gather_wait_{l + 1}", nxt[0], nxt[1], nxt[2], layer_plan, h2)[4:8])

    dh, loss_tile = _loss_grad(h, loss_target[0])
    loss = lax.psum(loss_tile[0, 0], ("x", "y", "c"))

    xplan = _exchange_plan(4, kinds)
    splan = _swap_plan([3, 4, 3, 4])
    place = jnp.stack([chip, ci]).astype(jnp.int32)
    cvec = ci.reshape(1).astype(jnp.int32)
    small_grads = [None] * L
    shard_grads = [None] * L
    dbiases = [None] * L

    def start_exchange(lp, swap):
        thru = _split_wait(f"grad_swap_wait_{lp}", swap[0], swap[1], swap[2], splan, swap[4])
        chip_sums = [_add_half(v, t, cvec, "grad_add_half") for v, t in zip(thru[:4], thru[4:])]
        slots = [lax.empty((4, p.shape[0], p.shape[1] // 4) if k == "col" else p.shape, BF16) for p, k in zip(chip_sums, kinds)]
        ssem, rsem, bufs, tok = _split_start(f"grad_exchange_start_{lp}", chip_sums + slots, xplan, 4 * 3)
        return (lp, ssem, rsem, bufs), tok

    def finish_exchange(pending, after):
        lp, ssem, rsem, thru = pending
        landed = _split_wait(f"grad_exchange_wait_{lp}", ssem, rsem, thru, xplan, after)
        bufs = [_sum_chips(landed[4 + a], landed[a], kinds[a], place, "grad_sum_chips") for a in range(4)]
        shard_grads[lp] = [j.reshape(2 * j.shape[1], j.shape[2]) for j in _join_halves(bufs)]

    swap = None
    exchange = None
    for l in reversed(range(L)):
        s = saved[l]
        wi, wo, wgu, wdn = full[l]
        g_po = s["g_po"] if swap is None else s["g_po"] + swap[3][0:1, 0:1]
        dffo, dgu, dh1, dmix, dyab, dyc, dg_po, dg_pf, dg_pm, dgg = _layer_tail_bwd(
            dh, s["ffo"], g_po, s["gu"], wdn, wgu, s["h1"], s["g_pf"], s["mix"], s["g_pm"], wo, s["yab"], s["yc"], s["gg"])
        after = dyc
        if swap is not None:
            started, after = start_exchange(l + 1, (*swap[:4], dh1))
            if exchange is not None:
                finish_exchange(exchange, after)
            exchange = started
        dq, dk, dv, dbiases[l] = _attn_bwd(s["qkv"], s["kv_t"], dyc, s["yc"], s["lse"], s["bias"], after)
        dpa, dh, dcw, dwbd, dps, dg_pre = _inproj_mixers_bwd(s["pa"], dyab, s["cw"], s["wbd"], s["ps"], [dq, dk, dv], wi,
                                                             s["h"], s["g_pre"], dh1)
        dparts = [dpa, dq, dk, dv]
        F2, DFF = s["gu"].shape[1], s["ff"].shape[1]
        grads = [_wgrad_concat(s["xn"], dparts, "wgrad_in"),
                 _wgrad(s["y"], dmix, D, D, "wgrad_out"),
                 _wgrad(s["hn"], dgu, D, F2 // 4, "wgrad_gate_up"),
                 _wgrad(s["ff"], dffo, DFF // 2, D, "wgrad_down")]
        small_grads[l] = [dcw[:3], _diag_blocks(dwbd, G), dps[0], None, dgg[0], dg_pre[0], dg_pm[0], dg_pf[0], dg_po[0]]
        views = [g.reshape(2, g.shape[0] // 2, g.shape[1]) if k == "col" else g.reshape(4, 2, g.shape[0] // 8, g.shape[1])
                 for g, k in zip(grads, kinds)]
        lands = [lax.empty((1,) + v.shape[1:] if v.ndim == 3 else (v.shape[0], 1) + v.shape[2:], F32) for v in views]
        swap = _split_start(f"grad_swap_start_{l}", views + lands, splan, 4)
    last, tok = start_exchange(0, (*swap[:4], swap[3]))
    if exchange is not None:
        finish_exchange(exchange, tok)
    for l in range(L):
        small_grads[l][3] = _bias_fold(dbiases[l], tok)[:NH, :NREL]

    names_shapes = [(L, 3, 4 * cs), pool_w.shape, pool_scale.shape, rel_bias.shape, group_gain.shape,
                    pre_mix_g.shape, post_mix_g.shape, pre_ffn_g.shape, post_ffn_g.shape]
    small_stacked = [jnp.stack([small_grads[l][k] for l in range(L)]) for k in range(len(names_shapes))]
    packed = _pack(small_stacked)
    M = packed.shape[0]
    total = _sum_slots(_allgather_small(packed).reshape(8, M, LANES), M, "small_sum_devices")
    g_small = _unpack(total, names_shapes)
    g_small[0] = lax.dynamic_slice_in_dim(g_small[0], chip * cs, cs, axis=2)
    finish_exchange(last, total)
    g_big = [jnp.stack([shard_grads[l][k] for l in range(L)]) for k in range(4)]

    def adam_big(w, g, m, v, name):
        shp = w.shape
        two = lambda a: a.reshape(shp[0] * shp[1], shp[2])
        return [o.reshape(shp) for o in _adamw(two(w), two(g), two(m), two(v), 256, name)]

    upd_in = adam_big(w_in, g_big[0], m_w_in, v_w_in, "adamw_in")
    upd_out = adam_big(w_out, g_big[1], m_w_out, v_w_out, "adamw_out")
    upd_gu = adam_big(w_gate_up, g_big[2], m_w_gate_up, v_w_gate_up, "adamw_gate_up")
    upd_dn = adam_big(w_down, g_big[3], m_w_down, v_w_down, "adamw_down")

    small_w = [conv_w, pool_w, pool_scale, rel_bias, group_gain, pre_mix_g, post_mix_g, pre_ffn_g, post_ffn_g]
    small_m = [m_conv_w, m_pool_w, m_pool_scale, m_rel_bias, m_group_gain, m_pre_mix_g, m_post_mix_g, m_pre_ffn_g, m_post_ffn_g]
    small_v = [v_conv_w, v_pool_w, v_pool_scale, v_rel_bias, v_group_gain, v_pre_mix_g, v_post_mix_g, v_pre_ffn_g, v_post_ffn_g]
    pw_, pg_, pm_, pv_ = _pack(small_w), _pack(g_small), _pack(small_m), _pack(small_v)
    shapes = [w.shape for w in small_w]
    upd_small = [_unpack(o, shapes) for o in _adamw(pw_, pg_, pm_, pv_, pw_.shape[0], "adamw_small")]

    def ordered(big4, small9):
        return [big4[0], big4[1], *small9, big4[2], big4[3]]

    grads = ordered(g_big, g_small)
    outs = [ordered([upd_in[k], upd_out[k], upd_gu[k], upd_dn[k]], upd_small[k]) for k in range(3)]
    return (loss, dh[None], *grads, *outs[0], *outs[1], *outs[2])
```

```python
import functools

import jax
import jax.numpy as jnp
from jax import lax
from jax.experimental import pallas as pl
from jax.experimental.pallas import tpu as pltpu

F32, BF16 = jnp.float32, jnp.bfloat16
EPS = 1e-6
CHUNK = 64
LEFT_CHUNKS = 8
REL_CLIP = 128
HEAD_DIM = 64
LANES = 128
POOL_WINDOWS = (2, 4, 8, 16)
HALO = 16
TB = LEFT_CHUNKS * CHUNK
TBF = 256
TBW = 1024
BAND = (LEFT_CHUNKS + 1) * CHUNK
SUB = 2 * CHUNK
BANDW = SUB + LEFT_CHUNKS * CHUNK
SKEW = 768
NEG = -1e30
RBP = 384
VMEM_LIMIT = 56 * 1024 * 1024
ADAM_LR, ADAM_B1, ADAM_B2, ADAM_EPS, ADAM_WD, ADAM_STEP = 0.001, 0.9, 0.999, 1e-08, 0.01, 10
MESH = pl.DeviceIdType.MESH
ANY = pl.BlockSpec(memory_space=pl.ANY)


def _params(*sem):
    kw = dict(vmem_limit_bytes=VMEM_LIMIT)
    if sem:
        kw["dimension_semantics"] = sem
    return pltpu.CompilerParams(**kw)


def _dot(a, b):
    return jnp.dot(a, b, preferred_element_type=F32)


def _dot_nt(a, b):
    return lax.dot_general(a, b, (((1,), (1,)), ((), ())), preferred_element_type=F32)


def _dot_tn(a, b):
    return lax.dot_general(a, b, (((0,), (0,)), ((), ())), preferred_element_type=F32)


def _rms(x, g):
    r = lax.rsqrt(jnp.mean(x * x, axis=-1, keepdims=True) + EPS)
    return x * r * g


def _rms_bwd(dy, x, g):
    r = lax.rsqrt(jnp.mean(x * x, axis=-1, keepdims=True) + EPS)
    xh = x * r
    dxh = dy * g
    dx = r * (dxh - xh * jnp.mean(dxh * xh, axis=-1, keepdims=True))
    return dx, jnp.sum(dy * xh, axis=0, keepdims=True)


def _full(shape):
    return pl.BlockSpec(shape, lambda *_: (0,) * len(shape))


def _acc_init(step, *refs):
    @pl.when(step == 0)
    def _():
        for r in refs:
            r[...] = jnp.zeros_like(r)


def _inproj_mixers_fwd(h, g, w, cw, wbd, ps):
    T, D = h.shape
    NQ = w.shape[1] - D
    NKV = 2 * NQ // 3
    DC = D // 4

    def body(h_ref, g_ref, w_ref, cw_ref, wbd_ref, ps_ref, xn_ref, pa_ref, qkv_ref, kvt_ref, yab_ref, halo):
        i = pl.program_id(0)
        _acc_init(i, halo)
        xn = _rms(h_ref[...], g_ref[...]).astype(BF16)
        xn_ref[...] = xn
        x = _dot(xn, w_ref[:, :D])
        pa_ref[...] = x
        qkv = _dot(xn, w_ref[:, D:])
        qkv_ref[...] = qkv.astype(BF16)
        kvt_ref[...] = qkv[:, NQ - NKV:].T.astype(BF16)
        hl = halo[...]
        gb, gc, u, pu = (x[:, k * DC:(k + 1) * DC] for k in range(4))
        z = gc * u
        z1, z2 = _conv_taps(z, hl[:, DC:2 * DC] * hl[:, 2 * DC:3 * DC])
        cwv = cw_ref[...]
        ya = gb * (cwv[2:3] * z + cwv[1:2] * z1 + cwv[0:1] * z2)
        d = _pool_d(pu, hl[:, 3 * DC:], _pool_count(i, DC))
        yb = _dot(d.astype(BF16), wbd_ref[...]) * ps_ref[...]
        yab_ref[...] = jnp.concatenate([ya, yb], axis=1)
        halo[...] = x[TB - HALO:, :]

    row = lambda n: pl.BlockSpec((TB, n), lambda i: (i, 0))
    return pl.pallas_call(
        body, name="inproj_mixers_fwd", grid=(T // TB,),
        in_specs=[row(D), _full((1, D)), _full(w.shape), _full((8, DC)), _full((DC, DC)), _full((1, DC))],
        out_specs=[row(D), row(D), row(NQ), pl.BlockSpec((NKV, TB), lambda i: (0, i)), row(2 * DC)],
        out_shape=[jax.ShapeDtypeStruct((T, D), BF16), jax.ShapeDtypeStruct((T, D), F32), jax.ShapeDtypeStruct((T, NQ), BF16),
                   jax.ShapeDtypeStruct((NKV, T), BF16), jax.ShapeDtypeStruct((T, 2 * DC), F32)],
        scratch_shapes=[pltpu.VMEM((HALO, D), F32)],
        compiler_params=_params("arbitrary"),
    )(h, g, w, cw, wbd, ps)


def _lane_groups(n, vals):
    lane = lax.broadcasted_iota(jnp.int32, (1, n), 1)
    q = n // 4
    return jnp.where(lane < q, vals[0], jnp.where(lane < 2 * q, vals[1], jnp.where(lane < 3 * q, vals[2], vals[3]))).astype(F32)


def _pick_group(levels, n):
    lane = lax.broadcasted_iota(jnp.int32, (1, n), 1)
    q = n // 4
    return jnp.where(lane < q, levels[0], jnp.where(lane < 2 * q, levels[1], jnp.where(lane < 3 * q, levels[2], levels[3])))


def _pool_count(blk, n):
    t1 = (blk * TB + 1 + lax.broadcasted_iota(jnp.int32, (TB, 1), 0)).astype(F32)
    return jnp.minimum(t1, _lane_groups(n, POOL_WINDOWS))


def _pool_d(pu, pu_halo, cnt):
    e = jnp.concatenate([pu_halo, pu], axis=0)
    s2 = e + pltpu.roll(e, 1, 0)
    s4 = s2 + pltpu.roll(s2, 2, 0)
    s8 = s4 + pltpu.roll(s4, 4, 0)
    s16 = s8 + pltpu.roll(s8, 8, 0)
    num = _pick_group([s2, s4, s8, s16], pu.shape[1])[HALO:]
    return num / cnt - pu


def _conv_taps(z, z_halo):
    e = jnp.concatenate([z_halo, z], axis=0)
    return pltpu.roll(e, 1, 0)[HALO:], pltpu.roll(e, 2, 0)[HALO:]


def _bias_bins(shape, col_dim):
    j = lax.broadcasted_iota(jnp.int32, shape, col_dim)
    b = lax.broadcasted_iota(jnp.int32, shape, 1 - col_dim)
    d = jnp.where(j < BAND, j, j - SKEW)
    live = jnp.logical_or(j < BAND, j > SKEW - CHUNK)
    bins = jnp.minimum(TB - d, REL_CLIP) + REL_CLIP
    return jnp.where(jnp.logical_and(live, bins == b), 1.0, 0.0).astype(F32)


def _skew_rows(x, left):
    row = lax.broadcasted_iota(jnp.int32, x.shape, 0)
    for b in range(SUB.bit_length() - 1):
        x = jnp.where(((row >> b) & 1) == 1, pltpu.roll(x, SKEW - (1 << b) if left else 1 << b, 1), x)
    return x


def _bias_build(rb):
    H = rb.shape[0]

    def body(rb_ref, o_ref):
        v = jnp.dot(rb_ref[...], _bias_bins((RBP, SKEW), 1), precision=lax.Precision.HIGHEST, preferred_element_type=F32)
        qc = lax.broadcasted_iota(jnp.int32, (SUB, BANDW), 0) >> (CHUNK.bit_length() - 1)
        kc = lax.broadcasted_iota(jnp.int32, (SUB, BANDW), 1) >> (CHUNK.bit_length() - 1)
        in_band = jnp.logical_and(kc >= qc, kc <= qc + LEFT_CHUNKS)
        for h in range(H):
            x = _skew_rows(jnp.broadcast_to(v[h:h + 1], (SUB, SKEW)), left=False)
            o_ref[h // 2, :, (h % 2) * SUB:(h % 2 + 1) * SUB] = jnp.where(in_band, x[:, :BANDW], NEG).T

    return pl.pallas_call(
        body, name="bias_build", in_specs=[_full(rb.shape)], out_specs=_full((H // 2, BANDW, 2 * SUB)),
        out_shape=jax.ShapeDtypeStruct((H // 2, BANDW, 2 * SUB), F32), grid=(1,),
        compiler_params=_params("arbitrary"),
    )(rb)


def _bias_fold(db, after):
    H = 2 * db.shape[0]

    def body(db_ref, after_ref, o_ref, sums):
        sums[...] = jnp.zeros_like(sums)
        for h in range(H):
            x = db_ref[h // 2, :, (h % 2) * SUB:(h % 2 + 1) * SUB].T
            x = _skew_rows(jnp.concatenate([x, jnp.zeros((SUB, SKEW - BANDW), F32)], axis=1), left=True)
            sums[h:h + 1, :] = jnp.sum(x, axis=0, keepdims=True)
        o_ref[...] = jnp.dot(sums[...], _bias_bins((SKEW, RBP), 0), precision=lax.Precision.HIGHEST, preferred_element_type=F32)

    return pl.pallas_call(
        body, name="bias_fold", grid=(1,), in_specs=[_full(db.shape), ANY], out_specs=_full((8, RBP)),
        out_shape=jax.ShapeDtypeStruct((8, RBP), F32), scratch_shapes=[pltpu.VMEM((8, SKEW), F32)],
        compiler_params=_params("arbitrary"),
    )(db, after)


def _both_heads(x):
    first = lax.broadcasted_iota(jnp.int32, (1, LANES), 1) < HEAD_DIM
    return jnp.concatenate([jnp.where(first, x, 0), jnp.where(first, 0, x)], axis=0)


def _own_head_rows(x2):
    n = x2.shape[1] // 2
    first = lax.broadcasted_iota(jnp.int32, (LANES, 1), 0) < HEAD_DIM
    return jnp.where(first, x2[:, :n], x2[:, n:])


def _key_tiles(s, first_block):
    return [t for t in range(s, s + BANDW // SUB) if not (first_block and t < TB // SUB)]


def _attn_fwd(qkv, kv_t, bias_t):
    T = qkv.shape[0]
    NP = qkv.shape[1] // (3 * LANES)
    NB = T // TB
    NS = TB // SUB
    scale = HEAD_DIM ** -0.5

    def body(q_ref, kc_ref, kp_ref, vtc_ref, vtp_ref, b_ref, o_ref, lse_ref):
        i = pl.program_id(1)

        def compute(first_block):
            q = q_ref[...] * scale
            kwin = jnp.concatenate([kp_ref[...], kc_ref[...]], axis=0)
            vt = jnp.concatenate([vtp_ref[...], vtc_ref[...]], axis=1)
            for s in range(NS):
                rows = slice(s * SUB, (s + 1) * SUB)
                tiles = _key_tiles(s, first_block)
                keys = slice(tiles[0] * SUB, (tiles[-1] + 1) * SUB)
                brows = slice((tiles[0] - s) * SUB, (tiles[-1] - s + 1) * SUB)
                q2 = _both_heads(q[rows])
                halves = []
                for a in range(2):
                    st = _dot_nt(kwin[keys], q2[a * SUB:(a + 1) * SUB]) + b_ref[0, brows, a * SUB:(a + 1) * SUB]
                    m = jnp.max(st, axis=0, keepdims=True)
                    p = jnp.exp(st - m)
                    l = jnp.sum(p, axis=0, keepdims=True)
                    halves.append(_dot(vt[:, keys], p.astype(BF16)) * (1.0 / l))
                    lse_ref[0, 0, a * NS + s:a * NS + s + 1, :] = m + jnp.log(l)
                o_ref[rows, :] = _own_head_rows(jnp.concatenate(halves, axis=1)).T

        pl.when(i == 0)(functools.partial(compute, True))
        pl.when(i > 0)(functools.partial(compute, False))

    prev = lambda i: jnp.maximum(i - 1, 0)
    return pl.pallas_call(
        body, name="attn_fwd", grid=(NP, NB),
        in_specs=[pl.BlockSpec((TB, LANES), lambda p, i: (i, p)),
                  pl.BlockSpec((TB, LANES), lambda p, i: (i, NP + p)),
                  pl.BlockSpec((TB, LANES), lambda p, i: (prev(i), NP + p)),
                  pl.BlockSpec((LANES, TB), lambda p, i: (NP + p, i)),
                  pl.BlockSpec((LANES, TB), lambda p, i: (NP + p, prev(i))),
                  pl.BlockSpec((1, BANDW, 2 * SUB), lambda p, i: (p, 0, 0))],
        out_specs=[pl.BlockSpec((TB, LANES), lambda p, i: (i, p)),
                   pl.BlockSpec((1, 1, 8, LANES), lambda p, i: (p, i, 0, 0))],
        out_shape=[jax.ShapeDtypeStruct((T, NP * LANES), F32), jax.ShapeDtypeStruct((NP, NB, 8, LANES), F32)],
        compiler_params=_params("parallel", "parallel"),
    )(qkv, qkv, qkv, kv_t, kv_t, bias_t)


def _group_bounds(D):
    return ((0, D // 4), (D // 4, D // 2), (D // 2, D))


def _ff_tiles(dff):
    half = dff // 2
    return [(slice(a, a + half), slice(dff + a, dff + a + half)) for a in (0, half)]


def _load_resident(step, *pairs_and_sems):
    @pl.when(step == 0)
    def _():
        cps = [pltpu.make_async_copy(src, dst, sem) for src, dst, sem in pairs_and_sems]
        for cp in cps:
            cp.start()
        for cp in cps:
            cp.wait()


def _layer_tail_fwd(yab, yc, gg, w_out, h, g_pm, g_pf, w_gu, w_dn, g_po):
    T, D = h.shape
    F2 = w_gu.shape[1]
    DFF = F2 // 2

    def body(yab_ref, yc_ref, gg_ref, wo_hbm, h_ref, gpm_ref, gpf_ref, wgu_hbm, wdn_hbm, gpo_ref,
             y_ref, mix_ref, h1_ref, hn_ref, gu_ref, ff_ref, ffo_ref, h2_ref, wo_v, wgu_v, wdn_v, sems):
        _load_resident(pl.program_id(0), (wo_hbm, wo_v, sems.at[0]), (wgu_hbm, wgu_v, sems.at[1]), (wdn_hbm, wdn_v, sems.at[2]))
        yraw = jnp.concatenate([yab_ref[...], yc_ref[...]], axis=1)
        ggv = gg_ref[...]
        y = jnp.concatenate([_rms(yraw[:, a:b], ggv[:, a:b]) for a, b in _group_bounds(D)], axis=1).astype(BF16)
        y_ref[...] = y
        mix = _dot(y, wo_v[...])
        mix_ref[...] = mix.astype(BF16)
        h1 = h_ref[...] + _rms(mix, gpm_ref[...])
        h1_ref[...] = h1
        hn = _rms(h1, gpf_ref[...]).astype(BF16)
        hn_ref[...] = hn
        ffo = jnp.zeros((TBF, D), F32)
        for sg, su in _ff_tiles(DFF):
            gate = _dot(hn, wgu_v[:, sg])
            up = _dot(hn, wgu_v[:, su])
            gu_ref[:, sg] = gate.astype(BF16)
            gu_ref[:, su] = up.astype(BF16)
            ff = (gate * jax.nn.sigmoid(gate) * up).astype(BF16)
            ff_ref[:, sg] = ff
            ffo = ffo + _dot(ff, wdn_v[sg, :])
        ffo_ref[...] = ffo.astype(BF16)
        h2_ref[...] = h1 + _rms(ffo, gpo_ref[...])

    row = lambda n: pl.BlockSpec((TBF, n), lambda i: (i, 0))
    gain = _full((1, D))
    f32, bf16 = (lambda n: jax.ShapeDtypeStruct((T, n), F32)), (lambda n: jax.ShapeDtypeStruct((T, n), BF16))
    return pl.pallas_call(
        body, name="layer_tail_fwd", grid=(T // TBF,),
        in_specs=[row(D // 2), row(D // 2), gain, ANY, row(D), gain, gain, ANY, ANY, gain],
        out_specs=[row(D), row(D), row(D), row(D), row(F2), row(DFF), row(D), row(D)],
        out_shape=[bf16(D), bf16(D), f32(D), bf16(D), bf16(F2), bf16(DFF), bf16(D), f32(D)],
        scratch_shapes=[pltpu.VMEM(w_out.shape, BF16), pltpu.VMEM(w_gu.shape, BF16), pltpu.VMEM(w_dn.shape, BF16),
                        pltpu.SemaphoreType.DMA((3,))],
        compiler_params=_params("arbitrary"),
    )(yab, yc, gg, w_out, h, g_pm, g_pf, w_gu, w_dn, g_po)


def _loss_grad(h, tgt):
    T, D = h.shape

    def body(h_ref, t_ref, dh_ref, loss_ref):
        _acc_init(pl.program_id(0), loss_ref)
        diff = h_ref[...] - t_ref[...]
        dh_ref[...] = diff * (1.0 / D)
        loss_ref[...] += 0.5 * jnp.sum(jnp.mean(diff * diff, axis=-1, keepdims=True))

    row = pl.BlockSpec((TB, D), lambda i: (i, 0))
    return pl.pallas_call(
        body, name="loss_grad", grid=(T // TB,),
        in_specs=[row, row], out_specs=[row, _full((8, LANES))],
        out_shape=[jax.ShapeDtypeStruct((T, D), F32), jax.ShapeDtypeStruct((8, LANES), F32)],
        compiler_params=_params("arbitrary"),
    )(h, tgt)


def _layer_tail_bwd(dh2, ffo, g_po, gu, w_dn, w_gu, h1, g_pf, mix, g_pm, w_out, yab, yc, gg):
    T, D = dh2.shape
    F2 = gu.shape[1]
    DFF = F2 // 2

    def body(dh_ref, ffo_ref, gpo_ref, gu_ref, wdn_hbm, wgu_hbm, h_ref, gpf_ref, mix_ref, gpm_ref, wo_hbm, yab_ref, yc_ref, gg_ref,
             dffo_ref, dgu_ref, dh1_ref, dmix_ref, dyab_ref, dyc_ref, dgpo_ref, dgpf_ref, dgpm_ref, dgg_ref,
             wdn_v, wgu_v, wo_v, sems):
        step = pl.program_id(0)
        _load_resident(step, (wdn_hbm, wdn_v, sems.at[0]), (wgu_hbm, wgu_v, sems.at[1]), (wo_hbm, wo_v, sems.at[2]))
        _acc_init(step, dgpo_ref, dgpf_ref, dgpm_ref, dgg_ref)
        dh = dh_ref[...]
        dffo, dg = _rms_bwd(dh, ffo_ref[...].astype(F32), gpo_ref[...])
        dgpo_ref[0:1, :] += dg
        dffo = dffo.astype(BF16)
        dffo_ref[...] = dffo
        dhn = jnp.zeros((TBF, D), F32)
        for sg, su in _ff_tiles(DFF):
            dff = _dot_nt(dffo, wdn_v[sg, :])
            gate, up = gu_ref[:, sg].astype(F32), gu_ref[:, su].astype(F32)
            sig = jax.nn.sigmoid(gate)
            dgate = (dff * up * (sig * (1.0 + gate * (1.0 - sig)))).astype(BF16)
            dup = (dff * (gate * sig)).astype(BF16)
            dgu_ref[:, sg] = dgate
            dgu_ref[:, su] = dup
            dhn = dhn + _dot_nt(dgate, wgu_v[:, sg]) + _dot_nt(dup, wgu_v[:, su])
        dx, dg = _rms_bwd(dhn, h_ref[...], gpf_ref[...])
        dgpf_ref[0:1, :] += dg
        dh1 = dh + dx
        dh1_ref[...] = dh1
        dmix, dg = _rms_bwd(dh1, mix_ref[...].astype(F32), gpm_ref[...])
        dgpm_ref[0:1, :] += dg
        dmix = dmix.astype(BF16)
        dmix_ref[...] = dmix
        dy = _dot_nt(dmix, wo_v[...])
        yraw = jnp.concatenate([yab_ref[...], yc_ref[...]], axis=1)
        ggv = gg_ref[...]
        parts = [_rms_bwd(dy[:, a:b], yraw[:, a:b], ggv[:, a:b]) for a, b in _group_bounds(D)]
        dgg_ref[0:1, :] += jnp.concatenate([p[1] for p in parts], axis=1)
        dyab_ref[...] = jnp.concatenate([parts[0][0], parts[1][0]], axis=1).astype(BF16)
        dyc_ref[...] = parts[2][0].astype(BF16)

    row = lambda n: pl.BlockSpec((TBF, n), lambda i: (i, 0))
    gain, acc = _full((1, D)), _full((8, D))
    f32, bf16 = (lambda n: jax.ShapeDtypeStruct((T, n), F32)), (lambda n: jax.ShapeDtypeStruct((T, n), BF16))
    acc_shape = jax.ShapeDtypeStruct((8, D), F32)
    return pl.pallas_call(
        body, name="layer_tail_bwd", grid=(T // TBF,),
        in_specs=[row(D), row(D), gain, row(F2), ANY, ANY, row(D), gain, row(D), gain, ANY, row(D // 2), row(D // 2), gain],
        out_specs=[row(D), row(F2), row(D), row(D), row(D // 2), row(D // 2), acc, acc, acc, acc],
        out_shape=[bf16(D), bf16(F2), f32(D), bf16(D), bf16(D // 2), bf16(D // 2), acc_shape, acc_shape, acc_shape, acc_shape],
        scratch_shapes=[pltpu.VMEM(w_dn.shape, BF16), pltpu.VMEM(w_gu.shape, BF16), pltpu.VMEM(w_out.shape, BF16),
                        pltpu.SemaphoreType.DMA((3,))],
        compiler_params=_params("arbitrary"),
    )(dh2, ffo, g_po, gu, w_dn, w_gu, h1, g_pf, mix, g_pm, w_out, yab, yc, gg)


def _attn_bwd(qkv, kv_t, dyc, yc, lse, bias_t, after):
    T = qkv.shape[0]
    NP = qkv.shape[1] // (3 * LANES)
    NB = T // TB
    NS = TB // SUB
    scale = HEAD_DIM ** -0.5

    def body(q_ref, kc_ref, kp_ref, vc_ref, vp_ref, ktc_ref, ktp_ref, do_ref, o_ref, lse_ref, b_ref, after_ref,
             dq_ref, dk_ref, dv_ref, db_ref, dk_carry, dv_carry, dkw, dvw):
        step = pl.program_id(1)
        i = NB - 1 - step
        _acc_init(step, dk_carry, dv_carry, db_ref)

        def compute(first_block):
            q = q_ref[...] * scale
            kwin = jnp.concatenate([kp_ref[...], kc_ref[...]], axis=0)
            vwin = jnp.concatenate([vp_ref[...], vc_ref[...]], axis=0)
            kt = jnp.concatenate([ktp_ref[...], ktc_ref[...]], axis=1)
            do = do_ref[...].astype(F32)
            dob = do.astype(BF16)
            prod = do * o_ref[...]
            first = lax.broadcasted_iota(jnp.int32, (1, LANES), 1) < HEAD_DIM
            ones = jnp.ones((8, LANES), F32)
            row_sums = lambda x: lax.dot_general(ones, x, (((1,), (1,)), ((), ())), precision=lax.Precision.HIGHEST,
                                                 preferred_element_type=F32)
            deltas = (row_sums(jnp.where(first, prod, 0.0)), row_sums(jnp.where(first, 0.0, prod)))
            written = set()
            for s in range(NS):
                rows = slice(s * SUB, (s + 1) * SUB)
                tiles = _key_tiles(s, first_block)
                keys = slice(tiles[0] * SUB, (tiles[-1] + 1) * SUB)
                brows = slice((tiles[0] - s) * SUB, (tiles[-1] - s + 1) * SUB)
                q2, do2 = _both_heads(q[rows]), _both_heads(dob[rows])
                lse = jnp.concatenate([lse_ref[0, 0, s:s + 1, :], lse_ref[0, 0, NS + s:NS + s + 1, :]], axis=1)
                delta = jnp.concatenate([deltas[0][0:1, rows], deltas[1][0:1, rows]], axis=1)
                p = jnp.exp(_dot_nt(kwin[keys], q2) + b_ref[0, brows, :] - lse)
                ds = p * (_dot_nt(vwin[keys], do2) - delta)
                db_ref[0, brows, :] += ds
                dsb = ds.astype(BF16)
                dk_c = _dot(dsb, q2)
                dv_c = _dot(p.astype(BF16), do2)
                dq_ref[rows, :] = (_own_head_rows(_dot(kt[:, keys], dsb)).T * scale).astype(BF16)
                for n, t in enumerate(tiles):
                    win, loc = slice(t * SUB, (t + 1) * SUB), slice(n * SUB, (n + 1) * SUB)
                    if t in written:
                        dkw[win, :] += dk_c[loc]
                        dvw[win, :] += dv_c[loc]
                    else:
                        dkw[win, :] = dk_c[loc]
                        dvw[win, :] = dv_c[loc]
                        written.add(t)
            dk_ref[...] = (dkw[TB:, :] + dk_carry[...]).astype(BF16)
            dv_ref[...] = (dvw[TB:, :] + dv_carry[...]).astype(BF16)
            if not first_block:
                dk_carry[...] = dkw[:TB, :]
                dv_carry[...] = dvw[:TB, :]

        pl.when(i == 0)(functools.partial(compute, True))
        pl.when(i > 0)(functools.partial(compute, False))

    blk = lambda s: NB - 1 - s
    prev = lambda s: jnp.maximum(NB - 2 - s, 0)
    rows = lambda which, off: pl.BlockSpec((TB, LANES), lambda p, s: (which(s), off + p))
    out = jax.ShapeDtypeStruct((T, NP * LANES), BF16)
    strip = pl.BlockSpec((1, BANDW, 2 * SUB), lambda p, s: (p, 0, 0))
    return pl.pallas_call(
        body, name="attn_bwd", grid=(NP, NB),
        in_specs=[rows(blk, 0), rows(blk, NP), rows(prev, NP), rows(blk, 2 * NP), rows(prev, 2 * NP),
                  pl.BlockSpec((LANES, TB), lambda p, s: (p, blk(s))), pl.BlockSpec((LANES, TB), lambda p, s: (p, prev(s))),
                  rows(blk, 0), rows(blk, 0), pl.BlockSpec((1, 1, 8, LANES), lambda p, s: (p, blk(s), 0, 0)), strip, ANY],
        out_specs=[rows(blk, 0), rows(blk, 0), rows(blk, 0), strip],
        out_shape=[out, out, out, jax.ShapeDtypeStruct((NP, BANDW, 2 * SUB), F32)],
        scratch_shapes=[pltpu.VMEM((TB, LANES), F32), pltpu.VMEM((TB, LANES), F32),
                        pltpu.VMEM((2 * TB, LANES), F32), pltpu.VMEM((2 * TB, LANES), F32)],
        compiler_params=_params("arbitrary", "arbitrary"),
    )(qkv, qkv, qkv, qkv, qkv, kv_t, kv_t, dyc, yc, lse, bias_t, after)


def _inproj_mixers_bwd(pa, dyab, cw, wbd, ps, dqkv, w, h, g, dh1):
    T, D = pa.shape
    DC = D // 4
    NB = T // TB
    N = TB + HALO
    widths = [p.shape[1] for p in dqkv]
    n = len(dqkv)

    def body(*refs):
        pa_ref, halo_ref, dy_ref, cw_ref, wbd_ref, ps_ref = refs[:6]
        parts = refs[6:6 + n]
        w_ref, h_ref, g_ref, dh1_ref, dpa_ref, dh_ref, dcw_ref, dwbd_ref, dps_ref, dg_ref, dc_carry, e_carry = refs[6 + n:]
        step = pl.program_id(0)
        i = NB - 1 - step
        _acc_init(step, dcw_ref, dwbd_ref, dps_ref, dg_ref, dc_carry, e_carry)
        x = pa_ref[...]
        hl = jnp.where(i > 0, halo_ref[...], 0.0)
        gb, gc, u, pu = (x[:, k * DC:(k + 1) * DC] for k in range(4))
        dy = dy_ref[...].astype(F32)
        dya, dyb = dy[:, :DC], dy[:, DC:]
        cwv = cw_ref[...]
        z = gc * u
        z1, z2 = _conv_taps(z, hl[:, DC:2 * DC] * hl[:, 2 * DC:3 * DC])
        dgb = dya * (cwv[2:3] * z + cwv[1:2] * z1 + cwv[0:1] * z2)
        dconv = dya * gb
        dcw_ref[0:1, :] += jnp.sum(dconv * z2, axis=0, keepdims=True)
        dcw_ref[1:2, :] += jnp.sum(dconv * z1, axis=0, keepdims=True)
        dcw_ref[2:3, :] += jnp.sum(dconv * z, axis=0, keepdims=True)
        ext = jnp.concatenate([dconv, dc_carry[...]], axis=0)
        dz = cwv[2:3] * dconv + cwv[1:2] * pltpu.roll(ext, N - 1, 0)[:TB] + cwv[0:1] * pltpu.roll(ext, N - 2, 0)[:TB]
        dc_carry[...] = dconv[:HALO]
        cnt = _pool_count(i, DC)
        d = _pool_d(pu, hl[:, 3 * DC:], cnt).astype(BF16)
        psv = ps_ref[...]
        wb = wbd_ref[...]
        dps_ref[0:1, :] += jnp.sum(dyb * _dot(d, wb), axis=0, keepdims=True)
        dys = (dyb * psv).astype(BF16)
        dwbd_ref[...] += _dot_tn(d, dys)
        dd = _dot_nt(dys, wb)
        e = dd / cnt
        ext = jnp.concatenate([e, e_carry[...]], axis=0)
        a2 = ext + pltpu.roll(ext, N - 1, 0)
        a4 = a2 + pltpu.roll(a2, N - 2, 0)
        a8 = a4 + pltpu.roll(a4, N - 4, 0)
        a16 = a8 + pltpu.roll(a8, N - 8, 0)
        dpu = _pick_group([a2, a4, a8, a16], DC)[:TB] - dd
        e_carry[...] = e[:HALO]
        dpa = jnp.concatenate([dgb, dz * u, dz * gc, dpu], axis=1).astype(BF16)
        dpa_ref[...] = dpa
        dxn, off = _dot_nt(dpa, w_ref[:, :D]), D
        for p_ref, wd in zip(parts, widths):
            dxn = dxn + _dot_nt(p_ref[...], w_ref[:, off:off + wd])
            off += wd
        dx, dg = _rms_bwd(dxn, h_ref[...], g_ref[...])
        dg_ref[0:1, :] += dg
        dh_ref[...] = dh1_ref[...] + dx

    blk = lambda m: pl.BlockSpec((TB, m), lambda s: (NB - 1 - s, 0))
    return pl.pallas_call(
        body, name="inproj_mixers_bwd", grid=(NB,),
        in_specs=[blk(D), pl.BlockSpec((HALO, D), lambda s: (jnp.maximum((NB - 1 - s) * (TB // HALO) - 1, 0), 0)),
                  blk(2 * DC), _full((8, DC)), _full((DC, DC)), _full((1, DC))] + [blk(wd) for wd in widths]
                 + [_full(w.shape), blk(D), _full((1, D)), blk(D)],
        out_specs=[blk(D), blk(D), _full((8, DC)), _full((DC, DC)), _full((8, DC)), _full((8, D))],
        out_shape=[jax.ShapeDtypeStruct((T, D), BF16), jax.ShapeDtypeStruct((T, D), F32), jax.ShapeDtypeStruct((8, DC), F32),
                   jax.ShapeDtypeStruct((DC, DC), F32), jax.ShapeDtypeStruct((8, DC), F32), jax.ShapeDtypeStruct((8, D), F32)],
        scratch_shapes=[pltpu.VMEM((HALO, DC), F32), pltpu.VMEM((HALO, DC), F32)],
        compiler_params=_params("arbitrary"),
    )(pa, pa, dyab, cw, wbd, ps, *dqkv, w, h, g, dh1)


def _wgrad(a, b, tk, tn, name):
    T, K = a.shape
    N = b.shape[1]

    def body(a_ref, b_ref, o_ref):
        _acc_init(pl.program_id(2), o_ref)
        o_ref[...] += _dot_tn(a_ref[...], b_ref[...])

    return pl.pallas_call(
        body, name=name, grid=(K // tk, N // tn, T // TBW),
        in_specs=[pl.BlockSpec((TBW, tk), lambda k, n, t: (t, k)), pl.BlockSpec((TBW, tn), lambda k, n, t: (t, n))],
        out_specs=pl.BlockSpec((tk, tn), lambda k, n, t: (k, n)),
        out_shape=jax.ShapeDtypeStruct((K, N), F32),
        compiler_params=_params("parallel", "parallel", "arbitrary"),
    )(a, b)


def _wgrad_concat(a, bs, name):
    T, K = a.shape
    widths = [b.shape[1] for b in bs]
    n = len(bs)

    def body(*refs):
        a_ref, b_refs, o_ref = refs[0], refs[1:1 + n], refs[1 + n]
        _acc_init(pl.program_id(0), o_ref)
        av, off = a_ref[...], 0
        for b_ref, wd in zip(b_refs, widths):
            o_ref[:, off:off + wd] += _dot_tn(av, b_ref[...])
            off += wd

    row = lambda m: pl.BlockSpec((TBW, m), lambda t: (t, 0))
    return pl.pallas_call(
        body, name=name, grid=(T // TBW,),
        in_specs=[row(K)] + [row(wd) for wd in widths],
        out_specs=_full((K, sum(widths))),
        out_shape=jax.ShapeDtypeStruct((K, sum(widths)), F32),
        compiler_params=_params("arbitrary"),
    )(a, *bs)


def _adamw(w, g, m, v, tr, name):
    R, C = w.shape

    def body(w_ref, g_ref, m_ref, v_ref, d_ref, nm_ref, nv_ref):
        gv = g_ref[...]
        nm = ADAM_B1 * m_ref[...] + (1.0 - ADAM_B1) * gv
        nv = ADAM_B2 * v_ref[...] + (1.0 - ADAM_B2) * (gv * gv)
        m_hat = nm / (1.0 - ADAM_B1 ** ADAM_STEP)
        v_hat = nv / (1.0 - ADAM_B2 ** ADAM_STEP)
        d_ref[...] = -ADAM_LR * (m_hat / (jnp.sqrt(v_hat) + ADAM_EPS) + ADAM_WD * w_ref[...])
        nm_ref[...] = nm
        nv_ref[...] = nv

    blk = pl.BlockSpec((tr, C), lambda i: (i, 0))
    out = jax.ShapeDtypeStruct((R, C), F32)
    return pl.pallas_call(
        body, name=name, grid=(R // tr,), in_specs=[blk] * 4, out_specs=[blk] * 3, out_shape=[out] * 3,
        compiler_params=_params("parallel"),
    )(w, g, m, v)


def _sum_slots(x, tr, name):
    n, R, C = x.shape

    def body(x_ref, o_ref):
        acc = x_ref[0]
        for k in range(1, n):
            acc = acc + x_ref[k]
        o_ref[...] = acc

    return pl.pallas_call(
        body, name=name, grid=(R // tr,),
        in_specs=[pl.BlockSpec((n, tr, C), lambda i: (0, i, 0))],
        out_specs=pl.BlockSpec((tr, C), lambda i: (i, 0)),
        out_shape=jax.ShapeDtypeStruct((R, C), F32),
        compiler_params=_params("parallel"),
    )(x)


def _add_half(view, other, c, name):
    if view.ndim == 3:
        _, R2, N = view.shape
        tr = 128
        grid = (R2 // tr,)
        in_specs = [pl.BlockSpec((1, tr, N), lambda i, c_ref: (c_ref[0], i, 0)), pl.BlockSpec((1, tr, N), lambda i, c_ref: (0, i, 0))]
        out_spec = pl.BlockSpec((tr, N), lambda i, c_ref: (i, 0))
        out_shape = jax.ShapeDtypeStruct((R2, N), BF16)

        def body(c_ref, a_ref, b_ref, o_ref):
            o_ref[...] = (a_ref[0] + b_ref[0]).astype(BF16)
    else:
        S, _, R2, C = view.shape
        grid = (S,)
        in_specs = [pl.BlockSpec((1, 1, R2, C), lambda s, c_ref: (s, c_ref[0], 0, 0)), pl.BlockSpec((1, 1, R2, C), lambda s, c_ref: (s, 0, 0, 0))]
        out_spec = pl.BlockSpec((1, R2, C), lambda s, c_ref: (s, 0, 0))
        out_shape = jax.ShapeDtypeStruct((S, R2, C), BF16)

        def body(c_ref, a_ref, b_ref, o_ref):
            o_ref[0] = (a_ref[0, 0] + b_ref[0, 0]).astype(BF16)

    return pl.pallas_call(
        body, name=name,
        grid_spec=pltpu.PrefetchScalarGridSpec(num_scalar_prefetch=1, grid=grid, in_specs=in_specs, out_specs=out_spec),
        out_shape=out_shape, compiler_params=_params("parallel"),
    )(c, view, other)


def _sum_chips(slots, part, kind, place, name):
    _, R2, C = slots.shape
    tr = min(R2, 128) if kind == "col" else R2

    def body(p_ref, s_ref, own_ref, o_ref):
        me = p_ref[0]
        own = own_ref[...] if kind == "col" else own_ref[0]
        acc = None
        for k in range(4):
            term = jnp.where(me == k, own, s_ref[k]).astype(F32)
            acc = term if acc is None else acc + term
        o_ref[0] = acc

    own_spec = (pl.BlockSpec((tr, C), lambda i, p: (i, p[0])) if kind == "col"
                else pl.BlockSpec((1, R2, C), lambda i, p: (p[0], 0, 0)))
    return pl.pallas_call(
        body, name=name,
        grid_spec=pltpu.PrefetchScalarGridSpec(
            num_scalar_prefetch=1, grid=(R2 // tr,),
            in_specs=[pl.BlockSpec((4, tr, C), lambda i, p: (0, i, 0)), own_spec],
            out_specs=pl.BlockSpec((1, tr, C), lambda i, p: (p[1], i, 0))),
        out_shape=jax.ShapeDtypeStruct((2, R2, C), F32), compiler_params=_params("parallel"),
    )(place, slots, part)


def _place():
    x, y, c = lax.axis_index("x"), lax.axis_index("y"), lax.axis_index("c")
    chips = [(1 - x, y), (x, 1 - y), (1 - x, 1 - y)]
    return x, y, c, 2 * x + y, chips


def _remote(src, dst, send_sems, recv_sems, k, to):
    return pltpu.make_async_remote_copy(src_ref=src, dst_ref=dst, send_sem=send_sems.at[k], recv_sem=recv_sems.at[k],
                                        device_id=to, device_id_type=MESH)


def _full_shape(shard_shape, kind):
    *lead, R, C = shard_shape
    return (*lead, R, 4 * C) if kind == "col" else (*lead, 4 * R, C)


def _slab(full_ref, shard_rc, kind, chip, half=None):
    R, C = shard_rc
    lead = (slice(None),) * (len(full_ref.shape) - 2)
    if kind == "col":
        rows = pl.ds(0, R) if half is None else pl.ds(half * (R // 2), R // 2)
        return full_ref.at[(*lead, rows, pl.ds(chip * C, C))]
    rows = pl.ds(chip * R, R) if half is None else pl.ds(chip * R + half * (R // 2), R // 2)
    return full_ref.at[(*lead, rows, slice(None))]


def _row_half(ref, half):
    R = ref.shape[-2]
    lead = (slice(None),) * (len(ref.shape) - 2)
    return ref.at[(*lead, pl.ds(half * (R // 2), R // 2), slice(None))]


def _allgather_weights(shards, kinds, after):
    n = len(shards)
    full_shapes = [_full_shape(s.shape, k) for s, k in zip(shards, kinds)]

    def body(*refs):
        ins, outs, (send_sems, recv_sems) = refs[:n], refs[n + 1:2 * n + 1], refs[2 * n + 1:]
        x, y, c, me, chips = _place()
        slab = lambda a, chip, half=None: _slab(outs[a], shards[a].shape[-2:], kinds[a], chip, half)
        my_half = lambda a: _row_half(ins[a], c)
        own = [_remote(ins[a], slab(a, me), send_sems, recv_sems, 6 * n + a, (x, y, 1 - c)) for a in range(n)]
        first = [_remote(my_half(a), slab(a, me, c), send_sems, recv_sems, j * n + a, (*chip, c))
                 for j, chip in enumerate(chips) for a in range(n)]
        for cp in first + own:
            cp.start()
        passed = []
        for j, chip in enumerate(chips):
            src = 2 * chip[0] + chip[1]
            for a in range(n):
                _remote(my_half(a), slab(a, src, c), send_sems, recv_sems, j * n + a, (x, y, c)).wait_recv()
                cp = _remote(slab(a, src, c), slab(a, src, c), send_sems, recv_sems, (3 + j) * n + a, (x, y, 1 - c))
                cp.start()
                passed.append(cp)
        for j, chip in enumerate(chips):
            src = 2 * chip[0] + chip[1]
            for a in range(n):
                _remote(my_half(a), slab(a, src, 1 - c), send_sems, recv_sems, (3 + j) * n + a, (x, y, c)).wait_recv()
        for cp in own:
            cp.wait_recv()
        for cp in first + passed + own:
            cp.wait_send()

    return pl.pallas_call(
        body, name="allgather_weights",
        in_specs=[ANY] * (n + 1), out_specs=[ANY] * n,
        out_shape=[jax.ShapeDtypeStruct(s, BF16) for s in full_shapes],
        scratch_shapes=[pltpu.SemaphoreType.DMA((7 * n,)), pltpu.SemaphoreType.DMA((7 * n,))],
        compiler_params=pltpu.CompilerParams(has_side_effects=True),
    )(*shards, after)


HBM = pl.BlockSpec(memory_space=pltpu.HBM)
SEM = pl.BlockSpec(memory_space=pltpu.SEMAPHORE)
DATAFLOW = pltpu.SideEffectType.DATAFLOW_SIDE_EFFECTING


def _split_start(name, bufs, plan, ncopies, after=None):
    nb = len(bufs)
    nin = nb + (after is not None)

    def body(*refs):
        send_sems, recv_sems, token = refs[nin], refs[nin + 1], refs[-1]
        for k, (src, dst, to) in enumerate(plan(refs[:nb])):
            _remote(src, dst, send_sems, recv_sems, k, to).start()
        token[...] = jnp.zeros_like(token)

    out = pl.pallas_call(
        body, name=name,
        out_shape=(pltpu.SemaphoreType.DMA((ncopies,)), pltpu.SemaphoreType.DMA((ncopies,)),
                   *[pltpu.HBM(b.shape, b.dtype) for b in bufs], jax.ShapeDtypeStruct((8, LANES), F32)),
        in_specs=[HBM] * nb + [ANY] * (nin - nb), out_specs=(SEM, SEM, *[HBM] * nb, pl.BlockSpec(memory_space=pltpu.VMEM)),
        input_output_aliases={i: 2 + i for i in range(nb)},
        compiler_params=pltpu.CompilerParams(has_side_effects=DATAFLOW),
    )(*[pltpu.with_memory_space_constraint(b, pltpu.HBM) for b in bufs], *([] if after is None else [after]))
    return out[0], out[1], list(out[2:2 + nb]), out[-1]


def _split_wait(name, send_sems, recv_sems, bufs, plan, after):
    nb = len(bufs)

    def body(*refs):
        s_sems, r_sems = refs[nb], refs[nb + 1]
        for k, (src, dst, to) in enumerate(plan(refs[:nb])):
            cp = _remote(src, dst, s_sems, r_sems, k, to)
            cp.wait_send()
            cp.wait_recv()

    return pl.pallas_call(
        body, name=name,
        out_shape=tuple(pltpu.HBM(b.shape, b.dtype) for b in bufs),
        in_specs=[HBM] * nb + [SEM, SEM, ANY], out_specs=tuple([HBM] * nb),
        input_output_aliases={i: i for i in range(nb)},
        compiler_params=pltpu.CompilerParams(has_side_effects=DATAFLOW),
    )(*bufs, send_sems, recv_sems, after)


def _gather_plan(n, shard_rcs, kinds):
    def plan(refs):
        x, y, c, me, chips = _place()
        out = []
        for a in range(n):
            shard, full = refs[a], refs[n + a]
            out.append((shard, _slab(full, shard_rcs[a], kinds[a], me), (x, y, 1 - c)))
            for chip in chips:
                for cc in (c, 1 - c):
                    out.append((_row_half(shard, c), _slab(full, shard_rcs[a], kinds[a], me, c), (*chip, cc)))
        return out
    return plan


def _exchange_plan(n, kinds):
    def plan(refs):
        x, y, c, me, chips = _place()
        out = []
        for a in range(n):
            part, slots = refs[a], refs[n + a]
            C = slots.shape[2]
            for chip in chips:
                dst = 2 * chip[0] + chip[1]
                src = part.at[:, pl.ds(dst * C, C)] if kinds[a] == "col" else part.at[dst]
                out.append((src, slots.at[me], (*chip, c)))
        return out
    return plan


def _swap_plan(ndims):
    n = len(ndims)

    def plan(refs):
        x, y, c, _, _ = _place()
        out = []
        for a in range(n):
            src = refs[a].at[pl.ds(1 - c, 1)] if ndims[a] == 3 else refs[a].at[:, pl.ds(1 - c, 1)]
            out.append((src, refs[n + a], (x, y, 1 - c)))
        return out
    return plan


def _join_halves(bufs):
    n = len(bufs)

    def body(*refs):
        outs, (send_sems, recv_sems) = refs[n:2 * n], refs[2 * n:]
        x, y, c, _, _ = _place()
        sends = [_remote(outs[a].at[c], outs[a].at[c], send_sems, recv_sems, a, (x, y, 1 - c)) for a in range(n)]
        for cp in sends:
            cp.start()
        for a in range(n):
            _remote(outs[a].at[c], outs[a].at[1 - c], send_sems, recv_sems, a, (x, y, c)).wait_recv()
        for cp in sends:
            cp.wait_send()

    return pl.pallas_call(
        body, name="grad_join_halves", in_specs=[ANY] * n, out_specs=[ANY] * n,
        out_shape=[jax.ShapeDtypeStruct(b.shape, F32) for b in bufs],
        input_output_aliases={a: a for a in range(n)},
        scratch_shapes=[pltpu.SemaphoreType.DMA((n,)), pltpu.SemaphoreType.DMA((n,))],
        compiler_params=pltpu.CompilerParams(has_side_effects=True),
    )(*bufs)


def _allgather_small(block):
    M, N = block.shape

    def body(x_ref, out_ref, send_sems, recv_sems, local_sem):
        x, y, c, _, chips = _place()
        me, sibling = (x, y, c), (x, y, 1 - c)

        def rows(px, py, pc):
            return out_ref.at[pl.ds((4 * px + 2 * py + pc) * M, M), :]

        def copy(k, blk, to, src=None):
            return _remote(rows(*blk) if src is None else src, rows(*blk), send_sems, recv_sems, k, to)

        mine = pltpu.make_async_copy(x_ref, rows(*me), local_sem)
        mine.start()
        first = [copy(0, me, sibling, src=x_ref)] + [copy(1 + j, me, (*chip, c), src=x_ref) for j, chip in enumerate(chips)]
        for cp in first:
            cp.start()
        passed = [copy(4 + j, (*chip, c), sibling) for j, chip in enumerate(chips)]
        for j, chip in enumerate(chips):
            copy(1 + j, (*chip, c), me).wait_recv()
            passed[j].start()
        copy(0, sibling, me).wait_recv()
        for j, chip in enumerate(chips):
            copy(4 + j, (*chip, 1 - c), me).wait_recv()
        for cp in first + passed:
            cp.wait_send()
        mine.wait()

    vm = pl.BlockSpec(memory_space=pltpu.VMEM)
    return pl.pallas_call(
        body, name="allgather_small", in_specs=[vm], out_specs=vm,
        out_shape=jax.ShapeDtypeStruct((8 * M, N), F32),
        scratch_shapes=[pltpu.SemaphoreType.DMA((7,)), pltpu.SemaphoreType.DMA((7,)), pltpu.SemaphoreType.DMA],
        compiler_params=pltpu.CompilerParams(has_side_effects=True, vmem_limit_bytes=VMEM_LIMIT),
    )(block)


def _pack(arrays):
    flat = jnp.concatenate([a.reshape(-1) for a in arrays])
    pad = (-flat.shape[0]) % (8 * LANES)
    return jnp.pad(flat, (0, pad)).reshape(-1, LANES)


def _unpack(packed, shapes):
    flat, out, off = packed.reshape(-1), [], 0
    for s in shapes:
        size = 1
        for d in s:
            size *= d
        out.append(flat[off:off + size].reshape(s))
        off += size
    return out


def _block_diag(pw):
    G, n, _ = pw.shape
    eye = jnp.eye(G, dtype=pw.dtype)
    return (eye[:, None, :, None] * pw[:, :, None, :]).reshape(G * n, G * n)


def _diag_blocks(m, G):
    n = m.shape[0] // G
    return jnp.stack([m[g * n:(g + 1) * n, g * n:(g + 1) * n] for g in range(G)])


def _pad_rows(a, rows):
    return jnp.pad(a, ((0, rows - a.shape[0]), (0, 0)))


def kernel(x, w_in, w_out, conv_w, pool_w, pool_scale, rel_bias, group_gain, pre_mix_g, post_mix_g, pre_ffn_g, post_ffn_g, w_gate_up, w_down, loss_target, m_w_in, m_w_out, m_conv_w, m_pool_w, m_pool_scale, m_rel_bias, m_group_gain, m_pre_mix_g, m_post_mix_g, m_pre_ffn_g, m_post_ffn_g, m_w_gate_up, m_w_down, v_w_in, v_w_out, v_conv_w, v_pool_w, v_pool_scale, v_rel_bias, v_group_gain, v_pre_mix_g, v_post_mix_g, v_pre_ffn_g, v_post_ffn_g, v_w_gate_up, v_w_down):
    L = w_in.shape[0]
    T, D = x.shape[1], x.shape[2]
    DC = D // 4
    NH = rel_bias.shape[1]
    NREL = rel_bias.shape[2]
    G = pool_w.shape[1]
    cs = conv_w.shape[2]
    assert TB == LEFT_CHUNKS * CHUNK and T % TB == 0 and D % (4 * LANES) == 0 and NH * HEAD_DIM == D // 2
    xi, yi, ci = lax.axis_index("x"), lax.axis_index("y"), lax.axis_index("c")
    chip = 2 * xi + yi

    kinds = ("col", "row", "col", "row")
    big = (w_in, w_out, w_gate_up, w_down)
    rcs = [w.shape[-2:] for w in big]
    conv_gathered = _allgather_small(_pack([conv_w]))
    conv_all = conv_gathered.reshape(8, -1)[:, :L * 3 * cs].reshape(4, 2, L, 3, cs)[:, 0]
    conv_full = jnp.moveaxis(conv_all, 0, 2).reshape(L, 3, 4 * cs)
    landing = lambda which: [lax.empty(_full_shape(rcs[k], kinds[k]), BF16) for k in which]
    wi0 = _allgather_weights([w_in[0:1].astype(BF16)], kinds[:1], conv_gathered)[0][0]
    rest_plan = _gather_plan(3, rcs[1:], kinds[1:])
    rest = _split_start("gather_start_0", [w[0].astype(BF16) for w in big[1:]] + landing((1, 2, 3)), rest_plan, 3 * 7, after=wi0)
    layer_plan = _gather_plan(4, rcs, kinds)
    full = [None] * L

    h = x[0]
    saved = []
    token = rest[3]
    for l in range(L):
        g_pre, g_pm, g_pf, g_po = (a[l][None] for a in (pre_mix_g, post_mix_g, pre_ffn_g, post_ffn_g))
        if 0 < l < L - 1:
            nxt = _split_start(f"gather_start_{l + 1}", [w[l + 1].astype(BF16) for w in big] + landing(range(4)), layer_plan, 4 * 7,
                               after=full[l][0])
            token = nxt[3]
        if l < L - 1:
            g_pre = g_pre + token[0:1, 0:1]
        gg, ps = group_gain[l][None], pool_scale[l][None]
        cw = _pad_rows(conv_full[l], 8)
        wbd = _block_diag(pool_w[l]).astype(BF16)
        bias = _bias_build(jnp.pad(rel_bias[l], ((0, 0), (0, RBP - NREL))))
        wi = wi0 if l == 0 else full[l][0]
        xn, pa, qkv, kv_t, yab = _inproj_mixers_fwd(h, g_pre, wi, cw, wbd, ps)
        yc, lse = _attn_fwd(qkv, kv_t, bias)
        if l == 0:
            full[0] = [wi0, *_split_wait("gather_wait_0", rest[0], rest[1], rest[2], rest_plan, yc)[3:6]]
            if L > 1:
                nxt = _split_start("gather_start_1", [w[1].astype(BF16) for w in big] + landing(range(4)), layer_plan, 4 * 7,
                                   after=full[0][1])
                g_pm = g_pm + nxt[3][0:1, 0:1]
        wi, wo, wgu, wdn = full[l]
        y, mix, h1, hn, gu, ff, ffo, h2 = _layer_tail_fwd(yab, yc, gg, wo, h, g_pm, g_pf, wgu, wdn, g_po)
        saved.append(dict(h=h, xn=xn, pa=pa, qkv=qkv, kv_t=kv_t, lse=lse, yab=yab, yc=yc, y=y, mix=mix, h1=h1, hn=hn, gu=gu, ff=ff, ffo=ffo,
                          cw=cw, wbd=wbd, bias=bias, ps=ps, gg=gg, g_pre=g_pre, g_pm=g_pm, g_pf=g_pf, g_po=g_po))
        h = h2
        if l + 1 < L:
            full[l + 1] = list(_split_wait(f"gather_wait_{l + 1}", nxt[0], nxt[1], nxt[2], layer_plan, h2)[4:8])

    dh, loss_tile = _loss_grad(h, loss_target[0])

    xplan = _exchange_plan(4, kinds)
    splan = _swap_plan([3, 4, 3, 4])
    place = jnp.stack([chip, ci]).astype(jnp.int32)
    cvec = ci.reshape(1).astype(jnp.int32)
    small_grads = [None] * L
    shard_grads = [None] * L
    dbiases = [None] * L

    def start_exchange(lp, swap):
        thru = _split_wait(f"grad_swap_wait_{lp}", swap[0], swap[1], swap[2], splan, swap[4])
        chip_sums = [_add_half(v, t, cvec, "grad_add_half") for v, t in zip(thru[:4], thru[4:])]
        slots = [lax.empty((4, p.shape[0], p.shape[1] // 4) if k == "col" else p.shape, BF16) for p, k in zip(chip_sums, kinds)]
        ssem, rsem, bufs, tok = _split_start(f"grad_exchange_start_{lp}", chip_sums + slots, xplan, 4 * 3)
        return (lp, ssem, rsem, bufs), tok

    def finish_exchange(pending, after):
        lp, ssem, rsem, thru = pending
        landed = _split_wait(f"grad_exchange_wait_{lp}", ssem, rsem, thru, xplan, after)
        bufs = [_sum_chips(landed[4 + a], landed[a], kinds[a], place, "grad_sum_chips") for a in range(4)]
        shard_grads[lp] = [j.reshape(2 * j.shape[1], j.shape[2]) for j in _join_halves(bufs)]

    swap = None
    exchange = None
    for l in reversed(range(L)):
        s = saved[l]
        wi, wo, wgu, wdn = full[l]
        g_po = s["g_po"] if swap is None else s["g_po"] + swap[3][0:1, 0:1]
        dffo, dgu, dh1, dmix, dyab, dyc, dg_po, dg_pf, dg_pm, dgg = _layer_tail_bwd(
            dh, s["ffo"], g_po, s["gu"], wdn, wgu, s["h1"], s["g_pf"], s["mix"], s["g_pm"], wo, s["yab"], s["yc"], s["gg"])
        after = dyc
        if swap is not None:
            started, after = start_exchange(l + 1, (*swap[:4], dh1))
            if exchange is not None:
                finish_exchange(exchange, after)
            exchange = started
        dq, dk, dv, dbiases[l] = _attn_bwd(s["qkv"], s["kv_t"], dyc, s["yc"], s["lse"], s["bias"], after)
        dpa, dh, dcw, dwbd, dps, dg_pre = _inproj_mixers_bwd(s["pa"], dyab, s["cw"], s["wbd"], s["ps"], [dq, dk, dv], wi,
                                                             s["h"], s["g_pre"], dh1)
        dparts = [dpa, dq, dk, dv]
        F2, DFF = s["gu"].shape[1], s["ff"].shape[1]
        grads = [_wgrad_concat(s["xn"], dparts, "wgrad_in"),
                 _wgrad(s["y"], dmix, D, D, "wgrad_out"),
                 _wgrad(s["hn"], dgu, D, F2 // 4, "wgrad_gate_up"),
                 _wgrad(s["ff"], dffo, DFF // 2, D, "wgrad_down")]
        small_grads[l] = [dcw[:3], _diag_blocks(dwbd, G), dps[0], None, dgg[0], dg_pre[0], dg_pm[0], dg_pf[0], dg_po[0]]
        views = [g.reshape(2, g.shape[0] // 2, g.shape[1]) if k == "col" else g.reshape(4, 2, g.shape[0] // 8, g.shape[1])
                 for g, k in zip(grads, kinds)]
        lands = [lax.empty((1,) + v.shape[1:] if v.ndim == 3 else (v.shape[0], 1) + v.shape[2:], F32) for v in views]
        swap = _split_start(f"grad_swap_start_{l}", views + lands, splan, 4)
    last, tok = start_exchange(0, (*swap[:4], swap[3]))
    if exchange is not None:
        finish_exchange(exchange, tok)
    for l in range(L):
        small_grads[l][3] = _bias_fold(dbiases[l], tok)[:NH, :NREL]

    names_shapes = [(L, 3, 4 * cs), pool_w.shape, pool_scale.shape, rel_bias.shape, group_gain.shape,
                    pre_mix_g.shape, post_mix_g.shape, pre_ffn_g.shape, post_ffn_g.shape]
    small_stacked = [jnp.stack([small_grads[l][k] for l in range(L)]) for k in range(len(names_shapes))]
    packed = _pack(small_stacked + [loss_tile[0:1, 0:1]])
    M = packed.shape[0]
    total = _sum_slots(_allgather_small(packed).reshape(8, M, LANES), M, "small_sum_devices")
    *g_small, loss = _unpack(total, names_shapes + [()])
    g_small[0] = lax.dynamic_slice_in_dim(g_small[0], chip * cs, cs, axis=2)
    finish_exchange(last, total)
    g_big = [jnp.stack([shard_grads[l][k] for l in range(L)]) for k in range(4)]

    def adam_big(w, g, m, v, name):
        shp = w.shape
        two = lambda a: a.reshape(shp[0] * shp[1], shp[2])
        return [o.reshape(shp) for o in _adamw(two(w), two(g), two(m), two(v), 256, name)]

    upd_in = adam_big(w_in, g_big[0], m_w_in, v_w_in, "adamw_in")
    upd_out = adam_big(w_out, g_big[1], m_w_out, v_w_out, "adamw_out")
    upd_gu = adam_big(w_gate_up, g_big[2], m_w_gate_up, v_w_gate_up, "adamw_gate_up")
    upd_dn = adam_big(w_down, g_big[3], m_w_down, v_w_down, "adamw_down")

    small_w = [conv_w, pool_w, pool_scale, rel_bias, group_gain, pre_mix_g, post_mix_g, pre_ffn_g, post_ffn_g]
    small_m = [m_conv_w, m_pool_w, m_pool_scale, m_rel_bias, m_group_gain, m_pre_mix_g, m_post_mix_g, m_pre_ffn_g, m_post_ffn_g]
    small_v = [v_conv_w, v_pool_w, v_pool_scale, v_rel_bias, v_group_gain, v_pre_mix_g, v_post_mix_g, v_pre_ffn_g, v_post_ffn_g]
    pw_, pg_, pm_, pv_ = _pack(small_w), _pack(g_small), _pack(small_m), _pack(small_v)
    shapes = [w.shape for w in small_w]
    upd_small = [_unpack(o, shapes) for o in _adamw(pw_, pg_, pm_, pv_, pw_.shape[0], "adamw_small")]

    def ordered(big4, small9):
        return [big4[0], big4[1], *small9, big4[2], big4[3]]

    grads = ordered(g_big, g_small)
    outs = [ordered([upd_in[k], upd_out[k], upd_gu[k], upd_dn[k]], upd_small[k]) for k in range(3)]
    return (loss, dh[None], *grads, *outs[0], *outs[1], *outs[2])
```

```python
import functools

import jax
import jax.numpy as jnp
from jax import lax
from jax.experimental import pallas as pl
from jax.experimental.pallas import tpu as pltpu

F32, BF16 = jnp.float32, jnp.bfloat16
EPS = 1e-6
CHUNK = 64
LEFT_CHUNKS = 8
REL_CLIP = 128
HEAD_DIM = 64
LANES = 128
POOL_WINDOWS = (2, 4, 8, 16)
HALO = 16
TB = LEFT_CHUNKS * CHUNK
TBF = 256
TBW = 1024
BAND = (LEFT_CHUNKS + 1) * CHUNK
SUB = 2 * CHUNK
BANDW = SUB + LEFT_CHUNKS * CHUNK
SKEW = 768
NEG = -1e30
RBP = 384
VMEM_LIMIT = 56 * 1024 * 1024
ADAM_LR, ADAM_B1, ADAM_B2, ADAM_EPS, ADAM_WD, ADAM_STEP = 0.001, 0.9, 0.999, 1e-08, 0.01, 10
MESH = pl.DeviceIdType.MESH
ANY = pl.BlockSpec(memory_space=pl.ANY)


def _params(*sem):
    kw = dict(vmem_limit_bytes=VMEM_LIMIT)
    if sem:
        kw["dimension_semantics"] = sem
    return pltpu.CompilerParams(**kw)


def _dot(a, b):
    return jnp.dot(a, b, preferred_element_type=F32)


def _dot_nt(a, b):
    return lax.dot_general(a, b, (((1,), (1,)), ((), ())), preferred_element_type=F32)


def _dot_tn(a, b):
    return lax.dot_general(a, b, (((0,), (0,)), ((), ())), preferred_element_type=F32)


def _rms(x, g):
    r = lax.rsqrt(jnp.mean(x * x, axis=-1, keepdims=True) + EPS)
    return x * r * g


def _rms_bwd(dy, x, g):
    r = lax.rsqrt(jnp.mean(x * x, axis=-1, keepdims=True) + EPS)
    xh = x * r
    dxh = dy * g
    dx = r * (dxh - xh * jnp.mean(dxh * xh, axis=-1, keepdims=True))
    return dx, jnp.sum(dy * xh, axis=0, keepdims=True)


def _full(shape):
    return pl.BlockSpec(shape, lambda *_: (0,) * len(shape))


def _acc_init(step, *refs):
    @pl.when(step == 0)
    def _():
        for r in refs:
            r[...] = jnp.zeros_like(r)


def _inproj_mixers_fwd(h, g, w, cw, wbd, ps):
    T, D = h.shape
    NQ = w.shape[1] - D
    NKV = 2 * NQ // 3
    DC = D // 4

    def body(h_ref, g_ref, w_ref, cw_ref, wbd_ref, ps_ref, xn_ref, pa_ref, qkv_ref, kvt_ref, yab_ref, halo):
        i = pl.program_id(0)
        _acc_init(i, halo)
        xn = _rms(h_ref[...], g_ref[...]).astype(BF16)
        xn_ref[...] = xn
        x = _dot(xn, w_ref[:, :D])
        pa_ref[...] = x
        qkv = _dot(xn, w_ref[:, D:])
        qkv_ref[...] = qkv.astype(BF16)
        kvt_ref[...] = qkv[:, NQ - NKV:].T.astype(BF16)
        hl = halo[...]
        gb, gc, u, pu = (x[:, k * DC:(k + 1) * DC] for k in range(4))
        z = gc * u
        z1, z2 = _conv_taps(z, hl[:, DC:2 * DC] * hl[:, 2 * DC:3 * DC])
        cwv = cw_ref[...]
        ya = gb * (cwv[2:3] * z + cwv[1:2] * z1 + cwv[0:1] * z2)
        d = _pool_d(pu, hl[:, 3 * DC:], _pool_count(i, DC))
        yb = _dot(d.astype(BF16), wbd_ref[...]) * ps_ref[...]
        yab_ref[...] = jnp.concatenate([ya, yb], axis=1)
        halo[...] = x[TB - HALO:, :]

    row = lambda n: pl.BlockSpec((TB, n), lambda i: (i, 0))
    return pl.pallas_call(
        body, name="inproj_mixers_fwd", grid=(T // TB,),
        in_specs=[row(D), _full((1, D)), _full(w.shape), _full((8, DC)), _full((DC, DC)), _full((1, DC))],
        out_specs=[row(D), row(D), row(NQ), pl.BlockSpec((NKV, TB), lambda i: (0, i)), row(2 * DC)],
        out_shape=[jax.ShapeDtypeStruct((T, D), BF16), jax.ShapeDtypeStruct((T, D), F32), jax.ShapeDtypeStruct((T, NQ), BF16),
                   jax.ShapeDtypeStruct((NKV, T), BF16), jax.ShapeDtypeStruct((T, 2 * DC), F32)],
        scratch_shapes=[pltpu.VMEM((HALO, D), F32)],
        compiler_params=_params("arbitrary"),
    )(h, g, w, cw, wbd, ps)


def _lane_groups(n, vals):
    lane = lax.broadcasted_iota(jnp.int32, (1, n), 1)
    q = n // 4
    return jnp.where(lane < q, vals[0], jnp.where(lane < 2 * q, vals[1], jnp.where(lane < 3 * q, vals[2], vals[3]))).astype(F32)


def _pick_group(levels, n):
    lane = lax.broadcasted_iota(jnp.int32, (1, n), 1)
    q = n // 4
    return jnp.where(lane < q, levels[0], jnp.where(lane < 2 * q, levels[1], jnp.where(lane < 3 * q, levels[2], levels[3])))


def _pool_count(blk, n):
    t1 = (blk * TB + 1 + lax.broadcasted_iota(jnp.int32, (TB, 1), 0)).astype(F32)
    return jnp.minimum(t1, _lane_groups(n, POOL_WINDOWS))


def _pool_d(pu, pu_halo, cnt):
    e = jnp.concatenate([pu_halo, pu], axis=0)
    s2 = e + pltpu.roll(e, 1, 0)
    s4 = s2 + pltpu.roll(s2, 2, 0)
    s8 = s4 + pltpu.roll(s4, 4, 0)
    s16 = s8 + pltpu.roll(s8, 8, 0)
    num = _pick_group([s2, s4, s8, s16], pu.shape[1])[HALO:]
    return num / cnt - pu


def _conv_taps(z, z_halo):
    e = jnp.concatenate([z_halo, z], axis=0)
    return pltpu.roll(e, 1, 0)[HALO:], pltpu.roll(e, 2, 0)[HALO:]


def _bias_bins(shape, col_dim):
    j = lax.broadcasted_iota(jnp.int32, shape, col_dim)
    b = lax.broadcasted_iota(jnp.int32, shape, 1 - col_dim)
    d = jnp.where(j < BAND, j, j - SKEW)
    live = jnp.logical_or(j < BAND, j > SKEW - CHUNK)
    bins = jnp.minimum(TB - d, REL_CLIP) + REL_CLIP
    return jnp.where(jnp.logical_and(live, bins == b), 1.0, 0.0).astype(F32)


def _skew_rows(x, left):
    row = lax.broadcasted_iota(jnp.int32, x.shape, 0)
    for b in range(SUB.bit_length() - 1):
        x = jnp.where(((row >> b) & 1) == 1, pltpu.roll(x, SKEW - (1 << b) if left else 1 << b, 1), x)
    return x


def _bias_build(rb):
    H = rb.shape[0]

    def body(rb_ref, o_ref):
        v = jnp.dot(rb_ref[...], _bias_bins((RBP, SKEW), 1), precision=lax.Precision.HIGHEST, preferred_element_type=F32)
        qc = lax.broadcasted_iota(jnp.int32, (SUB, BANDW), 0) >> (CHUNK.bit_length() - 1)
        kc = lax.broadcasted_iota(jnp.int32, (SUB, BANDW), 1) >> (CHUNK.bit_length() - 1)
        in_band = jnp.logical_and(kc >= qc, kc <= qc + LEFT_CHUNKS)
        for h in range(H):
            x = _skew_rows(jnp.broadcast_to(v[h:h + 1], (SUB, SKEW)), left=False)
            o_ref[h // 2, :, (h % 2) * SUB:(h % 2 + 1) * SUB] = jnp.where(in_band, x[:, :BANDW], NEG).T

    return pl.pallas_call(
        body, name="bias_build", in_specs=[_full(rb.shape)], out_specs=_full((H // 2, BANDW, 2 * SUB)),
        out_shape=jax.ShapeDtypeStruct((H // 2, BANDW, 2 * SUB), F32), grid=(1,),
        compiler_params=_params("arbitrary"),
    )(rb)


def _bias_fold(db, after):
    H = 2 * db.shape[0]

    def body(db_ref, after_ref, o_ref, sums):
        sums[...] = jnp.zeros_like(sums)
        for h in range(H):
            x = db_ref[h // 2, :, (h % 2) * SUB:(h % 2 + 1) * SUB].T
            x = _skew_rows(jnp.concatenate([x, jnp.zeros((SUB, SKEW - BANDW), F32)], axis=1), left=True)
            sums[h:h + 1, :] = jnp.sum(x, axis=0, keepdims=True)
        o_ref[...] = jnp.dot(sums[...], _bias_bins((SKEW, RBP), 0), precision=lax.Precision.HIGHEST, preferred_element_type=F32)

    return pl.pallas_call(
        body, name="bias_fold", grid=(1,), in_specs=[_full(db.shape), ANY], out_specs=_full((8, RBP)),
        out_shape=jax.ShapeDtypeStruct((8, RBP), F32), scratch_shapes=[pltpu.VMEM((8, SKEW), F32)],
        compiler_params=_params("arbitrary"),
    )(db, after)


def _both_heads(x):
    first = lax.broadcasted_iota(jnp.int32, (1, LANES), 1) < HEAD_DIM
    return jnp.concatenate([jnp.where(first, x, 0), jnp.where(first, 0, x)], axis=0)


def _own_head_rows(x2):
    n = x2.shape[1] // 2
    first = lax.broadcasted_iota(jnp.int32, (LANES, 1), 0) < HEAD_DIM
    return jnp.where(first, x2[:, :n], x2[:, n:])


def _key_tiles(s, first_block):
    return [t for t in range(s, s + BANDW // SUB) if not (first_block and t < TB // SUB)]


def _attn_fwd(qkv, kv_t, bias_t):
    T = qkv.shape[0]
    NP = qkv.shape[1] // (3 * LANES)
    NB = T // TB
    NS = TB // SUB
    scale = HEAD_DIM ** -0.5

    def body(q_ref, kc_ref, kp_ref, vtc_ref, vtp_ref, b_ref, o_ref, lse_ref):
        i = pl.program_id(1)

        def compute(first_block):
            q = q_ref[...] * scale
            kwin = jnp.concatenate([kp_ref[...], kc_ref[...]], axis=0)
            vt = jnp.concatenate([vtp_ref[...], vtc_ref[...]], axis=1)
            for s in range(NS):
                rows = slice(s * SUB, (s + 1) * SUB)
                tiles = _key_tiles(s, first_block)
                keys = slice(tiles[0] * SUB, (tiles[-1] + 1) * SUB)
                brows = slice((tiles[0] - s) * SUB, (tiles[-1] - s + 1) * SUB)
                q2 = _both_heads(q[rows])
                halves = []
                for a in range(2):
                    st = _dot_nt(kwin[keys], q2[a * SUB:(a + 1) * SUB]) + b_ref[0, brows, a * SUB:(a + 1) * SUB]
                    m = jnp.max(st, axis=0, keepdims=True)
                    p = jnp.exp(st - m)
                    l = jnp.sum(p, axis=0, keepdims=True)
                    halves.append(_dot(vt[:, keys], p.astype(BF16)) * (1.0 / l))
                    lse_ref[0, 0, a * NS + s:a * NS + s + 1, :] = m + jnp.log(l)
                o_ref[rows, :] = _own_head_rows(jnp.concatenate(halves, axis=1)).T

        pl.when(i == 0)(functools.partial(compute, True))
        pl.when(i > 0)(functools.partial(compute, False))

    prev = lambda i: jnp.maximum(i - 1, 0)
    return pl.pallas_call(
        body, name="attn_fwd", grid=(NP, NB),
        in_specs=[pl.BlockSpec((TB, LANES), lambda p, i: (i, p)),
                  pl.BlockSpec((TB, LANES), lambda p, i: (i, NP + p)),
                  pl.BlockSpec((TB, LANES), lambda p, i: (prev(i), NP + p)),
                  pl.BlockSpec((LANES, TB), lambda p, i: (NP + p, i)),
                  pl.BlockSpec((LANES, TB), lambda p, i: (NP + p, prev(i))),
                  pl.BlockSpec((1, BANDW, 2 * SUB), lambda p, i: (p, 0, 0))],
        out_specs=[pl.BlockSpec((TB, LANES), lambda p, i: (i, p)),
                   pl.BlockSpec((1, 1, 8, LANES), lambda p, i: (p, i, 0, 0))],
        out_shape=[jax.ShapeDtypeStruct((T, NP * LANES), F32), jax.ShapeDtypeStruct((NP, NB, 8, LANES), F32)],
        compiler_params=_params("parallel", "parallel"),
    )(qkv, qkv, qkv, kv_t, kv_t, bias_t)


def _group_bounds(D):
    return ((0, D // 4), (D // 4, D // 2), (D // 2, D))


def _ff_tiles(dff):
    half = dff // 2
    return [(slice(a, a + half), slice(dff + a, dff + a + half)) for a in (0, half)]


def _load_resident(step, *pairs_and_sems):
    @pl.when(step == 0)
    def _():
        cps = [pltpu.make_async_copy(src, dst, sem) for src, dst, sem in pairs_and_sems]
        for cp in cps:
            cp.start()
        for cp in cps:
            cp.wait()


def _layer_tail_fwd(yab, yc, gg, w_out, h, g_pm, g_pf, w_gu, w_dn, g_po):
    T, D = h.shape
    F2 = w_gu.shape[1]
    DFF = F2 // 2

    def body(yab_ref, yc_ref, gg_ref, wo_hbm, h_ref, gpm_ref, gpf_ref, wgu_hbm, wdn_hbm, gpo_ref,
             y_ref, mix_ref, h1_ref, hn_ref, gu_ref, ff_ref, ffo_ref, h2_ref, wo_v, wgu_v, wdn_v, sems):
        _load_resident(pl.program_id(0), (wo_hbm, wo_v, sems.at[0]), (wgu_hbm, wgu_v, sems.at[1]), (wdn_hbm, wdn_v, sems.at[2]))
        yraw = jnp.concatenate([yab_ref[...], yc_ref[...]], axis=1)
        ggv = gg_ref[...]
        y = jnp.concatenate([_rms(yraw[:, a:b], ggv[:, a:b]) for a, b in _group_bounds(D)], axis=1).astype(BF16)
        y_ref[...] = y
        mix = _dot(y, wo_v[...])
        mix_ref[...] = mix.astype(BF16)
        h1 = h_ref[...] + _rms(mix, gpm_ref[...])
        h1_ref[...] = h1
        hn = _rms(h1, gpf_ref[...]).astype(BF16)
        hn_ref[...] = hn
        ffo = jnp.zeros((TBF, D), F32)
        for sg, su in _ff_tiles(DFF):
            gate = _dot(hn, wgu_v[:, sg])
            up = _dot(hn, wgu_v[:, su])
            gu_ref[:, sg] = gate.astype(BF16)
            gu_ref[:, su] = up.astype(BF16)
            ff = (gate * jax.nn.sigmoid(gate) * up).astype(BF16)
            ff_ref[:, sg] = ff
            ffo = ffo + _dot(ff, wdn_v[sg, :])
        ffo_ref[...] = ffo.astype(BF16)
        h2_ref[...] = h1 + _rms(ffo, gpo_ref[...])

    row = lambda n: pl.BlockSpec((TBF, n), lambda i: (i, 0))
    gain = _full((1, D))
    f32, bf16 = (lambda n: jax.ShapeDtypeStruct((T, n), F32)), (lambda n: jax.ShapeDtypeStruct((T, n), BF16))
    return pl.pallas_call(
        body, name="layer_tail_fwd", grid=(T // TBF,),
        in_specs=[row(D // 2), row(D // 2), gain, ANY, row(D), gain, gain, ANY, ANY, gain],
        out_specs=[row(D), row(D), row(D), row(D), row(F2), row(DFF), row(D), row(D)],
        out_shape=[bf16(D), bf16(D), f32(D), bf16(D), bf16(F2), bf16(DFF), bf16(D), f32(D)],
        scratch_shapes=[pltpu.VMEM(w_out.shape, BF16), pltpu.VMEM(w_gu.shape, BF16), pltpu.VMEM(w_dn.shape, BF16),
                        pltpu.SemaphoreType.DMA((3,))],
        compiler_params=_params("arbitrary"),
    )(yab, yc, gg, w_out, h, g_pm, g_pf, w_gu, w_dn, g_po)


def _loss_grad(h, tgt):
    T, D = h.shape

    def body(h_ref, t_ref, dh_ref, loss_ref):
        _acc_init(pl.program_id(0), loss_ref)
        diff = h_ref[...] - t_ref[...]
        dh_ref[...] = diff * (1.0 / D)
        loss_ref[...] += 0.5 * jnp.sum(jnp.mean(diff * diff, axis=-1, keepdims=True))

    row = pl.BlockSpec((TB, D), lambda i: (i, 0))
    return pl.pallas_call(
        body, name="loss_grad", grid=(T // TB,),
        in_specs=[row, row], out_specs=[row, _full((8, LANES))],
        out_shape=[jax.ShapeDtypeStruct((T, D), F32), jax.ShapeDtypeStruct((8, LANES), F32)],
        compiler_params=_params("arbitrary"),
    )(h, tgt)


def _layer_tail_bwd(dh2, ffo, g_po, gu, w_dn, w_gu, h1, g_pf, mix, g_pm, w_out, yab, yc, gg):
    T, D = dh2.shape
    F2 = gu.shape[1]
    DFF = F2 // 2

    def body(dh_ref, ffo_ref, gpo_ref, gu_ref, wdn_hbm, wgu_hbm, h_ref, gpf_ref, mix_ref, gpm_ref, wo_hbm, yab_ref, yc_ref, gg_ref,
             dffo_ref, dgu_ref, dh1_ref, dmix_ref, dyab_ref, dyc_ref, dgpo_ref, dgpf_ref, dgpm_ref, dgg_ref,
             wdn_v, wgu_v, wo_v, sems):
        step = pl.program_id(0)
        _load_resident(step, (wdn_hbm, wdn_v, sems.at[0]), (wgu_hbm, wgu_v, sems.at[1]), (wo_hbm, wo_v, sems.at[2]))
        _acc_init(step, dgpo_ref, dgpf_ref, dgpm_ref, dgg_ref)
        dh = dh_ref[...]
        dffo, dg = _rms_bwd(dh, ffo_ref[...].astype(F32), gpo_ref[...])
        dgpo_ref[0:1, :] += dg
        dffo = dffo.astype(BF16)
        dffo_ref[...] = dffo
        dhn = jnp.zeros((TBF, D), F32)
        for sg, su in _ff_tiles(DFF):
            dff = _dot_nt(dffo, wdn_v[sg, :])
            gate, up = gu_ref[:, sg].astype(F32), gu_ref[:, su].astype(F32)
            sig = jax.nn.sigmoid(gate)
            dgate = (dff * up * (sig * (1.0 + gate * (1.0 - sig)))).astype(BF16)
            dup = (dff * (gate * sig)).astype(BF16)
            dgu_ref[:, sg] = dgate
            dgu_ref[:, su] = dup
            dhn = dhn + _dot_nt(dgate, wgu_v[:, sg]) + _dot_nt(dup, wgu_v[:, su])
        dx, dg = _rms_bwd(dhn, h_ref[...], gpf_ref[...])
        dgpf_ref[0:1, :] += dg
        dh1 = dh + dx
        dh1_ref[...] = dh1
        dmix, dg = _rms_bwd(dh1, mix_ref[...].astype(F32), gpm_ref[...])
        dgpm_ref[0:1, :] += dg
        dmix = dmix.astype(BF16)
        dmix_ref[...] = dmix
        dy = _dot_nt(dmix, wo_v[...])
        yraw = jnp.concatenate([yab_ref[...], yc_ref[...]], axis=1)
        ggv = gg_ref[...]
        parts = [_rms_bwd(dy[:, a:b], yraw[:, a:b], ggv[:, a:b]) for a, b in _group_bounds(D)]
        dgg_ref[0:1, :] += jnp.concatenate([p[1] for p in parts], axis=1)
        dyab_ref[...] = jnp.concatenate([parts[0][0], parts[1][0]], axis=1).astype(BF16)
        dyc_ref[...] = parts[2][0].astype(BF16)

    row = lambda n: pl.BlockSpec((TBF, n), lambda i: (i, 0))
    gain, acc = _full((1, D)), _full((8, D))
    f32, bf16 = (lambda n: jax.ShapeDtypeStruct((T, n), F32)), (lambda n: jax.ShapeDtypeStruct((T, n), BF16))
    acc_shape = jax.ShapeDtypeStruct((8, D), F32)
    return pl.pallas_call(
        body, name="layer_tail_bwd", grid=(T // TBF,),
        in_specs=[row(D), row(D), gain, row(F2), ANY, ANY, row(D), gain, row(D), gain, ANY, row(D // 2), row(D // 2), gain],
        out_specs=[row(D), row(F2), row(D), row(D), row(D // 2), row(D // 2), acc, acc, acc, acc],
        out_shape=[bf16(D), bf16(F2), f32(D), bf16(D), bf16(D // 2), bf16(D // 2), acc_shape, acc_shape, acc_shape, acc_shape],
        scratch_shapes=[pltpu.VMEM(w_dn.shape, BF16), pltpu.VMEM(w_gu.shape, BF16), pltpu.VMEM(w_out.shape, BF16),
                        pltpu.SemaphoreType.DMA((3,))],
        compiler_params=_params("arbitrary"),
    )(dh2, ffo, g_po, gu, w_dn, w_gu, h1, g_pf, mix, g_pm, w_out, yab, yc, gg)


def _attn_bwd(qkv, kv_t, dyc, yc, lse, bias_t, after):
    T = qkv.shape[0]
    NP = qkv.shape[1] // (3 * LANES)
    NB = T // TB
    NS = TB // SUB
    scale = HEAD_DIM ** -0.5

    def body(q_ref, kc_ref, kp_ref, vc_ref, vp_ref, ktc_ref, ktp_ref, do_ref, o_ref, lse_ref, b_ref, after_ref,
             dq_ref, dk_ref, dv_ref, db_ref, dk_carry, dv_carry, dkw, dvw):
        step = pl.program_id(1)
        i = NB - 1 - step
        _acc_init(step, dk_carry, dv_carry, db_ref)

        def compute(first_block):
            q = q_ref[...] * scale
            kwin = jnp.concatenate([kp_ref[...], kc_ref[...]], axis=0)
            vwin = jnp.concatenate([vp_ref[...], vc_ref[...]], axis=0)
            kt = jnp.concatenate([ktp_ref[...], ktc_ref[...]], axis=1)
            do = do_ref[...].astype(F32)
            dob = do.astype(BF16)
            prod = do * o_ref[...]
            first = lax.broadcasted_iota(jnp.int32, (1, LANES), 1) < HEAD_DIM
            ones = jnp.ones((8, LANES), F32)
            row_sums = lambda x: lax.dot_general(ones, x, (((1,), (1,)), ((), ())), precision=lax.Precision.HIGHEST,
                                                 preferred_element_type=F32)
            deltas = (row_sums(jnp.where(first, prod, 0.0)), row_sums(jnp.where(first, 0.0, prod)))
            written = set()
            for s in range(NS):
                rows = slice(s * SUB, (s + 1) * SUB)
                tiles = _key_tiles(s, first_block)
                keys = slice(tiles[0] * SUB, (tiles[-1] + 1) * SUB)
                brows = slice((tiles[0] - s) * SUB, (tiles[-1] - s + 1) * SUB)
                q2, do2 = _both_heads(q[rows]), _both_heads(dob[rows])
                lse = jnp.concatenate([lse_ref[0, 0, s:s + 1, :], lse_ref[0, 0, NS + s:NS + s + 1, :]], axis=1)
                delta = jnp.concatenate([deltas[0][0:1, rows], deltas[1][0:1, rows]], axis=1)
                p = jnp.exp(_dot_nt(kwin[keys], q2) + b_ref[0, brows, :] - lse)
                ds = p * (_dot_nt(vwin[keys], do2) - delta)
                db_ref[0, brows, :] += ds
                dsb = ds.astype(BF16)
                dk_c = _dot(dsb, q2)
                dv_c = _dot(p.astype(BF16), do2)
                dq_ref[rows, :] = (_own_head_rows(_dot(kt[:, keys], dsb)).T * scale).astype(BF16)
                for n, t in enumerate(tiles):
                    win, loc = slice(t * SUB, (t + 1) * SUB), slice(n * SUB, (n + 1) * SUB)
                    if t in written:
                        dkw[win, :] += dk_c[loc]
                        dvw[win, :] += dv_c[loc]
                    else:
                        dkw[win, :] = dk_c[loc]
                        dvw[win, :] = dv_c[loc]
                        written.add(t)
            dk_ref[...] = (dkw[TB:, :] + dk_carry[...]).astype(BF16)
            dv_ref[...] = (dvw[TB:, :] + dv_carry[...]).astype(BF16)
            if not first_block:
                dk_carry[...] = dkw[:TB, :]
                dv_carry[...] = dvw[:TB, :]

        pl.when(i == 0)(functools.partial(compute, True))
        pl.when(i > 0)(functools.partial(compute, False))

    blk = lambda s: NB - 1 - s
    prev = lambda s: jnp.maximum(NB - 2 - s, 0)
    rows = lambda which, off: pl.BlockSpec((TB, LANES), lambda p, s: (which(s), off + p))
    out = jax.ShapeDtypeStruct((T, NP * LANES), BF16)
    strip = pl.BlockSpec((1, BANDW, 2 * SUB), lambda p, s: (p, 0, 0))
    return pl.pallas_call(
        body, name="attn_bwd", grid=(NP, NB),
        in_specs=[rows(blk, 0), rows(blk, NP), rows(prev, NP), rows(blk, 2 * NP), rows(prev, 2 * NP),
                  pl.BlockSpec((LANES, TB), lambda p, s: (p, blk(s))), pl.BlockSpec((LANES, TB), lambda p, s: (p, prev(s))),
                  rows(blk, 0), rows(blk, 0), pl.BlockSpec((1, 1, 8, LANES), lambda p, s: (p, blk(s), 0, 0)), strip, ANY],
        out_specs=[rows(blk, 0), rows(blk, 0), rows(blk, 0), strip],
        out_shape=[out, out, out, jax.ShapeDtypeStruct((NP, BANDW, 2 * SUB), F32)],
        scratch_shapes=[pltpu.VMEM((TB, LANES), F32), pltpu.VMEM((TB, LANES), F32),
                        pltpu.VMEM((2 * TB, LANES), F32), pltpu.VMEM((2 * TB, LANES), F32)],
        compiler_params=_params("arbitrary", "arbitrary"),
    )(qkv, qkv, qkv, qkv, qkv, kv_t, kv_t, dyc, yc, lse, bias_t, after)


def _inproj_mixers_bwd(pa, dyab, cw, wbd, ps, dqkv, w, h, g, dh1):
    T, D = pa.shape
    DC = D // 4
    NB = T // TB
    N = TB + HALO
    widths = [p.shape[1] for p in dqkv]
    n = len(dqkv)

    def body(*refs):
        pa_ref, halo_ref, dy_ref, cw_ref, wbd_ref, ps_ref = refs[:6]
        parts = refs[6:6 + n]
        w_ref, h_ref, g_ref, dh1_ref, dpa_ref, dh_ref, dcw_ref, dwbd_ref, dps_ref, dg_ref, dc_carry, e_carry = refs[6 + n:]
        step = pl.program_id(0)
        i = NB - 1 - step
        _acc_init(step, dcw_ref, dwbd_ref, dps_ref, dg_ref, dc_carry, e_carry)
        x = pa_ref[...]
        hl = jnp.where(i > 0, halo_ref[...], 0.0)
        gb, gc, u, pu = (x[:, k * DC:(k + 1) * DC] for k in range(4))
        dy = dy_ref[...].astype(F32)
        dya, dyb = dy[:, :DC], dy[:, DC:]
        cwv = cw_ref[...]
        z = gc * u
        z1, z2 = _conv_taps(z, hl[:, DC:2 * DC] * hl[:, 2 * DC:3 * DC])
        dgb = dya * (cwv[2:3] * z + cwv[1:2] * z1 + cwv[0:1] * z2)
        dconv = dya * gb
        dcw_ref[0:1, :] += jnp.sum(dconv * z2, axis=0, keepdims=True)
        dcw_ref[1:2, :] += jnp.sum(dconv * z1, axis=0, keepdims=True)
        dcw_ref[2:3, :] += jnp.sum(dconv * z, axis=0, keepdims=True)
        ext = jnp.concatenate([dconv, dc_carry[...]], axis=0)
        dz = cwv[2:3] * dconv + cwv[1:2] * pltpu.roll(ext, N - 1, 0)[:TB] + cwv[0:1] * pltpu.roll(ext, N - 2, 0)[:TB]
        dc_carry[...] = dconv[:HALO]
        cnt = _pool_count(i, DC)
        d = _pool_d(pu, hl[:, 3 * DC:], cnt).astype(BF16)
        psv = ps_ref[...]
        wb = wbd_ref[...]
        dps_ref[0:1, :] += jnp.sum(dyb * _dot(d, wb), axis=0, keepdims=True)
        dys = (dyb * psv).astype(BF16)
        dwbd_ref[...] += _dot_tn(d, dys)
        dd = _dot_nt(dys, wb)
        e = dd / cnt
        ext = jnp.concatenate([e, e_carry[...]], axis=0)
        a2 = ext + pltpu.roll(ext, N - 1, 0)
        a4 = a2 + pltpu.roll(a2, N - 2, 0)
        a8 = a4 + pltpu.roll(a4, N - 4, 0)
        a16 = a8 + pltpu.roll(a8, N - 8, 0)
        dpu = _pick_group([a2, a4, a8, a16], DC)[:TB] - dd
        e_carry[...] = e[:HALO]
        dpa = jnp.concatenate([dgb, dz * u, dz * gc, dpu], axis=1).astype(BF16)
        dpa_ref[...] = dpa
        dxn, off = _dot_nt(dpa, w_ref[:, :D]), D
        for p_ref, wd in zip(parts, widths):
            dxn = dxn + _dot_nt(p_ref[...], w_ref[:, off:off + wd])
            off += wd
        dx, dg = _rms_bwd(dxn, h_ref[...], g_ref[...])
        dg_ref[0:1, :] += dg
        dh_ref[...] = dh1_ref[...] + dx

    blk = lambda m: pl.BlockSpec((TB, m), lambda s: (NB - 1 - s, 0))
    return pl.pallas_call(
        body, name="inproj_mixers_bwd", grid=(NB,),
        in_specs=[blk(D), pl.BlockSpec((HALO, D), lambda s: (jnp.maximum((NB - 1 - s) * (TB // HALO) - 1, 0), 0)),
                  blk(2 * DC), _full((8, DC)), _full((DC, DC)), _full((1, DC))] + [blk(wd) for wd in widths]
                 + [_full(w.shape), blk(D), _full((1, D)), blk(D)],
        out_specs=[blk(D), blk(D), _full((8, DC)), _full((DC, DC)), _full((8, DC)), _full((8, D))],
        out_shape=[jax.ShapeDtypeStruct((T, D), BF16), jax.ShapeDtypeStruct((T, D), F32), jax.ShapeDtypeStruct((8, DC), F32),
                   jax.ShapeDtypeStruct((DC, DC), F32), jax.ShapeDtypeStruct((8, DC), F32), jax.ShapeDtypeStruct((8, D), F32)],
        scratch_shapes=[pltpu.VMEM((HALO, DC), F32), pltpu.VMEM((HALO, DC), F32)],
        compiler_params=_params("arbitrary"),
    )(pa, pa, dyab, cw, wbd, ps, *dqkv, w, h, g, dh1)


def _wgrad(a, b, tk, tn, name):
    T, K = a.shape
    N = b.shape[1]

    def body(a_ref, b_ref, o_ref):
        _acc_init(pl.program_id(2), o_ref)
        o_ref[...] += _dot_tn(a_ref[...], b_ref[...])

    return pl.pallas_call(
        body, name=name, grid=(K // tk, N // tn, T // TBW),
        in_specs=[pl.BlockSpec((TBW, tk), lambda k, n, t: (t, k)), pl.BlockSpec((TBW, tn), lambda k, n, t: (t, n))],
        out_specs=pl.BlockSpec((tk, tn), lambda k, n, t: (k, n)),
        out_shape=jax.ShapeDtypeStruct((K, N), F32),
        compiler_params=_params("parallel", "parallel", "arbitrary"),
    )(a, b)


def _wgrad_concat(a, bs, name):
    T, K = a.shape
    widths = [b.shape[1] for b in bs]
    n = len(bs)

    def body(*refs):
        a_ref, b_refs, o_ref = refs[0], refs[1:1 + n], refs[1 + n]
        _acc_init(pl.program_id(0), o_ref)
        av, off = a_ref[...], 0
        for b_ref, wd in zip(b_refs, widths):
            o_ref[:, off:off + wd] += _dot_tn(av, b_ref[...])
            off += wd

    row = lambda m: pl.BlockSpec((TBW, m), lambda t: (t, 0))
    return pl.pallas_call(
        body, name=name, grid=(T // TBW,),
        in_specs=[row(K)] + [row(wd) for wd in widths],
        out_specs=_full((K, sum(widths))),
        out_shape=jax.ShapeDtypeStruct((K, sum(widths)), F32),
        compiler_params=_params("arbitrary"),
    )(a, *bs)


def _adamw(w, g, m, v, tr, name):
    R, C = w.shape

    def body(w_ref, g_ref, m_ref, v_ref, d_ref, nm_ref, nv_ref):
        gv = g_ref[...]
        nm = ADAM_B1 * m_ref[...] + (1.0 - ADAM_B1) * gv
        nv = ADAM_B2 * v_ref[...] + (1.0 - ADAM_B2) * (gv * gv)
        m_hat = nm / (1.0 - ADAM_B1 ** ADAM_STEP)
        v_hat = nv / (1.0 - ADAM_B2 ** ADAM_STEP)
        d_ref[...] = -ADAM_LR * (m_hat / (jnp.sqrt(v_hat) + ADAM_EPS) + ADAM_WD * w_ref[...])
        nm_ref[...] = nm
        nv_ref[...] = nv

    blk = pl.BlockSpec((tr, C), lambda i: (i, 0))
    out = jax.ShapeDtypeStruct((R, C), F32)
    return pl.pallas_call(
        body, name=name, grid=(R // tr,), in_specs=[blk] * 4, out_specs=[blk] * 3, out_shape=[out] * 3,
        compiler_params=_params("parallel"),
    )(w, g, m, v)


def _sum_slots(x, tr, name):
    n, R, C = x.shape

    def body(x_ref, o_ref):
        acc = x_ref[0]
        for k in range(1, n):
            acc = acc + x_ref[k]
        o_ref[...] = acc

    return pl.pallas_call(
        body, name=name, grid=(R // tr,),
        in_specs=[pl.BlockSpec((n, tr, C), lambda i: (0, i, 0))],
        out_specs=pl.BlockSpec((tr, C), lambda i: (i, 0)),
        out_shape=jax.ShapeDtypeStruct((R, C), F32),
        compiler_params=_params("parallel"),
    )(x)


def _add_half(view, other, c, name):
    if view.ndim == 3:
        _, R2, N = view.shape
        tr = 128
        grid = (R2 // tr,)
        in_specs = [pl.BlockSpec((1, tr, N), lambda i, c_ref: (c_ref[0], i, 0)), pl.BlockSpec((1, tr, N), lambda i, c_ref: (0, i, 0))]
        out_spec = pl.BlockSpec((tr, N), lambda i, c_ref: (i, 0))
        out_shape = jax.ShapeDtypeStruct((R2, N), BF16)

        def body(c_ref, a_ref, b_ref, o_ref):
            o_ref[...] = (a_ref[0] + b_ref[0]).astype(BF16)
    else:
        S, _, R2, C = view.shape
        grid = (S,)
        in_specs = [pl.BlockSpec((1, 1, R2, C), lambda s, c_ref: (s, c_ref[0], 0, 0)), pl.BlockSpec((1, 1, R2, C), lambda s, c_ref: (s, 0, 0, 0))]
        out_spec = pl.BlockSpec((1, R2, C), lambda s, c_ref: (s, 0, 0))
        out_shape = jax.ShapeDtypeStruct((S, R2, C), BF16)

        def body(c_ref, a_ref, b_ref, o_ref):
            o_ref[0] = (a_ref[0, 0] + b_ref[0, 0]).astype(BF16)

    return pl.pallas_call(
        body, name=name,
        grid_spec=pltpu.PrefetchScalarGridSpec(num_scalar_prefetch=1, grid=grid, in_specs=in_specs, out_specs=out_spec),
        out_shape=out_shape, compiler_params=_params("parallel"),
    )(c, view, other)


def _sum_chips(slots, part, kind, place, name):
    _, R2, C = slots.shape
    tr = min(R2, 128) if kind == "col" else R2

    def body(p_ref, s_ref, own_ref, o_ref):
        me = p_ref[0]
        own = own_ref[...] if kind == "col" else own_ref[0]
        acc = None
        for k in range(4):
            term = jnp.where(me == k, own, s_ref[k]).astype(F32)
            acc = term if acc is None else acc + term
        o_ref[0] = acc

    own_spec = (pl.BlockSpec((tr, C), lambda i, p: (i, p[0])) if kind == "col"
                else pl.BlockSpec((1, R2, C), lambda i, p: (p[0], 0, 0)))
    return pl.pallas_call(
        body, name=name,
        grid_spec=pltpu.PrefetchScalarGridSpec(
            num_scalar_prefetch=1, grid=(R2 // tr,),
            in_specs=[pl.BlockSpec((4, tr, C), lambda i, p: (0, i, 0)), own_spec],
            out_specs=pl.BlockSpec((1, tr, C), lambda i, p: (p[1], i, 0))),
        out_shape=jax.ShapeDtypeStruct((2, R2, C), F32), compiler_params=_params("parallel"),
    )(place, slots, part)


def _place():
    x, y, c = lax.axis_index("x"), lax.axis_index("y"), lax.axis_index("c")
    chips = [(1 - x, y), (x, 1 - y), (1 - x, 1 - y)]
    return x, y, c, 2 * x + y, chips


def _remote(src, dst, send_sems, recv_sems, k, to):
    return pltpu.make_async_remote_copy(src_ref=src, dst_ref=dst, send_sem=send_sems.at[k], recv_sem=recv_sems.at[k],
                                        device_id=to, device_id_type=MESH)


def _full_shape(shard_shape, kind):
    *lead, R, C = shard_shape
    return (*lead, R, 4 * C) if kind == "col" else (*lead, 4 * R, C)


def _slab(full_ref, shard_rc, kind, chip, half=None):
    R, C = shard_rc
    lead = (slice(None),) * (len(full_ref.shape) - 2)
    if kind == "col":
        rows = pl.ds(0, R) if half is None else pl.ds(half * (R // 2), R // 2)
        return full_ref.at[(*lead, rows, pl.ds(chip * C, C))]
    rows = pl.ds(chip * R, R) if half is None else pl.ds(chip * R + half * (R // 2), R // 2)
    return full_ref.at[(*lead, rows, slice(None))]


def _row_half(ref, half):
    R = ref.shape[-2]
    lead = (slice(None),) * (len(ref.shape) - 2)
    return ref.at[(*lead, pl.ds(half * (R // 2), R // 2), slice(None))]


def _allgather_weights(shards, kinds, after):
    n = len(shards)
    full_shapes = [_full_shape(s.shape, k) for s, k in zip(shards, kinds)]

    def body(*refs):
        ins, outs, (send_sems, recv_sems) = refs[:n], refs[n + 1:2 * n + 1], refs[2 * n + 1:]
        x, y, c, me, chips = _place()
        slab = lambda a, chip, half=None: _slab(outs[a], shards[a].shape[-2:], kinds[a], chip, half)
        my_half = lambda a: _row_half(ins[a], c)
        own = [_remote(ins[a], slab(a, me), send_sems, recv_sems, 6 * n + a, (x, y, 1 - c)) for a in range(n)]
        first = [_remote(my_half(a), slab(a, me, c), send_sems, recv_sems, j * n + a, (*chip, c))
                 for j, chip in enumerate(chips) for a in range(n)]
        for cp in first + own:
            cp.start()
        passed = []
        for j, chip in enumerate(chips):
            src = 2 * chip[0] + chip[1]
            for a in range(n):
                _remote(my_half(a), slab(a, src, c), send_sems, recv_sems, j * n + a, (x, y, c)).wait_recv()
                cp = _remote(slab(a, src, c), slab(a, src, c), send_sems, recv_sems, (3 + j) * n + a, (x, y, 1 - c))
                cp.start()
                passed.append(cp)
        for j, chip in enumerate(chips):
            src = 2 * chip[0] + chip[1]
            for a in range(n):
                _remote(my_half(a), slab(a, src, 1 - c), send_sems, recv_sems, (3 + j) * n + a, (x, y, c)).wait_recv()
        for cp in own:
            cp.wait_recv()
        for cp in first + passed + own:
            cp.wait_send()

    return pl.pallas_call(
        body, name="allgather_weights",
        in_specs=[ANY] * (n + 1), out_specs=[ANY] * n,
        out_shape=[jax.ShapeDtypeStruct(s, BF16) for s in full_shapes],
        scratch_shapes=[pltpu.SemaphoreType.DMA((7 * n,)), pltpu.SemaphoreType.DMA((7 * n,))],
        compiler_params=pltpu.CompilerParams(has_side_effects=True),
    )(*shards, after)


HBM = pl.BlockSpec(memory_space=pltpu.HBM)
SEM = pl.BlockSpec(memory_space=pltpu.SEMAPHORE)
DATAFLOW = pltpu.SideEffectType.DATAFLOW_SIDE_EFFECTING


def _split_start(name, bufs, plan, ncopies, after=None):
    nb = len(bufs)
    nin = nb + (after is not None)

    def body(*refs):
        send_sems, recv_sems, token = refs[nin], refs[nin + 1], refs[-1]
        for k, (src, dst, to) in enumerate(plan(refs[:nb])):
            _remote(src, dst, send_sems, recv_sems, k, to).start()
        token[...] = jnp.zeros_like(token)

    out = pl.pallas_call(
        body, name=name,
        out_shape=(pltpu.SemaphoreType.DMA((ncopies,)), pltpu.SemaphoreType.DMA((ncopies,)),
                   *[pltpu.HBM(b.shape, b.dtype) for b in bufs], jax.ShapeDtypeStruct((8, LANES), F32)),
        in_specs=[HBM] * nb + [ANY] * (nin - nb), out_specs=(SEM, SEM, *[HBM] * nb, pl.BlockSpec(memory_space=pltpu.VMEM)),
        input_output_aliases={i: 2 + i for i in range(nb)},
        compiler_params=pltpu.CompilerParams(has_side_effects=DATAFLOW),
    )(*[pltpu.with_memory_space_constraint(b, pltpu.HBM) for b in bufs], *([] if after is None else [after]))
    return out[0], out[1], list(out[2:2 + nb]), out[-1]


def _split_wait(name, send_sems, recv_sems, bufs, plan, after):
    nb = len(bufs)

    def body(*refs):
        s_sems, r_sems = refs[nb], refs[nb + 1]
        for k, (src, dst, to) in enumerate(plan(refs[:nb])):
            cp = _remote(src, dst, s_sems, r_sems, k, to)
            cp.wait_send()
            cp.wait_recv()

    return pl.pallas_call(
        body, name=name,
        out_shape=tuple(pltpu.HBM(b.shape, b.dtype) for b in bufs),
        in_specs=[HBM] * nb + [SEM, SEM, ANY], out_specs=tuple([HBM] * nb),
        input_output_aliases={i: i for i in range(nb)},
        compiler_params=pltpu.CompilerParams(has_side_effects=DATAFLOW),
    )(*bufs, send_sems, recv_sems, after)


def _gather_plan(n, shard_rcs, kinds):
    def plan(refs):
        x, y, c, me, chips = _place()
        out = []
        for a in range(n):
            shard, full = refs[a], refs[n + a]
            out.append((shard, _slab(full, shard_rcs[a], kinds[a], me), (x, y, 1 - c)))
            for chip in chips:
                for cc in (c, 1 - c):
                    out.append((_row_half(shard, c), _slab(full, shard_rcs[a], kinds[a], me, c), (*chip, cc)))
        return out
    return plan


def _exchange_plan(n, kinds):
    def plan(refs):
        x, y, c, me, chips = _place()
        out = []
        for a in range(n):
            part, slots = refs[a], refs[n + a]
            C = slots.shape[2]
            for chip in chips:
                dst = 2 * chip[0] + chip[1]
                src = part.at[:, pl.ds(dst * C, C)] if kinds[a] == "col" else part.at[dst]
                out.append((src, slots.at[me], (*chip, c)))
        return out
    return plan


def _swap_plan(ndims):
    n = len(ndims)

    def plan(refs):
        x, y, c, _, _ = _place()
        out = []
        for a in range(n):
            src = refs[a].at[pl.ds(1 - c, 1)] if ndims[a] == 3 else refs[a].at[:, pl.ds(1 - c, 1)]
            out.append((src, refs[n + a], (x, y, 1 - c)))
        return out
    return plan


def _join_halves(bufs):
    n = len(bufs)

    def body(*refs):
        outs, (send_sems, recv_sems) = refs[n:2 * n], refs[2 * n:]
        x, y, c, _, _ = _place()
        sends = [_remote(outs[a].at[c], outs[a].at[c], send_sems, recv_sems, a, (x, y, 1 - c)) for a in range(n)]
        for cp in sends:
            cp.start()
        for a in range(n):
            _remote(outs[a].at[c], outs[a].at[1 - c], send_sems, recv_sems, a, (x, y, c)).wait_recv()
        for cp in sends:
            cp.wait_send()

    return pl.pallas_call(
        body, name="grad_join_halves", in_specs=[ANY] * n, out_specs=[ANY] * n,
        out_shape=[jax.ShapeDtypeStruct(b.shape, F32) for b in bufs],
        input_output_aliases={a: a for a in range(n)},
        scratch_shapes=[pltpu.SemaphoreType.DMA((n,)), pltpu.SemaphoreType.DMA((n,))],
        compiler_params=pltpu.CompilerParams(has_side_effects=True),
    )(*bufs)


def _allgather_small(block):
    M, N = block.shape

    def body(x_ref, out_ref, send_sems, recv_sems, local_sem):
        x, y, c, _, chips = _place()
        me, sibling = (x, y, c), (x, y, 1 - c)

        def rows(px, py, pc):
            return out_ref.at[pl.ds((4 * px + 2 * py + pc) * M, M), :]

        def copy(k, blk, to, src=None):
            return _remote(rows(*blk) if src is None else src, rows(*blk), send_sems, recv_sems, k, to)

        mine = pltpu.make_async_copy(x_ref, rows(*me), local_sem)
        mine.start()
        first = [copy(0, me, sibling, src=x_ref)] + [copy(1 + j, me, (*chip, c), src=x_ref) for j, chip in enumerate(chips)]
        for cp in first:
            cp.start()
        passed = [copy(4 + j, (*chip, c), sibling) for j, chip in enumerate(chips)]
        for j, chip in enumerate(chips):
            copy(1 + j, (*chip, c), me).wait_recv()
            passed[j].start()
        copy(0, sibling, me).wait_recv()
        for j, chip in enumerate(chips):
            copy(4 + j, (*chip, 1 - c), me).wait_recv()
        for cp in first + passed:
            cp.wait_send()
        mine.wait()

    vm = pl.BlockSpec(memory_space=pltpu.VMEM)
    return pl.pallas_call(
        body, name="allgather_small", in_specs=[vm], out_specs=vm,
        out_shape=jax.ShapeDtypeStruct((8 * M, N), F32),
        scratch_shapes=[pltpu.SemaphoreType.DMA((7,)), pltpu.SemaphoreType.DMA((7,)), pltpu.SemaphoreType.DMA],
        compiler_params=pltpu.CompilerParams(has_side_effects=True, vmem_limit_bytes=VMEM_LIMIT),
    )(block)


def _pack(arrays):
    flat = jnp.concatenate([a.reshape(-1) for a in arrays])
    pad = (-flat.shape[0]) % (8 * LANES)
    return jnp.pad(flat, (0, pad)).reshape(-1, LANES)


def _unpack(packed, shapes):
    flat, out, off = packed.reshape(-1), [], 0
    for s in shapes:
        size = 1
        for d in s:
            size *= d
        out.append(flat[off:off + size].reshape(s))
        off += size
    return out


def _block_diag(pw):
    G, n, _ = pw.shape
    eye = jnp.eye(G, dtype=pw.dtype)
    return (eye[:, None, :, None] * pw[:, :, None, :]).reshape(G * n, G * n)


def _diag_blocks(m, G):
    n = m.shape[0] // G
    return jnp.stack([m[g * n:(g + 1) * n, g * n:(g + 1) * n] for g in range(G)])


def _pad_rows(a, rows):
    return jnp.pad(a, ((0, rows - a.shape[0]), (0, 0)))


def kernel(x, w_in, w_out, conv_w, pool_w, pool_scale, rel_bias, group_gain, pre_mix_g, post_mix_g, pre_ffn_g, post_ffn_g, w_gate_up, w_down, loss_target, m_w_in, m_w_out, m_conv_w, m_pool_w, m_pool_scale, m_rel_bias, m_group_gain, m_pre_mix_g, m_post_mix_g, m_pre_ffn_g, m_post_ffn_g, m_w_gate_up, m_w_down, v_w_in, v_w_out, v_conv_w, v_pool_w, v_pool_scale, v_rel_bias, v_group_gain, v_pre_mix_g, v_post_mix_g, v_pre_ffn_g, v_post_ffn_g, v_w_gate_up, v_w_down):
    L = w_in.shape[0]
    T, D = x.shape[1], x.shape[2]
    DC = D // 4
    NH = rel_bias.shape[1]
    NREL = rel_bias.shape[2]
    G = pool_w.shape[1]
    cs = conv_w.shape[2]
    assert TB == LEFT_CHUNKS * CHUNK and T % TB == 0 and D % (4 * LANES) == 0 and NH * HEAD_DIM == D // 2
    xi, yi, ci = lax.axis_index("x"), lax.axis_index("y"), lax.axis_index("c")
    chip = 2 * xi + yi

    kinds = ("col", "row", "col", "row")
    big = (w_in, w_out, w_gate_up, w_down)
    rcs = [w.shape[-2:] for w in big]
    conv_gathered = _allgather_small(_pack([conv_w]))
    conv_all = conv_gathered.reshape(8, -1)[:, :L * 3 * cs].reshape(4, 2, L, 3, cs)[:, 0]
    conv_full = jnp.moveaxis(conv_all, 0, 2).reshape(L, 3, 4 * cs)
    landing = lambda which: [lax.empty(_full_shape(rcs[k], kinds[k]), BF16) for k in which]
    wi0 = _allgather_weights([w_in[0:1].astype(BF16)], kinds[:1], conv_gathered)[0][0]
    rest_plan = _gather_plan(3, rcs[1:], kinds[1:])
    rest = _split_start("gather_start_0", [w[0].astype(BF16) for w in big[1:]] + landing((1, 2, 3)), rest_plan, 3 * 7, after=wi0)
    layer_plan = _gather_plan(4, rcs, kinds)
    full = [None] * L

    h = x[0]
    saved = []
    token = rest[3]
    for l in range(L):
        g_pre, g_pm, g_pf, g_po = (a[l][None] for a in (pre_mix_g, post_mix_g, pre_ffn_g, post_ffn_g))
        if 0 < l < L - 1:
            nxt = _split_start(f"gather_start_{l + 1}", [w[l + 1].astype(BF16) for w in big] + landing(range(4)), layer_plan, 4 * 7,
                               after=full[l][0])
            token = nxt[3]
        if l < L - 1:
            g_pre = g_pre + token[0:1, 0:1]
        gg, ps = group_gain[l][None], pool_scale[l][None]
        cw = _pad_rows(conv_full[l], 8)
        wbd = _block_diag(pool_w[l]).astype(BF16)
        bias = _bias_build(jnp.pad(rel_bias[l], ((0, 0), (0, RBP - NREL))))
        wi = wi0 if l == 0 else full[l][0]
        xn, pa, qkv, kv_t, yab = _inproj_mixers_fwd(h, g_pre, wi, cw, wbd, ps)
        yc, lse = _attn_fwd(qkv, kv_t, bias)
        if l == 0:
            full[0] = [wi0, *_split_wait("gather_wait_0", rest[0], rest[1], rest[2], rest_plan, yc)[3:6]]
            if L > 1:
                nxt = _split_start("gather_start_1", [w[1].astype(BF16) for w in big] + landing(range(4)), layer_plan, 4 * 7,
                                   after=full[0][1])
                g_pm = g_pm + nxt[3][0:1, 0:1]
        wi, wo, wgu, wdn = full[l]
        y, mix, h1, hn, gu, ff, ffo, h2 = _layer_tail_fwd(yab, yc, gg, wo, h, g_pm, g_pf, wgu, wdn, g_po)
        saved.append(dict(h=h, xn=xn, pa=pa, qkv=qkv, kv_t=kv_t, lse=lse, yab=yab, yc=yc, y=y, mix=mix, h1=h1, hn=hn, gu=gu, ff=ff, ffo=ffo,
                          cw=cw, wbd=wbd, bias=bias, ps=ps, gg=gg, g_pre=g_pre, g_pm=g_pm, g_pf=g_pf, g_po=g_po))
        h = h2
        if l + 1 < L:
            full[l + 1] = list(_split_wait(f"gather_wait_{l + 1}", nxt[0], nxt[1], nxt[2], layer_plan, h2)[4:8])

    dh, loss_tile = _loss_grad(h, loss_target[0])
    loss = lax.psum(loss_tile[0, 0], ("x", "y", "c"))

    xplan = _exchange_plan(4, kinds)
    splan = _swap_plan([3, 4, 3, 4])
    place = jnp.stack([chip, ci]).astype(jnp.int32)
    cvec = ci.reshape(1).astype(jnp.int32)
    small_grads = [None] * L
    shard_grads = [None] * L
    dbiases = [None] * L

    def start_exchange(lp, swap):
        thru = _split_wait(f"grad_swap_wait_{lp}", swap[0], swap[1], swap[2], splan, swap[4])
        chip_sums = [_add_half(v, t, cvec, "grad_add_half") for v, t in zip(thru[:4], thru[4:])]
        slots = [lax.empty((4, p.shape[0], p.shape[1] // 4) if k == "col" else p.shape, BF16) for p, k in zip(chip_sums, kinds)]
        ssem, rsem, bufs, tok = _split_start(f"grad_exchange_start_{lp}", chip_sums + slots, xplan, 4 * 3)
        return (lp, ssem, rsem, bufs), tok

    def finish_exchange(pending, after):
        lp, ssem, rsem, thru = pending
        landed = _split_wait(f"grad_exchange_wait_{lp}", ssem, rsem, thru, xplan, after)
        bufs = [_sum_chips(landed[4 + a], landed[a], kinds[a], place, "grad_sum_chips") for a in range(4)]
        shard_grads[lp] = [j.reshape(2 * j.shape[1], j.shape[2]) for j in _join_halves(bufs)]

    swap = None
    exchange = None
    for l in reversed(range(L)):
        s = saved[l]
        wi, wo, wgu, wdn = full[l]
        g_po = s["g_po"] if swap is None else s["g_po"] + swap[3][0:1, 0:1]
        dffo, dgu, dh1, dmix, dyab, dyc, dg_po, dg_pf, dg_pm, dgg = _layer_tail_bwd(
            dh, s["ffo"], g_po, s["gu"], wdn, wgu, s["h1"], s["g_pf"], s["mix"], s["g_pm"], wo, s["yab"], s["yc"], s["gg"])
        after = dyc
        if swap is not None:
            started, after = start_exchange(l + 1, (*swap[:4], dh1))
            if exchange is not None:
                finish_exchange(exchange, after)
            exchange = started
        dq, dk, dv, dbiases[l] = _attn_bwd(s["qkv"], s["kv_t"], dyc, s["yc"], s["lse"], s["bias"], after)
        dpa, dh, dcw, dwbd, dps, dg_pre = _inproj_mixers_bwd(s["pa"], dyab, s["cw"], s["wbd"], s["ps"], [dq, dk, dv], wi,
                                                             s["h"], s["g_pre"], dh1)
        dparts = [dpa, dq, dk, dv]
        F2, DFF = s["gu"].shape[1], s["ff"].shape[1]
        grads = [_wgrad_concat(s["xn"], dparts, "wgrad_in"),
                 _wgrad(s["y"], dmix, D, D, "wgrad_out"),
                 _wgrad(s["hn"], dgu, D, F2 // 4, "wgrad_gate_up"),
                 _wgrad(s["ff"], dffo, DFF // 2, D, "wgrad_down")]
        small_grads[l] = [dcw[:3], _diag_blocks(dwbd, G), dps[0], None, dgg[0], dg_pre[0], dg_pm[0], dg_pf[0], dg_po[0]]
        views = [g.reshape(2, g.shape[0] // 2, g.shape[1]) if k == "col" else g.reshape(4, 2, g.shape[0] // 8, g.shape[1])
                 for g, k in zip(grads, kinds)]
        lands = [lax.empty((1,) + v.shape[1:] if v.ndim == 3 else (v.shape[0], 1) + v.shape[2:], F32) for v in views]
        swap = _split_start(f"grad_swap_start_{l}", views + lands, splan, 4)
    last, tok = start_exchange(0, (*swap[:4], swap[3]))
    if exchange is not None:
        finish_exchange(exchange, tok)
    for l in range(L):
        small_grads[l][3] = _bias_fold(dbiases[l], tok)[:NH, :NREL]

    names_shapes = [(L, 3, 4 * cs), pool_w.shape, pool_scale.shape, rel_bias.shape, group_gain.shape,
                    pre_mix_g.shape, post_mix_g.shape, pre_ffn_g.shape, post_ffn_g.shape]
    small_stacked = [jnp.stack([small_grads[l][k] for l in range(L)]) for k in range(len(names_shapes))]
    packed = _pack(small_stacked)
    M = packed.shape[0]
    total = _sum_slots(_allgather_small(packed).reshape(8, M, LANES), M, "small_sum_devices")
    g_small = _unpack(total, names_shapes)
    g_small[0] = lax.dynamic_slice_in_dim(g_small[0], chip * cs, cs, axis=2)
    finish_exchange(last, total)
    g_big = [jnp.stack([shard_grads[l][k] for l in range(L)]) for k in range(4)]

    def adam_big(w, g, m, v, name):
        shp = w.shape
        two = lambda a: a.reshape(shp[0] * shp[1], shp[2])
        return [o.reshape(shp) for o in _adamw(two(w), two(g), two(m), two(v), 256, name)]

    upd_in = adam_big(w_in, g_big[0], m_w_in, v_w_in, "adamw_in")
    upd_out = adam_big(w_out, g_big[1], m_w_out, v_w_out, "adamw_out")
    upd_gu = adam_big(w_gate_up, g_big[2], m_w_gate_up, v_w_gate_up, "adamw_gate_up")
    upd_dn = adam_big(w_down, g_big[3], m_w_down, v_w_down, "adamw_down")

    small_w = [conv_w, pool_w, pool_scale, rel_bias, group_gain, pre_mix_g, post_mix_g, pre_ffn_g, post_ffn_g]
    small_m = [m_conv_w, m_pool_w, m_pool_scale, m_rel_bias, m_group_gain, m_pre_mix_g, m_post_mix_g, m_pre_ffn_g, m_post_ffn_g]
    small_v = [v_conv_w, v_pool_w, v_pool_scale, v_rel_bias, v_group_gain, v_pre_mix_g, v_post_mix_g, v_pre_ffn_g, v_post_ffn_g]
    pw_, pg_, pm_, pv_ = _pack(small_w), _pack(g_small), _pack(small_m), _pack(small_v)
    shapes = [w.shape for w in small_w]
    upd_small = [_unpack(o, shapes) for o in _adamw(pw_, pg_, pm_, pv_, pw_.shape[0], "adamw_small")]

    def ordered(big4, small9):
        return [big4[0], big4[1], *small9, big4[2], big4[3]]

    grads = ordered(g_big, g_small)
    outs = [ordered([upd_in[k], upd_out[k], upd_gu[k], upd_dn[k]], upd_small[k]) for k in range(3)]
    return (loss, dh[None], *grads, *outs[0], *outs[1], *outs[2])
```

```python
import functools

import jax
import jax.numpy as jnp
from jax import lax
from jax.experimental import pallas as pl
from jax.experimental.pallas import tpu as pltpu

F32, BF16 = jnp.float32, jnp.bfloat16
EPS = 1e-6
CHUNK = 64
LEFT_CHUNKS = 8
REL_CLIP = 128
HEAD_DIM = 64
LANES = 128
POOL_WINDOWS = (2, 4, 8, 16)
HALO = 16
TB = LEFT_CHUNKS * CHUNK
TBF = 256
TBW = 1024
BAND = (LEFT_CHUNKS + 1) * CHUNK
SUB = 2 * CHUNK
BANDW = SUB + LEFT_CHUNKS * CHUNK
SKEW = 768
NEG = -1e30
RBP = 384
VMEM_LIMIT = 56 * 1024 * 1024
ADAM_LR, ADAM_B1, ADAM_B2, ADAM_EPS, ADAM_WD, ADAM_STEP = 0.001, 0.9, 0.999, 1e-08, 0.01, 10
MESH = pl.DeviceIdType.MESH
ANY = pl.BlockSpec(memory_space=pl.ANY)


def _params(*sem):
    kw = dict(vmem_limit_bytes=VMEM_LIMIT)
    if sem:
        kw["dimension_semantics"] = sem
    return pltpu.CompilerParams(**kw)


def _dot(a, b):
    return jnp.dot(a, b, preferred_element_type=F32)


def _dot_nt(a, b):
    return lax.dot_general(a, b, (((1,), (1,)), ((), ())), preferred_element_type=F32)


def _dot_tn(a, b):
    return lax.dot_general(a, b, (((0,), (0,)), ((), ())), preferred_element_type=F32)


def _rms(x, g):
    r = lax.rsqrt(jnp.mean(x * x, axis=-1, keepdims=True) + EPS)
    return x * r * g


def _rms_bwd(dy, x, g):
    r = lax.rsqrt(jnp.mean(x * x, axis=-1, keepdims=True) + EPS)
    xh = x * r
    dxh = dy * g
    dx = r * (dxh - xh * jnp.mean(dxh * xh, axis=-1, keepdims=True))
    return dx, jnp.sum(dy * xh, axis=0, keepdims=True)


def _full(shape):
    return pl.BlockSpec(shape, lambda *_: (0,) * len(shape))


def _acc_init(step, *refs):
    @pl.when(step == 0)
    def _():
        for r in refs:
            r[...] = jnp.zeros_like(r)


def _inproj_mixers_fwd(h, g, w, cw, wbd, ps):
    T, D = h.shape
    NQ = w.shape[1] - D
    NKV = 2 * NQ // 3
    DC = D // 4

    def body(h_ref, g_ref, w_ref, cw_ref, wbd_ref, ps_ref, xn_ref, pa_ref, qkv_ref, kvt_ref, yab_ref, halo):
        i = pl.program_id(0)
        _acc_init(i, halo)
        xn = _rms(h_ref[...], g_ref[...]).astype(BF16)
        xn_ref[...] = xn
        x = _dot(xn, w_ref[:, :D])
        pa_ref[...] = x
        qkv = _dot(xn, w_ref[:, D:])
        qkv_ref[...] = qkv.astype(BF16)
        kvt_ref[...] = qkv[:, NQ - NKV:].T.astype(BF16)
        hl = halo[...]
        gb, gc, u, pu = (x[:, k * DC:(k + 1) * DC] for k in range(4))
        z = gc * u
        z1, z2 = _conv_taps(z, hl[:, DC:2 * DC] * hl[:, 2 * DC:3 * DC])
        cwv = cw_ref[...]
        ya = gb * (cwv[2:3] * z + cwv[1:2] * z1 + cwv[0:1] * z2)
        d = _pool_d(pu, hl[:, 3 * DC:], _pool_count(i, DC))
        yb = _dot(d.astype(BF16), wbd_ref[...]) * ps_ref[...]
        yab_ref[...] = jnp.concatenate([ya, yb], axis=1)
        halo[...] = x[TB - HALO:, :]

    row = lambda n: pl.BlockSpec((TB, n), lambda i: (i, 0))
    return pl.pallas_call(
        body, name="inproj_mixers_fwd", grid=(T // TB,),
        in_specs=[row(D), _full((1, D)), _full(w.shape), _full((8, DC)), _full((DC, DC)), _full((1, DC))],
        out_specs=[row(D), row(D), row(NQ), pl.BlockSpec((NKV, TB), lambda i: (0, i)), row(2 * DC)],
        out_shape=[jax.ShapeDtypeStruct((T, D), BF16), jax.ShapeDtypeStruct((T, D), F32), jax.ShapeDtypeStruct((T, NQ), BF16),
                   jax.ShapeDtypeStruct((NKV, T), BF16), jax.ShapeDtypeStruct((T, 2 * DC), F32)],
        scratch_shapes=[pltpu.VMEM((HALO, D), F32)],
        compiler_params=_params("arbitrary"),
    )(h, g, w, cw, wbd, ps)


def _lane_groups(n, vals):
    lane = lax.broadcasted_iota(jnp.int32, (1, n), 1)
    q = n // 4
    return jnp.where(lane < q, vals[0], jnp.where(lane < 2 * q, vals[1], jnp.where(lane < 3 * q, vals[2], vals[3]))).astype(F32)


def _pick_group(levels, n):
    lane = lax.broadcasted_iota(jnp.int32, (1, n), 1)
    q = n // 4
    return jnp.where(lane < q, levels[0], jnp.where(lane < 2 * q, levels[1], jnp.where(lane < 3 * q, levels[2], levels[3])))


def _pool_count(blk, n):
    t1 = (blk * TB + 1 + lax.broadcasted_iota(jnp.int32, (TB, 1), 0)).astype(F32)
    return jnp.minimum(t1, _lane_groups(n, POOL_WINDOWS))


def _pool_d(pu, pu_halo, cnt):
    e = jnp.concatenate([pu_halo, pu], axis=0)
    s2 = e + pltpu.roll(e, 1, 0)
    s4 = s2 + pltpu.roll(s2, 2, 0)
    s8 = s4 + pltpu.roll(s4, 4, 0)
    s16 = s8 + pltpu.roll(s8, 8, 0)
    num = _pick_group([s2, s4, s8, s16], pu.shape[1])[HALO:]
    return num / cnt - pu


def _conv_taps(z, z_halo):
    e = jnp.concatenate([z_halo, z], axis=0)
    return pltpu.roll(e, 1, 0)[HALO:], pltpu.roll(e, 2, 0)[HALO:]


def _bias_bins(shape, col_dim):
    j = lax.broadcasted_iota(jnp.int32, shape, col_dim)
    b = lax.broadcasted_iota(jnp.int32, shape, 1 - col_dim)
    d = jnp.where(j < BAND, j, j - SKEW)
    live = jnp.logical_or(j < BAND, j > SKEW - CHUNK)
    bins = jnp.minimum(TB - d, REL_CLIP) + REL_CLIP
    return jnp.where(jnp.logical_and(live, bins == b), 1.0, 0.0).astype(F32)


def _skew_rows(x, left):
    row = lax.broadcasted_iota(jnp.int32, x.shape, 0)
    for b in range(SUB.bit_length() - 1):
        x = jnp.where(((row >> b) & 1) == 1, pltpu.roll(x, SKEW - (1 << b) if left else 1 << b, 1), x)
    return x


def _bias_build(rb):
    H = rb.shape[0]

    def body(rb_ref, o_ref):
        v = jnp.dot(rb_ref[...], _bias_bins((RBP, SKEW), 1), precision=lax.Precision.HIGHEST, preferred_element_type=F32)
        qc = lax.broadcasted_iota(jnp.int32, (SUB, BANDW), 0) >> (CHUNK.bit_length() - 1)
        kc = lax.broadcasted_iota(jnp.int32, (SUB, BANDW), 1) >> (CHUNK.bit_length() - 1)
        in_band = jnp.logical_and(kc >= qc, kc <= qc + LEFT_CHUNKS)
        for h in range(H):
            x = _skew_rows(jnp.broadcast_to(v[h:h + 1], (SUB, SKEW)), left=False)
            o_ref[h // 2, :, (h % 2) * SUB:(h % 2 + 1) * SUB] = jnp.where(in_band, x[:, :BANDW], NEG).T

    return pl.pallas_call(
        body, name="bias_build", in_specs=[_full(rb.shape)], out_specs=_full((H // 2, BANDW, 2 * SUB)),
        out_shape=jax.ShapeDtypeStruct((H // 2, BANDW, 2 * SUB), F32), grid=(1,),
        compiler_params=_params("arbitrary"),
    )(rb)


def _bias_fold(db, after):
    H = 2 * db.shape[0]

    def body(db_ref, after_ref, o_ref, sums):
        sums[...] = jnp.zeros_like(sums)
        for h in range(H):
            x = db_ref[h // 2, :, (h % 2) * SUB:(h % 2 + 1) * SUB].T
            x = _skew_rows(jnp.concatenate([x, jnp.zeros((SUB, SKEW - BANDW), F32)], axis=1), left=True)
            sums[h:h + 1, :] = jnp.sum(x, axis=0, keepdims=True)
        o_ref[...] = jnp.dot(sums[...], _bias_bins((SKEW, RBP), 0), precision=lax.Precision.HIGHEST, preferred_element_type=F32)

    return pl.pallas_call(
        body, name="bias_fold", grid=(1,), in_specs=[_full(db.shape), ANY], out_specs=_full((8, RBP)),
        out_shape=jax.ShapeDtypeStruct((8, RBP), F32), scratch_shapes=[pltpu.VMEM((8, SKEW), F32)],
        compiler_params=_params("arbitrary"),
    )(db, after)


def _both_heads(x):
    first = lax.broadcasted_iota(jnp.int32, (1, LANES), 1) < HEAD_DIM
    return jnp.concatenate([jnp.where(first, x, 0), jnp.where(first, 0, x)], axis=0)


def _own_head_rows(x2):
    n = x2.shape[1] // 2
    first = lax.broadcasted_iota(jnp.int32, (LANES, 1), 0) < HEAD_DIM
    return jnp.where(first, x2[:, :n], x2[:, n:])


def _key_tiles(s, first_block):
    return [t for t in range(s, s + BANDW // SUB) if not (first_block and t < TB // SUB)]


def _attn_fwd(qkv, kv_t, bias_t):
    T = qkv.shape[0]
    NP = qkv.shape[1] // (3 * LANES)
    NB = T // TB
    NS = TB // SUB
    scale = HEAD_DIM ** -0.5

    def body(q_ref, kc_ref, kp_ref, vtc_ref, vtp_ref, b_ref, o_ref, lse_ref):
        i = pl.program_id(1)

        def compute(first_block):
            q = q_ref[...] * scale
            kwin = jnp.concatenate([kp_ref[...], kc_ref[...]], axis=0)
            vt = jnp.concatenate([vtp_ref[...], vtc_ref[...]], axis=1)
            for s in range(NS):
                rows = slice(s * SUB, (s + 1) * SUB)
                tiles = _key_tiles(s, first_block)
                keys = slice(tiles[0] * SUB, (tiles[-1] + 1) * SUB)
                brows = slice((tiles[0] - s) * SUB, (tiles[-1] - s + 1) * SUB)
                q2 = _both_heads(q[rows])
                halves = []
                for a in range(2):
                    st = _dot_nt(kwin[keys], q2[a * SUB:(a + 1) * SUB]) + b_ref[0, brows, a * SUB:(a + 1) * SUB]
                    m = jnp.max(st, axis=0, keepdims=True)
                    p = jnp.exp(st - m)
                    l = jnp.sum(p, axis=0, keepdims=True)
                    halves.append(_dot(vt[:, keys], p.astype(BF16)) * (1.0 / l))
                    lse_ref[0, 0, a * NS + s:a * NS + s + 1, :] = m + jnp.log(l)
                o_ref[rows, :] = _own_head_rows(jnp.concatenate(halves, axis=1)).T

        pl.when(i == 0)(functools.partial(compute, True))
        pl.when(i > 0)(functools.partial(compute, False))

    prev = lambda i: jnp.maximum(i - 1, 0)
    return pl.pallas_call(
        body, name="attn_fwd", grid=(NP, NB),
        in_specs=[pl.BlockSpec((TB, LANES), lambda p, i: (i, p)),
                  pl.BlockSpec((TB, LANES), lambda p, i: (i, NP + p)),
                  pl.BlockSpec((TB, LANES), lambda p, i: (prev(i), NP + p)),
                  pl.BlockSpec((LANES, TB), lambda p, i: (NP + p, i)),
                  pl.BlockSpec((LANES, TB), lambda p, i: (NP + p, prev(i))),
                  pl.BlockSpec((1, BANDW, 2 * SUB), lambda p, i: (p, 0, 0))],
        out_specs=[pl.BlockSpec((TB, LANES), lambda p, i: (i, p)),
                   pl.BlockSpec((1, 1, 8, LANES), lambda p, i: (p, i, 0, 0))],
        out_shape=[jax.ShapeDtypeStruct((T, NP * LANES), F32), jax.ShapeDtypeStruct((NP, NB, 8, LANES), F32)],
        compiler_params=_params("parallel", "parallel"),
    )(qkv, qkv, qkv, kv_t, kv_t, bias_t)


def _group_bounds(D):
    return ((0, D // 4), (D // 4, D // 2), (D // 2, D))


def _ff_tiles(dff):
    half = dff // 2
    return [(slice(a, a + half), slice(dff + a, dff + a + half)) for a in (0, half)]


def _load_resident(step, *pairs_and_sems):
    @pl.when(step == 0)
    def _():
        cps = [pltpu.make_async_copy(src, dst, sem) for src, dst, sem in pairs_and_sems]
        for cp in cps:
            cp.start()
        for cp in cps:
            cp.wait()


def _layer_tail_fwd(yab, yc, gg, w_out, h, g_pm, g_pf, w_gu, w_dn, g_po):
    T, D = h.shape
    F2 = w_gu.shape[1]
    DFF = F2 // 2

    def body(yab_ref, yc_ref, gg_ref, wo_hbm, h_ref, gpm_ref, gpf_ref, wgu_hbm, wdn_hbm, gpo_ref,
             y_ref, mix_ref, h1_ref, hn_ref, gu_ref, ff_ref, ffo_ref, h2_ref, wo_v, wgu_v, wdn_v, sems):
        _load_resident(pl.program_id(0), (wo_hbm, wo_v, sems.at[0]), (wgu_hbm, wgu_v, sems.at[1]), (wdn_hbm, wdn_v, sems.at[2]))
        yraw = jnp.concatenate([yab_ref[...], yc_ref[...]], axis=1)
        ggv = gg_ref[...]
        y = jnp.concatenate([_rms(yraw[:, a:b], ggv[:, a:b]) for a, b in _group_bounds(D)], axis=1).astype(BF16)
        y_ref[...] = y
        mix = _dot(y, wo_v[...])
        mix_ref[...] = mix.astype(BF16)
        h1 = h_ref[...] + _rms(mix, gpm_ref[...])
        h1_ref[...] = h1
        hn = _rms(h1, gpf_ref[...]).astype(BF16)
        hn_ref[...] = hn
        ffo = jnp.zeros((TBF, D), F32)
        for sg, su in _ff_tiles(DFF):
            gate = _dot(hn, wgu_v[:, sg])
            up = _dot(hn, wgu_v[:, su])
            gu_ref[:, sg] = gate.astype(BF16)
            gu_ref[:, su] = up.astype(BF16)
            ff = (gate * jax.nn.sigmoid(gate) * up).astype(BF16)
            ff_ref[:, sg] = ff
            ffo = ffo + _dot(ff, wdn_v[sg, :])
        ffo_ref[...] = ffo.astype(BF16)
        h2_ref[...] = h1 + _rms(ffo, gpo_ref[...])

    row = lambda n: pl.BlockSpec((TBF, n), lambda i: (i, 0))
    gain = _full((1, D))
    f32, bf16 = (lambda n: jax.ShapeDtypeStruct((T, n), F32)), (lambda n: jax.ShapeDtypeStruct((T, n), BF16))
    return pl.pallas_call(
        body, name="layer_tail_fwd", grid=(T // TBF,),
        in_specs=[row(D // 2), row(D // 2), gain, ANY, row(D), gain, gain, ANY, ANY, gain],
        out_specs=[row(D), row(D), row(D), row(D), row(F2), row(DFF), row(D), row(D)],
        out_shape=[bf16(D), bf16(D), f32(D), bf16(D), bf16(F2), bf16(DFF), bf16(D), f32(D)],
        scratch_shapes=[pltpu.VMEM(w_out.shape, BF16), pltpu.VMEM(w_gu.shape, BF16), pltpu.VMEM(w_dn.shape, BF16),
                        pltpu.SemaphoreType.DMA((3,))],
        compiler_params=_params("arbitrary"),
    )(yab, yc, gg, w_out, h, g_pm, g_pf, w_gu, w_dn, g_po)


def _loss_grad(h, tgt):
    T, D = h.shape

    def body(h_ref, t_ref, dh_ref, loss_ref):
        _acc_init(pl.program_id(0), loss_ref)
        diff = h_ref[...] - t_ref[...]
        dh_ref[...] = diff * (1.0 / D)
        loss_ref[...] += 0.5 * jnp.sum(jnp.mean(diff * diff, axis=-1, keepdims=True))

    row = pl.BlockSpec((TB, D), lambda i: (i, 0))
    return pl.pallas_call(
        body, name="loss_grad", grid=(T // TB,),
        in_specs=[row, row], out_specs=[row, _full((8, LANES))],
        out_shape=[jax.ShapeDtypeStruct((T, D), F32), jax.ShapeDtypeStruct((8, LANES), F32)],
        compiler_params=_params("arbitrary"),
    )(h, tgt)


def _layer_tail_bwd(dh2, ffo, g_po, gu, w_dn, w_gu, h1, g_pf, mix, g_pm, w_out, yab, yc, gg):
    T, D = dh2.shape
    F2 = gu.shape[1]
    DFF = F2 // 2

    def body(dh_ref, ffo_ref, gpo_ref, gu_ref, wdn_hbm, wgu_hbm, h_ref, gpf_ref, mix_ref, gpm_ref, wo_hbm, yab_ref, yc_ref, gg_ref,
             dffo_ref, dgu_ref, dh1_ref, dmix_ref, dyab_ref, dyc_ref, dgpo_ref, dgpf_ref, dgpm_ref, dgg_ref,
             wdn_v, wgu_v, wo_v, sems):
        step = pl.program_id(0)
        _load_resident(step, (wdn_hbm, wdn_v, sems.at[0]), (wgu_hbm, wgu_v, sems.at[1]), (wo_hbm, wo_v, sems.at[2]))
        _acc_init(step, dgpo_ref, dgpf_ref, dgpm_ref, dgg_ref)
        dh = dh_ref[...]
        dffo, dg = _rms_bwd(dh, ffo_ref[...].astype(F32), gpo_ref[...])
        dgpo_ref[0:1, :] += dg
        dffo = dffo.astype(BF16)
        dffo_ref[...] = dffo
        dhn = jnp.zeros((TBF, D), F32)
        for sg, su in _ff_tiles(DFF):
            dff = _dot_nt(dffo, wdn_v[sg, :])
            gate, up = gu_ref[:, sg].astype(F32), gu_ref[:, su].astype(F32)
            sig = jax.nn.sigmoid(gate)
            dgate = (dff * up * (sig * (1.0 + gate * (1.0 - sig)))).astype(BF16)
            dup = (dff * (gate * sig)).astype(BF16)
            dgu_ref[:, sg] = dgate
            dgu_ref[:, su] = dup
            dhn = dhn + _dot_nt(dgate, wgu_v[:, sg]) + _dot_nt(dup, wgu_v[:, su])
        dx, dg = _rms_bwd(dhn, h_ref[...], gpf_ref[...])
        dgpf_ref[0:1, :] += dg
        dh1 = dh + dx
        dh1_ref[...] = dh1
        dmix, dg = _rms_bwd(dh1, mix_ref[...].astype(F32), gpm_ref[...])
        dgpm_ref[0:1, :] += dg
        dmix = dmix.astype(BF16)
        dmix_ref[...] = dmix
        dy = _dot_nt(dmix, wo_v[...])
        yraw = jnp.concatenate([yab_ref[...], yc_ref[...]], axis=1)
        ggv = gg_ref[...]
        parts = [_rms_bwd(dy[:, a:b], yraw[:, a:b], ggv[:, a:b]) for a, b in _group_bounds(D)]
        dgg_ref[0:1, :] += jnp.concatenate([p[1] for p in parts], axis=1)
        dyab_ref[...] = jnp.concatenate([parts[0][0], parts[1][0]], axis=1).astype(BF16)
        dyc_ref[...] = parts[2][0].astype(BF16)

    row = lambda n: pl.BlockSpec((TBF, n), lambda i: (i, 0))
    gain, acc = _full((1, D)), _full((8, D))
    f32, bf16 = (lambda n: jax.ShapeDtypeStruct((T, n), F32)), (lambda n: jax.ShapeDtypeStruct((T, n), BF16))
    acc_shape = jax.ShapeDtypeStruct((8, D), F32)
    return pl.pallas_call(
        body, name="layer_tail_bwd", grid=(T // TBF,),
        in_specs=[row(D), row(D), gain, row(F2), ANY, ANY, row(D), gain, row(D), gain, ANY, row(D // 2), row(D // 2), gain],
        out_specs=[row(D), row(F2), row(D), row(D), row(D // 2), row(D // 2), acc, acc, acc, acc],
        out_shape=[bf16(D), bf16(F2), f32(D), bf16(D), bf16(D // 2), bf16(D // 2), acc_shape, acc_shape, acc_shape, acc_shape],
        scratch_shapes=[pltpu.VMEM(w_dn.shape, BF16), pltpu.VMEM(w_gu.shape, BF16), pltpu.VMEM(w_out.shape, BF16),
                        pltpu.SemaphoreType.DMA((3,))],
        compiler_params=_params("arbitrary"),
    )(dh2, ffo, g_po, gu, w_dn, w_gu, h1, g_pf, mix, g_pm, w_out, yab, yc, gg)


def _attn_bwd(qkv, kv_t, dyc, yc, lse, bias_t, after):
    T = qkv.shape[0]
    NP = qkv.shape[1] // (3 * LANES)
    NB = T // TB
    NS = TB // SUB
    scale = HEAD_DIM ** -0.5

    def body(q_ref, kc_ref, kp_ref, vc_ref, vp_ref, ktc_ref, ktp_ref, do_ref, o_ref, lse_ref, b_ref, after_ref,
             dq_ref, dk_ref, dv_ref, db_ref, dk_carry, dv_carry, dkw, dvw):
        step = pl.program_id(1)
        i = NB - 1 - step
        _acc_init(step, dk_carry, dv_carry, db_ref)

        def compute(first_block):
            q = q_ref[...] * scale
            kwin = jnp.concatenate([kp_ref[...], kc_ref[...]], axis=0)
            vwin = jnp.concatenate([vp_ref[...], vc_ref[...]], axis=0)
            kt = jnp.concatenate([ktp_ref[...], ktc_ref[...]], axis=1)
            do = do_ref[...].astype(F32)
            dob = do.astype(BF16)
            prod = do * o_ref[...]
            first = lax.broadcasted_iota(jnp.int32, (1, LANES), 1) < HEAD_DIM
            ones = jnp.ones((8, LANES), F32)
            row_sums = lambda x: lax.dot_general(ones, x, (((1,), (1,)), ((), ())), precision=lax.Precision.HIGHEST,
                                                 preferred_element_type=F32)
            deltas = (row_sums(jnp.where(first, prod, 0.0)), row_sums(jnp.where(first, 0.0, prod)))
            written = set()
            for s in range(NS):
                rows = slice(s * SUB, (s + 1) * SUB)
                tiles = _key_tiles(s, first_block)
                keys = slice(tiles[0] * SUB, (tiles[-1] + 1) * SUB)
                brows = slice((tiles[0] - s) * SUB, (tiles[-1] - s + 1) * SUB)
                q2, do2 = _both_heads(q[rows]), _both_heads(dob[rows])
                lse = jnp.concatenate([lse_ref[0, 0, s:s + 1, :], lse_ref[0, 0, NS + s:NS + s + 1, :]], axis=1)
                delta = jnp.concatenate([deltas[0][0:1, rows], deltas[1][0:1, rows]], axis=1)
                p = jnp.exp(_dot_nt(kwin[keys], q2) + b_ref[0, brows, :] - lse)
                ds = p * (_dot_nt(vwin[keys], do2) - delta)
                db_ref[0, brows, :] += ds
                dsb = ds.astype(BF16)
                dk_c = _dot(dsb, q2)
                dv_c = _dot(p.astype(BF16), do2)
                dq_ref[rows, :] = (_own_head_rows(_dot(kt[:, keys], dsb)).T * scale).astype(BF16)
                for n, t in enumerate(tiles):
                    win, loc = slice(t * SUB, (t + 1) * SUB), slice(n * SUB, (n + 1) * SUB)
                    if t in written:
                        dkw[win, :] += dk_c[loc]
                        dvw[win, :] += dv_c[loc]
                    else:
                        dkw[win, :] = dk_c[loc]
                        dvw[win, :] = dv_c[loc]
                        written.add(t)
            dk_ref[...] = (dkw[TB:, :] + dk_carry[...]).astype(BF16)
            dv_ref[...] = (dvw[TB:, :] + dv_carry[...]).astype(BF16)
            if not first_block:
                dk_carry[...] = dkw[:TB, :]
                dv_carry[...] = dvw[:TB, :]

        pl.when(i == 0)(functools.partial(compute, True))
        pl.when(i > 0)(functools.partial(compute, False))

    blk = lambda s: NB - 1 - s
    prev = lambda s: jnp.maximum(NB - 2 - s, 0)
    rows = lambda which, off: pl.BlockSpec((TB, LANES), lambda p, s: (which(s), off + p))
    out = jax.ShapeDtypeStruct((T, NP * LANES), BF16)
    strip = pl.BlockSpec((1, BANDW, 2 * SUB), lambda p, s: (p, 0, 0))
    return pl.pallas_call(
        body, name="attn_bwd", grid=(NP, NB),
        in_specs=[rows(blk, 0), rows(blk, NP), rows(prev, NP), rows(blk, 2 * NP), rows(prev, 2 * NP),
                  pl.BlockSpec((LANES, TB), lambda p, s: (p, blk(s))), pl.BlockSpec((LANES, TB), lambda p, s: (p, prev(s))),
                  rows(blk, 0), rows(blk, 0), pl.BlockSpec((1, 1, 8, LANES), lambda p, s: (p, blk(s), 0, 0)), strip, ANY],
        out_specs=[rows(blk, 0), rows(blk, 0), rows(blk, 0), strip],
        out_shape=[out, out, out, jax.ShapeDtypeStruct((NP, BANDW, 2 * SUB), F32)],
        scratch_shapes=[pltpu.VMEM((TB, LANES), F32), pltpu.VMEM((TB, LANES), F32),
                        pltpu.VMEM((2 * TB, LANES), F32), pltpu.VMEM((2 * TB, LANES), F32)],
        compiler_params=_params("arbitrary", "arbitrary"),
    )(qkv, qkv, qkv, qkv, qkv, kv_t, kv_t, dyc, yc, lse, bias_t, after)


def _inproj_mixers_bwd(pa, dyab, cw, wbd, ps, dqkv, w, h, g, dh1):
    T, D = pa.shape
    DC = D // 4
    NB = T // TB
    N = TB + HALO
    widths = [p.shape[1] for p in dqkv]
    n = len(dqkv)

    def body(*refs):
        pa_ref, halo_ref, dy_ref, cw_ref, wbd_ref, ps_ref = refs[:6]
        parts = refs[6:6 + n]
        w_ref, h_ref, g_ref, dh1_ref, dpa_ref, dh_ref, dcw_ref, dwbd_ref, dps_ref, dg_ref, dc_carry, e_carry = refs[6 + n:]
        step = pl.program_id(0)
        i = NB - 1 - step
        _acc_init(step, dcw_ref, dwbd_ref, dps_ref, dg_ref, dc_carry, e_carry)
        x = pa_ref[...]
        hl = jnp.where(i > 0, halo_ref[...], 0.0)
        gb, gc, u, pu = (x[:, k * DC:(k + 1) * DC] for k in range(4))
        dy = dy_ref[...].astype(F32)
        dya, dyb = dy[:, :DC], dy[:, DC:]
        cwv = cw_ref[...]
        z = gc * u
        z1, z2 = _conv_taps(z, hl[:, DC:2 * DC] * hl[:, 2 * DC:3 * DC])
        dgb = dya * (cwv[2:3] * z + cwv[1:2] * z1 + cwv[0:1] * z2)
        dconv = dya * gb
        dcw_ref[0:1, :] += jnp.sum(dconv * z2, axis=0, keepdims=True)
        dcw_ref[1:2, :] += jnp.sum(dconv * z1, axis=0, keepdims=True)
        dcw_ref[2:3, :] += jnp.sum(dconv * z, axis=0, keepdims=True)
        ext = jnp.concatenate([dconv, dc_carry[...]], axis=0)
        dz = cwv[2:3] * dconv + cwv[1:2] * pltpu.roll(ext, N - 1, 0)[:TB] + cwv[0:1] * pltpu.roll(ext, N - 2, 0)[:TB]
        dc_carry[...] = dconv[:HALO]
        cnt = _pool_count(i, DC)
        d = _pool_d(pu, hl[:, 3 * DC:], cnt).astype(BF16)
        psv = ps_ref[...]
        wb = wbd_ref[...]
        dps_ref[0:1, :] += jnp.sum(dyb * _dot(d, wb), axis=0, keepdims=True)
        dys = (dyb * psv).astype(BF16)
        dwbd_ref[...] += _dot_tn(d, dys)
        dd = _dot_nt(dys, wb)
        e = dd / cnt
        ext = jnp.concatenate([e, e_carry[...]], axis=0)
        a2 = ext + pltpu.roll(ext, N - 1, 0)
        a4 = a2 + pltpu.roll(a2, N - 2, 0)
        a8 = a4 + pltpu.roll(a4, N - 4, 0)
        a16 = a8 + pltpu.roll(a8, N - 8, 0)
        dpu = _pick_group([a2, a4, a8, a16], DC)[:TB] - dd
        e_carry[...] = e[:HALO]
        dpa = jnp.concatenate([dgb, dz * u, dz * gc, dpu], axis=1).astype(BF16)
        dpa_ref[...] = dpa
        dxn, off = _dot_nt(dpa, w_ref[:, :D]), D
        for p_ref, wd in zip(parts, widths):
            dxn = dxn + _dot_nt(p_ref[...], w_ref[:, off:off + wd])
            off += wd
        dx, dg = _rms_bwd(dxn, h_ref[...], g_ref[...])
        dg_ref[0:1, :] += dg
        dh_ref[...] = dh1_ref[...] + dx

    blk = lambda m: pl.BlockSpec((TB, m), lambda s: (NB - 1 - s, 0))
    return pl.pallas_call(
        body, name="inproj_mixers_bwd", grid=(NB,),
        in_specs=[blk(D), pl.BlockSpec((HALO, D), lambda s: (jnp.maximum((NB - 1 - s) * (TB // HALO) - 1, 0), 0)),
                  blk(2 * DC), _full((8, DC)), _full((DC, DC)), _full((1, DC))] + [blk(wd) for wd in widths]
                 + [_full(w.shape), blk(D), _full((1, D)), blk(D)],
        out_specs=[blk(D), blk(D), _full((8, DC)), _full((DC, DC)), _full((8, DC)), _full((8, D))],
        out_shape=[jax.ShapeDtypeStruct((T, D), BF16), jax.ShapeDtypeStruct((T, D), F32), jax.ShapeDtypeStruct((8, DC), F32),
                   jax.ShapeDtypeStruct((DC, DC), F32), jax.ShapeDtypeStruct((8, DC), F32), jax.ShapeDtypeStruct((8, D), F32)],
        scratch_shapes=[pltpu.VMEM((HALO, DC), F32), pltpu.VMEM((HALO, DC), F32)],
        compiler_params=_params("arbitrary"),
    )(pa, pa, dyab, cw, wbd, ps, *dqkv, w, h, g, dh1)


def _wgrad(a, b, tk, tn, name):
    T, K = a.shape
    N = b.shape[1]

    def body(a_ref, b_ref, o_ref):
        _acc_init(pl.program_id(2), o_ref)
        o_ref[...] += _dot_tn(a_ref[...], b_ref[...])

    return pl.pallas_call(
        body, name=name, grid=(K // tk, N // tn, T // TBW),
        in_specs=[pl.BlockSpec((TBW, tk), lambda k, n, t: (t, k)), pl.BlockSpec((TBW, tn), lambda k, n, t: (t, n))],
        out_specs=pl.BlockSpec((tk, tn), lambda k, n, t: (k, n)),
        out_shape=jax.ShapeDtypeStruct((K, N), F32),
        compiler_params=_params("parallel", "parallel", "arbitrary"),
    )(a, b)


def _wgrad_concat(a, bs, name):
    T, K = a.shape
    widths = [b.shape[1] for b in bs]
    n = len(bs)

    def body(*refs):
        a_ref, b_refs, o_ref = refs[0], refs[1:1 + n], refs[1 + n]
        _acc_init(pl.program_id(0), o_ref)
        av, off = a_ref[...], 0
        for b_ref, wd in zip(b_refs, widths):
            o_ref[:, off:off + wd] += _dot_tn(av, b_ref[...])
            off += wd

    row = lambda m: pl.BlockSpec((TBW, m), lambda t: (t, 0))
    return pl.pallas_call(
        body, name=name, grid=(T // TBW,),
        in_specs=[row(K)] + [row(wd) for wd in widths],
        out_specs=_full((K, sum(widths))),
        out_shape=jax.ShapeDtypeStruct((K, sum(widths)), F32),
        compiler_params=_params("arbitrary"),
    )(a, *bs)


def _adamw(w, g, m, v, tr, name):
    R, C = w.shape

    def body(w_ref, g_ref, m_ref, v_ref, d_ref, nm_ref, nv_ref):
        gv = g_ref[...]
        nm = ADAM_B1 * m_ref[...] + (1.0 - ADAM_B1) * gv
        nv = ADAM_B2 * v_ref[...] + (1.0 - ADAM_B2) * (gv * gv)
        m_hat = nm / (1.0 - ADAM_B1 ** ADAM_STEP)
        v_hat = nv / (1.0 - ADAM_B2 ** ADAM_STEP)
        d_ref[...] = -ADAM_LR * (m_hat / (jnp.sqrt(v_hat) + ADAM_EPS) + ADAM_WD * w_ref[...])
        nm_ref[...] = nm
        nv_ref[...] = nv

    blk = pl.BlockSpec((tr, C), lambda i: (i, 0))
    out = jax.ShapeDtypeStruct((R, C), F32)
    return pl.pallas_call(
        body, name=name, grid=(R // tr,), in_specs=[blk] * 4, out_specs=[blk] * 3, out_shape=[out] * 3,
        compiler_params=_params("parallel"),
    )(w, g, m, v)


def _sum_slots(x, tr, name):
    n, R, C = x.shape

    def body(x_ref, o_ref):
        acc = x_ref[0]
        for k in range(1, n):
            acc = acc + x_ref[k]
        o_ref[...] = acc

    return pl.pallas_call(
        body, name=name, grid=(R // tr,),
        in_specs=[pl.BlockSpec((n, tr, C), lambda i: (0, i, 0))],
        out_specs=pl.BlockSpec((tr, C), lambda i: (i, 0)),
        out_shape=jax.ShapeDtypeStruct((R, C), F32),
        compiler_params=_params("parallel"),
    )(x)


def _add_half(view, other, c, name):
    if view.ndim == 3:
        _, R2, N = view.shape
        tr = 128
        grid = (R2 // tr,)
        in_specs = [pl.BlockSpec((1, tr, N), lambda i, c_ref: (c_ref[0], i, 0)), pl.BlockSpec((1, tr, N), lambda i, c_ref: (0, i, 0))]
        out_spec = pl.BlockSpec((tr, N), lambda i, c_ref: (i, 0))
        out_shape = jax.ShapeDtypeStruct((R2, N), BF16)

        def body(c_ref, a_ref, b_ref, o_ref):
            o_ref[...] = (a_ref[0] + b_ref[0]).astype(BF16)
    else:
        S, _, R2, C = view.shape
        grid = (S,)
        in_specs = [pl.BlockSpec((1, 1, R2, C), lambda s, c_ref: (s, c_ref[0], 0, 0)), pl.BlockSpec((1, 1, R2, C), lambda s, c_ref: (s, 0, 0, 0))]
        out_spec = pl.BlockSpec((1, R2, C), lambda s, c_ref: (s, 0, 0))
        out_shape = jax.ShapeDtypeStruct((S, R2, C), BF16)

        def body(c_ref, a_ref, b_ref, o_ref):
            o_ref[0] = (a_ref[0, 0] + b_ref[0, 0]).astype(BF16)

    return pl.pallas_call(
        body, name=name,
        grid_spec=pltpu.PrefetchScalarGridSpec(num_scalar_prefetch=1, grid=grid, in_specs=in_specs, out_specs=out_spec),
        out_shape=out_shape, compiler_params=_params("parallel"),
    )(c, view, other)


def _sum_chips(slots, part, kind, place, name):
    _, R2, C = slots.shape
    tr = min(R2, 128) if kind == "col" else R2

    def body(p_ref, s_ref, own_ref, o_ref):
        me = p_ref[0]
        own = own_ref[...] if kind == "col" else own_ref[0]
        acc = None
        for k in range(4):
            term = jnp.where(me == k, own, s_ref[k]).astype(F32)
            acc = term if acc is None else acc + term
        o_ref[0] = acc

    own_spec = (pl.BlockSpec((tr, C), lambda i, p: (i, p[0])) if kind == "col"
                else pl.BlockSpec((1, R2, C), lambda i, p: (p[0], 0, 0)))
    return pl.pallas_call(
        body, name=name,
        grid_spec=pltpu.PrefetchScalarGridSpec(
            num_scalar_prefetch=1, grid=(R2 // tr,),
            in_specs=[pl.BlockSpec((4, tr, C), lambda i, p: (0, i, 0)), own_spec],
            out_specs=pl.BlockSpec((1, tr, C), lambda i, p: (p[1], i, 0))),
        out_shape=jax.ShapeDtypeStruct((2, R2, C), F32), compiler_params=_params("parallel"),
    )(place, slots, part)


def _place():
    x, y, c = lax.axis_index("x"), lax.axis_index("y"), lax.axis_index("c")
    chips = [(1 - x, y), (x, 1 - y), (1 - x, 1 - y)]
    return x, y, c, 2 * x + y, chips


def _remote(src, dst, send_sems, recv_sems, k, to):
    return pltpu.make_async_remote_copy(src_ref=src, dst_ref=dst, send_sem=send_sems.at[k], recv_sem=recv_sems.at[k],
                                        device_id=to, device_id_type=MESH)


def _full_shape(shard_shape, kind):
    *lead, R, C = shard_shape
    return (*lead, R, 4 * C) if kind == "col" else (*lead, 4 * R, C)


def _slab(full_ref, shard_rc, kind, chip, half=None):
    R, C = shard_rc
    lead = (slice(None),) * (len(full_ref.shape) - 2)
    if kind == "col":
        rows = pl.ds(0, R) if half is None else pl.ds(half * (R // 2), R // 2)
        return full_ref.at[(*lead, rows, pl.ds(chip * C, C))]
    rows = pl.ds(chip * R, R) if half is None else pl.ds(chip * R + half * (R // 2), R // 2)
    return full_ref.at[(*lead, rows, slice(None))]


def _row_half(ref, half):
    R = ref.shape[-2]
    lead = (slice(None),) * (len(ref.shape) - 2)
    return ref.at[(*lead, pl.ds(half * (R // 2), R // 2), slice(None))]


def _allgather_weights(shards, kinds, after):
    n = len(shards)
    full_shapes = [_full_shape(s.shape, k) for s, k in zip(shards, kinds)]

    def body(*refs):
        ins, outs, (send_sems, recv_sems) = refs[:n], refs[n + 1:2 * n + 1], refs[2 * n + 1:]
        x, y, c, me, chips = _place()
        slab = lambda a, chip, half=None: _slab(outs[a], shards[a].shape[-2:], kinds[a], chip, half)
        my_half = lambda a: _row_half(ins[a], c)
        own = [_remote(ins[a], slab(a, me), send_sems, recv_sems, 6 * n + a, (x, y, 1 - c)) for a in range(n)]
        first = [_remote(my_half(a), slab(a, me, c), send_sems, recv_sems, j * n + a, (*chip, c))
                 for j, chip in enumerate(chips) for a in range(n)]
        for cp in first + own:
            cp.start()
        passed = []
        for j, chip in enumerate(chips):
            src = 2 * chip[0] + chip[1]
            for a in range(n):
                _remote(my_half(a), slab(a, src, c), send_sems, recv_sems, j * n + a, (x, y, c)).wait_recv()
                cp = _remote(slab(a, src, c), slab(a, src, c), send_sems, recv_sems, (3 + j) * n + a, (x, y, 1 - c))
                cp.start()
                passed.append(cp)
        for j, chip in enumerate(chips):
            src = 2 * chip[0] + chip[1]
            for a in range(n):
                _remote(my_half(a), slab(a, src, 1 - c), send_sems, recv_sems, (3 + j) * n + a, (x, y, c)).wait_recv()
        for cp in own:
            cp.wait_recv()
        for cp in first + passed + own:
            cp.wait_send()

    return pl.pallas_call(
        body, name="allgather_weights",
        in_specs=[ANY] * (n + 1), out_specs=[ANY] * n,
        out_shape=[jax.ShapeDtypeStruct(s, BF16) for s in full_shapes],
        scratch_shapes=[pltpu.SemaphoreType.DMA((7 * n,)), pltpu.SemaphoreType.DMA((7 * n,))],
        compiler_params=pltpu.CompilerParams(has_side_effects=True),
    )(*shards, after)


HBM = pl.BlockSpec(memory_space=pltpu.HBM)
SEM = pl.BlockSpec(memory_space=pltpu.SEMAPHORE)
DATAFLOW = pltpu.SideEffectType.DATAFLOW_SIDE_EFFECTING


def _split_start(name, bufs, plan, ncopies, after=None):
    nb = len(bufs)
    nin = nb + (after is not None)

    def body(*refs):
        send_sems, recv_sems, token = refs[nin], refs[nin + 1], refs[-1]
        for k, (src, dst, to) in enumerate(plan(refs[:nb])):
            _remote(src, dst, send_sems, recv_sems, k, to).start()
        token[...] = jnp.zeros_like(token)

    out = pl.pallas_call(
        body, name=name,
        out_shape=(pltpu.SemaphoreType.DMA((ncopies,)), pltpu.SemaphoreType.DMA((ncopies,)),
                   *[pltpu.HBM(b.shape, b.dtype) for b in bufs], jax.ShapeDtypeStruct((8, LANES), F32)),
        in_specs=[HBM] * nb + [ANY] * (nin - nb), out_specs=(SEM, SEM, *[HBM] * nb, pl.BlockSpec(memory_space=pltpu.VMEM)),
        input_output_aliases={i: 2 + i for i in range(nb)},
        compiler_params=pltpu.CompilerParams(has_side_effects=DATAFLOW),
    )(*[pltpu.with_memory_space_constraint(b, pltpu.HBM) for b in bufs], *([] if after is None else [after]))
    return out[0], out[1], list(out[2:2 + nb]), out[-1]


def _split_wait(name, send_sems, recv_sems, bufs, plan, after):
    nb = len(bufs)

    def body(*refs):
        s_sems, r_sems = refs[nb], refs[nb + 1]
        for k, (src, dst, to) in enumerate(plan(refs[:nb])):
            cp = _remote(src, dst, s_sems, r_sems, k, to)
            cp.wait_send()
            cp.wait_recv()

    return pl.pallas_call(
        body, name=name,
        out_shape=tuple(pltpu.HBM(b.shape, b.dtype) for b in bufs),
        in_specs=[HBM] * nb + [SEM, SEM, ANY], out_specs=tuple([HBM] * nb),
        input_output_aliases={i: i for i in range(nb)},
        compiler_params=pltpu.CompilerParams(has_side_effects=DATAFLOW),
    )(*bufs, send_sems, recv_sems, after)


def _gather_plan(n, shard_rcs, kinds):
    def plan(refs):
        x, y, c, me, chips = _place()
        out = []
        for a in range(n):
            shard, full = refs[a], refs[n + a]
            out.append((shard, _slab(full, shard_rcs[a], kinds[a], me), (x, y, 1 - c)))
            for chip in chips:
                for cc in (c, 1 - c):
                    out.append((_row_half(shard, c), _slab(full, shard_rcs[a], kinds[a], me, c), (*chip, cc)))
        return out
    return plan


def _exchange_plan(n, kinds):
    def plan(refs):
        x, y, c, me, chips = _place()
        out = []
        for a in range(n):
            part, slots = refs[a], refs[n + a]
            C = slots.shape[2]
            for chip in chips:
                dst = 2 * chip[0] + chip[1]
                src = part.at[:, pl.ds(dst * C, C)] if kinds[a] == "col" else part.at[dst]
                out.append((src, slots.at[me], (*chip, c)))
        return out
    return plan


def _swap_plan(ndims):
    n = len(ndims)

    def plan(refs):
        x, y, c, _, _ = _place()
        out = []
        for a in range(n):
            src = refs[a].at[pl.ds(1 - c, 1)] if ndims[a] == 3 else refs[a].at[:, pl.ds(1 - c, 1)]
            out.append((src, refs[n + a], (x, y, 1 - c)))
        return out
    return plan


def _join_halves(bufs):
    n = len(bufs)

    def body(*refs):
        outs, (send_sems, recv_sems) = refs[n:2 * n], refs[2 * n:]
        x, y, c, _, _ = _place()
        sends = [_remote(outs[a].at[c], outs[a].at[c], send_sems, recv_sems, a, (x, y, 1 - c)) for a in range(n)]
        for cp in sends:
            cp.start()
        for a in range(n):
            _remote(outs[a].at[c], outs[a].at[1 - c], send_sems, recv_sems, a, (x, y, c)).wait_recv()
        for cp in sends:
            cp.wait_send()

    return pl.pallas_call(
        body, name="grad_join_halves", in_specs=[ANY] * n, out_specs=[ANY] * n,
        out_shape=[jax.ShapeDtypeStruct(b.shape, F32) for b in bufs],
        input_output_aliases={a: a for a in range(n)},
        scratch_shapes=[pltpu.SemaphoreType.DMA((n,)), pltpu.SemaphoreType.DMA((n,))],
        compiler_params=pltpu.CompilerParams(has_side_effects=True),
    )(*bufs)


def _allgather_small(block):
    M, N = block.shape

    def body(x_ref, out_ref, send_sems, recv_sems, local_sem):
        x, y, c, _, chips = _place()
        me, sibling = (x, y, c), (x, y, 1 - c)

        def rows(px, py, pc):
            return out_ref.at[pl.ds((4 * px + 2 * py + pc) * M, M), :]

        def copy(k, blk, to, src=None):
            return _remote(rows(*blk) if src is None else src, rows(*blk), send_sems, recv_sems, k, to)

        mine = pltpu.make_async_copy(x_ref, rows(*me), local_sem)
        mine.start()
        first = [copy(0, me, sibling, src=x_ref)] + [copy(1 + j, me, (*chip, c), src=x_ref) for j, chip in enumerate(chips)]
        for cp in first:
            cp.start()
        passed = [copy(4 + j, (*chip, c), sibling) for j, chip in enumerate(chips)]
        for j, chip in enumerate(chips):
            copy(1 + j, (*chip, c), me).wait_recv()
            passed[j].start()
        copy(0, sibling, me).wait_recv()
        for j, chip in enumerate(chips):
            copy(4 + j, (*chip, 1 - c), me).wait_recv()
        for cp in first + passed:
            cp.wait_send()
        mine.wait()

    vm = pl.BlockSpec(memory_space=pltpu.VMEM)
    return pl.pallas_call(
        body, name="allgather_small", in_specs=[vm], out_specs=vm,
        out_shape=jax.ShapeDtypeStruct((8 * M, N), F32),
        scratch_shapes=[pltpu.SemaphoreType.DMA((7,)), pltpu.SemaphoreType.DMA((7,)), pltpu.SemaphoreType.DMA],
        compiler_params=pltpu.CompilerParams(has_side_effects=True, vmem_limit_bytes=VMEM_LIMIT),
    )(block)


def _pack(arrays):
    flat = jnp.concatenate([a.reshape(-1) for a in arrays])
    pad = (-flat.shape[0]) % (8 * LANES)
    return jnp.pad(flat, (0, pad)).reshape(-1, LANES)


def _unpack(packed, shapes):
    flat, out, off = packed.reshape(-1), [], 0
    for s in shapes:
        size = 1
        for d in s:
            size *= d
        out.append(flat[off:off + size].reshape(s))
        off += size
    return out


def _block_diag(pw):
    G, n, _ = pw.shape
    eye = jnp.eye(G, dtype=pw.dtype)
    return (eye[:, None, :, None] * pw[:, :, None, :]).reshape(G * n, G * n)


def _diag_blocks(m, G):
    n = m.shape[0] // G
    return jnp.stack([m[g * n:(g + 1) * n, g * n:(g + 1) * n] for g in range(G)])


def _pad_rows(a, rows):
    return jnp.pad(a, ((0, rows - a.shape[0]), (0, 0)))


def kernel(x, w_in, w_out, conv_w, pool_w, pool_scale, rel_bias, group_gain, pre_mix_g, post_mix_g, pre_ffn_g, post_ffn_g, w_gate_up, w_down, loss_target, m_w_in, m_w_out, m_conv_w, m_pool_w, m_pool_scale, m_rel_bias, m_group_gain, m_pre_mix_g, m_post_mix_g, m_pre_ffn_g, m_post_ffn_g, m_w_gate_up, m_w_down, v_w_in, v_w_out, v_conv_w, v_pool_w, v_pool_scale, v_rel_bias, v_group_gain, v_pre_mix_g, v_post_mix_g, v_pre_ffn_g, v_post_ffn_g, v_w_gate_up, v_w_down):
    L = w_in.shape[0]
    T, D = x.shape[1], x.shape[2]
    DC = D // 4
    NH = rel_bias.shape[1]
    NREL = rel_bias.shape[2]
    G = pool_w.shape[1]
    cs = conv_w.shape[2]
    assert TB == LEFT_CHUNKS * CHUNK and T % TB == 0 and D % (4 * LANES) == 0 and NH * HEAD_DIM == D // 2
    xi, yi, ci = lax.axis_index("x"), lax.axis_index("y"), lax.axis_index("c")
    chip = 2 * xi + yi

    kinds = ("col", "row", "col", "row")
    big = (w_in, w_out, w_gate_up, w_down)
    rcs = [w.shape[-2:] for w in big]
    conv_gathered = _allgather_small(_pack([conv_w]))
    conv_all = conv_gathered.reshape(8, -1)[:, :L * 3 * cs].reshape(4, 2, L, 3, cs)[:, 0]
    conv_full = jnp.moveaxis(conv_all, 0, 2).reshape(L, 3, 4 * cs)
    landing = lambda which: [lax.empty(_full_shape(rcs[k], kinds[k]), BF16) for k in which]
    wi0 = _allgather_weights([w_in[0:1].astype(BF16)], kinds[:1], conv_gathered)[0][0]
    rest_plan = _gather_plan(3, rcs[1:], kinds[1:])
    rest = _split_start("gather_start_0", [w[0].astype(BF16) for w in big[1:]] + landing((1, 2, 3)), rest_plan, 3 * 7, after=wi0)
    layer_plan = _gather_plan(4, rcs, kinds)
    full = [None] * L

    h = x[0]
    saved = []
    token = rest[3]
    for l in range(L):
        g_pre, g_pm, g_pf, g_po = (a[l][None] for a in (pre_mix_g, post_mix_g, pre_ffn_g, post_ffn_g))
        if 0 < l < L - 1:
            nxt = _split_start(f"gather_start_{l + 1}", [w[l + 1].astype(BF16) for w in big] + landing(range(4)), layer_plan, 4 * 7,
                               after=full[l][0])
            token = nxt[3]
        if l < L - 1:
            g_pre = g_pre + token[0:1, 0:1]
        gg, ps = group_gain[l][None], pool_scale[l][None]
        cw = _pad_rows(conv_full[l], 8)
        wbd = _block_diag(pool_w[l]).astype(BF16)
        bias = _bias_build(jnp.pad(rel_bias[l], ((0, 0), (0, RBP - NREL))))
        wi = wi0 if l == 0 else full[l][0]
        xn, pa, qkv, kv_t, yab = _inproj_mixers_fwd(h, g_pre, wi, cw, wbd, ps)
        yc, lse = _attn_fwd(qkv, kv_t, bias)
        if l == 0:
            full[0] = [wi0, *_split_wait("gather_wait_0", rest[0], rest[1], rest[2], rest_plan, yc)[3:6]]
            if L > 1:
                nxt = _split_start("gather_start_1", [w[1].astype(BF16) for w in big] + landing(range(4)), layer_plan, 4 * 7,
                                   after=full[0][1])
                g_pm = g_pm + nxt[3][0:1, 0:1]
        wi, wo, wgu, wdn = full[l]
        y, mix, h1, hn, gu, ff, ffo, h2 = _layer_tail_fwd(yab, yc, gg, wo, h, g_pm, g_pf, wgu, wdn, g_po)
        saved.append(dict(h=h, xn=xn, pa=pa, qkv=qkv, kv_t=kv_t, lse=lse, yab=yab, yc=yc, y=y, mix=mix, h1=h1, hn=hn, gu=gu, ff=ff, ffo=ffo,
                          cw=cw, wbd=wbd, bias=bias, ps=ps, gg=gg, g_pre=g_pre, g_pm=g_pm, g_pf=g_pf, g_po=g_po))
        h = h2
        if l + 1 < L:
            full[l + 1] = list(_split_wait(f"gather_wait_{l + 1}", nxt[0], nxt[1], nxt[2], layer_plan, h2)[4:8])

    dh, loss_tile = _loss_grad(h, loss_target[0])
    loss = lax.psum(loss_tile[0, 0], ("x", "y", "c"))

    place = jnp.stack([chip, ci]).astype(jnp.int32)
    cvec = ci.reshape(1).astype(jnp.int32)
    small_grads = [None] * L
    shard_grads = [[None] * 4 for _ in range(L)]
    dbiases = [None] * L

    def start_swap(tag, grads, ks, after=None):
        views = [g.reshape(2, g.shape[0] // 2, g.shape[1]) if kinds[k] == "col" else g.reshape(4, 2, g.shape[0] // 8, g.shape[1])
                 for g, k in zip(grads, ks)]
        lands = [lax.empty((1,) + v.shape[1:] if v.ndim == 3 else (v.shape[0], 1) + v.shape[2:], F32) for v in views]
        plan = _swap_plan([v.ndim for v in views])
        ssem, rsem, bufs, tok = _split_start(f"grad_swap_start_{tag}", views + lands, plan, len(ks), after=after)
        return (tag, ks, plan, ssem, rsem, bufs), tok

    def start_exchange(swap, after):
        tag, ks, plan, ssem, rsem, bufs = swap
        n = len(ks)
        thru = _split_wait(f"grad_swap_wait_{tag}", ssem, rsem, bufs, plan, after)
        chip_sums = [_add_half(v, t, cvec, "grad_add_half") for v, t in zip(thru[:n], thru[n:])]
        slots = [lax.empty((4, p.shape[0], p.shape[1] // 4) if kinds[k] == "col" else p.shape, BF16) for p, k in zip(chip_sums, ks)]
        xplan = _exchange_plan(n, [kinds[k] for k in ks])
        ssem, rsem, bufs, tok = _split_start(f"grad_exchange_start_{tag}", chip_sums + slots, xplan, 3 * n)
        return (tag, ks, xplan, ssem, rsem, bufs), tok

    def finish_exchange(layer, pending, after):
        tag, ks, xplan, ssem, rsem, bufs = pending
        n = len(ks)
        landed = _split_wait(f"grad_exchange_wait_{tag}", ssem, rsem, bufs, xplan, after)
        halves = [_sum_chips(landed[n + a], landed[a], kinds[k], place, "grad_sum_chips") for a, k in enumerate(ks)]
        for k, j in zip(ks, _join_halves(halves)):
            shard_grads[layer][k] = j.reshape(2 * j.shape[1], j.shape[2])

    swap = None
    swap_tok = None
    exchange = None
    for l in reversed(range(L)):
        s = saved[l]
        wi, wo, wgu, wdn = full[l]
        g_po = s["g_po"] if swap is None else s["g_po"] + swap_tok[0:1, 0:1]
        dffo, dgu, dh1, dmix, dyab, dyc, dg_po, dg_pf, dg_pm, dgg = _layer_tail_bwd(
            dh, s["ffo"], g_po, s["gu"], wdn, wgu, s["h1"], s["g_pf"], s["mix"], s["g_pm"], wo, s["yab"], s["yc"], s["gg"])
        after = dyc
        if swap is not None:
            started, after = start_exchange(swap, dh1)
            if exchange is not None:
                finish_exchange(l + 2, exchange, after)
            exchange = started
        F2, DFF = s["gu"].shape[1], s["ff"].shape[1]
        ffn_grads = lambda: [_wgrad(s["hn"], dgu, D, F2 // 4, "wgrad_gate_up"), _wgrad(s["ff"], dffo, DFF // 2, D, "wgrad_down")]
        g_pre = s["g_pre"]
        if l == 0:
            ffn_swap, after = start_swap("0f", ffn_grads(), (2, 3), after=after)
        dq, dk, dv, dbiases[l] = _attn_bwd(s["qkv"], s["kv_t"], dyc, s["yc"], s["lse"], s["bias"], after)
        if l == 0:
            ffn_exchange, tok = start_exchange(ffn_swap, dq)
            g_pre = g_pre + tok[0:1, 0:1]
        dpa, dh, dcw, dwbd, dps, dg_pre = _inproj_mixers_bwd(s["pa"], dyab, s["cw"], s["wbd"], s["ps"], [dq, dk, dv], wi,
                                                             s["h"], g_pre, dh1)
        dparts = [dpa, dq, dk, dv]
        mixer_grads = [_wgrad_concat(s["xn"], dparts, "wgrad_in"), _wgrad(s["y"], dmix, D, D, "wgrad_out")]
        small_grads[l] = [dcw[:3], _diag_blocks(dwbd, G), dps[0], None, dgg[0], dg_pre[0], dg_pm[0], dg_pf[0], dg_po[0]]
        if l == 0:
            swap, swap_tok = start_swap("0m", mixer_grads, (0, 1))
        else:
            swap, swap_tok = start_swap(str(l), mixer_grads + ffn_grads(), (0, 1, 2, 3))
    last, tok = start_exchange(swap, swap_tok)
    if exchange is not None:
        finish_exchange(1, exchange, tok)
    for l in range(L):
        small_grads[l][3] = _bias_fold(dbiases[l], tok)[:NH, :NREL]

    names_shapes = [(L, 3, 4 * cs), pool_w.shape, pool_scale.shape, rel_bias.shape, group_gain.shape,
                    pre_mix_g.shape, post_mix_g.shape, pre_ffn_g.shape, post_ffn_g.shape]
    small_stacked = [jnp.stack([small_grads[l][k] for l in range(L)]) for k in range(len(names_shapes))]
    packed = _pack(small_stacked)
    M = packed.shape[0]
    total = _sum_slots(_allgather_small(packed).reshape(8, M, LANES), M, "small_sum_devices")
    g_small = _unpack(total, names_shapes)
    g_small[0] = lax.dynamic_slice_in_dim(g_small[0], chip * cs, cs, axis=2)
    finish_exchange(0, ffn_exchange, total)
    finish_exchange(0, last, total)
    g_big = [jnp.stack([shard_grads[l][k] for l in range(L)]) for k in range(4)]

    def adam_big(w, g, m, v, name):
        shp = w.shape
        two = lambda a: a.reshape(shp[0] * shp[1], shp[2])
        return [o.reshape(shp) for o in _adamw(two(w), two(g), two(m), two(v), 256, name)]

    upd_in = adam_big(w_in, g_big[0], m_w_in, v_w_in, "adamw_in")
    upd_out = adam_big(w_out, g_big[1], m_w_out, v_w_out, "adamw_out")
    upd_gu = adam_big(w_gate_up, g_big[2], m_w_gate_up, v_w_gate_up, "adamw_gate_up")
    upd_dn = adam_big(w_down, g_big[3], m_w_down, v_w_down, "adamw_down")

    small_w = [conv_w, pool_w, pool_scale, rel_bias, group_gain, pre_mix_g, post_mix_g, pre_ffn_g, post_ffn_g]
    small_m = [m_conv_w, m_pool_w, m_pool_scale, m_rel_bias, m_group_gain, m_pre_mix_g, m_post_mix_g, m_pre_ffn_g, m_post_ffn_g]
    small_v = [v_conv_w, v_pool_w, v_pool_scale, v_rel_bias, v_group_gain, v_pre_mix_g, v_post_mix_g, v_pre_ffn_g, v_post_ffn_g]
    pw_, pg_, pm_, pv_ = _pack(small_w), _pack(g_small), _pack(small_m), _pack(small_v)
    shapes = [w.shape for w in small_w]
    upd_small = [_unpack(o, shapes) for o in _adamw(pw_, pg_, pm_, pv_, pw_.shape[0], "adamw_small")]

    def ordered(big4, small9):
        return [big4[0], big4[1], *small9, big4[2], big4[3]]

    grads = ordered(g_big, g_small)
    outs = [ordered([upd_in[k], upd_out[k], upd_gu[k], upd_dn[k]], upd_small[k]) for k in range(3)]
    return (loss, dh[None], *grads, *outs[0], *outs[1], *outs[2])
```

```python
import functools

import jax
import jax.numpy as jnp
from jax import lax
from jax.experimental import pallas as pl
from jax.experimental.pallas import tpu as pltpu

F32, BF16 = jnp.float32, jnp.bfloat16
EPS = 1e-6
CHUNK = 64
LEFT_CHUNKS = 8
REL_CLIP = 128
HEAD_DIM = 64
LANES = 128
POOL_WINDOWS = (2, 4, 8, 16)
HALO = 16
TB = LEFT_CHUNKS * CHUNK
TBF = 256
TBW = 1024
BAND = (LEFT_CHUNKS + 1) * CHUNK
SUB = 2 * CHUNK
BANDW = SUB + LEFT_CHUNKS * CHUNK
SKEW = 768
NEG = -1e30
RBP = 384
VMEM_LIMIT = 56 * 1024 * 1024
ADAM_LR, ADAM_B1, ADAM_B2, ADAM_EPS, ADAM_WD, ADAM_STEP = 0.001, 0.9, 0.999, 1e-08, 0.01, 10
MESH = pl.DeviceIdType.MESH
ANY = pl.BlockSpec(memory_space=pl.ANY)


def _params(*sem):
    kw = dict(vmem_limit_bytes=VMEM_LIMIT)
    if sem:
        kw["dimension_semantics"] = sem
    return pltpu.CompilerParams(**kw)


def _dot(a, b):
    return jnp.dot(a, b, preferred_element_type=F32)


def _dot_nt(a, b):
    return lax.dot_general(a, b, (((1,), (1,)), ((), ())), preferred_element_type=F32)


def _dot_tn(a, b):
    return lax.dot_general(a, b, (((0,), (0,)), ((), ())), preferred_element_type=F32)


def _rms(x, g):
    r = lax.rsqrt(jnp.mean(x * x, axis=-1, keepdims=True) + EPS)
    return x * r * g


def _rms_bwd(dy, x, g):
    r = lax.rsqrt(jnp.mean(x * x, axis=-1, keepdims=True) + EPS)
    xh = x * r
    dxh = dy * g
    dx = r * (dxh - xh * jnp.mean(dxh * xh, axis=-1, keepdims=True))
    return dx, jnp.sum(dy * xh, axis=0, keepdims=True)


def _full(shape):
    return pl.BlockSpec(shape, lambda *_: (0,) * len(shape))


def _acc_init(step, *refs):
    @pl.when(step == 0)
    def _():
        for r in refs:
            r[...] = jnp.zeros_like(r)


def _inproj_mixers_fwd(h, g, w, cw, wbd, ps):
    T, D = h.shape
    NQ = w.shape[1] - D
    NKV = 2 * NQ // 3
    DC = D // 4

    def body(h_ref, g_ref, w_ref, cw_ref, wbd_ref, ps_ref, xn_ref, pa_ref, qkv_ref, kvt_ref, yab_ref, halo):
        i = pl.program_id(0)
        _acc_init(i, halo)
        xn = _rms(h_ref[...], g_ref[...]).astype(BF16)
        xn_ref[...] = xn
        x = _dot(xn, w_ref[:, :D])
        pa_ref[...] = x
        qkv = _dot(xn, w_ref[:, D:])
        qkv_ref[...] = qkv.astype(BF16)
        kvt_ref[...] = qkv[:, NQ - NKV:].T.astype(BF16)
        hl = halo[...]
        gb, gc, u, pu = (x[:, k * DC:(k + 1) * DC] for k in range(4))
        z = gc * u
        z1, z2 = _conv_taps(z, hl[:, DC:2 * DC] * hl[:, 2 * DC:3 * DC])
        cwv = cw_ref[...]
        ya = gb * (cwv[2:3] * z + cwv[1:2] * z1 + cwv[0:1] * z2)
        d = _pool_d(pu, hl[:, 3 * DC:], _pool_count(i, DC))
        yb = _dot(d.astype(BF16), wbd_ref[...]) * ps_ref[...]
        yab_ref[...] = jnp.concatenate([ya, yb], axis=1)
        halo[...] = x[TB - HALO:, :]

    row = lambda n: pl.BlockSpec((TB, n), lambda i: (i, 0))
    return pl.pallas_call(
        body, name="inproj_mixers_fwd", grid=(T // TB,),
        in_specs=[row(D), _full((1, D)), _full(w.shape), _full((8, DC)), _full((DC, DC)), _full((1, DC))],
        out_specs=[row(D), row(D), row(NQ), pl.BlockSpec((NKV, TB), lambda i: (0, i)), row(2 * DC)],
        out_shape=[jax.ShapeDtypeStruct((T, D), BF16), jax.ShapeDtypeStruct((T, D), F32), jax.ShapeDtypeStruct((T, NQ), BF16),
                   jax.ShapeDtypeStruct((NKV, T), BF16), jax.ShapeDtypeStruct((T, 2 * DC), F32)],
        scratch_shapes=[pltpu.VMEM((HALO, D), F32)],
        compiler_params=_params("arbitrary"),
    )(h, g, w, cw, wbd, ps)


def _lane_groups(n, vals):
    lane = lax.broadcasted_iota(jnp.int32, (1, n), 1)
    q = n // 4
    return jnp.where(lane < q, vals[0], jnp.where(lane < 2 * q, vals[1], jnp.where(lane < 3 * q, vals[2], vals[3]))).astype(F32)


def _pick_group(levels, n):
    lane = lax.broadcasted_iota(jnp.int32, (1, n), 1)
    q = n // 4
    return jnp.where(lane < q, levels[0], jnp.where(lane < 2 * q, levels[1], jnp.where(lane < 3 * q, levels[2], levels[3])))


def _pool_count(blk, n):
    t1 = (blk * TB + 1 + lax.broadcasted_iota(jnp.int32, (TB, 1), 0)).astype(F32)
    return jnp.minimum(t1, _lane_groups(n, POOL_WINDOWS))


def _pool_d(pu, pu_halo, cnt):
    e = jnp.concatenate([pu_halo, pu], axis=0)
    s2 = e + pltpu.roll(e, 1, 0)
    s4 = s2 + pltpu.roll(s2, 2, 0)
    s8 = s4 + pltpu.roll(s4, 4, 0)
    s16 = s8 + pltpu.roll(s8, 8, 0)
    num = _pick_group([s2, s4, s8, s16], pu.shape[1])[HALO:]
    return num / cnt - pu


def _conv_taps(z, z_halo):
    e = jnp.concatenate([z_halo, z], axis=0)
    return pltpu.roll(e, 1, 0)[HALO:], pltpu.roll(e, 2, 0)[HALO:]


def _bias_bins(shape, col_dim):
    j = lax.broadcasted_iota(jnp.int32, shape, col_dim)
    b = lax.broadcasted_iota(jnp.int32, shape, 1 - col_dim)
    d = jnp.where(j < BAND, j, j - SKEW)
    live = jnp.logical_or(j < BAND, j > SKEW - CHUNK)
    bins = jnp.minimum(TB - d, REL_CLIP) + REL_CLIP
    return jnp.where(jnp.logical_and(live, bins == b), 1.0, 0.0).astype(F32)


def _skew_rows(x, left):
    row = lax.broadcasted_iota(jnp.int32, x.shape, 0)
    for b in range(SUB.bit_length() - 1):
        x = jnp.where(((row >> b) & 1) == 1, pltpu.roll(x, SKEW - (1 << b) if left else 1 << b, 1), x)
    return x


def _bias_build(rb):
    H = rb.shape[0]

    def body(rb_ref, o_ref):
        v = jnp.dot(rb_ref[...], _bias_bins((RBP, SKEW), 1), precision=lax.Precision.HIGHEST, preferred_element_type=F32)
        qc = lax.broadcasted_iota(jnp.int32, (SUB, BANDW), 0) >> (CHUNK.bit_length() - 1)
        kc = lax.broadcasted_iota(jnp.int32, (SUB, BANDW), 1) >> (CHUNK.bit_length() - 1)
        in_band = jnp.logical_and(kc >= qc, kc <= qc + LEFT_CHUNKS)
        for h in range(H):
            x = _skew_rows(jnp.broadcast_to(v[h:h + 1], (SUB, SKEW)), left=False)
            o_ref[h // 2, :, (h % 2) * SUB:(h % 2 + 1) * SUB] = jnp.where(in_band, x[:, :BANDW], NEG).T

    return pl.pallas_call(
        body, name="bias_build", in_specs=[_full(rb.shape)], out_specs=_full((H // 2, BANDW, 2 * SUB)),
        out_shape=jax.ShapeDtypeStruct((H // 2, BANDW, 2 * SUB), F32), grid=(1,),
        compiler_params=_params("arbitrary"),
    )(rb)


def _bias_fold(db, after):
    H = 2 * db.shape[0]

    def body(db_ref, after_ref, o_ref, sums):
        sums[...] = jnp.zeros_like(sums)
        for h in range(H):
            x = db_ref[h // 2, :, (h % 2) * SUB:(h % 2 + 1) * SUB].T
            x = _skew_rows(jnp.concatenate([x, jnp.zeros((SUB, SKEW - BANDW), F32)], axis=1), left=True)
            sums[h:h + 1, :] = jnp.sum(x, axis=0, keepdims=True)
        o_ref[...] = jnp.dot(sums[...], _bias_bins((SKEW, RBP), 0), precision=lax.Precision.HIGHEST, preferred_element_type=F32)

    return pl.pallas_call(
        body, name="bias_fold", grid=(1,), in_specs=[_full(db.shape), ANY], out_specs=_full((8, RBP)),
        out_shape=jax.ShapeDtypeStruct((8, RBP), F32), scratch_shapes=[pltpu.VMEM((8, SKEW), F32)],
        compiler_params=_params("arbitrary"),
    )(db, after)


def _both_heads(x):
    first = lax.broadcasted_iota(jnp.int32, (1, LANES), 1) < HEAD_DIM
    return jnp.concatenate([jnp.where(first, x, 0), jnp.where(first, 0, x)], axis=0)


def _own_head_rows(x2):
    n = x2.shape[1] // 2
    first = lax.broadcasted_iota(jnp.int32, (LANES, 1), 0) < HEAD_DIM
    return jnp.where(first, x2[:, :n], x2[:, n:])


def _key_tiles(s, first_block):
    return [t for t in range(s, s + BANDW // SUB) if not (first_block and t < TB // SUB)]


def _attn_fwd(qkv, kv_t, bias_t):
    T = qkv.shape[0]
    NP = qkv.shape[1] // (3 * LANES)
    NB = T // TB
    NS = TB // SUB
    scale = HEAD_DIM ** -0.5

    def body(q_ref, kc_ref, kp_ref, vtc_ref, vtp_ref, b_ref, o_ref, lse_ref):
        i = pl.program_id(1)

        def compute(first_block):
            q = q_ref[...] * scale
            kwin = jnp.concatenate([kp_ref[...], kc_ref[...]], axis=0)
            vt = jnp.concatenate([vtp_ref[...], vtc_ref[...]], axis=1)
            for s in range(NS):
                rows = slice(s * SUB, (s + 1) * SUB)
                tiles = _key_tiles(s, first_block)
                keys = slice(tiles[0] * SUB, (tiles[-1] + 1) * SUB)
                brows = slice((tiles[0] - s) * SUB, (tiles[-1] - s + 1) * SUB)
                q2 = _both_heads(q[rows])
                halves = []
                for a in range(2):
                    st = _dot_nt(kwin[keys], q2[a * SUB:(a + 1) * SUB]) + b_ref[0, brows, a * SUB:(a + 1) * SUB]
                    m = jnp.max(st, axis=0, keepdims=True)
                    p = jnp.exp(st - m)
                    l = jnp.sum(p, axis=0, keepdims=True)
                    halves.append(_dot(vt[:, keys], p.astype(BF16)) * (1.0 / l))
                    lse_ref[0, 0, a * NS + s:a * NS + s + 1, :] = m + jnp.log(l)
                o_ref[rows, :] = _own_head_rows(jnp.concatenate(halves, axis=1)).T

        pl.when(i == 0)(functools.partial(compute, True))
        pl.when(i > 0)(functools.partial(compute, False))

    prev = lambda i: jnp.maximum(i - 1, 0)
    return pl.pallas_call(
        body, name="attn_fwd", grid=(NP, NB),
        in_specs=[pl.BlockSpec((TB, LANES), lambda p, i: (i, p)),
                  pl.BlockSpec((TB, LANES), lambda p, i: (i, NP + p)),
                  pl.BlockSpec((TB, LANES), lambda p, i: (prev(i), NP + p)),
                  pl.BlockSpec((LANES, TB), lambda p, i: (NP + p, i)),
                  pl.BlockSpec((LANES, TB), lambda p, i: (NP + p, prev(i))),
                  pl.BlockSpec((1, BANDW, 2 * SUB), lambda p, i: (p, 0, 0))],
        out_specs=[pl.BlockSpec((TB, LANES), lambda p, i: (i, p)),
                   pl.BlockSpec((1, 1, 8, LANES), lambda p, i: (p, i, 0, 0))],
        out_shape=[jax.ShapeDtypeStruct((T, NP * LANES), F32), jax.ShapeDtypeStruct((NP, NB, 8, LANES), F32)],
        compiler_params=_params("parallel", "parallel"),
    )(qkv, qkv, qkv, kv_t, kv_t, bias_t)


def _group_bounds(D):
    return ((0, D // 4), (D // 4, D // 2), (D // 2, D))


def _ff_tiles(dff):
    half = dff // 2
    return [(slice(a, a + half), slice(dff + a, dff + a + half)) for a in (0, half)]


def _load_resident(step, *pairs_and_sems):
    @pl.when(step == 0)
    def _():
        cps = [pltpu.make_async_copy(src, dst, sem) for src, dst, sem in pairs_and_sems]
        for cp in cps:
            cp.start()
        for cp in cps:
            cp.wait()


def _layer_tail_fwd(yab, yc, gg, w_out, h, g_pm, g_pf, w_gu, w_dn, g_po):
    T, D = h.shape
    F2 = w_gu.shape[1]
    DFF = F2 // 2

    def body(yab_ref, yc_ref, gg_ref, wo_hbm, h_ref, gpm_ref, gpf_ref, wgu_hbm, wdn_hbm, gpo_ref,
             y_ref, mix_ref, h1_ref, hn_ref, gu_ref, ff_ref, ffo_ref, h2_ref, wo_v, wgu_v, wdn_v, sems):
        _load_resident(pl.program_id(0), (wo_hbm, wo_v, sems.at[0]), (wgu_hbm, wgu_v, sems.at[1]), (wdn_hbm, wdn_v, sems.at[2]))
        yraw = jnp.concatenate([yab_ref[...], yc_ref[...]], axis=1)
        ggv = gg_ref[...]
        y = jnp.concatenate([_rms(yraw[:, a:b], ggv[:, a:b]) for a, b in _group_bounds(D)], axis=1).astype(BF16)
        y_ref[...] = y
        mix = _dot(y, wo_v[...])
        mix_ref[...] = mix.astype(BF16)
        h1 = h_ref[...] + _rms(mix, gpm_ref[...])
        h1_ref[...] = h1
        hn = _rms(h1, gpf_ref[...]).astype(BF16)
        hn_ref[...] = hn
        ffo = jnp.zeros((TBF, D), F32)
        for sg, su in _ff_tiles(DFF):
            gate = _dot(hn, wgu_v[:, sg])
            up = _dot(hn, wgu_v[:, su])
            gu_ref[:, sg] = gate.astype(BF16)
            gu_ref[:, su] = up.astype(BF16)
            ff = (gate * jax.nn.sigmoid(gate) * up).astype(BF16)
            ff_ref[:, sg] = ff
            ffo = ffo + _dot(ff, wdn_v[sg, :])
        ffo_ref[...] = ffo.astype(BF16)
        h2_ref[...] = h1 + _rms(ffo, gpo_ref[...])

    row = lambda n: pl.BlockSpec((TBF, n), lambda i: (i, 0))
    gain = _full((1, D))
    f32, bf16 = (lambda n: jax.ShapeDtypeStruct((T, n), F32)), (lambda n: jax.ShapeDtypeStruct((T, n), BF16))
    return pl.pallas_call(
        body, name="layer_tail_fwd", grid=(T // TBF,),
        in_specs=[row(D // 2), row(D // 2), gain, ANY, row(D), gain, gain, ANY, ANY, gain],
        out_specs=[row(D), row(D), row(D), row(D), row(F2), row(DFF), row(D), row(D)],
        out_shape=[bf16(D), bf16(D), f32(D), bf16(D), bf16(F2), bf16(DFF), bf16(D), f32(D)],
        scratch_shapes=[pltpu.VMEM(w_out.shape, BF16), pltpu.VMEM(w_gu.shape, BF16), pltpu.VMEM(w_dn.shape, BF16),
                        pltpu.SemaphoreType.DMA((3,))],
        compiler_params=_params("arbitrary"),
    )(yab, yc, gg, w_out, h, g_pm, g_pf, w_gu, w_dn, g_po)


def _loss_grad(h, tgt):
    T, D = h.shape

    def body(h_ref, t_ref, dh_ref, loss_ref):
        _acc_init(pl.program_id(0), loss_ref)
        diff = h_ref[...] - t_ref[...]
        dh_ref[...] = diff * (1.0 / D)
        loss_ref[...] += 0.5 * jnp.sum(jnp.mean(diff * diff, axis=-1, keepdims=True))

    row = pl.BlockSpec((TB, D), lambda i: (i, 0))
    return pl.pallas_call(
        body, name="loss_grad", grid=(T // TB,),
        in_specs=[row, row], out_specs=[row, _full((8, LANES))],
        out_shape=[jax.ShapeDtypeStruct((T, D), F32), jax.ShapeDtypeStruct((8, LANES), F32)],
        compiler_params=_params("arbitrary"),
    )(h, tgt)


def _layer_tail_bwd(dh2, ffo, g_po, gu, w_dn, w_gu, h1, g_pf, mix, g_pm, w_out, yab, yc, gg):
    T, D = dh2.shape
    F2 = gu.shape[1]
    DFF = F2 // 2

    def body(dh_ref, ffo_ref, gpo_ref, gu_ref, wdn_hbm, wgu_hbm, h_ref, gpf_ref, mix_ref, gpm_ref, wo_hbm, yab_ref, yc_ref, gg_ref,
             dffo_ref, dgu_ref, dh1_ref, dmix_ref, dyab_ref, dyc_ref, dgpo_ref, dgpf_ref, dgpm_ref, dgg_ref,
             wdn_v, wgu_v, wo_v, sems):
        step = pl.program_id(0)
        _load_resident(step, (wdn_hbm, wdn_v, sems.at[0]), (wgu_hbm, wgu_v, sems.at[1]), (wo_hbm, wo_v, sems.at[2]))
        _acc_init(step, dgpo_ref, dgpf_ref, dgpm_ref, dgg_ref)
        dh = dh_ref[...]
        dffo, dg = _rms_bwd(dh, ffo_ref[...].astype(F32), gpo_ref[...])
        dgpo_ref[0:1, :] += dg
        dffo = dffo.astype(BF16)
        dffo_ref[...] = dffo
        dhn = jnp.zeros((TBF, D), F32)
        for sg, su in _ff_tiles(DFF):
            dff = _dot_nt(dffo, wdn_v[sg, :])
            gate, up = gu_ref[:, sg].astype(F32), gu_ref[:, su].astype(F32)
            sig = jax.nn.sigmoid(gate)
            dgate = (dff * up * (sig * (1.0 + gate * (1.0 - sig)))).astype(BF16)
            dup = (dff * (gate * sig)).astype(BF16)
            dgu_ref[:, sg] = dgate
            dgu_ref[:, su] = dup
            dhn = dhn + _dot_nt(dgate, wgu_v[:, sg]) + _dot_nt(dup, wgu_v[:, su])
        dx, dg = _rms_bwd(dhn, h_ref[...], gpf_ref[...])
        dgpf_ref[0:1, :] += dg
        dh1 = dh + dx
        dh1_ref[...] = dh1
        dmix, dg = _rms_bwd(dh1, mix_ref[...].astype(F32), gpm_ref[...])
        dgpm_ref[0:1, :] += dg
        dmix = dmix.astype(BF16)
        dmix_ref[...] = dmix
        dy = _dot_nt(dmix, wo_v[...])
        yraw = jnp.concatenate([yab_ref[...], yc_ref[...]], axis=1)
        ggv = gg_ref[...]
        parts = [_rms_bwd(dy[:, a:b], yraw[:, a:b], ggv[:, a:b]) for a, b in _group_bounds(D)]
        dgg_ref[0:1, :] += jnp.concatenate([p[1] for p in parts], axis=1)
        dyab_ref[...] = jnp.concatenate([parts[0][0], parts[1][0]], axis=1).astype(BF16)
        dyc_ref[...] = parts[2][0].astype(BF16)

    row = lambda n: pl.BlockSpec((TBF, n), lambda i: (i, 0))
    gain, acc = _full((1, D)), _full((8, D))
    f32, bf16 = (lambda n: jax.ShapeDtypeStruct((T, n), F32)), (lambda n: jax.ShapeDtypeStruct((T, n), BF16))
    acc_shape = jax.ShapeDtypeStruct((8, D), F32)
    return pl.pallas_call(
        body, name="layer_tail_bwd", grid=(T // TBF,),
        in_specs=[row(D), row(D), gain, row(F2), ANY, ANY, row(D), gain, row(D), gain, ANY, row(D // 2), row(D // 2), gain],
        out_specs=[row(D), row(F2), row(D), row(D), row(D // 2), row(D // 2), acc, acc, acc, acc],
        out_shape=[bf16(D), bf16(F2), f32(D), bf16(D), bf16(D // 2), bf16(D // 2), acc_shape, acc_shape, acc_shape, acc_shape],
        scratch_shapes=[pltpu.VMEM(w_dn.shape, BF16), pltpu.VMEM(w_gu.shape, BF16), pltpu.VMEM(w_out.shape, BF16),
                        pltpu.SemaphoreType.DMA((3,))],
        compiler_params=_params("arbitrary"),
    )(dh2, ffo, g_po, gu, w_dn, w_gu, h1, g_pf, mix, g_pm, w_out, yab, yc, gg)


def _attn_bwd(qkv, kv_t, dyc, yc, lse, bias_t, after):
    T = qkv.shape[0]
    NP = qkv.shape[1] // (3 * LANES)
    NB = T // TB
    NS = TB // SUB
    scale = HEAD_DIM ** -0.5

    def body(q_ref, kc_ref, kp_ref, vc_ref, vp_ref, ktc_ref, ktp_ref, do_ref, o_ref, lse_ref, b_ref, after_ref,
             dq_ref, dk_ref, dv_ref, db_ref, dk_carry, dv_carry, dkw, dvw):
        step = pl.program_id(1)
        i = NB - 1 - step
        _acc_init(step, dk_carry, dv_carry, db_ref)

        def compute(first_block):
            q = q_ref[...] * scale
            kwin = jnp.concatenate([kp_ref[...], kc_ref[...]], axis=0)
            vwin = jnp.concatenate([vp_ref[...], vc_ref[...]], axis=0)
            kt = jnp.concatenate([ktp_ref[...], ktc_ref[...]], axis=1)
            do = do_ref[...].astype(F32)
            dob = do.astype(BF16)
            prod = do * o_ref[...]
            first = lax.broadcasted_iota(jnp.int32, (1, LANES), 1) < HEAD_DIM
            ones = jnp.ones((8, LANES), F32)
            row_sums = lambda x: lax.dot_general(ones, x, (((1,), (1,)), ((), ())), precision=lax.Precision.HIGHEST,
                                                 preferred_element_type=F32)
            deltas = (row_sums(jnp.where(first, prod, 0.0)), row_sums(jnp.where(first, 0.0, prod)))
            written = set()
            for s in range(NS):
                rows = slice(s * SUB, (s + 1) * SUB)
                tiles = _key_tiles(s, first_block)
                keys = slice(tiles[0] * SUB, (tiles[-1] + 1) * SUB)
                brows = slice((tiles[0] - s) * SUB, (tiles[-1] - s + 1) * SUB)
                q2, do2 = _both_heads(q[rows]), _both_heads(dob[rows])
                lse = jnp.concatenate([lse_ref[0, 0, s:s + 1, :], lse_ref[0, 0, NS + s:NS + s + 1, :]], axis=1)
                delta = jnp.concatenate([deltas[0][0:1, rows], deltas[1][0:1, rows]], axis=1)
                p = jnp.exp(_dot_nt(kwin[keys], q2) + b_ref[0, brows, :] - lse)
                ds = p * (_dot_nt(vwin[keys], do2) - delta)
                db_ref[0, brows, :] += ds
                dsb = ds.astype(BF16)
                dk_c = _dot(dsb, q2)
                dv_c = _dot(p.astype(BF16), do2)
                dq_ref[rows, :] = (_own_head_rows(_dot(kt[:, keys], dsb)).T * scale).astype(BF16)
                for n, t in enumerate(tiles):
                    win, loc = slice(t * SUB, (t + 1) * SUB), slice(n * SUB, (n + 1) * SUB)
                    if t in written:
                        dkw[win, :] += dk_c[loc]
                        dvw[win, :] += dv_c[loc]
                    else:
                        dkw[win, :] = dk_c[loc]
                        dvw[win, :] = dv_c[loc]
                        written.add(t)
            dk_ref[...] = (dkw[TB:, :] + dk_carry[...]).astype(BF16)
            dv_ref[...] = (dvw[TB:, :] + dv_carry[...]).astype(BF16)
            if not first_block:
                dk_carry[...] = dkw[:TB, :]
                dv_carry[...] = dvw[:TB, :]

        pl.when(i == 0)(functools.partial(compute, True))
        pl.when(i > 0)(functools.partial(compute, False))

    blk = lambda s: NB - 1 - s
    prev = lambda s: jnp.maximum(NB - 2 - s, 0)
    rows = lambda which, off: pl.BlockSpec((TB, LANES), lambda p, s: (which(s), off + p))
    out = jax.ShapeDtypeStruct((T, NP * LANES), BF16)
    strip = pl.BlockSpec((1, BANDW, 2 * SUB), lambda p, s: (p, 0, 0))
    return pl.pallas_call(
        body, name="attn_bwd", grid=(NP, NB),
        in_specs=[rows(blk, 0), rows(blk, NP), rows(prev, NP), rows(blk, 2 * NP), rows(prev, 2 * NP),
                  pl.BlockSpec((LANES, TB), lambda p, s: (p, blk(s))), pl.BlockSpec((LANES, TB), lambda p, s: (p, prev(s))),
                  rows(blk, 0), rows(blk, 0), pl.BlockSpec((1, 1, 8, LANES), lambda p, s: (p, blk(s), 0, 0)), strip, ANY],
        out_specs=[rows(blk, 0), rows(blk, 0), rows(blk, 0), strip],
        out_shape=[out, out, out, jax.ShapeDtypeStruct((NP, BANDW, 2 * SUB), F32)],
        scratch_shapes=[pltpu.VMEM((TB, LANES), F32), pltpu.VMEM((TB, LANES), F32),
                        pltpu.VMEM((2 * TB, LANES), F32), pltpu.VMEM((2 * TB, LANES), F32)],
        compiler_params=_params("arbitrary", "arbitrary"),
    )(qkv, qkv, qkv, qkv, qkv, kv_t, kv_t, dyc, yc, lse, bias_t, after)


def _inproj_mixers_bwd(pa, dyab, cw, wbd, ps, dqkv, w, h, g, dh1):
    T, D = pa.shape
    DC = D // 4
    NB = T // TB
    N = TB + HALO
    widths = [p.shape[1] for p in dqkv]
    n = len(dqkv)

    def body(*refs):
        pa_ref, halo_ref, dy_ref, cw_ref, wbd_ref, ps_ref = refs[:6]
        parts = refs[6:6 + n]
        w_ref, h_ref, g_ref, dh1_ref, dpa_ref, dh_ref, dcw_ref, dwbd_ref, dps_ref, dg_ref, dc_carry, e_carry = refs[6 + n:]
        step = pl.program_id(0)
        i = NB - 1 - step
        _acc_init(step, dcw_ref, dwbd_ref, dps_ref, dg_ref, dc_carry, e_carry)
        x = pa_ref[...]
        hl = jnp.where(i > 0, halo_ref[...], 0.0)
        gb, gc, u, pu = (x[:, k * DC:(k + 1) * DC] for k in range(4))
        dy = dy_ref[...].astype(F32)
        dya, dyb = dy[:, :DC], dy[:, DC:]
        cwv = cw_ref[...]
        z = gc * u
        z1, z2 = _conv_taps(z, hl[:, DC:2 * DC] * hl[:, 2 * DC:3 * DC])
        dgb = dya * (cwv[2:3] * z + cwv[1:2] * z1 + cwv[0:1] * z2)
        dconv = dya * gb
        dcw_ref[0:1, :] += jnp.sum(dconv * z2, axis=0, keepdims=True)
        dcw_ref[1:2, :] += jnp.sum(dconv * z1, axis=0, keepdims=True)
        dcw_ref[2:3, :] += jnp.sum(dconv * z, axis=0, keepdims=True)
        ext = jnp.concatenate([dconv, dc_carry[...]], axis=0)
        dz = cwv[2:3] * dconv + cwv[1:2] * pltpu.roll(ext, N - 1, 0)[:TB] + cwv[0:1] * pltpu.roll(ext, N - 2, 0)[:TB]
        dc_carry[...] = dconv[:HALO]
        cnt = _pool_count(i, DC)
        d = _pool_d(pu, hl[:, 3 * DC:], cnt).astype(BF16)
        psv = ps_ref[...]
        wb = wbd_ref[...]
        dps_ref[0:1, :] += jnp.sum(dyb * _dot(d, wb), axis=0, keepdims=True)
        dys = (dyb * psv).astype(BF16)
        dwbd_ref[...] += _dot_tn(d, dys)
        dd = _dot_nt(dys, wb)
        e = dd / cnt
        ext = jnp.concatenate([e, e_carry[...]], axis=0)
        a2 = ext + pltpu.roll(ext, N - 1, 0)
        a4 = a2 + pltpu.roll(a2, N - 2, 0)
        a8 = a4 + pltpu.roll(a4, N - 4, 0)
        a16 = a8 + pltpu.roll(a8, N - 8, 0)
        dpu = _pick_group([a2, a4, a8, a16], DC)[:TB] - dd
        e_carry[...] = e[:HALO]
        dpa = jnp.concatenate([dgb, dz * u, dz * gc, dpu], axis=1).astype(BF16)
        dpa_ref[...] = dpa
        dxn, off = _dot_nt(dpa, w_ref[:, :D]), D
        for p_ref, wd in zip(parts, widths):
            dxn = dxn + _dot_nt(p_ref[...], w_ref[:, off:off + wd])
            off += wd
        dx, dg = _rms_bwd(dxn, h_ref[...], g_ref[...])
        dg_ref[0:1, :] += dg
        dh_ref[...] = dh1_ref[...] + dx

    blk = lambda m: pl.BlockSpec((TB, m), lambda s: (NB - 1 - s, 0))
    return pl.pallas_call(
        body, name="inproj_mixers_bwd", grid=(NB,),
        in_specs=[blk(D), pl.BlockSpec((HALO, D), lambda s: (jnp.maximum((NB - 1 - s) * (TB // HALO) - 1, 0), 0)),
                  blk(2 * DC), _full((8, DC)), _full((DC, DC)), _full((1, DC))] + [blk(wd) for wd in widths]
                 + [_full(w.shape), blk(D), _full((1, D)), blk(D)],
        out_specs=[blk(D), blk(D), _full((8, DC)), _full((DC, DC)), _full((8, DC)), _full((8, D))],
        out_shape=[jax.ShapeDtypeStruct((T, D), BF16), jax.ShapeDtypeStruct((T, D), F32), jax.ShapeDtypeStruct((8, DC), F32),
                   jax.ShapeDtypeStruct((DC, DC), F32), jax.ShapeDtypeStruct((8, DC), F32), jax.ShapeDtypeStruct((8, D), F32)],
        scratch_shapes=[pltpu.VMEM((HALO, DC), F32), pltpu.VMEM((HALO, DC), F32)],
        compiler_params=_params("arbitrary"),
    )(pa, pa, dyab, cw, wbd, ps, *dqkv, w, h, g, dh1)


def _wgrad(a, b, tk, tn, name):
    T, K = a.shape
    N = b.shape[1]
    tt = min(2 * TBW, T)

    def body(a_ref, b_ref, o_ref):
        _acc_init(pl.program_id(2), o_ref)
        o_ref[...] += _dot_tn(a_ref[...], b_ref[...])

    return pl.pallas_call(
        body, name=name, grid=(K // tk, N // tn, T // tt),
        in_specs=[pl.BlockSpec((tt, tk), lambda k, n, t: (t, k)), pl.BlockSpec((tt, tn), lambda k, n, t: (t, n))],
        out_specs=pl.BlockSpec((tk, tn), lambda k, n, t: (k, n)),
        out_shape=jax.ShapeDtypeStruct((K, N), F32),
        compiler_params=_params("parallel", "parallel", "arbitrary"),
    )(a, b)


def _wgrad_concat(a, bs, name):
    T, K = a.shape
    widths = [b.shape[1] for b in bs]
    n = len(bs)

    def body(*refs):
        a_ref, b_refs, o_ref = refs[0], refs[1:1 + n], refs[1 + n]
        _acc_init(pl.program_id(0), o_ref)
        av, off = a_ref[...], 0
        for b_ref, wd in zip(b_refs, widths):
            o_ref[:, off:off + wd] += _dot_tn(av, b_ref[...])
            off += wd

    row = lambda m: pl.BlockSpec((TBW, m), lambda t: (t, 0))
    return pl.pallas_call(
        body, name=name, grid=(T // TBW,),
        in_specs=[row(K)] + [row(wd) for wd in widths],
        out_specs=_full((K, sum(widths))),
        out_shape=jax.ShapeDtypeStruct((K, sum(widths)), F32),
        compiler_params=_params("arbitrary"),
    )(a, *bs)


def _adamw(w, g, m, v, tr, name):
    R, C = w.shape

    def body(w_ref, g_ref, m_ref, v_ref, d_ref, nm_ref, nv_ref):
        gv = g_ref[...]
        nm = ADAM_B1 * m_ref[...] + (1.0 - ADAM_B1) * gv
        nv = ADAM_B2 * v_ref[...] + (1.0 - ADAM_B2) * (gv * gv)
        m_hat = nm / (1.0 - ADAM_B1 ** ADAM_STEP)
        v_hat = nv / (1.0 - ADAM_B2 ** ADAM_STEP)
        d_ref[...] = -ADAM_LR * (m_hat / (jnp.sqrt(v_hat) + ADAM_EPS) + ADAM_WD * w_ref[...])
        nm_ref[...] = nm
        nv_ref[...] = nv

    blk = pl.BlockSpec((tr, C), lambda i: (i, 0))
    out = jax.ShapeDtypeStruct((R, C), F32)
    return pl.pallas_call(
        body, name=name, grid=(R // tr,), in_specs=[blk] * 4, out_specs=[blk] * 3, out_shape=[out] * 3,
        compiler_params=_params("parallel"),
    )(w, g, m, v)


def _sum_slots(x, tr, name):
    n, R, C = x.shape

    def body(x_ref, o_ref):
        acc = x_ref[0]
        for k in range(1, n):
            acc = acc + x_ref[k]
        o_ref[...] = acc

    return pl.pallas_call(
        body, name=name, grid=(R // tr,),
        in_specs=[pl.BlockSpec((n, tr, C), lambda i: (0, i, 0))],
        out_specs=pl.BlockSpec((tr, C), lambda i: (i, 0)),
        out_shape=jax.ShapeDtypeStruct((R, C), F32),
        compiler_params=_params("parallel"),
    )(x)


def _add_half(view, other, c, name):
    if view.ndim == 3:
        _, R2, N = view.shape
        tr = 128
        grid = (R2 // tr,)
        in_specs = [pl.BlockSpec((1, tr, N), lambda i, c_ref: (c_ref[0], i, 0)), pl.BlockSpec((1, tr, N), lambda i, c_ref: (0, i, 0))]
        out_spec = pl.BlockSpec((tr, N), lambda i, c_ref: (i, 0))
        out_shape = jax.ShapeDtypeStruct((R2, N), BF16)

        def body(c_ref, a_ref, b_ref, o_ref):
            o_ref[...] = (a_ref[0] + b_ref[0]).astype(BF16)
    else:
        S, _, R2, C = view.shape
        grid = (S,)
        in_specs = [pl.BlockSpec((1, 1, R2, C), lambda s, c_ref: (s, c_ref[0], 0, 0)), pl.BlockSpec((1, 1, R2, C), lambda s, c_ref: (s, 0, 0, 0))]
        out_spec = pl.BlockSpec((1, R2, C), lambda s, c_ref: (s, 0, 0))
        out_shape = jax.ShapeDtypeStruct((S, R2, C), BF16)

        def body(c_ref, a_ref, b_ref, o_ref):
            o_ref[0] = (a_ref[0, 0] + b_ref[0, 0]).astype(BF16)

    return pl.pallas_call(
        body, name=name,
        grid_spec=pltpu.PrefetchScalarGridSpec(num_scalar_prefetch=1, grid=grid, in_specs=in_specs, out_specs=out_spec),
        out_shape=out_shape, compiler_params=_params("parallel"),
    )(c, view, other)


def _sum_chips(slots, part, kind, place, name):
    _, R2, C = slots.shape
    tr = min(R2, 128) if kind == "col" else R2

    def body(p_ref, s_ref, own_ref, o_ref):
        me = p_ref[0]
        own = own_ref[...] if kind == "col" else own_ref[0]
        acc = None
        for k in range(4):
            term = jnp.where(me == k, own, s_ref[k]).astype(F32)
            acc = term if acc is None else acc + term
        o_ref[0] = acc

    own_spec = (pl.BlockSpec((tr, C), lambda i, p: (i, p[0])) if kind == "col"
                else pl.BlockSpec((1, R2, C), lambda i, p: (p[0], 0, 0)))
    return pl.pallas_call(
        body, name=name,
        grid_spec=pltpu.PrefetchScalarGridSpec(
            num_scalar_prefetch=1, grid=(R2 // tr,),
            in_specs=[pl.BlockSpec((4, tr, C), lambda i, p: (0, i, 0)), own_spec],
            out_specs=pl.BlockSpec((1, tr, C), lambda i, p: (p[1], i, 0))),
        out_shape=jax.ShapeDtypeStruct((2, R2, C), F32), compiler_params=_params("parallel"),
    )(place, slots, part)


def _place():
    x, y, c = lax.axis_index("x"), lax.axis_index("y"), lax.axis_index("c")
    chips = [(1 - x, y), (x, 1 - y), (1 - x, 1 - y)]
    return x, y, c, 2 * x + y, chips


def _remote(src, dst, send_sems, recv_sems, k, to):
    return pltpu.make_async_remote_copy(src_ref=src, dst_ref=dst, send_sem=send_sems.at[k], recv_sem=recv_sems.at[k],
                                        device_id=to, device_id_type=MESH)


def _full_shape(shard_shape, kind):
    *lead, R, C = shard_shape
    return (*lead, R, 4 * C) if kind == "col" else (*lead, 4 * R, C)


def _slab(full_ref, shard_rc, kind, chip, half=None):
    R, C = shard_rc
    lead = (slice(None),) * (len(full_ref.shape) - 2)
    if kind == "col":
        rows = pl.ds(0, R) if half is None else pl.ds(half * (R // 2), R // 2)
        return full_ref.at[(*lead, rows, pl.ds(chip * C, C))]
    rows = pl.ds(chip * R, R) if half is None else pl.ds(chip * R + half * (R // 2), R // 2)
    return full_ref.at[(*lead, rows, slice(None))]


def _row_half(ref, half):
    R = ref.shape[-2]
    lead = (slice(None),) * (len(ref.shape) - 2)
    return ref.at[(*lead, pl.ds(half * (R // 2), R // 2), slice(None))]


def _allgather_weights(shards, kinds, after):
    n = len(shards)
    full_shapes = [_full_shape(s.shape, k) for s, k in zip(shards, kinds)]

    def body(*refs):
        ins, outs, (send_sems, recv_sems) = refs[:n], refs[n + 1:2 * n + 1], refs[2 * n + 1:]
        x, y, c, me, chips = _place()
        slab = lambda a, chip, half=None: _slab(outs[a], shards[a].shape[-2:], kinds[a], chip, half)
        my_half = lambda a: _row_half(ins[a], c)
        own = [_remote(ins[a], slab(a, me), send_sems, recv_sems, 6 * n + a, (x, y, 1 - c)) for a in range(n)]
        first = [_remote(my_half(a), slab(a, me, c), send_sems, recv_sems, j * n + a, (*chip, c))
                 for j, chip in enumerate(chips) for a in range(n)]
        for cp in first + own:
            cp.start()
        passed = []
        for j, chip in enumerate(chips):
            src = 2 * chip[0] + chip[1]
            for a in range(n):
                _remote(my_half(a), slab(a, src, c), send_sems, recv_sems, j * n + a, (x, y, c)).wait_recv()
                cp = _remote(slab(a, src, c), slab(a, src, c), send_sems, recv_sems, (3 + j) * n + a, (x, y, 1 - c))
                cp.start()
                passed.append(cp)
        for j, chip in enumerate(chips):
            src = 2 * chip[0] + chip[1]
            for a in range(n):
                _remote(my_half(a), slab(a, src, 1 - c), send_sems, recv_sems, (3 + j) * n + a, (x, y, c)).wait_recv()
        for cp in own:
            cp.wait_recv()
        for cp in first + passed + own:
            cp.wait_send()

    return pl.pallas_call(
        body, name="allgather_weights",
        in_specs=[ANY] * (n + 1), out_specs=[ANY] * n,
        out_shape=[jax.ShapeDtypeStruct(s, BF16) for s in full_shapes],
        scratch_shapes=[pltpu.SemaphoreType.DMA((7 * n,)), pltpu.SemaphoreType.DMA((7 * n,))],
        compiler_params=pltpu.CompilerParams(has_side_effects=True),
    )(*shards, after)


HBM = pl.BlockSpec(memory_space=pltpu.HBM)
SEM = pl.BlockSpec(memory_space=pltpu.SEMAPHORE)
DATAFLOW = pltpu.SideEffectType.DATAFLOW_SIDE_EFFECTING


def _split_start(name, bufs, plan, ncopies, after=None):
    nb = len(bufs)
    nin = nb + (after is not None)

    def body(*refs):
        send_sems, recv_sems, token = refs[nin], refs[nin + 1], refs[-1]
        for k, (src, dst, to) in enumerate(plan(refs[:nb])):
            _remote(src, dst, send_sems, recv_sems, k, to).start()
        token[...] = jnp.zeros_like(token)

    out = pl.pallas_call(
        body, name=name,
        out_shape=(pltpu.SemaphoreType.DMA((ncopies,)), pltpu.SemaphoreType.DMA((ncopies,)),
                   *[pltpu.HBM(b.shape, b.dtype) for b in bufs], jax.ShapeDtypeStruct((8, LANES), F32)),
        in_specs=[HBM] * nb + [ANY] * (nin - nb), out_specs=(SEM, SEM, *[HBM] * nb, pl.BlockSpec(memory_space=pltpu.VMEM)),
        input_output_aliases={i: 2 + i for i in range(nb)},
        compiler_params=pltpu.CompilerParams(has_side_effects=DATAFLOW),
    )(*[pltpu.with_memory_space_constraint(b, pltpu.HBM) for b in bufs], *([] if after is None else [after]))
    return out[0], out[1], list(out[2:2 + nb]), out[-1]


def _split_wait(name, send_sems, recv_sems, bufs, plan, after):
    nb = len(bufs)

    def body(*refs):
        s_sems, r_sems = refs[nb], refs[nb + 1]
        for k, (src, dst, to) in enumerate(plan(refs[:nb])):
            cp = _remote(src, dst, s_sems, r_sems, k, to)
            cp.wait_send()
            cp.wait_recv()

    return pl.pallas_call(
        body, name=name,
        out_shape=tuple(pltpu.HBM(b.shape, b.dtype) for b in bufs),
        in_specs=[HBM] * nb + [SEM, SEM, ANY], out_specs=tuple([HBM] * nb),
        input_output_aliases={i: i for i in range(nb)},
        compiler_params=pltpu.CompilerParams(has_side_effects=DATAFLOW),
    )(*bufs, send_sems, recv_sems, after)


def _gather_plan(n, shard_rcs, kinds):
    def plan(refs):
        x, y, c, me, chips = _place()
        out = []
        for a in range(n):
            shard, full = refs[a], refs[n + a]
            out.append((shard, _slab(full, shard_rcs[a], kinds[a], me), (x, y, 1 - c)))
            for chip in chips:
                for cc in (c, 1 - c):
                    out.append((_row_half(shard, c), _slab(full, shard_rcs[a], kinds[a], me, c), (*chip, cc)))
        return out
    return plan


def _exchange_plan(n, kinds):
    def plan(refs):
        x, y, c, me, chips = _place()
        out = []
        for a in range(n):
            part, slots = refs[a], refs[n + a]
            C = slots.shape[2]
            for chip in chips:
                dst = 2 * chip[0] + chip[1]
                src = part.at[:, pl.ds(dst * C, C)] if kinds[a] == "col" else part.at[dst]
                out.append((src, slots.at[me], (*chip, c)))
        return out
    return plan


def _swap_plan(ndims):
    n = len(ndims)

    def plan(refs):
        x, y, c, _, _ = _place()
        out = []
        for a in range(n):
            src = refs[a].at[pl.ds(1 - c, 1)] if ndims[a] == 3 else refs[a].at[:, pl.ds(1 - c, 1)]
            out.append((src, refs[n + a], (x, y, 1 - c)))
        return out
    return plan


def _join_halves(bufs):
    n = len(bufs)

    def body(*refs):
        outs, (send_sems, recv_sems) = refs[n:2 * n], refs[2 * n:]
        x, y, c, _, _ = _place()
        sends = [_remote(outs[a].at[c], outs[a].at[c], send_sems, recv_sems, a, (x, y, 1 - c)) for a in range(n)]
        for cp in sends:
            cp.start()
        for a in range(n):
            _remote(outs[a].at[c], outs[a].at[1 - c], send_sems, recv_sems, a, (x, y, c)).wait_recv()
        for cp in sends:
            cp.wait_send()

    return pl.pallas_call(
        body, name="grad_join_halves", in_specs=[ANY] * n, out_specs=[ANY] * n,
        out_shape=[jax.ShapeDtypeStruct(b.shape, F32) for b in bufs],
        input_output_aliases={a: a for a in range(n)},
        scratch_shapes=[pltpu.SemaphoreType.DMA((n,)), pltpu.SemaphoreType.DMA((n,))],
        compiler_params=pltpu.CompilerParams(has_side_effects=True),
    )(*bufs)


def _allgather_small(block):
    M, N = block.shape

    def body(x_ref, out_ref, send_sems, recv_sems, local_sem):
        x, y, c, _, chips = _place()
        me, sibling = (x, y, c), (x, y, 1 - c)

        def rows(px, py, pc):
            return out_ref.at[pl.ds((4 * px + 2 * py + pc) * M, M), :]

        def copy(k, blk, to, src=None):
            return _remote(rows(*blk) if src is None else src, rows(*blk), send_sems, recv_sems, k, to)

        mine = pltpu.make_async_copy(x_ref, rows(*me), local_sem)
        mine.start()
        first = [copy(0, me, sibling, src=x_ref)] + [copy(1 + j, me, (*chip, c), src=x_ref) for j, chip in enumerate(chips)]
        for cp in first:
            cp.start()
        passed = [copy(4 + j, (*chip, c), sibling) for j, chip in enumerate(chips)]
        for j, chip in enumerate(chips):
            copy(1 + j, (*chip, c), me).wait_recv()
            passed[j].start()
        copy(0, sibling, me).wait_recv()
        for j, chip in enumerate(chips):
            copy(4 + j, (*chip, 1 - c), me).wait_recv()
        for cp in first + passed:
            cp.wait_send()
        mine.wait()

    vm = pl.BlockSpec(memory_space=pltpu.VMEM)
    return pl.pallas_call(
        body, name="allgather_small", in_specs=[vm], out_specs=vm,
        out_shape=jax.ShapeDtypeStruct((8 * M, N), F32),
        scratch_shapes=[pltpu.SemaphoreType.DMA((7,)), pltpu.SemaphoreType.DMA((7,)), pltpu.SemaphoreType.DMA],
        compiler_params=pltpu.CompilerParams(has_side_effects=True, vmem_limit_bytes=VMEM_LIMIT),
    )(block)


def _pack(arrays):
    flat = jnp.concatenate([a.reshape(-1) for a in arrays])
    pad = (-flat.shape[0]) % (8 * LANES)
    return jnp.pad(flat, (0, pad)).reshape(-1, LANES)


def _unpack(packed, shapes):
    flat, out, off = packed.reshape(-1), [], 0
    for s in shapes:
        size = 1
        for d in s:
            size *= d
        out.append(flat[off:off + size].reshape(s))
        off += size
    return out


def _block_diag(pw):
    G, n, _ = pw.shape
    eye = jnp.eye(G, dtype=pw.dtype)
    return (eye[:, None, :, None] * pw[:, :, None, :]).reshape(G * n, G * n)


def _diag_blocks(m, G):
    n = m.shape[0] // G
    return jnp.stack([m[g * n:(g + 1) * n, g * n:(g + 1) * n] for g in range(G)])


def _pad_rows(a, rows):
    return jnp.pad(a, ((0, rows - a.shape[0]), (0, 0)))


def kernel(x, w_in, w_out, conv_w, pool_w, pool_scale, rel_bias, group_gain, pre_mix_g, post_mix_g, pre_ffn_g, post_ffn_g, w_gate_up, w_down, loss_target, m_w_in, m_w_out, m_conv_w, m_pool_w, m_pool_scale, m_rel_bias, m_group_gain, m_pre_mix_g, m_post_mix_g, m_pre_ffn_g, m_post_ffn_g, m_w_gate_up, m_w_down, v_w_in, v_w_out, v_conv_w, v_pool_w, v_pool_scale, v_rel_bias, v_group_gain, v_pre_mix_g, v_post_mix_g, v_pre_ffn_g, v_post_ffn_g, v_w_gate_up, v_w_down):
    L = w_in.shape[0]
    T, D = x.shape[1], x.shape[2]
    DC = D // 4
    NH = rel_bias.shape[1]
    NREL = rel_bias.shape[2]
    G = pool_w.shape[1]
    cs = conv_w.shape[2]
    assert TB == LEFT_CHUNKS * CHUNK and T % TB == 0 and D % (4 * LANES) == 0 and NH * HEAD_DIM == D // 2
    xi, yi, ci = lax.axis_index("x"), lax.axis_index("y"), lax.axis_index("c")
    chip = 2 * xi + yi

    kinds = ("col", "row", "col", "row")
    big = (w_in, w_out, w_gate_up, w_down)
    rcs = [w.shape[-2:] for w in big]
    conv_gathered = _allgather_small(_pack([conv_w]))
    conv_all = conv_gathered.reshape(8, -1)[:, :L * 3 * cs].reshape(4, 2, L, 3, cs)[:, 0]
    conv_full = jnp.moveaxis(conv_all, 0, 2).reshape(L, 3, 4 * cs)
    landing = lambda which: [lax.empty(_full_shape(rcs[k], kinds[k]), BF16) for k in which]
    wi0 = _allgather_weights([w_in[0:1].astype(BF16)], kinds[:1], conv_gathered)[0][0]
    rest_plan = _gather_plan(3, rcs[1:], kinds[1:])
    rest = _split_start("gather_start_0", [w[0].astype(BF16) for w in big[1:]] + landing((1, 2, 3)), rest_plan, 3 * 7, after=wi0)
    layer_plan = _gather_plan(4, rcs, kinds)
    full = [None] * L

    h = x[0]
    saved = []
    token = rest[3]
    for l in range(L):
        g_pre, g_pm, g_pf, g_po = (a[l][None] for a in (pre_mix_g, post_mix_g, pre_ffn_g, post_ffn_g))
        if 0 < l < L - 1:
            nxt = _split_start(f"gather_start_{l + 1}", [w[l + 1].astype(BF16) for w in big] + landing(range(4)), layer_plan, 4 * 7,
                               after=full[l][0])
            token = nxt[3]
        if l < L - 1:
            g_pre = g_pre + token[0:1, 0:1]
        gg, ps = group_gain[l][None], pool_scale[l][None]
        cw = _pad_rows(conv_full[l], 8)
        wbd = _block_diag(pool_w[l]).astype(BF16)
        bias = _bias_build(jnp.pad(rel_bias[l], ((0, 0), (0, RBP - NREL))))
        wi = wi0 if l == 0 else full[l][0]
        xn, pa, qkv, kv_t, yab = _inproj_mixers_fwd(h, g_pre, wi, cw, wbd, ps)
        yc, lse = _attn_fwd(qkv, kv_t, bias)
        if l == 0:
            full[0] = [wi0, *_split_wait("gather_wait_0", rest[0], rest[1], rest[2], rest_plan, yc)[3:6]]
            if L > 1:
                nxt = _split_start("gather_start_1", [w[1].astype(BF16) for w in big] + landing(range(4)), layer_plan, 4 * 7,
                                   after=full[0][1])
                g_pm = g_pm + nxt[3][0:1, 0:1]
        wi, wo, wgu, wdn = full[l]
        y, mix, h1, hn, gu, ff, ffo, h2 = _layer_tail_fwd(yab, yc, gg, wo, h, g_pm, g_pf, wgu, wdn, g_po)
        saved.append(dict(h=h, xn=xn, pa=pa, qkv=qkv, kv_t=kv_t, lse=lse, yab=yab, yc=yc, y=y, mix=mix, h1=h1, hn=hn, gu=gu, ff=ff, ffo=ffo,
                          cw=cw, wbd=wbd, bias=bias, ps=ps, gg=gg, g_pre=g_pre, g_pm=g_pm, g_pf=g_pf, g_po=g_po))
        h = h2
        if l + 1 < L:
            full[l + 1] = list(_split_wait(f"gather_wait_{l + 1}", nxt[0], nxt[1], nxt[2], layer_plan, h2)[4:8])

    dh, loss_tile = _loss_grad(h, loss_target[0])
    loss = lax.psum(loss_tile[0, 0], ("x", "y", "c"))

    place = jnp.stack([chip, ci]).astype(jnp.int32)
    cvec = ci.reshape(1).astype(jnp.int32)
    small_grads = [None] * L
    shard_grads = [[None] * 4 for _ in range(L)]
    dbiases = [None] * L

    def start_swap(tag, grads, ks, after=None):
        views = [g.reshape(2, g.shape[0] // 2, g.shape[1]) if kinds[k] == "col" else g.reshape(4, 2, g.shape[0] // 8, g.shape[1])
                 for g, k in zip(grads, ks)]
        lands = [lax.empty((1,) + v.shape[1:] if v.ndim == 3 else (v.shape[0], 1) + v.shape[2:], F32) for v in views]
        plan = _swap_plan([v.ndim for v in views])
        ssem, rsem, bufs, tok = _split_start(f"grad_swap_start_{tag}", views + lands, plan, len(ks), after=after)
        return (tag, ks, plan, ssem, rsem, bufs), tok

    def start_exchange(swap, after):
        tag, ks, plan, ssem, rsem, bufs = swap
        n = len(ks)
        thru = _split_wait(f"grad_swap_wait_{tag}", ssem, rsem, bufs, plan, after)
        chip_sums = [_add_half(v, t, cvec, "grad_add_half") for v, t in zip(thru[:n], thru[n:])]
        slots = [lax.empty((4, p.shape[0], p.shape[1] // 4) if kinds[k] == "col" else p.shape, BF16) for p, k in zip(chip_sums, ks)]
        xplan = _exchange_plan(n, [kinds[k] for k in ks])
        ssem, rsem, bufs, tok = _split_start(f"grad_exchange_start_{tag}", chip_sums + slots, xplan, 3 * n)
        return (tag, ks, xplan, ssem, rsem, bufs), tok

    def finish_exchange(layer, pending, after):
        tag, ks, xplan, ssem, rsem, bufs = pending
        n = len(ks)
        landed = _split_wait(f"grad_exchange_wait_{tag}", ssem, rsem, bufs, xplan, after)
        halves = [_sum_chips(landed[n + a], landed[a], kinds[k], place, "grad_sum_chips") for a, k in enumerate(ks)]
        for k, j in zip(ks, _join_halves(halves)):
            shard_grads[layer][k] = j.reshape(2 * j.shape[1], j.shape[2])

    swap = None
    swap_tok = None
    exchange = None
    for l in reversed(range(L)):
        s = saved[l]
        wi, wo, wgu, wdn = full[l]
        g_po = s["g_po"] if swap is None else s["g_po"] + swap_tok[0:1, 0:1]
        dffo, dgu, dh1, dmix, dyab, dyc, dg_po, dg_pf, dg_pm, dgg = _layer_tail_bwd(
            dh, s["ffo"], g_po, s["gu"], wdn, wgu, s["h1"], s["g_pf"], s["mix"], s["g_pm"], wo, s["yab"], s["yc"], s["gg"])
        after = dyc
        if swap is not None:
            started, after = start_exchange(swap, dh1)
            if exchange is not None:
                finish_exchange(l + 2, exchange, after)
            exchange = started
        F2, DFF = s["gu"].shape[1], s["ff"].shape[1]
        ffn_grads = lambda: [_wgrad(s["hn"], dgu, D, F2 // 4, "wgrad_gate_up"), _wgrad(s["ff"], dffo, DFF // 2, D, "wgrad_down")]
        g_pre = s["g_pre"]
        if l == 0:
            ffn_swap, after = start_swap("0f", ffn_grads(), (2, 3), after=after)
        dq, dk, dv, dbiases[l] = _attn_bwd(s["qkv"], s["kv_t"], dyc, s["yc"], s["lse"], s["bias"], after)
        if l == 0:
            ffn_exchange, tok = start_exchange(ffn_swap, dq)
            g_pre = g_pre + tok[0:1, 0:1]
        dpa, dh, dcw, dwbd, dps, dg_pre = _inproj_mixers_bwd(s["pa"], dyab, s["cw"], s["wbd"], s["ps"], [dq, dk, dv], wi,
                                                             s["h"], g_pre, dh1)
        dparts = [dpa, dq, dk, dv]
        mixer_grads = [_wgrad_concat(s["xn"], dparts, "wgrad_in"), _wgrad(s["y"], dmix, D, D, "wgrad_out")]
        small_grads[l] = [dcw[:3], _diag_blocks(dwbd, G), dps[0], None, dgg[0], dg_pre[0], dg_pm[0], dg_pf[0], dg_po[0]]
        if l == 0:
            swap, swap_tok = start_swap("0m", mixer_grads, (0, 1))
        else:
            swap, swap_tok = start_swap(str(l), mixer_grads + ffn_grads(), (0, 1, 2, 3))
    last, tok = start_exchange(swap, swap_tok)
    if exchange is not None:
        finish_exchange(1, exchange, tok)
    for l in range(L):
        small_grads[l][3] = _bias_fold(dbiases[l], tok)[:NH, :NREL]

    names_shapes = [(L, 3, 4 * cs), pool_w.shape, pool_scale.shape, rel_bias.shape, group_gain.shape,
                    pre_mix_g.shape, post_mix_g.shape, pre_ffn_g.shape, post_ffn_g.shape]
    small_stacked = [jnp.stack([small_grads[l][k] for l in range(L)]) for k in range(len(names_shapes))]
    packed = _pack(small_stacked)
    M = packed.shape[0]
    total = _sum_slots(_allgather_small(packed).reshape(8, M, LANES), M, "small_sum_devices")
    g_small = _unpack(total, names_shapes)
    g_small[0] = lax.dynamic_slice_in_dim(g_small[0], chip * cs, cs, axis=2)
    finish_exchange(0, ffn_exchange, total)
    finish_exchange(0, last, total)
    g_big = [jnp.stack([shard_grads[l][k] for l in range(L)]) for k in range(4)]

    def adam_big(w, g, m, v, name):
        shp = w.shape
        two = lambda a: a.reshape(shp[0] * shp[1], shp[2])
        return [o.reshape(shp) for o in _adamw(two(w), two(g), two(m), two(v), 256, name)]

    upd_in = adam_big(w_in, g_big[0], m_w_in, v_w_in, "adamw_in")
    upd_out = adam_big(w_out, g_big[1], m_w_out, v_w_out, "adamw_out")
    upd_gu = adam_big(w_gate_up, g_big[2], m_w_gate_up, v_w_gate_up, "adamw_gate_up")
    upd_dn = adam_big(w_down, g_big[3], m_w_down, v_w_down, "adamw_down")

    small_w = [conv_w, pool_w, pool_scale, rel_bias, group_gain, pre_mix_g, post_mix_g, pre_ffn_g, post_ffn_g]
    small_m = [m_conv_w, m_pool_w, m_pool_scale, m_rel_bias, m_group_gain, m_pre_mix_g, m_post_mix_g, m_pre_ffn_g, m_post_ffn_g]
    small_v = [v_conv_w, v_pool_w, v_pool_scale, v_rel_bias, v_group_gain, v_pre_mix_g, v_post_mix_g, v_pre_ffn_g, v_post_ffn_g]
    pw_, pg_, pm_, pv_ = _pack(small_w), _pack(g_small), _pack(small_m), _pack(small_v)
    shapes = [w.shape for w in small_w]
    upd_small = [_unpack(o, shapes) for o in _adamw(pw_, pg_, pm_, pv_, pw_.shape[0], "adamw_small")]

    def ordered(big4, small9):
        return [big4[0], big4[1], *small9, big4[2], big4[3]]

    grads = ordered(g_big, g_small)
    outs = [ordered([upd_in[k], upd_out[k], upd_gu[k], upd_dn[k]], upd_small[k]) for k in range(3)]
    return (loss, dh[None], *grads, *outs[0], *outs[1], *outs[2])
```

```python
import functools

import jax
import jax.numpy as jnp
from jax import lax
from jax.experimental import pallas as pl
from jax.experimental.pallas import tpu as pltpu

F32, BF16 = jnp.float32, jnp.bfloat16
EPS = 1e-6
CHUNK = 64
LEFT_CHUNKS = 8
REL_CLIP = 128
HEAD_DIM = 64
LANES = 128
POOL_WINDOWS = (2, 4, 8, 16)
HALO = 16
TB = LEFT_CHUNKS * CHUNK
TBF = 256
TBW = 1024
BAND = (LEFT_CHUNKS + 1) * CHUNK
SUB = 2 * CHUNK
BANDW = SUB + LEFT_CHUNKS * CHUNK
SKEW = 768
NEG = -1e30
RBP = 384
VMEM_LIMIT = 56 * 1024 * 1024
ADAM_LR, ADAM_B1, ADAM_B2, ADAM_EPS, ADAM_WD, ADAM_STEP = 0.001, 0.9, 0.999, 1e-08, 0.01, 10
MESH = pl.DeviceIdType.MESH
ANY = pl.BlockSpec(memory_space=pl.ANY)


def _params(*sem):
    kw = dict(vmem_limit_bytes=VMEM_LIMIT)
    if sem:
        kw["dimension_semantics"] = sem
    return pltpu.CompilerParams(**kw)


def _dot(a, b):
    return jnp.dot(a, b, preferred_element_type=F32)


def _dot_nt(a, b):
    return lax.dot_general(a, b, (((1,), (1,)), ((), ())), preferred_element_type=F32)


def _dot_tn(a, b):
    return lax.dot_general(a, b, (((0,), (0,)), ((), ())), preferred_element_type=F32)


def _rms(x, g):
    r = lax.rsqrt(jnp.mean(x * x, axis=-1, keepdims=True) + EPS)
    return x * r * g


def _rms_bwd(dy, x, g):
    r = lax.rsqrt(jnp.mean(x * x, axis=-1, keepdims=True) + EPS)
    xh = x * r
    dxh = dy * g
    dx = r * (dxh - xh * jnp.mean(dxh * xh, axis=-1, keepdims=True))
    return dx, jnp.sum(dy * xh, axis=0, keepdims=True)


def _full(shape):
    return pl.BlockSpec(shape, lambda *_: (0,) * len(shape))


def _acc_init(step, *refs):
    @pl.when(step == 0)
    def _():
        for r in refs:
            r[...] = jnp.zeros_like(r)


def _inproj_mixers_fwd(h, g, w, cw, wbd, ps):
    T, D = h.shape
    NQ = w.shape[1] - D
    NKV = 2 * NQ // 3
    DC = D // 4

    def body(h_ref, g_ref, w_ref, cw_ref, wbd_ref, ps_ref, xn_ref, pa_ref, qkv_ref, kvt_ref, yab_ref, halo):
        i = pl.program_id(0)
        _acc_init(i, halo)
        xn = _rms(h_ref[...], g_ref[...]).astype(BF16)
        xn_ref[...] = xn
        x = _dot(xn, w_ref[:, :D])
        pa_ref[...] = x
        qkv = _dot(xn, w_ref[:, D:])
        qkv_ref[...] = qkv.astype(BF16)
        kvt_ref[...] = qkv[:, NQ - NKV:].T.astype(BF16)
        hl = halo[...]
        gb, gc, u, pu = (x[:, k * DC:(k + 1) * DC] for k in range(4))
        z = gc * u
        z1, z2 = _conv_taps(z, hl[:, DC:2 * DC] * hl[:, 2 * DC:3 * DC])
        cwv = cw_ref[...]
        ya = gb * (cwv[2:3] * z + cwv[1:2] * z1 + cwv[0:1] * z2)
        d = _pool_d(pu, hl[:, 3 * DC:], _pool_count(i, DC))
        yb = _dot(d.astype(BF16), wbd_ref[...]) * ps_ref[...]
        yab_ref[...] = jnp.concatenate([ya, yb], axis=1)
        halo[...] = x[TB - HALO:, :]

    row = lambda n: pl.BlockSpec((TB, n), lambda i: (i, 0))
    return pl.pallas_call(
        body, name="inproj_mixers_fwd", grid=(T // TB,),
        in_specs=[row(D), _full((1, D)), _full(w.shape), _full((8, DC)), _full((DC, DC)), _full((1, DC))],
        out_specs=[row(D), row(D), row(NQ), pl.BlockSpec((NKV, TB), lambda i: (0, i)), row(2 * DC)],
        out_shape=[jax.ShapeDtypeStruct((T, D), BF16), jax.ShapeDtypeStruct((T, D), F32), jax.ShapeDtypeStruct((T, NQ), BF16),
                   jax.ShapeDtypeStruct((NKV, T), BF16), jax.ShapeDtypeStruct((T, 2 * DC), F32)],
        scratch_shapes=[pltpu.VMEM((HALO, D), F32)],
        compiler_params=_params("arbitrary"),
    )(h, g, w, cw, wbd, ps)


def _lane_groups(n, vals):
    lane = lax.broadcasted_iota(jnp.int32, (1, n), 1)
    q = n // 4
    return jnp.where(lane < q, vals[0], jnp.where(lane < 2 * q, vals[1], jnp.where(lane < 3 * q, vals[2], vals[3]))).astype(F32)


def _pick_group(levels, n):
    lane = lax.broadcasted_iota(jnp.int32, (1, n), 1)
    q = n // 4
    return jnp.where(lane < q, levels[0], jnp.where(lane < 2 * q, levels[1], jnp.where(lane < 3 * q, levels[2], levels[3])))


def _pool_count(blk, n):
    t1 = (blk * TB + 1 + lax.broadcasted_iota(jnp.int32, (TB, 1), 0)).astype(F32)
    return jnp.minimum(t1, _lane_groups(n, POOL_WINDOWS))


def _pool_d(pu, pu_halo, cnt):
    e = jnp.concatenate([pu_halo, pu], axis=0)
    s2 = e + pltpu.roll(e, 1, 0)
    s4 = s2 + pltpu.roll(s2, 2, 0)
    s8 = s4 + pltpu.roll(s4, 4, 0)
    s16 = s8 + pltpu.roll(s8, 8, 0)
    num = _pick_group([s2, s4, s8, s16], pu.shape[1])[HALO:]
    return num / cnt - pu


def _conv_taps(z, z_halo):
    e = jnp.concatenate([z_halo, z], axis=0)
    return pltpu.roll(e, 1, 0)[HALO:], pltpu.roll(e, 2, 0)[HALO:]


def _bias_bins(shape, col_dim):
    j = lax.broadcasted_iota(jnp.int32, shape, col_dim)
    b = lax.broadcasted_iota(jnp.int32, shape, 1 - col_dim)
    d = jnp.where(j < BAND, j, j - SKEW)
    live = jnp.logical_or(j < BAND, j > SKEW - CHUNK)
    bins = jnp.minimum(TB - d, REL_CLIP) + REL_CLIP
    return jnp.where(jnp.logical_and(live, bins == b), 1.0, 0.0).astype(F32)


def _skew_rows(x, left):
    row = lax.broadcasted_iota(jnp.int32, x.shape, 0)
    for b in range(SUB.bit_length() - 1):
        x = jnp.where(((row >> b) & 1) == 1, pltpu.roll(x, SKEW - (1 << b) if left else 1 << b, 1), x)
    return x


def _bias_build(rb):
    H = rb.shape[0]

    def body(rb_ref, o_ref):
        v = jnp.dot(rb_ref[...], _bias_bins((RBP, SKEW), 1), precision=lax.Precision.HIGHEST, preferred_element_type=F32)
        qc = lax.broadcasted_iota(jnp.int32, (SUB, BANDW), 0) >> (CHUNK.bit_length() - 1)
        kc = lax.broadcasted_iota(jnp.int32, (SUB, BANDW), 1) >> (CHUNK.bit_length() - 1)
        in_band = jnp.logical_and(kc >= qc, kc <= qc + LEFT_CHUNKS)
        for h in range(H):
            x = _skew_rows(jnp.broadcast_to(v[h:h + 1], (SUB, SKEW)), left=False)
            o_ref[h // 2, :, (h % 2) * SUB:(h % 2 + 1) * SUB] = jnp.where(in_band, x[:, :BANDW], NEG).T

    return pl.pallas_call(
        body, name="bias_build", in_specs=[_full(rb.shape)], out_specs=_full((H // 2, BANDW, 2 * SUB)),
        out_shape=jax.ShapeDtypeStruct((H // 2, BANDW, 2 * SUB), F32), grid=(1,),
        compiler_params=_params("arbitrary"),
    )(rb)


def _bias_fold(db, after):
    H = 2 * db.shape[0]

    def body(db_ref, after_ref, o_ref, sums):
        sums[...] = jnp.zeros_like(sums)
        for h in range(H):
            x = db_ref[h // 2, :, (h % 2) * SUB:(h % 2 + 1) * SUB].T
            x = _skew_rows(jnp.concatenate([x, jnp.zeros((SUB, SKEW - BANDW), F32)], axis=1), left=True)
            sums[h:h + 1, :] = jnp.sum(x, axis=0, keepdims=True)
        o_ref[...] = jnp.dot(sums[...], _bias_bins((SKEW, RBP), 0), precision=lax.Precision.HIGHEST, preferred_element_type=F32)

    return pl.pallas_call(
        body, name="bias_fold", grid=(1,), in_specs=[_full(db.shape), ANY], out_specs=_full((8, RBP)),
        out_shape=jax.ShapeDtypeStruct((8, RBP), F32), scratch_shapes=[pltpu.VMEM((8, SKEW), F32)],
        compiler_params=_params("arbitrary"),
    )(db, after)


def _both_heads(x):
    first = lax.broadcasted_iota(jnp.int32, (1, LANES), 1) < HEAD_DIM
    return jnp.concatenate([jnp.where(first, x, 0), jnp.where(first, 0, x)], axis=0)


def _own_head_rows(x2):
    n = x2.shape[1] // 2
    first = lax.broadcasted_iota(jnp.int32, (LANES, 1), 0) < HEAD_DIM
    return jnp.where(first, x2[:, :n], x2[:, n:])


def _key_tiles(s, first_block):
    return [t for t in range(s, s + BANDW // SUB) if not (first_block and t < TB // SUB)]


def _attn_fwd(qkv, kv_t, bias_t):
    T = qkv.shape[0]
    NP = qkv.shape[1] // (3 * LANES)
    NB = T // TB
    NS = TB // SUB
    scale = HEAD_DIM ** -0.5

    def body(q_ref, kc_ref, kp_ref, vtc_ref, vtp_ref, b_ref, o_ref, lse_ref):
        i = pl.program_id(1)

        def compute(first_block):
            q = q_ref[...] * scale
            kwin = jnp.concatenate([kp_ref[...], kc_ref[...]], axis=0)
            vt = jnp.concatenate([vtp_ref[...], vtc_ref[...]], axis=1)
            for s in range(NS):
                rows = slice(s * SUB, (s + 1) * SUB)
                tiles = _key_tiles(s, first_block)
                keys = slice(tiles[0] * SUB, (tiles[-1] + 1) * SUB)
                brows = slice((tiles[0] - s) * SUB, (tiles[-1] - s + 1) * SUB)
                q2 = _both_heads(q[rows])
                halves = []
                for a in range(2):
                    st = _dot_nt(kwin[keys], q2[a * SUB:(a + 1) * SUB]) + b_ref[0, brows, a * SUB:(a + 1) * SUB]
                    m = jnp.max(st, axis=0, keepdims=True)
                    p = jnp.exp(st - m)
                    l = jnp.sum(p, axis=0, keepdims=True)
                    halves.append(_dot(vt[:, keys], p.astype(BF16)) * (1.0 / l))
                    lse_ref[0, 0, a * NS + s:a * NS + s + 1, :] = m + jnp.log(l)
                o_ref[rows, :] = _own_head_rows(jnp.concatenate(halves, axis=1)).T

        pl.when(i == 0)(functools.partial(compute, True))
        pl.when(i > 0)(functools.partial(compute, False))

    prev = lambda i: jnp.maximum(i - 1, 0)
    return pl.pallas_call(
        body, name="attn_fwd", grid=(NP, NB),
        in_specs=[pl.BlockSpec((TB, LANES), lambda p, i: (i, p)),
                  pl.BlockSpec((TB, LANES), lambda p, i: (i, NP + p)),
                  pl.BlockSpec((TB, LANES), lambda p, i: (prev(i), NP + p)),
                  pl.BlockSpec((LANES, TB), lambda p, i: (NP + p, i)),
                  pl.BlockSpec((LANES, TB), lambda p, i: (NP + p, prev(i))),
                  pl.BlockSpec((1, BANDW, 2 * SUB), lambda p, i: (p, 0, 0))],
        out_specs=[pl.BlockSpec((TB, LANES), lambda p, i: (i, p)),
                   pl.BlockSpec((1, 1, 8, LANES), lambda p, i: (p, i, 0, 0))],
        out_shape=[jax.ShapeDtypeStruct((T, NP * LANES), F32), jax.ShapeDtypeStruct((NP, NB, 8, LANES), F32)],
        compiler_params=_params("parallel", "parallel"),
    )(qkv, qkv, qkv, kv_t, kv_t, bias_t)


def _group_bounds(D):
    return ((0, D // 4), (D // 4, D // 2), (D // 2, D))


def _ff_tiles(dff):
    half = dff // 2
    return [(slice(a, a + half), slice(dff + a, dff + a + half)) for a in (0, half)]


def _load_resident(step, *pairs_and_sems):
    @pl.when(step == 0)
    def _():
        cps = [pltpu.make_async_copy(src, dst, sem) for src, dst, sem in pairs_and_sems]
        for cp in cps:
            cp.start()
        for cp in cps:
            cp.wait()


def _layer_tail_fwd(yab, yc, gg, w_out, h, g_pm, g_pf, w_gu, w_dn, g_po):
    T, D = h.shape
    F2 = w_gu.shape[1]
    DFF = F2 // 2

    def body(yab_ref, yc_ref, gg_ref, wo_hbm, h_ref, gpm_ref, gpf_ref, wgu_hbm, wdn_hbm, gpo_ref,
             y_ref, mix_ref, h1_ref, hn_ref, gu_ref, ff_ref, ffo_ref, h2_ref, wo_v, wgu_v, wdn_v, sems):
        _load_resident(pl.program_id(0), (wo_hbm, wo_v, sems.at[0]), (wgu_hbm, wgu_v, sems.at[1]), (wdn_hbm, wdn_v, sems.at[2]))
        yraw = jnp.concatenate([yab_ref[...], yc_ref[...]], axis=1)
        ggv = gg_ref[...]
        y = jnp.concatenate([_rms(yraw[:, a:b], ggv[:, a:b]) for a, b in _group_bounds(D)], axis=1).astype(BF16)
        y_ref[...] = y
        mix = _dot(y, wo_v[...])
        mix_ref[...] = mix.astype(BF16)
        h1 = h_ref[...] + _rms(mix, gpm_ref[...])
        h1_ref[...] = h1
        hn = _rms(h1, gpf_ref[...]).astype(BF16)
        hn_ref[...] = hn
        ffo = jnp.zeros((TBF, D), F32)
        for sg, su in _ff_tiles(DFF):
            gate = _dot(hn, wgu_v[:, sg])
            up = _dot(hn, wgu_v[:, su])
            gu_ref[:, sg] = gate.astype(BF16)
            gu_ref[:, su] = up.astype(BF16)
            ff = (gate * jax.nn.sigmoid(gate) * up).astype(BF16)
            ff_ref[:, sg] = ff
            ffo = ffo + _dot(ff, wdn_v[sg, :])
        ffo_ref[...] = ffo.astype(BF16)
        h2_ref[...] = h1 + _rms(ffo, gpo_ref[...])

    row = lambda n: pl.BlockSpec((TBF, n), lambda i: (i, 0))
    gain = _full((1, D))
    f32, bf16 = (lambda n: jax.ShapeDtypeStruct((T, n), F32)), (lambda n: jax.ShapeDtypeStruct((T, n), BF16))
    return pl.pallas_call(
        body, name="layer_tail_fwd", grid=(T // TBF,),
        in_specs=[row(D // 2), row(D // 2), gain, ANY, row(D), gain, gain, ANY, ANY, gain],
        out_specs=[row(D), row(D), row(D), row(D), row(F2), row(DFF), row(D), row(D)],
        out_shape=[bf16(D), bf16(D), f32(D), bf16(D), bf16(F2), bf16(DFF), bf16(D), f32(D)],
        scratch_shapes=[pltpu.VMEM(w_out.shape, BF16), pltpu.VMEM(w_gu.shape, BF16), pltpu.VMEM(w_dn.shape, BF16),
                        pltpu.SemaphoreType.DMA((3,))],
        compiler_params=_params("arbitrary"),
    )(yab, yc, gg, w_out, h, g_pm, g_pf, w_gu, w_dn, g_po)


def _loss_grad(h, tgt):
    T, D = h.shape

    def body(h_ref, t_ref, dh_ref, loss_ref):
        _acc_init(pl.program_id(0), loss_ref)
        diff = h_ref[...] - t_ref[...]
        dh_ref[...] = diff * (1.0 / D)
        loss_ref[...] += 0.5 * jnp.sum(jnp.mean(diff * diff, axis=-1, keepdims=True))

    row = pl.BlockSpec((TB, D), lambda i: (i, 0))
    return pl.pallas_call(
        body, name="loss_grad", grid=(T // TB,),
        in_specs=[row, row], out_specs=[row, _full((8, LANES))],
        out_shape=[jax.ShapeDtypeStruct((T, D), F32), jax.ShapeDtypeStruct((8, LANES), F32)],
        compiler_params=_params("arbitrary"),
    )(h, tgt)


def _layer_tail_bwd(dh2, ffo, g_po, gu, w_dn, w_gu, h1, g_pf, mix, g_pm, w_out, yab, yc, gg):
    T, D = dh2.shape
    F2 = gu.shape[1]
    DFF = F2 // 2

    def body(dh_ref, ffo_ref, gpo_ref, gu_ref, wdn_hbm, wgu_hbm, h_ref, gpf_ref, mix_ref, gpm_ref, wo_hbm, yab_ref, yc_ref, gg_ref,
             dffo_ref, dgu_ref, dh1_ref, dmix_ref, dyab_ref, dyc_ref, dgpo_ref, dgpf_ref, dgpm_ref, dgg_ref,
             wdn_v, wgu_v, wo_v, sems):
        step = pl.program_id(0)
        _load_resident(step, (wdn_hbm, wdn_v, sems.at[0]), (wgu_hbm, wgu_v, sems.at[1]), (wo_hbm, wo_v, sems.at[2]))
        _acc_init(step, dgpo_ref, dgpf_ref, dgpm_ref, dgg_ref)
        dh = dh_ref[...]
        dffo, dg = _rms_bwd(dh, ffo_ref[...].astype(F32), gpo_ref[...])
        dgpo_ref[0:1, :] += dg
        dffo = dffo.astype(BF16)
        dffo_ref[...] = dffo
        dhn = jnp.zeros((TBF, D), F32)
        for sg, su in _ff_tiles(DFF):
            dff = _dot_nt(dffo, wdn_v[sg, :])
            gate, up = gu_ref[:, sg].astype(F32), gu_ref[:, su].astype(F32)
            sig = jax.nn.sigmoid(gate)
            dgate = (dff * up * (sig * (1.0 + gate * (1.0 - sig)))).astype(BF16)
            dup = (dff * (gate * sig)).astype(BF16)
            dgu_ref[:, sg] = dgate
            dgu_ref[:, su] = dup
            dhn = dhn + _dot_nt(dgate, wgu_v[:, sg]) + _dot_nt(dup, wgu_v[:, su])
        dx, dg = _rms_bwd(dhn, h_ref[...], gpf_ref[...])
        dgpf_ref[0:1, :] += dg
        dh1 = dh + dx
        dh1_ref[...] = dh1
        dmix, dg = _rms_bwd(dh1, mix_ref[...].astype(F32), gpm_ref[...])
        dgpm_ref[0:1, :] += dg
        dmix = dmix.astype(BF16)
        dmix_ref[...] = dmix
        dy = _dot_nt(dmix, wo_v[...])
        yraw = jnp.concatenate([yab_ref[...], yc_ref[...]], axis=1)
        ggv = gg_ref[...]
        parts = [_rms_bwd(dy[:, a:b], yraw[:, a:b], ggv[:, a:b]) for a, b in _group_bounds(D)]
        dgg_ref[0:1, :] += jnp.concatenate([p[1] for p in parts], axis=1)
        dyab_ref[...] = jnp.concatenate([parts[0][0], parts[1][0]], axis=1).astype(BF16)
        dyc_ref[...] = parts[2][0].astype(BF16)

    row = lambda n: pl.BlockSpec((TBF, n), lambda i: (i, 0))
    gain, acc = _full((1, D)), _full((8, D))
    f32, bf16 = (lambda n: jax.ShapeDtypeStruct((T, n), F32)), (lambda n: jax.ShapeDtypeStruct((T, n), BF16))
    acc_shape = jax.ShapeDtypeStruct((8, D), F32)
    return pl.pallas_call(
        body, name="layer_tail_bwd", grid=(T // TBF,),
        in_specs=[row(D), row(D), gain, row(F2), ANY, ANY, row(D), gain, row(D), gain, ANY, row(D // 2), row(D // 2), gain],
        out_specs=[row(D), row(F2), row(D), row(D), row(D // 2), row(D // 2), acc, acc, acc, acc],
        out_shape=[bf16(D), bf16(F2), f32(D), bf16(D), bf16(D // 2), bf16(D // 2), acc_shape, acc_shape, acc_shape, acc_shape],
        scratch_shapes=[pltpu.VMEM(w_dn.shape, BF16), pltpu.VMEM(w_gu.shape, BF16), pltpu.VMEM(w_out.shape, BF16),
                        pltpu.SemaphoreType.DMA((3,))],
        compiler_params=_params("arbitrary"),
    )(dh2, ffo, g_po, gu, w_dn, w_gu, h1, g_pf, mix, g_pm, w_out, yab, yc, gg)


def _attn_bwd(qkv, kv_t, dyc, yc, lse, bias_t, after):
    T = qkv.shape[0]
    NP = qkv.shape[1] // (3 * LANES)
    NB = T // TB
    NS = TB // SUB
    scale = HEAD_DIM ** -0.5

    def body(q_ref, kc_ref, kp_ref, vc_ref, vp_ref, ktc_ref, ktp_ref, do_ref, o_ref, lse_ref, b_ref, after_ref,
             dq_ref, dk_ref, dv_ref, db_ref, dk_carry, dv_carry, dkw, dvw):
        step = pl.program_id(1)
        i = NB - 1 - step
        _acc_init(step, dk_carry, dv_carry, db_ref)

        def compute(first_block):
            q = q_ref[...] * scale
            kwin = jnp.concatenate([kp_ref[...], kc_ref[...]], axis=0)
            vwin = jnp.concatenate([vp_ref[...], vc_ref[...]], axis=0)
            kt = jnp.concatenate([ktp_ref[...], ktc_ref[...]], axis=1)
            do = do_ref[...].astype(F32)
            dob = do.astype(BF16)
            prod = do * o_ref[...]
            first = lax.broadcasted_iota(jnp.int32, (1, LANES), 1) < HEAD_DIM
            ones = jnp.ones((8, LANES), F32)
            row_sums = lambda x: lax.dot_general(ones, x, (((1,), (1,)), ((), ())), precision=lax.Precision.HIGHEST,
                                                 preferred_element_type=F32)
            deltas = (row_sums(jnp.where(first, prod, 0.0)), row_sums(jnp.where(first, 0.0, prod)))
            written = set()
            for s in range(NS):
                rows = slice(s * SUB, (s + 1) * SUB)
                tiles = _key_tiles(s, first_block)
                keys = slice(tiles[0] * SUB, (tiles[-1] + 1) * SUB)
                brows = slice((tiles[0] - s) * SUB, (tiles[-1] - s + 1) * SUB)
                q2, do2 = _both_heads(q[rows]), _both_heads(dob[rows])
                lse = jnp.concatenate([lse_ref[0, 0, s:s + 1, :], lse_ref[0, 0, NS + s:NS + s + 1, :]], axis=1)
                delta = jnp.concatenate([deltas[0][0:1, rows], deltas[1][0:1, rows]], axis=1)
                p = jnp.exp(_dot_nt(kwin[keys], q2) + b_ref[0, brows, :] - lse)
                ds = p * (_dot_nt(vwin[keys], do2) - delta)
                db_ref[0, brows, :] += ds
                dsb = ds.astype(BF16)
                dk_c = _dot(dsb, q2)
                dv_c = _dot(p.astype(BF16), do2)
                dq_ref[rows, :] = (_own_head_rows(_dot(kt[:, keys], dsb)).T * scale).astype(BF16)
                for n, t in enumerate(tiles):
                    win, loc = slice(t * SUB, (t + 1) * SUB), slice(n * SUB, (n + 1) * SUB)
                    if t in written:
                        dkw[win, :] += dk_c[loc]
                        dvw[win, :] += dv_c[loc]
                    else:
                        dkw[win, :] = dk_c[loc]
                        dvw[win, :] = dv_c[loc]
                        written.add(t)
            dk_ref[...] = (dkw[TB:, :] + dk_carry[...]).astype(BF16)
            dv_ref[...] = (dvw[TB:, :] + dv_carry[...]).astype(BF16)
            if not first_block:
                dk_carry[...] = dkw[:TB, :]
                dv_carry[...] = dvw[:TB, :]

        pl.when(i == 0)(functools.partial(compute, True))
        pl.when(i > 0)(functools.partial(compute, False))

    blk = lambda s: NB - 1 - s
    prev = lambda s: jnp.maximum(NB - 2 - s, 0)
    rows = lambda which, off: pl.BlockSpec((TB, LANES), lambda p, s: (which(s), off + p))
    out = jax.ShapeDtypeStruct((T, NP * LANES), BF16)
    strip = pl.BlockSpec((1, BANDW, 2 * SUB), lambda p, s: (p, 0, 0))
    return pl.pallas_call(
        body, name="attn_bwd", grid=(NP, NB),
        in_specs=[rows(blk, 0), rows(blk, NP), rows(prev, NP), rows(blk, 2 * NP), rows(prev, 2 * NP),
                  pl.BlockSpec((LANES, TB), lambda p, s: (p, blk(s))), pl.BlockSpec((LANES, TB), lambda p, s: (p, prev(s))),
                  rows(blk, 0), rows(blk, 0), pl.BlockSpec((1, 1, 8, LANES), lambda p, s: (p, blk(s), 0, 0)), strip, ANY],
        out_specs=[rows(blk, 0), rows(blk, 0), rows(blk, 0), strip],
        out_shape=[out, out, out, jax.ShapeDtypeStruct((NP, BANDW, 2 * SUB), F32)],
        scratch_shapes=[pltpu.VMEM((TB, LANES), F32), pltpu.VMEM((TB, LANES), F32),
                        pltpu.VMEM((2 * TB, LANES), F32), pltpu.VMEM((2 * TB, LANES), F32)],
        compiler_params=_params("arbitrary", "arbitrary"),
    )(qkv, qkv, qkv, qkv, qkv, kv_t, kv_t, dyc, yc, lse, bias_t, after)


def _inproj_mixers_bwd(pa, dyab, cw, wbd, ps, dqkv, w, h, g, dh1):
    T, D = pa.shape
    DC = D // 4
    NB = T // TB
    N = TB + HALO
    widths = [p.shape[1] for p in dqkv]
    n = len(dqkv)

    def body(*refs):
        pa_ref, halo_ref, dy_ref, cw_ref, wbd_ref, ps_ref = refs[:6]
        parts = refs[6:6 + n]
        w_ref, h_ref, g_ref, dh1_ref, dpa_ref, dh_ref, dcw_ref, dwbd_ref, dps_ref, dg_ref, dc_carry, e_carry = refs[6 + n:]
        step = pl.program_id(0)
        i = NB - 1 - step
        _acc_init(step, dcw_ref, dwbd_ref, dps_ref, dg_ref, dc_carry, e_carry)
        x = pa_ref[...]
        hl = jnp.where(i > 0, halo_ref[...], 0.0)
        gb, gc, u, pu = (x[:, k * DC:(k + 1) * DC] for k in range(4))
        dy = dy_ref[...].astype(F32)
        dya, dyb = dy[:, :DC], dy[:, DC:]
        cwv = cw_ref[...]
        z = gc * u
        z1, z2 = _conv_taps(z, hl[:, DC:2 * DC] * hl[:, 2 * DC:3 * DC])
        dgb = dya * (cwv[2:3] * z + cwv[1:2] * z1 + cwv[0:1] * z2)
        dconv = dya * gb
        dcw_ref[0:1, :] += jnp.sum(dconv * z2, axis=0, keepdims=True)
        dcw_ref[1:2, :] += jnp.sum(dconv * z1, axis=0, keepdims=True)
        dcw_ref[2:3, :] += jnp.sum(dconv * z, axis=0, keepdims=True)
        ext = jnp.concatenate([dconv, dc_carry[...]], axis=0)
        dz = cwv[2:3] * dconv + cwv[1:2] * pltpu.roll(ext, N - 1, 0)[:TB] + cwv[0:1] * pltpu.roll(ext, N - 2, 0)[:TB]
        dc_carry[...] = dconv[:HALO]
        cnt = _pool_count(i, DC)
        d = _pool_d(pu, hl[:, 3 * DC:], cnt).astype(BF16)
        psv = ps_ref[...]
        wb = wbd_ref[...]
        dps_ref[0:1, :] += jnp.sum(dyb * _dot(d, wb), axis=0, keepdims=True)
        dys = (dyb * psv).astype(BF16)
        dwbd_ref[...] += _dot_tn(d, dys)
        dd = _dot_nt(dys, wb)
        e = dd / cnt
        ext = jnp.concatenate([e, e_carry[...]], axis=0)
        a2 = ext + pltpu.roll(ext, N - 1, 0)
        a4 = a2 + pltpu.roll(a2, N - 2, 0)
        a8 = a4 + pltpu.roll(a4, N - 4, 0)
        a16 = a8 + pltpu.roll(a8, N - 8, 0)
        dpu = _pick_group([a2, a4, a8, a16], DC)[:TB] - dd
        e_carry[...] = e[:HALO]
        dpa = jnp.concatenate([dgb, dz * u, dz * gc, dpu], axis=1).astype(BF16)
        dpa_ref[...] = dpa
        dxn, off = _dot_nt(dpa, w_ref[:, :D]), D
        for p_ref, wd in zip(parts, widths):
            dxn = dxn + _dot_nt(p_ref[...], w_ref[:, off:off + wd])
            off += wd
        dx, dg = _rms_bwd(dxn, h_ref[...], g_ref[...])
        dg_ref[0:1, :] += dg
        dh_ref[...] = dh1_ref[...] + dx

    blk = lambda m: pl.BlockSpec((TB, m), lambda s: (NB - 1 - s, 0))
    return pl.pallas_call(
        body, name="inproj_mixers_bwd", grid=(NB,),
        in_specs=[blk(D), pl.BlockSpec((HALO, D), lambda s: (jnp.maximum((NB - 1 - s) * (TB // HALO) - 1, 0), 0)),
                  blk(2 * DC), _full((8, DC)), _full((DC, DC)), _full((1, DC))] + [blk(wd) for wd in widths]
                 + [_full(w.shape), blk(D), _full((1, D)), blk(D)],
        out_specs=[blk(D), blk(D), _full((8, DC)), _full((DC, DC)), _full((8, DC)), _full((8, D))],
        out_shape=[jax.ShapeDtypeStruct((T, D), BF16), jax.ShapeDtypeStruct((T, D), F32), jax.ShapeDtypeStruct((8, DC), F32),
                   jax.ShapeDtypeStruct((DC, DC), F32), jax.ShapeDtypeStruct((8, DC), F32), jax.ShapeDtypeStruct((8, D), F32)],
        scratch_shapes=[pltpu.VMEM((HALO, DC), F32), pltpu.VMEM((HALO, DC), F32)],
        compiler_params=_params("arbitrary"),
    )(pa, pa, dyab, cw, wbd, ps, *dqkv, w, h, g, dh1)


def _wgrad(a, b, tk, tn, name):
    T, K = a.shape
    N = b.shape[1]
    tt = min(2 * TBW, T)

    def body(a_ref, b_ref, o_ref):
        _acc_init(pl.program_id(2), o_ref)
        o_ref[...] += _dot_tn(a_ref[...], b_ref[...])

    return pl.pallas_call(
        body, name=name, grid=(K // tk, N // tn, T // tt),
        in_specs=[pl.BlockSpec((tt, tk), lambda k, n, t: (t, k)), pl.BlockSpec((tt, tn), lambda k, n, t: (t, n))],
        out_specs=pl.BlockSpec((tk, tn), lambda k, n, t: (k, n)),
        out_shape=jax.ShapeDtypeStruct((K, N), F32),
        compiler_params=_params("parallel", "parallel", "arbitrary"),
    )(a, b)


def _wgrad_concat(a, bs, name):
    T, K = a.shape
    widths = [b.shape[1] for b in bs]
    n = len(bs)

    def body(*refs):
        a_ref, b_refs, o_ref = refs[0], refs[1:1 + n], refs[1 + n]
        _acc_init(pl.program_id(0), o_ref)
        av, off = a_ref[...], 0
        for b_ref, wd in zip(b_refs, widths):
            o_ref[:, off:off + wd] += _dot_tn(av, b_ref[...])
            off += wd

    tt = min(2 * TBW, T)
    row = lambda m: pl.BlockSpec((tt, m), lambda t: (t, 0))
    return pl.pallas_call(
        body, name=name, grid=(T // tt,),
        in_specs=[row(K)] + [row(wd) for wd in widths],
        out_specs=_full((K, sum(widths))),
        out_shape=jax.ShapeDtypeStruct((K, sum(widths)), F32),
        compiler_params=_params("arbitrary"),
    )(a, *bs)


def _adamw(w, g, m, v, tr, name):
    R, C = w.shape

    def body(w_ref, g_ref, m_ref, v_ref, d_ref, nm_ref, nv_ref):
        gv = g_ref[...]
        nm = ADAM_B1 * m_ref[...] + (1.0 - ADAM_B1) * gv
        nv = ADAM_B2 * v_ref[...] + (1.0 - ADAM_B2) * (gv * gv)
        m_hat = nm / (1.0 - ADAM_B1 ** ADAM_STEP)
        v_hat = nv / (1.0 - ADAM_B2 ** ADAM_STEP)
        d_ref[...] = -ADAM_LR * (m_hat / (jnp.sqrt(v_hat) + ADAM_EPS) + ADAM_WD * w_ref[...])
        nm_ref[...] = nm
        nv_ref[...] = nv

    blk = pl.BlockSpec((tr, C), lambda i: (i, 0))
    out = jax.ShapeDtypeStruct((R, C), F32)
    return pl.pallas_call(
        body, name=name, grid=(R // tr,), in_specs=[blk] * 4, out_specs=[blk] * 3, out_shape=[out] * 3,
        compiler_params=_params("parallel"),
    )(w, g, m, v)


def _sum_slots(x, tr, name):
    n, R, C = x.shape

    def body(x_ref, o_ref):
        acc = x_ref[0]
        for k in range(1, n):
            acc = acc + x_ref[k]
        o_ref[...] = acc

    return pl.pallas_call(
        body, name=name, grid=(R // tr,),
        in_specs=[pl.BlockSpec((n, tr, C), lambda i: (0, i, 0))],
        out_specs=pl.BlockSpec((tr, C), lambda i: (i, 0)),
        out_shape=jax.ShapeDtypeStruct((R, C), F32),
        compiler_params=_params("parallel"),
    )(x)


def _add_half(view, other, c, name):
    if view.ndim == 3:
        _, R2, N = view.shape
        tr = min(R2, 256)
        grid = (R2 // tr,)
        in_specs = [pl.BlockSpec((1, tr, N), lambda i, c_ref: (c_ref[0], i, 0)), pl.BlockSpec((1, tr, N), lambda i, c_ref: (0, i, 0))]
        out_spec = pl.BlockSpec((tr, N), lambda i, c_ref: (i, 0))
        out_shape = jax.ShapeDtypeStruct((R2, N), BF16)

        def body(c_ref, a_ref, b_ref, o_ref):
            o_ref[...] = (a_ref[0] + b_ref[0]).astype(BF16)
    else:
        S, _, R2, C = view.shape
        grid = (S,)
        in_specs = [pl.BlockSpec((1, 1, R2, C), lambda s, c_ref: (s, c_ref[0], 0, 0)), pl.BlockSpec((1, 1, R2, C), lambda s, c_ref: (s, 0, 0, 0))]
        out_spec = pl.BlockSpec((1, R2, C), lambda s, c_ref: (s, 0, 0))
        out_shape = jax.ShapeDtypeStruct((S, R2, C), BF16)

        def body(c_ref, a_ref, b_ref, o_ref):
            o_ref[0] = (a_ref[0, 0] + b_ref[0, 0]).astype(BF16)

    return pl.pallas_call(
        body, name=name,
        grid_spec=pltpu.PrefetchScalarGridSpec(num_scalar_prefetch=1, grid=grid, in_specs=in_specs, out_specs=out_spec),
        out_shape=out_shape, compiler_params=_params("parallel"),
    )(c, view, other)


def _sum_chips(slots, part, kind, place, name):
    _, R2, C = slots.shape
    tr = min(R2, 256) if kind == "col" else R2

    def body(p_ref, s_ref, own_ref, o_ref):
        me = p_ref[0]
        own = own_ref[...] if kind == "col" else own_ref[0]
        acc = None
        for k in range(4):
            term = jnp.where(me == k, own, s_ref[k]).astype(F32)
            acc = term if acc is None else acc + term
        o_ref[0] = acc

    own_spec = (pl.BlockSpec((tr, C), lambda i, p: (i, p[0])) if kind == "col"
                else pl.BlockSpec((1, R2, C), lambda i, p: (p[0], 0, 0)))
    return pl.pallas_call(
        body, name=name,
        grid_spec=pltpu.PrefetchScalarGridSpec(
            num_scalar_prefetch=1, grid=(R2 // tr,),
            in_specs=[pl.BlockSpec((4, tr, C), lambda i, p: (0, i, 0)), own_spec],
            out_specs=pl.BlockSpec((1, tr, C), lambda i, p: (p[1], i, 0))),
        out_shape=jax.ShapeDtypeStruct((2, R2, C), F32), compiler_params=_params("parallel"),
    )(place, slots, part)


def _place():
    x, y, c = lax.axis_index("x"), lax.axis_index("y"), lax.axis_index("c")
    chips = [(1 - x, y), (x, 1 - y), (1 - x, 1 - y)]
    return x, y, c, 2 * x + y, chips


def _remote(src, dst, send_sems, recv_sems, k, to):
    return pltpu.make_async_remote_copy(src_ref=src, dst_ref=dst, send_sem=send_sems.at[k], recv_sem=recv_sems.at[k],
                                        device_id=to, device_id_type=MESH)


def _full_shape(shard_shape, kind):
    *lead, R, C = shard_shape
    return (*lead, R, 4 * C) if kind == "col" else (*lead, 4 * R, C)


def _slab(full_ref, shard_rc, kind, chip, half=None):
    R, C = shard_rc
    lead = (slice(None),) * (len(full_ref.shape) - 2)
    if kind == "col":
        rows = pl.ds(0, R) if half is None else pl.ds(half * (R // 2), R // 2)
        return full_ref.at[(*lead, rows, pl.ds(chip * C, C))]
    rows = pl.ds(chip * R, R) if half is None else pl.ds(chip * R + half * (R // 2), R // 2)
    return full_ref.at[(*lead, rows, slice(None))]


def _row_half(ref, half):
    R = ref.shape[-2]
    lead = (slice(None),) * (len(ref.shape) - 2)
    return ref.at[(*lead, pl.ds(half * (R // 2), R // 2), slice(None))]


def _allgather_weights(shards, kinds, after):
    n = len(shards)
    full_shapes = [_full_shape(s.shape, k) for s, k in zip(shards, kinds)]

    def body(*refs):
        ins, outs, (send_sems, recv_sems) = refs[:n], refs[n + 1:2 * n + 1], refs[2 * n + 1:]
        x, y, c, me, chips = _place()
        slab = lambda a, chip, half=None: _slab(outs[a], shards[a].shape[-2:], kinds[a], chip, half)
        my_half = lambda a: _row_half(ins[a], c)
        own = [_remote(ins[a], slab(a, me), send_sems, recv_sems, 6 * n + a, (x, y, 1 - c)) for a in range(n)]
        first = [_remote(my_half(a), slab(a, me, c), send_sems, recv_sems, j * n + a, (*chip, c))
                 for j, chip in enumerate(chips) for a in range(n)]
        for cp in first + own:
            cp.start()
        passed = []
        for j, chip in enumerate(chips):
            src = 2 * chip[0] + chip[1]
            for a in range(n):
                _remote(my_half(a), slab(a, src, c), send_sems, recv_sems, j * n + a, (x, y, c)).wait_recv()
                cp = _remote(slab(a, src, c), slab(a, src, c), send_sems, recv_sems, (3 + j) * n + a, (x, y, 1 - c))
                cp.start()
                passed.append(cp)
        for j, chip in enumerate(chips):
            src = 2 * chip[0] + chip[1]
            for a in range(n):
                _remote(my_half(a), slab(a, src, 1 - c), send_sems, recv_sems, (3 + j) * n + a, (x, y, c)).wait_recv()
        for cp in own:
            cp.wait_recv()
        for cp in first + passed + own:
            cp.wait_send()

    return pl.pallas_call(
        body, name="allgather_weights",
        in_specs=[ANY] * (n + 1), out_specs=[ANY] * n,
        out_shape=[jax.ShapeDtypeStruct(s, BF16) for s in full_shapes],
        scratch_shapes=[pltpu.SemaphoreType.DMA((7 * n,)), pltpu.SemaphoreType.DMA((7 * n,))],
        compiler_params=pltpu.CompilerParams(has_side_effects=True),
    )(*shards, after)


HBM = pl.BlockSpec(memory_space=pltpu.HBM)
SEM = pl.BlockSpec(memory_space=pltpu.SEMAPHORE)
DATAFLOW = pltpu.SideEffectType.DATAFLOW_SIDE_EFFECTING


def _split_start(name, bufs, plan, ncopies, after=None):
    nb = len(bufs)
    nin = nb + (after is not None)

    def body(*refs):
        send_sems, recv_sems, token = refs[nin], refs[nin + 1], refs[-1]
        for k, (src, dst, to) in enumerate(plan(refs[:nb])):
            _remote(src, dst, send_sems, recv_sems, k, to).start()
        token[...] = jnp.zeros_like(token)

    out = pl.pallas_call(
        body, name=name,
        out_shape=(pltpu.SemaphoreType.DMA((ncopies,)), pltpu.SemaphoreType.DMA((ncopies,)),
                   *[pltpu.HBM(b.shape, b.dtype) for b in bufs], jax.ShapeDtypeStruct((8, LANES), F32)),
        in_specs=[HBM] * nb + [ANY] * (nin - nb), out_specs=(SEM, SEM, *[HBM] * nb, pl.BlockSpec(memory_space=pltpu.VMEM)),
        input_output_aliases={i: 2 + i for i in range(nb)},
        compiler_params=pltpu.CompilerParams(has_side_effects=DATAFLOW),
    )(*[pltpu.with_memory_space_constraint(b, pltpu.HBM) for b in bufs], *([] if after is None else [after]))
    return out[0], out[1], list(out[2:2 + nb]), out[-1]


def _split_wait(name, send_sems, recv_sems, bufs, plan, after):
    nb = len(bufs)

    def body(*refs):
        s_sems, r_sems = refs[nb], refs[nb + 1]
        for k, (src, dst, to) in enumerate(plan(refs[:nb])):
            cp = _remote(src, dst, s_sems, r_sems, k, to)
            cp.wait_send()
            cp.wait_recv()

    return pl.pallas_call(
        body, name=name,
        out_shape=tuple(pltpu.HBM(b.shape, b.dtype) for b in bufs),
        in_specs=[HBM] * nb + [SEM, SEM, ANY], out_specs=tuple([HBM] * nb),
        input_output_aliases={i: i for i in range(nb)},
        compiler_params=pltpu.CompilerParams(has_side_effects=DATAFLOW),
    )(*bufs, send_sems, recv_sems, after)


def _gather_plan(n, shard_rcs, kinds):
    def plan(refs):
        x, y, c, me, chips = _place()
        out = []
        for a in range(n):
            shard, full = refs[a], refs[n + a]
            out.append((shard, _slab(full, shard_rcs[a], kinds[a], me), (x, y, 1 - c)))
            for chip in chips:
                for cc in (c, 1 - c):
                    out.append((_row_half(shard, c), _slab(full, shard_rcs[a], kinds[a], me, c), (*chip, cc)))
        return out
    return plan


def _exchange_plan(n, kinds):
    def plan(refs):
        x, y, c, me, chips = _place()
        out = []
        for a in range(n):
            part, slots = refs[a], refs[n + a]
            C = slots.shape[2]
            for chip in chips:
                dst = 2 * chip[0] + chip[1]
                src = part.at[:, pl.ds(dst * C, C)] if kinds[a] == "col" else part.at[dst]
                out.append((src, slots.at[me], (*chip, c)))
        return out
    return plan


def _swap_plan(ndims):
    n = len(ndims)

    def plan(refs):
        x, y, c, _, _ = _place()
        out = []
        for a in range(n):
            src = refs[a].at[pl.ds(1 - c, 1)] if ndims[a] == 3 else refs[a].at[:, pl.ds(1 - c, 1)]
            out.append((src, refs[n + a], (x, y, 1 - c)))
        return out
    return plan


def _join_halves(bufs):
    n = len(bufs)

    def body(*refs):
        outs, (send_sems, recv_sems) = refs[n:2 * n], refs[2 * n:]
        x, y, c, _, _ = _place()
        sends = [_remote(outs[a].at[c], outs[a].at[c], send_sems, recv_sems, a, (x, y, 1 - c)) for a in range(n)]
        for cp in sends:
            cp.start()
        for a in range(n):
            _remote(outs[a].at[c], outs[a].at[1 - c], send_sems, recv_sems, a, (x, y, c)).wait_recv()
        for cp in sends:
            cp.wait_send()

    return pl.pallas_call(
        body, name="grad_join_halves", in_specs=[ANY] * n, out_specs=[ANY] * n,
        out_shape=[jax.ShapeDtypeStruct(b.shape, F32) for b in bufs],
        input_output_aliases={a: a for a in range(n)},
        scratch_shapes=[pltpu.SemaphoreType.DMA((n,)), pltpu.SemaphoreType.DMA((n,))],
        compiler_params=pltpu.CompilerParams(has_side_effects=True),
    )(*bufs)


def _allgather_small(block):
    M, N = block.shape

    def body(x_ref, out_ref, send_sems, recv_sems, local_sem):
        x, y, c, _, chips = _place()
        me, sibling = (x, y, c), (x, y, 1 - c)

        def rows(px, py, pc):
            return out_ref.at[pl.ds((4 * px + 2 * py + pc) * M, M), :]

        def copy(k, blk, to, src=None):
            return _remote(rows(*blk) if src is None else src, rows(*blk), send_sems, recv_sems, k, to)

        mine = pltpu.make_async_copy(x_ref, rows(*me), local_sem)
        mine.start()
        first = [copy(0, me, sibling, src=x_ref)] + [copy(1 + j, me, (*chip, c), src=x_ref) for j, chip in enumerate(chips)]
        for cp in first:
            cp.start()
        passed = [copy(4 + j, (*chip, c), sibling) for j, chip in enumerate(chips)]
        for j, chip in enumerate(chips):
            copy(1 + j, (*chip, c), me).wait_recv()
            passed[j].start()
        copy(0, sibling, me).wait_recv()
        for j, chip in enumerate(chips):
            copy(4 + j, (*chip, 1 - c), me).wait_recv()
        for cp in first + passed:
            cp.wait_send()
        mine.wait()

    vm = pl.BlockSpec(memory_space=pltpu.VMEM)
    return pl.pallas_call(
        body, name="allgather_small", in_specs=[vm], out_specs=vm,
        out_shape=jax.ShapeDtypeStruct((8 * M, N), F32),
        scratch_shapes=[pltpu.SemaphoreType.DMA((7,)), pltpu.SemaphoreType.DMA((7,)), pltpu.SemaphoreType.DMA],
        compiler_params=pltpu.CompilerParams(has_side_effects=True, vmem_limit_bytes=VMEM_LIMIT),
    )(block)


def _pack(arrays):
    flat = jnp.concatenate([a.reshape(-1) for a in arrays])
    pad = (-flat.shape[0]) % (8 * LANES)
    return jnp.pad(flat, (0, pad)).reshape(-1, LANES)


def _unpack(packed, shapes):
    flat, out, off = packed.reshape(-1), [], 0
    for s in shapes:
        size = 1
        for d in s:
            size *= d
        out.append(flat[off:off + size].reshape(s))
        off += size
    return out


def _block_diag(pw):
    G, n, _ = pw.shape
    eye = jnp.eye(G, dtype=pw.dtype)
    return (eye[:, None, :, None] * pw[:, :, None, :]).reshape(G * n, G * n)


def _diag_blocks(m, G):
    n = m.shape[0] // G
    return jnp.stack([m[g * n:(g + 1) * n, g * n:(g + 1) * n] for g in range(G)])


def _pad_rows(a, rows):
    return jnp.pad(a, ((0, rows - a.shape[0]), (0, 0)))


def kernel(x, w_in, w_out, conv_w, pool_w, pool_scale, rel_bias, group_gain, pre_mix_g, post_mix_g, pre_ffn_g, post_ffn_g, w_gate_up, w_down, loss_target, m_w_in, m_w_out, m_conv_w, m_pool_w, m_pool_scale, m_rel_bias, m_group_gain, m_pre_mix_g, m_post_mix_g, m_pre_ffn_g, m_post_ffn_g, m_w_gate_up, m_w_down, v_w_in, v_w_out, v_conv_w, v_pool_w, v_pool_scale, v_rel_bias, v_group_gain, v_pre_mix_g, v_post_mix_g, v_pre_ffn_g, v_post_ffn_g, v_w_gate_up, v_w_down):
    L = w_in.shape[0]
    T, D = x.shape[1], x.shape[2]
    DC = D // 4
    NH = rel_bias.shape[1]
    NREL = rel_bias.shape[2]
    G = pool_w.shape[1]
    cs = conv_w.shape[2]
    assert TB == LEFT_CHUNKS * CHUNK and T % TB == 0 and D % (4 * LANES) == 0 and NH * HEAD_DIM == D // 2
    xi, yi, ci = lax.axis_index("x"), lax.axis_index("y"), lax.axis_index("c")
    chip = 2 * xi + yi

    kinds = ("col", "row", "col", "row")
    big = (w_in, w_out, w_gate_up, w_down)
    rcs = [w.shape[-2:] for w in big]
    conv_gathered = _allgather_small(_pack([conv_w]))
    conv_all = conv_gathered.reshape(8, -1)[:, :L * 3 * cs].reshape(4, 2, L, 3, cs)[:, 0]
    conv_full = jnp.moveaxis(conv_all, 0, 2).reshape(L, 3, 4 * cs)
    landing = lambda which: [lax.empty(_full_shape(rcs[k], kinds[k]), BF16) for k in which]
    wi0 = _allgather_weights([w_in[0:1].astype(BF16)], kinds[:1], conv_gathered)[0][0]
    rest_plan = _gather_plan(3, rcs[1:], kinds[1:])
    rest = _split_start("gather_start_0", [w[0].astype(BF16) for w in big[1:]] + landing((1, 2, 3)), rest_plan, 3 * 7, after=wi0)
    layer_plan = _gather_plan(4, rcs, kinds)
    full = [None] * L

    h = x[0]
    saved = []
    token = rest[3]
    for l in range(L):
        g_pre, g_pm, g_pf, g_po = (a[l][None] for a in (pre_mix_g, post_mix_g, pre_ffn_g, post_ffn_g))
        if 0 < l < L - 1:
            nxt = _split_start(f"gather_start_{l + 1}", [w[l + 1].astype(BF16) for w in big] + landing(range(4)), layer_plan, 4 * 7,
                               after=full[l][0])
            token = nxt[3]
        if l < L - 1:
            g_pre = g_pre + token[0:1, 0:1]
        gg, ps = group_gain[l][None], pool_scale[l][None]
        cw = _pad_rows(conv_full[l], 8)
        wbd = _block_diag(pool_w[l]).astype(BF16)
        bias = _bias_build(jnp.pad(rel_bias[l], ((0, 0), (0, RBP - NREL))))
        wi = wi0 if l == 0 else full[l][0]
        xn, pa, qkv, kv_t, yab = _inproj_mixers_fwd(h, g_pre, wi, cw, wbd, ps)
        yc, lse = _attn_fwd(qkv, kv_t, bias)
        if l == 0:
            full[0] = [wi0, *_split_wait("gather_wait_0", rest[0], rest[1], rest[2], rest_plan, yc)[3:6]]
            if L > 1:
                nxt = _split_start("gather_start_1", [w[1].astype(BF16) for w in big] + landing(range(4)), layer_plan, 4 * 7,
                                   after=full[0][1])
                g_pm = g_pm + nxt[3][0:1, 0:1]
        wi, wo, wgu, wdn = full[l]
        y, mix, h1, hn, gu, ff, ffo, h2 = _layer_tail_fwd(yab, yc, gg, wo, h, g_pm, g_pf, wgu, wdn, g_po)
        saved.append(dict(h=h, xn=xn, pa=pa, qkv=qkv, kv_t=kv_t, lse=lse, yab=yab, yc=yc, y=y, mix=mix, h1=h1, hn=hn, gu=gu, ff=ff, ffo=ffo,
                          cw=cw, wbd=wbd, bias=bias, ps=ps, gg=gg, g_pre=g_pre, g_pm=g_pm, g_pf=g_pf, g_po=g_po))
        h = h2
        if l + 1 < L:
            full[l + 1] = list(_split_wait(f"gather_wait_{l + 1}", nxt[0], nxt[1], nxt[2], layer_plan, h2)[4:8])

    dh, loss_tile = _loss_grad(h, loss_target[0])
    loss = lax.psum(loss_tile[0, 0], ("x", "y", "c"))

    place = jnp.stack([chip, ci]).astype(jnp.int32)
    cvec = ci.reshape(1).astype(jnp.int32)
    small_grads = [None] * L
    shard_grads = [[None] * 4 for _ in range(L)]
    dbiases = [None] * L

    def start_swap(tag, grads, ks, after=None):
        views = [g.reshape(2, g.shape[0] // 2, g.shape[1]) if kinds[k] == "col" else g.reshape(4, 2, g.shape[0] // 8, g.shape[1])
                 for g, k in zip(grads, ks)]
        lands = [lax.empty((1,) + v.shape[1:] if v.ndim == 3 else (v.shape[0], 1) + v.shape[2:], F32) for v in views]
        plan = _swap_plan([v.ndim for v in views])
        ssem, rsem, bufs, tok = _split_start(f"grad_swap_start_{tag}", views + lands, plan, len(ks), after=after)
        return (tag, ks, plan, ssem, rsem, bufs), tok

    def start_exchange(swap, after):
        tag, ks, plan, ssem, rsem, bufs = swap
        n = len(ks)
        thru = _split_wait(f"grad_swap_wait_{tag}", ssem, rsem, bufs, plan, after)
        chip_sums = [_add_half(v, t, cvec, "grad_add_half") for v, t in zip(thru[:n], thru[n:])]
        slots = [lax.empty((4, p.shape[0], p.shape[1] // 4) if kinds[k] == "col" else p.shape, BF16) for p, k in zip(chip_sums, ks)]
        xplan = _exchange_plan(n, [kinds[k] for k in ks])
        ssem, rsem, bufs, tok = _split_start(f"grad_exchange_start_{tag}", chip_sums + slots, xplan, 3 * n)
        return (tag, ks, xplan, ssem, rsem, bufs), tok

    def finish_exchange(layer, pending, after):
        tag, ks, xplan, ssem, rsem, bufs = pending
        n = len(ks)
        landed = _split_wait(f"grad_exchange_wait_{tag}", ssem, rsem, bufs, xplan, after)
        halves = [_sum_chips(landed[n + a], landed[a], kinds[k], place, "grad_sum_chips") for a, k in enumerate(ks)]
        for k, j in zip(ks, _join_halves(halves)):
            shard_grads[layer][k] = j.reshape(2 * j.shape[1], j.shape[2])

    swap = None
    swap_tok = None
    exchange = None
    for l in reversed(range(L)):
        s = saved[l]
        wi, wo, wgu, wdn = full[l]
        g_po = s["g_po"] if swap is None else s["g_po"] + swap_tok[0:1, 0:1]
        dffo, dgu, dh1, dmix, dyab, dyc, dg_po, dg_pf, dg_pm, dgg = _layer_tail_bwd(
            dh, s["ffo"], g_po, s["gu"], wdn, wgu, s["h1"], s["g_pf"], s["mix"], s["g_pm"], wo, s["yab"], s["yc"], s["gg"])
        after = dyc
        if swap is not None:
            started, after = start_exchange(swap, dh1)
            if exchange is not None:
                finish_exchange(l + 2, exchange, after)
            exchange = started
        F2, DFF = s["gu"].shape[1], s["ff"].shape[1]
        ffn_grads = lambda: [_wgrad(s["hn"], dgu, D, F2 // 4, "wgrad_gate_up"), _wgrad(s["ff"], dffo, DFF // 2, D, "wgrad_down")]
        g_pre = s["g_pre"]
        if l == 0:
            ffn_swap, after = start_swap("0f", ffn_grads(), (2, 3), after=after)
        dq, dk, dv, dbiases[l] = _attn_bwd(s["qkv"], s["kv_t"], dyc, s["yc"], s["lse"], s["bias"], after)
        if l == 0:
            ffn_exchange, tok = start_exchange(ffn_swap, dq)
            g_pre = g_pre + tok[0:1, 0:1]
        dpa, dh, dcw, dwbd, dps, dg_pre = _inproj_mixers_bwd(s["pa"], dyab, s["cw"], s["wbd"], s["ps"], [dq, dk, dv], wi,
                                                             s["h"], g_pre, dh1)
        dparts = [dpa, dq, dk, dv]
        mixer_grads = [_wgrad_concat(s["xn"], dparts, "wgrad_in"), _wgrad(s["y"], dmix, D, D, "wgrad_out")]
        small_grads[l] = [dcw[:3], _diag_blocks(dwbd, G), dps[0], None, dgg[0], dg_pre[0], dg_pm[0], dg_pf[0], dg_po[0]]
        if l == 0:
            swap, swap_tok = start_swap("0m", mixer_grads, (0, 1))
        else:
            swap, swap_tok = start_swap(str(l), mixer_grads + ffn_grads(), (0, 1, 2, 3))
    last, tok = start_exchange(swap, swap_tok)
    if exchange is not None:
        finish_exchange(1, exchange, tok)
    for l in range(L):
        small_grads[l][3] = _bias_fold(dbiases[l], tok)[:NH, :NREL]

    names_shapes = [(L, 3, 4 * cs), pool_w.shape, pool_scale.shape, rel_bias.shape, group_gain.shape,
                    pre_mix_g.shape, post_mix_g.shape, pre_ffn_g.shape, post_ffn_g.shape]
    small_stacked = [jnp.stack([small_grads[l][k] for l in range(L)]) for k in range(len(names_shapes))]
    packed = _pack(small_stacked)
    M = packed.shape[0]
    total = _sum_slots(_allgather_small(packed).reshape(8, M, LANES), M, "small_sum_devices")
    g_small = _unpack(total, names_shapes)
    g_small[0] = lax.dynamic_slice_in_dim(g_small[0], chip * cs, cs, axis=2)
    finish_exchange(0, ffn_exchange, total)
    finish_exchange(0, last, total)
    g_big = [jnp.stack([shard_grads[l][k] for l in range(L)]) for k in range(4)]

    def adam_big(w, g, m, v, name):
        shp = w.shape
        two = lambda a: a.reshape(shp[0] * shp[1], shp[2])
        return [o.reshape(shp) for o in _adamw(two(w), two(g), two(m), two(v), 256, name)]

    upd_in = adam_big(w_in, g_big[0], m_w_in, v_w_in, "adamw_in")
    upd_out = adam_big(w_out, g_big[1], m_w_out, v_w_out, "adamw_out")
    upd_gu = adam_big(w_gate_up, g_big[2], m_w_gate_up, v_w_gate_up, "adamw_gate_up")
    upd_dn = adam_big(w_down, g_big[3], m_w_down, v_w_down, "adamw_down")

    small_w = [conv_w, pool_w, pool_scale, rel_bias, group_gain, pre_mix_g, post_mix_g, pre_ffn_g, post_ffn_g]
    small_m = [m_conv_w, m_pool_w, m_pool_scale, m_rel_bias, m_group_gain, m_pre_mix_g, m_post_mix_g, m_pre_ffn_g, m_post_ffn_g]
    small_v = [v_conv_w, v_pool_w, v_pool_scale, v_rel_bias, v_group_gain, v_pre_mix_g, v_post_mix_g, v_pre_ffn_g, v_post_ffn_g]
    pw_, pg_, pm_, pv_ = _pack(small_w), _pack(g_small), _pack(small_m), _pack(small_v)
    shapes = [w.shape for w in small_w]
    upd_small = [_unpack(o, shapes) for o in _adamw(pw_, pg_, pm_, pv_, pw_.shape[0], "adamw_small")]

    def ordered(big4, small9):
        return [big4[0], big4[1], *small9, big4[2], big4[3]]

    grads = ordered(g_big, g_small)
    outs = [ordered([upd_in[k], upd_out[k], upd_gu[k], upd_dn[k]], upd_small[k]) for k in range(3)]
    return (loss, dh[None], *grads, *outs[0], *outs[1], *outs[2])
```
